```python
import math
import jax, jax.numpy as jnp
from jax import lax
import numpy as np

D_MODEL = 1024
BATCH = 4
SEQ = 4096
DEPTH = 2
DEC_BATCH = 128
DEC_SEQ = 1
PAST_LEN = 2048
PAGE_SIZE = 128

HEAD_DIM = 64
C_A = D_MODEL // 4
C_B = D_MODEL // 4
C_C = D_MODEL // 4
C_D = D_MODEL - C_A - C_B - C_C
H_A = C_A // HEAD_DIM
N_KV_A = 2
R_A = H_A // N_KV_A
ROT_DIM = HEAD_DIM // 4
ROPE_THETA = 500000.0
CMP_BLOCK = 32
CMP_STRIDE = 16
CMP_HIDDEN = 2 * HEAD_DIM
SEL_BLOCK = 64
TOP_K = 16
WINDOW = 512
Q_BLOCK = 128
NEG_INF = -1e30
FORCED_SCORE = 1e9
BLOCKED_SCORE = -1e9
H_B = C_B // HEAD_DIM
LORA_W = C_B // 16
LORA_A = C_B // 16
LORA_G = C_B // 8
SHIFT_B = 3 * C_B + LORA_W + LORA_A + LORA_G
RWKV_GN_EPS = 64e-5
S5_CH = 16
G_C = C_C // S5_CH
S5_P = 64
H_D = C_D // HEAD_DIM
RET_CHUNK = 128
RET_THETA = 10000.0
RET_GN_EPS = 1e-5
N_GROUPS = 4
EXP_PER_GROUP = 4
N_EXPERTS = N_GROUPS * EXP_PER_GROUP
TOP_E = 2
D_EXPERT = D_MODEL // 4
RMS_EPS = 1e-6
SPLIT_SIZES = (C_A, 6 * N_KV_A * HEAD_DIM, 3 * H_A, SHIFT_B, C_C, 4 * C_D)
D_IN = sum(SPLIT_SIZES)

kernel_name = "hymba_nsa_rwkv7_s5_retnet_hmoe_step"

F32 = jnp.float32


def _offsets(sizes):
    return [int(s) for s in np.cumsum(sizes)[:-1]]


def rmsnorm(x, w):
    xf = x.astype(F32)
    y = xf * lax.rsqrt(jnp.mean(xf * xf, -1, keepdims=True) + RMS_EPS)
    return (y * w.astype(F32)).astype(x.dtype)


def head_norm(y, eps):
    yc = y - jnp.mean(y, -1, keepdims=True)
    return yc * lax.rsqrt(jnp.mean(yc * yc, -1, keepdims=True) + eps)


def rope(x, pos, rot_dim, theta):
    half = rot_dim // 2
    inv = theta ** (-jnp.arange(half, dtype=F32) / half)
    ang = jnp.asarray(pos, F32)[:, None] * inv[None, :]
    cos, sin = jnp.cos(ang)[:, None, :], jnp.sin(ang)[:, None, :]
    xf = x.astype(F32)
    x1, x2 = xf[..., :half], xf[..., half:rot_dim]
    out = jnp.concatenate([x1 * cos - x2 * sin, x2 * cos + x1 * sin, xf[..., rot_dim:]], -1)
    return out.astype(x.dtype)


def masked_softmax(s, mask):
    s = jnp.where(mask, s.astype(F32), NEG_INF)
    return jnp.where(mask, jax.nn.softmax(s, axis=-1), 0.0)


def attend(q, k, v, mask):
    s = jnp.einsum('...qgrd,...kgd->...qgrk', q, k) * HEAD_DIM ** -0.5
    p = masked_softmax(s, mask)
    return jnp.einsum('...qgrk,...kgd->...qgrd', p, v.astype(F32))


def nsa_compress(rows, w1, b1, w2):
    B, Tk = rows.shape[:2]
    m = CMP_BLOCK // CMP_STRIDE
    n_cmp = (Tk - CMP_BLOCK) // CMP_STRIDE + 1
    n_chunks = n_cmp + m - 1
    chunks = rows[:, :n_chunks * CMP_STRIDE].reshape(B, n_chunks, CMP_STRIDE, N_KV_A, HEAD_DIM)
    w1r = w1.reshape(m, CMP_STRIDE, HEAD_DIM, CMP_HIDDEN)
    h = b1.astype(F32)
    for j in range(m):
        h = h + jnp.einsum('bcsgd,sdf->bcgf', chunks[:, j:j + n_cmp], w1r[j])
    return jnp.einsum('bcgf,fd->bcgd', jax.nn.gelu(h), w2)


def cmp_to_sel_matrix(n_cmp, n_sel):
    starts = np.arange(n_cmp) * CMP_STRIDE
    sel_s = np.arange(n_sel) * SEL_BLOCK
    ov = np.minimum(starts[:, None] + CMP_BLOCK, sel_s[None] + SEL_BLOCK) - np.maximum(starts[:, None], sel_s[None])
    return jnp.asarray(np.clip(ov, 0, None) / CMP_BLOCK, dtype=F32)


def nsa_cmp_slc(q, q_pos, kc, vc, ks_blk, vs_blk, ov):
    B, Tq = q.shape[:2]
    n_cmp, n_sel = kc.shape[1], ks_blk.shape[2]
    cmp_end = jnp.arange(n_cmp) * CMP_STRIDE + CMP_BLOCK - 1
    s = jnp.einsum('bqgrd,bngd->bqgrn', q, kc) * HEAD_DIM ** -0.5
    p_cmp = masked_softmax(s, (cmp_end[None, :] <= q_pos[:, None])[None, :, None, None, :])
    o_cmp = jnp.einsum('bqgrn,bngd->bqgrd', p_cmp, vc.astype(F32))
    imp = jnp.einsum('bqgrn,nj->bqgj', p_cmp, ov)
    blk = jnp.arange(n_sel)[None, :]
    cur = (q_pos // SEL_BLOCK)[:, None]
    forced = (blk == 0) | (blk == cur) | (blk == cur - 1)
    causal = blk * SEL_BLOCK <= q_pos[:, None]
    imp = jnp.where(forced[None, :, None, :], FORCED_SCORE, jnp.where(causal[None, :, None, :], imp, BLOCKED_SCORE))
    _, idx = lax.top_k(imp, min(TOP_K, n_sel))
    n_k = idx.shape[-1]
    bi = jnp.arange(B)[:, None, None, None]
    gi = jnp.arange(N_KV_A)[None, None, :, None]
    ksel = ks_blk[bi, gi, idx]
    vsel = vs_blk[bi, gi, idx]
    s2 = jnp.einsum('bqgrd,bqgksd->bqgrks', q, ksel) * HEAD_DIM ** -0.5
    kpos = idx[..., None] * SEL_BLOCK + jnp.arange(SEL_BLOCK)
    mask2 = (kpos <= q_pos[None, :, None, None, None]).reshape(B, Tq, N_KV_A, 1, n_k * SEL_BLOCK)
    p2 = masked_softmax(s2.reshape(B, Tq, N_KV_A, R_A, n_k * SEL_BLOCK), mask2)
    o_slc = jnp.einsum('bqgrm,bqgmd->bqgrd', p2, vsel.reshape(B, Tq, N_KV_A, n_k * SEL_BLOCK, HEAD_DIM).astype(F32))
    return o_cmp, o_slc


def nsa_window_banded(q, kw, vw):
    B, T = q.shape[:2]
    nb, nw = T // Q_BLOCK, WINDOW // Q_BLOCK
    padw = ((0, 0), (WINDOW, 0), (0, 0), (0, 0))
    cidx = np.arange(nb)[:, None] + np.arange(nw + 1)[None, :]
    band = lambda z: jnp.pad(z, padw).reshape(B, nb + nw, Q_BLOCK, N_KV_A, HEAD_DIM)[:, cidx].reshape(B, nb, (nw + 1) * Q_BLOCK, N_KV_A, HEAD_DIM)
    qpos = np.arange(T).reshape(nb, Q_BLOCK)
    kpos = (np.arange(nb) * Q_BLOCK - WINDOW)[:, None] + np.arange((nw + 1) * Q_BLOCK)[None, :]
    diff = qpos[:, :, None] - kpos[:, None, :]
    mask = (diff >= 0) & (diff < WINDOW) & (kpos[:, None, :] >= 0)
    o = attend(q.reshape(B, nb, Q_BLOCK, N_KV_A, R_A, HEAD_DIM), band(kw), band(vw), mask[None, :, :, None, None, :])
    return o.reshape(B, T, N_KV_A, R_A, HEAD_DIM)


def nsa_mixer(q_cols, kv_cols, gate_cols, pos, cmp_w1, cmp_b1, cmp_w2, past_rows, win_buf):
    B, T, _ = q_cols.shape
    q = rope(q_cols.reshape(B, T, H_A, HEAD_DIM), pos, ROT_DIM, ROPE_THETA).reshape(B, T, N_KV_A, R_A, HEAD_DIM)
    kv = kv_cols.reshape(B, T, 6, N_KV_A, HEAD_DIM)
    keys = rope(kv[:, :, 0::2].reshape(B, T, 3 * N_KV_A, HEAD_DIM), pos, ROT_DIM, ROPE_THETA).reshape(B, T, 3, N_KV_A, HEAD_DIM)
    vals = kv[:, :, 1::2]
    rows = jnp.stack([keys[:, :, 0], vals[:, :, 0], keys[:, :, 1], vals[:, :, 1]], axis=2)
    win_rows = jnp.stack([keys[:, :, 2], vals[:, :, 2]], axis=2)
    full = rows if past_rows is None else jnp.concatenate([past_rows.astype(rows.dtype), rows], axis=1)
    Tk = full.shape[1]
    kc = nsa_compress(full[:, :, 0], cmp_w1[0], cmp_b1[0], cmp_w2[0])
    vc = nsa_compress(full[:, :, 1], cmp_w1[1], cmp_b1[1], cmp_w2[1])
    n_sel = -(-Tk // SEL_BLOCK)
    sel = jnp.pad(full[:, :, 2:4], ((0, 0), (0, n_sel * SEL_BLOCK - Tk), (0, 0), (0, 0), (0, 0)))
    sel = sel.reshape(B, n_sel, SEL_BLOCK, 2, N_KV_A, HEAD_DIM).transpose(3, 0, 4, 1, 2, 5)
    ov = cmp_to_sel_matrix(kc.shape[1], n_sel)
    branch = lambda qb, pb: nsa_cmp_slc(qb, pb, kc, vc, sel[0], sel[1], ov)
    qpos = jnp.asarray(pos, jnp.int32)
    if T > Q_BLOCK and T % Q_BLOCK == 0:
        nb = T // Q_BLOCK
        qb = jnp.swapaxes(q.reshape(B, nb, Q_BLOCK, N_KV_A, R_A, HEAD_DIM), 0, 1)
        o_c, o_s = lax.map(lambda a: branch(a[0], a[1]), (qb, qpos.reshape(nb, Q_BLOCK)))
        o_c = jnp.swapaxes(o_c, 0, 1).reshape(B, T, N_KV_A, R_A, HEAD_DIM)
        o_s = jnp.swapaxes(o_s, 0, 1).reshape(B, T, N_KV_A, R_A, HEAD_DIM)
    else:
        o_c, o_s = branch(q, qpos)
    if win_buf is None:
        o_w = nsa_window_banded(q, win_rows[:, :, 0], win_rows[:, :, 1])
        new_win = win_rows[:, T - min(WINDOW, T):]
    else:
        wb = win_buf.shape[1]
        kw = jnp.concatenate([win_buf.astype(win_rows.dtype), win_rows], axis=1)
        kpos = int(pos[0]) - wb + np.arange(wb + T)
        diff = pos[:, None] - kpos[None, :]
        mask = (diff >= 0) & (diff < WINDOW)
        o_w = attend(q, kw[:, :, 0], kw[:, :, 1], mask[None, :, None, None, :])
        new_win = kw[:, wb + T - min(WINDOW, wb + T):]
    g = jax.nn.sigmoid(gate_cols.astype(F32)).reshape(B, T, N_KV_A, R_A, 3)
    o = g[..., 0:1] * o_c + g[..., 1:2] * o_s + g[..., 2:3] * o_w
    return o.reshape(B, T, C_A), rows, new_win


def rwkv_step(S, inp):
    r, w, k, v, kk, a = inp
    sk = jnp.einsum('bhij,bhj->bhi', S, kk)
    S = S * w[:, :, None, :] - sk[..., None] * (kk * a)[:, :, None, :] + v[..., None] * k[:, :, None, :]
    return S, jnp.einsum('bhij,bhj->bhi', S, r)


def rwkv_mixer(cols, shift0, S0, mu, vec, w_up, a_up, g_up):
    B, T, _ = cols.shape
    cols = cols.astype(F32)
    prev = jnp.concatenate([shift0.astype(F32)[:, None], cols[:, :-1]], axis=1)
    xs = cols + mu.astype(F32) * (prev - cols)
    r, k, v, wd, ad, gd = jnp.split(xs, _offsets((C_B, C_B, C_B, LORA_W, LORA_A, LORA_G)), axis=-1)
    w0, a0, k_k, k_a, r_k, lnx_w, lnx_b = vec.astype(F32)
    w_log = -jax.nn.softplus(-(w0 + jnp.tanh(wd) @ w_up.astype(F32))) - 0.5
    decay = jnp.exp(-jnp.exp(w_log))
    a = jax.nn.sigmoid(a0 + ad @ a_up.astype(F32))
    g = jax.nn.sigmoid(gd) @ g_up.astype(F32)
    hs = lambda z: z.reshape(B, T, H_B, HEAD_DIM)
    kk = hs(k * k_k)
    kk = kk * lax.rsqrt(jnp.sum(kk * kk, -1, keepdims=True) + 1e-12)
    k = k * (1.0 + (a - 1.0) * k_a)
    r_h, w_h, k_h, v_h, a_h = hs(r), hs(decay), hs(k), hs(v), hs(a)
    seq = tuple(jnp.moveaxis(z, 1, 0) for z in (r_h, w_h, k_h, v_h, kk, a_h))
    S_T, y = lax.scan(rwkv_step, S0.astype(F32), seq)
    y = head_norm(jnp.moveaxis(y, 0, 1), RWKV_GN_EPS).reshape(B, T, C_B) * lnx_w + lnx_b
    bonus = jnp.sum(r_h * k_h * r_k.reshape(H_B, HEAD_DIM), -1, keepdims=True) * v_h
    return (y + bonus.reshape(B, T, C_B)) * g, S_T, cols[:, -1]


def s5_mixer(u_cols, x0, lam_re, lam_im, b, c, d, log_step, w_glu):
    B, T, _ = u_cols.shape
    u = u_cols.astype(F32).reshape(B, T, G_C, S5_CH)
    lr, li = lam_re.astype(F32), lam_im.astype(F32)
    dt = jnp.exp(log_step.astype(F32))[:, None]
    mag = jnp.exp(lr * dt)
    ar, ai = mag * jnp.cos(li * dt), mag * jnp.sin(li * dt)
    nr, ni = ar - 1.0, ai
    den = lr * lr + li * li
    fr, fi = (nr * lr + ni * li) / den, (ni * lr - nr * li) / den
    b_re, b_im = b[0].astype(F32), b[1].astype(F32)
    bbr = fr[..., None] * b_re - fi[..., None] * b_im
    bbi = fr[..., None] * b_im + fi[..., None] * b_re
    bur = jnp.einsum('gpc,btgc->btgp', bbr, u)
    bui = jnp.einsum('gpc,btgc->btgp', bbi, u)

    def combine(e1, e2):
        a1r, a1i, b1r, b1i = e1
        a2r, a2i, b2r, b2i = e2
        return (a2r * a1r - a2i * a1i, a2r * a1i + a2i * a1r,
                a2r * b1r - a2i * b1i + b2r, a2r * b1i + a2i * b1r + b2i)

    acr, aci, xr, xi = lax.associative_scan(
        combine, (jnp.broadcast_to(ar, bur.shape), jnp.broadcast_to(ai, bur.shape), bur, bui), axis=1)
    x0r, x0i = x0[..., 0].astype(F32)[:, None], x0[..., 1].astype(F32)[:, None]
    xr, xi = xr + acr * x0r - aci * x0i, xi + acr * x0i + aci * x0r
    y = jnp.einsum('gcp,btgp->btgc', c[0].astype(F32), xr) - jnp.einsum('gcp,btgp->btgc', c[1].astype(F32), xi) + d.astype(F32) * u
    z = jax.nn.gelu(y.reshape(B, T, C_C))
    out = z * jax.nn.sigmoid(z @ w_glu.astype(F32))
    return out, jnp.stack([xr[:, -1], xi[:, -1]], axis=-1)


def retention(q, k, v, R0):
    B, T, H, D = q.shape
    C = RET_CHUNK if T % RET_CHUNK == 0 else T
    n = T // C
    log_g = jnp.log1p(-jnp.exp2(-5.0 - jnp.arange(H, dtype=F32)))
    i = jnp.arange(C, dtype=F32)
    diff = i[:, None] - i[None, :]
    dmat = jnp.where(diff >= 0, jnp.exp(jnp.maximum(diff, 0.0)[None] * log_g[:, None, None]), 0.0)
    q_dec = jnp.exp((i + 1.0)[None] * log_g[:, None])
    k_dec = jnp.exp((C - 1.0 - i)[None] * log_g[:, None])
    chunk_dec = jnp.exp(C * log_g)
    qc, kc, vc = q.reshape(B, n, C, H, D), k.reshape(B, n, C, H, D), v.reshape(B, n, C, H, D)
    s = jnp.einsum('bnihd,bnjhd->bnhij', qc, kc) * dmat
    inner = jnp.einsum('bnhij,bnjhe->bnihe', s, vc)
    kv = jnp.einsum('bnjhd,hj,bnjhe->bnhde', kc, k_dec, vc)

    def step(R, inp):
        q_n, kv_n = inp
        cross = jnp.einsum('bihd,hi,bhde->bihe', q_n, q_dec, R)
        return chunk_dec[None, :, None, None] * R + kv_n, cross

    R_T, cross = lax.scan(step, R0, (jnp.moveaxis(qc, 1, 0), jnp.moveaxis(kv, 1, 0)))
    return (inner + jnp.moveaxis(cross, 0, 1)).reshape(B, T, H, D), R_T


def retnet_mixer(cols, pos, R0, gn_w):
    B, T, _ = cols.shape
    q, k, v, g = jnp.split(cols.astype(F32), 4, axis=-1)
    hs = lambda z: z.reshape(B, T, H_D, HEAD_DIM)
    q = rope(hs(q), pos, HEAD_DIM, RET_THETA)
    k = rope(hs(k), pos, HEAD_DIM, RET_THETA) * HEAD_DIM ** -0.5
    o, R = retention(q, k, hs(v), R0.astype(F32))
    o = head_norm(o, RET_GN_EPS).reshape(B, T, C_D) * gn_w.astype(F32)
    return jax.nn.silu(g) * o, R


def hier_moe(h, w_grp, b_grp, w_exp, b_exp, w_gate, w_up, w_down):
    B, T, D = h.shape
    t = h.reshape(B * T, D)
    p_grp = jax.nn.softmax((t @ w_grp + b_grp).astype(F32), axis=-1)
    g_val, g_idx = lax.top_k(p_grp, 1)
    logits_e = jnp.einsum('nd,gde->nge', t, w_exp) + b_exp
    logits_e = jnp.take_along_axis(logits_e, g_idx[:, :, None], axis=1)[:, 0]
    e_val, e_idx = lax.top_k(jax.nn.softmax(logits_e.astype(F32), axis=-1), TOP_E)
    e_val = e_val / jnp.sum(e_val, -1, keepdims=True)
    w_e = jnp.einsum('nk,nke->ne', e_val, jax.nn.one_hot(e_idx, EXP_PER_GROUP, dtype=F32))
    comb = (jax.nn.one_hot(g_idx[:, 0], N_GROUPS, dtype=F32)[:, :, None] * (g_val * w_e)[:, None, :]).reshape(-1, N_EXPERTS)
    hg = jnp.einsum('nd,edf->nef', t, w_gate)
    hu = jnp.einsum('nd,edf->nef', t, w_up)
    act = jax.nn.silu(hg.astype(F32)) * hu.astype(F32) * comb[:, :, None]
    return jnp.einsum('nef,efd->nd', act, w_down.astype(F32)).reshape(B, T, D)


def trunk_layer(x, pos, lw, past_rows, win_buf, s_rwkv, s_shift, s_s5, s_ret):
    h = rmsnorm(x, lw['norm_mix'])
    proj = h @ lw['w_in']
    q_a, kv_a, gate_a, cols_b, u_c, cols_d = jnp.split(proj, _offsets(SPLIT_SIZES), axis=-1)
    o_a, rows, win = nsa_mixer(q_a, kv_a, gate_a, pos, lw['nsa_cmp_w1'], lw['nsa_cmp_b1'], lw['nsa_cmp_w2'], past_rows, win_buf)
    o_b, s_rwkv, s_shift = rwkv_mixer(cols_b, s_shift, s_rwkv, lw['rwkv_mu'], lw['rwkv_vec'], lw['rwkv_w_up'], lw['rwkv_a_up'], lw['rwkv_g_up'])
    o_c, s_s5 = s5_mixer(u_c, s_s5, lw['s5_lambda_re'], lw['s5_lambda_im'], lw['s5_b'], lw['s5_c'], lw['s5_d'], lw['s5_log_step'], lw['s5_w_glu'])
    o_d, s_ret = retnet_mixer(cols_d, pos, s_ret, lw['ret_gn'])
    mix = jnp.concatenate([o_a, o_b, o_c, o_d], axis=-1).astype(x.dtype)
    x = x + mix @ lw['w_out']
    x = x + hier_moe(rmsnorm(x, lw['norm_ffn']), lw['moe_w_grp'], lw['moe_b_grp'], lw['moe_w_exp'], lw['moe_b_exp'],
                     lw['moe_w_gate'], lw['moe_w_up'], lw['moe_w_down']).astype(x.dtype)
    return x, (rows, win, s_rwkv, s_shift, s_s5, s_ret)


def setup_inputs(seed: int = 0) -> dict:
    key = jax.random.key(seed)
    ks = iter(jax.random.split(key, 64))
    nrm = lambda shape, scale=1.0: scale * jax.random.normal(next(ks), shape, F32)
    n_pages = PAST_LEN // PAGE_SIZE
    n_pool = (5 * DEC_BATCH * n_pages) // 4
    win_buf = min(WINDOW, PAST_LEN)
    perm = jax.random.permutation(next(ks), n_pool)
    page_table = perm[:DEC_BATCH * n_pages].reshape(DEC_BATCH, n_pages).astype(jnp.int32)
    vec_off = jnp.array([-0.5, 0.0, 0.85, 1.0, 0.0, 1.0, 0.0], F32)[None, :, None]
    vec_scl = jnp.array([0.5, 0.3, 0.05, 0.05, 0.1, 0.05, 0.01], F32)[None, :, None]
    return {
        'x_prompt': nrm((BATCH, SEQ, D_MODEL)),
        'x_sample': nrm((DEC_BATCH, DEC_SEQ, D_MODEL)),
        'cache_nsa_kv': nrm((DEPTH, n_pool, PAGE_SIZE, 4, N_KV_A, HEAD_DIM)),
        'cache_nsa_win': nrm((DEPTH, DEC_BATCH, win_buf, 2, N_KV_A, HEAD_DIM)),
        'state_rwkv': nrm((DEPTH, DEC_BATCH, H_B, HEAD_DIM, HEAD_DIM), 0.5),
        'state_rwkv_shift': nrm((DEPTH, DEC_BATCH, SHIFT_B)),
        'state_s5': nrm((DEPTH, DEC_BATCH, G_C, S5_P, 2), 0.5),
        'state_ret': nrm((DEPTH, DEC_BATCH, H_D, HEAD_DIM, HEAD_DIM), 0.5),
        'page_table': page_table,
        'norm_mix': 1.0 + nrm((DEPTH, D_MODEL), 0.05),
        'w_in': nrm((DEPTH, D_MODEL, D_IN), D_MODEL ** -0.5),
        'nsa_cmp_w1': nrm((DEPTH, 2, CMP_BLOCK * HEAD_DIM, CMP_HIDDEN), (CMP_BLOCK * HEAD_DIM) ** -0.5),
        'nsa_cmp_b1': nrm((DEPTH, 2, CMP_HIDDEN), 0.01),
        'nsa_cmp_w2': nrm((DEPTH, 2, CMP_HIDDEN, HEAD_DIM), CMP_HIDDEN ** -0.5),
        'rwkv_mu': jax.random.uniform(next(ks), (DEPTH, SHIFT_B), F32),
        'rwkv_vec': vec_off + vec_scl * nrm((DEPTH, 7, C_B)),
        'rwkv_w_up': nrm((DEPTH, LORA_W, C_B), LORA_W ** -0.5),
        'rwkv_a_up': nrm((DEPTH, LORA_A, C_B), LORA_A ** -0.5),
        'rwkv_g_up': nrm((DEPTH, LORA_G, C_B), LORA_G ** -0.5),
        's5_lambda_re': -0.5 + nrm((DEPTH, G_C, S5_P), 0.01),
        's5_lambda_im': math.pi * jnp.arange(S5_P, dtype=F32) + nrm((DEPTH, G_C, S5_P), 0.01),
        's5_b': nrm((DEPTH, 2, G_C, S5_P, S5_CH), (2.0 * S5_CH) ** -0.5),
        's5_c': nrm((DEPTH, 2, G_C, S5_CH, S5_P), (2.0 * S5_P) ** -0.5),
        's5_d': nrm((DEPTH, G_C, S5_CH)),
        's5_log_step': jax.random.uniform(next(ks), (DEPTH, G_C), F32, math.log(1e-3), math.log(1e-1)),
        's5_w_glu': nrm((DEPTH, C_C, C_C), C_C ** -0.5),
        'ret_gn': 1.0 + nrm((DEPTH, C_D), 0.05),
        'w_out': nrm((DEPTH, D_MODEL, D_MODEL), D_MODEL ** -0.5),
        'norm_ffn': 1.0 + nrm((DEPTH, D_MODEL), 0.05),
        'moe_w_grp': nrm((DEPTH, D_MODEL, N_GROUPS), D_MODEL ** -0.5),
        'moe_b_grp': nrm((DEPTH, N_GROUPS), 0.01),
        'moe_w_exp': nrm((DEPTH, N_GROUPS, D_MODEL, EXP_PER_GROUP), D_MODEL ** -0.5),
        'moe_b_exp': nrm((DEPTH, N_GROUPS, EXP_PER_GROUP), 0.01),
        'moe_w_gate': nrm((DEPTH, N_EXPERTS, D_MODEL, D_EXPERT), D_MODEL ** -0.5),
        'moe_w_up': nrm((DEPTH, N_EXPERTS, D_MODEL, D_EXPERT), D_MODEL ** -0.5),
        'moe_w_down': nrm((DEPTH, N_EXPERTS, D_EXPERT, D_MODEL), D_EXPERT ** -0.5),
        'norm_final': 1.0 + nrm((D_MODEL,), 0.05),
    }


def reference(x_prompt, x_sample, cache_nsa_kv, cache_nsa_win, state_rwkv, state_rwkv_shift, state_s5, state_ret,
              page_table, norm_mix, w_in, nsa_cmp_w1, nsa_cmp_b1, nsa_cmp_w2, rwkv_mu, rwkv_vec, rwkv_w_up, rwkv_a_up,
              rwkv_g_up, s5_lambda_re, s5_lambda_im, s5_b, s5_c, s5_d, s5_log_step, s5_w_glu, ret_gn, w_out, norm_ffn,
              moe_w_grp, moe_b_grp, moe_w_exp, moe_b_exp, moe_w_gate, moe_w_up, moe_w_down, norm_final):
    stacked = dict(norm_mix=norm_mix, w_in=w_in, nsa_cmp_w1=nsa_cmp_w1, nsa_cmp_b1=nsa_cmp_b1, nsa_cmp_w2=nsa_cmp_w2,
                   rwkv_mu=rwkv_mu, rwkv_vec=rwkv_vec, rwkv_w_up=rwkv_w_up, rwkv_a_up=rwkv_a_up, rwkv_g_up=rwkv_g_up,
                   s5_lambda_re=s5_lambda_re, s5_lambda_im=s5_lambda_im, s5_b=s5_b, s5_c=s5_c, s5_d=s5_d,
                   s5_log_step=s5_log_step, s5_w_glu=s5_w_glu, ret_gn=ret_gn, w_out=w_out, norm_ffn=norm_ffn,
                   moe_w_grp=moe_w_grp, moe_b_grp=moe_b_grp, moe_w_exp=moe_w_exp, moe_b_exp=moe_b_exp,
                   moe_w_gate=moe_w_gate, moe_w_up=moe_w_up, moe_w_down=moe_w_down)
    bp, tp = x_prompt.shape[:2]
    bs, ts = x_sample.shape[:2]
    past_len = page_table.shape[1] * cache_nsa_kv.shape[2]
    pos_p = np.arange(tp)
    pos_s = past_len + np.arange(ts)
    xp, xs = x_prompt, x_sample
    sts_p, sts_s = [], []
    for l in range(DEPTH):
        lw = {name: arr[l] for name, arr in stacked.items()}
        xp, st_p = trunk_layer(xp, pos_p, lw, None, None,
                               jnp.zeros((bp, H_B, HEAD_DIM, HEAD_DIM), F32), jnp.zeros((bp, SHIFT_B), F32),
                               jnp.zeros((bp, G_C, S5_P, 2), F32), jnp.zeros((bp, H_D, HEAD_DIM, HEAD_DIM), F32))
        past = cache_nsa_kv[l][page_table].reshape(bs, past_len, 4, N_KV_A, HEAD_DIM)
        xs, st_s = trunk_layer(xs, pos_s, lw, past, cache_nsa_win[l], state_rwkv[l], state_rwkv_shift[l],
                               state_s5[l], state_ret[l])
        sts_p.append(st_p)
        sts_s.append(st_s)
    y_prompt = rmsnorm(xp, norm_final)
    y_sample = rmsnorm(xs, norm_final)
    new_p = [jnp.stack([st[i] for st in sts_p]) for i in range(6)]
    new_s = [jnp.stack([st[i] for st in sts_s]) for i in range(6)]
    return (y_prompt, y_sample, new_p[0], new_s[0], new_p[1], new_s[1], new_p[2], new_s[2],
            new_p[3], new_s[3], new_p[4], new_s[4], new_p[5], new_s[5])
```

```python
import functools
import math

import numpy as np
import jax
import jax.numpy as jnp
from jax import lax
from jax.experimental import pallas as pl
from jax.experimental.pallas import tpu as pltpu

F32 = jnp.float32
BF16 = jnp.bfloat16
HI = lax.Precision.HIGHEST

D_MODEL = 1024
DEPTH = 2
HEAD_DIM = 64
C_A = C_B = C_C = C_D = 256
H_A = 4
N_KV_A = 2
R_A = 2
ROT_DIM = 16
ROPE_THETA = 500000.0
CMP_BLOCK = 32
CMP_STRIDE = 16
CMP_HIDDEN = 128
SEL_BLOCK = 64
TOP_K = 16
WINDOW = 512
NEG_INF = -1e30
FORCED_SCORE = 1e9
BLOCKED_SCORE = -1e9
H_B = 4
LORA_W = 16
LORA_A = 16
LORA_G = 32
SHIFT_B = 832
SHIFT_PAD = 896
RWKV_GN_EPS = 64e-5
S5_CH = 16
G_C = 16
S5_P = 64
H_D = 4
RET_CHUNK = 128
RET_THETA = 10000.0
RET_GN_EPS = 1e-5
N_GROUPS = 4
EXP_PER_GROUP = 4
N_EXPERTS = 16
D_EXPERT = 256
RMS_EPS = 1e-6
SPLIT_SIZES = (C_A, 6 * N_KV_A * HEAD_DIM, 3 * H_A, SHIFT_B, C_C, 4 * C_D)
LANE = 128
VMEM_LIMIT = 56 * 1024 * 1024


def _cp(*sem):
    return pltpu.CompilerParams(dimension_semantics=sem, vmem_limit_bytes=VMEM_LIMIT)


def _offsets(sizes):
    return [int(s) for s in np.cumsum(sizes)[:-1]]


def _pad_to(a, n, axis):
    pad = [(0, 0)] * a.ndim
    pad[axis] = (0, n - a.shape[axis])
    return jnp.pad(a, pad)


def _block_ones(n, blk, dtype):
    r = lax.broadcasted_iota(jnp.int32, (n, n), 0) // blk
    c = lax.broadcasted_iota(jnp.int32, (n, n), 1) // blk
    return (r == c).astype(dtype)


def _dot2(x, ones_bf16):
    hi = x.astype(BF16)
    lo = (x - hi.astype(F32)).astype(BF16)
    return (jnp.dot(hi, ones_bf16, preferred_element_type=F32)
            + jnp.dot(lo, ones_bf16, preferred_element_type=F32))


def _rope_tables(pos, rot_dim, theta, period, reps):
    half = rot_dim // 2
    inv = theta ** (-jnp.arange(half, dtype=F32) / half)
    ang = jnp.asarray(pos, F32)[:, None] * inv[None, :]
    cos, sin = jnp.cos(ang), jnp.sin(ang)
    n = ang.shape[0]
    rest = period - rot_dim
    c = jnp.concatenate([cos, cos, jnp.ones((n, rest), F32)], -1)
    s = jnp.concatenate([-sin, sin, jnp.zeros((n, rest), F32)], -1)
    return jnp.tile(c, (1, reps)), jnp.tile(s, (1, reps))


def _proj_body(x_ref, nw_ref, w_ref, cos_ref, sin_ref, q_ref, kv_ref, g_ref, cb_ref, u_ref, cd_ref):
    x = x_ref[...]
    h = x * lax.rsqrt(jnp.mean(x * x, axis=-1, keepdims=True) + RMS_EPS) * nw_ref[...]
    hb = h.astype(BF16)
    c = cos_ref[...]
    s = sin_ref[...]
    first = (lax.broadcasted_iota(jnp.int32, c.shape, 1) % HEAD_DIM) < (ROT_DIM // 2)

    def rope(z):
        sw = jnp.where(first, pltpu.roll(z, LANE - ROT_DIM // 2, 1), pltpu.roll(z, ROT_DIM // 2, 1))
        return z * c + sw * s

    def dot(a, b):
        return jnp.dot(hb, w_ref[:, a:b], preferred_element_type=F32)

    for j in range(2):
        q_ref[:, LANE * j:LANE * (j + 1)] = rope(dot(LANE * j, LANE * (j + 1)))
    for j in range(6):
        z = dot(256 + LANE * j, 256 + LANE * (j + 1))
        kv_ref[:, LANE * j:LANE * (j + 1)] = rope(z) if j % 2 == 0 else z
    g_ref[...] = dot(1024, 1152)
    cb_ref[...] = dot(1152, 2048)
    u_ref[...] = dot(2048, 2304)
    cd_ref[...] = dot(2304, 3328)


def _proj(x2d, norm_w, w_all, cos_t, sin_t, tm):
    n = x2d.shape[0]
    t_tiles = cos_t.shape[0] // tm
    row = lambda i: (i, 0)
    fixed = lambda i: (0, 0)
    tab = lambda i: (i % t_tiles, 0)
    widths = (256, 768, 128, SHIFT_PAD, 256, 1024)
    return pl.pallas_call(
        _proj_body,
        grid=(n // tm,),
        in_specs=[pl.BlockSpec((tm, D_MODEL), row), pl.BlockSpec((1, D_MODEL), fixed),
                  pl.BlockSpec((D_MODEL, 3328), fixed), pl.BlockSpec((tm, LANE), tab), pl.BlockSpec((tm, LANE), tab)],
        out_specs=[pl.BlockSpec((tm, w), row) for w in widths],
        out_shape=[jax.ShapeDtypeStruct((n, w), F32) for w in widths],
        compiler_params=_cp("parallel"),
        name="proj",
    )(x2d, norm_w.reshape(1, D_MODEL), w_all, cos_t, sin_t)


def _cmp_body(xk_ref, xv_ref, w1_ref, b1_ref, w2_ref, kc_ref, vc_ref, vct_ref, xc_ref, *, n_chunks):
    lane = lax.broadcasted_iota(jnp.int32, (n_chunks, LANE), 1)
    lo = lane < HEAD_DIM
    for pair in range(CMP_STRIDE // 2):
        for kind, x_ref in enumerate((xk_ref, xv_ref)):
            ak = x_ref[0, pl.ds(2 * pair, n_chunks, stride=CMP_STRIDE), :]
            bk = x_ref[0, pl.ds(2 * pair + 1, n_chunks, stride=CMP_STRIDE), :]
            xc_ref[kind, 0, :, LANE * pair:LANE * (pair + 1)] = jnp.where(lo, ak, pltpu.roll(bk, HEAD_DIM, 1))
            xc_ref[kind, 1, :, LANE * pair:LANE * (pair + 1)] = jnp.where(lo, pltpu.roll(ak, HEAD_DIM, 1), bk)
    outs = []
    for kind in range(2):
        hs = []
        for g in range(N_KV_A):
            hh = jnp.dot(xc_ref[kind, g].astype(BF16), w1_ref[kind], preferred_element_type=F32)
            h = b1_ref[kind] + hh[:, :CMP_HIDDEN] + pltpu.roll(hh[:, CMP_HIDDEN:], n_chunks - 1, 0)
            hs.append(jax.nn.gelu(h))
        act = jnp.concatenate(hs, axis=1).astype(BF16)
        outs.append(jnp.dot(act, w2_ref[kind], preferred_element_type=F32))
    kc_ref[0] = outs[0]
    vc_ref[0] = outs[1]
    vct_ref[0] = outs[1].T


def _nsa_compress(rows, w1cat, b1, w2bd):
    b, tk = rows.shape[0], rows.shape[1]
    n_chunks = tk // CMP_STRIDE
    fixed3 = lambda i: (0, 0, 0)
    return pl.pallas_call(
        functools.partial(_cmp_body, n_chunks=n_chunks),
        grid=(b,),
        in_specs=[pl.BlockSpec((1, tk, LANE), lambda i: (i, 0, 0)), pl.BlockSpec((1, tk, LANE), lambda i: (i, 0, 1)),
                  pl.BlockSpec((2, CMP_STRIDE * HEAD_DIM, 2 * CMP_HIDDEN), fixed3),
                  pl.BlockSpec((2, 1, CMP_HIDDEN), fixed3),
                  pl.BlockSpec((2, 2 * CMP_HIDDEN, LANE), fixed3)],
        out_specs=[pl.BlockSpec((1, n_chunks, LANE), lambda i: (i, 0, 0)),
                   pl.BlockSpec((1, n_chunks, LANE), lambda i: (i, 0, 0)),
                   pl.BlockSpec((1, LANE, n_chunks), lambda i: (i, 0, 0))],
        out_shape=[jax.ShapeDtypeStruct((b, n_chunks, LANE), F32), jax.ShapeDtypeStruct((b, n_chunks, LANE), F32),
                   jax.ShapeDtypeStruct((b, LANE, n_chunks), F32)],
        scratch_shapes=[pltpu.VMEM((2, N_KV_A, n_chunks, CMP_STRIDE * HEAD_DIM), F32)],
        compiler_params=_cp("parallel"),
        name="nsa_compress",
    )(rows, rows, w1cat, b1, w2bd)


def _cmp_weights(cmp_w1, cmp_b1, cmp_w2):
    m = CMP_BLOCK // CMP_STRIDE
    w1r = cmp_w1.reshape(2, m, CMP_STRIDE * HEAD_DIM, CMP_HIDDEN)
    w1cat = jnp.concatenate([w1r[:, j] for j in range(m)], axis=-1).astype(BF16)
    z = jnp.zeros_like(cmp_w2)
    w2bd = jnp.concatenate([jnp.concatenate([cmp_w2, z], -1), jnp.concatenate([z, cmp_w2], -1)], axis=1).astype(BF16)
    return w1cat, cmp_b1.reshape(2, 1, CMP_HIDDEN), w2bd


def _cmp_to_sel_t(n_chunks, n_cmp, n_sel):
    starts = np.arange(n_chunks) * CMP_STRIDE
    sel_s = np.arange(n_sel) * SEL_BLOCK
    ov = np.minimum(starts[:, None] + CMP_BLOCK, sel_s[None] + SEL_BLOCK) - np.maximum(starts[:, None], sel_s[None])
    ov = np.clip(ov, 0, None) / CMP_BLOCK
    ov[n_cmp:] = 0.0
    return jnp.asarray(ov.T, dtype=F32)


def _masked_softmax_cols(s, mask):
    m = jnp.max(jnp.where(mask, s, NEG_INF), axis=0, keepdims=True)
    e = jnp.where(mask, jnp.exp(s - m), 0.0)
    den = jnp.sum(e, axis=0, keepdims=True)
    return e * jnp.where(den > 0.0, 1.0 / den, 0.0)


def _nsa_prompt_body(qt_ref, gt_ref, kc_ref, vct_ref, ovt_ref, ks_ref, vst_ref, kw_ref, vwt_ref, o_ref, sel_ref,
                     *, n_cmp, n_sel, qb_size):
    qb = pl.program_id(1)
    tq = qb_size
    n_chunks = kc_ref.shape[1]
    qpos = qb * tq + lax.broadcasted_iota(jnp.int32, (1, tq), 1)
    qpos2 = jnp.concatenate([qpos, qpos], axis=1)
    zeros_q = jnp.zeros((HEAD_DIM, 2 * tq), F32)
    gates = jax.nn.sigmoid(gt_ref[0])
    kc = kc_ref[0].astype(BF16)
    n_idx = lax.broadcasted_iota(jnp.int32, (n_chunks, 2 * tq), 0)
    cmp_mask = (n_idx * CMP_STRIDE + (CMP_BLOCK - 1) <= qpos2) & (n_idx < n_cmp)
    blk = lax.broadcasted_iota(jnp.int32, (n_sel, tq), 0)
    cur = qpos // SEL_BLOCK
    forced = (blk == 0) | (blk == cur) | (blk == cur - 1)
    causal_blk = blk * SEL_BLOCK <= qpos
    kiota = lax.broadcasted_iota(jnp.int32, (tq, 2 * tq), 0)

    for g in range(N_KV_A):
        q64 = jnp.concatenate([qt_ref[0, (2 * g) * HEAD_DIM:(2 * g + 1) * HEAD_DIM, :],
                               qt_ref[0, (2 * g + 1) * HEAD_DIM:(2 * g + 2) * HEAD_DIM, :]], axis=1) * (HEAD_DIM ** -0.5)
        qpad = (jnp.concatenate([q64, zeros_q], axis=0) if g == 0 else jnp.concatenate([zeros_q, q64], axis=0)).astype(BF16)

        p = _masked_softmax_cols(jnp.dot(kc, qpad, preferred_element_type=F32), cmp_mask)
        o_cmp = jnp.dot(vct_ref[0, g * HEAD_DIM:(g + 1) * HEAD_DIM, :].astype(BF16), p.astype(BF16), preferred_element_type=F32)
        psum = p[:, :tq] + p[:, tq:]
        imp = jnp.dot(ovt_ref[...], psum, precision=HI, preferred_element_type=F32)
        imp = jnp.where(forced, FORCED_SCORE, jnp.where(causal_blk, imp, BLOCKED_SCORE))
        rank = jnp.zeros((n_sel, tq), F32)
        for i in range(n_sel):
            row = imp[i:i + 1, :]
            rank = rank + jnp.where((row > imp) | ((row == imp) & (blk > i)), 1.0, 0.0)
        sel_ref[...] = jnp.where(rank < float(min(TOP_K, n_sel)), 1.0, 0.0)

        def attend(j, carry, k_ref, vt_ref, use_sel):
            m, l, acc = carry
            off = pl.multiple_of(j * tq, tq)
            kt = k_ref[0, pl.ds(off, tq), :].astype(BF16)
            s = jnp.dot(kt, qpad, preferred_element_type=F32)
            kpos = off + kiota
            diff = qpos2 - kpos
            if use_sel:
                per_tile = tq // SEL_BLOCK
                rows = [jnp.broadcast_to(sel_ref[pl.ds(j * per_tile + a, 1), :], (SEL_BLOCK, tq)) for a in range(per_tile)]
                selm = jnp.concatenate(rows, axis=0)
                mask = (jnp.concatenate([selm, selm], axis=1) > 0.0) & (diff >= 0)
            else:
                mask = (diff >= 0) & (diff < WINDOW)
            m_new = jnp.maximum(m, jnp.max(jnp.where(mask, s, NEG_INF), axis=0, keepdims=True))
            alpha = jnp.exp(m - m_new)
            e = jnp.where(mask, jnp.exp(s - m_new), 0.0)
            l_new = alpha * l + jnp.sum(e, axis=0, keepdims=True)
            vt = vt_ref[0, g * HEAD_DIM:(g + 1) * HEAD_DIM, pl.ds(off, tq)].astype(BF16)
            acc_new = alpha * acc + jnp.dot(vt, e.astype(BF16), preferred_element_type=F32)
            return m_new, l_new, acc_new

        init = (jnp.full((1, 2 * tq), NEG_INF, F32), jnp.zeros((1, 2 * tq), F32), jnp.zeros((HEAD_DIM, 2 * tq), F32))
        _, l_s, acc_s = lax.fori_loop(0, qb + 1, functools.partial(attend, k_ref=ks_ref, vt_ref=vst_ref, use_sel=True), init)
        o_slc = acc_s / l_s
        lo_w = jnp.maximum(qb - WINDOW // tq, 0)
        _, l_w, acc_w = lax.fori_loop(lo_w, qb + 1, functools.partial(attend, k_ref=kw_ref, vt_ref=vwt_ref, use_sel=False), init)
        o_win = acc_w / l_w

        for r in range(R_A):
            h = 2 * g + r
            gr = gates[3 * h:3 * h + 3, :]
            sl = slice(r * tq, (r + 1) * tq)
            o_ref[0, h * HEAD_DIM:(h + 1) * HEAD_DIM, :] = (gr[0:1] * o_cmp[:, sl] + gr[1:2] * o_slc[:, sl] + gr[2:3] * o_win[:, sl])


def _nsa_prompt(qt, gt, kc, vct, ovt, kv, vst, vwt, n_cmp):
    b, _, t = qt.shape
    tq = 128
    n_sel = t // SEL_BLOCK
    n_chunks = kc.shape[1]
    per_b = lambda i, j: (i, 0, 0)
    return pl.pallas_call(
        functools.partial(_nsa_prompt_body, n_cmp=n_cmp, n_sel=n_sel, qb_size=tq),
        grid=(b, t // tq),
        in_specs=[pl.BlockSpec((1, 256, tq), lambda i, j: (i, 0, j)),
                  pl.BlockSpec((1, 16, tq), lambda i, j: (i, 0, j)),
                  pl.BlockSpec((1, n_chunks, LANE), per_b),
                  pl.BlockSpec((1, LANE, n_chunks), per_b),
                  pl.BlockSpec((n_sel, n_chunks), lambda i, j: (0, 0)),
                  pl.BlockSpec((1, t, LANE), lambda i, j: (i, 0, 2)),
                  pl.BlockSpec((1, LANE, t), per_b),
                  pl.BlockSpec((1, t, LANE), lambda i, j: (i, 0, 4)),
                  pl.BlockSpec((1, LANE, t), per_b)],
        out_specs=pl.BlockSpec((1, 256, tq), lambda i, j: (i, 0, j)),
        out_shape=jax.ShapeDtypeStruct((b, 256, t), F32),
        scratch_shapes=[pltpu.VMEM((n_sel, tq), F32)],
        compiler_params=_cp("parallel", "arbitrary"),
        name="nsa_prompt",
    )(qt, gt, kc, vct, ovt, kv, vst, kv, vwt)


def _nsa_prompt_mixer(q, kv, gate, lw, b, t):
    kv3 = kv.reshape(b, t, 768)
    n_chunks = t // CMP_STRIDE
    n_cmp = (t - CMP_BLOCK) // CMP_STRIDE + 1
    kc, _, vct = _nsa_compress(kv3, *lw['cmp'])
    ovt = _cmp_to_sel_t(n_chunks, n_cmp, t // SEL_BLOCK)
    qt = jnp.swapaxes(q.reshape(b, t, 256), 1, 2)
    gt = jnp.swapaxes(gate.reshape(b, t, LANE)[:, :, :16], 1, 2)
    vst = jnp.swapaxes(kv3[:, :, 384:512], 1, 2)
    vwt = jnp.swapaxes(kv3[:, :, 640:768], 1, 2)
    ot = _nsa_prompt(qt, gt, kc, vct, ovt, kv3, vst, vwt, n_cmp)
    return jnp.swapaxes(ot, 1, 2).reshape(b * t, 256)


def _rwkv_prep_body(c_ref, s0_ref, mu_ref, vec_ref, wup_ref, aup_ref, gup_ref,
                    r_ref, lw_ref, k_ref, v_ref, kk_ref, ka_ref, g_ref, bonus_ref, carry_ref, *, tiles_per_seq):
    i = pl.program_id(0)
    cols = c_ref[...]
    tm = cols.shape[0]

    @pl.when(i % tiles_per_seq == 0)
    def _():
        carry_ref[...] = s0_ref[0]

    prev = pltpu.roll(cols, 1, 0)
    row0 = lax.broadcasted_iota(jnp.int32, cols.shape, 0) == 0
    prev = jnp.where(row0, carry_ref[...], prev)
    carry_ref[...] = cols[tm - 1:tm, :]
    xs = cols + mu_ref[...] * (prev - cols)
    r, k, v, lo = xs[:, 0:256], xs[:, 256:512], xs[:, 512:768], xs[:, 768:896]
    vec = vec_ref[...]
    w0, a0, k_k, k_a, r_k = vec[0:1], vec[1:2], vec[2:3], vec[3:4], vec[4:5]
    dot_hi = lambda x, w: jnp.dot(x, w, precision=HI, preferred_element_type=F32)
    w_log = -jax.nn.softplus(-(w0 + dot_hi(jnp.tanh(lo), wup_ref[...]))) - 0.5
    a = jax.nn.sigmoid(a0 + dot_hi(lo, aup_ref[...]))
    g_ref[...] = dot_hi(jax.nn.sigmoid(lo), gup_ref[...])
    ones = _block_ones(256, HEAD_DIM, F32)
    kk = k * k_k
    kk = kk * lax.rsqrt(dot_hi(kk * kk, ones) + 1e-12)
    k2 = k * (1.0 + (a - 1.0) * k_a)
    r_ref[...] = r
    lw_ref[...] = -jnp.exp(w_log)
    k_ref[...] = k2
    v_ref[...] = v
    kk_ref[...] = kk
    ka_ref[...] = kk * a
    bonus_ref[...] = dot_hi(r * k2 * r_k, ones) * v


def _rwkv_prep(colsb, shift0, lw, t, tm):
    n = colsb.shape[0]
    tiles_per_seq = t // tm
    row = lambda i: (i, 0)
    fixed = lambda i: (0, 0)
    outs = [jax.ShapeDtypeStruct((n, 256), F32)] * 8
    return pl.pallas_call(
        functools.partial(_rwkv_prep_body, tiles_per_seq=tiles_per_seq),
        grid=(n // tm,),
        in_specs=[pl.BlockSpec((tm, SHIFT_PAD), row),
                  pl.BlockSpec((1, 1, SHIFT_PAD), lambda i: (i // tiles_per_seq, 0, 0)),
                  pl.BlockSpec((1, SHIFT_PAD), fixed), pl.BlockSpec((8, 256), fixed),
                  pl.BlockSpec((LANE, 256), fixed), pl.BlockSpec((LANE, 256), fixed), pl.BlockSpec((LANE, 256), fixed)],
        out_specs=[pl.BlockSpec((tm, 256), row)] * 8,
        out_shape=outs,
        scratch_shapes=[pltpu.VMEM((1, SHIFT_PAD), F32)],
        compiler_params=_cp("arbitrary"),
        name="rwkv_prep",
    )(colsb, shift0.reshape(-1, 1, SHIFT_PAD), lw['rwkv_mu'], lw['rwkv_vec'], lw['rwkv_wup'], lw['rwkv_aup'], lw['rwkv_gup'])


def _rwkv_scan_body(r_ref, lw_ref, k_ref, v_ref, kk_ref, ka_ref, g_ref, bonus_ref, s0_ref, vec_ref,
                    o_ref, st_ref, s_scr, y_scr, *, nb, tl):
    @pl.when(pl.program_id(1) == 0)
    def _():
        s_scr[...] = s0_ref[...]

    ones = _block_ones(256, HEAD_DIM, BF16)
    isub = lax.broadcasted_iota(jnp.int32, (HEAD_DIM, 256), 0)
    ilane = lax.broadcasted_iota(jnp.int32, (HEAD_DIM, 256), 1) % HEAD_DIM
    msel = (isub == ilane).astype(F32)

    def step(t, carry):
        for b in range(nb):
            s = s_scr[b]
            row = lambda ref: ref[b, pl.ds(t, 1), :]
            lhs = jnp.concatenate([s * row(kk_ref), msel * row(v_ref)], axis=0)
            res = _dot2(lhs, ones)
            sk, vrep = res[:HEAD_DIM], res[HEAD_DIM:]
            s = s * jnp.exp(row(lw_ref)) - sk * row(ka_ref) + vrep * row(k_ref)
            s_scr[b] = s
            yrep = _dot2(s * row(r_ref), ones)
            y_scr[b, pl.ds(t, 1), :] = jnp.sum(yrep * msel, axis=0, keepdims=True)
        return carry

    lax.fori_loop(0, tl, step, 0)
    st_ref[...] = s_scr[...]
    vec = vec_ref[...]
    lnx_w, lnx_b = vec[5:6], vec[6:7]
    avg = _block_ones(256, HEAD_DIM, F32) * (1.0 / HEAD_DIM)
    for b in range(nb):
        y = y_scr[b]
        yc = y - jnp.dot(y, avg, precision=HI, preferred_element_type=F32)
        yn = yc * lax.rsqrt(jnp.dot(yc * yc, avg, precision=HI, preferred_element_type=F32) + RWKV_GN_EPS)
        o_ref[b] = (yn * lnx_w + lnx_b + bonus_ref[b]) * g_ref[b]


def _rwkv_scan(prep, s0, vec, b, t, nb, tl):
    r, lw, k, v, kk, ka, g, bonus = [a.reshape(b, t, 256) for a in prep]
    seq = pl.BlockSpec((nb, tl, 256), lambda i, j: (i, j, 0))
    st = pl.BlockSpec((nb, HEAD_DIM, 256), lambda i, j: (i, 0, 0))
    return pl.pallas_call(
        functools.partial(_rwkv_scan_body, nb=nb, tl=tl),
        grid=(b // nb, t // tl),
        in_specs=[seq] * 8 + [st, pl.BlockSpec((8, 256), lambda i, j: (0, 0))],
        out_specs=[seq, st],
        out_shape=[jax.ShapeDtypeStruct((b, t, 256), F32), jax.ShapeDtypeStruct((b, HEAD_DIM, 256), F32)],
        scratch_shapes=[pltpu.VMEM((nb, HEAD_DIM, 256), F32), pltpu.VMEM((nb, tl, 256), F32)],
        compiler_params=_cp("parallel", "arbitrary"),
        name="rwkv_scan",
    )(r, lw, k, v, kk, ka, g, bonus, s0, vec)


def _rwkv_mixer(colsb, shift0, s0, lw, b, t, tm, nb, tl):
    prep = _rwkv_prep(colsb, shift0, lw, t, tm)
    s0l = jnp.transpose(s0, (0, 2, 1, 3)).reshape(b, HEAD_DIM, 256)
    o, st = _rwkv_scan(prep, s0l, lw['rwkv_vec'], b, t, nb, tl)
    st = jnp.transpose(st.reshape(b, HEAD_DIM, H_B, HEAD_DIM), (0, 2, 1, 3))
    shift = colsb.reshape(b, t, SHIFT_PAD)[:, -1, :SHIFT_B]
    return o.reshape(b * t, 256), st, shift


def _s5_params(lw):
    lr, li = lw['s5_lambda_re'], lw['s5_lambda_im']
    dt = jnp.exp(lw['s5_log_step'])[:, None]
    mag = jnp.exp(lr * dt)
    ar, ai = mag * jnp.cos(li * dt), mag * jnp.sin(li * dt)
    nr, ni = ar - 1.0, ai
    den = lr * lr + li * li
    fr, fi = (nr * lr + ni * li) / den, (ni * lr - nr * li) / den
    b_re, b_im = lw['s5_b'][0], lw['s5_b'][1]
    bbr = fr[..., None] * b_re - fi[..., None] * b_im
    bbi = fr[..., None] * b_im + fi[..., None] * b_re
    eye = jnp.eye(G_C, dtype=F32)
    bd_in = lambda m: jnp.einsum('gpc,gh->gchp', m, eye).reshape(G_C * S5_CH, G_C * S5_P)
    bd_out = lambda m: jnp.einsum('gcp,gh->gphc', m, eye).reshape(G_C * S5_P, G_C * S5_CH)
    b_big = jnp.concatenate([bd_in(bbr), bd_in(bbi)], axis=1)
    c_big = jnp.concatenate([bd_out(lw['s5_c'][0]), -bd_out(lw['s5_c'][1])], axis=0)
    a_row = jnp.concatenate([ar.reshape(1, -1), ai.reshape(1, -1)], axis=1)
    return a_row, b_big, c_big, lw['s5_d'].reshape(1, C_C)


def _s5_body(u_ref, x0_ref, a_ref, b_ref, c_ref, d_ref, wg_ref, o_ref, xt_ref, x_scr, bu_scr, xs_scr, *, nb, tt, mm_dtype, prec):
    @pl.when(pl.program_id(0) == 0)
    def _():
        x_scr[...] = x0_ref[...]

    np_ = G_C * S5_P
    ncb = np_ // LANE
    for b in range(nb):
        bu = jnp.dot(u_ref[b].astype(mm_dtype), b_ref[...], precision=prec, preferred_element_type=F32)
        for cb in range(2 * ncb):
            bu_scr[cb, b * tt:(b + 1) * tt, :] = bu[:, cb * LANE:(cb + 1) * LANE]
    a = a_ref[...]

    def step(t, carry):
        x = x_scr[...]
        rows = pl.ds(t, nb, stride=tt)
        new = [None] * (2 * ncb)
        for cb in range(ncb):
            re, im = slice(cb * LANE, (cb + 1) * LANE), slice(np_ + cb * LANE, np_ + (cb + 1) * LANE)
            ar, ai, xr, xi = a[:, re], a[:, im], x[:, re], x[:, im]
            new[cb] = ar * xr - ai * xi + bu_scr[cb, rows, :]
            new[ncb + cb] = ar * xi + ai * xr + bu_scr[ncb + cb, rows, :]
            xs_scr[cb, rows, :] = new[cb]
            xs_scr[ncb + cb, rows, :] = new[ncb + cb]
        x_scr[...] = jnp.concatenate(new, axis=1)
        return carry

    lax.fori_loop(0, tt, step, 0)
    xt_ref[...] = x_scr[...]
    for b in range(nb):
        u = u_ref[b]
        xs = jnp.concatenate([xs_scr[cb, b * tt:(b + 1) * tt, :] for cb in range(2 * ncb)], axis=1)
        y = jnp.dot(xs.astype(mm_dtype), c_ref[...], precision=prec, preferred_element_type=F32) + d_ref[...] * u
        z = jax.nn.gelu(y)
        o_ref[b] = z * jax.nn.sigmoid(jnp.dot(z.astype(mm_dtype), wg_ref[...], precision=prec, preferred_element_type=F32))


def _s5_mixer(u, x0, lw, b, t, tt, exact):
    a_row, b_big, c_big, d_row = lw['s5']
    mm_dtype = F32 if exact else BF16
    prec = HI if exact else None
    x0l = jnp.concatenate([x0[..., 0].reshape(b, -1), x0[..., 1].reshape(b, -1)], axis=1)
    np2 = 2 * G_C * S5_P
    fixed = lambda i: (0, 0)
    o, xt = pl.pallas_call(
        functools.partial(_s5_body, nb=b, tt=tt, mm_dtype=mm_dtype, prec=prec),
        grid=(t // tt,),
        in_specs=[pl.BlockSpec((b, tt, C_C), lambda i: (0, i, 0)), pl.BlockSpec((b, np2), fixed),
                  pl.BlockSpec((1, np2), fixed), pl.BlockSpec((C_C, np2), fixed), pl.BlockSpec((np2, C_C), fixed),
                  pl.BlockSpec((1, C_C), fixed), pl.BlockSpec((C_C, C_C), fixed)],
        out_specs=[pl.BlockSpec((b, tt, C_C), lambda i: (0, i, 0)), pl.BlockSpec((b, np2), fixed)],
        out_shape=[jax.ShapeDtypeStruct((b, t, C_C), F32), jax.ShapeDtypeStruct((b, np2), F32)],
        scratch_shapes=[pltpu.VMEM((b, np2), F32), pltpu.VMEM((np2 // LANE, b * tt, LANE), F32),
                        pltpu.VMEM((np2 // LANE, b * tt, LANE), F32)],
        compiler_params=_cp("arbitrary"),
        name="s5",
    )(u.reshape(b, t, C_C), x0l, a_row, b_big.astype(mm_dtype), c_big.astype(mm_dtype), d_row, lw['s5_w_glu'].astype(mm_dtype))
    xt = xt.reshape(b, 2, G_C, S5_P)
    return o.reshape(b * t, C_C), jnp.stack([xt[:, 0], xt[:, 1]], axis=-1)


def _ret_tables(pos, c):
    cos, sin = _rope_tables(pos, HEAD_DIM, RET_THETA, HEAD_DIM, H_D)
    log_g = jnp.log1p(-jnp.exp2(-5.0 - jnp.arange(H_D, dtype=F32)))
    i = jnp.arange(c, dtype=F32)
    diff = i[:, None] - i[None, :]
    dmat = jnp.where(diff >= 0, jnp.exp(jnp.maximum(diff, 0.0)[None] * log_g[:, None, None]), 0.0).reshape(H_D * c, c)
    q_dec = jnp.repeat(jnp.exp((i + 1.0)[None] * log_g[:, None]).T, HEAD_DIM, axis=1)
    k_dec = jnp.repeat(jnp.exp((c - 1.0 - i)[None] * log_g[:, None]).T, HEAD_DIM, axis=1)
    chunk_dec = jnp.repeat(jnp.exp(c * log_g), HEAD_DIM).reshape(256, 1)
    return cos, sin, dmat, q_dec, k_dec, chunk_dec


def _ret_body(c_ref, cos_ref, sin_ref, dmat_ref, qdec_ref, kdec_ref, cdec_ref, r0_ref, gn_ref, o_ref, rt_ref, r_scr, *, c):
    @pl.when(pl.program_id(1) == 0)
    def _():
        r_scr[...] = r0_ref[0]

    x = c_ref[0]
    q, k, v, g = x[:, 0:256], x[:, 256:512], x[:, 512:768], x[:, 768:1024]
    cs, sn = cos_ref[...], sin_ref[...]
    lane = lax.broadcasted_iota(jnp.int32, (c, 256), 1)
    first = (lane % HEAD_DIM) < (HEAD_DIM // 2)

    def rope(z):
        sw = jnp.where(first, pltpu.roll(z, 256 - HEAD_DIM // 2, 1), pltpu.roll(z, HEAD_DIM // 2, 1))
        return z * cs + sw * sn

    q = rope(q)
    k = rope(k) * (HEAD_DIM ** -0.5)
    head = lane // HEAD_DIM
    kb, vb = k.astype(BF16), v.astype(BF16)
    qstack = jnp.concatenate([jnp.where(head == h, q, 0.0) for h in range(H_D)], axis=0).astype(BF16)
    s = lax.dot_general(qstack, kb, (((1,), (1,)), ((), ())), preferred_element_type=F32) * dmat_ref[...]
    pv = jnp.dot(s.astype(BF16), vb, preferred_element_type=F32)
    inner = jnp.zeros((c, 256), F32)
    for h in range(H_D):
        inner = inner + jnp.where(head == h, pv[h * c:(h + 1) * c], 0.0)
    r_old = r_scr[...]
    cross = jnp.dot((q * qdec_ref[...]).astype(BF16), r_old.astype(BF16), preferred_element_type=F32)
    kv = lax.dot_general((k * kdec_ref[...]).astype(BF16), vb, (((0,), (0,)), ((), ())), preferred_element_type=F32)
    bd = _block_ones(256, HEAD_DIM, F32)
    r_new = cdec_ref[...] * r_old + kv * bd
    r_scr[...] = r_new
    rt_ref[0] = r_new
    o = inner + cross
    avg = bd * (1.0 / HEAD_DIM)
    oc = o - jnp.dot(o, avg, precision=HI, preferred_element_type=F32)
    on = oc * lax.rsqrt(jnp.dot(oc * oc, avg, precision=HI, preferred_element_type=F32) + RET_GN_EPS)
    o_ref[0] = jax.nn.silu(g) * (on * gn_ref[...])


def _ret_mixer(colsd, r0, lw, tabs, b, t):
    c = RET_CHUNK if t % RET_CHUNK == 0 else t
    cos, sin, dmat, q_dec, k_dec, chunk_dec = tabs
    eye = jnp.eye(H_D, dtype=F32)
    r0l = jnp.einsum('bhde,hg->bhdge', r0, eye).reshape(b, 256, 256)
    n_t = t // c
    fixed = lambda i, j: (0, 0)
    o, rt = pl.pallas_call(
        functools.partial(_ret_body, c=c),
        grid=(b, n_t),
        in_specs=[pl.BlockSpec((1, c, 1024), lambda i, j: (i, j, 0)),
                  pl.BlockSpec((c, 256), lambda i, j: (j, 0)), pl.BlockSpec((c, 256), lambda i, j: (j, 0)),
                  pl.BlockSpec((H_D * c, c), fixed), pl.BlockSpec((c, 256), fixed), pl.BlockSpec((c, 256), fixed),
                  pl.BlockSpec((256, 1), fixed), pl.BlockSpec((1, 256, 256), lambda i, j: (i, 0, 0)),
                  pl.BlockSpec((1, 256), fixed)],
        out_specs=[pl.BlockSpec((1, c, 256), lambda i, j: (i, j, 0)), pl.BlockSpec((1, 256, 256), lambda i, j: (i, 0, 0))],
        out_shape=[jax.ShapeDtypeStruct((b, t, 256), F32), jax.ShapeDtypeStruct((b, 256, 256), F32)],
        scratch_shapes=[pltpu.VMEM((256, 256), F32)],
        compiler_params=_cp("parallel", "arbitrary"),
        name="retention",
    )(colsd.reshape(b, t, 1024), cos, sin, dmat, q_dec, k_dec, chunk_dec, r0l, lw['ret_gn'].reshape(1, 256))
    rt = jnp.einsum('bhdge,hg->bhde', rt.reshape(b, H_D, HEAD_DIM, H_D, HEAD_DIM), eye)
    return o.reshape(b * t, 256), rt


def _partner(x, d, period):
    pos = lax.broadcasted_iota(jnp.int32, x.shape, 1) % period
    return jnp.where(pos + d < period, pltpu.roll(x, LANE - d, 1), pltpu.roll(x, period - d, 1))


def _out_body(x_ref, oa_ref, ob_ref, oc_ref, od_ref, w_ref, nw_ref, wr_ref, br_ref, x1_ref, h_ref, comb_ref):
    acc = x_ref[...]
    for i, ref in enumerate((oa_ref, ob_ref, oc_ref, od_ref)):
        acc = acc + jnp.dot(ref[...].astype(BF16), w_ref[256 * i:256 * (i + 1), :], preferred_element_type=F32)
    x1_ref[...] = acc
    h = acc * lax.rsqrt(jnp.mean(acc * acc, axis=-1, keepdims=True) + RMS_EPS) * nw_ref[...]
    h_ref[...] = h.astype(BF16)
    logits = jnp.dot(h, wr_ref[...], precision=HI, preferred_element_type=F32) + br_ref[...]
    le, lg = logits[:, :LANE], logits[:, LANE:]
    lane = lax.broadcasted_iota(jnp.int32, le.shape, 1)
    mg = jnp.max(lg, axis=-1, keepdims=True)
    eg = jnp.exp(lg - mg)
    pg = eg / (jnp.sum(eg, axis=-1, keepdims=True) * (1.0 / 32.0))
    gidx = (lane % N_EXPERTS) // EXP_PER_GROUP
    g_rank = jnp.zeros_like(pg)
    for d in range(1, N_GROUPS):
        other = pltpu.roll(pg, LANE - EXP_PER_GROUP * d, 1)
        wrapped = gidx + d >= N_GROUPS
        g_rank = g_rank + jnp.where((other > pg) | ((other == pg) & wrapped), 1.0, 0.0)
    kidx = lane % EXP_PER_GROUP
    others = [_partner(le, d, EXP_PER_GROUP) for d in range(1, EXP_PER_GROUP)]
    me = functools.reduce(jnp.maximum, others, le)
    ee = jnp.exp(le - me)
    se = ee
    for d in range(1, EXP_PER_GROUP):
        se = se + _partner(ee, d, EXP_PER_GROUP)
    pe = ee / se
    e_rank = jnp.zeros_like(pe)
    for d in range(1, EXP_PER_GROUP):
        other = _partner(pe, d, EXP_PER_GROUP)
        wrapped = kidx + d >= EXP_PER_GROUP
        e_rank = e_rank + jnp.where((other > pe) | ((other == pe) & wrapped), 1.0, 0.0)
    top = jnp.where(e_rank < 2.0, pe, 0.0)
    den = top
    for d in range(1, EXP_PER_GROUP):
        den = den + _partner(top, d, EXP_PER_GROUP)
    comb = jnp.where((g_rank < 1.0) & (lane < N_EXPERTS), pg * (top / den), 0.0)
    comb_ref[...] = comb


def _out_router(x, oa, ob, oc, od, lw, tm):
    n = x.shape[0]
    row = lambda i: (i, 0)
    fixed = lambda i: (0, 0)
    mix = pl.BlockSpec((tm, 256), row)
    return pl.pallas_call(
        _out_body,
        grid=(n // tm,),
        in_specs=[pl.BlockSpec((tm, D_MODEL), row), mix, mix, mix, mix,
                  pl.BlockSpec((D_MODEL, D_MODEL), fixed), pl.BlockSpec((1, D_MODEL), fixed),
                  pl.BlockSpec((D_MODEL, 2 * LANE), fixed), pl.BlockSpec((1, 2 * LANE), fixed)],
        out_specs=[pl.BlockSpec((tm, D_MODEL), row), pl.BlockSpec((tm, D_MODEL), row), pl.BlockSpec((tm, LANE), row)],
        out_shape=[jax.ShapeDtypeStruct((n, D_MODEL), F32), jax.ShapeDtypeStruct((n, D_MODEL), BF16),
                   jax.ShapeDtypeStruct((n, LANE), F32)],
        compiler_params=_cp("parallel"),
        name="out_router",
    )(x, oa, ob, oc, od, lw['w_out'], lw['norm_ffn'], lw['w_router'], lw['b_router'])


def _router_weights(w_grp, b_grp, w_exp, b_exp):
    we = jnp.transpose(w_exp, (1, 0, 2)).reshape(D_MODEL, N_EXPERTS)
    wg = jnp.repeat(w_grp, EXP_PER_GROUP, axis=1)
    reps = LANE // N_EXPERTS
    w = jnp.concatenate([jnp.tile(we, (1, reps)), jnp.tile(wg, (1, reps))], axis=1)
    b = jnp.concatenate([jnp.tile(b_exp.reshape(1, N_EXPERTS), (1, reps)),
                         jnp.tile(jnp.repeat(b_grp, EXP_PER_GROUP).reshape(1, N_EXPERTS), (1, reps))], axis=1)
    return w, b


def _moe_body(h_ref, comb_ref, x1_ref, wg_ref, wu_ref, wd_ref, nf_ref, x2_ref, y_ref, acc_ref):
    e = pl.program_id(1)

    @pl.when(e == 0)
    def _():
        acc_ref[...] = x1_ref[...]

    h = h_ref[...]
    comb = comb_ref[...]
    lane = lax.broadcasted_iota(jnp.int32, comb.shape, 1)
    c = jnp.sum(jnp.where(lane == e, comb, 0.0), axis=-1, keepdims=True)
    hg = jnp.dot(h, wg_ref[0], preferred_element_type=F32)
    hu = jnp.dot(h, wu_ref[0], preferred_element_type=F32)
    act = (jax.nn.silu(hg) * hu * c).astype(BF16)
    acc_ref[...] += jnp.dot(act, wd_ref[0], preferred_element_type=F32)

    @pl.when(e == N_EXPERTS - 1)
    def _():
        x2 = acc_ref[...]
        x2_ref[...] = x2
        y_ref[...] = x2 * lax.rsqrt(jnp.mean(x2 * x2, axis=-1, keepdims=True) + RMS_EPS) * nf_ref[...]


def _moe(h, comb, x1, lw, norm_final, tm):
    n = x1.shape[0]
    row = lambda i, e: (i, 0)
    per_e = lambda i, e: (e, 0, 0)
    return pl.pallas_call(
        _moe_body,
        grid=(n // tm, N_EXPERTS),
        in_specs=[pl.BlockSpec((tm, D_MODEL), row), pl.BlockSpec((tm, LANE), row), pl.BlockSpec((tm, D_MODEL), row),
                  pl.BlockSpec((1, D_MODEL, D_EXPERT), per_e), pl.BlockSpec((1, D_MODEL, D_EXPERT), per_e),
                  pl.BlockSpec((1, D_EXPERT, D_MODEL), per_e), pl.BlockSpec((1, D_MODEL), lambda i, e: (0, 0))],
        out_specs=[pl.BlockSpec((tm, D_MODEL), row), pl.BlockSpec((tm, D_MODEL), row)],
        out_shape=[jax.ShapeDtypeStruct((n, D_MODEL), F32), jax.ShapeDtypeStruct((n, D_MODEL), F32)],
        scratch_shapes=[pltpu.VMEM((tm, D_MODEL), F32)],
        compiler_params=_cp("parallel", "arbitrary"),
        name="moe",
    )(h, comb, x1, lw['moe_wg'], lw['moe_wu'], lw['moe_wd'], norm_final.reshape(1, D_MODEL))


def _head_norm(y, eps):
    yc = y - jnp.mean(y, -1, keepdims=True)
    return yc * lax.rsqrt(jnp.mean(yc * yc, -1, keepdims=True) + eps)


def _masked_softmax(s, mask):
    s = jnp.where(mask, s.astype(F32), NEG_INF)
    return jnp.where(mask, jax.nn.softmax(s, axis=-1), 0.0)


def _attend(q, k, v, mask):
    s = jnp.einsum('...qgrd,...kgd->...qgrk', q, k) * HEAD_DIM ** -0.5
    p = _masked_softmax(s, mask)
    return jnp.einsum('...qgrk,...kgd->...qgrd', p, v.astype(F32))


def _x_nsa_compress(rows, w1, b1, w2):
    B, Tk = rows.shape[:2]
    m = CMP_BLOCK // CMP_STRIDE
    n_cmp = (Tk - CMP_BLOCK) // CMP_STRIDE + 1
    n_chunks = n_cmp + m - 1
    chunks = rows[:, :n_chunks * CMP_STRIDE].reshape(B, n_chunks, CMP_STRIDE, N_KV_A, HEAD_DIM)
    w1r = w1.reshape(m, CMP_STRIDE, HEAD_DIM, CMP_HIDDEN)
    h = b1.astype(F32)
    for j in range(m):
        h = h + jnp.einsum('bcsgd,sdf->bcgf', chunks[:, j:j + n_cmp], w1r[j])
    return jnp.einsum('bcgf,fd->bcgd', jax.nn.gelu(h), w2)


def _x_nsa_sample(q, kv, gate, pos, cmp_w1, cmp_b1, cmp_w2, past_rows, win_buf):
    B = q.shape[0]
    T = 1
    q = q.reshape(B, T, N_KV_A, R_A, HEAD_DIM)
    kv = kv.reshape(B, T, 6, N_KV_A, HEAD_DIM)
    rows = kv[:, :, 0:4]
    win_rows = kv[:, :, 4:6]
    full = jnp.concatenate([past_rows, rows], axis=1)
    Tk = full.shape[1]
    kc = _x_nsa_compress(full[:, :, 0], cmp_w1[0], cmp_b1[0], cmp_w2[0])
    vc = _x_nsa_compress(full[:, :, 1], cmp_w1[1], cmp_b1[1], cmp_w2[1])
    n_sel = -(-Tk // SEL_BLOCK)
    sel = jnp.pad(full[:, :, 2:4], ((0, 0), (0, n_sel * SEL_BLOCK - Tk), (0, 0), (0, 0), (0, 0)))
    sel = sel.reshape(B, n_sel, SEL_BLOCK, 2, N_KV_A, HEAD_DIM).transpose(3, 0, 4, 1, 2, 5)
    n_cmp = kc.shape[1]
    ov = _cmp_to_sel_t(n_cmp, n_cmp, n_sel).T
    q_pos = jnp.asarray(pos, jnp.int32)
    ks_blk, vs_blk = sel[0], sel[1]
    cmp_end = jnp.arange(n_cmp) * CMP_STRIDE + CMP_BLOCK - 1
    s = jnp.einsum('bqgrd,bngd->bqgrn', q, kc) * HEAD_DIM ** -0.5
    p_cmp = _masked_softmax(s, (cmp_end[None, :] <= q_pos[:, None])[None, :, None, None, :])
    o_c = jnp.einsum('bqgrn,bngd->bqgrd', p_cmp, vc.astype(F32))
    imp = jnp.einsum('bqgrn,nj->bqgj', p_cmp, ov)
    blk = jnp.arange(n_sel)[None, :]
    cur = (q_pos // SEL_BLOCK)[:, None]
    forced = (blk == 0) | (blk == cur) | (blk == cur - 1)
    causal = blk * SEL_BLOCK <= q_pos[:, None]
    imp = jnp.where(forced[None, :, None, :], FORCED_SCORE, jnp.where(causal[None, :, None, :], imp, BLOCKED_SCORE))
    _, idx = lax.top_k(imp, min(TOP_K, n_sel))
    n_k = idx.shape[-1]
    bi = jnp.arange(B)[:, None, None, None]
    gi = jnp.arange(N_KV_A)[None, None, :, None]
    ksel = ks_blk[bi, gi, idx]
    vsel = vs_blk[bi, gi, idx]
    s2 = jnp.einsum('bqgrd,bqgksd->bqgrks', q, ksel) * HEAD_DIM ** -0.5
    kpos = idx[..., None] * SEL_BLOCK + jnp.arange(SEL_BLOCK)
    mask2 = (kpos <= q_pos[None, :, None, None, None]).reshape(B, T, N_KV_A, 1, n_k * SEL_BLOCK)
    p2 = _masked_softmax(s2.reshape(B, T, N_KV_A, R_A, n_k * SEL_BLOCK), mask2)
    o_s = jnp.einsum('bqgrm,bqgmd->bqgrd', p2, vsel.reshape(B, T, N_KV_A, n_k * SEL_BLOCK, HEAD_DIM).astype(F32))
    wb = win_buf.shape[1]
    kw = jnp.concatenate([win_buf, win_rows], axis=1)
    kpos = int(pos[0]) - wb + np.arange(wb + T)
    diff = pos[:, None] - kpos[None, :]
    mask = (diff >= 0) & (diff < WINDOW)
    o_w = _attend(q, kw[:, :, 0], kw[:, :, 1], mask[None, :, None, None, :])
    new_win = kw[:, wb + T - min(WINDOW, wb + T):]
    g = jax.nn.sigmoid(gate.astype(F32)).reshape(B, T, N_KV_A, R_A, 3)
    o = g[..., 0:1] * o_c + g[..., 1:2] * o_s + g[..., 2:3] * o_w
    return o.reshape(B, C_A), rows, new_win


def _x_rwkv_sample(cols, shift0, S0, mu, vec, w_up, a_up, g_up):
    B = cols.shape[0]
    xs = cols + mu * (shift0 - cols)
    r, k, v, wd, ad, gd = jnp.split(xs, _offsets((C_B, C_B, C_B, LORA_W, LORA_A, LORA_G)), axis=-1)
    w0, a0, k_k, k_a, r_k, lnx_w, lnx_b = vec
    w_log = -jax.nn.softplus(-(w0 + jnp.tanh(wd) @ w_up)) - 0.5
    decay = jnp.exp(-jnp.exp(w_log))
    a = jax.nn.sigmoid(a0 + ad @ a_up)
    g = jax.nn.sigmoid(gd) @ g_up
    hs = lambda z: z.reshape(B, H_B, HEAD_DIM)
    kk = hs(k * k_k)
    kk = kk * lax.rsqrt(jnp.sum(kk * kk, -1, keepdims=True) + 1e-12)
    k = k * (1.0 + (a - 1.0) * k_a)
    r_h, w_h, k_h, v_h, a_h = hs(r), hs(decay), hs(k), hs(v), hs(a)
    sk = jnp.einsum('bhij,bhj->bhi', S0, kk)
    S = S0 * w_h[:, :, None, :] - sk[..., None] * (kk * a_h)[:, :, None, :] + v_h[..., None] * k_h[:, :, None, :]
    y = jnp.einsum('bhij,bhj->bhi', S, r_h)
    y = _head_norm(y, RWKV_GN_EPS).reshape(B, C_B) * lnx_w + lnx_b
    bonus = jnp.sum(r_h * k_h * r_k.reshape(H_B, HEAD_DIM), -1, keepdims=True) * v_h
    return (y + bonus.reshape(B, C_B)) * g, S, cols


def _x_s5_sample(u, x0, lw):
    a_row, b_big, c_big, d_row = lw['s5']
    b = u.shape[0]
    np_ = G_C * S5_P
    x0l = jnp.concatenate([x0[..., 0].reshape(b, -1), x0[..., 1].reshape(b, -1)], axis=1)
    bu = jnp.dot(u, b_big, precision=HI)
    ar, ai = a_row[:, :np_], a_row[:, np_:]
    xr = ar * x0l[:, :np_] - ai * x0l[:, np_:] + bu[:, :np_]
    xi = ar * x0l[:, np_:] + ai * x0l[:, :np_] + bu[:, np_:]
    y = jnp.dot(jnp.concatenate([xr, xi], axis=1), c_big, precision=HI) + d_row * u
    z = jax.nn.gelu(y)
    out = z * jax.nn.sigmoid(jnp.dot(z, lw['s5_w_glu'], precision=HI))
    return out, jnp.stack([xr.reshape(b, G_C, S5_P), xi.reshape(b, G_C, S5_P)], axis=-1)


def _x_ret_sample(cols, cos, sin, R0, gn_w):
    B = cols.shape[0]
    q, k, v, g = jnp.split(cols, 4, axis=-1)
    half = HEAD_DIM // 2

    def rope(z):
        z = z.reshape(B, H_D, HEAD_DIM)
        sw = jnp.concatenate([z[..., half:], z[..., :half]], -1)
        return z * cos.reshape(B, H_D, HEAD_DIM) + sw * sin.reshape(B, H_D, HEAD_DIM)

    q = rope(q)
    k = rope(k) * HEAD_DIM ** -0.5
    v = v.reshape(B, H_D, HEAD_DIM)
    gamma = 1.0 - jnp.exp2(-5.0 - jnp.arange(H_D, dtype=F32))
    log_g = jnp.log1p(-jnp.exp2(-5.0 - jnp.arange(H_D, dtype=F32)))
    inner = jnp.einsum('bhd,bhd->bh', q, k)[..., None] * v
    cross = jnp.einsum('bhd,h,bhde->bhe', q, jnp.exp(log_g), R0)
    R = jnp.exp(log_g)[None, :, None, None] * R0 + jnp.einsum('bhd,bhe->bhde', k, v)
    del gamma
    o = _head_norm(inner + cross, RET_GN_EPS).reshape(B, C_D) * gn_w
    return jax.nn.silu(g) * o, R


def _prep_layer(l, p):
    w_in = p['w_in'][l]
    o = _offsets(SPLIT_SIZES)
    segs = jnp.split(w_in, o, axis=1)
    w_all = jnp.concatenate([segs[0], segs[1], _pad_to(segs[2], LANE, 1), _pad_to(segs[3], SHIFT_PAD, 1), segs[4], segs[5]],
                            axis=1).astype(BF16)
    lw = {'w_all': w_all, 'norm_mix': p['norm_mix'][l]}
    lw['cmp'] = _cmp_weights(p['nsa_cmp_w1'][l], p['nsa_cmp_b1'][l], p['nsa_cmp_w2'][l])
    lw['cmp_raw'] = (p['nsa_cmp_w1'][l], p['nsa_cmp_b1'][l], p['nsa_cmp_w2'][l])
    lw['rwkv_mu'] = _pad_to(p['rwkv_mu'][l].reshape(1, SHIFT_B), SHIFT_PAD, 1)
    lw['rwkv_vec'] = _pad_to(p['rwkv_vec'][l], 8, 0)
    z = lambda a, b: jnp.zeros((a, b), F32)
    lw['rwkv_wup'] = jnp.concatenate([p['rwkv_w_up'][l], z(LANE - LORA_W, C_B)], axis=0)
    lw['rwkv_aup'] = jnp.concatenate([z(LORA_W, C_B), p['rwkv_a_up'][l], z(LANE - LORA_W - LORA_A, C_B)], axis=0)
    lw['rwkv_gup'] = jnp.concatenate([z(LORA_W + LORA_A, C_B), p['rwkv_g_up'][l], z(LANE - LORA_W - LORA_A - LORA_G, C_B)], axis=0)
    lw['rwkv_raw'] = (p['rwkv_mu'][l], p['rwkv_vec'][l], p['rwkv_w_up'][l], p['rwkv_a_up'][l], p['rwkv_g_up'][l])
    for name in ('s5_lambda_re', 's5_lambda_im', 's5_b', 's5_c', 's5_d', 's5_log_step', 's5_w_glu', 'ret_gn'):
        lw[name] = p[name][l]
    lw['s5'] = _s5_params(lw)
    lw['w_out'] = p['w_out'][l].astype(BF16)
    lw['norm_ffn'] = p['norm_ffn'][l].reshape(1, D_MODEL)
    lw['w_router'], lw['b_router'] = _router_weights(p['moe_w_grp'][l], p['moe_b_grp'][l], p['moe_w_exp'][l], p['moe_b_exp'][l])
    lw['moe_wg'] = p['moe_w_gate'][l].astype(BF16)
    lw['moe_wu'] = p['moe_w_up'][l].astype(BF16)
    lw['moe_wd'] = p['moe_w_down'][l].astype(BF16)
    return lw


def _prompt_layer(x, lw, tabs, b, t, norm_final):
    cos_a, sin_a, ret_tabs = tabs
    q, kv, gate, colsb, u, colsd = _proj(x, lw['norm_mix'], lw['w_all'], cos_a, sin_a, 512)
    o_a = _nsa_prompt_mixer(q, kv, gate, lw, b, t)
    o_b, s_rwkv, s_shift = _rwkv_mixer(colsb, jnp.zeros((b, SHIFT_PAD), F32), jnp.zeros((b, H_B, HEAD_DIM, HEAD_DIM), F32),
                                       lw, b, t, 512, b, 256)
    o_c, s_s5 = _s5_mixer(u, jnp.zeros((b, G_C, S5_P, 2), F32), lw, b, t, 256, False)
    o_d, s_ret = _ret_mixer(colsd, jnp.zeros((b, H_D, HEAD_DIM, HEAD_DIM), F32), lw, ret_tabs, b, t)
    x1, h, comb = _out_router(x, o_a, o_b, o_c, o_d, lw, 512)
    x2, y = _moe(h, comb, x1, lw, norm_final, 1024)
    kv3 = kv.reshape(b, t, 768)
    rows = kv3[:, :, :512].reshape(b, t, 4, N_KV_A, HEAD_DIM)
    win = kv3[:, t - min(WINDOW, t):, 512:].reshape(b, min(WINDOW, t), 2, N_KV_A, HEAD_DIM)
    return x2, y, (rows, win, s_rwkv, s_shift, s_s5, s_ret)


def _sample_layer(x, lw, tabs, b, pos, past, win_buf, s_rwkv, s_shift, s_s5, s_ret, norm_final):
    cos_a, sin_a, ret_cs = tabs
    q, kv, gate, colsb, u, colsd = _proj(x, lw['norm_mix'], lw['w_all'], cos_a, sin_a, b)
    o_a, rows, win = _x_nsa_sample(q, kv, gate[:, :12], pos, *lw['cmp_raw'], past, win_buf)
    o_b, s_rwkv, s_shift = _x_rwkv_sample(colsb[:, :SHIFT_B], s_shift, s_rwkv, *lw['rwkv_raw'])
    o_c, s_s5 = _x_s5_sample(u, s_s5, lw)
    o_d, s_ret = _x_ret_sample(colsd, ret_cs[0], ret_cs[1], s_ret, lw['ret_gn'])
    x1, h, comb = _out_router(x, o_a, o_b, o_c, o_d, lw, b)
    x2, y = _moe(h, comb, x1, lw, norm_final, b)
    return x2, y, (rows, win, s_rwkv, s_shift, s_s5, s_ret)


def kernel(x_prompt, x_sample, cache_nsa_kv, cache_nsa_win, state_rwkv, state_rwkv_shift, state_s5, state_ret, page_table, norm_mix, w_in, nsa_cmp_w1, nsa_cmp_b1, nsa_cmp_w2, rwkv_mu, rwkv_vec, rwkv_w_up, rwkv_a_up, rwkv_g_up, s5_lambda_re, s5_lambda_im, s5_b, s5_c, s5_d, s5_log_step, s5_w_glu, ret_gn, w_out, norm_ffn, moe_w_grp, moe_b_grp, moe_w_exp, moe_b_exp, moe_w_gate, moe_w_up, moe_w_down, norm_final):
    p = dict(norm_mix=norm_mix, w_in=w_in, nsa_cmp_w1=nsa_cmp_w1, nsa_cmp_b1=nsa_cmp_b1, nsa_cmp_w2=nsa_cmp_w2,
             rwkv_mu=rwkv_mu, rwkv_vec=rwkv_vec, rwkv_w_up=rwkv_w_up, rwkv_a_up=rwkv_a_up, rwkv_g_up=rwkv_g_up,
             s5_lambda_re=s5_lambda_re, s5_lambda_im=s5_lambda_im, s5_b=s5_b, s5_c=s5_c, s5_d=s5_d,
             s5_log_step=s5_log_step, s5_w_glu=s5_w_glu, ret_gn=ret_gn, w_out=w_out, norm_ffn=norm_ffn,
             moe_w_grp=moe_w_grp, moe_b_grp=moe_b_grp, moe_w_exp=moe_w_exp, moe_b_exp=moe_b_exp,
             moe_w_gate=moe_w_gate, moe_w_up=moe_w_up, moe_w_down=moe_w_down)
    bp, tp = x_prompt.shape[:2]
    bs, ts = x_sample.shape[:2]
    past_len = page_table.shape[1] * cache_nsa_kv.shape[2]
    pos_p = np.arange(tp)
    pos_s = past_len + np.arange(ts)
    c = RET_CHUNK if tp % RET_CHUNK == 0 else tp
    tabs_p = _rope_tables(pos_p, ROT_DIM, ROPE_THETA, HEAD_DIM, 2) + (_ret_tables(pos_p, c),)
    pos_rows = np.repeat(pos_s, bs)
    tabs_s = _rope_tables(pos_rows, ROT_DIM, ROPE_THETA, HEAD_DIM, 2) + (_rope_tables(pos_rows, HEAD_DIM, RET_THETA, HEAD_DIM, H_D),)
    xp = x_prompt.reshape(bp * tp, D_MODEL)
    xs = x_sample.reshape(bs * ts, D_MODEL)
    sts_p, sts_s = [], []
    yp = ys = None
    for l in range(DEPTH):
        lw = _prep_layer(l, p)
        xp, yp, st_p = _prompt_layer(xp, lw, tabs_p, bp, tp, norm_final)
        past = cache_nsa_kv[l][page_table].reshape(bs, past_len, 4, N_KV_A, HEAD_DIM)
        xs, ys, st_s = _sample_layer(xs, lw, tabs_s, bs, pos_s, past, cache_nsa_win[l], state_rwkv[l],
                                     state_rwkv_shift[l], state_s5[l], state_ret[l], norm_final)
        rows, win, s1, s2, s3, s4 = st_s
        sts_s.append((rows, win, s1, s2, s3, s4))
        sts_p.append(st_p)
    new_p = [jnp.stack([st[i] for st in sts_p]) for i in range(6)]
    new_s = [jnp.stack([st[i] for st in sts_s]) for i in range(6)]
    return (yp.reshape(bp, tp, D_MODEL), ys.reshape(bs, ts, D_MODEL), new_p[0], new_s[0], new_p[1], new_s[1],
            new_p[2], new_s[2], new_p[3], new_s[3], new_p[4], new_s[4], new_p[5], new_s[5])
```

```python
import functools
import math

import numpy as np
import jax
import jax.numpy as jnp
from jax import lax
from jax.experimental import pallas as pl
from jax.experimental.pallas import tpu as pltpu

F32 = jnp.float32
BF16 = jnp.bfloat16
HI = lax.Precision.HIGHEST

D_MODEL = 1024
DEPTH = 2
HEAD_DIM = 64
C_A = C_B = C_C = C_D = 256
H_A = 4
N_KV_A = 2
R_A = 2
ROT_DIM = 16
ROPE_THETA = 500000.0
CMP_BLOCK = 32
CMP_STRIDE = 16
CMP_HIDDEN = 128
SEL_BLOCK = 64
TOP_K = 16
WINDOW = 512
NEG_INF = -1e30
FORCED_SCORE = 1e9
BLOCKED_SCORE = -1e9
H_B = 4
LORA_W = 16
LORA_A = 16
LORA_G = 32
SHIFT_B = 832
SHIFT_PAD = 896
RWKV_GN_EPS = 64e-5
S5_CH = 16
G_C = 16
S5_P = 64
H_D = 4
RET_CHUNK = 128
RET_THETA = 10000.0
RET_GN_EPS = 1e-5
N_GROUPS = 4
EXP_PER_GROUP = 4
N_EXPERTS = 16
D_EXPERT = 256
RMS_EPS = 1e-6
SPLIT_SIZES = (C_A, 6 * N_KV_A * HEAD_DIM, 3 * H_A, SHIFT_B, C_C, 4 * C_D)
LANE = 128
VMEM_LIMIT = 56 * 1024 * 1024


def _cp(*sem):
    return pltpu.CompilerParams(dimension_semantics=sem, vmem_limit_bytes=VMEM_LIMIT)


def _offsets(sizes):
    return [int(s) for s in np.cumsum(sizes)[:-1]]


def _pad_to(a, n, axis):
    pad = [(0, 0)] * a.ndim
    pad[axis] = (0, n - a.shape[axis])
    return jnp.pad(a, pad)


def _block_ones(n, blk, dtype):
    r = lax.broadcasted_iota(jnp.int32, (n, n), 0) // blk
    c = lax.broadcasted_iota(jnp.int32, (n, n), 1) // blk
    return (r == c).astype(dtype)


def _dot2(x, ones_bf16):
    hi = x.astype(BF16)
    lo = (x - hi.astype(F32)).astype(BF16)
    return (jnp.dot(hi, ones_bf16, preferred_element_type=F32)
            + jnp.dot(lo, ones_bf16, preferred_element_type=F32))


def _rope_tables(pos, rot_dim, theta, period, reps):
    half = rot_dim // 2
    inv = theta ** (-jnp.arange(half, dtype=F32) / half)
    ang = jnp.asarray(pos, F32)[:, None] * inv[None, :]
    cos, sin = jnp.cos(ang), jnp.sin(ang)
    n = ang.shape[0]
    rest = period - rot_dim
    c = jnp.concatenate([cos, cos, jnp.ones((n, rest), F32)], -1)
    s = jnp.concatenate([-sin, sin, jnp.zeros((n, rest), F32)], -1)
    return jnp.tile(c, (1, reps)), jnp.tile(s, (1, reps))


def _proj_body(x_ref, nw_ref, w_ref, cos_ref, sin_ref, q_ref, kv_ref, g_ref, cb_ref, u_ref, cd_ref):
    x = x_ref[...]
    h = x * lax.rsqrt(jnp.mean(x * x, axis=-1, keepdims=True) + RMS_EPS) * nw_ref[...]
    hb = h.astype(BF16)
    c = cos_ref[...]
    s = sin_ref[...]
    first = (lax.broadcasted_iota(jnp.int32, c.shape, 1) % HEAD_DIM) < (ROT_DIM // 2)

    def rope(z):
        sw = jnp.where(first, pltpu.roll(z, LANE - ROT_DIM // 2, 1), pltpu.roll(z, ROT_DIM // 2, 1))
        return z * c + sw * s

    def dot(a, b):
        return jnp.dot(hb, w_ref[:, a:b], preferred_element_type=F32)

    for j in range(2):
        q_ref[:, LANE * j:LANE * (j + 1)] = rope(dot(LANE * j, LANE * (j + 1)))
    for j in range(6):
        z = dot(256 + LANE * j, 256 + LANE * (j + 1))
        kv_ref[:, LANE * j:LANE * (j + 1)] = rope(z) if j % 2 == 0 else z
    g_ref[...] = dot(1024, 1152)
    cb_ref[...] = dot(1152, 2048)
    u_ref[...] = dot(2048, 2304)
    cd_ref[...] = dot(2304, 3328)


def _proj(x2d, norm_w, w_all, cos_t, sin_t, tm):
    n = x2d.shape[0]
    t_tiles = cos_t.shape[0] // tm
    row = lambda i: (i, 0)
    fixed = lambda i: (0, 0)
    tab = lambda i: (i % t_tiles, 0)
    widths = (256, 768, 128, SHIFT_PAD, 256, 1024)
    return pl.pallas_call(
        _proj_body,
        grid=(n // tm,),
        in_specs=[pl.BlockSpec((tm, D_MODEL), row), pl.BlockSpec((1, D_MODEL), fixed),
                  pl.BlockSpec((D_MODEL, 3328), fixed), pl.BlockSpec((tm, LANE), tab), pl.BlockSpec((tm, LANE), tab)],
        out_specs=[pl.BlockSpec((tm, w), row) for w in widths],
        out_shape=[jax.ShapeDtypeStruct((n, w), F32) for w in widths],
        compiler_params=_cp("parallel"),
        name="proj",
    )(x2d, norm_w.reshape(1, D_MODEL), w_all, cos_t, sin_t)


def _cmp_body(xk_ref, xv_ref, w1_ref, b1_ref, w2_ref, kc_ref, vc_ref, vct_ref, xc_ref, *, n_chunks):
    lane = lax.broadcasted_iota(jnp.int32, (n_chunks, LANE), 1)
    lo = lane < HEAD_DIM
    for pair in range(CMP_STRIDE // 2):
        for kind, x_ref in enumerate((xk_ref, xv_ref)):
            ak = x_ref[0, pl.ds(2 * pair, n_chunks, stride=CMP_STRIDE), :]
            bk = x_ref[0, pl.ds(2 * pair + 1, n_chunks, stride=CMP_STRIDE), :]
            xc_ref[kind, 0, :, LANE * pair:LANE * (pair + 1)] = jnp.where(lo, ak, pltpu.roll(bk, HEAD_DIM, 1))
            xc_ref[kind, 1, :, LANE * pair:LANE * (pair + 1)] = jnp.where(lo, pltpu.roll(ak, HEAD_DIM, 1), bk)
    outs = []
    for kind in range(2):
        hs = []
        for g in range(N_KV_A):
            hh = jnp.dot(xc_ref[kind, g].astype(BF16), w1_ref[kind], preferred_element_type=F32)
            h = b1_ref[kind] + hh[:, :CMP_HIDDEN] + pltpu.roll(hh[:, CMP_HIDDEN:], n_chunks - 1, 0)
            hs.append(jax.nn.gelu(h))
        act = jnp.concatenate(hs, axis=1).astype(BF16)
        outs.append(jnp.dot(act, w2_ref[kind], preferred_element_type=F32))
    kc_ref[0] = outs[0]
    vc_ref[0] = outs[1]
    vct_ref[0] = outs[1].T


def _nsa_compress(rows, w1cat, b1, w2bd):
    b, tk = rows.shape[0], rows.shape[1]
    n_chunks = tk // CMP_STRIDE
    fixed3 = lambda i: (0, 0, 0)
    return pl.pallas_call(
        functools.partial(_cmp_body, n_chunks=n_chunks),
        grid=(b,),
        in_specs=[pl.BlockSpec((1, tk, LANE), lambda i: (i, 0, 0)), pl.BlockSpec((1, tk, LANE), lambda i: (i, 0, 1)),
                  pl.BlockSpec((2, CMP_STRIDE * HEAD_DIM, 2 * CMP_HIDDEN), fixed3),
                  pl.BlockSpec((2, 1, CMP_HIDDEN), fixed3),
                  pl.BlockSpec((2, 2 * CMP_HIDDEN, LANE), fixed3)],
        out_specs=[pl.BlockSpec((1, n_chunks, LANE), lambda i: (i, 0, 0)),
                   pl.BlockSpec((1, n_chunks, LANE), lambda i: (i, 0, 0)),
                   pl.BlockSpec((1, LANE, n_chunks), lambda i: (i, 0, 0))],
        out_shape=[jax.ShapeDtypeStruct((b, n_chunks, LANE), F32), jax.ShapeDtypeStruct((b, n_chunks, LANE), F32),
                   jax.ShapeDtypeStruct((b, LANE, n_chunks), F32)],
        scratch_shapes=[pltpu.VMEM((2, N_KV_A, n_chunks, CMP_STRIDE * HEAD_DIM), F32)],
        compiler_params=_cp("parallel"),
        name="nsa_compress",
    )(rows, rows, w1cat, b1, w2bd)


def _cmp_weights(cmp_w1, cmp_b1, cmp_w2):
    m = CMP_BLOCK // CMP_STRIDE
    w1r = cmp_w1.reshape(2, m, CMP_STRIDE * HEAD_DIM, CMP_HIDDEN)
    w1cat = jnp.concatenate([w1r[:, j] for j in range(m)], axis=-1).astype(BF16)
    z = jnp.zeros_like(cmp_w2)
    w2bd = jnp.concatenate([jnp.concatenate([cmp_w2, z], -1), jnp.concatenate([z, cmp_w2], -1)], axis=1).astype(BF16)
    return w1cat, cmp_b1.reshape(2, 1, CMP_HIDDEN), w2bd


def _cmp_to_sel_t(n_chunks, n_cmp, n_sel):
    starts = np.arange(n_chunks) * CMP_STRIDE
    sel_s = np.arange(n_sel) * SEL_BLOCK
    ov = np.minimum(starts[:, None] + CMP_BLOCK, sel_s[None] + SEL_BLOCK) - np.maximum(starts[:, None], sel_s[None])
    ov = np.clip(ov, 0, None) / CMP_BLOCK
    ov[n_cmp:] = 0.0
    return jnp.asarray(ov.T, dtype=F32)


def _masked_softmax_cols(s, mask):
    m = jnp.max(jnp.where(mask, s, NEG_INF), axis=0, keepdims=True)
    e = jnp.where(mask, jnp.exp(s - m), 0.0)
    den = jnp.sum(e, axis=0, keepdims=True)
    return e * jnp.where(den > 0.0, 1.0 / den, 0.0)


def _nsa_prompt_body(qt_ref, gt_ref, kc_ref, vct_ref, ovt_ref, ks_ref, vst_ref, kw_ref, vwt_ref, o_ref, sel_ref,
                     *, n_cmp, n_sel, qb_size):
    qb = pl.program_id(1)
    tq = qb_size
    n_chunks = kc_ref.shape[1]
    qpos = qb * tq + lax.broadcasted_iota(jnp.int32, (1, tq), 1)
    qpos2 = jnp.concatenate([qpos, qpos], axis=1)
    zeros_q = jnp.zeros((HEAD_DIM, 2 * tq), F32)
    gates = jax.nn.sigmoid(gt_ref[0])
    kc = kc_ref[0].astype(BF16)
    n_idx = lax.broadcasted_iota(jnp.int32, (n_chunks, 2 * tq), 0)
    cmp_mask = (n_idx * CMP_STRIDE + (CMP_BLOCK - 1) <= qpos2) & (n_idx < n_cmp)
    blk = lax.broadcasted_iota(jnp.int32, (n_sel, tq), 0)
    cur = qpos // SEL_BLOCK
    forced = (blk == 0) | (blk == cur) | (blk == cur - 1)
    causal_blk = blk * SEL_BLOCK <= qpos
    tk = 2 * tq
    kiota = lax.broadcasted_iota(jnp.int32, (tk, 2 * tq), 0)
    qpads, o_cmps = [], []

    for g in range(N_KV_A):
        q64 = jnp.concatenate([qt_ref[0, (2 * g) * HEAD_DIM:(2 * g + 1) * HEAD_DIM, :],
                               qt_ref[0, (2 * g + 1) * HEAD_DIM:(2 * g + 2) * HEAD_DIM, :]], axis=1) * (HEAD_DIM ** -0.5)
        qpad = (jnp.concatenate([q64, zeros_q], axis=0) if g == 0 else jnp.concatenate([zeros_q, q64], axis=0)).astype(BF16)
        qpads.append(qpad)

        p = _masked_softmax_cols(jnp.dot(kc, qpad, preferred_element_type=F32), cmp_mask)
        o_cmps.append(jnp.dot(vct_ref[0, g * HEAD_DIM:(g + 1) * HEAD_DIM, :].astype(BF16), p.astype(BF16),
                              preferred_element_type=F32))
        psum = p[:, :tq] + p[:, tq:]
        imp = jnp.dot(ovt_ref[...], psum, precision=HI, preferred_element_type=F32)
        imp = jnp.where(forced, FORCED_SCORE, jnp.where(causal_blk, imp, BLOCKED_SCORE))
        rank = jnp.zeros((n_sel, tq), F32)
        for i in range(n_sel):
            row = imp[i:i + 1, :]
            rank = rank + jnp.where((row > imp) | ((row == imp) & (blk > i)), 1.0, 0.0)
        sel_ref[g] = jnp.where(rank < float(min(TOP_K, n_sel)), 1.0, 0.0)

    def attend(j, carry, k_ref, vt_ref, use_sel):
        off = pl.multiple_of(j * tk, tk)
        kt = k_ref[0, pl.ds(off, tk), :].astype(BF16)
        diff = qpos2 - (off + kiota)
        base = (diff >= 0) if use_sel else ((diff >= 0) & (diff < WINDOW))
        out = []
        for g in range(N_KV_A):
            m, l, acc = carry[3 * g:3 * g + 3]
            s = jnp.dot(kt, qpads[g], preferred_element_type=F32)
            if use_sel:
                per_tile = tk // SEL_BLOCK
                rows = [jnp.broadcast_to(sel_ref[g, pl.ds(j * per_tile + a, 1), :], (SEL_BLOCK, tq)) for a in range(per_tile)]
                selm = jnp.concatenate(rows, axis=0)
                mask = (jnp.concatenate([selm, selm], axis=1) > 0.0) & base
            else:
                mask = base
            m_new = jnp.maximum(m, jnp.max(jnp.where(mask, s, NEG_INF), axis=0, keepdims=True))
            alpha = jnp.exp(m - m_new)
            e = jnp.where(mask, jnp.exp(s - m_new), 0.0)
            l_new = alpha * l + jnp.sum(e, axis=0, keepdims=True)
            vt = vt_ref[0, g * HEAD_DIM:(g + 1) * HEAD_DIM, pl.ds(off, tk)].astype(BF16)
            acc_new = alpha * acc + jnp.dot(vt, e.astype(BF16), preferred_element_type=F32)
            out += [m_new, l_new, acc_new]
        return tuple(out)

    init = (jnp.full((1, 2 * tq), NEG_INF, F32), jnp.zeros((1, 2 * tq), F32), jnp.zeros((HEAD_DIM, 2 * tq), F32)) * N_KV_A
    per = tk // tq
    hi = (qb + per) // per
    res_s = lax.fori_loop(0, hi, functools.partial(attend, k_ref=ks_ref, vt_ref=vst_ref, use_sel=True), init)
    lo_w = jnp.maximum(qb - WINDOW // tq, 0) // per
    res_w = lax.fori_loop(lo_w, hi, functools.partial(attend, k_ref=kw_ref, vt_ref=vwt_ref, use_sel=False), init)

    for g in range(N_KV_A):
        o_slc = res_s[3 * g + 2] / res_s[3 * g + 1]
        o_win = res_w[3 * g + 2] / res_w[3 * g + 1]
        for r in range(R_A):
            h = 2 * g + r
            gr = gates[3 * h:3 * h + 3, :]
            sl = slice(r * tq, (r + 1) * tq)
            o_ref[0, h * HEAD_DIM:(h + 1) * HEAD_DIM, :] = (gr[0:1] * o_cmps[g][:, sl] + gr[1:2] * o_slc[:, sl]
                                                          + gr[2:3] * o_win[:, sl])


def _nsa_prompt(qt, gt, kc, vct, ovt, kv, vst, vwt, n_cmp):
    b, _, t = qt.shape
    tq = 128
    n_sel = t // SEL_BLOCK
    n_chunks = kc.shape[1]
    per_b = lambda i, j: (i, 0, 0)
    return pl.pallas_call(
        functools.partial(_nsa_prompt_body, n_cmp=n_cmp, n_sel=n_sel, qb_size=tq),
        grid=(b, t // tq),
        in_specs=[pl.BlockSpec((1, 256, tq), lambda i, j: (i, 0, j)),
                  pl.BlockSpec((1, 16, tq), lambda i, j: (i, 0, j)),
                  pl.BlockSpec((1, n_chunks, LANE), per_b),
                  pl.BlockSpec((1, LANE, n_chunks), per_b),
                  pl.BlockSpec((n_sel, n_chunks), lambda i, j: (0, 0)),
                  pl.BlockSpec((1, t, LANE), lambda i, j: (i, 0, 2)),
                  pl.BlockSpec((1, LANE, t), per_b),
                  pl.BlockSpec((1, t, LANE), lambda i, j: (i, 0, 4)),
                  pl.BlockSpec((1, LANE, t), per_b)],
        out_specs=pl.BlockSpec((1, 256, tq), lambda i, j: (i, 0, j)),
        out_shape=jax.ShapeDtypeStruct((b, 256, t), F32),
        scratch_shapes=[pltpu.VMEM((N_KV_A, n_sel, tq), F32)],
        compiler_params=_cp("parallel", "arbitrary"),
        name="nsa_prompt",
    )(qt, gt, kc, vct, ovt, kv, vst, kv, vwt)


def _nsa_prompt_mixer(q, kv, gate, lw, b, t):
    kv3 = kv.reshape(b, t, 768)
    n_chunks = t // CMP_STRIDE
    n_cmp = (t - CMP_BLOCK) // CMP_STRIDE + 1
    kc, _, vct = _nsa_compress(kv3, *lw['cmp'])
    ovt = _cmp_to_sel_t(n_chunks, n_cmp, t // SEL_BLOCK)
    qt = jnp.swapaxes(q.reshape(b, t, 256), 1, 2)
    gt = jnp.swapaxes(gate.reshape(b, t, LANE)[:, :, :16], 1, 2)
    vst = jnp.swapaxes(kv3[:, :, 384:512], 1, 2)
    vwt = jnp.swapaxes(kv3[:, :, 640:768], 1, 2)
    ot = _nsa_prompt(qt, gt, kc, vct, ovt, kv3, vst, vwt, n_cmp)
    return jnp.swapaxes(ot, 1, 2).reshape(b * t, 256)


def _softmax_rows_with_extra(s, mask, s_new):
    m = jnp.maximum(jnp.max(jnp.where(mask, s, NEG_INF), axis=-1, keepdims=True), s_new)
    e = jnp.where(mask, jnp.exp(s - m), 0.0)
    e_new = jnp.exp(s_new - m)
    return e, e_new, 1.0 / (jnp.sum(e, axis=-1, keepdims=True) + e_new)


def _nsa_sample_body(pt_ref, *refs, n_pages, page, pos, n_sel, n_cmp, wb):
    del pt_ref
    n_in = 4 * n_pages
    pages = refs[:n_in]
    qbd_ref, new_ref, gate_ref, win_ref, ov_ref, w1_ref, b1_ref, w2_ref, o_ref, nw_ref, xc_ref = refs[n_in:]
    pg = lambda p, kind: pages[4 * p + kind]
    cpp = page // CMP_STRIDE
    nck = n_pages * cpp
    lane8 = lax.broadcasted_iota(jnp.int32, (8, LANE), 1)
    lo8 = lane8 < HEAD_DIM
    nt = lambda a, b: lax.dot_general(a, b, (((1,), (1,)), ((), ())), preferred_element_type=F32)

    for p in range(n_pages):
        for kind in range(2):
            r = pg(p, kind)
            for pair in range(CMP_STRIDE // 2):
                a = r[0, pl.ds(2 * pair, cpp, stride=CMP_STRIDE), :]
                b = r[0, pl.ds(2 * pair + 1, cpp, stride=CMP_STRIDE), :]
                cols = slice(LANE * pair, LANE * (pair + 1))
                xc_ref[kind, cpp * p:cpp * (p + 1), cols] = jnp.where(lo8, a, pltpu.roll(b, HEAD_DIM, 1))
                xc_ref[kind, nck + cpp * p:nck + cpp * (p + 1), cols] = jnp.where(lo8, pltpu.roll(a, HEAD_DIM, 1), b)
    cmp_out = []
    for kind in range(2):
        hh = jnp.dot(xc_ref[kind].astype(BF16), w1_ref[kind], preferred_element_type=F32)
        hs = []
        for g in range(N_KV_A):
            hg = hh[g * nck:(g + 1) * nck]
            hs.append(jax.nn.gelu(b1_ref[kind] + hg[:, :CMP_HIDDEN] + pltpu.roll(hg[:, CMP_HIDDEN:], nck - 1, 0)))
        cmp_out.append(jnp.dot(jnp.concatenate(hs, axis=1).astype(BF16), w2_ref[kind], preferred_element_type=F32))
    kc, vc = cmp_out

    q = qbd_ref[0] * (HEAD_DIM ** -0.5)
    qb = q.astype(BF16)
    new = new_ref[0]

    n_idx = lax.broadcasted_iota(jnp.int32, (8, nck), 1)
    cmask = (n_idx * CMP_STRIDE + (CMP_BLOCK - 1) <= pos) & (n_idx < n_cmp)
    s = nt(qb, kc.astype(BF16))
    m = jnp.max(jnp.where(cmask, s, NEG_INF), axis=-1, keepdims=True)
    e = jnp.where(cmask, jnp.exp(s - m), 0.0)
    den = jnp.sum(e, axis=-1, keepdims=True)
    p_cmp = e * jnp.where(den > 0.0, 1.0 / den, 0.0)
    o_cmp = jnp.dot(p_cmp.astype(BF16), vc.astype(BF16), preferred_element_type=F32)

    row8 = lax.broadcasted_iota(jnp.int32, (8, nck), 0)
    psum = jnp.where(row8 == 0, p_cmp[0:1] + p_cmp[1:2], jnp.where(row8 == 1, p_cmp[2:3] + p_cmp[3:4], 0.0))
    imp = jnp.dot(psum, ov_ref[...], precision=HI, preferred_element_type=F32)
    cur = pos // SEL_BLOCK
    forced = (lane8 == 0) | (lane8 == cur) | (lane8 == cur - 1)
    imp = jnp.where(forced, FORCED_SCORE, jnp.where(lane8 * SEL_BLOCK <= pos, imp, BLOCKED_SCORE))
    imp = jnp.where(lane8 < n_sel, imp, -3e38)
    rank = jnp.zeros((8, LANE), F32)
    for i in range(n_sel):
        col = imp[:, i:i + 1]
        rank = rank + jnp.where((col > imp) | ((col == imp) & (lane8 > i)), 1.0, 0.0)
    sel = jnp.where((rank < float(min(TOP_K, n_sel))) & (lane8 < n_sel), 1.0, 0.0)
    rsel = lax.broadcasted_iota(jnp.int32, (8, LANE), 0)
    selh = jnp.where(rsel < R_A, sel[0:1], jnp.where(rsel < 2 * R_A, sel[1:2], 0.0))

    per_page = page // SEL_BLOCK
    s_t, m_t = [], []
    for p in range(n_pages):
        s_t.append(nt(qb, pg(p, 2)[0].astype(BF16)))
        blk_sel = selh[:, per_page * p:per_page * p + 1]
        for a in range(1, per_page):
            blk_sel = jnp.where(lane8 < a * SEL_BLOCK, blk_sel, selh[:, per_page * p + a:per_page * p + a + 1])
        kpos = p * page + lane8
        m_t.append((blk_sel > 0.0) & (kpos <= pos))
    s_all = jnp.concatenate(s_t, axis=1)
    mk_all = jnp.concatenate(m_t, axis=1)
    s_new = jnp.sum(q * new[2:3], axis=-1, keepdims=True)
    e, e_new, inv = _softmax_rows_with_extra(s_all, mk_all, s_new)
    acc = e_new * new[3:4]
    for p in range(n_pages):
        acc = acc + jnp.dot(e[:, p * page:(p + 1) * page].astype(BF16), pg(p, 3)[0].astype(BF16), preferred_element_type=F32)
    o_slc = acc * inv

    win = win_ref[0]
    widx = lax.broadcasted_iota(jnp.int32, (8, wb), 1)
    diff = wb - widx
    s_w = nt(qb, win[:, :LANE].astype(BF16))
    s_wnew = jnp.sum(q * new[4:5], axis=-1, keepdims=True)
    e, e_new, inv = _softmax_rows_with_extra(s_w, (diff >= 0) & (diff < WINDOW), s_wnew)
    o_win = (jnp.dot(e.astype(BF16), win[:, LANE:].astype(BF16), preferred_element_type=F32) + e_new * new[5:6]) * inv

    gts = jax.nn.sigmoid(gate_ref[0])
    o = gts[:, 0:1] * o_cmp + gts[:, 1:2] * o_slc + gts[:, 2:3] * o_win
    lo1 = lo8[0:1]
    o_ref[0] = jnp.concatenate([jnp.where(lo1, o[0:1], pltpu.roll(o[1:2], HEAD_DIM, 1)),
                                jnp.where(lo1, pltpu.roll(o[2:3], HEAD_DIM, 1), o[3:4])], axis=1)
    ridx = lax.broadcasted_iota(jnp.int32, (wb, 2 * LANE), 0)
    new_row = jnp.concatenate([new[4:5], new[5:6]], axis=1)
    nw_ref[0] = jnp.where(ridx == wb - 1, new_row, pltpu.roll(win, wb - 1, 0))


def _nsa_sample(q, kv, gate, cache_kv, cache_win, page_table, cmp_w, pos):
    b = q.shape[0]
    n_pool, page = cache_kv.shape[:2]
    n_pages = page_table.shape[1]
    wb = cache_win.shape[1]
    assert wb == WINDOW and page % CMP_STRIDE == 0 and page % SEL_BLOCK == 0 and page == LANE
    tk = n_pages * page + 1
    n_cmp = (tk - CMP_BLOCK) // CMP_STRIDE + 1
    nck = n_pages * page // CMP_STRIDE
    assert n_cmp <= nck
    n_sel = -(-tk // SEL_BLOCK)
    assert n_sel <= LANE and pos // SEL_BLOCK == n_sel - 1
    ov = _pad_to(_cmp_to_sel_t(nck, n_cmp, n_sel).T, LANE, 1)
    w1cat, b1, w2bd = cmp_w
    q4 = q.reshape(b, H_A, HEAD_DIM)
    z = jnp.zeros_like(q4)
    first = (jnp.arange(H_A) // R_A == 0)[None, :, None]
    qbd = jnp.concatenate([jnp.where(first, q4, z), jnp.where(first, z, q4)], axis=-1)
    qbd = _pad_to(qbd, 8, 1)
    new = _pad_to(kv.reshape(b, 6, LANE), 8, 1)
    g8 = _pad_to(_pad_to(gate[:, :3 * H_A].reshape(b, H_A, 3), LANE, 2), 8, 1)
    cache3 = cache_kv.reshape(n_pool, page, 4 * LANE)
    win3 = cache_win.reshape(b, wb, 2 * LANE)
    page_specs = [pl.BlockSpec((1, page, LANE), functools.partial(lambda i, pt, p, kind: (pt[i, p], 0, kind), p=p, kind=kind))
                  for p in range(n_pages) for kind in range(4)]
    per_b = lambda i, pt: (i, 0, 0)
    fixed2 = lambda i, pt: (0, 0)
    fixed3 = lambda i, pt: (0, 0, 0)
    grid_spec = pltpu.PrefetchScalarGridSpec(
        num_scalar_prefetch=1,
        grid=(b,),
        in_specs=page_specs + [pl.BlockSpec((1, 8, LANE), per_b), pl.BlockSpec((1, 8, LANE), per_b), pl.BlockSpec((1, 8, LANE), per_b),
                               pl.BlockSpec((1, wb, 2 * LANE), per_b), pl.BlockSpec((nck, LANE), fixed2),
                               pl.BlockSpec((2, CMP_STRIDE * HEAD_DIM, 2 * CMP_HIDDEN), fixed3),
                               pl.BlockSpec((2, 1, CMP_HIDDEN), fixed3), pl.BlockSpec((2, 2 * CMP_HIDDEN, LANE), fixed3)],
        out_specs=[pl.BlockSpec((1, 1, 256), per_b), pl.BlockSpec((1, wb, 2 * LANE), per_b)],
        scratch_shapes=[pltpu.VMEM((2, 2 * nck, CMP_STRIDE * HEAD_DIM), F32)],
    )
    o, nw = pl.pallas_call(
        functools.partial(_nsa_sample_body, n_pages=n_pages, page=page, pos=pos, n_sel=n_sel, n_cmp=n_cmp, wb=wb),
        grid_spec=grid_spec,
        out_shape=[jax.ShapeDtypeStruct((b, 1, 256), F32), jax.ShapeDtypeStruct((b, wb, 2 * LANE), F32)],
        compiler_params=_cp("parallel"),
        name="nsa_sample",
    )(page_table, *([cache3] * (4 * n_pages)), qbd, new, g8, win3, ov, w1cat, b1, w2bd)
    return o.reshape(b, 256), nw.reshape(b, wb, 2, N_KV_A, HEAD_DIM)


def _rwkv_prep_body(c_ref, s0_ref, mu_ref, vec_ref, wup_ref, aup_ref, gup_ref,
                    r_ref, lw_ref, k_ref, v_ref, kk_ref, ka_ref, g_ref, bonus_ref, carry_ref, *, tiles_per_seq):
    i = pl.program_id(0)
    cols = c_ref[...]
    tm = cols.shape[0]

    @pl.when(i % tiles_per_seq == 0)
    def _():
        carry_ref[...] = s0_ref[0]

    prev = pltpu.roll(cols, 1, 0)
    row0 = lax.broadcasted_iota(jnp.int32, cols.shape, 0) == 0
    prev = jnp.where(row0, carry_ref[...], prev)
    carry_ref[...] = cols[tm - 1:tm, :]
    xs = cols + mu_ref[...] * (prev - cols)
    r, k, v, lo = xs[:, 0:256], xs[:, 256:512], xs[:, 512:768], xs[:, 768:896]
    vec = vec_ref[...]
    w0, a0, k_k, k_a, r_k = vec[0:1], vec[1:2], vec[2:3], vec[3:4], vec[4:5]
    dot_hi = lambda x, w: jnp.dot(x, w, precision=HI, preferred_element_type=F32)
    w_log = -jax.nn.softplus(-(w0 + dot_hi(jnp.tanh(lo), wup_ref[...]))) - 0.5
    a = jax.nn.sigmoid(a0 + dot_hi(lo, aup_ref[...]))
    g_ref[...] = dot_hi(jax.nn.sigmoid(lo), gup_ref[...])
    ones = _block_ones(256, HEAD_DIM, F32)
    kk = k * k_k
    kk = kk * lax.rsqrt(dot_hi(kk * kk, ones) + 1e-12)
    k2 = k * (1.0 + (a - 1.0) * k_a)
    r_ref[...] = r
    lw_ref[...] = -jnp.exp(w_log)
    k_ref[...] = k2
    v_ref[...] = v
    kk_ref[...] = kk
    ka_ref[...] = kk * a
    bonus_ref[...] = dot_hi(r * k2 * r_k, ones) * v


def _rwkv_prep(colsb, shift0, lw, t, tm):
    n = colsb.shape[0]
    tiles_per_seq = t // tm
    row = lambda i: (i, 0)
    fixed = lambda i: (0, 0)
    outs = [jax.ShapeDtypeStruct((n, 256), F32)] * 8
    return pl.pallas_call(
        functools.partial(_rwkv_prep_body, tiles_per_seq=tiles_per_seq),
        grid=(n // tm,),
        in_specs=[pl.BlockSpec((tm, SHIFT_PAD), row),
                  pl.BlockSpec((1, 1, SHIFT_PAD), lambda i: (i // tiles_per_seq, 0, 0)),
                  pl.BlockSpec((1, SHIFT_PAD), fixed), pl.BlockSpec((8, 256), fixed),
                  pl.BlockSpec((LANE, 256), fixed), pl.BlockSpec((LANE, 256), fixed), pl.BlockSpec((LANE, 256), fixed)],
        out_specs=[pl.BlockSpec((tm, 256), row)] * 8,
        out_shape=outs,
        scratch_shapes=[pltpu.VMEM((1, SHIFT_PAD), F32)],
        compiler_params=_cp("arbitrary"),
        name="rwkv_prep",
    )(colsb, shift0.reshape(-1, 1, SHIFT_PAD), lw['rwkv_mu'], lw['rwkv_vec'], lw['rwkv_wup'], lw['rwkv_aup'], lw['rwkv_gup'])


def _rwkv_scan_body(r_ref, lw_ref, k_ref, v_ref, kk_ref, ka_ref, g_ref, bonus_ref, s0_ref, vec_ref,
                    o_ref, st_ref, s_scr, y_scr, *, nb, tl):
    @pl.when(pl.program_id(1) == 0)
    def _():
        s_scr[...] = s0_ref[...]

    ones = _block_ones(256, HEAD_DIM, BF16)
    isub = lax.broadcasted_iota(jnp.int32, (HEAD_DIM, 256), 0)
    ilane = lax.broadcasted_iota(jnp.int32, (HEAD_DIM, 256), 1) % HEAD_DIM
    msel = (isub == ilane).astype(F32)

    nr = nb * HEAD_DIM

    def step(t, carry):
        tp = jnp.maximum(t - 1, 0)
        states = [s_scr[b] for b in range(nb)]
        t1 = jnp.concatenate([states[b] * kk_ref[b, pl.ds(t, 1), :] for b in range(nb)], axis=0)
        t1h = t1.astype(BF16)
        t1l = (t1 - t1h.astype(F32)).astype(BF16)
        t2h, t2l = [], []
        for b in range(nb):
            v = v_ref[b, pl.ds(t, 1), :]
            vh = v.astype(BF16).astype(F32)
            t2h.append((msel * vh).astype(BF16))
            t2l.append((msel * (v - vh)).astype(BF16))
        t3 = [(states[b] * r_ref[b, pl.ds(tp, 1), :]).astype(BF16) for b in range(nb)]
        res_h = jnp.dot(jnp.concatenate([t1h] + t2h + t3, axis=0), ones, preferred_element_type=F32)
        res_l = jnp.dot(jnp.concatenate([t1l] + t2l, axis=0), ones, preferred_element_type=F32)
        for b in range(nb):
            rows = slice(b * HEAD_DIM, (b + 1) * HEAD_DIM)
            sk = res_h[rows] + res_l[rows]
            vrep = res_h[nr:2 * nr][rows] + res_l[nr:2 * nr][rows]
            s_scr[b] = (states[b] * jnp.exp(lw_ref[b, pl.ds(t, 1), :]) - sk * ka_ref[b, pl.ds(t, 1), :]
                        + vrep * k_ref[b, pl.ds(t, 1), :])
            y_scr[b, pl.ds(tp, 1), :] = jnp.sum(res_h[2 * nr:][rows] * msel, axis=0, keepdims=True)
        return carry

    lax.fori_loop(0, tl, step, 0)
    for b in range(nb):
        yrep = jnp.dot((s_scr[b] * r_ref[b, tl - 1:tl, :]).astype(BF16), ones, preferred_element_type=F32)
        y_scr[b, tl - 1:tl, :] = jnp.sum(yrep * msel, axis=0, keepdims=True)
    st_ref[...] = s_scr[...]
    vec = vec_ref[...]
    lnx_w, lnx_b = vec[5:6], vec[6:7]
    avg = _block_ones(256, HEAD_DIM, F32) * (1.0 / HEAD_DIM)
    for b in range(nb):
        y = y_scr[b]
        yc = y - jnp.dot(y, avg, precision=HI, preferred_element_type=F32)
        yn = yc * lax.rsqrt(jnp.dot(yc * yc, avg, precision=HI, preferred_element_type=F32) + RWKV_GN_EPS)
        o_ref[b] = (yn * lnx_w + lnx_b + bonus_ref[b]) * g_ref[b]


def _rwkv_scan(prep, s0, vec, b, t, nb, tl):
    r, lw, k, v, kk, ka, g, bonus = [a.reshape(b, t, 256) for a in prep]
    seq = pl.BlockSpec((nb, tl, 256), lambda i, j: (i, j, 0))
    st = pl.BlockSpec((nb, HEAD_DIM, 256), lambda i, j: (i, 0, 0))
    return pl.pallas_call(
        functools.partial(_rwkv_scan_body, nb=nb, tl=tl),
        grid=(b // nb, t // tl),
        in_specs=[seq] * 8 + [st, pl.BlockSpec((8, 256), lambda i, j: (0, 0))],
        out_specs=[seq, st],
        out_shape=[jax.ShapeDtypeStruct((b, t, 256), F32), jax.ShapeDtypeStruct((b, HEAD_DIM, 256), F32)],
        scratch_shapes=[pltpu.VMEM((nb, HEAD_DIM, 256), F32), pltpu.VMEM((nb, tl, 256), F32)],
        compiler_params=_cp("parallel", "arbitrary"),
        name="rwkv_scan",
    )(r, lw, k, v, kk, ka, g, bonus, s0, vec)


def _rwkv_mixer(colsb, shift0, s0, lw, b, t, tm, nb, tl):
    prep = _rwkv_prep(colsb, shift0, lw, t, tm)
    s0l = jnp.transpose(s0, (0, 2, 1, 3)).reshape(b, HEAD_DIM, 256)
    o, st = _rwkv_scan(prep, s0l, lw['rwkv_vec'], b, t, nb, tl)
    st = jnp.transpose(st.reshape(b, HEAD_DIM, H_B, HEAD_DIM), (0, 2, 1, 3))
    shift = colsb.reshape(b, t, SHIFT_PAD)[:, -1, :SHIFT_B]
    return o.reshape(b * t, 256), st, shift


def _s5_params(lw):
    lr, li = lw['s5_lambda_re'], lw['s5_lambda_im']
    dt = jnp.exp(lw['s5_log_step'])[:, None]
    mag = jnp.exp(lr * dt)
    ar, ai = mag * jnp.cos(li * dt), mag * jnp.sin(li * dt)
    nr, ni = ar - 1.0, ai
    den = lr * lr + li * li
    fr, fi = (nr * lr + ni * li) / den, (ni * lr - nr * li) / den
    b_re, b_im = lw['s5_b'][0], lw['s5_b'][1]
    bbr = fr[..., None] * b_re - fi[..., None] * b_im
    bbi = fr[..., None] * b_im + fi[..., None] * b_re
    eye = jnp.eye(G_C, dtype=F32)
    bd_in = lambda m: jnp.einsum('gpc,gh->gchp', m, eye).reshape(G_C * S5_CH, G_C * S5_P)
    bd_out = lambda m: jnp.einsum('gcp,gh->gphc', m, eye).reshape(G_C * S5_P, G_C * S5_CH)
    b_big = jnp.concatenate([bd_in(bbr), bd_in(bbi)], axis=1)
    c_big = jnp.concatenate([bd_out(lw['s5_c'][0]), -bd_out(lw['s5_c'][1])], axis=0)
    a_row = jnp.concatenate([ar.reshape(1, -1), ai.reshape(1, -1)], axis=1)
    return a_row, b_big, c_big, lw['s5_d'].reshape(1, C_C)


def _s5_body(u_ref, x0_ref, a_ref, b_ref, c_ref, d_ref, wg_ref, o_ref, xt_ref, x_scr, bu_scr, xs_scr, *, nb, tt, mm_dtype, prec):
    @pl.when(pl.program_id(0) == 0)
    def _():
        x_scr[...] = x0_ref[...]

    np_ = G_C * S5_P
    ncb = np_ // LANE
    for b in range(nb):
        bu = jnp.dot(u_ref[b].astype(mm_dtype), b_ref[...], precision=prec, preferred_element_type=F32)
        for cb in range(2 * ncb):
            bu_scr[cb, b * tt:(b + 1) * tt, :] = bu[:, cb * LANE:(cb + 1) * LANE]
    a = a_ref[...]

    def step(t, carry):
        x = x_scr[...]
        rows = pl.ds(t, nb, stride=tt)
        new = [None] * (2 * ncb)
        for cb in range(ncb):
            re, im = slice(cb * LANE, (cb + 1) * LANE), slice(np_ + cb * LANE, np_ + (cb + 1) * LANE)
            ar, ai, xr, xi = a[:, re], a[:, im], x[:, re], x[:, im]
            new[cb] = ar * xr - ai * xi + bu_scr[cb, rows, :]
            new[ncb + cb] = ar * xi + ai * xr + bu_scr[ncb + cb, rows, :]
            xs_scr[cb, rows, :] = new[cb]
            xs_scr[ncb + cb, rows, :] = new[ncb + cb]
        x_scr[...] = jnp.concatenate(new, axis=1)
        return carry

    lax.fori_loop(0, tt, step, 0)
    xt_ref[...] = x_scr[...]
    for b in range(nb):
        u = u_ref[b]
        xs = jnp.concatenate([xs_scr[cb, b * tt:(b + 1) * tt, :] for cb in range(2 * ncb)], axis=1)
        y = jnp.dot(xs.astype(mm_dtype), c_ref[...], precision=prec, preferred_element_type=F32) + d_ref[...] * u
        z = jax.nn.gelu(y)
        o_ref[b] = z * jax.nn.sigmoid(jnp.dot(z.astype(mm_dtype), wg_ref[...], precision=prec, preferred_element_type=F32))


def _s5_mixer(u, x0, lw, b, t, tt, exact):
    a_row, b_big, c_big, d_row = lw['s5']
    mm_dtype = F32 if exact else BF16
    prec = HI if exact else None
    x0l = jnp.concatenate([x0[..., 0].reshape(b, -1), x0[..., 1].reshape(b, -1)], axis=1)
    np2 = 2 * G_C * S5_P
    fixed = lambda i: (0, 0)
    o, xt = pl.pallas_call(
        functools.partial(_s5_body, nb=b, tt=tt, mm_dtype=mm_dtype, prec=prec),
        grid=(t // tt,),
        in_specs=[pl.BlockSpec((b, tt, C_C), lambda i: (0, i, 0)), pl.BlockSpec((b, np2), fixed),
                  pl.BlockSpec((1, np2), fixed), pl.BlockSpec((C_C, np2), fixed), pl.BlockSpec((np2, C_C), fixed),
                  pl.BlockSpec((1, C_C), fixed), pl.BlockSpec((C_C, C_C), fixed)],
        out_specs=[pl.BlockSpec((b, tt, C_C), lambda i: (0, i, 0)), pl.BlockSpec((b, np2), fixed)],
        out_shape=[jax.ShapeDtypeStruct((b, t, C_C), F32), jax.ShapeDtypeStruct((b, np2), F32)],
        scratch_shapes=[pltpu.VMEM((b, np2), F32), pltpu.VMEM((np2 // LANE, b * tt, LANE), F32),
                        pltpu.VMEM((np2 // LANE, b * tt, LANE), F32)],
        compiler_params=_cp("arbitrary"),
        name="s5",
    )(u.reshape(b, t, C_C), x0l, a_row, b_big.astype(mm_dtype), c_big.astype(mm_dtype), d_row, lw['s5_w_glu'].astype(mm_dtype))
    xt = xt.reshape(b, 2, G_C, S5_P)
    return o.reshape(b * t, C_C), jnp.stack([xt[:, 0], xt[:, 1]], axis=-1)


def _ret_tables(pos, c):
    cos, sin = _rope_tables(pos, HEAD_DIM, RET_THETA, HEAD_DIM, H_D)
    log_g = jnp.log1p(-jnp.exp2(-5.0 - jnp.arange(H_D, dtype=F32)))
    i = jnp.arange(c, dtype=F32)
    diff = i[:, None] - i[None, :]
    dmat = jnp.where(diff >= 0, jnp.exp(jnp.maximum(diff, 0.0)[None] * log_g[:, None, None]), 0.0).reshape(H_D * c, c)
    q_dec = jnp.repeat(jnp.exp((i + 1.0)[None] * log_g[:, None]).T, HEAD_DIM, axis=1)
    k_dec = jnp.repeat(jnp.exp((c - 1.0 - i)[None] * log_g[:, None]).T, HEAD_DIM, axis=1)
    chunk_dec = jnp.repeat(jnp.exp(c * log_g), HEAD_DIM).reshape(256, 1)
    return cos, sin, dmat, q_dec, k_dec, chunk_dec


def _ret_body(c_ref, cos_ref, sin_ref, dmat_ref, qdec_ref, kdec_ref, cdec_ref, r0_ref, gn_ref, o_ref, rt_ref, r_scr, *, c):
    @pl.when(pl.program_id(1) == 0)
    def _():
        r_scr[...] = r0_ref[0]

    x = c_ref[0]
    q, k, v, g = x[:, 0:256], x[:, 256:512], x[:, 512:768], x[:, 768:1024]
    cs, sn = cos_ref[...], sin_ref[...]
    lane = lax.broadcasted_iota(jnp.int32, (c, 256), 1)
    first = (lane % HEAD_DIM) < (HEAD_DIM // 2)

    def rope(z):
        sw = jnp.where(first, pltpu.roll(z, 256 - HEAD_DIM // 2, 1), pltpu.roll(z, HEAD_DIM // 2, 1))
        return z * cs + sw * sn

    q = rope(q)
    k = rope(k) * (HEAD_DIM ** -0.5)
    head = lane // HEAD_DIM
    kb, vb = k.astype(BF16), v.astype(BF16)
    qstack = jnp.concatenate([jnp.where(head == h, q, 0.0) for h in range(H_D)], axis=0).astype(BF16)
    s = lax.dot_general(qstack, kb, (((1,), (1,)), ((), ())), preferred_element_type=F32) * dmat_ref[...]
    pv = jnp.dot(s.astype(BF16), vb, preferred_element_type=F32)
    inner = jnp.zeros((c, 256), F32)
    for h in range(H_D):
        inner = inner + jnp.where(head == h, pv[h * c:(h + 1) * c], 0.0)
    r_old = r_scr[...]
    cross = jnp.dot((q * qdec_ref[...]).astype(BF16), r_old.astype(BF16), preferred_element_type=F32)
    kv = lax.dot_general((k * kdec_ref[...]).astype(BF16), vb, (((0,), (0,)), ((), ())), preferred_element_type=F32)
    bd = _block_ones(256, HEAD_DIM, F32)
    r_new = cdec_ref[...] * r_old + kv * bd
    r_scr[...] = r_new
    rt_ref[0] = r_new
    o = inner + cross
    avg = bd * (1.0 / HEAD_DIM)
    oc = o - jnp.dot(o, avg, precision=HI, preferred_element_type=F32)
    on = oc * lax.rsqrt(jnp.dot(oc * oc, avg, precision=HI, preferred_element_type=F32) + RET_GN_EPS)
    o_ref[0] = jax.nn.silu(g) * (on * gn_ref[...])


def _ret_mixer(colsd, r0, lw, tabs, b, t):
    c = RET_CHUNK if t % RET_CHUNK == 0 else t
    cos, sin, dmat, q_dec, k_dec, chunk_dec = tabs
    eye = jnp.eye(H_D, dtype=F32)
    r0l = jnp.einsum('bhde,hg->bhdge', r0, eye).reshape(b, 256, 256)
    n_t = t // c
    fixed = lambda i, j: (0, 0)
    o, rt = pl.pallas_call(
        functools.partial(_ret_body, c=c),
        grid=(b, n_t),
        in_specs=[pl.BlockSpec((1, c, 1024), lambda i, j: (i, j, 0)),
                  pl.BlockSpec((c, 256), lambda i, j: (j, 0)), pl.BlockSpec((c, 256), lambda i, j: (j, 0)),
                  pl.BlockSpec((H_D * c, c), fixed), pl.BlockSpec((c, 256), fixed), pl.BlockSpec((c, 256), fixed),
                  pl.BlockSpec((256, 1), fixed), pl.BlockSpec((1, 256, 256), lambda i, j: (i, 0, 0)),
                  pl.BlockSpec((1, 256), fixed)],
        out_specs=[pl.BlockSpec((1, c, 256), lambda i, j: (i, j, 0)), pl.BlockSpec((1, 256, 256), lambda i, j: (i, 0, 0))],
        out_shape=[jax.ShapeDtypeStruct((b, t, 256), F32), jax.ShapeDtypeStruct((b, 256, 256), F32)],
        scratch_shapes=[pltpu.VMEM((256, 256), F32)],
        compiler_params=_cp("parallel", "arbitrary"),
        name="retention",
    )(colsd.reshape(b, t, 1024), cos, sin, dmat, q_dec, k_dec, chunk_dec, r0l, lw['ret_gn'].reshape(1, 256))
    rt = jnp.einsum('bhdge,hg->bhde', rt.reshape(b, H_D, HEAD_DIM, H_D, HEAD_DIM), eye)
    return o.reshape(b * t, 256), rt


def _partner(x, d, period):
    pos = lax.broadcasted_iota(jnp.int32, x.shape, 1) % period
    return jnp.where(pos + d < period, pltpu.roll(x, LANE - d, 1), pltpu.roll(x, period - d, 1))


def _out_body(x_ref, oa_ref, ob_ref, oc_ref, od_ref, w_ref, nw_ref, wr_ref, br_ref, x1_ref, h_ref, comb_ref):
    acc = x_ref[...]
    for i, ref in enumerate((oa_ref, ob_ref, oc_ref, od_ref)):
        acc = acc + jnp.dot(ref[...].astype(BF16), w_ref[256 * i:256 * (i + 1), :], preferred_element_type=F32)
    x1_ref[...] = acc
    h = acc * lax.rsqrt(jnp.mean(acc * acc, axis=-1, keepdims=True) + RMS_EPS) * nw_ref[...]
    h_ref[...] = h.astype(BF16)
    logits = jnp.dot(h, wr_ref[...], precision=HI, preferred_element_type=F32) + br_ref[...]
    le, lg = logits[:, :LANE], logits[:, LANE:]
    lane = lax.broadcasted_iota(jnp.int32, le.shape, 1)
    mg = jnp.max(lg, axis=-1, keepdims=True)
    eg = jnp.exp(lg - mg)
    pg = eg / (jnp.sum(eg, axis=-1, keepdims=True) * (1.0 / 32.0))
    gidx = (lane % N_EXPERTS) // EXP_PER_GROUP
    g_rank = jnp.zeros_like(pg)
    for d in range(1, N_GROUPS):
        other = pltpu.roll(pg, LANE - EXP_PER_GROUP * d, 1)
        wrapped = gidx + d >= N_GROUPS
        g_rank = g_rank + jnp.where((other > pg) | ((other == pg) & wrapped), 1.0, 0.0)
    kidx = lane % EXP_PER_GROUP
    others = [_partner(le, d, EXP_PER_GROUP) for d in range(1, EXP_PER_GROUP)]
    me = functools.reduce(jnp.maximum, others, le)
    ee = jnp.exp(le - me)
    se = ee
    for d in range(1, EXP_PER_GROUP):
        se = se + _partner(ee, d, EXP_PER_GROUP)
    pe = ee / se
    e_rank = jnp.zeros_like(pe)
    for d in range(1, EXP_PER_GROUP):
        other = _partner(pe, d, EXP_PER_GROUP)
        wrapped = kidx + d >= EXP_PER_GROUP
        e_rank = e_rank + jnp.where((other > pe) | ((other == pe) & wrapped), 1.0, 0.0)
    top = jnp.where(e_rank < 2.0, pe, 0.0)
    den = top
    for d in range(1, EXP_PER_GROUP):
        den = den + _partner(top, d, EXP_PER_GROUP)
    comb = jnp.where((g_rank < 1.0) & (lane < N_EXPERTS), pg * (top / den), 0.0)
    comb_ref[...] = comb


def _out_router(x, oa, ob, oc, od, lw, tm):
    n = x.shape[0]
    row = lambda i: (i, 0)
    fixed = lambda i: (0, 0)
    mix = pl.BlockSpec((tm, 256), row)
    return pl.pallas_call(
        _out_body,
        grid=(n // tm,),
        in_specs=[pl.BlockSpec((tm, D_MODEL), row), mix, mix, mix, mix,
                  pl.BlockSpec((D_MODEL, D_MODEL), fixed), pl.BlockSpec((1, D_MODEL), fixed),
                  pl.BlockSpec((D_MODEL, 2 * LANE), fixed), pl.BlockSpec((1, 2 * LANE), fixed)],
        out_specs=[pl.BlockSpec((tm, D_MODEL), row), pl.BlockSpec((tm, D_MODEL), row), pl.BlockSpec((tm, LANE), row)],
        out_shape=[jax.ShapeDtypeStruct((n, D_MODEL), F32), jax.ShapeDtypeStruct((n, D_MODEL), BF16),
                   jax.ShapeDtypeStruct((n, LANE), F32)],
        compiler_params=_cp("parallel"),
        name="out_router",
    )(x, oa, ob, oc, od, lw['w_out'], lw['norm_ffn'], lw['w_router'], lw['b_router'])


def _router_weights(w_grp, b_grp, w_exp, b_exp):
    we = jnp.transpose(w_exp, (1, 0, 2)).reshape(D_MODEL, N_EXPERTS)
    wg = jnp.repeat(w_grp, EXP_PER_GROUP, axis=1)
    reps = LANE // N_EXPERTS
    w = jnp.concatenate([jnp.tile(we, (1, reps)), jnp.tile(wg, (1, reps))], axis=1)
    b = jnp.concatenate([jnp.tile(b_exp.reshape(1, N_EXPERTS), (1, reps)),
                         jnp.tile(jnp.repeat(b_grp, EXP_PER_GROUP).reshape(1, N_EXPERTS), (1, reps))], axis=1)
    return w, b


def _moe_body(h_ref, comb_ref, x1_ref, wg_ref, wu_ref, wd_ref, nf_ref, x2_ref, y_ref, acc_ref):
    e = pl.program_id(1)

    @pl.when(e == 0)
    def _():
        acc_ref[...] = x1_ref[...]

    h = h_ref[...]
    comb = comb_ref[...]
    lane = lax.broadcasted_iota(jnp.int32, comb.shape, 1)
    c = jnp.sum(jnp.where(lane == e, comb, 0.0), axis=-1, keepdims=True)
    hg = jnp.dot(h, wg_ref[0], preferred_element_type=F32)
    hu = jnp.dot(h, wu_ref[0], preferred_element_type=F32)
    act = (jax.nn.silu(hg) * hu * c).astype(BF16)
    acc_ref[...] += jnp.dot(act, wd_ref[0], preferred_element_type=F32)

    @pl.when(e == N_EXPERTS - 1)
    def _():
        x2 = acc_ref[...]
        x2_ref[...] = x2
        y_ref[...] = x2 * lax.rsqrt(jnp.mean(x2 * x2, axis=-1, keepdims=True) + RMS_EPS) * nf_ref[...]


def _moe(h, comb, x1, lw, norm_final, tm):
    n = x1.shape[0]
    row = lambda i, e: (i, 0)
    per_e = lambda i, e: (e, 0, 0)
    return pl.pallas_call(
        _moe_body,
        grid=(n // tm, N_EXPERTS),
        in_specs=[pl.BlockSpec((tm, D_MODEL), row), pl.BlockSpec((tm, LANE), row), pl.BlockSpec((tm, D_MODEL), row),
                  pl.BlockSpec((1, D_MODEL, D_EXPERT), per_e), pl.BlockSpec((1, D_MODEL, D_EXPERT), per_e),
                  pl.BlockSpec((1, D_EXPERT, D_MODEL), per_e), pl.BlockSpec((1, D_MODEL), lambda i, e: (0, 0))],
        out_specs=[pl.BlockSpec((tm, D_MODEL), row), pl.BlockSpec((tm, D_MODEL), row)],
        out_shape=[jax.ShapeDtypeStruct((n, D_MODEL), F32), jax.ShapeDtypeStruct((n, D_MODEL), F32)],
        scratch_shapes=[pltpu.VMEM((tm, D_MODEL), F32)],
        compiler_params=_cp("parallel", "arbitrary"),
        name="moe",
    )(h, comb, x1, lw['moe_wg'], lw['moe_wu'], lw['moe_wd'], norm_final.reshape(1, D_MODEL))


def _head_norm(y, eps):
    yc = y - jnp.mean(y, -1, keepdims=True)
    return yc * lax.rsqrt(jnp.mean(yc * yc, -1, keepdims=True) + eps)


def _masked_softmax(s, mask):
    s = jnp.where(mask, s.astype(F32), NEG_INF)
    return jnp.where(mask, jax.nn.softmax(s, axis=-1), 0.0)


def _attend(q, k, v, mask):
    s = jnp.einsum('...qgrd,...kgd->...qgrk', q, k) * HEAD_DIM ** -0.5
    p = _masked_softmax(s, mask)
    return jnp.einsum('...qgrk,...kgd->...qgrd', p, v.astype(F32))


def _x_nsa_compress(rows, w1, b1, w2):
    B, Tk = rows.shape[:2]
    m = CMP_BLOCK // CMP_STRIDE
    n_cmp = (Tk - CMP_BLOCK) // CMP_STRIDE + 1
    n_chunks = n_cmp + m - 1
    chunks = rows[:, :n_chunks * CMP_STRIDE].reshape(B, n_chunks, CMP_STRIDE, N_KV_A, HEAD_DIM)
    w1r = w1.reshape(m, CMP_STRIDE, HEAD_DIM, CMP_HIDDEN)
    h = b1.astype(F32)
    for j in range(m):
        h = h + jnp.einsum('bcsgd,sdf->bcgf', chunks[:, j:j + n_cmp], w1r[j])
    return jnp.einsum('bcgf,fd->bcgd', jax.nn.gelu(h), w2)


def _x_nsa_sample(q, kv, gate, pos, cmp_w1, cmp_b1, cmp_w2, past_rows, win_buf):
    B = q.shape[0]
    T = 1
    q = q.reshape(B, T, N_KV_A, R_A, HEAD_DIM)
    kv = kv.reshape(B, T, 6, N_KV_A, HEAD_DIM)
    rows = kv[:, :, 0:4]
    win_rows = kv[:, :, 4:6]
    full = jnp.concatenate([past_rows, rows], axis=1)
    Tk = full.shape[1]
    kc = _x_nsa_compress(full[:, :, 0], cmp_w1[0], cmp_b1[0], cmp_w2[0])
    vc = _x_nsa_compress(full[:, :, 1], cmp_w1[1], cmp_b1[1], cmp_w2[1])
    n_sel = -(-Tk // SEL_BLOCK)
    sel = jnp.pad(full[:, :, 2:4], ((0, 0), (0, n_sel * SEL_BLOCK - Tk), (0, 0), (0, 0), (0, 0)))
    sel = sel.reshape(B, n_sel, SEL_BLOCK, 2, N_KV_A, HEAD_DIM).transpose(3, 0, 4, 1, 2, 5)
    n_cmp = kc.shape[1]
    ov = _cmp_to_sel_t(n_cmp, n_cmp, n_sel).T
    q_pos = jnp.asarray(pos, jnp.int32)
    ks_blk, vs_blk = sel[0], sel[1]
    cmp_end = jnp.arange(n_cmp) * CMP_STRIDE + CMP_BLOCK - 1
    s = jnp.einsum('bqgrd,bngd->bqgrn', q, kc) * HEAD_DIM ** -0.5
    p_cmp = _masked_softmax(s, (cmp_end[None, :] <= q_pos[:, None])[None, :, None, None, :])
    o_c = jnp.einsum('bqgrn,bngd->bqgrd', p_cmp, vc.astype(F32))
    imp = jnp.einsum('bqgrn,nj->bqgj', p_cmp, ov)
    blk = jnp.arange(n_sel)[None, :]
    cur = (q_pos // SEL_BLOCK)[:, None]
    forced = (blk == 0) | (blk == cur) | (blk == cur - 1)
    causal = blk * SEL_BLOCK <= q_pos[:, None]
    imp = jnp.where(forced[None, :, None, :], FORCED_SCORE, jnp.where(causal[None, :, None, :], imp, BLOCKED_SCORE))
    _, idx = lax.top_k(imp, min(TOP_K, n_sel))
    n_k = idx.shape[-1]
    bi = jnp.arange(B)[:, None, None, None]
    gi = jnp.arange(N_KV_A)[None, None, :, None]
    ksel = ks_blk[bi, gi, idx]
    vsel = vs_blk[bi, gi, idx]
    s2 = jnp.einsum('bqgrd,bqgksd->bqgrks', q, ksel) * HEAD_DIM ** -0.5
    kpos = idx[..., None] * SEL_BLOCK + jnp.arange(SEL_BLOCK)
    mask2 = (kpos <= q_pos[None, :, None, None, None]).reshape(B, T, N_KV_A, 1, n_k * SEL_BLOCK)
    p2 = _masked_softmax(s2.reshape(B, T, N_KV_A, R_A, n_k * SEL_BLOCK), mask2)
    o_s = jnp.einsum('bqgrm,bqgmd->bqgrd', p2, vsel.reshape(B, T, N_KV_A, n_k * SEL_BLOCK, HEAD_DIM).astype(F32))
    wb = win_buf.shape[1]
    kw = jnp.concatenate([win_buf, win_rows], axis=1)
    kpos = int(pos[0]) - wb + np.arange(wb + T)
    diff = pos[:, None] - kpos[None, :]
    mask = (diff >= 0) & (diff < WINDOW)
    o_w = _attend(q, kw[:, :, 0], kw[:, :, 1], mask[None, :, None, None, :])
    new_win = kw[:, wb + T - min(WINDOW, wb + T):]
    g = jax.nn.sigmoid(gate.astype(F32)).reshape(B, T, N_KV_A, R_A, 3)
    o = g[..., 0:1] * o_c + g[..., 1:2] * o_s + g[..., 2:3] * o_w
    return o.reshape(B, C_A), rows, new_win


def _x_rwkv_sample(cols, shift0, S0, mu, vec, w_up, a_up, g_up):
    B = cols.shape[0]
    xs = cols + mu * (shift0 - cols)
    r, k, v, wd, ad, gd = jnp.split(xs, _offsets((C_B, C_B, C_B, LORA_W, LORA_A, LORA_G)), axis=-1)
    w0, a0, k_k, k_a, r_k, lnx_w, lnx_b = vec
    w_log = -jax.nn.softplus(-(w0 + jnp.tanh(wd) @ w_up)) - 0.5
    decay = jnp.exp(-jnp.exp(w_log))
    a = jax.nn.sigmoid(a0 + ad @ a_up)
    g = jax.nn.sigmoid(gd) @ g_up
    hs = lambda z: z.reshape(B, H_B, HEAD_DIM)
    kk = hs(k * k_k)
    kk = kk * lax.rsqrt(jnp.sum(kk * kk, -1, keepdims=True) + 1e-12)
    k = k * (1.0 + (a - 1.0) * k_a)
    r_h, w_h, k_h, v_h, a_h = hs(r), hs(decay), hs(k), hs(v), hs(a)
    sk = jnp.einsum('bhij,bhj->bhi', S0, kk)
    S = S0 * w_h[:, :, None, :] - sk[..., None] * (kk * a_h)[:, :, None, :] + v_h[..., None] * k_h[:, :, None, :]
    y = jnp.einsum('bhij,bhj->bhi', S, r_h)
    y = _head_norm(y, RWKV_GN_EPS).reshape(B, C_B) * lnx_w + lnx_b
    bonus = jnp.sum(r_h * k_h * r_k.reshape(H_B, HEAD_DIM), -1, keepdims=True) * v_h
    return (y + bonus.reshape(B, C_B)) * g, S, cols


def _x_s5_sample(u, x0, lw):
    a_row, b_big, c_big, d_row = lw['s5']
    b = u.shape[0]
    np_ = G_C * S5_P
    x0l = jnp.concatenate([x0[..., 0].reshape(b, -1), x0[..., 1].reshape(b, -1)], axis=1)
    bu = jnp.dot(u, b_big, precision=HI)
    ar, ai = a_row[:, :np_], a_row[:, np_:]
    xr = ar * x0l[:, :np_] - ai * x0l[:, np_:] + bu[:, :np_]
    xi = ar * x0l[:, np_:] + ai * x0l[:, :np_] + bu[:, np_:]
    y = jnp.dot(jnp.concatenate([xr, xi], axis=1), c_big, precision=HI) + d_row * u
    z = jax.nn.gelu(y)
    out = z * jax.nn.sigmoid(jnp.dot(z, lw['s5_w_glu'], precision=HI))
    return out, jnp.stack([xr.reshape(b, G_C, S5_P), xi.reshape(b, G_C, S5_P)], axis=-1)


def _x_ret_sample(cols, cos, sin, R0, gn_w):
    B = cols.shape[0]
    q, k, v, g = jnp.split(cols, 4, axis=-1)
    half = HEAD_DIM // 2

    def rope(z):
        z = z.reshape(B, H_D, HEAD_DIM)
        sw = jnp.concatenate([z[..., half:], z[..., :half]], -1)
        return z * cos.reshape(B, H_D, HEAD_DIM) + sw * sin.reshape(B, H_D, HEAD_DIM)

    q = rope(q)
    k = rope(k) * HEAD_DIM ** -0.5
    v = v.reshape(B, H_D, HEAD_DIM)
    gamma = 1.0 - jnp.exp2(-5.0 - jnp.arange(H_D, dtype=F32))
    log_g = jnp.log1p(-jnp.exp2(-5.0 - jnp.arange(H_D, dtype=F32)))
    inner = jnp.einsum('bhd,bhd->bh', q, k)[..., None] * v
    cross = jnp.einsum('bhd,h,bhde->bhe', q, jnp.exp(log_g), R0)
    R = jnp.exp(log_g)[None, :, None, None] * R0 + jnp.einsum('bhd,bhe->bhde', k, v)
    del gamma
    o = _head_norm(inner + cross, RET_GN_EPS).reshape(B, C_D) * gn_w
    return jax.nn.silu(g) * o, R


def _prep_layer(l, p):
    w_in = p['w_in'][l]
    o = _offsets(SPLIT_SIZES)
    segs = jnp.split(w_in, o, axis=1)
    w_all = jnp.concatenate([segs[0], segs[1], _pad_to(segs[2], LANE, 1), _pad_to(segs[3], SHIFT_PAD, 1), segs[4], segs[5]],
                            axis=1).astype(BF16)
    lw = {'w_all': w_all, 'norm_mix': p['norm_mix'][l]}
    lw['cmp'] = _cmp_weights(p['nsa_cmp_w1'][l], p['nsa_cmp_b1'][l], p['nsa_cmp_w2'][l])
    lw['cmp_raw'] = (p['nsa_cmp_w1'][l], p['nsa_cmp_b1'][l], p['nsa_cmp_w2'][l])
    lw['rwkv_mu'] = _pad_to(p['rwkv_mu'][l].reshape(1, SHIFT_B), SHIFT_PAD, 1)
    lw['rwkv_vec'] = _pad_to(p['rwkv_vec'][l], 8, 0)
    z = lambda a, b: jnp.zeros((a, b), F32)
    lw['rwkv_wup'] = jnp.concatenate([p['rwkv_w_up'][l], z(LANE - LORA_W, C_B)], axis=0)
    lw['rwkv_aup'] = jnp.concatenate([z(LORA_W, C_B), p['rwkv_a_up'][l], z(LANE - LORA_W - LORA_A, C_B)], axis=0)
    lw['rwkv_gup'] = jnp.concatenate([z(LORA_W + LORA_A, C_B), p['rwkv_g_up'][l], z(LANE - LORA_W - LORA_A - LORA_G, C_B)], axis=0)
    lw['rwkv_raw'] = (p['rwkv_mu'][l], p['rwkv_vec'][l], p['rwkv_w_up'][l], p['rwkv_a_up'][l], p['rwkv_g_up'][l])
    for name in ('s5_lambda_re', 's5_lambda_im', 's5_b', 's5_c', 's5_d', 's5_log_step', 's5_w_glu', 'ret_gn'):
        lw[name] = p[name][l]
    lw['s5'] = _s5_params(lw)
    lw['w_out'] = p['w_out'][l].astype(BF16)
    lw['norm_ffn'] = p['norm_ffn'][l].reshape(1, D_MODEL)
    lw['w_router'], lw['b_router'] = _router_weights(p['moe_w_grp'][l], p['moe_b_grp'][l], p['moe_w_exp'][l], p['moe_b_exp'][l])
    lw['moe_wg'] = p['moe_w_gate'][l].astype(BF16)
    lw['moe_wu'] = p['moe_w_up'][l].astype(BF16)
    lw['moe_wd'] = p['moe_w_down'][l].astype(BF16)
    return lw


def _prompt_layer(x, lw, tabs, b, t, norm_final):
    cos_a, sin_a, ret_tabs = tabs
    q, kv, gate, colsb, u, colsd = _proj(x, lw['norm_mix'], lw['w_all'], cos_a, sin_a, 512)
    o_a = _nsa_prompt_mixer(q, kv, gate, lw, b, t)
    o_b, s_rwkv, s_shift = _rwkv_mixer(colsb, jnp.zeros((b, SHIFT_PAD), F32), jnp.zeros((b, H_B, HEAD_DIM, HEAD_DIM), F32),
                                       lw, b, t, 512, b, 256)
    o_c, s_s5 = _s5_mixer(u, jnp.zeros((b, G_C, S5_P, 2), F32), lw, b, t, 256, False)
    o_d, s_ret = _ret_mixer(colsd, jnp.zeros((b, H_D, HEAD_DIM, HEAD_DIM), F32), lw, ret_tabs, b, t)
    x1, h, comb = _out_router(x, o_a, o_b, o_c, o_d, lw, 512)
    x2, y = _moe(h, comb, x1, lw, norm_final, 1024)
    kv3 = kv.reshape(b, t, 768)
    rows = kv3[:, :, :512].reshape(b, t, 4, N_KV_A, HEAD_DIM)
    win = kv3[:, t - min(WINDOW, t):, 512:].reshape(b, min(WINDOW, t), 2, N_KV_A, HEAD_DIM)
    return x2, y, (rows, win, s_rwkv, s_shift, s_s5, s_ret)


def _sample_layer(x, lw, tabs, b, pos, cache_kv, page_table, win_buf, s_rwkv, s_shift, s_s5, s_ret, norm_final):
    cos_a, sin_a, ret_cs = tabs
    q, kv, gate, colsb, u, colsd = _proj(x, lw['norm_mix'], lw['w_all'], cos_a, sin_a, b)
    o_a, win = _nsa_sample(q, kv, gate, cache_kv, win_buf, page_table, lw['cmp'], int(pos[0]))
    rows = kv[:, :512].reshape(b, 1, 4, N_KV_A, HEAD_DIM)
    o_b, s_rwkv, s_shift = _x_rwkv_sample(colsb[:, :SHIFT_B], s_shift, s_rwkv, *lw['rwkv_raw'])
    o_c, s_s5 = _x_s5_sample(u, s_s5, lw)
    o_d, s_ret = _x_ret_sample(colsd, ret_cs[0], ret_cs[1], s_ret, lw['ret_gn'])
    x1, h, comb = _out_router(x, o_a, o_b, o_c, o_d, lw, b)
    x2, y = _moe(h, comb, x1, lw, norm_final, b)
    return x2, y, (rows, win, s_rwkv, s_shift, s_s5, s_ret)


def kernel(x_prompt, x_sample, cache_nsa_kv, cache_nsa_win, state_rwkv, state_rwkv_shift, state_s5, state_ret, page_table, norm_mix, w_in, nsa_cmp_w1, nsa_cmp_b1, nsa_cmp_w2, rwkv_mu, rwkv_vec, rwkv_w_up, rwkv_a_up, rwkv_g_up, s5_lambda_re, s5_lambda_im, s5_b, s5_c, s5_d, s5_log_step, s5_w_glu, ret_gn, w_out, norm_ffn, moe_w_grp, moe_b_grp, moe_w_exp, moe_b_exp, moe_w_gate, moe_w_up, moe_w_down, norm_final):
    p = dict(norm_mix=norm_mix, w_in=w_in, nsa_cmp_w1=nsa_cmp_w1, nsa_cmp_b1=nsa_cmp_b1, nsa_cmp_w2=nsa_cmp_w2,
             rwkv_mu=rwkv_mu, rwkv_vec=rwkv_vec, rwkv_w_up=rwkv_w_up, rwkv_a_up=rwkv_a_up, rwkv_g_up=rwkv_g_up,
             s5_lambda_re=s5_lambda_re, s5_lambda_im=s5_lambda_im, s5_b=s5_b, s5_c=s5_c, s5_d=s5_d,
             s5_log_step=s5_log_step, s5_w_glu=s5_w_glu, ret_gn=ret_gn, w_out=w_out, norm_ffn=norm_ffn,
             moe_w_grp=moe_w_grp, moe_b_grp=moe_b_grp, moe_w_exp=moe_w_exp, moe_b_exp=moe_b_exp,
             moe_w_gate=moe_w_gate, moe_w_up=moe_w_up, moe_w_down=moe_w_down)
    bp, tp = x_prompt.shape[:2]
    bs, ts = x_sample.shape[:2]
    past_len = page_table.shape[1] * cache_nsa_kv.shape[2]
    pos_p = np.arange(tp)
    pos_s = past_len + np.arange(ts)
    c = RET_CHUNK if tp % RET_CHUNK == 0 else tp
    tabs_p = _rope_tables(pos_p, ROT_DIM, ROPE_THETA, HEAD_DIM, 2) + (_ret_tables(pos_p, c),)
    pos_rows = np.repeat(pos_s, bs)
    tabs_s = _rope_tables(pos_rows, ROT_DIM, ROPE_THETA, HEAD_DIM, 2) + (_rope_tables(pos_rows, HEAD_DIM, RET_THETA, HEAD_DIM, H_D),)
    xp = x_prompt.reshape(bp * tp, D_MODEL)
    xs = x_sample.reshape(bs * ts, D_MODEL)
    sts_p, sts_s = [], []
    yp = ys = None
    for l in range(DEPTH):
        lw = _prep_layer(l, p)
        xp, yp, st_p = _prompt_layer(xp, lw, tabs_p, bp, tp, norm_final)
        xs, ys, st_s = _sample_layer(xs, lw, tabs_s, bs, pos_s, cache_nsa_kv[l], page_table, cache_nsa_win[l], state_rwkv[l],
                                     state_rwkv_shift[l], state_s5[l], state_ret[l], norm_final)
        rows, win, s1, s2, s3, s4 = st_s
        sts_s.append((rows, win, s1, s2, s3, s4))
        sts_p.append(st_p)
    new_p = [jnp.stack([st[i] for st in sts_p]) for i in range(6)]
    new_s = [jnp.stack([st[i] for st in sts_s]) for i in range(6)]
    return (yp.reshape(bp, tp, D_MODEL), ys.reshape(bs, ts, D_MODEL), new_p[0], new_s[0], new_p[1], new_s[1],
            new_p[2], new_s[2], new_p[3], new_s[3], new_p[4], new_s[4], new_p[5], new_s[5])
```

```python
import functools
import math

import numpy as np
import jax
import jax.numpy as jnp
from jax import lax
from jax.experimental import pallas as pl
from jax.experimental.pallas import tpu as pltpu

F32 = jnp.float32
BF16 = jnp.bfloat16
HI = lax.Precision.HIGHEST

D_MODEL = 1024
DEPTH = 2
HEAD_DIM = 64
C_A = C_B = C_C = C_D = 256
H_A = 4
N_KV_A = 2
R_A = 2
ROT_DIM = 16
ROPE_THETA = 500000.0
CMP_BLOCK = 32
CMP_STRIDE = 16
CMP_HIDDEN = 128
SEL_BLOCK = 64
TOP_K = 16
WINDOW = 512
NEG_INF = -1e30
FORCED_SCORE = 1e9
BLOCKED_SCORE = -1e9
H_B = 4
LORA_W = 16
LORA_A = 16
LORA_G = 32
SHIFT_B = 832
SHIFT_PAD = 896
RWKV_GN_EPS = 64e-5
S5_CH = 16
G_C = 16
S5_P = 64
H_D = 4
RET_CHUNK = 128
RET_THETA = 10000.0
RET_GN_EPS = 1e-5
N_GROUPS = 4
EXP_PER_GROUP = 4
N_EXPERTS = 16
D_EXPERT = 256
RMS_EPS = 1e-6
SPLIT_SIZES = (C_A, 6 * N_KV_A * HEAD_DIM, 3 * H_A, SHIFT_B, C_C, 4 * C_D)
LANE = 128
VMEM_LIMIT = 56 * 1024 * 1024


def _cp(*sem):
    return pltpu.CompilerParams(dimension_semantics=sem, vmem_limit_bytes=VMEM_LIMIT)


def _offsets(sizes):
    return [int(s) for s in np.cumsum(sizes)[:-1]]


def _pad_to(a, n, axis):
    pad = [(0, 0)] * a.ndim
    pad[axis] = (0, n - a.shape[axis])
    return jnp.pad(a, pad)


def _block_ones(n, blk, dtype):
    r = lax.broadcasted_iota(jnp.int32, (n, n), 0) // blk
    c = lax.broadcasted_iota(jnp.int32, (n, n), 1) // blk
    return (r == c).astype(dtype)


def _dot2(x, ones_bf16):
    hi = x.astype(BF16)
    lo = (x - hi.astype(F32)).astype(BF16)
    return (jnp.dot(hi, ones_bf16, preferred_element_type=F32)
            + jnp.dot(lo, ones_bf16, preferred_element_type=F32))


def _rope_tables(pos, rot_dim, theta, period, reps):
    half = rot_dim // 2
    inv = theta ** (-jnp.arange(half, dtype=F32) / half)
    ang = jnp.asarray(pos, F32)[:, None] * inv[None, :]
    cos, sin = jnp.cos(ang), jnp.sin(ang)
    n = ang.shape[0]
    rest = period - rot_dim
    c = jnp.concatenate([cos, cos, jnp.ones((n, rest), F32)], -1)
    s = jnp.concatenate([-sin, sin, jnp.zeros((n, rest), F32)], -1)
    return jnp.tile(c, (1, reps)), jnp.tile(s, (1, reps))


def _proj_body(x_ref, nw_ref, w_ref, cos_ref, sin_ref, q_ref, kv_ref, g_ref, cb_ref, u_ref, cd_ref):
    x = x_ref[...]
    h = x * lax.rsqrt(jnp.mean(x * x, axis=-1, keepdims=True) + RMS_EPS) * nw_ref[...]
    hb = h.astype(BF16)
    c = cos_ref[...]
    s = sin_ref[...]
    first = (lax.broadcasted_iota(jnp.int32, c.shape, 1) % HEAD_DIM) < (ROT_DIM // 2)

    def rope(z):
        sw = jnp.where(first, pltpu.roll(z, LANE - ROT_DIM // 2, 1), pltpu.roll(z, ROT_DIM // 2, 1))
        return z * c + sw * s

    def dot(a, b):
        return jnp.dot(hb, w_ref[:, a:b], preferred_element_type=F32)

    for j in range(2):
        q_ref[:, LANE * j:LANE * (j + 1)] = rope(dot(LANE * j, LANE * (j + 1)))
    for j in range(6):
        z = dot(256 + LANE * j, 256 + LANE * (j + 1))
        kv_ref[:, LANE * j:LANE * (j + 1)] = rope(z) if j % 2 == 0 else z
    g_ref[...] = dot(1024, 1152)
    cb_ref[...] = dot(1152, 2048)
    u_ref[...] = dot(2048, 2304)
    cd_ref[...] = dot(2304, 3328)


def _proj(x2d, norm_w, w_all, cos_t, sin_t, tm):
    n = x2d.shape[0]
    t_tiles = cos_t.shape[0] // tm
    row = lambda i: (i, 0)
    fixed = lambda i: (0, 0)
    tab = lambda i: (i % t_tiles, 0)
    widths = (256, 768, 128, SHIFT_PAD, 256, 1024)
    return pl.pallas_call(
        _proj_body,
        grid=(n // tm,),
        in_specs=[pl.BlockSpec((tm, D_MODEL), row), pl.BlockSpec((1, D_MODEL), fixed),
                  pl.BlockSpec((D_MODEL, 3328), fixed), pl.BlockSpec((tm, LANE), tab), pl.BlockSpec((tm, LANE), tab)],
        out_specs=[pl.BlockSpec((tm, w), row) for w in widths],
        out_shape=[jax.ShapeDtypeStruct((n, w), F32) for w in widths],
        compiler_params=_cp("parallel"),
        name="proj",
    )(x2d, norm_w.reshape(1, D_MODEL), w_all, cos_t, sin_t)


def _cmp_body(xk_ref, xv_ref, w1_ref, b1_ref, w2_ref, kc_ref, vc_ref, vct_ref, xc_ref, *, n_chunks):
    lane = lax.broadcasted_iota(jnp.int32, (n_chunks, LANE), 1)
    lo = lane < HEAD_DIM
    for pair in range(CMP_STRIDE // 2):
        for kind, x_ref in enumerate((xk_ref, xv_ref)):
            ak = x_ref[0, pl.ds(2 * pair, n_chunks, stride=CMP_STRIDE), :]
            bk = x_ref[0, pl.ds(2 * pair + 1, n_chunks, stride=CMP_STRIDE), :]
            xc_ref[kind, 0, :, LANE * pair:LANE * (pair + 1)] = jnp.where(lo, ak, pltpu.roll(bk, HEAD_DIM, 1))
            xc_ref[kind, 1, :, LANE * pair:LANE * (pair + 1)] = jnp.where(lo, pltpu.roll(ak, HEAD_DIM, 1), bk)
    outs = []
    for kind in range(2):
        hs = []
        for g in range(N_KV_A):
            hh = jnp.dot(xc_ref[kind, g].astype(BF16), w1_ref[kind], preferred_element_type=F32)
            h = b1_ref[kind] + hh[:, :CMP_HIDDEN] + pltpu.roll(hh[:, CMP_HIDDEN:], n_chunks - 1, 0)
            hs.append(jax.nn.gelu(h))
        act = jnp.concatenate(hs, axis=1).astype(BF16)
        outs.append(jnp.dot(act, w2_ref[kind], preferred_element_type=F32))
    kc_ref[0] = outs[0]
    vc_ref[0] = outs[1]
    vct_ref[0] = outs[1].T


def _nsa_compress(rows, w1cat, b1, w2bd):
    b, tk = rows.shape[0], rows.shape[1]
    n_chunks = tk // CMP_STRIDE
    fixed3 = lambda i: (0, 0, 0)
    return pl.pallas_call(
        functools.partial(_cmp_body, n_chunks=n_chunks),
        grid=(b,),
        in_specs=[pl.BlockSpec((1, tk, LANE), lambda i: (i, 0, 0)), pl.BlockSpec((1, tk, LANE), lambda i: (i, 0, 1)),
                  pl.BlockSpec((2, CMP_STRIDE * HEAD_DIM, 2 * CMP_HIDDEN), fixed3),
                  pl.BlockSpec((2, 1, CMP_HIDDEN), fixed3),
                  pl.BlockSpec((2, 2 * CMP_HIDDEN, LANE), fixed3)],
        out_specs=[pl.BlockSpec((1, n_chunks, LANE), lambda i: (i, 0, 0)),
                   pl.BlockSpec((1, n_chunks, LANE), lambda i: (i, 0, 0)),
                   pl.BlockSpec((1, LANE, n_chunks), lambda i: (i, 0, 0))],
        out_shape=[jax.ShapeDtypeStruct((b, n_chunks, LANE), F32), jax.ShapeDtypeStruct((b, n_chunks, LANE), F32),
                   jax.ShapeDtypeStruct((b, LANE, n_chunks), F32)],
        scratch_shapes=[pltpu.VMEM((2, N_KV_A, n_chunks, CMP_STRIDE * HEAD_DIM), F32)],
        compiler_params=_cp("parallel"),
        name="nsa_compress",
    )(rows, rows, w1cat, b1, w2bd)


def _cmp_weights(cmp_w1, cmp_b1, cmp_w2):
    m = CMP_BLOCK // CMP_STRIDE
    w1r = cmp_w1.reshape(2, m, CMP_STRIDE * HEAD_DIM, CMP_HIDDEN)
    w1cat = jnp.concatenate([w1r[:, j] for j in range(m)], axis=-1).astype(BF16)
    z = jnp.zeros_like(cmp_w2)
    w2bd = jnp.concatenate([jnp.concatenate([cmp_w2, z], -1), jnp.concatenate([z, cmp_w2], -1)], axis=1).astype(BF16)
    return w1cat, cmp_b1.reshape(2, 1, CMP_HIDDEN), w2bd


def _cmp_to_sel_t(n_chunks, n_cmp, n_sel):
    starts = np.arange(n_chunks) * CMP_STRIDE
    sel_s = np.arange(n_sel) * SEL_BLOCK
    ov = np.minimum(starts[:, None] + CMP_BLOCK, sel_s[None] + SEL_BLOCK) - np.maximum(starts[:, None], sel_s[None])
    ov = np.clip(ov, 0, None) / CMP_BLOCK
    ov[n_cmp:] = 0.0
    return jnp.asarray(ov.T, dtype=F32)


def _masked_softmax_cols(s, mask):
    m = jnp.max(jnp.where(mask, s, NEG_INF), axis=0, keepdims=True)
    e = jnp.where(mask, jnp.exp(s - m), 0.0)
    den = jnp.sum(e, axis=0, keepdims=True)
    return e * jnp.where(den > 0.0, 1.0 / den, 0.0)


def _nsa_prompt_body(qt_ref, gt_ref, kc_ref, vct_ref, ovt_ref, ks_ref, vst_ref, kw_ref, vwt_ref, o_ref, sel_ref,
                     *, n_cmp, n_sel, qb_size):
    qb = pl.program_id(1)
    tq = qb_size
    n_chunks = kc_ref.shape[1]
    qpos = qb * tq + lax.broadcasted_iota(jnp.int32, (1, tq), 1)
    qpos2 = jnp.concatenate([qpos, qpos], axis=1)
    zeros_q = jnp.zeros((HEAD_DIM, 2 * tq), F32)
    gates = jax.nn.sigmoid(gt_ref[0])
    kc = kc_ref[0].astype(BF16)
    n_idx = lax.broadcasted_iota(jnp.int32, (n_chunks, 2 * tq), 0)
    cmp_mask = (n_idx * CMP_STRIDE + (CMP_BLOCK - 1) <= qpos2) & (n_idx < n_cmp)
    blk = lax.broadcasted_iota(jnp.int32, (n_sel, tq), 0)
    cur = qpos // SEL_BLOCK
    forced = (blk == 0) | (blk == cur) | (blk == cur - 1)
    causal_blk = blk * SEL_BLOCK <= qpos
    tk = 2 * tq
    kiota = lax.broadcasted_iota(jnp.int32, (tk, 2 * tq), 0)
    qpads, o_cmps = [], []

    for g in range(N_KV_A):
        q64 = jnp.concatenate([qt_ref[0, (2 * g) * HEAD_DIM:(2 * g + 1) * HEAD_DIM, :],
                               qt_ref[0, (2 * g + 1) * HEAD_DIM:(2 * g + 2) * HEAD_DIM, :]], axis=1) * (HEAD_DIM ** -0.5)
        qpad = (jnp.concatenate([q64, zeros_q], axis=0) if g == 0 else jnp.concatenate([zeros_q, q64], axis=0)).astype(BF16)
        qpads.append(qpad)

        p = _masked_softmax_cols(jnp.dot(kc, qpad, preferred_element_type=F32), cmp_mask)
        o_cmps.append(jnp.dot(vct_ref[0, g * HEAD_DIM:(g + 1) * HEAD_DIM, :].astype(BF16), p.astype(BF16),
                              preferred_element_type=F32))
        psum = p[:, :tq] + p[:, tq:]
        imp = jnp.dot(ovt_ref[...], psum, precision=HI, preferred_element_type=F32)
        imp = jnp.where(forced, FORCED_SCORE, jnp.where(causal_blk, imp, BLOCKED_SCORE))
        rank = jnp.zeros((n_sel, tq), F32)
        for i in range(n_sel):
            row = imp[i:i + 1, :]
            rank = rank + jnp.where((row > imp) | ((row == imp) & (blk > i)), 1.0, 0.0)
        sel_ref[g] = jnp.where(rank < float(min(TOP_K, n_sel)), 1.0, 0.0)

    def attend(j, carry, k_ref, vt_ref, use_sel):
        off = pl.multiple_of(j * tk, tk)
        kt = k_ref[0, pl.ds(off, tk), :].astype(BF16)
        diff = qpos2 - (off + kiota)
        base = (diff >= 0) if use_sel else ((diff >= 0) & (diff < WINDOW))
        out = []
        for g in range(N_KV_A):
            m, l, acc = carry[3 * g:3 * g + 3]
            s = jnp.dot(kt, qpads[g], preferred_element_type=F32)
            if use_sel:
                per_tile = tk // SEL_BLOCK
                rows = [jnp.broadcast_to(sel_ref[g, pl.ds(j * per_tile + a, 1), :], (SEL_BLOCK, tq)) for a in range(per_tile)]
                selm = jnp.concatenate(rows, axis=0)
                mask = (jnp.concatenate([selm, selm], axis=1) > 0.0) & base
            else:
                mask = base
            m_new = jnp.maximum(m, jnp.max(jnp.where(mask, s, NEG_INF), axis=0, keepdims=True))
            alpha = jnp.exp(m - m_new)
            e = jnp.where(mask, jnp.exp(s - m_new), 0.0)
            l_new = alpha * l + jnp.sum(e, axis=0, keepdims=True)
            vt = vt_ref[0, g * HEAD_DIM:(g + 1) * HEAD_DIM, pl.ds(off, tk)].astype(BF16)
            acc_new = alpha * acc + jnp.dot(vt, e.astype(BF16), preferred_element_type=F32)
            out += [m_new, l_new, acc_new]
        return tuple(out)

    init = (jnp.full((1, 2 * tq), NEG_INF, F32), jnp.zeros((1, 2 * tq), F32), jnp.zeros((HEAD_DIM, 2 * tq), F32)) * N_KV_A
    per = tk // tq
    hi = (qb + per) // per
    res_s = lax.fori_loop(0, hi, functools.partial(attend, k_ref=ks_ref, vt_ref=vst_ref, use_sel=True), init)
    lo_w = jnp.maximum(qb - WINDOW // tq, 0) // per
    res_w = lax.fori_loop(lo_w, hi, functools.partial(attend, k_ref=kw_ref, vt_ref=vwt_ref, use_sel=False), init)

    for g in range(N_KV_A):
        o_slc = res_s[3 * g + 2] / res_s[3 * g + 1]
        o_win = res_w[3 * g + 2] / res_w[3 * g + 1]
        for r in range(R_A):
            h = 2 * g + r
            gr = gates[3 * h:3 * h + 3, :]
            sl = slice(r * tq, (r + 1) * tq)
            o_ref[0, h * HEAD_DIM:(h + 1) * HEAD_DIM, :] = (gr[0:1] * o_cmps[g][:, sl] + gr[1:2] * o_slc[:, sl]
                                                          + gr[2:3] * o_win[:, sl])


def _nsa_prompt(qt, gt, kc, vct, ovt, kv, vst, vwt, n_cmp):
    b, _, t = qt.shape
    tq = 128
    n_sel = t // SEL_BLOCK
    n_chunks = kc.shape[1]
    per_b = lambda i, j: (i, 0, 0)
    return pl.pallas_call(
        functools.partial(_nsa_prompt_body, n_cmp=n_cmp, n_sel=n_sel, qb_size=tq),
        grid=(b, t // tq),
        in_specs=[pl.BlockSpec((1, 256, tq), lambda i, j: (i, 0, j)),
                  pl.BlockSpec((1, 16, tq), lambda i, j: (i, 0, j)),
                  pl.BlockSpec((1, n_chunks, LANE), per_b),
                  pl.BlockSpec((1, LANE, n_chunks), per_b),
                  pl.BlockSpec((n_sel, n_chunks), lambda i, j: (0, 0)),
                  pl.BlockSpec((1, t, LANE), lambda i, j: (i, 0, 2)),
                  pl.BlockSpec((1, LANE, t), per_b),
                  pl.BlockSpec((1, t, LANE), lambda i, j: (i, 0, 4)),
                  pl.BlockSpec((1, LANE, t), per_b)],
        out_specs=pl.BlockSpec((1, 256, tq), lambda i, j: (i, 0, j)),
        out_shape=jax.ShapeDtypeStruct((b, 256, t), F32),
        scratch_shapes=[pltpu.VMEM((N_KV_A, n_sel, tq), F32)],
        compiler_params=_cp("parallel", "arbitrary"),
        name="nsa_prompt",
    )(qt, gt, kc, vct, ovt, kv, vst, kv, vwt)


def _nsa_prompt_mixer(q, kv, gate, lw, b, t):
    kv3 = kv.reshape(b, t, 768)
    n_chunks = t // CMP_STRIDE
    n_cmp = (t - CMP_BLOCK) // CMP_STRIDE + 1
    kc, _, vct = _nsa_compress(kv3, *lw['cmp'])
    ovt = _cmp_to_sel_t(n_chunks, n_cmp, t // SEL_BLOCK)
    qt = jnp.swapaxes(q.reshape(b, t, 256), 1, 2)
    gt = jnp.swapaxes(gate.reshape(b, t, LANE)[:, :, :16], 1, 2)
    vst = jnp.swapaxes(kv3[:, :, 384:512], 1, 2)
    vwt = jnp.swapaxes(kv3[:, :, 640:768], 1, 2)
    ot = _nsa_prompt(qt, gt, kc, vct, ovt, kv3, vst, vwt, n_cmp)
    return jnp.swapaxes(ot, 1, 2).reshape(b * t, 256)


def _softmax_rows_with_extra(s, mask, s_new):
    m = jnp.maximum(jnp.max(jnp.where(mask, s, NEG_INF), axis=-1, keepdims=True), s_new)
    e = jnp.where(mask, jnp.exp(s - m), 0.0)
    e_new = jnp.exp(s_new - m)
    return e, e_new, 1.0 / (jnp.sum(e, axis=-1, keepdims=True) + e_new)


def _nsa_sample_body(pt_ref, *refs, n_pages, page, pos, n_sel, n_cmp, wb):
    del pt_ref
    n_in = 4 * n_pages
    pages = refs[:n_in]
    qbd_ref, new_ref, gate_ref, win_ref, ov_ref, w1_ref, b1_ref, w2_ref, o_ref, nw_ref, tok_ref, xc_ref = refs[n_in:]
    pg = lambda p, kind: pages[4 * p + kind]
    nck = n_pages * page // CMP_STRIDE
    lane8 = lax.broadcasted_iota(jnp.int32, (8, LANE), 1)
    lo8 = lane8 < HEAD_DIM
    nt = lambda a, b: lax.dot_general(a, b, (((1,), (1,)), ((), ())), preferred_element_type=F32)

    lo_c = lax.broadcasted_iota(jnp.int32, (nck, LANE), 1) < HEAD_DIM
    for kind in range(2):
        for p in range(n_pages):
            tok_ref[kind, p * page:(p + 1) * page, :] = pg(p, kind)[0].T
        for pair in range(CMP_STRIDE // 2):
            a = tok_ref[kind, pl.ds(2 * pair, nck, stride=CMP_STRIDE), :]
            b = tok_ref[kind, pl.ds(2 * pair + 1, nck, stride=CMP_STRIDE), :]
            cols = slice(LANE * pair, LANE * (pair + 1))
            xc_ref[kind, 0:nck, cols] = jnp.where(lo_c, a, pltpu.roll(b, HEAD_DIM, 1))
            xc_ref[kind, nck:2 * nck, cols] = jnp.where(lo_c, pltpu.roll(a, HEAD_DIM, 1), b)
    cmp_out = []
    for kind in range(2):
        hh = jnp.dot(xc_ref[kind].astype(BF16), w1_ref[kind], preferred_element_type=F32)
        hs = []
        for g in range(N_KV_A):
            hg = hh[g * nck:(g + 1) * nck]
            hs.append(jax.nn.gelu(b1_ref[kind] + hg[:, :CMP_HIDDEN] + pltpu.roll(hg[:, CMP_HIDDEN:], nck - 1, 0)))
        cmp_out.append(jnp.dot(jnp.concatenate(hs, axis=1).astype(BF16), w2_ref[kind], preferred_element_type=F32))
    kc, vc = cmp_out

    q = qbd_ref[0] * (HEAD_DIM ** -0.5)
    qb = q.astype(BF16)
    new = new_ref[0]

    n_idx = lax.broadcasted_iota(jnp.int32, (8, nck), 1)
    cmask = (n_idx * CMP_STRIDE + (CMP_BLOCK - 1) <= pos) & (n_idx < n_cmp)
    s = nt(qb, kc.astype(BF16))
    m = jnp.max(jnp.where(cmask, s, NEG_INF), axis=-1, keepdims=True)
    e = jnp.where(cmask, jnp.exp(s - m), 0.0)
    den = jnp.sum(e, axis=-1, keepdims=True)
    p_cmp = e * jnp.where(den > 0.0, 1.0 / den, 0.0)
    o_cmp = jnp.dot(p_cmp.astype(BF16), vc.astype(BF16), preferred_element_type=F32)

    row8 = lax.broadcasted_iota(jnp.int32, (8, nck), 0)
    psum = jnp.where(row8 == 0, p_cmp[0:1] + p_cmp[1:2], jnp.where(row8 == 1, p_cmp[2:3] + p_cmp[3:4], 0.0))
    imp = jnp.dot(psum, ov_ref[...], precision=HI, preferred_element_type=F32)
    cur = pos // SEL_BLOCK
    forced = (lane8 == 0) | (lane8 == cur) | (lane8 == cur - 1)
    imp = jnp.where(forced, FORCED_SCORE, jnp.where(lane8 * SEL_BLOCK <= pos, imp, BLOCKED_SCORE))
    imp = jnp.where(lane8 < n_sel, imp, -3e38)
    rank = jnp.zeros((8, LANE), F32)
    for i in range(n_sel):
        col = imp[:, i:i + 1]
        rank = rank + jnp.where((col > imp) | ((col == imp) & (lane8 > i)), 1.0, 0.0)
    sel = jnp.where((rank < float(min(TOP_K, n_sel))) & (lane8 < n_sel), 1.0, 0.0)
    rsel = lax.broadcasted_iota(jnp.int32, (8, LANE), 0)
    selh = jnp.where(rsel < R_A, sel[0:1], jnp.where(rsel < 2 * R_A, sel[1:2], 0.0))

    per_page = page // SEL_BLOCK
    s_t, m_t = [], []
    for p in range(n_pages):
        s_t.append(jnp.dot(qb, pg(p, 2)[0].astype(BF16), preferred_element_type=F32))
        blk_sel = selh[:, per_page * p:per_page * p + 1]
        for a in range(1, per_page):
            blk_sel = jnp.where(lane8 < a * SEL_BLOCK, blk_sel, selh[:, per_page * p + a:per_page * p + a + 1])
        kpos = p * page + lane8
        m_t.append((blk_sel > 0.0) & (kpos <= pos))
    s_all = jnp.concatenate(s_t, axis=1)
    mk_all = jnp.concatenate(m_t, axis=1)
    s_new = jnp.sum(q * new[2:3], axis=-1, keepdims=True)
    e, e_new, inv = _softmax_rows_with_extra(s_all, mk_all, s_new)
    acc = e_new * new[3:4]
    for p in range(n_pages):
        acc = acc + nt(e[:, p * page:(p + 1) * page].astype(BF16), pg(p, 3)[0].astype(BF16))
    o_slc = acc * inv

    kw, vw = win_ref[0], win_ref[1]
    widx = lax.broadcasted_iota(jnp.int32, (8, wb), 1)
    diff = wb - widx
    s_w = jnp.dot(qb, kw.astype(BF16), preferred_element_type=F32)
    s_wnew = jnp.sum(q * new[4:5], axis=-1, keepdims=True)
    e, e_new, inv = _softmax_rows_with_extra(s_w, (diff >= 0) & (diff < WINDOW), s_wnew)
    o_win = (nt(e.astype(BF16), vw.astype(BF16)) + e_new * new[5:6]) * inv

    gts = jax.nn.sigmoid(gate_ref[0])
    o = gts[:, 0:1] * o_cmp + gts[:, 1:2] * o_slc + gts[:, 2:3] * o_win
    lo1 = lo8[0:1]
    o_ref[0] = jnp.concatenate([jnp.where(lo1, o[0:1], pltpu.roll(o[1:2], HEAD_DIM, 1)),
                                jnp.where(lo1, pltpu.roll(o[2:3], HEAD_DIM, 1), o[3:4])], axis=1)
    last = lax.broadcasted_iota(jnp.int32, (LANE, wb), 1) == wb - 1
    new_t = new.T
    nw_ref[0] = jnp.where(last, new_t[:, 4:5], pltpu.roll(kw, wb - 1, 1))
    nw_ref[1] = jnp.where(last, new_t[:, 5:6], pltpu.roll(vw, wb - 1, 1))


def _nsa_sample(q, kv, gate, cache_kv, cache_win, layer, page_table, cmp_w, pos):
    b = q.shape[0]
    n_pool, page = cache_kv.shape[1:3]
    n_pages = page_table.shape[1]
    wb = cache_win.shape[2]
    assert wb == WINDOW and page % CMP_STRIDE == 0 and page % SEL_BLOCK == 0 and page == LANE
    tk = n_pages * page + 1
    n_cmp = (tk - CMP_BLOCK) // CMP_STRIDE + 1
    nck = n_pages * page // CMP_STRIDE
    assert n_cmp <= nck
    n_sel = -(-tk // SEL_BLOCK)
    assert n_sel <= LANE and pos // SEL_BLOCK == n_sel - 1
    ov = _pad_to(_cmp_to_sel_t(nck, n_cmp, n_sel).T, LANE, 1)
    w1cat, b1, w2bd = cmp_w
    q4 = q.reshape(b, H_A, HEAD_DIM)
    z = jnp.zeros_like(q4)
    first = (jnp.arange(H_A) // R_A == 0)[None, :, None]
    qbd = jnp.concatenate([jnp.where(first, q4, z), jnp.where(first, z, q4)], axis=-1)
    qbd = _pad_to(qbd, 8, 1)
    new = _pad_to(kv.reshape(b, 6, LANE), 8, 1)
    g8 = _pad_to(_pad_to(gate[:, :3 * H_A].reshape(b, H_A, 3), LANE, 2), 8, 1)
    cache3 = jnp.transpose(cache_kv, (0, 1, 3, 4, 5, 2)).reshape(-1, LANE, page)
    win3 = jnp.transpose(cache_win, (0, 1, 3, 4, 5, 2)).reshape(-1, LANE, wb)
    page_specs = [pl.BlockSpec((1, LANE, page),
                               functools.partial(lambda i, pt, p, kind: ((layer * n_pool + pt[i, p]) * 4 + kind, 0, 0), p=p, kind=kind))
                  for p in range(n_pages) for kind in range(4)]
    per_b = lambda i, pt: (i, 0, 0)
    fixed2 = lambda i, pt: (0, 0)
    fixed3 = lambda i, pt: (0, 0, 0)
    grid_spec = pltpu.PrefetchScalarGridSpec(
        num_scalar_prefetch=1,
        grid=(b,),
        in_specs=page_specs + [pl.BlockSpec((1, 8, LANE), per_b), pl.BlockSpec((1, 8, LANE), per_b), pl.BlockSpec((1, 8, LANE), per_b),
                               pl.BlockSpec((2, LANE, wb), lambda i, pt: (layer * b + i, 0, 0)), pl.BlockSpec((nck, LANE), fixed2),
                               pl.BlockSpec((2, CMP_STRIDE * HEAD_DIM, 2 * CMP_HIDDEN), fixed3),
                               pl.BlockSpec((2, 1, CMP_HIDDEN), fixed3), pl.BlockSpec((2, 2 * CMP_HIDDEN, LANE), fixed3)],
        out_specs=[pl.BlockSpec((1, 1, 256), per_b), pl.BlockSpec((2, LANE, wb), per_b)],
        scratch_shapes=[pltpu.VMEM((2, n_pages * page, LANE), F32), pltpu.VMEM((2, 2 * nck, CMP_STRIDE * HEAD_DIM), F32)],
    )
    o, nw = pl.pallas_call(
        functools.partial(_nsa_sample_body, n_pages=n_pages, page=page, pos=pos, n_sel=n_sel, n_cmp=n_cmp, wb=wb),
        grid_spec=grid_spec,
        out_shape=[jax.ShapeDtypeStruct((b, 1, 256), F32), jax.ShapeDtypeStruct((2 * b, LANE, wb), F32)],
        compiler_params=_cp("parallel"),
        name="nsa_sample",
    )(page_table, *([cache3] * (4 * n_pages)), qbd, new, g8, win3, ov, w1cat, b1, w2bd)
    return o.reshape(b, 256), jnp.transpose(nw.reshape(b, 2, N_KV_A, HEAD_DIM, wb), (0, 4, 1, 2, 3))


def _rwkv_prep_body(c_ref, s0_ref, mu_ref, vec_ref, wup_ref, aup_ref, gup_ref,
                    r_ref, lw_ref, k_ref, v_ref, kk_ref, ka_ref, g_ref, bonus_ref, carry_ref, *, tiles_per_seq):
    i = pl.program_id(0)
    cols = c_ref[...]
    tm = cols.shape[0]

    @pl.when(i % tiles_per_seq == 0)
    def _():
        carry_ref[...] = s0_ref[0]

    prev = pltpu.roll(cols, 1, 0)
    row0 = lax.broadcasted_iota(jnp.int32, cols.shape, 0) == 0
    prev = jnp.where(row0, carry_ref[...], prev)
    carry_ref[...] = cols[tm - 1:tm, :]
    _rwkv_features(cols, prev, mu_ref, vec_ref, wup_ref, aup_ref, gup_ref,
                   r_ref, lw_ref, k_ref, v_ref, kk_ref, ka_ref, g_ref, bonus_ref)


def _rwkv_step_prep_body(c_ref, s0_ref, mu_ref, vec_ref, wup_ref, aup_ref, gup_ref,
                         r_ref, lw_ref, k_ref, v_ref, kk_ref, ka_ref, g_ref, bonus_ref):
    _rwkv_features(c_ref[...], s0_ref[...], mu_ref, vec_ref, wup_ref, aup_ref, gup_ref,
                   r_ref, lw_ref, k_ref, v_ref, kk_ref, ka_ref, g_ref, bonus_ref)


def _rwkv_features(cols, prev, mu_ref, vec_ref, wup_ref, aup_ref, gup_ref,
                   r_ref, lw_ref, k_ref, v_ref, kk_ref, ka_ref, g_ref, bonus_ref):
    xs = cols + mu_ref[...] * (prev - cols)
    r, k, v, lo = xs[:, 0:256], xs[:, 256:512], xs[:, 512:768], xs[:, 768:896]
    vec = vec_ref[...]
    w0, a0, k_k, k_a, r_k = vec[0:1], vec[1:2], vec[2:3], vec[3:4], vec[4:5]
    dot_hi = lambda x, w: jnp.dot(x, w, precision=HI, preferred_element_type=F32)
    w_log = -jax.nn.softplus(-(w0 + dot_hi(jnp.tanh(lo), wup_ref[...]))) - 0.5
    a = jax.nn.sigmoid(a0 + dot_hi(lo, aup_ref[...]))
    g_ref[...] = dot_hi(jax.nn.sigmoid(lo), gup_ref[...])
    ones = _block_ones(256, HEAD_DIM, F32)
    kk = k * k_k
    kk = kk * lax.rsqrt(dot_hi(kk * kk, ones) + 1e-12)
    k2 = k * (1.0 + (a - 1.0) * k_a)
    r_ref[...] = r
    lw_ref[...] = -jnp.exp(w_log)
    k_ref[...] = k2
    v_ref[...] = v
    kk_ref[...] = kk
    ka_ref[...] = kk * a
    bonus_ref[...] = dot_hi(r * k2 * r_k, ones) * v


def _rwkv_prep(colsb, shift0, lw, t, tm):
    n = colsb.shape[0]
    tiles_per_seq = t // tm
    row = lambda i: (i, 0)
    fixed = lambda i: (0, 0)
    outs = [jax.ShapeDtypeStruct((n, 256), F32)] * 8
    return pl.pallas_call(
        functools.partial(_rwkv_prep_body, tiles_per_seq=tiles_per_seq),
        grid=(n // tm,),
        in_specs=[pl.BlockSpec((tm, SHIFT_PAD), row),
                  pl.BlockSpec((1, 1, SHIFT_PAD), lambda i: (i // tiles_per_seq, 0, 0)),
                  pl.BlockSpec((1, SHIFT_PAD), fixed), pl.BlockSpec((8, 256), fixed),
                  pl.BlockSpec((LANE, 256), fixed), pl.BlockSpec((LANE, 256), fixed), pl.BlockSpec((LANE, 256), fixed)],
        out_specs=[pl.BlockSpec((tm, 256), row)] * 8,
        out_shape=outs,
        scratch_shapes=[pltpu.VMEM((1, SHIFT_PAD), F32)],
        compiler_params=_cp("arbitrary"),
        name="rwkv_prep",
    )(colsb, shift0.reshape(-1, 1, SHIFT_PAD), lw['rwkv_mu'], lw['rwkv_vec'], lw['rwkv_wup'], lw['rwkv_aup'], lw['rwkv_gup'])


def _rwkv_scan_body(r_ref, lw_ref, k_ref, v_ref, kk_ref, ka_ref, g_ref, bonus_ref, s0_ref, vec_ref,
                    o_ref, st_ref, s_scr, y_scr, *, nb, tl):
    @pl.when(pl.program_id(1) == 0)
    def _():
        s_scr[...] = s0_ref[...]

    ones = _block_ones(256, HEAD_DIM, BF16)
    isub = lax.broadcasted_iota(jnp.int32, (HEAD_DIM, 256), 0)
    ilane = lax.broadcasted_iota(jnp.int32, (HEAD_DIM, 256), 1) % HEAD_DIM
    msel = (isub == ilane).astype(F32)

    nr = nb * HEAD_DIM

    def step(t, carry):
        tp = jnp.maximum(t - 1, 0)
        states = [s_scr[b] for b in range(nb)]
        t1 = jnp.concatenate([states[b] * kk_ref[b, pl.ds(t, 1), :] for b in range(nb)], axis=0)
        t1h = t1.astype(BF16)
        t1l = (t1 - t1h.astype(F32)).astype(BF16)
        t2h, t2l = [], []
        for b in range(nb):
            v = v_ref[b, pl.ds(t, 1), :]
            vh = v.astype(BF16).astype(F32)
            t2h.append((msel * vh).astype(BF16))
            t2l.append((msel * (v - vh)).astype(BF16))
        t3 = [(states[b] * r_ref[b, pl.ds(tp, 1), :]).astype(BF16) for b in range(nb)]
        res_h = jnp.dot(jnp.concatenate([t1h] + t2h + t3, axis=0), ones, preferred_element_type=F32)
        res_l = jnp.dot(jnp.concatenate([t1l] + t2l, axis=0), ones, preferred_element_type=F32)
        for b in range(nb):
            rows = slice(b * HEAD_DIM, (b + 1) * HEAD_DIM)
            sk = res_h[rows] + res_l[rows]
            vrep = res_h[nr:2 * nr][rows] + res_l[nr:2 * nr][rows]
            s_scr[b] = (states[b] * jnp.exp(lw_ref[b, pl.ds(t, 1), :]) - sk * ka_ref[b, pl.ds(t, 1), :]
                        + vrep * k_ref[b, pl.ds(t, 1), :])
            y_scr[b, pl.ds(tp, 1), :] = jnp.sum(res_h[2 * nr:][rows] * msel, axis=0, keepdims=True)
        return carry

    lax.fori_loop(0, tl, step, 0)
    for b in range(nb):
        yrep = jnp.dot((s_scr[b] * r_ref[b, tl - 1:tl, :]).astype(BF16), ones, preferred_element_type=F32)
        y_scr[b, tl - 1:tl, :] = jnp.sum(yrep * msel, axis=0, keepdims=True)
    st_ref[...] = s_scr[...]
    vec = vec_ref[...]
    lnx_w, lnx_b = vec[5:6], vec[6:7]
    avg = _block_ones(256, HEAD_DIM, F32) * (1.0 / HEAD_DIM)
    for b in range(nb):
        y = y_scr[b]
        yc = y - jnp.dot(y, avg, precision=HI, preferred_element_type=F32)
        yn = yc * lax.rsqrt(jnp.dot(yc * yc, avg, precision=HI, preferred_element_type=F32) + RWKV_GN_EPS)
        o_ref[b] = (yn * lnx_w + lnx_b + bonus_ref[b]) * g_ref[b]


def _rwkv_scan(prep, s0, vec, b, t, nb, tl):
    r, lw, k, v, kk, ka, g, bonus = [a.reshape(b, t, 256) for a in prep]
    seq = pl.BlockSpec((nb, tl, 256), lambda i, j: (i, j, 0))
    st = pl.BlockSpec((nb, HEAD_DIM, 256), lambda i, j: (i, 0, 0))
    return pl.pallas_call(
        functools.partial(_rwkv_scan_body, nb=nb, tl=tl),
        grid=(b // nb, t // tl),
        in_specs=[seq] * 8 + [st, pl.BlockSpec((8, 256), lambda i, j: (0, 0))],
        out_specs=[seq, st],
        out_shape=[jax.ShapeDtypeStruct((b, t, 256), F32), jax.ShapeDtypeStruct((b, HEAD_DIM, 256), F32)],
        scratch_shapes=[pltpu.VMEM((nb, HEAD_DIM, 256), F32), pltpu.VMEM((nb, tl, 256), F32)],
        compiler_params=_cp("parallel", "arbitrary"),
        name="rwkv_scan",
    )(r, lw, k, v, kk, ka, g, bonus, s0, vec)


def _rwkv_mixer(colsb, shift0, s0, lw, b, t, tm, nb, tl):
    prep = _rwkv_prep(colsb, shift0, lw, t, tm)
    s0l = jnp.transpose(s0, (0, 2, 1, 3)).reshape(b, HEAD_DIM, 256)
    o, st = _rwkv_scan(prep, s0l, lw['rwkv_vec'], b, t, nb, tl)
    st = jnp.transpose(st.reshape(b, HEAD_DIM, H_B, HEAD_DIM), (0, 2, 1, 3))
    shift = colsb.reshape(b, t, SHIFT_PAD)[:, -1, :SHIFT_B]
    return o.reshape(b * t, 256), st, shift


def _segment_norm(y, eps):
    avg = _block_ones(256, HEAD_DIM, F32) * (1.0 / HEAD_DIM)
    yc = y - jnp.dot(y, avg, precision=HI, preferred_element_type=F32)
    return yc * lax.rsqrt(jnp.dot(yc * yc, avg, precision=HI, preferred_element_type=F32) + eps)


def _rwkv_step_body(r_ref, lw_ref, k_ref, v_ref, kk_ref, ka_ref, g_ref, bonus_ref, vec_ref, s_ref, o_ref, st_ref, ft_scr, y_scr):
    h = pl.program_id(0)

    @pl.when(h == 0)
    def _():
        for n, ref in enumerate((r_ref, lw_ref, k_ref, v_ref, kk_ref, ka_ref)):
            ft_scr[n] = ref[...].T

    base = pl.multiple_of(h * HEAD_DIM, HEAD_DIM)
    head = lambda n: ft_scr[n, pl.ds(base, HEAD_DIM), :]
    r_t, w_t, k_t, kk_t, ka_t = head(0), jnp.exp(head(1)), head(2), head(4), head(5)

    def body(i, carry):
        rows = pl.ds(pl.multiple_of(i * HEAD_DIM, HEAD_DIM), HEAD_DIM)
        s = s_ref[rows, :]
        sk = jnp.sum(s * kk_t, axis=0, keepdims=True)
        s = s * w_t - sk * ka_t + ft_scr[3, pl.ds(base + i, 1), :] * k_t
        st_ref[rows, :] = s
        y_scr[pl.ds(base + i, 1), :] = jnp.sum(s * r_t, axis=0, keepdims=True)
        return carry

    lax.fori_loop(0, HEAD_DIM, body, 0)

    @pl.when(h == H_B - 1)
    def _():
        vec = vec_ref[...]
        o_ref[...] = (_segment_norm(y_scr[...].T, RWKV_GN_EPS) * vec[5:6] + vec[6:7] + bonus_ref[...]) * g_ref[...]


def _rwkv_step(colsb, shift0, s_all, layer, lw):
    b = colsb.shape[0]
    hd2 = HEAD_DIM * HEAD_DIM
    full = lambda *_: (0, 0)
    feat = pl.BlockSpec((b, 256), full)
    prep = pl.pallas_call(
        _rwkv_step_prep_body,
        grid=(1,),
        in_specs=[pl.BlockSpec((b, SHIFT_PAD), full), pl.BlockSpec((b, SHIFT_PAD), full), pl.BlockSpec((1, SHIFT_PAD), full),
                  pl.BlockSpec((8, 256), full)] + [pl.BlockSpec((LANE, 256), full)] * 3,
        out_specs=[feat] * 8,
        out_shape=[jax.ShapeDtypeStruct((b, 256), F32)] * 8,
        compiler_params=_cp("arbitrary"),
        name="rwkv_step_prep",
    )(colsb, shift0, lw['rwkv_mu'], lw['rwkv_vec'], lw['rwkv_wup'], lw['rwkv_aup'], lw['rwkv_gup'])
    s_rows = jnp.transpose(s_all, (0, 2, 3, 4, 1)).reshape(-1, b)
    o, st = pl.pallas_call(
        _rwkv_step_body,
        grid=(H_B,),
        in_specs=[feat] * 8 + [pl.BlockSpec((8, 256), full), pl.BlockSpec((hd2, b), lambda h: (layer * H_B + h, 0))],
        out_specs=[feat, pl.BlockSpec((hd2, b), lambda h: (h, 0))],
        out_shape=[jax.ShapeDtypeStruct((b, 256), F32), jax.ShapeDtypeStruct((H_B * hd2, b), F32)],
        scratch_shapes=[pltpu.VMEM((6, 256, b), F32), pltpu.VMEM((256, b), F32)],
        compiler_params=_cp("arbitrary"),
        name="rwkv_step",
    )(*prep, lw['rwkv_vec'], s_rows)
    return o, jnp.transpose(st.reshape(H_B, HEAD_DIM, HEAD_DIM, b), (3, 0, 1, 2))


def _ret_step_body(c_ref, cos_ref, sin_ref, gn_ref, r0_ref, o_ref, rt_ref, ft_scr, acc_scr):
    h = pl.program_id(0)

    @pl.when(h == 0)
    def _():
        x = c_ref[...]
        cs, sn = cos_ref[...], sin_ref[...]
        first = (lax.broadcasted_iota(jnp.int32, cs.shape, 1) % HEAD_DIM) < (HEAD_DIM // 2)

        def rope(z):
            sw = jnp.where(first, pltpu.roll(z, 256 - HEAD_DIM // 2, 1), pltpu.roll(z, HEAD_DIM // 2, 1))
            return z * cs + sw * sn

        ft_scr[0] = rope(x[:, 0:256]).T
        ft_scr[1] = (rope(x[:, 256:512]) * (HEAD_DIM ** -0.5)).T
        ft_scr[2] = x[:, 512:768].T

    base = pl.multiple_of(h * HEAD_DIM, HEAD_DIM)
    nseq = rt_ref.shape[1]
    gamma = jnp.exp(jnp.log1p(-jnp.exp2(-5.0 - jnp.full((1, nseq), h, jnp.int32).astype(F32))))
    q_t, k_t, v_t = [ft_scr[n, pl.ds(base, HEAD_DIM), :] for n in range(3)]
    qk = jnp.sum(q_t * k_t, axis=0, keepdims=True)

    def body(d, cross):
        rows = pl.ds(pl.multiple_of(d * HEAD_DIM, HEAD_DIM), HEAD_DIM)
        r_old = r0_ref[rows, :]
        rt_ref[rows, :] = gamma * r_old + ft_scr[1, pl.ds(base + d, 1), :] * v_t
        return cross + ft_scr[0, pl.ds(base + d, 1), :] * r_old

    cross = lax.fori_loop(0, HEAD_DIM, body, jnp.zeros((HEAD_DIM, nseq), F32))
    acc_scr[pl.ds(base, HEAD_DIM), :] = qk * v_t + gamma * cross

    @pl.when(h == H_D - 1)
    def _():
        o_ref[...] = jax.nn.silu(c_ref[:, 768:1024]) * (_segment_norm(acc_scr[...].T, RET_GN_EPS) * gn_ref[...])


def _ret_step(colsd, cos, sin, r_all, layer, gn):
    b = colsd.shape[0]
    hd2 = HEAD_DIM * HEAD_DIM
    full = lambda *_: (0, 0)
    r_rows = jnp.transpose(r_all, (0, 2, 3, 4, 1)).reshape(-1, b)
    o, rt = pl.pallas_call(
        _ret_step_body,
        grid=(H_D,),
        in_specs=[pl.BlockSpec((b, 1024), full), pl.BlockSpec((b, 256), full), pl.BlockSpec((b, 256), full),
                  pl.BlockSpec((1, 256), full), pl.BlockSpec((hd2, b), lambda h: (layer * H_D + h, 0))],
        out_specs=[pl.BlockSpec((b, 256), full), pl.BlockSpec((hd2, b), lambda h: (h, 0))],
        out_shape=[jax.ShapeDtypeStruct((b, 256), F32), jax.ShapeDtypeStruct((H_D * hd2, b), F32)],
        scratch_shapes=[pltpu.VMEM((3, 256, b), F32), pltpu.VMEM((256, b), F32)],
        compiler_params=_cp("arbitrary"),
        name="ret_step",
    )(colsd, cos, sin, gn.reshape(1, 256), r_rows)
    return o, jnp.transpose(rt.reshape(H_D, HEAD_DIM, HEAD_DIM, b), (3, 0, 1, 2))


def _s5_step_body(u_ref, x0_ref, a1_ref, a2_ref, bt_ref, ct_ref, d_ref, wg_ref, o_ref, xt_ref):
    u = u_ref[...]
    dot_hi = lambda a, b: jnp.dot(a, b, precision=HI, preferred_element_type=F32)
    x0 = x0_ref[...]
    n = x0.shape[0]
    even = (lax.broadcasted_iota(jnp.int32, x0.shape, 0) % 2) == 0
    partner = jnp.where(even, pltpu.roll(x0, n - 1, 0), pltpu.roll(x0, 1, 0))
    x = a1_ref[...] * x0 + a2_ref[...] * partner + dot_hi(bt_ref[...], u.T)
    xt_ref[...] = x
    y = dot_hi(ct_ref[...], x).T + d_ref[...] * u
    z = jax.nn.gelu(y)
    o_ref[...] = z * jax.nn.sigmoid(dot_hi(z, wg_ref[...]))


def _s5_step(u, x_all, layer, lw):
    b = u.shape[0]
    a1, a2, bt, ct, d_row = lw['s5_step']
    n = 2 * G_C * S5_P
    full = lambda *_: (0, 0)
    x_rows = jnp.transpose(x_all, (0, 2, 3, 4, 1)).reshape(-1, b)
    o, xt = pl.pallas_call(
        _s5_step_body,
        grid=(1,),
        in_specs=[pl.BlockSpec((b, C_C), full), pl.BlockSpec((n, b), lambda i: (layer, 0)), pl.BlockSpec((n, 1), full),
                  pl.BlockSpec((n, 1), full), pl.BlockSpec((n, C_C), full), pl.BlockSpec((C_C, n), full),
                  pl.BlockSpec((1, C_C), full), pl.BlockSpec((C_C, C_C), full)],
        out_specs=[pl.BlockSpec((b, C_C), full), pl.BlockSpec((n, b), full)],
        out_shape=[jax.ShapeDtypeStruct((b, C_C), F32), jax.ShapeDtypeStruct((n, b), F32)],
        compiler_params=_cp("arbitrary"),
        name="s5_step",
    )(u, x_rows, a1, a2, bt, ct, d_row, lw['s5_w_glu'])
    return o, jnp.transpose(xt.reshape(G_C, S5_P, 2, b), (3, 0, 1, 2))


def _s5_params(lw):
    lr, li = lw['s5_lambda_re'], lw['s5_lambda_im']
    dt = jnp.exp(lw['s5_log_step'])[:, None]
    mag = jnp.exp(lr * dt)
    ar, ai = mag * jnp.cos(li * dt), mag * jnp.sin(li * dt)
    nr, ni = ar - 1.0, ai
    den = lr * lr + li * li
    fr, fi = (nr * lr + ni * li) / den, (ni * lr - nr * li) / den
    b_re, b_im = lw['s5_b'][0], lw['s5_b'][1]
    bbr = fr[..., None] * b_re - fi[..., None] * b_im
    bbi = fr[..., None] * b_im + fi[..., None] * b_re
    eye = jnp.eye(G_C, dtype=F32)
    bd_in = lambda m: jnp.einsum('gpc,gh->gchp', m, eye).reshape(G_C * S5_CH, G_C * S5_P)
    bd_out = lambda m: jnp.einsum('gcp,gh->gphc', m, eye).reshape(G_C * S5_P, G_C * S5_CH)
    b_big = jnp.concatenate([bd_in(bbr), bd_in(bbi)], axis=1)
    c_big = jnp.concatenate([bd_out(lw['s5_c'][0]), -bd_out(lw['s5_c'][1])], axis=0)
    a_row = jnp.concatenate([ar.reshape(1, -1), ai.reshape(1, -1)], axis=1)
    d_row = lw['s5_d'].reshape(1, C_C)
    n = 2 * G_C * S5_P
    a1 = jnp.stack([ar, ar], axis=-1).reshape(n, 1)
    a2 = jnp.stack([-ai, ai], axis=-1).reshape(n, 1)
    bt = jnp.stack([jnp.einsum('gpc,gh->gphc', bbr, eye), jnp.einsum('gpc,gh->gphc', bbi, eye)], axis=2).reshape(n, G_C * S5_CH)
    ct = jnp.stack([jnp.einsum('gcp,gh->gchp', lw['s5_c'][0], eye), -jnp.einsum('gcp,gh->gchp', lw['s5_c'][1], eye)],
                   axis=-1).reshape(G_C * S5_CH, n)
    return (a_row, b_big, c_big, d_row), (a1, a2, bt, ct, d_row)


def _s5_body(u_ref, x0_ref, a_ref, b_ref, c_ref, d_ref, wg_ref, o_ref, xt_ref, x_scr, bu_scr, xs_scr, *, nb, tt, mm_dtype, prec):
    @pl.when(pl.program_id(0) == 0)
    def _():
        x_scr[...] = x0_ref[...]

    np_ = G_C * S5_P
    ncb = np_ // LANE
    for b in range(nb):
        bu = jnp.dot(u_ref[b].astype(mm_dtype), b_ref[...], precision=prec, preferred_element_type=F32)
        for cb in range(2 * ncb):
            bu_scr[cb, b * tt:(b + 1) * tt, :] = bu[:, cb * LANE:(cb + 1) * LANE]
    a = a_ref[...]

    def step(t, carry):
        x = x_scr[...]
        rows = pl.ds(t, nb, stride=tt)
        new = [None] * (2 * ncb)
        for cb in range(ncb):
            re, im = slice(cb * LANE, (cb + 1) * LANE), slice(np_ + cb * LANE, np_ + (cb + 1) * LANE)
            ar, ai, xr, xi = a[:, re], a[:, im], x[:, re], x[:, im]
            new[cb] = ar * xr - ai * xi + bu_scr[cb, rows, :]
            new[ncb + cb] = ar * xi + ai * xr + bu_scr[ncb + cb, rows, :]
            xs_scr[cb, rows, :] = new[cb]
            xs_scr[ncb + cb, rows, :] = new[ncb + cb]
        x_scr[...] = jnp.concatenate(new, axis=1)
        return carry

    lax.fori_loop(0, tt, step, 0)
    xt_ref[...] = x_scr[...]
    for b in range(nb):
        u = u_ref[b]
        xs = jnp.concatenate([xs_scr[cb, b * tt:(b + 1) * tt, :] for cb in range(2 * ncb)], axis=1)
        y = jnp.dot(xs.astype(mm_dtype), c_ref[...], precision=prec, preferred_element_type=F32) + d_ref[...] * u
        z = jax.nn.gelu(y)
        o_ref[b] = z * jax.nn.sigmoid(jnp.dot(z.astype(mm_dtype), wg_ref[...], precision=prec, preferred_element_type=F32))


def _s5_mixer(u, x0, lw, b, t, tt, exact):
    a_row, b_big, c_big, d_row = lw['s5']
    mm_dtype = F32 if exact else BF16
    prec = HI if exact else None
    x0l = jnp.concatenate([x0[..., 0].reshape(b, -1), x0[..., 1].reshape(b, -1)], axis=1)
    np2 = 2 * G_C * S5_P
    fixed = lambda i: (0, 0)
    o, xt = pl.pallas_call(
        functools.partial(_s5_body, nb=b, tt=tt, mm_dtype=mm_dtype, prec=prec),
        grid=(t // tt,),
        in_specs=[pl.BlockSpec((b, tt, C_C), lambda i: (0, i, 0)), pl.BlockSpec((b, np2), fixed),
                  pl.BlockSpec((1, np2), fixed), pl.BlockSpec((C_C, np2), fixed), pl.BlockSpec((np2, C_C), fixed),
                  pl.BlockSpec((1, C_C), fixed), pl.BlockSpec((C_C, C_C), fixed)],
        out_specs=[pl.BlockSpec((b, tt, C_C), lambda i: (0, i, 0)), pl.BlockSpec((b, np2), fixed)],
        out_shape=[jax.ShapeDtypeStruct((b, t, C_C), F32), jax.ShapeDtypeStruct((b, np2), F32)],
        scratch_shapes=[pltpu.VMEM((b, np2), F32), pltpu.VMEM((np2 // LANE, b * tt, LANE), F32),
                        pltpu.VMEM((np2 // LANE, b * tt, LANE), F32)],
        compiler_params=_cp("arbitrary"),
        name="s5",
    )(u.reshape(b, t, C_C), x0l, a_row, b_big.astype(mm_dtype), c_big.astype(mm_dtype), d_row, lw['s5_w_glu'].astype(mm_dtype))
    xt = xt.reshape(b, 2, G_C, S5_P)
    return o.reshape(b * t, C_C), jnp.stack([xt[:, 0], xt[:, 1]], axis=-1)


def _ret_tables(pos, c):
    cos, sin = _rope_tables(pos, HEAD_DIM, RET_THETA, HEAD_DIM, H_D)
    log_g = jnp.log1p(-jnp.exp2(-5.0 - jnp.arange(H_D, dtype=F32)))
    i = jnp.arange(c, dtype=F32)
    diff = i[:, None] - i[None, :]
    dmat = jnp.where(diff >= 0, jnp.exp(jnp.maximum(diff, 0.0)[None] * log_g[:, None, None]), 0.0).reshape(H_D * c, c)
    q_dec = jnp.repeat(jnp.exp((i + 1.0)[None] * log_g[:, None]).T, HEAD_DIM, axis=1)
    k_dec = jnp.repeat(jnp.exp((c - 1.0 - i)[None] * log_g[:, None]).T, HEAD_DIM, axis=1)
    chunk_dec = jnp.repeat(jnp.exp(c * log_g), HEAD_DIM).reshape(256, 1)
    return cos, sin, dmat, q_dec, k_dec, chunk_dec


def _ret_body(c_ref, cos_ref, sin_ref, dmat_ref, qdec_ref, kdec_ref, cdec_ref, r0_ref, gn_ref, o_ref, rt_ref, r_scr, *, c):
    @pl.when(pl.program_id(1) == 0)
    def _():
        r_scr[...] = r0_ref[0]

    x = c_ref[0]
    q, k, v, g = x[:, 0:256], x[:, 256:512], x[:, 512:768], x[:, 768:1024]
    cs, sn = cos_ref[...], sin_ref[...]
    lane = lax.broadcasted_iota(jnp.int32, (c, 256), 1)
    first = (lane % HEAD_DIM) < (HEAD_DIM // 2)

    def rope(z):
        sw = jnp.where(first, pltpu.roll(z, 256 - HEAD_DIM // 2, 1), pltpu.roll(z, HEAD_DIM // 2, 1))
        return z * cs + sw * sn

    q = rope(q)
    k = rope(k) * (HEAD_DIM ** -0.5)
    head = lane // HEAD_DIM
    kb, vb = k.astype(BF16), v.astype(BF16)
    qstack = jnp.concatenate([jnp.where(head == h, q, 0.0) for h in range(H_D)], axis=0).astype(BF16)
    s = lax.dot_general(qstack, kb, (((1,), (1,)), ((), ())), preferred_element_type=F32) * dmat_ref[...]
    pv = jnp.dot(s.astype(BF16), vb, preferred_element_type=F32)
    inner = jnp.zeros((c, 256), F32)
    for h in range(H_D):
        inner = inner + jnp.where(head == h, pv[h * c:(h + 1) * c], 0.0)
    r_old = r_scr[...]
    cross = jnp.dot((q * qdec_ref[...]).astype(BF16), r_old.astype(BF16), preferred_element_type=F32)
    kv = lax.dot_general((k * kdec_ref[...]).astype(BF16), vb, (((0,), (0,)), ((), ())), preferred_element_type=F32)
    bd = _block_ones(256, HEAD_DIM, F32)
    r_new = cdec_ref[...] * r_old + kv * bd
    r_scr[...] = r_new
    rt_ref[0] = r_new
    o = inner + cross
    avg = bd * (1.0 / HEAD_DIM)
    oc = o - jnp.dot(o, avg, precision=HI, preferred_element_type=F32)
    on = oc * lax.rsqrt(jnp.dot(oc * oc, avg, precision=HI, preferred_element_type=F32) + RET_GN_EPS)
    o_ref[0] = jax.nn.silu(g) * (on * gn_ref[...])


def _ret_mixer(colsd, r0, lw, tabs, b, t):
    c = RET_CHUNK if t % RET_CHUNK == 0 else t
    cos, sin, dmat, q_dec, k_dec, chunk_dec = tabs
    eye = jnp.eye(H_D, dtype=F32)
    r0l = jnp.einsum('bhde,hg->bhdge', r0, eye).reshape(b, 256, 256)
    n_t = t // c
    fixed = lambda i, j: (0, 0)
    o, rt = pl.pallas_call(
        functools.partial(_ret_body, c=c),
        grid=(b, n_t),
        in_specs=[pl.BlockSpec((1, c, 1024), lambda i, j: (i, j, 0)),
                  pl.BlockSpec((c, 256), lambda i, j: (j, 0)), pl.BlockSpec((c, 256), lambda i, j: (j, 0)),
                  pl.BlockSpec((H_D * c, c), fixed), pl.BlockSpec((c, 256), fixed), pl.BlockSpec((c, 256), fixed),
                  pl.BlockSpec((256, 1), fixed), pl.BlockSpec((1, 256, 256), lambda i, j: (i, 0, 0)),
                  pl.BlockSpec((1, 256), fixed)],
        out_specs=[pl.BlockSpec((1, c, 256), lambda i, j: (i, j, 0)), pl.BlockSpec((1, 256, 256), lambda i, j: (i, 0, 0))],
        out_shape=[jax.ShapeDtypeStruct((b, t, 256), F32), jax.ShapeDtypeStruct((b, 256, 256), F32)],
        scratch_shapes=[pltpu.VMEM((256, 256), F32)],
        compiler_params=_cp("parallel", "arbitrary"),
        name="retention",
    )(colsd.reshape(b, t, 1024), cos, sin, dmat, q_dec, k_dec, chunk_dec, r0l, lw['ret_gn'].reshape(1, 256))
    rt = jnp.einsum('bhdge,hg->bhde', rt.reshape(b, H_D, HEAD_DIM, H_D, HEAD_DIM), eye)
    return o.reshape(b * t, 256), rt


def _partner(x, d, period):
    pos = lax.broadcasted_iota(jnp.int32, x.shape, 1) % period
    return jnp.where(pos + d < period, pltpu.roll(x, LANE - d, 1), pltpu.roll(x, period - d, 1))


def _out_body(x_ref, oa_ref, ob_ref, oc_ref, od_ref, w_ref, nw_ref, wr_ref, br_ref, x1_ref, h_ref, comb_ref):
    acc = x_ref[...]
    for i, ref in enumerate((oa_ref, ob_ref, oc_ref, od_ref)):
        acc = acc + jnp.dot(ref[...].astype(BF16), w_ref[256 * i:256 * (i + 1), :], preferred_element_type=F32)
    x1_ref[...] = acc
    h = acc * lax.rsqrt(jnp.mean(acc * acc, axis=-1, keepdims=True) + RMS_EPS) * nw_ref[...]
    h_ref[...] = h.astype(BF16)
    logits = jnp.dot(h, wr_ref[...], precision=HI, preferred_element_type=F32) + br_ref[...]
    le, lg = logits[:, :LANE], logits[:, LANE:]
    lane = lax.broadcasted_iota(jnp.int32, le.shape, 1)
    mg = jnp.max(lg, axis=-1, keepdims=True)
    eg = jnp.exp(lg - mg)
    pg = eg / (jnp.sum(eg, axis=-1, keepdims=True) * (1.0 / 32.0))
    gidx = (lane % N_EXPERTS) // EXP_PER_GROUP
    g_rank = jnp.zeros_like(pg)
    for d in range(1, N_GROUPS):
        other = pltpu.roll(pg, LANE - EXP_PER_GROUP * d, 1)
        wrapped = gidx + d >= N_GROUPS
        g_rank = g_rank + jnp.where((other > pg) | ((other == pg) & wrapped), 1.0, 0.0)
    kidx = lane % EXP_PER_GROUP
    others = [_partner(le, d, EXP_PER_GROUP) for d in range(1, EXP_PER_GROUP)]
    me = functools.reduce(jnp.maximum, others, le)
    ee = jnp.exp(le - me)
    se = ee
    for d in range(1, EXP_PER_GROUP):
        se = se + _partner(ee, d, EXP_PER_GROUP)
    pe = ee / se
    e_rank = jnp.zeros_like(pe)
    for d in range(1, EXP_PER_GROUP):
        other = _partner(pe, d, EXP_PER_GROUP)
        wrapped = kidx + d >= EXP_PER_GROUP
        e_rank = e_rank + jnp.where((other > pe) | ((other == pe) & wrapped), 1.0, 0.0)
    top = jnp.where(e_rank < 2.0, pe, 0.0)
    den = top
    for d in range(1, EXP_PER_GROUP):
        den = den + _partner(top, d, EXP_PER_GROUP)
    comb = jnp.where((g_rank < 1.0) & (lane < N_EXPERTS), pg * (top / den), 0.0)
    comb_ref[...] = comb


def _out_router(x, oa, ob, oc, od, lw, tm):
    n = x.shape[0]
    row = lambda i: (i, 0)
    fixed = lambda i: (0, 0)
    mix = pl.BlockSpec((tm, 256), row)
    return pl.pallas_call(
        _out_body,
        grid=(n // tm,),
        in_specs=[pl.BlockSpec((tm, D_MODEL), row), mix, mix, mix, mix,
                  pl.BlockSpec((D_MODEL, D_MODEL), fixed), pl.BlockSpec((1, D_MODEL), fixed),
                  pl.BlockSpec((D_MODEL, 2 * LANE), fixed), pl.BlockSpec((1, 2 * LANE), fixed)],
        out_specs=[pl.BlockSpec((tm, D_MODEL), row), pl.BlockSpec((tm, D_MODEL), row), pl.BlockSpec((tm, LANE), row)],
        out_shape=[jax.ShapeDtypeStruct((n, D_MODEL), F32), jax.ShapeDtypeStruct((n, D_MODEL), BF16),
                   jax.ShapeDtypeStruct((n, LANE), F32)],
        compiler_params=_cp("parallel"),
        name="out_router",
    )(x, oa, ob, oc, od, lw['w_out'], lw['norm_ffn'], lw['w_router'], lw['b_router'])


def _router_weights(w_grp, b_grp, w_exp, b_exp):
    we = jnp.transpose(w_exp, (1, 0, 2)).reshape(D_MODEL, N_EXPERTS)
    wg = jnp.repeat(w_grp, EXP_PER_GROUP, axis=1)
    reps = LANE // N_EXPERTS
    w = jnp.concatenate([jnp.tile(we, (1, reps)), jnp.tile(wg, (1, reps))], axis=1)
    b = jnp.concatenate([jnp.tile(b_exp.reshape(1, N_EXPERTS), (1, reps)),
                         jnp.tile(jnp.repeat(b_grp, EXP_PER_GROUP).reshape(1, N_EXPERTS), (1, reps))], axis=1)
    return w, b


def _moe_body(h_ref, comb_ref, x1_ref, wg_ref, wu_ref, wd_ref, nf_ref, *out_refs, final):
    acc_ref = out_refs[-1]
    e = pl.program_id(1)

    @pl.when(e == 0)
    def _():
        acc_ref[...] = x1_ref[...]

    h = h_ref[...]
    comb = comb_ref[...]
    lane = lax.broadcasted_iota(jnp.int32, comb.shape, 1)
    c = jnp.sum(jnp.where(lane == e, comb, 0.0), axis=-1, keepdims=True)
    hg = jnp.dot(h, wg_ref[0], preferred_element_type=F32)
    hu = jnp.dot(h, wu_ref[0], preferred_element_type=F32)
    act = (jax.nn.silu(hg) * hu * c).astype(BF16)
    acc_ref[...] += jnp.dot(act, wd_ref[0], preferred_element_type=F32)

    @pl.when(e == N_EXPERTS - 1)
    def _():
        x2 = acc_ref[...]
        if final:
            out_refs[0][...] = x2 * lax.rsqrt(jnp.mean(x2 * x2, axis=-1, keepdims=True) + RMS_EPS) * nf_ref[...]
        else:
            out_refs[0][...] = x2


def _moe(h, comb, x1, lw, norm_final, tm, final):
    n = x1.shape[0]
    row = lambda i, e: (i, 0)
    per_e = lambda i, e: (e, 0, 0)
    return pl.pallas_call(
        functools.partial(_moe_body, final=final),
        grid=(n // tm, N_EXPERTS),
        in_specs=[pl.BlockSpec((tm, D_MODEL), row), pl.BlockSpec((tm, LANE), row), pl.BlockSpec((tm, D_MODEL), row),
                  pl.BlockSpec((1, D_MODEL, D_EXPERT), per_e), pl.BlockSpec((1, D_MODEL, D_EXPERT), per_e),
                  pl.BlockSpec((1, D_EXPERT, D_MODEL), per_e), pl.BlockSpec((1, D_MODEL), lambda i, e: (0, 0))],
        out_specs=pl.BlockSpec((tm, D_MODEL), row),
        out_shape=jax.ShapeDtypeStruct((n, D_MODEL), F32),
        scratch_shapes=[pltpu.VMEM((tm, D_MODEL), F32)],
        compiler_params=_cp("parallel", "arbitrary"),
        name="moe",
    )(h, comb, x1, lw['moe_wg'], lw['moe_wu'], lw['moe_wd'], norm_final.reshape(1, D_MODEL))


def _head_norm(y, eps):
    yc = y - jnp.mean(y, -1, keepdims=True)
    return yc * lax.rsqrt(jnp.mean(yc * yc, -1, keepdims=True) + eps)


def _masked_softmax(s, mask):
    s = jnp.where(mask, s.astype(F32), NEG_INF)
    return jnp.where(mask, jax.nn.softmax(s, axis=-1), 0.0)


def _attend(q, k, v, mask):
    s = jnp.einsum('...qgrd,...kgd->...qgrk', q, k) * HEAD_DIM ** -0.5
    p = _masked_softmax(s, mask)
    return jnp.einsum('...qgrk,...kgd->...qgrd', p, v.astype(F32))


def _x_nsa_compress(rows, w1, b1, w2):
    B, Tk = rows.shape[:2]
    m = CMP_BLOCK // CMP_STRIDE
    n_cmp = (Tk - CMP_BLOCK) // CMP_STRIDE + 1
    n_chunks = n_cmp + m - 1
    chunks = rows[:, :n_chunks * CMP_STRIDE].reshape(B, n_chunks, CMP_STRIDE, N_KV_A, HEAD_DIM)
    w1r = w1.reshape(m, CMP_STRIDE, HEAD_DIM, CMP_HIDDEN)
    h = b1.astype(F32)
    for j in range(m):
        h = h + jnp.einsum('bcsgd,sdf->bcgf', chunks[:, j:j + n_cmp], w1r[j])
    return jnp.einsum('bcgf,fd->bcgd', jax.nn.gelu(h), w2)


def _x_nsa_sample(q, kv, gate, pos, cmp_w1, cmp_b1, cmp_w2, past_rows, win_buf):
    B = q.shape[0]
    T = 1
    q = q.reshape(B, T, N_KV_A, R_A, HEAD_DIM)
    kv = kv.reshape(B, T, 6, N_KV_A, HEAD_DIM)
    rows = kv[:, :, 0:4]
    win_rows = kv[:, :, 4:6]
    full = jnp.concatenate([past_rows, rows], axis=1)
    Tk = full.shape[1]
    kc = _x_nsa_compress(full[:, :, 0], cmp_w1[0], cmp_b1[0], cmp_w2[0])
    vc = _x_nsa_compress(full[:, :, 1], cmp_w1[1], cmp_b1[1], cmp_w2[1])
    n_sel = -(-Tk // SEL_BLOCK)
    sel = jnp.pad(full[:, :, 2:4], ((0, 0), (0, n_sel * SEL_BLOCK - Tk), (0, 0), (0, 0), (0, 0)))
    sel = sel.reshape(B, n_sel, SEL_BLOCK, 2, N_KV_A, HEAD_DIM).transpose(3, 0, 4, 1, 2, 5)
    n_cmp = kc.shape[1]
    ov = _cmp_to_sel_t(n_cmp, n_cmp, n_sel).T
    q_pos = jnp.asarray(pos, jnp.int32)
    ks_blk, vs_blk = sel[0], sel[1]
    cmp_end = jnp.arange(n_cmp) * CMP_STRIDE + CMP_BLOCK - 1
    s = jnp.einsum('bqgrd,bngd->bqgrn', q, kc) * HEAD_DIM ** -0.5
    p_cmp = _masked_softmax(s, (cmp_end[None, :] <= q_pos[:, None])[None, :, None, None, :])
    o_c = jnp.einsum('bqgrn,bngd->bqgrd', p_cmp, vc.astype(F32))
    imp = jnp.einsum('bqgrn,nj->bqgj', p_cmp, ov)
    blk = jnp.arange(n_sel)[None, :]
    cur = (q_pos // SEL_BLOCK)[:, None]
    forced = (blk == 0) | (blk == cur) | (blk == cur - 1)
    causal = blk * SEL_BLOCK <= q_pos[:, None]
    imp = jnp.where(forced[None, :, None, :], FORCED_SCORE, jnp.where(causal[None, :, None, :], imp, BLOCKED_SCORE))
    _, idx = lax.top_k(imp, min(TOP_K, n_sel))
    n_k = idx.shape[-1]
    bi = jnp.arange(B)[:, None, None, None]
    gi = jnp.arange(N_KV_A)[None, None, :, None]
    ksel = ks_blk[bi, gi, idx]
    vsel = vs_blk[bi, gi, idx]
    s2 = jnp.einsum('bqgrd,bqgksd->bqgrks', q, ksel) * HEAD_DIM ** -0.5
    kpos = idx[..., None] * SEL_BLOCK + jnp.arange(SEL_BLOCK)
    mask2 = (kpos <= q_pos[None, :, None, None, None]).reshape(B, T, N_KV_A, 1, n_k * SEL_BLOCK)
    p2 = _masked_softmax(s2.reshape(B, T, N_KV_A, R_A, n_k * SEL_BLOCK), mask2)
    o_s = jnp.einsum('bqgrm,bqgmd->bqgrd', p2, vsel.reshape(B, T, N_KV_A, n_k * SEL_BLOCK, HEAD_DIM).astype(F32))
    wb = win_buf.shape[1]
    kw = jnp.concatenate([win_buf, win_rows], axis=1)
    kpos = int(pos[0]) - wb + np.arange(wb + T)
    diff = pos[:, None] - kpos[None, :]
    mask = (diff >= 0) & (diff < WINDOW)
    o_w = _attend(q, kw[:, :, 0], kw[:, :, 1], mask[None, :, None, None, :])
    new_win = kw[:, wb + T - min(WINDOW, wb + T):]
    g = jax.nn.sigmoid(gate.astype(F32)).reshape(B, T, N_KV_A, R_A, 3)
    o = g[..., 0:1] * o_c + g[..., 1:2] * o_s + g[..., 2:3] * o_w
    return o.reshape(B, C_A), rows, new_win


def _x_rwkv_sample(cols, shift0, S0, mu, vec, w_up, a_up, g_up):
    B = cols.shape[0]
    xs = cols + mu * (shift0 - cols)
    r, k, v, wd, ad, gd = jnp.split(xs, _offsets((C_B, C_B, C_B, LORA_W, LORA_A, LORA_G)), axis=-1)
    w0, a0, k_k, k_a, r_k, lnx_w, lnx_b = vec
    w_log = -jax.nn.softplus(-(w0 + jnp.tanh(wd) @ w_up)) - 0.5
    decay = jnp.exp(-jnp.exp(w_log))
    a = jax.nn.sigmoid(a0 + ad @ a_up)
    g = jax.nn.sigmoid(gd) @ g_up
    hs = lambda z: z.reshape(B, H_B, HEAD_DIM)
    kk = hs(k * k_k)
    kk = kk * lax.rsqrt(jnp.sum(kk * kk, -1, keepdims=True) + 1e-12)
    k = k * (1.0 + (a - 1.0) * k_a)
    r_h, w_h, k_h, v_h, a_h = hs(r), hs(decay), hs(k), hs(v), hs(a)
    sk = jnp.einsum('bhij,bhj->bhi', S0, kk)
    S = S0 * w_h[:, :, None, :] - sk[..., None] * (kk * a_h)[:, :, None, :] + v_h[..., None] * k_h[:, :, None, :]
    y = jnp.einsum('bhij,bhj->bhi', S, r_h)
    y = _head_norm(y, RWKV_GN_EPS).reshape(B, C_B) * lnx_w + lnx_b
    bonus = jnp.sum(r_h * k_h * r_k.reshape(H_B, HEAD_DIM), -1, keepdims=True) * v_h
    return (y + bonus.reshape(B, C_B)) * g, S, cols


def _x_s5_sample(u, x0, lw):
    a_row, b_big, c_big, d_row = lw['s5']
    b = u.shape[0]
    np_ = G_C * S5_P
    x0l = jnp.concatenate([x0[..., 0].reshape(b, -1), x0[..., 1].reshape(b, -1)], axis=1)
    bu = jnp.dot(u, b_big, precision=HI)
    ar, ai = a_row[:, :np_], a_row[:, np_:]
    xr = ar * x0l[:, :np_] - ai * x0l[:, np_:] + bu[:, :np_]
    xi = ar * x0l[:, np_:] + ai * x0l[:, :np_] + bu[:, np_:]
    y = jnp.dot(jnp.concatenate([xr, xi], axis=1), c_big, precision=HI) + d_row * u
    z = jax.nn.gelu(y)
    out = z * jax.nn.sigmoid(jnp.dot(z, lw['s5_w_glu'], precision=HI))
    return out, jnp.stack([xr.reshape(b, G_C, S5_P), xi.reshape(b, G_C, S5_P)], axis=-1)


def _x_ret_sample(cols, cos, sin, R0, gn_w):
    B = cols.shape[0]
    q, k, v, g = jnp.split(cols, 4, axis=-1)
    half = HEAD_DIM // 2

    def rope(z):
        z = z.reshape(B, H_D, HEAD_DIM)
        sw = jnp.concatenate([z[..., half:], z[..., :half]], -1)
        return z * cos.reshape(B, H_D, HEAD_DIM) + sw * sin.reshape(B, H_D, HEAD_DIM)

    q = rope(q)
    k = rope(k) * HEAD_DIM ** -0.5
    v = v.reshape(B, H_D, HEAD_DIM)
    gamma = 1.0 - jnp.exp2(-5.0 - jnp.arange(H_D, dtype=F32))
    log_g = jnp.log1p(-jnp.exp2(-5.0 - jnp.arange(H_D, dtype=F32)))
    inner = jnp.einsum('bhd,bhd->bh', q, k)[..., None] * v
    cross = jnp.einsum('bhd,h,bhde->bhe', q, jnp.exp(log_g), R0)
    R = jnp.exp(log_g)[None, :, None, None] * R0 + jnp.einsum('bhd,bhe->bhde', k, v)
    del gamma
    o = _head_norm(inner + cross, RET_GN_EPS).reshape(B, C_D) * gn_w
    return jax.nn.silu(g) * o, R


def _prep_layer(l, p):
    w_in = p['w_in'][l]
    o = _offsets(SPLIT_SIZES)
    segs = jnp.split(w_in, o, axis=1)
    w_all = jnp.concatenate([segs[0], segs[1], _pad_to(segs[2], LANE, 1), _pad_to(segs[3], SHIFT_PAD, 1), segs[4], segs[5]],
                            axis=1).astype(BF16)
    lw = {'layer': l, 'w_all': w_all, 'norm_mix': p['norm_mix'][l]}
    lw['cmp'] = _cmp_weights(p['nsa_cmp_w1'][l], p['nsa_cmp_b1'][l], p['nsa_cmp_w2'][l])
    lw['cmp_raw'] = (p['nsa_cmp_w1'][l], p['nsa_cmp_b1'][l], p['nsa_cmp_w2'][l])
    lw['rwkv_mu'] = _pad_to(p['rwkv_mu'][l].reshape(1, SHIFT_B), SHIFT_PAD, 1)
    lw['rwkv_vec'] = _pad_to(p['rwkv_vec'][l], 8, 0)
    z = lambda a, b: jnp.zeros((a, b), F32)
    lw['rwkv_wup'] = jnp.concatenate([p['rwkv_w_up'][l], z(LANE - LORA_W, C_B)], axis=0)
    lw['rwkv_aup'] = jnp.concatenate([z(LORA_W, C_B), p['rwkv_a_up'][l], z(LANE - LORA_W - LORA_A, C_B)], axis=0)
    lw['rwkv_gup'] = jnp.concatenate([z(LORA_W + LORA_A, C_B), p['rwkv_g_up'][l], z(LANE - LORA_W - LORA_A - LORA_G, C_B)], axis=0)
    lw['rwkv_raw'] = (p['rwkv_mu'][l], p['rwkv_vec'][l], p['rwkv_w_up'][l], p['rwkv_a_up'][l], p['rwkv_g_up'][l])
    for name in ('s5_lambda_re', 's5_lambda_im', 's5_b', 's5_c', 's5_d', 's5_log_step', 's5_w_glu', 'ret_gn'):
        lw[name] = p[name][l]
    lw['s5'], lw['s5_step'] = _s5_params(lw)
    lw['w_out'] = p['w_out'][l].astype(BF16)
    lw['norm_ffn'] = p['norm_ffn'][l].reshape(1, D_MODEL)
    lw['w_router'], lw['b_router'] = _router_weights(p['moe_w_grp'][l], p['moe_b_grp'][l], p['moe_w_exp'][l], p['moe_b_exp'][l])
    lw['moe_wg'] = p['moe_w_gate'][l].astype(BF16)
    lw['moe_wu'] = p['moe_w_up'][l].astype(BF16)
    lw['moe_wd'] = p['moe_w_down'][l].astype(BF16)
    return lw


def _prompt_layer(x, lw, tabs, b, t, norm_final):
    cos_a, sin_a, ret_tabs = tabs
    q, kv, gate, colsb, u, colsd = _proj(x, lw['norm_mix'], lw['w_all'], cos_a, sin_a, 512)
    o_a = _nsa_prompt_mixer(q, kv, gate, lw, b, t)
    o_b, s_rwkv, s_shift = _rwkv_mixer(colsb, jnp.zeros((b, SHIFT_PAD), F32), jnp.zeros((b, H_B, HEAD_DIM, HEAD_DIM), F32),
                                       lw, b, t, 512, b, 256)
    o_c, s_s5 = _s5_mixer(u, jnp.zeros((b, G_C, S5_P, 2), F32), lw, b, t, 256, False)
    o_d, s_ret = _ret_mixer(colsd, jnp.zeros((b, H_D, HEAD_DIM, HEAD_DIM), F32), lw, ret_tabs, b, t)
    x1, h, comb = _out_router(x, o_a, o_b, o_c, o_d, lw, 512)
    x2 = _moe(h, comb, x1, lw, norm_final, 1024, lw['layer'] == DEPTH - 1)
    kv3 = kv.reshape(b, t, 768)
    rows = kv3[:, :, :512].reshape(b, t, 4, N_KV_A, HEAD_DIM)
    win = kv3[:, t - min(WINDOW, t):, 512:].reshape(b, min(WINDOW, t), 2, N_KV_A, HEAD_DIM)
    return x2, (rows, win, s_rwkv, s_shift, s_s5, s_ret)


def _sample_layer(x, lw, tabs, b, pos, cache_kv, page_table, win_buf, s_rwkv, s_shift, s_s5, s_ret, norm_final):
    cos_a, sin_a, ret_cs = tabs
    q, kv, gate, colsb, u, colsd = _proj(x, lw['norm_mix'], lw['w_all'], cos_a, sin_a, b)
    o_a, win = _nsa_sample(q, kv, gate, cache_kv, win_buf, lw['layer'], page_table, lw['cmp'], int(pos[0]))
    rows = kv[:, :512].reshape(b, 1, 4, N_KV_A, HEAD_DIM)
    o_b, s_rwkv = _rwkv_step(colsb, _pad_to(s_shift, SHIFT_PAD, 1), s_rwkv, lw['layer'], lw)
    s_shift = colsb[:, :SHIFT_B]
    o_c, s_s5 = _s5_step(u, s_s5, lw['layer'], lw)
    o_d, s_ret = _ret_step(colsd, ret_cs[0], ret_cs[1], s_ret, lw['layer'], lw['ret_gn'])
    x1, h, comb = _out_router(x, o_a, o_b, o_c, o_d, lw, b)
    x2 = _moe(h, comb, x1, lw, norm_final, b, lw['layer'] == DEPTH - 1)
    return x2, (rows, win, s_rwkv, s_shift, s_s5, s_ret)


def kernel(x_prompt, x_sample, cache_nsa_kv, cache_nsa_win, state_rwkv, state_rwkv_shift, state_s5, state_ret, page_table, norm_mix, w_in, nsa_cmp_w1, nsa_cmp_b1, nsa_cmp_w2, rwkv_mu, rwkv_vec, rwkv_w_up, rwkv_a_up, rwkv_g_up, s5_lambda_re, s5_lambda_im, s5_b, s5_c, s5_d, s5_log_step, s5_w_glu, ret_gn, w_out, norm_ffn, moe_w_grp, moe_b_grp, moe_w_exp, moe_b_exp, moe_w_gate, moe_w_up, moe_w_down, norm_final):
    p = dict(norm_mix=norm_mix, w_in=w_in, nsa_cmp_w1=nsa_cmp_w1, nsa_cmp_b1=nsa_cmp_b1, nsa_cmp_w2=nsa_cmp_w2,
             rwkv_mu=rwkv_mu, rwkv_vec=rwkv_vec, rwkv_w_up=rwkv_w_up, rwkv_a_up=rwkv_a_up, rwkv_g_up=rwkv_g_up,
             s5_lambda_re=s5_lambda_re, s5_lambda_im=s5_lambda_im, s5_b=s5_b, s5_c=s5_c, s5_d=s5_d,
             s5_log_step=s5_log_step, s5_w_glu=s5_w_glu, ret_gn=ret_gn, w_out=w_out, norm_ffn=norm_ffn,
             moe_w_grp=moe_w_grp, moe_b_grp=moe_b_grp, moe_w_exp=moe_w_exp, moe_b_exp=moe_b_exp,
             moe_w_gate=moe_w_gate, moe_w_up=moe_w_up, moe_w_down=moe_w_down)
    bp, tp = x_prompt.shape[:2]
    bs, ts = x_sample.shape[:2]
    past_len = page_table.shape[1] * cache_nsa_kv.shape[2]
    pos_p = np.arange(tp)
    pos_s = past_len + np.arange(ts)
    c = RET_CHUNK if tp % RET_CHUNK == 0 else tp
    tabs_p = _rope_tables(pos_p, ROT_DIM, ROPE_THETA, HEAD_DIM, 2) + (_ret_tables(pos_p, c),)
    pos_rows = np.repeat(pos_s, bs)
    tabs_s = _rope_tables(pos_rows, ROT_DIM, ROPE_THETA, HEAD_DIM, 2) + (_rope_tables(pos_rows, HEAD_DIM, RET_THETA, HEAD_DIM, H_D),)
    xp = x_prompt.reshape(bp * tp, D_MODEL)
    xs = x_sample.reshape(bs * ts, D_MODEL)
    sts_p, sts_s = [], []
    for l in range(DEPTH):
        lw = _prep_layer(l, p)
        xp, st_p = _prompt_layer(xp, lw, tabs_p, bp, tp, norm_final)
        xs, st_s = _sample_layer(xs, lw, tabs_s, bs, pos_s, cache_nsa_kv, page_table, cache_nsa_win, state_rwkv,
                                     state_rwkv_shift[l], state_s5, state_ret, norm_final)
        rows, win, s1, s2, s3, s4 = st_s
        sts_s.append((rows, win, s1, s2, s3, s4))
        sts_p.append(st_p)
    new_p = [jnp.stack([st[i] for st in sts_p]) for i in range(6)]
    new_s = [jnp.stack([st[i] for st in sts_s]) for i in range(6)]
    return (xp.reshape(bp, tp, D_MODEL), xs.reshape(bs, ts, D_MODEL), new_p[0], new_s[0], new_p[1], new_s[1],
            new_p[2], new_s[2], new_p[3], new_s[3], new_p[4], new_s[4], new_p[5], new_s[5])
```

```python
import functools
import math

import numpy as np
import jax
import jax.numpy as jnp
from jax import lax
from jax.experimental import pallas as pl
from jax.experimental.pallas import tpu as pltpu

F32 = jnp.float32
BF16 = jnp.bfloat16
HI = lax.Precision.HIGHEST

D_MODEL = 1024
DEPTH = 2
HEAD_DIM = 64
C_A = C_B = C_C = C_D = 256
H_A = 4
N_KV_A = 2
R_A = 2
ROT_DIM = 16
ROPE_THETA = 500000.0
CMP_BLOCK = 32
CMP_STRIDE = 16
CMP_HIDDEN = 128
SEL_BLOCK = 64
TOP_K = 16
WINDOW = 512
NEG_INF = -1e30
FORCED_SCORE = 1e9
BLOCKED_SCORE = -1e9
H_B = 4
LORA_W = 16
LORA_A = 16
LORA_G = 32
SHIFT_B = 832
SHIFT_PAD = 896
RWKV_GN_EPS = 64e-5
S5_CH = 16
G_C = 16
S5_P = 64
H_D = 4
RET_CHUNK = 128
RET_THETA = 10000.0
RET_GN_EPS = 1e-5
N_GROUPS = 4
EXP_PER_GROUP = 4
N_EXPERTS = 16
D_EXPERT = 256
RMS_EPS = 1e-6
SPLIT_SIZES = (C_A, 6 * N_KV_A * HEAD_DIM, 3 * H_A, SHIFT_B, C_C, 4 * C_D)
LANE = 128
VMEM_LIMIT = 56 * 1024 * 1024


def _cp(*sem):
    return pltpu.CompilerParams(dimension_semantics=sem, vmem_limit_bytes=VMEM_LIMIT)


def _offsets(sizes):
    return [int(s) for s in np.cumsum(sizes)[:-1]]


def _pad_to(a, n, axis):
    pad = [(0, 0)] * a.ndim
    pad[axis] = (0, n - a.shape[axis])
    return jnp.pad(a, pad)


def _block_ones(n, blk, dtype):
    r = lax.broadcasted_iota(jnp.int32, (n, n), 0) // blk
    c = lax.broadcasted_iota(jnp.int32, (n, n), 1) // blk
    return (r == c).astype(dtype)


def _dot2(x, ones_bf16):
    hi = x.astype(BF16)
    lo = (x - hi.astype(F32)).astype(BF16)
    return (jnp.dot(hi, ones_bf16, preferred_element_type=F32)
            + jnp.dot(lo, ones_bf16, preferred_element_type=F32))


def _rope_tables(pos, rot_dim, theta, period, reps):
    half = rot_dim // 2
    inv = theta ** (-jnp.arange(half, dtype=F32) / half)
    ang = jnp.asarray(pos, F32)[:, None] * inv[None, :]
    cos, sin = jnp.cos(ang), jnp.sin(ang)
    n = ang.shape[0]
    rest = period - rot_dim
    c = jnp.concatenate([cos, cos, jnp.ones((n, rest), F32)], -1)
    s = jnp.concatenate([-sin, sin, jnp.zeros((n, rest), F32)], -1)
    return jnp.tile(c, (1, reps)), jnp.tile(s, (1, reps))


PRECISE_COLS = 384


def _proj_body(x_ref, nw_ref, w_ref, wlo_ref, cos_ref, sin_ref, q_ref, kv_ref, g_ref, cb_ref, u_ref, cd_ref):
    x = x_ref[...]
    h = x * lax.rsqrt(jnp.mean(x * x, axis=-1, keepdims=True) + RMS_EPS) * nw_ref[...]
    hb = h.astype(BF16)
    h_lo = (h - hb.astype(F32)).astype(BF16)
    c = cos_ref[...]
    s = sin_ref[...]
    first = (lax.broadcasted_iota(jnp.int32, c.shape, 1) % HEAD_DIM) < (ROT_DIM // 2)

    def rope(z):
        sw = jnp.where(first, pltpu.roll(z, LANE - ROT_DIM // 2, 1), pltpu.roll(z, ROT_DIM // 2, 1))
        return z * c + sw * s

    def dot(a, b):
        z = jnp.dot(hb, w_ref[:, a:b], preferred_element_type=F32)
        if b <= PRECISE_COLS:
            z = z + (jnp.dot(hb, wlo_ref[:, a:b], preferred_element_type=F32)
                     + jnp.dot(h_lo, w_ref[:, a:b], preferred_element_type=F32))
        return z

    for j in range(2):
        q_ref[:, LANE * j:LANE * (j + 1)] = rope(dot(LANE * j, LANE * (j + 1)))
    for j in range(6):
        z = dot(256 + LANE * j, 256 + LANE * (j + 1))
        kv_ref[:, LANE * j:LANE * (j + 1)] = rope(z) if j % 2 == 0 else z
    g_ref[...] = dot(1024, 1152)
    cb_ref[...] = dot(1152, 2048)
    u_ref[...] = dot(2048, 2304)
    cd_ref[...] = dot(2304, 3328)


def _proj(x2d, norm_w, w_all, w_lo, cos_t, sin_t, tm):
    n = x2d.shape[0]
    t_tiles = cos_t.shape[0] // tm
    row = lambda i: (i, 0)
    fixed = lambda i: (0, 0)
    tab = lambda i: (i % t_tiles, 0)
    widths = (256, 768, 128, SHIFT_PAD, 256, 1024)
    return pl.pallas_call(
        _proj_body,
        grid=(n // tm,),
        in_specs=[pl.BlockSpec((tm, D_MODEL), row), pl.BlockSpec((1, D_MODEL), fixed),
                  pl.BlockSpec((D_MODEL, 3328), fixed), pl.BlockSpec((D_MODEL, PRECISE_COLS), fixed),
                  pl.BlockSpec((tm, LANE), tab), pl.BlockSpec((tm, LANE), tab)],
        out_specs=[pl.BlockSpec((tm, w), row) for w in widths],
        out_shape=[jax.ShapeDtypeStruct((n, w), F32) for w in widths],
        compiler_params=_cp("parallel"),
        name="proj",
    )(x2d, norm_w.reshape(1, D_MODEL), w_all, w_lo, cos_t, sin_t)


def _cmp_mlp(xc, kind, w1_ref, b1_ref, w2_ref, w1k_ref, w2k_ref, nck):
    if kind == 0:
        hh = _dot3(xc, w1k_ref[...])
    else:
        hh = jnp.dot(xc.astype(BF16), w1_ref[kind], preferred_element_type=F32)
    hs = []
    for g in range(N_KV_A):
        hg = hh[g * nck:(g + 1) * nck]
        hs.append(jax.nn.gelu(b1_ref[kind] + hg[:, :CMP_HIDDEN] + pltpu.roll(hg[:, CMP_HIDDEN:], nck - 1, 0)))
    act = jnp.concatenate(hs, axis=1)
    if kind == 0:
        return _dot3(act, w2k_ref[...])
    return jnp.dot(act.astype(BF16), w2_ref[kind], preferred_element_type=F32)


def _cmp_body(xk_ref, xv_ref, w1_ref, b1_ref, w2_ref, w1k_ref, w2k_ref, kc_ref, vc_ref, vct_ref, xc_ref, *, n_chunks):
    lane = lax.broadcasted_iota(jnp.int32, (n_chunks, LANE), 1)
    lo = lane < HEAD_DIM
    for pair in range(CMP_STRIDE // 2):
        for kind, x_ref in enumerate((xk_ref, xv_ref)):
            ak = x_ref[0, pl.ds(2 * pair, n_chunks, stride=CMP_STRIDE), :]
            bk = x_ref[0, pl.ds(2 * pair + 1, n_chunks, stride=CMP_STRIDE), :]
            xc_ref[kind, 0:n_chunks, LANE * pair:LANE * (pair + 1)] = jnp.where(lo, ak, pltpu.roll(bk, HEAD_DIM, 1))
            xc_ref[kind, n_chunks:2 * n_chunks, LANE * pair:LANE * (pair + 1)] = jnp.where(lo, pltpu.roll(ak, HEAD_DIM, 1), bk)
    outs = [_cmp_mlp(xc_ref[kind], kind, w1_ref, b1_ref, w2_ref, w1k_ref, w2k_ref, n_chunks) for kind in range(2)]
    kc_ref[0] = outs[0]
    vc_ref[0] = outs[1]
    vct_ref[0] = outs[1].T


def _nsa_compress(rows, w1cat, b1, w2bd, w1k, w2k):
    b, tk = rows.shape[0], rows.shape[1]
    n_chunks = tk // CMP_STRIDE
    fixed3 = lambda i: (0, 0, 0)
    fixed2 = lambda i: (0, 0)
    return pl.pallas_call(
        functools.partial(_cmp_body, n_chunks=n_chunks),
        grid=(b,),
        in_specs=[pl.BlockSpec((1, tk, LANE), lambda i: (i, 0, 0)), pl.BlockSpec((1, tk, LANE), lambda i: (i, 0, 1)),
                  pl.BlockSpec((2, CMP_STRIDE * HEAD_DIM, 2 * CMP_HIDDEN), fixed3),
                  pl.BlockSpec((2, 1, CMP_HIDDEN), fixed3),
                  pl.BlockSpec((2, 2 * CMP_HIDDEN, LANE), fixed3),
                  pl.BlockSpec((CMP_STRIDE * HEAD_DIM, 2 * CMP_HIDDEN), fixed2), pl.BlockSpec((2 * CMP_HIDDEN, LANE), fixed2)],
        out_specs=[pl.BlockSpec((1, n_chunks, LANE), lambda i: (i, 0, 0)),
                   pl.BlockSpec((1, n_chunks, LANE), lambda i: (i, 0, 0)),
                   pl.BlockSpec((1, LANE, n_chunks), lambda i: (i, 0, 0))],
        out_shape=[jax.ShapeDtypeStruct((b, n_chunks, LANE), F32), jax.ShapeDtypeStruct((b, n_chunks, LANE), F32),
                   jax.ShapeDtypeStruct((b, LANE, n_chunks), F32)],
        scratch_shapes=[pltpu.VMEM((2, N_KV_A * n_chunks, CMP_STRIDE * HEAD_DIM), F32)],
        compiler_params=_cp("parallel"),
        name="nsa_compress",
    )(rows, rows, w1cat, b1, w2bd, w1k, w2k)


def _cmp_weights(cmp_w1, cmp_b1, cmp_w2):
    m = CMP_BLOCK // CMP_STRIDE
    w1r = cmp_w1.reshape(2, m, CMP_STRIDE * HEAD_DIM, CMP_HIDDEN)
    w1cat = jnp.concatenate([w1r[:, j] for j in range(m)], axis=-1)
    z = jnp.zeros_like(cmp_w2)
    w2bd = jnp.concatenate([jnp.concatenate([cmp_w2, z], -1), jnp.concatenate([z, cmp_w2], -1)], axis=1)
    return w1cat.astype(BF16), cmp_b1.reshape(2, 1, CMP_HIDDEN), w2bd.astype(BF16), w1cat[0], w2bd[0]


def _cmp_to_sel_t(n_chunks, n_cmp, n_sel):
    starts = np.arange(n_chunks) * CMP_STRIDE
    sel_s = np.arange(n_sel) * SEL_BLOCK
    ov = np.minimum(starts[:, None] + CMP_BLOCK, sel_s[None] + SEL_BLOCK) - np.maximum(starts[:, None], sel_s[None])
    ov = np.clip(ov, 0, None) / CMP_BLOCK
    ov[n_cmp:] = 0.0
    return jnp.asarray(ov.T, dtype=F32)


def _masked_softmax_cols(s, mask):
    m = jnp.max(jnp.where(mask, s, NEG_INF), axis=0, keepdims=True)
    e = jnp.where(mask, jnp.exp(s - m), 0.0)
    den = jnp.sum(e, axis=0, keepdims=True)
    return e * jnp.where(den > 0.0, 1.0 / den, 0.0)


def _nsa_prompt_body(qt_ref, gt_ref, kc_ref, vct_ref, ovt_ref, ks_ref, vst_ref, kw_ref, vwt_ref, o_ref, sel_ref,
                     *, n_cmp, n_sel, qb_size):
    qb = pl.program_id(1)
    tq = qb_size
    n_chunks = kc_ref.shape[1]
    qpos = qb * tq + lax.broadcasted_iota(jnp.int32, (1, tq), 1)
    qpos2 = jnp.concatenate([qpos, qpos], axis=1)
    zeros_q = jnp.zeros((HEAD_DIM, 2 * tq), F32)
    gates = jax.nn.sigmoid(gt_ref[0])
    kc = kc_ref[0]
    n_idx = lax.broadcasted_iota(jnp.int32, (n_chunks, 2 * tq), 0)
    cmp_mask = (n_idx * CMP_STRIDE + (CMP_BLOCK - 1) <= qpos2) & (n_idx < n_cmp)
    blk = lax.broadcasted_iota(jnp.int32, (n_sel, tq), 0)
    cur = qpos // SEL_BLOCK
    forced = (blk == 0) | (blk == cur) | (blk == cur - 1)
    causal_blk = blk * SEL_BLOCK <= qpos
    tk = 2 * tq
    kiota = lax.broadcasted_iota(jnp.int32, (tk, 2 * tq), 0)
    qpads, o_cmps = [], []

    for g in range(N_KV_A):
        q64 = jnp.concatenate([qt_ref[0, (2 * g) * HEAD_DIM:(2 * g + 1) * HEAD_DIM, :],
                               qt_ref[0, (2 * g + 1) * HEAD_DIM:(2 * g + 2) * HEAD_DIM, :]], axis=1) * (HEAD_DIM ** -0.5)
        qpad32 = jnp.concatenate([q64, zeros_q], axis=0) if g == 0 else jnp.concatenate([zeros_q, q64], axis=0)
        qpad = qpad32.astype(BF16)
        qpads.append(qpad)

        p = _masked_softmax_cols(_dot3(kc, qpad32), cmp_mask)
        o_cmps.append(jnp.dot(vct_ref[0, g * HEAD_DIM:(g + 1) * HEAD_DIM, :].astype(BF16), p.astype(BF16),
                              preferred_element_type=F32))
        psum = p[:, :tq] + p[:, tq:]
        imp = jnp.dot(ovt_ref[...], psum, precision=HI, preferred_element_type=F32)
        imp = jnp.where(forced, FORCED_SCORE, jnp.where(causal_blk, imp, BLOCKED_SCORE))
        rank = jnp.zeros((n_sel, tq), F32)
        for i in range(n_sel):
            row = imp[i:i + 1, :]
            rank = rank + jnp.where((row > imp) | ((row == imp) & (blk > i)), 1.0, 0.0)
        sel_ref[g] = jnp.where(rank < float(min(TOP_K, n_sel)), 1.0, 0.0)

    def attend(j, carry, k_ref, vt_ref, use_sel):
        off = pl.multiple_of(j * tk, tk)
        kt = k_ref[0, pl.ds(off, tk), :].astype(BF16)
        diff = qpos2 - (off + kiota)
        base = (diff >= 0) if use_sel else ((diff >= 0) & (diff < WINDOW))
        out = []
        for g in range(N_KV_A):
            m, l, acc = carry[3 * g:3 * g + 3]
            s = jnp.dot(kt, qpads[g], preferred_element_type=F32)
            if use_sel:
                per_tile = tk // SEL_BLOCK
                rows = [jnp.broadcast_to(sel_ref[g, pl.ds(j * per_tile + a, 1), :], (SEL_BLOCK, tq)) for a in range(per_tile)]
                selm = jnp.concatenate(rows, axis=0)
                mask = (jnp.concatenate([selm, selm], axis=1) > 0.0) & base
            else:
                mask = base
            m_new = jnp.maximum(m, jnp.max(jnp.where(mask, s, NEG_INF), axis=0, keepdims=True))
            alpha = jnp.exp(m - m_new)
            e = jnp.where(mask, jnp.exp(s - m_new), 0.0)
            l_new = alpha * l + jnp.sum(e, axis=0, keepdims=True)
            vt = vt_ref[0, g * HEAD_DIM:(g + 1) * HEAD_DIM, pl.ds(off, tk)].astype(BF16)
            acc_new = alpha * acc + jnp.dot(vt, e.astype(BF16), preferred_element_type=F32)
            out += [m_new, l_new, acc_new]
        return tuple(out)

    init = (jnp.full((1, 2 * tq), NEG_INF, F32), jnp.zeros((1, 2 * tq), F32), jnp.zeros((HEAD_DIM, 2 * tq), F32)) * N_KV_A
    per = tk // tq
    hi = (qb + per) // per
    res_s = lax.fori_loop(0, hi, functools.partial(attend, k_ref=ks_ref, vt_ref=vst_ref, use_sel=True), init)
    lo_w = jnp.maximum(qb - WINDOW // tq, 0) // per
    res_w = lax.fori_loop(lo_w, hi, functools.partial(attend, k_ref=kw_ref, vt_ref=vwt_ref, use_sel=False), init)

    for g in range(N_KV_A):
        o_slc = res_s[3 * g + 2] / res_s[3 * g + 1]
        o_win = res_w[3 * g + 2] / res_w[3 * g + 1]
        for r in range(R_A):
            h = 2 * g + r
            gr = gates[3 * h:3 * h + 3, :]
            sl = slice(r * tq, (r + 1) * tq)
            o_ref[0, h * HEAD_DIM:(h + 1) * HEAD_DIM, :] = (gr[0:1] * o_cmps[g][:, sl] + gr[1:2] * o_slc[:, sl]
                                                          + gr[2:3] * o_win[:, sl])


def _nsa_prompt(qt, gt, kc, vct, ovt, kv, vst, vwt, n_cmp):
    b, _, t = qt.shape
    tq = 128
    n_sel = t // SEL_BLOCK
    n_chunks = kc.shape[1]
    per_b = lambda i, j: (i, 0, 0)
    return pl.pallas_call(
        functools.partial(_nsa_prompt_body, n_cmp=n_cmp, n_sel=n_sel, qb_size=tq),
        grid=(b, t // tq),
        in_specs=[pl.BlockSpec((1, 256, tq), lambda i, j: (i, 0, j)),
                  pl.BlockSpec((1, 16, tq), lambda i, j: (i, 0, j)),
                  pl.BlockSpec((1, n_chunks, LANE), per_b),
                  pl.BlockSpec((1, LANE, n_chunks), per_b),
                  pl.BlockSpec((n_sel, n_chunks), lambda i, j: (0, 0)),
                  pl.BlockSpec((1, t, LANE), lambda i, j: (i, 0, 2)),
                  pl.BlockSpec((1, LANE, t), per_b),
                  pl.BlockSpec((1, t, LANE), lambda i, j: (i, 0, 4)),
                  pl.BlockSpec((1, LANE, t), per_b)],
        out_specs=pl.BlockSpec((1, 256, tq), lambda i, j: (i, 0, j)),
        out_shape=jax.ShapeDtypeStruct((b, 256, t), F32),
        scratch_shapes=[pltpu.VMEM((N_KV_A, n_sel, tq), F32)],
        compiler_params=_cp("parallel", "arbitrary"),
        name="nsa_prompt",
    )(qt, gt, kc, vct, ovt, kv, vst, kv, vwt)


def _nsa_prompt_mixer(q, kv, gate, lw, b, t):
    kv3 = kv.reshape(b, t, 768)
    n_chunks = t // CMP_STRIDE
    n_cmp = (t - CMP_BLOCK) // CMP_STRIDE + 1
    kc, _, vct = _nsa_compress(kv3, *lw['cmp'])
    ovt = _cmp_to_sel_t(n_chunks, n_cmp, t // SEL_BLOCK)
    qt = jnp.swapaxes(q.reshape(b, t, 256), 1, 2)
    gt = jnp.swapaxes(gate.reshape(b, t, LANE)[:, :, :16], 1, 2)
    vst = jnp.swapaxes(kv3[:, :, 384:512], 1, 2)
    vwt = jnp.swapaxes(kv3[:, :, 640:768], 1, 2)
    ot = _nsa_prompt(qt, gt, kc, vct, ovt, kv3, vst, vwt, n_cmp)
    return jnp.swapaxes(ot, 1, 2).reshape(b * t, 256)


def _softmax_rows_with_extra(s, mask, s_new):
    m = jnp.maximum(jnp.max(jnp.where(mask, s, NEG_INF), axis=-1, keepdims=True), s_new)
    e = jnp.where(mask, jnp.exp(s - m), 0.0)
    e_new = jnp.exp(s_new - m)
    return e, e_new, 1.0 / (jnp.sum(e, axis=-1, keepdims=True) + e_new)


def _nsa_sample_body(pt_ref, *refs, n_pages, page, pos, n_sel, n_cmp, wb):
    del pt_ref
    n_in = 4 * n_pages
    pages = refs[:n_in]
    (qbd_ref, new_ref, gate_ref, win_ref, ov_ref, w1_ref, b1_ref, w2_ref, w1k_ref, w2k_ref,
     o_ref, nw_ref, tok_ref, xc_ref) = refs[n_in:]
    pg = lambda p, kind: pages[4 * p + kind]
    nck = n_pages * page // CMP_STRIDE
    lane8 = lax.broadcasted_iota(jnp.int32, (8, LANE), 1)
    lo8 = lane8 < HEAD_DIM
    nt = lambda a, b: lax.dot_general(a, b, (((1,), (1,)), ((), ())), preferred_element_type=F32)

    lo_c = lax.broadcasted_iota(jnp.int32, (nck, LANE), 1) < HEAD_DIM
    for kind in range(2):
        for p in range(n_pages):
            tok_ref[kind, p * page:(p + 1) * page, :] = pg(p, kind)[0].T
        for pair in range(CMP_STRIDE // 2):
            a = tok_ref[kind, pl.ds(2 * pair, nck, stride=CMP_STRIDE), :]
            b = tok_ref[kind, pl.ds(2 * pair + 1, nck, stride=CMP_STRIDE), :]
            cols = slice(LANE * pair, LANE * (pair + 1))
            xc_ref[kind, 0:nck, cols] = jnp.where(lo_c, a, pltpu.roll(b, HEAD_DIM, 1))
            xc_ref[kind, nck:2 * nck, cols] = jnp.where(lo_c, pltpu.roll(a, HEAD_DIM, 1), b)
    kc, vc = [_cmp_mlp(xc_ref[kind], kind, w1_ref, b1_ref, w2_ref, w1k_ref, w2k_ref, nck) for kind in range(2)]

    q = qbd_ref[0] * (HEAD_DIM ** -0.5)
    qb = q.astype(BF16)
    new = new_ref[0]

    n_idx = lax.broadcasted_iota(jnp.int32, (8, nck), 1)
    cmask = (n_idx * CMP_STRIDE + (CMP_BLOCK - 1) <= pos) & (n_idx < n_cmp)
    s = _dot3(q, kc, _NT)
    m = jnp.max(jnp.where(cmask, s, NEG_INF), axis=-1, keepdims=True)
    e = jnp.where(cmask, jnp.exp(s - m), 0.0)
    den = jnp.sum(e, axis=-1, keepdims=True)
    p_cmp = e * jnp.where(den > 0.0, 1.0 / den, 0.0)
    o_cmp = jnp.dot(p_cmp.astype(BF16), vc.astype(BF16), preferred_element_type=F32)

    row8 = lax.broadcasted_iota(jnp.int32, (8, nck), 0)
    psum = jnp.where(row8 == 0, p_cmp[0:1] + p_cmp[1:2], jnp.where(row8 == 1, p_cmp[2:3] + p_cmp[3:4], 0.0))
    imp = jnp.dot(psum, ov_ref[...], precision=HI, preferred_element_type=F32)
    cur = pos // SEL_BLOCK
    forced = (lane8 == 0) | (lane8 == cur) | (lane8 == cur - 1)
    imp = jnp.where(forced, FORCED_SCORE, jnp.where(lane8 * SEL_BLOCK <= pos, imp, BLOCKED_SCORE))
    imp = jnp.where(lane8 < n_sel, imp, -3e38)
    rank = jnp.zeros((8, LANE), F32)
    for i in range(n_sel):
        col = imp[:, i:i + 1]
        rank = rank + jnp.where((col > imp) | ((col == imp) & (lane8 > i)), 1.0, 0.0)
    sel = jnp.where((rank < float(min(TOP_K, n_sel))) & (lane8 < n_sel), 1.0, 0.0)
    rsel = lax.broadcasted_iota(jnp.int32, (8, LANE), 0)
    selh = jnp.where(rsel < R_A, sel[0:1], jnp.where(rsel < 2 * R_A, sel[1:2], 0.0))

    per_page = page // SEL_BLOCK
    s_t, m_t = [], []
    for p in range(n_pages):
        s_t.append(jnp.dot(qb, pg(p, 2)[0].astype(BF16), preferred_element_type=F32))
        blk_sel = selh[:, per_page * p:per_page * p + 1]
        for a in range(1, per_page):
            blk_sel = jnp.where(lane8 < a * SEL_BLOCK, blk_sel, selh[:, per_page * p + a:per_page * p + a + 1])
        kpos = p * page + lane8
        m_t.append((blk_sel > 0.0) & (kpos <= pos))
    s_all = jnp.concatenate(s_t, axis=1)
    mk_all = jnp.concatenate(m_t, axis=1)
    s_new = jnp.sum(q * new[2:3], axis=-1, keepdims=True)
    e, e_new, inv = _softmax_rows_with_extra(s_all, mk_all, s_new)
    acc = e_new * new[3:4]
    for p in range(n_pages):
        acc = acc + nt(e[:, p * page:(p + 1) * page].astype(BF16), pg(p, 3)[0].astype(BF16))
    o_slc = acc * inv

    kw, vw = win_ref[0], win_ref[1]
    widx = lax.broadcasted_iota(jnp.int32, (8, wb), 1)
    diff = wb - widx
    s_w = jnp.dot(qb, kw.astype(BF16), preferred_element_type=F32)
    s_wnew = jnp.sum(q * new[4:5], axis=-1, keepdims=True)
    e, e_new, inv = _softmax_rows_with_extra(s_w, (diff >= 0) & (diff < WINDOW), s_wnew)
    o_win = (nt(e.astype(BF16), vw.astype(BF16)) + e_new * new[5:6]) * inv

    gts = jax.nn.sigmoid(gate_ref[0])
    o = gts[:, 0:1] * o_cmp + gts[:, 1:2] * o_slc + gts[:, 2:3] * o_win
    lo1 = lo8[0:1]
    o_ref[0] = jnp.concatenate([jnp.where(lo1, o[0:1], pltpu.roll(o[1:2], HEAD_DIM, 1)),
                                jnp.where(lo1, pltpu.roll(o[2:3], HEAD_DIM, 1), o[3:4])], axis=1)
    last = lax.broadcasted_iota(jnp.int32, (LANE, wb), 1) == wb - 1
    new_t = new.T
    nw_ref[0] = jnp.where(last, new_t[:, 4:5], pltpu.roll(kw, wb - 1, 1))
    nw_ref[1] = jnp.where(last, new_t[:, 5:6], pltpu.roll(vw, wb - 1, 1))


def _nsa_sample(q, kv, gate, cache_kv, cache_win, layer, page_table, cmp_w, pos):
    b = q.shape[0]
    n_pool, page = cache_kv.shape[1:3]
    n_pages = page_table.shape[1]
    wb = cache_win.shape[2]
    assert wb == WINDOW and page % CMP_STRIDE == 0 and page % SEL_BLOCK == 0 and page == LANE
    tk = n_pages * page + 1
    n_cmp = (tk - CMP_BLOCK) // CMP_STRIDE + 1
    nck = n_pages * page // CMP_STRIDE
    assert n_cmp <= nck
    n_sel = -(-tk // SEL_BLOCK)
    assert n_sel <= LANE and pos // SEL_BLOCK == n_sel - 1
    ov = _pad_to(_cmp_to_sel_t(nck, n_cmp, n_sel).T, LANE, 1)
    w1cat, b1, w2bd, w1k, w2k = cmp_w
    q4 = q.reshape(b, H_A, HEAD_DIM)
    z = jnp.zeros_like(q4)
    first = (jnp.arange(H_A) // R_A == 0)[None, :, None]
    qbd = jnp.concatenate([jnp.where(first, q4, z), jnp.where(first, z, q4)], axis=-1)
    qbd = _pad_to(qbd, 8, 1)
    new = _pad_to(kv.reshape(b, 6, LANE), 8, 1)
    g8 = _pad_to(_pad_to(gate[:, :3 * H_A].reshape(b, H_A, 3), LANE, 2), 8, 1)
    cache3 = jnp.transpose(cache_kv, (0, 1, 3, 4, 5, 2)).reshape(-1, LANE, page)
    win3 = jnp.transpose(cache_win, (0, 1, 3, 4, 5, 2)).reshape(-1, LANE, wb)
    page_specs = [pl.BlockSpec((1, LANE, page),
                               functools.partial(lambda i, pt, p, kind: ((layer * n_pool + pt[i, p]) * 4 + kind, 0, 0), p=p, kind=kind))
                  for p in range(n_pages) for kind in range(4)]
    per_b = lambda i, pt: (i, 0, 0)
    fixed2 = lambda i, pt: (0, 0)
    fixed3 = lambda i, pt: (0, 0, 0)
    grid_spec = pltpu.PrefetchScalarGridSpec(
        num_scalar_prefetch=1,
        grid=(b,),
        in_specs=page_specs + [pl.BlockSpec((1, 8, LANE), per_b), pl.BlockSpec((1, 8, LANE), per_b), pl.BlockSpec((1, 8, LANE), per_b),
                               pl.BlockSpec((2, LANE, wb), lambda i, pt: (layer * b + i, 0, 0)), pl.BlockSpec((nck, LANE), fixed2),
                               pl.BlockSpec((2, CMP_STRIDE * HEAD_DIM, 2 * CMP_HIDDEN), fixed3),
                               pl.BlockSpec((2, 1, CMP_HIDDEN), fixed3), pl.BlockSpec((2, 2 * CMP_HIDDEN, LANE), fixed3),
                               pl.BlockSpec((CMP_STRIDE * HEAD_DIM, 2 * CMP_HIDDEN), fixed2),
                               pl.BlockSpec((2 * CMP_HIDDEN, LANE), fixed2)],
        out_specs=[pl.BlockSpec((1, 1, 256), per_b), pl.BlockSpec((2, LANE, wb), per_b)],
        scratch_shapes=[pltpu.VMEM((2, n_pages * page, LANE), F32), pltpu.VMEM((2, 2 * nck, CMP_STRIDE * HEAD_DIM), F32)],
    )
    o, nw = pl.pallas_call(
        functools.partial(_nsa_sample_body, n_pages=n_pages, page=page, pos=pos, n_sel=n_sel, n_cmp=n_cmp, wb=wb),
        grid_spec=grid_spec,
        out_shape=[jax.ShapeDtypeStruct((b, 1, 256), F32), jax.ShapeDtypeStruct((2 * b, LANE, wb), F32)],
        compiler_params=_cp("parallel"),
        name="nsa_sample",
    )(page_table, *([cache3] * (4 * n_pages)), qbd, new, g8, win3, ov, w1cat, b1, w2bd, w1k, w2k)
    return o.reshape(b, 256), jnp.transpose(nw.reshape(b, 2, N_KV_A, HEAD_DIM, wb), (0, 4, 1, 2, 3))


def _rwkv_prep_body(c_ref, s0_ref, mu_ref, vec_ref, wup_ref, aup_ref, gup_ref,
                    r_ref, lw_ref, k_ref, v_ref, kk_ref, ka_ref, g_ref, bonus_ref, carry_ref, *, tiles_per_seq):
    i = pl.program_id(0)
    cols = c_ref[...]
    tm = cols.shape[0]

    @pl.when(i % tiles_per_seq == 0)
    def _():
        carry_ref[...] = s0_ref[0]

    prev = pltpu.roll(cols, 1, 0)
    row0 = lax.broadcasted_iota(jnp.int32, cols.shape, 0) == 0
    prev = jnp.where(row0, carry_ref[...], prev)
    carry_ref[...] = cols[tm - 1:tm, :]
    _rwkv_features(cols, prev, mu_ref, vec_ref, wup_ref, aup_ref, gup_ref,
                   r_ref, lw_ref, k_ref, v_ref, kk_ref, ka_ref, g_ref, bonus_ref)


def _rwkv_step_prep_body(c_ref, s0_ref, mu_ref, vec_ref, wup_ref, aup_ref, gup_ref,
                         r_ref, lw_ref, k_ref, v_ref, kk_ref, ka_ref, g_ref, bonus_ref):
    _rwkv_features(c_ref[...], s0_ref[...], mu_ref, vec_ref, wup_ref, aup_ref, gup_ref,
                   r_ref, lw_ref, k_ref, v_ref, kk_ref, ka_ref, g_ref, bonus_ref)


def _rwkv_features(cols, prev, mu_ref, vec_ref, wup_ref, aup_ref, gup_ref,
                   r_ref, lw_ref, k_ref, v_ref, kk_ref, ka_ref, g_ref, bonus_ref):
    xs = cols + mu_ref[...] * (prev - cols)
    r, k, v, lo = xs[:, 0:256], xs[:, 256:512], xs[:, 512:768], xs[:, 768:896]
    vec = vec_ref[...]
    w0, a0, k_k, k_a, r_k = vec[0:1], vec[1:2], vec[2:3], vec[3:4], vec[4:5]
    dot_hi = lambda x, w: jnp.dot(x, w, precision=HI, preferred_element_type=F32)
    w_log = -jax.nn.softplus(-(w0 + dot_hi(jnp.tanh(lo), wup_ref[...]))) - 0.5
    a = jax.nn.sigmoid(a0 + dot_hi(lo, aup_ref[...]))
    g_ref[...] = dot_hi(jax.nn.sigmoid(lo), gup_ref[...])
    ones = _block_ones(256, HEAD_DIM, F32)
    kk = k * k_k
    kk = kk * lax.rsqrt(dot_hi(kk * kk, ones) + 1e-12)
    k2 = k * (1.0 + (a - 1.0) * k_a)
    r_ref[...] = r
    lw_ref[...] = -jnp.exp(w_log)
    k_ref[...] = k2
    v_ref[...] = v
    kk_ref[...] = kk
    ka_ref[...] = kk * a
    bonus_ref[...] = dot_hi(r * k2 * r_k, ones) * v


def _rwkv_prep(colsb, shift0, lw, t, tm):
    n = colsb.shape[0]
    tiles_per_seq = t // tm
    row = lambda i: (i, 0)
    fixed = lambda i: (0, 0)
    outs = [jax.ShapeDtypeStruct((n, 256), F32)] * 8
    return pl.pallas_call(
        functools.partial(_rwkv_prep_body, tiles_per_seq=tiles_per_seq),
        grid=(n // tm,),
        in_specs=[pl.BlockSpec((tm, SHIFT_PAD), row),
                  pl.BlockSpec((1, 1, SHIFT_PAD), lambda i: (i // tiles_per_seq, 0, 0)),
                  pl.BlockSpec((1, SHIFT_PAD), fixed), pl.BlockSpec((8, 256), fixed),
                  pl.BlockSpec((LANE, 256), fixed), pl.BlockSpec((LANE, 256), fixed), pl.BlockSpec((LANE, 256), fixed)],
        out_specs=[pl.BlockSpec((tm, 256), row)] * 8,
        out_shape=outs,
        scratch_shapes=[pltpu.VMEM((1, SHIFT_PAD), F32)],
        compiler_params=_cp("arbitrary"),
        name="rwkv_prep",
    )(colsb, shift0.reshape(-1, 1, SHIFT_PAD), lw['rwkv_mu'], lw['rwkv_vec'], lw['rwkv_wup'], lw['rwkv_aup'], lw['rwkv_gup'])


def _rwkv_scan_body(r_ref, lw_ref, k_ref, v_ref, kk_ref, ka_ref, g_ref, bonus_ref, s0_ref, vec_ref,
                    o_ref, st_ref, s_scr, y_scr, *, nb, tl):
    @pl.when(pl.program_id(1) == 0)
    def _():
        s_scr[...] = s0_ref[...]

    ones = _block_ones(256, HEAD_DIM, BF16)
    isub = lax.broadcasted_iota(jnp.int32, (HEAD_DIM, 256), 0)
    ilane = lax.broadcasted_iota(jnp.int32, (HEAD_DIM, 256), 1) % HEAD_DIM
    msel = (isub == ilane).astype(F32)

    nr = nb * HEAD_DIM

    def step(t, carry):
        tp = jnp.maximum(t - 1, 0)
        states = [s_scr[b] for b in range(nb)]
        t1 = jnp.concatenate([states[b] * kk_ref[b, pl.ds(t, 1), :] for b in range(nb)], axis=0)
        t1h = t1.astype(BF16)
        t1l = (t1 - t1h.astype(F32)).astype(BF16)
        t2h, t2l = [], []
        for b in range(nb):
            v = v_ref[b, pl.ds(t, 1), :]
            vh = v.astype(BF16).astype(F32)
            t2h.append((msel * vh).astype(BF16))
            t2l.append((msel * (v - vh)).astype(BF16))
        t3 = [(states[b] * r_ref[b, pl.ds(tp, 1), :]).astype(BF16) for b in range(nb)]
        res_h = jnp.dot(jnp.concatenate([t1h] + t2h + t3, axis=0), ones, preferred_element_type=F32)
        res_l = jnp.dot(jnp.concatenate([t1l] + t2l, axis=0), ones, preferred_element_type=F32)
        for b in range(nb):
            rows = slice(b * HEAD_DIM, (b + 1) * HEAD_DIM)
            sk = res_h[rows] + res_l[rows]
            vrep = res_h[nr:2 * nr][rows] + res_l[nr:2 * nr][rows]
            s_scr[b] = (states[b] * jnp.exp(lw_ref[b, pl.ds(t, 1), :]) - sk * ka_ref[b, pl.ds(t, 1), :]
                        + vrep * k_ref[b, pl.ds(t, 1), :])
            y_scr[b, pl.ds(tp, 1), :] = jnp.sum(res_h[2 * nr:][rows] * msel, axis=0, keepdims=True)
        return carry

    lax.fori_loop(0, tl, step, 0)
    for b in range(nb):
        yrep = jnp.dot((s_scr[b] * r_ref[b, tl - 1:tl, :]).astype(BF16), ones, preferred_element_type=F32)
        y_scr[b, tl - 1:tl, :] = jnp.sum(yrep * msel, axis=0, keepdims=True)
    st_ref[...] = s_scr[...]
    vec = vec_ref[...]
    lnx_w, lnx_b = vec[5:6], vec[6:7]
    avg = _block_ones(256, HEAD_DIM, F32) * (1.0 / HEAD_DIM)
    for b in range(nb):
        y = y_scr[b]
        yc = y - jnp.dot(y, avg, precision=HI, preferred_element_type=F32)
        yn = yc * lax.rsqrt(jnp.dot(yc * yc, avg, precision=HI, preferred_element_type=F32) + RWKV_GN_EPS)
        o_ref[b] = (yn * lnx_w + lnx_b + bonus_ref[b]) * g_ref[b]


def _rwkv_scan(prep, s0, vec, b, t, nb, tl):
    r, lw, k, v, kk, ka, g, bonus = [a.reshape(b, t, 256) for a in prep]
    seq = pl.BlockSpec((nb, tl, 256), lambda i, j: (i, j, 0))
    st = pl.BlockSpec((nb, HEAD_DIM, 256), lambda i, j: (i, 0, 0))
    return pl.pallas_call(
        functools.partial(_rwkv_scan_body, nb=nb, tl=tl),
        grid=(b // nb, t // tl),
        in_specs=[seq] * 8 + [st, pl.BlockSpec((8, 256), lambda i, j: (0, 0))],
        out_specs=[seq, st],
        out_shape=[jax.ShapeDtypeStruct((b, t, 256), F32), jax.ShapeDtypeStruct((b, HEAD_DIM, 256), F32)],
        scratch_shapes=[pltpu.VMEM((nb, HEAD_DIM, 256), F32), pltpu.VMEM((nb, tl, 256), F32)],
        compiler_params=_cp("parallel", "arbitrary"),
        name="rwkv_scan",
    )(r, lw, k, v, kk, ka, g, bonus, s0, vec)


RWKV_CHUNK = 64
_PRE_TILES = 9


def _split_bf16(x):
    hi = x.astype(BF16)
    return hi, (x - hi.astype(F32)).astype(BF16)


def _dot3(a, b, dims=(((1,), (0,)), ((), ()))):
    ah, al = _split_bf16(a)
    bh, bl = _split_bf16(b)
    dg = lambda x, y: lax.dot_general(x, y, dims, preferred_element_type=F32)
    return dg(ah, bh) + dg(ah, bl) + dg(al, bh)


_NT = (((1,), (1,)), ((), ()))


def _rwkv_chunk_body(r_ref, lw_ref, k_ref, v_ref, kk_ref, ka_ref, g_ref, bonus_ref, s0_ref, vec_ref,
                     o_ref, st_ref, s_scr, y_scr, *, nb, tl):
    L = RWKV_CHUNK
    nc = tl // L

    @pl.when(pl.program_id(1) == 0)
    def _():
        s_scr[...] = s0_ref[...]

    ri = lax.broadcasted_iota(jnp.int32, (L, L), 0)
    ci = lax.broadcasted_iota(jnp.int32, (L, L), 1)
    strict, incl = ri > ci, ri >= ci
    ltri = incl.astype(F32)
    eye = (ri == ci).astype(F32)

    bnn = (((2,), (1,)), ((0,), (0,)))
    bnt = (((2,), (2,)), ((0,), (0,)))

    def chunk(c, carry):
        rows = pl.ds(pl.multiple_of(c * L, L), L)
        lhs_l, rhs_l, v_l, kw_l, wl_l = [], [], [], [], []
        for b in range(nb):
            r, lw, k, v, kk, ka = [ref[b, rows, :] for ref in (r_ref, lw_ref, k_ref, v_ref, kk_ref, ka_ref)]
            cl = jnp.dot(ltri, lw, precision=HI, preferred_element_type=F32)
            e_neg = jnp.exp(-cl)
            e_rem = jnp.exp(cl[L - 1:L, :] - cl)
            kkd, rd = kk * jnp.exp(cl - lw), r * jnp.exp(cl)
            kinv, kainv, kw, kaw = k * e_neg, ka * e_neg, k * e_rem, ka * e_rem
            w_last = jnp.exp(cl[L - 1:L, :])
            for h in range(H_B):
                sl = slice(h * HEAD_DIM, (h + 1) * HEAD_DIM)
                lhs_l.append(jnp.concatenate([kkd[:, sl], rd[:, sl]], axis=0))
                rhs_l.append(jnp.concatenate([kinv[:, sl], kainv[:, sl]], axis=0))
                v_l.append(v[:, sl])
                kw_l.append(jnp.concatenate([kw[:, sl], kaw[:, sl]], axis=0))
                wl_l.append(w_last[:, sl])
        lhs, rhs, vs, kws, wl = [jnp.stack(x) for x in (lhs_l, rhs_l, v_l, kw_l, wl_l)]
        gm = _dot3(lhs, rhs, bnt)
        a_vk = jnp.where(strict, gm[:, :L, :L], 0.0)
        n1 = jnp.where(strict, -gm[:, :L, L:], 0.0)
        t_inv, pw = eye + n1, n1
        for _ in range(5):
            pw = _dot3(pw, pw, bnn)
            t_inv = _dot3(t_inv, eye + pw, bnn)
        s = s_scr[...].reshape(nb * H_B, HEAD_DIM, HEAD_DIM)
        xs = _dot3(lhs, s, bnt)
        u = _dot3(t_inv, xs[:, :L] + _dot3(a_vk, vs, bnn), bnn)
        b_vk = jnp.where(incl, gm[:, L:, :L], 0.0).astype(BF16)
        b_uk = jnp.where(incl, gm[:, L:, L:], 0.0).astype(BF16)
        y = (xs[:, L:] + lax.dot_general(b_vk, vs.astype(BF16), bnn, preferred_element_type=F32)
             - lax.dot_general(b_uk, u.astype(BF16), bnn, preferred_element_type=F32))
        vu_t = jnp.stack([jnp.concatenate([vs[n], -u[n]], axis=0).T for n in range(nb * H_B)])
        s_new = s * wl + _dot3(vu_t, kws, bnn)
        s_scr[...] = s_new.reshape(nb, H_B, HEAD_DIM, HEAD_DIM)
        for b in range(nb):
            for h in range(H_B):
                y_scr[b, rows, h * HEAD_DIM:(h + 1) * HEAD_DIM] = y[b * H_B + h]
        return carry

    lax.fori_loop(0, nc, chunk, 0)
    st_ref[...] = s_scr[...]
    vec = vec_ref[...]
    for b in range(nb):
        o_ref[b] = (_segment_norm(y_scr[b], RWKV_GN_EPS) * vec[5:6] + vec[6:7] + bonus_ref[b]) * g_ref[b]


def _rwkv_chunked(prep, s0, vec, b, t, nb, tl):
    arrs = [a.reshape(b, t, 256) for a in prep]
    seq = pl.BlockSpec((nb, tl, 256), lambda i, j: (i, j, 0))
    st = pl.BlockSpec((nb, H_B, HEAD_DIM, HEAD_DIM), lambda i, j: (i, 0, 0, 0))
    return pl.pallas_call(
        functools.partial(_rwkv_chunk_body, nb=nb, tl=tl),
        grid=(b // nb, t // tl),
        in_specs=[seq] * 8 + [st, pl.BlockSpec((8, 256), lambda i, j: (0, 0))],
        out_specs=[seq, st],
        out_shape=[jax.ShapeDtypeStruct((b, t, 256), F32), jax.ShapeDtypeStruct((b, H_B, HEAD_DIM, HEAD_DIM), F32)],
        scratch_shapes=[pltpu.VMEM((nb, H_B, HEAD_DIM, HEAD_DIM), F32), pltpu.VMEM((nb, tl, 256), F32)],
        compiler_params=_cp("parallel", "arbitrary"),
        name="rwkv_chunked",
    )(*arrs, s0, vec)


def _rwkv_mixer(colsb, shift0, s0, lw, b, t, tm, nb, tl):
    prep = _rwkv_prep(colsb, shift0, lw, t, tm)
    o, st = _rwkv_chunked(prep, s0, lw['rwkv_vec'], b, t, nb, tl)
    shift = colsb.reshape(b, t, SHIFT_PAD)[:, -1, :SHIFT_B]
    return o.reshape(b * t, 256), st, shift


def _segment_norm(y, eps):
    avg = _block_ones(256, HEAD_DIM, F32) * (1.0 / HEAD_DIM)
    yc = y - jnp.dot(y, avg, precision=HI, preferred_element_type=F32)
    return yc * lax.rsqrt(jnp.dot(yc * yc, avg, precision=HI, preferred_element_type=F32) + eps)


def _rwkv_step_body(r_ref, lw_ref, k_ref, v_ref, kk_ref, ka_ref, g_ref, bonus_ref, vec_ref, s_ref, o_ref, st_ref, ft_scr, y_scr):
    h = pl.program_id(0)

    @pl.when(h == 0)
    def _():
        for n, ref in enumerate((r_ref, lw_ref, k_ref, v_ref, kk_ref, ka_ref)):
            ft_scr[n] = ref[...].T

    base = pl.multiple_of(h * HEAD_DIM, HEAD_DIM)
    head = lambda n: ft_scr[n, pl.ds(base, HEAD_DIM), :]
    r_t, w_t, k_t, kk_t, ka_t = head(0), jnp.exp(head(1)), head(2), head(4), head(5)

    def body(i, carry):
        rows = pl.ds(pl.multiple_of(i * HEAD_DIM, HEAD_DIM), HEAD_DIM)
        s = s_ref[rows, :]
        sk = jnp.sum(s * kk_t, axis=0, keepdims=True)
        s = s * w_t - sk * ka_t + ft_scr[3, pl.ds(base + i, 1), :] * k_t
        st_ref[rows, :] = s
        y_scr[pl.ds(base + i, 1), :] = jnp.sum(s * r_t, axis=0, keepdims=True)
        return carry

    lax.fori_loop(0, HEAD_DIM, body, 0)

    @pl.when(h == H_B - 1)
    def _():
        vec = vec_ref[...]
        o_ref[...] = (_segment_norm(y_scr[...].T, RWKV_GN_EPS) * vec[5:6] + vec[6:7] + bonus_ref[...]) * g_ref[...]


def _rwkv_step(colsb, shift0, s_all, layer, lw):
    b = colsb.shape[0]
    hd2 = HEAD_DIM * HEAD_DIM
    full = lambda *_: (0, 0)
    feat = pl.BlockSpec((b, 256), full)
    prep = pl.pallas_call(
        _rwkv_step_prep_body,
        grid=(1,),
        in_specs=[pl.BlockSpec((b, SHIFT_PAD), full), pl.BlockSpec((b, SHIFT_PAD), full), pl.BlockSpec((1, SHIFT_PAD), full),
                  pl.BlockSpec((8, 256), full)] + [pl.BlockSpec((LANE, 256), full)] * 3,
        out_specs=[feat] * 8,
        out_shape=[jax.ShapeDtypeStruct((b, 256), F32)] * 8,
        compiler_params=_cp("arbitrary"),
        name="rwkv_step_prep",
    )(colsb, shift0, lw['rwkv_mu'], lw['rwkv_vec'], lw['rwkv_wup'], lw['rwkv_aup'], lw['rwkv_gup'])
    s_rows = jnp.transpose(s_all, (0, 2, 3, 4, 1)).reshape(-1, b)
    o, st = pl.pallas_call(
        _rwkv_step_body,
        grid=(H_B,),
        in_specs=[feat] * 8 + [pl.BlockSpec((8, 256), full), pl.BlockSpec((hd2, b), lambda h: (layer * H_B + h, 0))],
        out_specs=[feat, pl.BlockSpec((hd2, b), lambda h: (h, 0))],
        out_shape=[jax.ShapeDtypeStruct((b, 256), F32), jax.ShapeDtypeStruct((H_B * hd2, b), F32)],
        scratch_shapes=[pltpu.VMEM((6, 256, b), F32), pltpu.VMEM((256, b), F32)],
        compiler_params=_cp("arbitrary"),
        name="rwkv_step",
    )(*prep, lw['rwkv_vec'], s_rows)
    return o, jnp.transpose(st.reshape(H_B, HEAD_DIM, HEAD_DIM, b), (3, 0, 1, 2))


def _ret_step_body(c_ref, cos_ref, sin_ref, gn_ref, r0_ref, o_ref, rt_ref, ft_scr, acc_scr):
    h = pl.program_id(0)

    @pl.when(h == 0)
    def _():
        x = c_ref[...]
        cs, sn = cos_ref[...], sin_ref[...]
        first = (lax.broadcasted_iota(jnp.int32, cs.shape, 1) % HEAD_DIM) < (HEAD_DIM // 2)

        def rope(z):
            sw = jnp.where(first, pltpu.roll(z, 256 - HEAD_DIM // 2, 1), pltpu.roll(z, HEAD_DIM // 2, 1))
            return z * cs + sw * sn

        ft_scr[0] = rope(x[:, 0:256]).T
        ft_scr[1] = (rope(x[:, 256:512]) * (HEAD_DIM ** -0.5)).T
        ft_scr[2] = x[:, 512:768].T

    base = pl.multiple_of(h * HEAD_DIM, HEAD_DIM)
    nseq = rt_ref.shape[1]
    gamma = jnp.exp(jnp.log1p(-jnp.exp2(-5.0 - jnp.full((1, nseq), h, jnp.int32).astype(F32))))
    q_t, k_t, v_t = [ft_scr[n, pl.ds(base, HEAD_DIM), :] for n in range(3)]
    qk = jnp.sum(q_t * k_t, axis=0, keepdims=True)

    def body(d, cross):
        rows = pl.ds(pl.multiple_of(d * HEAD_DIM, HEAD_DIM), HEAD_DIM)
        r_old = r0_ref[rows, :]
        rt_ref[rows, :] = gamma * r_old + ft_scr[1, pl.ds(base + d, 1), :] * v_t
        return cross + ft_scr[0, pl.ds(base + d, 1), :] * r_old

    cross = lax.fori_loop(0, HEAD_DIM, body, jnp.zeros((HEAD_DIM, nseq), F32))
    acc_scr[pl.ds(base, HEAD_DIM), :] = qk * v_t + gamma * cross

    @pl.when(h == H_D - 1)
    def _():
        o_ref[...] = jax.nn.silu(c_ref[:, 768:1024]) * (_segment_norm(acc_scr[...].T, RET_GN_EPS) * gn_ref[...])


def _ret_step(colsd, cos, sin, r_all, layer, gn):
    b = colsd.shape[0]
    hd2 = HEAD_DIM * HEAD_DIM
    full = lambda *_: (0, 0)
    r_rows = jnp.transpose(r_all, (0, 2, 3, 4, 1)).reshape(-1, b)
    o, rt = pl.pallas_call(
        _ret_step_body,
        grid=(H_D,),
        in_specs=[pl.BlockSpec((b, 1024), full), pl.BlockSpec((b, 256), full), pl.BlockSpec((b, 256), full),
                  pl.BlockSpec((1, 256), full), pl.BlockSpec((hd2, b), lambda h: (layer * H_D + h, 0))],
        out_specs=[pl.BlockSpec((b, 256), full), pl.BlockSpec((hd2, b), lambda h: (h, 0))],
        out_shape=[jax.ShapeDtypeStruct((b, 256), F32), jax.ShapeDtypeStruct((H_D * hd2, b), F32)],
        scratch_shapes=[pltpu.VMEM((3, 256, b), F32), pltpu.VMEM((256, b), F32)],
        compiler_params=_cp("arbitrary"),
        name="ret_step",
    )(colsd, cos, sin, gn.reshape(1, 256), r_rows)
    return o, jnp.transpose(rt.reshape(H_D, HEAD_DIM, HEAD_DIM, b), (3, 0, 1, 2))


def _s5_step_body(u_ref, x0_ref, a1_ref, a2_ref, bt_ref, ct_ref, d_ref, wg_ref, o_ref, xt_ref):
    u = u_ref[...]
    dot_hi = lambda a, b: jnp.dot(a, b, precision=HI, preferred_element_type=F32)
    x0 = x0_ref[...]
    n = x0.shape[0]
    even = (lax.broadcasted_iota(jnp.int32, x0.shape, 0) % 2) == 0
    partner = jnp.where(even, pltpu.roll(x0, n - 1, 0), pltpu.roll(x0, 1, 0))
    x = a1_ref[...] * x0 + a2_ref[...] * partner + dot_hi(bt_ref[...], u.T)
    xt_ref[...] = x
    y = dot_hi(ct_ref[...], x).T + d_ref[...] * u
    z = jax.nn.gelu(y)
    o_ref[...] = z * jax.nn.sigmoid(dot_hi(z, wg_ref[...]))


def _s5_step(u, x_all, layer, lw):
    b = u.shape[0]
    a1, a2, bt, ct, d_row = lw['s5_step']
    n = 2 * G_C * S5_P
    full = lambda *_: (0, 0)
    x_rows = jnp.transpose(x_all, (0, 2, 3, 4, 1)).reshape(-1, b)
    o, xt = pl.pallas_call(
        _s5_step_body,
        grid=(1,),
        in_specs=[pl.BlockSpec((b, C_C), full), pl.BlockSpec((n, b), lambda i: (layer, 0)), pl.BlockSpec((n, 1), full),
                  pl.BlockSpec((n, 1), full), pl.BlockSpec((n, C_C), full), pl.BlockSpec((C_C, n), full),
                  pl.BlockSpec((1, C_C), full), pl.BlockSpec((C_C, C_C), full)],
        out_specs=[pl.BlockSpec((b, C_C), full), pl.BlockSpec((n, b), full)],
        out_shape=[jax.ShapeDtypeStruct((b, C_C), F32), jax.ShapeDtypeStruct((n, b), F32)],
        compiler_params=_cp("arbitrary"),
        name="s5_step",
    )(u, x_rows, a1, a2, bt, ct, d_row, lw['s5_w_glu'])
    return o, jnp.transpose(xt.reshape(G_C, S5_P, 2, b), (3, 0, 1, 2))


def _s5_params(lw):
    lr, li = lw['s5_lambda_re'], lw['s5_lambda_im']
    dt = jnp.exp(lw['s5_log_step'])[:, None]
    mag = jnp.exp(lr * dt)
    ar, ai = mag * jnp.cos(li * dt), mag * jnp.sin(li * dt)
    nr, ni = ar - 1.0, ai
    den = lr * lr + li * li
    fr, fi = (nr * lr + ni * li) / den, (ni * lr - nr * li) / den
    b_re, b_im = lw['s5_b'][0], lw['s5_b'][1]
    bbr = fr[..., None] * b_re - fi[..., None] * b_im
    bbi = fr[..., None] * b_im + fi[..., None] * b_re
    eye = jnp.eye(G_C, dtype=F32)
    bd_in = lambda m: jnp.einsum('gpc,gh->gchp', m, eye).reshape(G_C * S5_CH, G_C * S5_P)
    bd_out = lambda m: jnp.einsum('gcp,gh->gphc', m, eye).reshape(G_C * S5_P, G_C * S5_CH)
    b_big = jnp.concatenate([bd_in(bbr), bd_in(bbi)], axis=1)
    c_big = jnp.concatenate([bd_out(lw['s5_c'][0]), -bd_out(lw['s5_c'][1])], axis=0)
    a_row = jnp.concatenate([ar.reshape(1, -1), ai.reshape(1, -1)], axis=1)
    d_row = lw['s5_d'].reshape(1, C_C)
    n = 2 * G_C * S5_P
    a1 = jnp.stack([ar, ar], axis=-1).reshape(n, 1)
    a2 = jnp.stack([-ai, ai], axis=-1).reshape(n, 1)
    bt = jnp.stack([jnp.einsum('gpc,gh->gphc', bbr, eye), jnp.einsum('gpc,gh->gphc', bbi, eye)], axis=2).reshape(n, G_C * S5_CH)
    ct = jnp.stack([jnp.einsum('gcp,gh->gchp', lw['s5_c'][0], eye), -jnp.einsum('gcp,gh->gchp', lw['s5_c'][1], eye)],
                   axis=-1).reshape(G_C * S5_CH, n)
    return (a_row, b_big, c_big, d_row), (a1, a2, bt, ct, d_row)


def _s5_body(u_ref, x0_ref, a_ref, b_ref, c_ref, d_ref, wg_ref, o_ref, xt_ref, x_scr, bu_scr, xs_scr, *, nb, tt, mm_dtype, prec):
    @pl.when(pl.program_id(0) == 0)
    def _():
        x_scr[...] = x0_ref[...]

    np_ = G_C * S5_P
    ncb = np_ // LANE
    for b in range(nb):
        bu = jnp.dot(u_ref[b].astype(mm_dtype), b_ref[...], precision=prec, preferred_element_type=F32)
        for cb in range(2 * ncb):
            bu_scr[cb, b * tt:(b + 1) * tt, :] = bu[:, cb * LANE:(cb + 1) * LANE]
    a = a_ref[...]

    def step(t, x):
        rows = pl.ds(t, nb, stride=tt)
        new = [None] * (2 * ncb)
        for cb in range(ncb):
            re, im = slice(cb * LANE, (cb + 1) * LANE), slice(np_ + cb * LANE, np_ + (cb + 1) * LANE)
            ar, ai, xr, xi = a[:, re], a[:, im], x[:, re], x[:, im]
            new[cb] = ar * xr - ai * xi + bu_scr[cb, rows, :]
            new[ncb + cb] = ar * xi + ai * xr + bu_scr[ncb + cb, rows, :]
            xs_scr[cb, rows, :] = new[cb]
            xs_scr[ncb + cb, rows, :] = new[ncb + cb]
        return jnp.concatenate(new, axis=1)

    x_last = lax.fori_loop(0, tt, step, x_scr[...], unroll=8)
    x_scr[...] = x_last
    xt_ref[...] = x_last
    for b in range(nb):
        u = u_ref[b]
        xs = jnp.concatenate([xs_scr[cb, b * tt:(b + 1) * tt, :] for cb in range(2 * ncb)], axis=1)
        y = jnp.dot(xs.astype(mm_dtype), c_ref[...], precision=prec, preferred_element_type=F32) + d_ref[...] * u
        z = jax.nn.gelu(y)
        o_ref[b] = z * jax.nn.sigmoid(jnp.dot(z.astype(mm_dtype), wg_ref[...], precision=prec, preferred_element_type=F32))


def _s5_mixer(u, x0, lw, b, t, tt, exact):
    a_row, b_big, c_big, d_row = lw['s5']
    mm_dtype = F32 if exact else BF16
    prec = HI if exact else None
    x0l = jnp.concatenate([x0[..., 0].reshape(b, -1), x0[..., 1].reshape(b, -1)], axis=1)
    np2 = 2 * G_C * S5_P
    fixed = lambda i: (0, 0)
    o, xt = pl.pallas_call(
        functools.partial(_s5_body, nb=b, tt=tt, mm_dtype=mm_dtype, prec=prec),
        grid=(t // tt,),
        in_specs=[pl.BlockSpec((b, tt, C_C), lambda i: (0, i, 0)), pl.BlockSpec((b, np2), fixed),
                  pl.BlockSpec((1, np2), fixed), pl.BlockSpec((C_C, np2), fixed), pl.BlockSpec((np2, C_C), fixed),
                  pl.BlockSpec((1, C_C), fixed), pl.BlockSpec((C_C, C_C), fixed)],
        out_specs=[pl.BlockSpec((b, tt, C_C), lambda i: (0, i, 0)), pl.BlockSpec((b, np2), fixed)],
        out_shape=[jax.ShapeDtypeStruct((b, t, C_C), F32), jax.ShapeDtypeStruct((b, np2), F32)],
        scratch_shapes=[pltpu.VMEM((b, np2), F32), pltpu.VMEM((np2 // LANE, b * tt, LANE), F32),
                        pltpu.VMEM((np2 // LANE, b * tt, LANE), F32)],
        compiler_params=_cp("arbitrary"),
        name="s5",
    )(u.reshape(b, t, C_C), x0l, a_row, b_big.astype(mm_dtype), c_big.astype(mm_dtype), d_row, lw['s5_w_glu'].astype(mm_dtype))
    xt = xt.reshape(b, 2, G_C, S5_P)
    return o.reshape(b * t, C_C), jnp.stack([xt[:, 0], xt[:, 1]], axis=-1)


def _ret_tables(pos, c):
    cos, sin = _rope_tables(pos, HEAD_DIM, RET_THETA, HEAD_DIM, H_D)
    log_g = jnp.log1p(-jnp.exp2(-5.0 - jnp.arange(H_D, dtype=F32)))
    i = jnp.arange(c, dtype=F32)
    diff = i[:, None] - i[None, :]
    dmat = jnp.where(diff >= 0, jnp.exp(jnp.maximum(diff, 0.0)[None] * log_g[:, None, None]), 0.0).reshape(H_D * c, c)
    q_dec = jnp.repeat(jnp.exp((i + 1.0)[None] * log_g[:, None]).T, HEAD_DIM, axis=1)
    k_dec = jnp.repeat(jnp.exp((c - 1.0 - i)[None] * log_g[:, None]).T, HEAD_DIM, axis=1)
    chunk_dec = jnp.repeat(jnp.exp(c * log_g), HEAD_DIM).reshape(256, 1)
    return cos, sin, dmat, q_dec, k_dec, chunk_dec


def _ret_body(c_ref, cos_ref, sin_ref, dmat_ref, qdec_ref, kdec_ref, cdec_ref, r0_ref, gn_ref, o_ref, rt_ref, r_scr, *, c):
    @pl.when(pl.program_id(1) == 0)
    def _():
        r_scr[...] = r0_ref[0]

    x = c_ref[0]
    q, k, v, g = x[:, 0:256], x[:, 256:512], x[:, 512:768], x[:, 768:1024]
    cs, sn = cos_ref[...], sin_ref[...]
    lane = lax.broadcasted_iota(jnp.int32, (c, 256), 1)
    first = (lane % HEAD_DIM) < (HEAD_DIM // 2)

    def rope(z):
        sw = jnp.where(first, pltpu.roll(z, 256 - HEAD_DIM // 2, 1), pltpu.roll(z, HEAD_DIM // 2, 1))
        return z * cs + sw * sn

    q = rope(q)
    k = rope(k) * (HEAD_DIM ** -0.5)
    head = lane // HEAD_DIM
    kb, vb = k.astype(BF16), v.astype(BF16)
    qstack = jnp.concatenate([jnp.where(head == h, q, 0.0) for h in range(H_D)], axis=0).astype(BF16)
    s = lax.dot_general(qstack, kb, (((1,), (1,)), ((), ())), preferred_element_type=F32) * dmat_ref[...]
    pv = jnp.dot(s.astype(BF16), vb, preferred_element_type=F32)
    inner = jnp.zeros((c, 256), F32)
    for h in range(H_D):
        inner = inner + jnp.where(head == h, pv[h * c:(h + 1) * c], 0.0)
    r_old = r_scr[...]
    cross = jnp.dot((q * qdec_ref[...]).astype(BF16), r_old.astype(BF16), preferred_element_type=F32)
    kv = lax.dot_general((k * kdec_ref[...]).astype(BF16), vb, (((0,), (0,)), ((), ())), preferred_element_type=F32)
    bd = _block_ones(256, HEAD_DIM, F32)
    r_new = cdec_ref[...] * r_old + kv * bd
    r_scr[...] = r_new
    rt_ref[0] = r_new
    o = inner + cross
    avg = bd * (1.0 / HEAD_DIM)
    oc = o - jnp.dot(o, avg, precision=HI, preferred_element_type=F32)
    on = oc * lax.rsqrt(jnp.dot(oc * oc, avg, precision=HI, preferred_element_type=F32) + RET_GN_EPS)
    o_ref[0] = jax.nn.silu(g) * (on * gn_ref[...])


def _ret_mixer(colsd, r0, lw, tabs, b, t):
    c = RET_CHUNK if t % RET_CHUNK == 0 else t
    cos, sin, dmat, q_dec, k_dec, chunk_dec = tabs
    eye = jnp.eye(H_D, dtype=F32)
    r0l = jnp.einsum('bhde,hg->bhdge', r0, eye).reshape(b, 256, 256)
    n_t = t // c
    fixed = lambda i, j: (0, 0)
    o, rt = pl.pallas_call(
        functools.partial(_ret_body, c=c),
        grid=(b, n_t),
        in_specs=[pl.BlockSpec((1, c, 1024), lambda i, j: (i, j, 0)),
                  pl.BlockSpec((c, 256), lambda i, j: (j, 0)), pl.BlockSpec((c, 256), lambda i, j: (j, 0)),
                  pl.BlockSpec((H_D * c, c), fixed), pl.BlockSpec((c, 256), fixed), pl.BlockSpec((c, 256), fixed),
                  pl.BlockSpec((256, 1), fixed), pl.BlockSpec((1, 256, 256), lambda i, j: (i, 0, 0)),
                  pl.BlockSpec((1, 256), fixed)],
        out_specs=[pl.BlockSpec((1, c, 256), lambda i, j: (i, j, 0)), pl.BlockSpec((1, 256, 256), lambda i, j: (i, 0, 0))],
        out_shape=[jax.ShapeDtypeStruct((b, t, 256), F32), jax.ShapeDtypeStruct((b, 256, 256), F32)],
        scratch_shapes=[pltpu.VMEM((256, 256), F32)],
        compiler_params=_cp("parallel", "arbitrary"),
        name="retention",
    )(colsd.reshape(b, t, 1024), cos, sin, dmat, q_dec, k_dec, chunk_dec, r0l, lw['ret_gn'].reshape(1, 256))
    rt = jnp.einsum('bhdge,hg->bhde', rt.reshape(b, H_D, HEAD_DIM, H_D, HEAD_DIM), eye)
    return o.reshape(b * t, 256), rt


def _partner(x, d, period):
    pos = lax.broadcasted_iota(jnp.int32, x.shape, 1) % period
    return jnp.where(pos + d < period, pltpu.roll(x, LANE - d, 1), pltpu.roll(x, period - d, 1))


def _out_body(x_ref, oa_ref, ob_ref, oc_ref, od_ref, w_ref, nw_ref, wr_ref, br_ref, x1_ref, h_ref, comb_ref):
    acc = x_ref[...]
    for i, ref in enumerate((oa_ref, ob_ref, oc_ref, od_ref)):
        acc = acc + jnp.dot(ref[...].astype(BF16), w_ref[256 * i:256 * (i + 1), :], preferred_element_type=F32)
    x1_ref[...] = acc
    h = acc * lax.rsqrt(jnp.mean(acc * acc, axis=-1, keepdims=True) + RMS_EPS) * nw_ref[...]
    h_ref[...] = h.astype(BF16)
    logits = jnp.dot(h, wr_ref[...], precision=HI, preferred_element_type=F32) + br_ref[...]
    le, lg = logits[:, :LANE], logits[:, LANE:]
    lane = lax.broadcasted_iota(jnp.int32, le.shape, 1)
    mg = jnp.max(lg, axis=-1, keepdims=True)
    eg = jnp.exp(lg - mg)
    pg = eg / (jnp.sum(eg, axis=-1, keepdims=True) * (1.0 / 32.0))
    gidx = (lane % N_EXPERTS) // EXP_PER_GROUP
    g_rank = jnp.zeros_like(pg)
    for d in range(1, N_GROUPS):
        other = pltpu.roll(pg, LANE - EXP_PER_GROUP * d, 1)
        wrapped = gidx + d >= N_GROUPS
        g_rank = g_rank + jnp.where((other > pg) | ((other == pg) & wrapped), 1.0, 0.0)
    kidx = lane % EXP_PER_GROUP
    others = [_partner(le, d, EXP_PER_GROUP) for d in range(1, EXP_PER_GROUP)]
    me = functools.reduce(jnp.maximum, others, le)
    ee = jnp.exp(le - me)
    se = ee
    for d in range(1, EXP_PER_GROUP):
        se = se + _partner(ee, d, EXP_PER_GROUP)
    pe = ee / se
    e_rank = jnp.zeros_like(pe)
    for d in range(1, EXP_PER_GROUP):
        other = _partner(pe, d, EXP_PER_GROUP)
        wrapped = kidx + d >= EXP_PER_GROUP
        e_rank = e_rank + jnp.where((other > pe) | ((other == pe) & wrapped), 1.0, 0.0)
    top = jnp.where(e_rank < 2.0, pe, 0.0)
    den = top
    for d in range(1, EXP_PER_GROUP):
        den = den + _partner(top, d, EXP_PER_GROUP)
    comb = jnp.where((g_rank < 1.0) & (lane < N_EXPERTS), pg * (top / den), 0.0)
    comb_ref[...] = comb


def _out_router(x, oa, ob, oc, od, lw, tm):
    n = x.shape[0]
    row = lambda i: (i, 0)
    fixed = lambda i: (0, 0)
    mix = pl.BlockSpec((tm, 256), row)
    return pl.pallas_call(
        _out_body,
        grid=(n // tm,),
        in_specs=[pl.BlockSpec((tm, D_MODEL), row), mix, mix, mix, mix,
                  pl.BlockSpec((D_MODEL, D_MODEL), fixed), pl.BlockSpec((1, D_MODEL), fixed),
                  pl.BlockSpec((D_MODEL, 2 * LANE), fixed), pl.BlockSpec((1, 2 * LANE), fixed)],
        out_specs=[pl.BlockSpec((tm, D_MODEL), row), pl.BlockSpec((tm, D_MODEL), row), pl.BlockSpec((tm, LANE), row)],
        out_shape=[jax.ShapeDtypeStruct((n, D_MODEL), F32), jax.ShapeDtypeStruct((n, D_MODEL), BF16),
                   jax.ShapeDtypeStruct((n, LANE), F32)],
        compiler_params=_cp("parallel"),
        name="out_router",
    )(x, oa, ob, oc, od, lw['w_out'], lw['norm_ffn'], lw['w_router'], lw['b_router'])


def _router_weights(w_grp, b_grp, w_exp, b_exp):
    we = jnp.transpose(w_exp, (1, 0, 2)).reshape(D_MODEL, N_EXPERTS)
    wg = jnp.repeat(w_grp, EXP_PER_GROUP, axis=1)
    reps = LANE // N_EXPERTS
    w = jnp.concatenate([jnp.tile(we, (1, reps)), jnp.tile(wg, (1, reps))], axis=1)
    b = jnp.concatenate([jnp.tile(b_exp.reshape(1, N_EXPERTS), (1, reps)),
                         jnp.tile(jnp.repeat(b_grp, EXP_PER_GROUP).reshape(1, N_EXPERTS), (1, reps))], axis=1)
    return w, b


def _moe_body(h_ref, comb_ref, x1_ref, wg_ref, wu_ref, wd_ref, nf_ref, *out_refs, final):
    acc_ref = out_refs[-1]
    e = pl.program_id(1)

    @pl.when(e == 0)
    def _():
        acc_ref[...] = x1_ref[...]

    h = h_ref[...]
    comb = comb_ref[...]
    lane = lax.broadcasted_iota(jnp.int32, comb.shape, 1)
    c = jnp.sum(jnp.where(lane == e, comb, 0.0), axis=-1, keepdims=True)
    hg = jnp.dot(h, wg_ref[0], preferred_element_type=F32)
    hu = jnp.dot(h, wu_ref[0], preferred_element_type=F32)
    act = (jax.nn.silu(hg) * hu * c).astype(BF16)
    acc_ref[...] += jnp.dot(act, wd_ref[0], preferred_element_type=F32)

    @pl.when(e == N_EXPERTS - 1)
    def _():
        x2 = acc_ref[...]
        if final:
            out_refs[0][...] = x2 * lax.rsqrt(jnp.mean(x2 * x2, axis=-1, keepdims=True) + RMS_EPS) * nf_ref[...]
        else:
            out_refs[0][...] = x2


def _moe(h, comb, x1, lw, norm_final, tm, final):
    n = x1.shape[0]
    row = lambda i, e: (i, 0)
    per_e = lambda i, e: (e, 0, 0)
    return pl.pallas_call(
        functools.partial(_moe_body, final=final),
        grid=(n // tm, N_EXPERTS),
        in_specs=[pl.BlockSpec((tm, D_MODEL), row), pl.BlockSpec((tm, LANE), row), pl.BlockSpec((tm, D_MODEL), row),
                  pl.BlockSpec((1, D_MODEL, D_EXPERT), per_e), pl.BlockSpec((1, D_MODEL, D_EXPERT), per_e),
                  pl.BlockSpec((1, D_EXPERT, D_MODEL), per_e), pl.BlockSpec((1, D_MODEL), lambda i, e: (0, 0))],
        out_specs=pl.BlockSpec((tm, D_MODEL), row),
        out_shape=jax.ShapeDtypeStruct((n, D_MODEL), F32),
        scratch_shapes=[pltpu.VMEM((tm, D_MODEL), F32)],
        compiler_params=_cp("parallel", "arbitrary"),
        name="moe",
    )(h, comb, x1, lw['moe_wg'], lw['moe_wu'], lw['moe_wd'], norm_final.reshape(1, D_MODEL))


def _head_norm(y, eps):
    yc = y - jnp.mean(y, -1, keepdims=True)
    return yc * lax.rsqrt(jnp.mean(yc * yc, -1, keepdims=True) + eps)


def _masked_softmax(s, mask):
    s = jnp.where(mask, s.astype(F32), NEG_INF)
    return jnp.where(mask, jax.nn.softmax(s, axis=-1), 0.0)


def _attend(q, k, v, mask):
    s = jnp.einsum('...qgrd,...kgd->...qgrk', q, k) * HEAD_DIM ** -0.5
    p = _masked_softmax(s, mask)
    return jnp.einsum('...qgrk,...kgd->...qgrd', p, v.astype(F32))


def _x_nsa_compress(rows, w1, b1, w2):
    B, Tk = rows.shape[:2]
    m = CMP_BLOCK // CMP_STRIDE
    n_cmp = (Tk - CMP_BLOCK) // CMP_STRIDE + 1
    n_chunks = n_cmp + m - 1
    chunks = rows[:, :n_chunks * CMP_STRIDE].reshape(B, n_chunks, CMP_STRIDE, N_KV_A, HEAD_DIM)
    w1r = w1.reshape(m, CMP_STRIDE, HEAD_DIM, CMP_HIDDEN)
    h = b1.astype(F32)
    for j in range(m):
        h = h + jnp.einsum('bcsgd,sdf->bcgf', chunks[:, j:j + n_cmp], w1r[j])
    return jnp.einsum('bcgf,fd->bcgd', jax.nn.gelu(h), w2)


def _x_nsa_sample(q, kv, gate, pos, cmp_w1, cmp_b1, cmp_w2, past_rows, win_buf):
    B = q.shape[0]
    T = 1
    q = q.reshape(B, T, N_KV_A, R_A, HEAD_DIM)
    kv = kv.reshape(B, T, 6, N_KV_A, HEAD_DIM)
    rows = kv[:, :, 0:4]
    win_rows = kv[:, :, 4:6]
    full = jnp.concatenate([past_rows, rows], axis=1)
    Tk = full.shape[1]
    kc = _x_nsa_compress(full[:, :, 0], cmp_w1[0], cmp_b1[0], cmp_w2[0])
    vc = _x_nsa_compress(full[:, :, 1], cmp_w1[1], cmp_b1[1], cmp_w2[1])
    n_sel = -(-Tk // SEL_BLOCK)
    sel = jnp.pad(full[:, :, 2:4], ((0, 0), (0, n_sel * SEL_BLOCK - Tk), (0, 0), (0, 0), (0, 0)))
    sel = sel.reshape(B, n_sel, SEL_BLOCK, 2, N_KV_A, HEAD_DIM).transpose(3, 0, 4, 1, 2, 5)
    n_cmp = kc.shape[1]
    ov = _cmp_to_sel_t(n_cmp, n_cmp, n_sel).T
    q_pos = jnp.asarray(pos, jnp.int32)
    ks_blk, vs_blk = sel[0], sel[1]
    cmp_end = jnp.arange(n_cmp) * CMP_STRIDE + CMP_BLOCK - 1
    s = jnp.einsum('bqgrd,bngd->bqgrn', q, kc) * HEAD_DIM ** -0.5
    p_cmp = _masked_softmax(s, (cmp_end[None, :] <= q_pos[:, None])[None, :, None, None, :])
    o_c = jnp.einsum('bqgrn,bngd->bqgrd', p_cmp, vc.astype(F32))
    imp = jnp.einsum('bqgrn,nj->bqgj', p_cmp, ov)
    blk = jnp.arange(n_sel)[None, :]
    cur = (q_pos // SEL_BLOCK)[:, None]
    forced = (blk == 0) | (blk == cur) | (blk == cur - 1)
    causal = blk * SEL_BLOCK <= q_pos[:, None]
    imp = jnp.where(forced[None, :, None, :], FORCED_SCORE, jnp.where(causal[None, :, None, :], imp, BLOCKED_SCORE))
    _, idx = lax.top_k(imp, min(TOP_K, n_sel))
    n_k = idx.shape[-1]
    bi = jnp.arange(B)[:, None, None, None]
    gi = jnp.arange(N_KV_A)[None, None, :, None]
    ksel = ks_blk[bi, gi, idx]
    vsel = vs_blk[bi, gi, idx]
    s2 = jnp.einsum('bqgrd,bqgksd->bqgrks', q, ksel) * HEAD_DIM ** -0.5
    kpos = idx[..., None] * SEL_BLOCK + jnp.arange(SEL_BLOCK)
    mask2 = (kpos <= q_pos[None, :, None, None, None]).reshape(B, T, N_KV_A, 1, n_k * SEL_BLOCK)
    p2 = _masked_softmax(s2.reshape(B, T, N_KV_A, R_A, n_k * SEL_BLOCK), mask2)
    o_s = jnp.einsum('bqgrm,bqgmd->bqgrd', p2, vsel.reshape(B, T, N_KV_A, n_k * SEL_BLOCK, HEAD_DIM).astype(F32))
    wb = win_buf.shape[1]
    kw = jnp.concatenate([win_buf, win_rows], axis=1)
    kpos = int(pos[0]) - wb + np.arange(wb + T)
    diff = pos[:, None] - kpos[None, :]
    mask = (diff >= 0) & (diff < WINDOW)
    o_w = _attend(q, kw[:, :, 0], kw[:, :, 1], mask[None, :, None, None, :])
    new_win = kw[:, wb + T - min(WINDOW, wb + T):]
    g = jax.nn.sigmoid(gate.astype(F32)).reshape(B, T, N_KV_A, R_A, 3)
    o = g[..., 0:1] * o_c + g[..., 1:2] * o_s + g[..., 2:3] * o_w
    return o.reshape(B, C_A), rows, new_win


def _x_rwkv_sample(cols, shift0, S0, mu, vec, w_up, a_up, g_up):
    B = cols.shape[0]
    xs = cols + mu * (shift0 - cols)
    r, k, v, wd, ad, gd = jnp.split(xs, _offsets((C_B, C_B, C_B, LORA_W, LORA_A, LORA_G)), axis=-1)
    w0, a0, k_k, k_a, r_k, lnx_w, lnx_b = vec
    w_log = -jax.nn.softplus(-(w0 + jnp.tanh(wd) @ w_up)) - 0.5
    decay = jnp.exp(-jnp.exp(w_log))
    a = jax.nn.sigmoid(a0 + ad @ a_up)
    g = jax.nn.sigmoid(gd) @ g_up
    hs = lambda z: z.reshape(B, H_B, HEAD_DIM)
    kk = hs(k * k_k)
    kk = kk * lax.rsqrt(jnp.sum(kk * kk, -1, keepdims=True) + 1e-12)
    k = k * (1.0 + (a - 1.0) * k_a)
    r_h, w_h, k_h, v_h, a_h = hs(r), hs(decay), hs(k), hs(v), hs(a)
    sk = jnp.einsum('bhij,bhj->bhi', S0, kk)
    S = S0 * w_h[:, :, None, :] - sk[..., None] * (kk * a_h)[:, :, None, :] + v_h[..., None] * k_h[:, :, None, :]
    y = jnp.einsum('bhij,bhj->bhi', S, r_h)
    y = _head_norm(y, RWKV_GN_EPS).reshape(B, C_B) * lnx_w + lnx_b
    bonus = jnp.sum(r_h * k_h * r_k.reshape(H_B, HEAD_DIM), -1, keepdims=True) * v_h
    return (y + bonus.reshape(B, C_B)) * g, S, cols


def _x_s5_sample(u, x0, lw):
    a_row, b_big, c_big, d_row = lw['s5']
    b = u.shape[0]
    np_ = G_C * S5_P
    x0l = jnp.concatenate([x0[..., 0].reshape(b, -1), x0[..., 1].reshape(b, -1)], axis=1)
    bu = jnp.dot(u, b_big, precision=HI)
    ar, ai = a_row[:, :np_], a_row[:, np_:]
    xr = ar * x0l[:, :np_] - ai * x0l[:, np_:] + bu[:, :np_]
    xi = ar * x0l[:, np_:] + ai * x0l[:, :np_] + bu[:, np_:]
    y = jnp.dot(jnp.concatenate([xr, xi], axis=1), c_big, precision=HI) + d_row * u
    z = jax.nn.gelu(y)
    out = z * jax.nn.sigmoid(jnp.dot(z, lw['s5_w_glu'], precision=HI))
    return out, jnp.stack([xr.reshape(b, G_C, S5_P), xi.reshape(b, G_C, S5_P)], axis=-1)


def _x_ret_sample(cols, cos, sin, R0, gn_w):
    B = cols.shape[0]
    q, k, v, g = jnp.split(cols, 4, axis=-1)
    half = HEAD_DIM // 2

    def rope(z):
        z = z.reshape(B, H_D, HEAD_DIM)
        sw = jnp.concatenate([z[..., half:], z[..., :half]], -1)
        return z * cos.reshape(B, H_D, HEAD_DIM) + sw * sin.reshape(B, H_D, HEAD_DIM)

    q = rope(q)
    k = rope(k) * HEAD_DIM ** -0.5
    v = v.reshape(B, H_D, HEAD_DIM)
    gamma = 1.0 - jnp.exp2(-5.0 - jnp.arange(H_D, dtype=F32))
    log_g = jnp.log1p(-jnp.exp2(-5.0 - jnp.arange(H_D, dtype=F32)))
    inner = jnp.einsum('bhd,bhd->bh', q, k)[..., None] * v
    cross = jnp.einsum('bhd,h,bhde->bhe', q, jnp.exp(log_g), R0)
    R = jnp.exp(log_g)[None, :, None, None] * R0 + jnp.einsum('bhd,bhe->bhde', k, v)
    del gamma
    o = _head_norm(inner + cross, RET_GN_EPS).reshape(B, C_D) * gn_w
    return jax.nn.silu(g) * o, R


def _prep_layer(l, p):
    w_in = p['w_in'][l]
    o = _offsets(SPLIT_SIZES)
    segs = jnp.split(w_in, o, axis=1)
    w_all = jnp.concatenate([segs[0], segs[1], _pad_to(segs[2], LANE, 1), _pad_to(segs[3], SHIFT_PAD, 1), segs[4], segs[5]],
                            axis=1).astype(BF16)
    w_prec = w_in[:, :PRECISE_COLS]
    w_lo = (w_prec - w_prec.astype(BF16).astype(F32)).astype(BF16)
    lw = {'layer': l, 'w_all': w_all, 'w_lo': w_lo, 'norm_mix': p['norm_mix'][l]}
    lw['cmp'] = _cmp_weights(p['nsa_cmp_w1'][l], p['nsa_cmp_b1'][l], p['nsa_cmp_w2'][l])
    lw['cmp_raw'] = (p['nsa_cmp_w1'][l], p['nsa_cmp_b1'][l], p['nsa_cmp_w2'][l])
    lw['rwkv_mu'] = _pad_to(p['rwkv_mu'][l].reshape(1, SHIFT_B), SHIFT_PAD, 1)
    lw['rwkv_vec'] = _pad_to(p['rwkv_vec'][l], 8, 0)
    z = lambda a, b: jnp.zeros((a, b), F32)
    lw['rwkv_wup'] = jnp.concatenate([p['rwkv_w_up'][l], z(LANE - LORA_W, C_B)], axis=0)
    lw['rwkv_aup'] = jnp.concatenate([z(LORA_W, C_B), p['rwkv_a_up'][l], z(LANE - LORA_W - LORA_A, C_B)], axis=0)
    lw['rwkv_gup'] = jnp.concatenate([z(LORA_W + LORA_A, C_B), p['rwkv_g_up'][l], z(LANE - LORA_W - LORA_A - LORA_G, C_B)], axis=0)
    lw['rwkv_raw'] = (p['rwkv_mu'][l], p['rwkv_vec'][l], p['rwkv_w_up'][l], p['rwkv_a_up'][l], p['rwkv_g_up'][l])
    for name in ('s5_lambda_re', 's5_lambda_im', 's5_b', 's5_c', 's5_d', 's5_log_step', 's5_w_glu', 'ret_gn'):
        lw[name] = p[name][l]
    lw['s5'], lw['s5_step'] = _s5_params(lw)
    lw['w_out'] = p['w_out'][l].astype(BF16)
    lw['norm_ffn'] = p['norm_ffn'][l].reshape(1, D_MODEL)
    lw['w_router'], lw['b_router'] = _router_weights(p['moe_w_grp'][l], p['moe_b_grp'][l], p['moe_w_exp'][l], p['moe_b_exp'][l])
    lw['moe_wg'] = p['moe_w_gate'][l].astype(BF16)
    lw['moe_wu'] = p['moe_w_up'][l].astype(BF16)
    lw['moe_wd'] = p['moe_w_down'][l].astype(BF16)
    return lw


def _prompt_layer(x, lw, tabs, b, t, norm_final):
    cos_a, sin_a, ret_tabs = tabs
    q, kv, gate, colsb, u, colsd = _proj(x, lw['norm_mix'], lw['w_all'], lw['w_lo'], cos_a, sin_a, 512)
    o_a = _nsa_prompt_mixer(q, kv, gate, lw, b, t)
    o_b, s_rwkv, s_shift = _rwkv_mixer(colsb, jnp.zeros((b, SHIFT_PAD), F32), jnp.zeros((b, H_B, HEAD_DIM, HEAD_DIM), F32),
                                       lw, b, t, 512, b, 256)
    o_c, s_s5 = _s5_mixer(u, jnp.zeros((b, G_C, S5_P, 2), F32), lw, b, t, 256, False)
    o_d, s_ret = _ret_mixer(colsd, jnp.zeros((b, H_D, HEAD_DIM, HEAD_DIM), F32), lw, ret_tabs, b, t)
    x1, h, comb = _out_router(x, o_a, o_b, o_c, o_d, lw, 512)
    x2 = _moe(h, comb, x1, lw, norm_final, 1024, lw['layer'] == DEPTH - 1)
    kv3 = kv.reshape(b, t, 768)
    rows = kv3[:, :, :512].reshape(b, t, 4, N_KV_A, HEAD_DIM)
    win = kv3[:, t - min(WINDOW, t):, 512:].reshape(b, min(WINDOW, t), 2, N_KV_A, HEAD_DIM)
    return x2, (rows, win, s_rwkv, s_shift, s_s5, s_ret)


def _sample_layer(x, lw, tabs, b, pos, cache_kv, page_table, win_buf, s_rwkv, s_shift, s_s5, s_ret, norm_final):
    cos_a, sin_a, ret_cs = tabs
    q, kv, gate, colsb, u, colsd = _proj(x, lw['norm_mix'], lw['w_all'], lw['w_lo'], cos_a, sin_a, b)
    o_a, win = _nsa_sample(q, kv, gate, cache_kv, win_buf, lw['layer'], page_table, lw['cmp'], int(pos[0]))
    rows = kv[:, :512].reshape(b, 1, 4, N_KV_A, HEAD_DIM)
    o_b, s_rwkv = _rwkv_step(colsb, _pad_to(s_shift, SHIFT_PAD, 1), s_rwkv, lw['layer'], lw)
    s_shift = colsb[:, :SHIFT_B]
    o_c, s_s5 = _s5_step(u, s_s5, lw['layer'], lw)
    o_d, s_ret = _ret_step(colsd, ret_cs[0], ret_cs[1], s_ret, lw['layer'], lw['ret_gn'])
    x1, h, comb = _out_router(x, o_a, o_b, o_c, o_d, lw, b)
    x2 = _moe(h, comb, x1, lw, norm_final, b, lw['layer'] == DEPTH - 1)
    return x2, (rows, win, s_rwkv, s_shift, s_s5, s_ret)


def kernel(x_prompt, x_sample, cache_nsa_kv, cache_nsa_win, state_rwkv, state_rwkv_shift, state_s5, state_ret, page_table, norm_mix, w_in, nsa_cmp_w1, nsa_cmp_b1, nsa_cmp_w2, rwkv_mu, rwkv_vec, rwkv_w_up, rwkv_a_up, rwkv_g_up, s5_lambda_re, s5_lambda_im, s5_b, s5_c, s5_d, s5_log_step, s5_w_glu, ret_gn, w_out, norm_ffn, moe_w_grp, moe_b_grp, moe_w_exp, moe_b_exp, moe_w_gate, moe_w_up, moe_w_down, norm_final):
    p = dict(norm_mix=norm_mix, w_in=w_in, nsa_cmp_w1=nsa_cmp_w1, nsa_cmp_b1=nsa_cmp_b1, nsa_cmp_w2=nsa_cmp_w2,
             rwkv_mu=rwkv_mu, rwkv_vec=rwkv_vec, rwkv_w_up=rwkv_w_up, rwkv_a_up=rwkv_a_up, rwkv_g_up=rwkv_g_up,
             s5_lambda_re=s5_lambda_re, s5_lambda_im=s5_lambda_im, s5_b=s5_b, s5_c=s5_c, s5_d=s5_d,
             s5_log_step=s5_log_step, s5_w_glu=s5_w_glu, ret_gn=ret_gn, w_out=w_out, norm_ffn=norm_ffn,
             moe_w_grp=moe_w_grp, moe_b_grp=moe_b_grp, moe_w_exp=moe_w_exp, moe_b_exp=moe_b_exp,
             moe_w_gate=moe_w_gate, moe_w_up=moe_w_up, moe_w_down=moe_w_down)
    bp, tp = x_prompt.shape[:2]
    bs, ts = x_sample.shape[:2]
    past_len = page_table.shape[1] * cache_nsa_kv.shape[2]
    pos_p = np.arange(tp)
    pos_s = past_len + np.arange(ts)
    c = RET_CHUNK if tp % RET_CHUNK == 0 else tp
    tabs_p = _rope_tables(pos_p, ROT_DIM, ROPE_THETA, HEAD_DIM, 2) + (_ret_tables(pos_p, c),)
    pos_rows = np.repeat(pos_s, bs)
    tabs_s = _rope_tables(pos_rows, ROT_DIM, ROPE_THETA, HEAD_DIM, 2) + (_rope_tables(pos_rows, HEAD_DIM, RET_THETA, HEAD_DIM, H_D),)
    xp = x_prompt.reshape(bp * tp, D_MODEL)
    xs = x_sample.reshape(bs * ts, D_MODEL)
    sts_p, sts_s = [], []
    for l in range(DEPTH):
        lw = _prep_layer(l, p)
        xp, st_p = _prompt_layer(xp, lw, tabs_p, bp, tp, norm_final)
        xs, st_s = _sample_layer(xs, lw, tabs_s, bs, pos_s, cache_nsa_kv, page_table, cache_nsa_win, state_rwkv,
                                     state_rwkv_shift[l], state_s5, state_ret, norm_final)
        rows, win, s1, s2, s3, s4 = st_s
        sts_s.append((rows, win, s1, s2, s3, s4))
        sts_p.append(st_p)
    new_p = [jnp.stack([st[i] for st in sts_p]) for i in range(6)]
    new_s = [jnp.stack([st[i] for st in sts_s]) for i in range(6)]
    return (xp.reshape(bp, tp, D_MODEL), xs.reshape(bs, ts, D_MODEL), new_p[0], new_s[0], new_p[1], new_s[1],
            new_p[2], new_s[2], new_p[3], new_s[3], new_p[4], new_s[4], new_p[5], new_s[5])
```

```python
import functools
import math

import numpy as np
import jax
import jax.numpy as jnp
from jax import lax
from jax.experimental import pallas as pl
from jax.experimental.pallas import tpu as pltpu

F32 = jnp.float32
BF16 = jnp.bfloat16
HI = lax.Precision.HIGHEST

D_MODEL = 1024
DEPTH = 2
HEAD_DIM = 64
C_A = C_B = C_C = C_D = 256
H_A = 4
N_KV_A = 2
R_A = 2
ROT_DIM = 16
ROPE_THETA = 500000.0
CMP_BLOCK = 32
CMP_STRIDE = 16
CMP_HIDDEN = 128
SEL_BLOCK = 64
TOP_K = 16
WINDOW = 512
NEG_INF = -1e30
FORCED_SCORE = 1e9
BLOCKED_SCORE = -1e9
H_B = 4
LORA_W = 16
LORA_A = 16
LORA_G = 32
SHIFT_B = 832
SHIFT_PAD = 896
RWKV_GN_EPS = 64e-5
S5_CH = 16
G_C = 16
S5_P = 64
H_D = 4
RET_CHUNK = 128
RET_THETA = 10000.0
RET_GN_EPS = 1e-5
N_GROUPS = 4
EXP_PER_GROUP = 4
N_EXPERTS = 16
D_EXPERT = 256
RMS_EPS = 1e-6
SPLIT_SIZES = (C_A, 6 * N_KV_A * HEAD_DIM, 3 * H_A, SHIFT_B, C_C, 4 * C_D)
LANE = 128
VMEM_LIMIT = 56 * 1024 * 1024


def _cp(*sem):
    return pltpu.CompilerParams(dimension_semantics=sem, vmem_limit_bytes=VMEM_LIMIT)


def _offsets(sizes):
    return [int(s) for s in np.cumsum(sizes)[:-1]]


def _pad_to(a, n, axis):
    pad = [(0, 0)] * a.ndim
    pad[axis] = (0, n - a.shape[axis])
    return jnp.pad(a, pad)


def _block_ones(n, blk, dtype):
    r = lax.broadcasted_iota(jnp.int32, (n, n), 0) // blk
    c = lax.broadcasted_iota(jnp.int32, (n, n), 1) // blk
    return (r == c).astype(dtype)


def _dot2(x, ones_bf16):
    hi = x.astype(BF16)
    lo = (x - hi.astype(F32)).astype(BF16)
    return (jnp.dot(hi, ones_bf16, preferred_element_type=F32)
            + jnp.dot(lo, ones_bf16, preferred_element_type=F32))


def _rope_tables(pos, rot_dim, theta, period, reps):
    half = rot_dim // 2
    inv = theta ** (-jnp.arange(half, dtype=F32) / half)
    ang = jnp.asarray(pos, F32)[:, None] * inv[None, :]
    cos, sin = jnp.cos(ang), jnp.sin(ang)
    n = ang.shape[0]
    rest = period - rot_dim
    c = jnp.concatenate([cos, cos, jnp.ones((n, rest), F32)], -1)
    s = jnp.concatenate([-sin, sin, jnp.zeros((n, rest), F32)], -1)
    return jnp.tile(c, (1, reps)), jnp.tile(s, (1, reps))


PRECISE_COLS = 384


def _proj_body(x_ref, nw_ref, w_ref, wlo_ref, cos_ref, sin_ref, q_ref, kv_ref, g_ref, cb_ref, u_ref, cd_ref):
    x = x_ref[...]
    h = x * lax.rsqrt(jnp.mean(x * x, axis=-1, keepdims=True) + RMS_EPS) * nw_ref[...]
    hb = h.astype(BF16)
    h_lo = (h - hb.astype(F32)).astype(BF16)
    c = cos_ref[...]
    s = sin_ref[...]
    first = (lax.broadcasted_iota(jnp.int32, c.shape, 1) % HEAD_DIM) < (ROT_DIM // 2)

    def rope(z):
        sw = jnp.where(first, pltpu.roll(z, LANE - ROT_DIM // 2, 1), pltpu.roll(z, ROT_DIM // 2, 1))
        return z * c + sw * s

    def dot(a, b):
        z = jnp.dot(hb, w_ref[:, a:b], preferred_element_type=F32)
        if b <= PRECISE_COLS:
            z = z + (jnp.dot(hb, wlo_ref[:, a:b], preferred_element_type=F32)
                     + jnp.dot(h_lo, w_ref[:, a:b], preferred_element_type=F32))
        return z

    for j in range(2):
        q_ref[:, LANE * j:LANE * (j + 1)] = rope(dot(LANE * j, LANE * (j + 1)))
    for j in range(6):
        z = dot(256 + LANE * j, 256 + LANE * (j + 1))
        kv_ref[:, LANE * j:LANE * (j + 1)] = rope(z) if j % 2 == 0 else z
    g_ref[...] = dot(1024, 1152)
    cb_ref[...] = dot(1152, 2048)
    u_ref[...] = dot(2048, 2304)
    cd_ref[...] = dot(2304, 3328)


def _proj(x2d, norm_w, w_all, w_lo, cos_t, sin_t, tm):
    n = x2d.shape[0]
    t_tiles = cos_t.shape[0] // tm
    row = lambda i: (i, 0)
    fixed = lambda i: (0, 0)
    tab = lambda i: (i % t_tiles, 0)
    widths = (256, 768, 128, SHIFT_PAD, 256, 1024)
    return pl.pallas_call(
        _proj_body,
        grid=(n // tm,),
        in_specs=[pl.BlockSpec((tm, D_MODEL), row), pl.BlockSpec((1, D_MODEL), fixed),
                  pl.BlockSpec((D_MODEL, 3328), fixed), pl.BlockSpec((D_MODEL, PRECISE_COLS), fixed),
                  pl.BlockSpec((tm, LANE), tab), pl.BlockSpec((tm, LANE), tab)],
        out_specs=[pl.BlockSpec((tm, w), row) for w in widths],
        out_shape=[jax.ShapeDtypeStruct((n, w), F32) for w in widths],
        compiler_params=_cp("parallel"),
        name="proj",
    )(x2d, norm_w.reshape(1, D_MODEL), w_all, w_lo, cos_t, sin_t)


def _cmp_mlp(xc, kind, w1_ref, b1_ref, w2_ref, w1k_ref, w2k_ref, nck):
    if kind == 0:
        hh = _dot3(xc, w1k_ref[...])
    else:
        hh = jnp.dot(xc.astype(BF16), w1_ref[kind], preferred_element_type=F32)
    hs = []
    for g in range(N_KV_A):
        hg = hh[g * nck:(g + 1) * nck]
        hs.append(jax.nn.gelu(b1_ref[kind] + hg[:, :CMP_HIDDEN] + pltpu.roll(hg[:, CMP_HIDDEN:], nck - 1, 0)))
    act = jnp.concatenate(hs, axis=1)
    if kind == 0:
        return _dot3(act, w2k_ref[...])
    return jnp.dot(act.astype(BF16), w2_ref[kind], preferred_element_type=F32)


def _cmp_body(xk_ref, xv_ref, w1_ref, b1_ref, w2_ref, w1k_ref, w2k_ref, kc_ref, vc_ref, vct_ref, xc_ref, *, n_chunks):
    lane = lax.broadcasted_iota(jnp.int32, (n_chunks, LANE), 1)
    lo = lane < HEAD_DIM
    for pair in range(CMP_STRIDE // 2):
        for kind, x_ref in enumerate((xk_ref, xv_ref)):
            ak = x_ref[0, pl.ds(2 * pair, n_chunks, stride=CMP_STRIDE), :]
            bk = x_ref[0, pl.ds(2 * pair + 1, n_chunks, stride=CMP_STRIDE), :]
            xc_ref[kind, 0:n_chunks, LANE * pair:LANE * (pair + 1)] = jnp.where(lo, ak, pltpu.roll(bk, HEAD_DIM, 1))
            xc_ref[kind, n_chunks:2 * n_chunks, LANE * pair:LANE * (pair + 1)] = jnp.where(lo, pltpu.roll(ak, HEAD_DIM, 1), bk)
    outs = [_cmp_mlp(xc_ref[kind], kind, w1_ref, b1_ref, w2_ref, w1k_ref, w2k_ref, n_chunks) for kind in range(2)]
    kc_ref[0] = outs[0]
    vc_ref[0] = outs[1]
    vct_ref[0] = outs[1].T


def _nsa_compress(rows, w1cat, b1, w2bd, w1k, w2k):
    b, tk = rows.shape[0], rows.shape[1]
    n_chunks = tk // CMP_STRIDE
    fixed3 = lambda i: (0, 0, 0)
    fixed2 = lambda i: (0, 0)
    return pl.pallas_call(
        functools.partial(_cmp_body, n_chunks=n_chunks),
        grid=(b,),
        in_specs=[pl.BlockSpec((1, tk, LANE), lambda i: (i, 0, 0)), pl.BlockSpec((1, tk, LANE), lambda i: (i, 0, 1)),
                  pl.BlockSpec((2, CMP_STRIDE * HEAD_DIM, 2 * CMP_HIDDEN), fixed3),
                  pl.BlockSpec((2, 1, CMP_HIDDEN), fixed3),
                  pl.BlockSpec((2, 2 * CMP_HIDDEN, LANE), fixed3),
                  pl.BlockSpec((CMP_STRIDE * HEAD_DIM, 2 * CMP_HIDDEN), fixed2), pl.BlockSpec((2 * CMP_HIDDEN, LANE), fixed2)],
        out_specs=[pl.BlockSpec((1, n_chunks, LANE), lambda i: (i, 0, 0)),
                   pl.BlockSpec((1, n_chunks, LANE), lambda i: (i, 0, 0)),
                   pl.BlockSpec((1, LANE, n_chunks), lambda i: (i, 0, 0))],
        out_shape=[jax.ShapeDtypeStruct((b, n_chunks, LANE), F32), jax.ShapeDtypeStruct((b, n_chunks, LANE), F32),
                   jax.ShapeDtypeStruct((b, LANE, n_chunks), F32)],
        scratch_shapes=[pltpu.VMEM((2, N_KV_A * n_chunks, CMP_STRIDE * HEAD_DIM), F32)],
        compiler_params=_cp("parallel"),
        name="nsa_compress",
    )(rows, rows, w1cat, b1, w2bd, w1k, w2k)


def _cmp_weights(cmp_w1, cmp_b1, cmp_w2):
    m = CMP_BLOCK // CMP_STRIDE
    w1r = cmp_w1.reshape(2, m, CMP_STRIDE * HEAD_DIM, CMP_HIDDEN)
    w1cat = jnp.concatenate([w1r[:, j] for j in range(m)], axis=-1)
    z = jnp.zeros_like(cmp_w2)
    w2bd = jnp.concatenate([jnp.concatenate([cmp_w2, z], -1), jnp.concatenate([z, cmp_w2], -1)], axis=1)
    return w1cat.astype(BF16), cmp_b1.reshape(2, 1, CMP_HIDDEN), w2bd.astype(BF16), w1cat[0], w2bd[0]


def _cmp_to_sel_t(n_chunks, n_cmp, n_sel):
    starts = np.arange(n_chunks) * CMP_STRIDE
    sel_s = np.arange(n_sel) * SEL_BLOCK
    ov = np.minimum(starts[:, None] + CMP_BLOCK, sel_s[None] + SEL_BLOCK) - np.maximum(starts[:, None], sel_s[None])
    ov = np.clip(ov, 0, None) / CMP_BLOCK
    ov[n_cmp:] = 0.0
    return jnp.asarray(ov.T, dtype=F32)


def _masked_softmax_cols(s, mask):
    m = jnp.max(jnp.where(mask, s, NEG_INF), axis=0, keepdims=True)
    e = jnp.where(mask, jnp.exp(s - m), 0.0)
    den = jnp.sum(e, axis=0, keepdims=True)
    return e * jnp.where(den > 0.0, 1.0 / den, 0.0)


def _nsa_prompt_body(qt_ref, gt_ref, kc_ref, vct_ref, ovt_ref, ks_ref, vst_ref, kw_ref, vwt_ref, o_ref, sel_ref,
                     *, n_cmp, n_sel, qb_size):
    qb = pl.program_id(1)
    tq = qb_size
    n_chunks = kc_ref.shape[1]
    qpos = qb * tq + lax.broadcasted_iota(jnp.int32, (1, tq), 1)
    qpos2 = jnp.concatenate([qpos, qpos], axis=1)
    zeros_q = jnp.zeros((HEAD_DIM, 2 * tq), F32)
    gates = jax.nn.sigmoid(gt_ref[0])
    kc = kc_ref[0]
    n_idx = lax.broadcasted_iota(jnp.int32, (n_chunks, 2 * tq), 0)
    cmp_mask = (n_idx * CMP_STRIDE + (CMP_BLOCK - 1) <= qpos2) & (n_idx < n_cmp)
    blk = lax.broadcasted_iota(jnp.int32, (n_sel, tq), 0)
    cur = qpos // SEL_BLOCK
    forced = (blk == 0) | (blk == cur) | (blk == cur - 1)
    causal_blk = blk * SEL_BLOCK <= qpos
    tk = 2 * tq
    kiota = lax.broadcasted_iota(jnp.int32, (tk, 2 * tq), 0)
    qpads, o_cmps = [], []

    for g in range(N_KV_A):
        q64 = jnp.concatenate([qt_ref[0, (2 * g) * HEAD_DIM:(2 * g + 1) * HEAD_DIM, :],
                               qt_ref[0, (2 * g + 1) * HEAD_DIM:(2 * g + 2) * HEAD_DIM, :]], axis=1) * (HEAD_DIM ** -0.5)
        qpad32 = jnp.concatenate([q64, zeros_q], axis=0) if g == 0 else jnp.concatenate([zeros_q, q64], axis=0)
        qpad = qpad32.astype(BF16)
        qpads.append(qpad)

        p = _masked_softmax_cols(_dot3(kc, qpad32), cmp_mask)
        o_cmps.append(jnp.dot(vct_ref[0, g * HEAD_DIM:(g + 1) * HEAD_DIM, :].astype(BF16), p.astype(BF16),
                              preferred_element_type=F32))
        psum = p[:, :tq] + p[:, tq:]
        imp = jnp.dot(ovt_ref[...], psum, precision=HI, preferred_element_type=F32)
        imp = jnp.where(forced, FORCED_SCORE, jnp.where(causal_blk, imp, BLOCKED_SCORE))
        rank = jnp.zeros((n_sel, tq), F32)
        for i in range(n_sel):
            row = imp[i:i + 1, :]
            rank = rank + jnp.where((row > imp) | ((row == imp) & (blk > i)), 1.0, 0.0)
        sel_ref[g] = jnp.where(rank < float(min(TOP_K, n_sel)), 1.0, 0.0)

    q_all = jnp.concatenate(qpads, axis=1)
    qpos4 = jnp.concatenate([qpos2, qpos2], axis=1)

    def attend(j, carry, k_ref, vt_ref, use_sel, tk):
        m, l, acc = carry
        off = pl.multiple_of(j * tk, tk)
        kt = k_ref[0, pl.ds(off, tk), :].astype(BF16)
        diff = qpos4 - (off + lax.broadcasted_iota(jnp.int32, (tk, 4 * tq), 0))
        s = jnp.dot(kt, q_all, preferred_element_type=F32)
        if use_sel:
            per_tile = tk // SEL_BLOCK
            sels = []
            for g in range(N_KV_A):
                rows = [jnp.broadcast_to(sel_ref[g, pl.ds(j * per_tile + a, 1), :], (SEL_BLOCK, tq)) for a in range(per_tile)]
                selm = jnp.concatenate(rows, axis=0)
                sels += [selm, selm]
            mask = (jnp.concatenate(sels, axis=1) > 0.0) & (diff >= 0)
        else:
            mask = (diff >= 0) & (diff < WINDOW)
        m_new = jnp.maximum(m, jnp.max(jnp.where(mask, s, NEG_INF), axis=0, keepdims=True))
        alpha = jnp.exp(m - m_new)
        e = jnp.where(mask, jnp.exp(s - m_new), 0.0)
        l_new = alpha * l + jnp.sum(e, axis=0, keepdims=True)
        vt = vt_ref[0, :, pl.ds(off, tk)].astype(BF16)
        return m_new, l_new, alpha * acc + jnp.dot(vt, e.astype(BF16), preferred_element_type=F32)

    init = (jnp.full((1, 4 * tq), NEG_INF, F32), jnp.zeros((1, 4 * tq), F32), jnp.zeros((2 * HEAD_DIM, 4 * tq), F32))
    tk_s, tk_w = 4 * tq, 2 * tq
    _, l_s, acc_s = lax.fori_loop(0, (qb * tq + tk_s) // tk_s,
                                  functools.partial(attend, k_ref=ks_ref, vt_ref=vst_ref, use_sel=True, tk=tk_s), init)
    _, l_w, acc_w = lax.fori_loop(jnp.maximum(qb * tq - WINDOW, 0) // tk_w, (qb * tq + tk_w) // tk_w,
                                  functools.partial(attend, k_ref=kw_ref, vt_ref=vwt_ref, use_sel=False, tk=tk_w), init)

    for g in range(N_KV_A):
        blk_g = (slice(g * HEAD_DIM, (g + 1) * HEAD_DIM), slice(g * 2 * tq, (g + 1) * 2 * tq))
        o_slc = acc_s[blk_g] / l_s[:, blk_g[1]]
        o_win = acc_w[blk_g] / l_w[:, blk_g[1]]
        for r in range(R_A):
            h = 2 * g + r
            gr = gates[3 * h:3 * h + 3, :]
            sl = slice(r * tq, (r + 1) * tq)
            o_ref[0, h * HEAD_DIM:(h + 1) * HEAD_DIM, :] = (gr[0:1] * o_cmps[g][:, sl] + gr[1:2] * o_slc[:, sl]
                                                          + gr[2:3] * o_win[:, sl])


def _nsa_prompt(qt, gt, kc, vct, ovt, kv, vst, vwt, n_cmp):
    b, _, t = qt.shape
    tq = 128
    n_sel = t // SEL_BLOCK
    n_chunks = kc.shape[1]
    per_b = lambda i, j: (i, 0, 0)
    return pl.pallas_call(
        functools.partial(_nsa_prompt_body, n_cmp=n_cmp, n_sel=n_sel, qb_size=tq),
        grid=(b, t // tq),
        in_specs=[pl.BlockSpec((1, 256, tq), lambda i, j: (i, 0, j)),
                  pl.BlockSpec((1, 16, tq), lambda i, j: (i, 0, j)),
                  pl.BlockSpec((1, n_chunks, LANE), per_b),
                  pl.BlockSpec((1, LANE, n_chunks), per_b),
                  pl.BlockSpec((n_sel, n_chunks), lambda i, j: (0, 0)),
                  pl.BlockSpec((1, t, LANE), lambda i, j: (i, 0, 2)),
                  pl.BlockSpec((1, LANE, t), per_b),
                  pl.BlockSpec((1, t, LANE), lambda i, j: (i, 0, 4)),
                  pl.BlockSpec((1, LANE, t), per_b)],
        out_specs=pl.BlockSpec((1, 256, tq), lambda i, j: (i, 0, j)),
        out_shape=jax.ShapeDtypeStruct((b, 256, t), F32),
        scratch_shapes=[pltpu.VMEM((N_KV_A, n_sel, tq), F32)],
        compiler_params=_cp("parallel", "arbitrary"),
        name="nsa_prompt",
    )(qt, gt, kc, vct, ovt, kv, vst, kv, vwt)


def _nsa_prompt_mixer(q, kv, gate, lw, b, t):
    kv3 = kv.reshape(b, t, 768)
    n_chunks = t // CMP_STRIDE
    n_cmp = (t - CMP_BLOCK) // CMP_STRIDE + 1
    kc, _, vct = _nsa_compress(kv3, *lw['cmp'])
    ovt = _cmp_to_sel_t(n_chunks, n_cmp, t // SEL_BLOCK)
    qt = jnp.swapaxes(q.reshape(b, t, 256), 1, 2)
    gt = jnp.swapaxes(gate.reshape(b, t, LANE)[:, :, :16], 1, 2)
    vst = jnp.swapaxes(kv3[:, :, 384:512], 1, 2)
    vwt = jnp.swapaxes(kv3[:, :, 640:768], 1, 2)
    ot = _nsa_prompt(qt, gt, kc, vct, ovt, kv3, vst, vwt, n_cmp)
    return jnp.swapaxes(ot, 1, 2).reshape(b * t, 256)


def _softmax_rows_with_extra(s, mask, s_new):
    m = jnp.maximum(jnp.max(jnp.where(mask, s, NEG_INF), axis=-1, keepdims=True), s_new)
    e = jnp.where(mask, jnp.exp(s - m), 0.0)
    e_new = jnp.exp(s_new - m)
    return e, e_new, 1.0 / (jnp.sum(e, axis=-1, keepdims=True) + e_new)


def _nsa_sample_body(pt_ref, *refs, n_pages, page, pos, n_sel, n_cmp, wb):
    del pt_ref
    n_in = n_pages
    pages = refs[:n_in]
    (qbd_ref, new_ref, gate_ref, win_ref, ov_ref, w1_ref, b1_ref, w2_ref, w1k_ref, w2k_ref,
     o_ref, nw_ref, tok_ref, xc_ref) = refs[n_in:]
    pg = lambda p, kind: pages[p].at[0, kind:kind + 1]
    nck = n_pages * page // CMP_STRIDE
    lane8 = lax.broadcasted_iota(jnp.int32, (8, LANE), 1)
    lo8 = lane8 < HEAD_DIM
    nt = lambda a, b: lax.dot_general(a, b, (((1,), (1,)), ((), ())), preferred_element_type=F32)

    lo_c = lax.broadcasted_iota(jnp.int32, (nck, LANE), 1) < HEAD_DIM
    for kind in range(2):
        for p in range(n_pages):
            tok_ref[kind, p * page:(p + 1) * page, :] = pg(p, kind)[0].T
        for pair in range(CMP_STRIDE // 2):
            a = tok_ref[kind, pl.ds(2 * pair, nck, stride=CMP_STRIDE), :]
            b = tok_ref[kind, pl.ds(2 * pair + 1, nck, stride=CMP_STRIDE), :]
            cols = slice(LANE * pair, LANE * (pair + 1))
            xc_ref[kind, 0:nck, cols] = jnp.where(lo_c, a, pltpu.roll(b, HEAD_DIM, 1))
            xc_ref[kind, nck:2 * nck, cols] = jnp.where(lo_c, pltpu.roll(a, HEAD_DIM, 1), b)
    kc, vc = [_cmp_mlp(xc_ref[kind], kind, w1_ref, b1_ref, w2_ref, w1k_ref, w2k_ref, nck) for kind in range(2)]

    q = qbd_ref[0] * (HEAD_DIM ** -0.5)
    qb = q.astype(BF16)
    new = new_ref[0]

    n_idx = lax.broadcasted_iota(jnp.int32, (8, nck), 1)
    cmask = (n_idx * CMP_STRIDE + (CMP_BLOCK - 1) <= pos) & (n_idx < n_cmp)
    s = _dot3(q, kc, _NT)
    m = jnp.max(jnp.where(cmask, s, NEG_INF), axis=-1, keepdims=True)
    e = jnp.where(cmask, jnp.exp(s - m), 0.0)
    den = jnp.sum(e, axis=-1, keepdims=True)
    p_cmp = e * jnp.where(den > 0.0, 1.0 / den, 0.0)
    o_cmp = jnp.dot(p_cmp.astype(BF16), vc.astype(BF16), preferred_element_type=F32)

    row8 = lax.broadcasted_iota(jnp.int32, (8, nck), 0)
    psum = jnp.where(row8 == 0, p_cmp[0:1] + p_cmp[1:2], jnp.where(row8 == 1, p_cmp[2:3] + p_cmp[3:4], 0.0))
    imp = jnp.dot(psum, ov_ref[...], precision=HI, preferred_element_type=F32)
    cur = pos // SEL_BLOCK
    forced = (lane8 == 0) | (lane8 == cur) | (lane8 == cur - 1)
    imp = jnp.where(forced, FORCED_SCORE, jnp.where(lane8 * SEL_BLOCK <= pos, imp, BLOCKED_SCORE))
    imp = jnp.where(lane8 < n_sel, imp, -3e38)
    rank = jnp.zeros((8, LANE), F32)
    for i in range(n_sel):
        col = imp[:, i:i + 1]
        rank = rank + jnp.where((col > imp) | ((col == imp) & (lane8 > i)), 1.0, 0.0)
    sel = jnp.where((rank < float(min(TOP_K, n_sel))) & (lane8 < n_sel), 1.0, 0.0)
    rsel = lax.broadcasted_iota(jnp.int32, (8, LANE), 0)
    selh = jnp.where(rsel < R_A, sel[0:1], jnp.where(rsel < 2 * R_A, sel[1:2], 0.0))

    per_page = page // SEL_BLOCK
    s_t, m_t = [], []
    for p in range(n_pages):
        s_t.append(jnp.dot(qb, pg(p, 2)[0].astype(BF16), preferred_element_type=F32))
        blk_sel = selh[:, per_page * p:per_page * p + 1]
        for a in range(1, per_page):
            blk_sel = jnp.where(lane8 < a * SEL_BLOCK, blk_sel, selh[:, per_page * p + a:per_page * p + a + 1])
        kpos = p * page + lane8
        m_t.append((blk_sel > 0.0) & (kpos <= pos))
    s_all = jnp.concatenate(s_t, axis=1)
    mk_all = jnp.concatenate(m_t, axis=1)
    s_new = jnp.sum(q * new[2:3], axis=-1, keepdims=True)
    e, e_new, inv = _softmax_rows_with_extra(s_all, mk_all, s_new)
    acc = e_new * new[3:4]
    for p in range(n_pages):
        acc = acc + nt(e[:, p * page:(p + 1) * page].astype(BF16), pg(p, 3)[0].astype(BF16))
    o_slc = acc * inv

    kw, vw = win_ref[0], win_ref[1]
    widx = lax.broadcasted_iota(jnp.int32, (8, wb), 1)
    diff = wb - widx
    s_w = jnp.dot(qb, kw.astype(BF16), preferred_element_type=F32)
    s_wnew = jnp.sum(q * new[4:5], axis=-1, keepdims=True)
    e, e_new, inv = _softmax_rows_with_extra(s_w, (diff >= 0) & (diff < WINDOW), s_wnew)
    o_win = (nt(e.astype(BF16), vw.astype(BF16)) + e_new * new[5:6]) * inv

    gts = jax.nn.sigmoid(gate_ref[0])
    o = gts[:, 0:1] * o_cmp + gts[:, 1:2] * o_slc + gts[:, 2:3] * o_win
    lo1 = lo8[0:1]
    o_ref[0] = jnp.concatenate([jnp.where(lo1, o[0:1], pltpu.roll(o[1:2], HEAD_DIM, 1)),
                                jnp.where(lo1, pltpu.roll(o[2:3], HEAD_DIM, 1), o[3:4])], axis=1)
    last = lax.broadcasted_iota(jnp.int32, (LANE, wb), 1) == wb - 1
    new_t = new.T
    nw_ref[0] = jnp.where(last, new_t[:, 4:5], pltpu.roll(kw, wb - 1, 1))
    nw_ref[1] = jnp.where(last, new_t[:, 5:6], pltpu.roll(vw, wb - 1, 1))


def _nsa_sample(q, kv, gate, cache_kv, cache_win, layer, page_table, cmp_w, pos):
    b = q.shape[0]
    n_pool, page = cache_kv.shape[1:3]
    n_pages = page_table.shape[1]
    wb = cache_win.shape[2]
    assert wb == WINDOW and page % CMP_STRIDE == 0 and page % SEL_BLOCK == 0 and page == LANE
    tk = n_pages * page + 1
    n_cmp = (tk - CMP_BLOCK) // CMP_STRIDE + 1
    nck = n_pages * page // CMP_STRIDE
    assert n_cmp <= nck
    n_sel = -(-tk // SEL_BLOCK)
    assert n_sel <= LANE and pos // SEL_BLOCK == n_sel - 1
    ov = _pad_to(_cmp_to_sel_t(nck, n_cmp, n_sel).T, LANE, 1)
    w1cat, b1, w2bd, w1k, w2k = cmp_w
    q4 = q.reshape(b, H_A, HEAD_DIM)
    z = jnp.zeros_like(q4)
    first = (jnp.arange(H_A) // R_A == 0)[None, :, None]
    qbd = jnp.concatenate([jnp.where(first, q4, z), jnp.where(first, z, q4)], axis=-1)
    qbd = _pad_to(qbd, 8, 1)
    new = _pad_to(kv.reshape(b, 6, LANE), 8, 1)
    g8 = _pad_to(_pad_to(gate[:, :3 * H_A].reshape(b, H_A, 3), LANE, 2), 8, 1)
    cache3 = jnp.transpose(cache_kv, (0, 1, 3, 4, 5, 2)).reshape(-1, 4, LANE, page)
    win3 = jnp.transpose(cache_win, (0, 1, 3, 4, 5, 2)).reshape(-1, LANE, wb)
    page_specs = [pl.BlockSpec((1, 4, LANE, page), functools.partial(lambda i, pt, p: (layer * n_pool + pt[i, p], 0, 0, 0), p=p))
                  for p in range(n_pages)]
    per_b = lambda i, pt: (i, 0, 0)
    fixed2 = lambda i, pt: (0, 0)
    fixed3 = lambda i, pt: (0, 0, 0)
    grid_spec = pltpu.PrefetchScalarGridSpec(
        num_scalar_prefetch=1,
        grid=(b,),
        in_specs=page_specs + [pl.BlockSpec((1, 8, LANE), per_b), pl.BlockSpec((1, 8, LANE), per_b), pl.BlockSpec((1, 8, LANE), per_b),
                               pl.BlockSpec((2, LANE, wb), lambda i, pt: (layer * b + i, 0, 0)), pl.BlockSpec((nck, LANE), fixed2),
                               pl.BlockSpec((2, CMP_STRIDE * HEAD_DIM, 2 * CMP_HIDDEN), fixed3),
                               pl.BlockSpec((2, 1, CMP_HIDDEN), fixed3), pl.BlockSpec((2, 2 * CMP_HIDDEN, LANE), fixed3),
                               pl.BlockSpec((CMP_STRIDE * HEAD_DIM, 2 * CMP_HIDDEN), fixed2),
                               pl.BlockSpec((2 * CMP_HIDDEN, LANE), fixed2)],
        out_specs=[pl.BlockSpec((1, 1, 256), per_b), pl.BlockSpec((2, LANE, wb), per_b)],
        scratch_shapes=[pltpu.VMEM((2, n_pages * page, LANE), F32), pltpu.VMEM((2, 2 * nck, CMP_STRIDE * HEAD_DIM), F32)],
    )
    o, nw = pl.pallas_call(
        functools.partial(_nsa_sample_body, n_pages=n_pages, page=page, pos=pos, n_sel=n_sel, n_cmp=n_cmp, wb=wb),
        grid_spec=grid_spec,
        out_shape=[jax.ShapeDtypeStruct((b, 1, 256), F32), jax.ShapeDtypeStruct((2 * b, LANE, wb), F32)],
        compiler_params=_cp("parallel"),
        name="nsa_sample",
    )(page_table, *([cache3] * n_pages), qbd, new, g8, win3, ov, w1cat, b1, w2bd, w1k, w2k)
    return o.reshape(b, 256), jnp.transpose(nw.reshape(b, 2, N_KV_A, HEAD_DIM, wb), (0, 4, 1, 2, 3))


def _rwkv_prep_body(c_ref, s0_ref, mu_ref, vec_ref, wup_ref, aup_ref, gup_ref,
                    r_ref, lw_ref, k_ref, v_ref, kk_ref, ka_ref, g_ref, bonus_ref, carry_ref, *, tiles_per_seq):
    i = pl.program_id(0)
    cols = c_ref[...]
    tm = cols.shape[0]

    @pl.when(i % tiles_per_seq == 0)
    def _():
        carry_ref[...] = s0_ref[0]

    prev = pltpu.roll(cols, 1, 0)
    row0 = lax.broadcasted_iota(jnp.int32, cols.shape, 0) == 0
    prev = jnp.where(row0, carry_ref[...], prev)
    carry_ref[...] = cols[tm - 1:tm, :]
    _rwkv_features(cols, prev, mu_ref, vec_ref, wup_ref, aup_ref, gup_ref,
                   r_ref, lw_ref, k_ref, v_ref, kk_ref, ka_ref, g_ref, bonus_ref)


def _rwkv_step_prep_body(c_ref, s0_ref, mu_ref, vec_ref, wup_ref, aup_ref, gup_ref,
                         r_ref, lw_ref, k_ref, v_ref, kk_ref, ka_ref, g_ref, bonus_ref):
    _rwkv_features(c_ref[...], s0_ref[...], mu_ref, vec_ref, wup_ref, aup_ref, gup_ref,
                   r_ref, lw_ref, k_ref, v_ref, kk_ref, ka_ref, g_ref, bonus_ref)


def _rwkv_features(cols, prev, mu_ref, vec_ref, wup_ref, aup_ref, gup_ref,
                   r_ref, lw_ref, k_ref, v_ref, kk_ref, ka_ref, g_ref, bonus_ref):
    xs = cols + mu_ref[...] * (prev - cols)
    r, k, v, lo = xs[:, 0:256], xs[:, 256:512], xs[:, 512:768], xs[:, 768:896]
    vec = vec_ref[...]
    w0, a0, k_k, k_a, r_k = vec[0:1], vec[1:2], vec[2:3], vec[3:4], vec[4:5]
    dot_hi = lambda x, w: jnp.dot(x, w, precision=HI, preferred_element_type=F32)
    w_log = -jax.nn.softplus(-(w0 + dot_hi(jnp.tanh(lo), wup_ref[...]))) - 0.5
    a = jax.nn.sigmoid(a0 + dot_hi(lo, aup_ref[...]))
    g_ref[...] = dot_hi(jax.nn.sigmoid(lo), gup_ref[...])
    ones = _block_ones(256, HEAD_DIM, F32)
    kk = k * k_k
    kk = kk * lax.rsqrt(dot_hi(kk * kk, ones) + 1e-12)
    k2 = k * (1.0 + (a - 1.0) * k_a)
    r_ref[...] = r
    lw_ref[...] = -jnp.exp(w_log)
    k_ref[...] = k2
    v_ref[...] = v
    kk_ref[...] = kk
    ka_ref[...] = kk * a
    bonus_ref[...] = dot_hi(r * k2 * r_k, ones) * v


def _rwkv_prep(colsb, shift0, lw, t, tm):
    n = colsb.shape[0]
    tiles_per_seq = t // tm
    row = lambda i: (i, 0)
    fixed = lambda i: (0, 0)
    outs = [jax.ShapeDtypeStruct((n, 256), F32)] * 8
    return pl.pallas_call(
        functools.partial(_rwkv_prep_body, tiles_per_seq=tiles_per_seq),
        grid=(n // tm,),
        in_specs=[pl.BlockSpec((tm, SHIFT_PAD), row),
                  pl.BlockSpec((1, 1, SHIFT_PAD), lambda i: (i // tiles_per_seq, 0, 0)),
                  pl.BlockSpec((1, SHIFT_PAD), fixed), pl.BlockSpec((8, 256), fixed),
                  pl.BlockSpec((LANE, 256), fixed), pl.BlockSpec((LANE, 256), fixed), pl.BlockSpec((LANE, 256), fixed)],
        out_specs=[pl.BlockSpec((tm, 256), row)] * 8,
        out_shape=outs,
        scratch_shapes=[pltpu.VMEM((1, SHIFT_PAD), F32)],
        compiler_params=_cp("arbitrary"),
        name="rwkv_prep",
    )(colsb, shift0.reshape(-1, 1, SHIFT_PAD), lw['rwkv_mu'], lw['rwkv_vec'], lw['rwkv_wup'], lw['rwkv_aup'], lw['rwkv_gup'])


def _rwkv_scan_body(r_ref, lw_ref, k_ref, v_ref, kk_ref, ka_ref, g_ref, bonus_ref, s0_ref, vec_ref,
                    o_ref, st_ref, s_scr, y_scr, *, nb, tl):
    @pl.when(pl.program_id(1) == 0)
    def _():
        s_scr[...] = s0_ref[...]

    ones = _block_ones(256, HEAD_DIM, BF16)
    isub = lax.broadcasted_iota(jnp.int32, (HEAD_DIM, 256), 0)
    ilane = lax.broadcasted_iota(jnp.int32, (HEAD_DIM, 256), 1) % HEAD_DIM
    msel = (isub == ilane).astype(F32)

    nr = nb * HEAD_DIM

    def step(t, carry):
        tp = jnp.maximum(t - 1, 0)
        states = [s_scr[b] for b in range(nb)]
        t1 = jnp.concatenate([states[b] * kk_ref[b, pl.ds(t, 1), :] for b in range(nb)], axis=0)
        t1h = t1.astype(BF16)
        t1l = (t1 - t1h.astype(F32)).astype(BF16)
        t2h, t2l = [], []
        for b in range(nb):
            v = v_ref[b, pl.ds(t, 1), :]
            vh = v.astype(BF16).astype(F32)
            t2h.append((msel * vh).astype(BF16))
            t2l.append((msel * (v - vh)).astype(BF16))
        t3 = [(states[b] * r_ref[b, pl.ds(tp, 1), :]).astype(BF16) for b in range(nb)]
        res_h = jnp.dot(jnp.concatenate([t1h] + t2h + t3, axis=0), ones, preferred_element_type=F32)
        res_l = jnp.dot(jnp.concatenate([t1l] + t2l, axis=0), ones, preferred_element_type=F32)
        for b in range(nb):
            rows = slice(b * HEAD_DIM, (b + 1) * HEAD_DIM)
            sk = res_h[rows] + res_l[rows]
            vrep = res_h[nr:2 * nr][rows] + res_l[nr:2 * nr][rows]
            s_scr[b] = (states[b] * jnp.exp(lw_ref[b, pl.ds(t, 1), :]) - sk * ka_ref[b, pl.ds(t, 1), :]
                        + vrep * k_ref[b, pl.ds(t, 1), :])
            y_scr[b, pl.ds(tp, 1), :] = jnp.sum(res_h[2 * nr:][rows] * msel, axis=0, keepdims=True)
        return carry

    lax.fori_loop(0, tl, step, 0)
    for b in range(nb):
        yrep = jnp.dot((s_scr[b] * r_ref[b, tl - 1:tl, :]).astype(BF16), ones, preferred_element_type=F32)
        y_scr[b, tl - 1:tl, :] = jnp.sum(yrep * msel, axis=0, keepdims=True)
    st_ref[...] = s_scr[...]
    vec = vec_ref[...]
    lnx_w, lnx_b = vec[5:6], vec[6:7]
    avg = _block_ones(256, HEAD_DIM, F32) * (1.0 / HEAD_DIM)
    for b in range(nb):
        y = y_scr[b]
        yc = y - jnp.dot(y, avg, precision=HI, preferred_element_type=F32)
        yn = yc * lax.rsqrt(jnp.dot(yc * yc, avg, precision=HI, preferred_element_type=F32) + RWKV_GN_EPS)
        o_ref[b] = (yn * lnx_w + lnx_b + bonus_ref[b]) * g_ref[b]


def _rwkv_scan(prep, s0, vec, b, t, nb, tl):
    r, lw, k, v, kk, ka, g, bonus = [a.reshape(b, t, 256) for a in prep]
    seq = pl.BlockSpec((nb, tl, 256), lambda i, j: (i, j, 0))
    st = pl.BlockSpec((nb, HEAD_DIM, 256), lambda i, j: (i, 0, 0))
    return pl.pallas_call(
        functools.partial(_rwkv_scan_body, nb=nb, tl=tl),
        grid=(b // nb, t // tl),
        in_specs=[seq] * 8 + [st, pl.BlockSpec((8, 256), lambda i, j: (0, 0))],
        out_specs=[seq, st],
        out_shape=[jax.ShapeDtypeStruct((b, t, 256), F32), jax.ShapeDtypeStruct((b, HEAD_DIM, 256), F32)],
        scratch_shapes=[pltpu.VMEM((nb, HEAD_DIM, 256), F32), pltpu.VMEM((nb, tl, 256), F32)],
        compiler_params=_cp("parallel", "arbitrary"),
        name="rwkv_scan",
    )(r, lw, k, v, kk, ka, g, bonus, s0, vec)


RWKV_CHUNK = 64
_PRE_TILES = 9


def _split_bf16(x):
    hi = x.astype(BF16)
    return hi, (x - hi.astype(F32)).astype(BF16)


def _dot3(a, b, dims=(((1,), (0,)), ((), ()))):
    ah, al = _split_bf16(a)
    bh, bl = _split_bf16(b)
    dg = lambda x, y: lax.dot_general(x, y, dims, preferred_element_type=F32)
    return dg(ah, bh) + dg(ah, bl) + dg(al, bh)


_NT = (((1,), (1,)), ((), ()))


def _rwkv_chunk_body(r_ref, lw_ref, k_ref, v_ref, kk_ref, ka_ref, g_ref, bonus_ref, s0_ref, vec_ref,
                     o_ref, st_ref, s_scr, y_scr, *, nb, tl):
    L = RWKV_CHUNK
    nc = tl // L

    @pl.when(pl.program_id(1) == 0)
    def _():
        s_scr[...] = s0_ref[...]

    ri = lax.broadcasted_iota(jnp.int32, (L, L), 0)
    ci = lax.broadcasted_iota(jnp.int32, (L, L), 1)
    strict, incl = ri > ci, ri >= ci
    ltri = incl.astype(F32)
    eye = (ri == ci).astype(F32)

    bnn = (((2,), (1,)), ((0,), (0,)))
    bnt = (((2,), (2,)), ((0,), (0,)))

    def chunk(c, carry):
        rows = pl.ds(pl.multiple_of(c * L, L), L)
        lhs_l, rhs_l, v_l, kw_l, wl_l = [], [], [], [], []
        for b in range(nb):
            r, lw, k, v, kk, ka = [ref[b, rows, :] for ref in (r_ref, lw_ref, k_ref, v_ref, kk_ref, ka_ref)]
            cl = jnp.dot(ltri, lw, precision=HI, preferred_element_type=F32)
            e_neg = jnp.exp(-cl)
            e_rem = jnp.exp(cl[L - 1:L, :] - cl)
            kkd, rd = kk * jnp.exp(cl - lw), r * jnp.exp(cl)
            kinv, kainv, kw, kaw = k * e_neg, ka * e_neg, k * e_rem, ka * e_rem
            w_last = jnp.exp(cl[L - 1:L, :])
            for h in range(H_B):
                sl = slice(h * HEAD_DIM, (h + 1) * HEAD_DIM)
                lhs_l.append(jnp.concatenate([kkd[:, sl], rd[:, sl]], axis=0))
                rhs_l.append(jnp.concatenate([kinv[:, sl], kainv[:, sl]], axis=0))
                v_l.append(v[:, sl])
                kw_l.append(jnp.concatenate([kw[:, sl], kaw[:, sl]], axis=0))
                wl_l.append(w_last[:, sl])
        lhs, rhs, vs, kws, wl = [jnp.stack(x) for x in (lhs_l, rhs_l, v_l, kw_l, wl_l)]
        gm = _dot3(lhs, rhs, bnt)
        a_vk = jnp.where(strict, gm[:, :L, :L], 0.0)
        n1 = jnp.where(strict, -gm[:, :L, L:], 0.0)
        t_inv, pw = eye + n1, n1
        for _ in range(5):
            pw = _dot3(pw, pw, bnn)
            t_inv = _dot3(t_inv, eye + pw, bnn)
        s = s_scr[...].reshape(nb * H_B, HEAD_DIM, HEAD_DIM)
        xs = _dot3(lhs, s, bnt)
        u = _dot3(t_inv, xs[:, :L] + _dot3(a_vk, vs, bnn), bnn)
        b_vk = jnp.where(incl, gm[:, L:, :L], 0.0).astype(BF16)
        b_uk = jnp.where(incl, gm[:, L:, L:], 0.0).astype(BF16)
        y = (xs[:, L:] + lax.dot_general(b_vk, vs.astype(BF16), bnn, preferred_element_type=F32)
             - lax.dot_general(b_uk, u.astype(BF16), bnn, preferred_element_type=F32))
        vu_t = jnp.stack([jnp.concatenate([vs[n], -u[n]], axis=0).T for n in range(nb * H_B)])
        s_new = s * wl + _dot3(vu_t, kws, bnn)
        s_scr[...] = s_new.reshape(nb, H_B, HEAD_DIM, HEAD_DIM)
        for b in range(nb):
            for h in range(H_B):
                y_scr[b, rows, h * HEAD_DIM:(h + 1) * HEAD_DIM] = y[b * H_B + h]
        return carry

    lax.fori_loop(0, nc, chunk, 0)
    st_ref[...] = s_scr[...]
    vec = vec_ref[...]
    for b in range(nb):
        o_ref[b] = (_segment_norm(y_scr[b], RWKV_GN_EPS) * vec[5:6] + vec[6:7] + bonus_ref[b]) * g_ref[b]


def _rwkv_chunked(prep, s0, vec, b, t, nb, tl):
    arrs = [a.reshape(b, t, 256) for a in prep]
    seq = pl.BlockSpec((nb, tl, 256), lambda i, j: (i, j, 0))
    st = pl.BlockSpec((nb, H_B, HEAD_DIM, HEAD_DIM), lambda i, j: (i, 0, 0, 0))
    return pl.pallas_call(
        functools.partial(_rwkv_chunk_body, nb=nb, tl=tl),
        grid=(b // nb, t // tl),
        in_specs=[seq] * 8 + [st, pl.BlockSpec((8, 256), lambda i, j: (0, 0))],
        out_specs=[seq, st],
        out_shape=[jax.ShapeDtypeStruct((b, t, 256), F32), jax.ShapeDtypeStruct((b, H_B, HEAD_DIM, HEAD_DIM), F32)],
        scratch_shapes=[pltpu.VMEM((nb, H_B, HEAD_DIM, HEAD_DIM), F32), pltpu.VMEM((nb, tl, 256), F32)],
        compiler_params=_cp("parallel", "arbitrary"),
        name="rwkv_chunked",
    )(*arrs, s0, vec)


def _rwkv_mixer(colsb, shift0, s0, lw, b, t, tm, nb, tl):
    prep = _rwkv_prep(colsb, shift0, lw, t, tm)
    o, st = _rwkv_chunked(prep, s0, lw['rwkv_vec'], b, t, nb, tl)
    shift = colsb.reshape(b, t, SHIFT_PAD)[:, -1, :SHIFT_B]
    return o.reshape(b * t, 256), st, shift


def _segment_norm(y, eps):
    avg = _block_ones(256, HEAD_DIM, F32) * (1.0 / HEAD_DIM)
    yc = y - jnp.dot(y, avg, precision=HI, preferred_element_type=F32)
    return yc * lax.rsqrt(jnp.dot(yc * yc, avg, precision=HI, preferred_element_type=F32) + eps)


def _rwkv_step_body(r_ref, lw_ref, k_ref, v_ref, kk_ref, ka_ref, g_ref, bonus_ref, vec_ref, s_ref, o_ref, st_ref, ft_scr, y_scr):
    h = pl.program_id(0)

    @pl.when(h == 0)
    def _():
        for n, ref in enumerate((r_ref, lw_ref, k_ref, v_ref, kk_ref, ka_ref)):
            ft_scr[n] = ref[...].T

    base = pl.multiple_of(h * HEAD_DIM, HEAD_DIM)
    head = lambda n: ft_scr[n, pl.ds(base, HEAD_DIM), :]
    r_t, w_t, k_t, kk_t, ka_t = head(0), jnp.exp(head(1)), head(2), head(4), head(5)

    def body(i, carry):
        rows = pl.ds(pl.multiple_of(i * HEAD_DIM, HEAD_DIM), HEAD_DIM)
        s = s_ref[rows, :]
        sk = jnp.sum(s * kk_t, axis=0, keepdims=True)
        s = s * w_t - sk * ka_t + ft_scr[3, pl.ds(base + i, 1), :] * k_t
        st_ref[rows, :] = s
        y_scr[pl.ds(base + i, 1), :] = jnp.sum(s * r_t, axis=0, keepdims=True)
        return carry

    lax.fori_loop(0, HEAD_DIM, body, 0)

    @pl.when(h == H_B - 1)
    def _():
        vec = vec_ref[...]
        o_ref[...] = (_segment_norm(y_scr[...].T, RWKV_GN_EPS) * vec[5:6] + vec[6:7] + bonus_ref[...]) * g_ref[...]


def _rwkv_step(colsb, shift0, s_all, layer, lw):
    b = colsb.shape[0]
    hd2 = HEAD_DIM * HEAD_DIM
    full = lambda *_: (0, 0)
    feat = pl.BlockSpec((b, 256), full)
    prep = pl.pallas_call(
        _rwkv_step_prep_body,
        grid=(1,),
        in_specs=[pl.BlockSpec((b, SHIFT_PAD), full), pl.BlockSpec((b, SHIFT_PAD), full), pl.BlockSpec((1, SHIFT_PAD), full),
                  pl.BlockSpec((8, 256), full)] + [pl.BlockSpec((LANE, 256), full)] * 3,
        out_specs=[feat] * 8,
        out_shape=[jax.ShapeDtypeStruct((b, 256), F32)] * 8,
        compiler_params=_cp("arbitrary"),
        name="rwkv_step_prep",
    )(colsb, shift0, lw['rwkv_mu'], lw['rwkv_vec'], lw['rwkv_wup'], lw['rwkv_aup'], lw['rwkv_gup'])
    s_rows = jnp.transpose(s_all, (0, 2, 3, 4, 1)).reshape(-1, b)
    o, st = pl.pallas_call(
        _rwkv_step_body,
        grid=(H_B,),
        in_specs=[feat] * 8 + [pl.BlockSpec((8, 256), full), pl.BlockSpec((hd2, b), lambda h: (layer * H_B + h, 0))],
        out_specs=[feat, pl.BlockSpec((hd2, b), lambda h: (h, 0))],
        out_shape=[jax.ShapeDtypeStruct((b, 256), F32), jax.ShapeDtypeStruct((H_B * hd2, b), F32)],
        scratch_shapes=[pltpu.VMEM((6, 256, b), F32), pltpu.VMEM((256, b), F32)],
        compiler_params=_cp("arbitrary"),
        name="rwkv_step",
    )(*prep, lw['rwkv_vec'], s_rows)
    return o, jnp.transpose(st.reshape(H_B, HEAD_DIM, HEAD_DIM, b), (3, 0, 1, 2))


def _ret_step_body(c_ref, cos_ref, sin_ref, gn_ref, r0_ref, o_ref, rt_ref, ft_scr, acc_scr):
    h = pl.program_id(0)

    @pl.when(h == 0)
    def _():
        x = c_ref[...]
        cs, sn = cos_ref[...], sin_ref[...]
        first = (lax.broadcasted_iota(jnp.int32, cs.shape, 1) % HEAD_DIM) < (HEAD_DIM // 2)

        def rope(z):
            sw = jnp.where(first, pltpu.roll(z, 256 - HEAD_DIM // 2, 1), pltpu.roll(z, HEAD_DIM // 2, 1))
            return z * cs + sw * sn

        ft_scr[0] = rope(x[:, 0:256]).T
        ft_scr[1] = (rope(x[:, 256:512]) * (HEAD_DIM ** -0.5)).T
        ft_scr[2] = x[:, 512:768].T

    base = pl.multiple_of(h * HEAD_DIM, HEAD_DIM)
    nseq = rt_ref.shape[1]
    gamma = jnp.exp(jnp.log1p(-jnp.exp2(-5.0 - jnp.full((1, nseq), h, jnp.int32).astype(F32))))
    q_t, k_t, v_t = [ft_scr[n, pl.ds(base, HEAD_DIM), :] for n in range(3)]
    qk = jnp.sum(q_t * k_t, axis=0, keepdims=True)

    def body(d, cross):
        rows = pl.ds(pl.multiple_of(d * HEAD_DIM, HEAD_DIM), HEAD_DIM)
        r_old = r0_ref[rows, :]
        rt_ref[rows, :] = gamma * r_old + ft_scr[1, pl.ds(base + d, 1), :] * v_t
        return cross + ft_scr[0, pl.ds(base + d, 1), :] * r_old

    cross = lax.fori_loop(0, HEAD_DIM, body, jnp.zeros((HEAD_DIM, nseq), F32))
    acc_scr[pl.ds(base, HEAD_DIM), :] = qk * v_t + gamma * cross

    @pl.when(h == H_D - 1)
    def _():
        o_ref[...] = jax.nn.silu(c_ref[:, 768:1024]) * (_segment_norm(acc_scr[...].T, RET_GN_EPS) * gn_ref[...])


def _ret_step(colsd, cos, sin, r_all, layer, gn):
    b = colsd.shape[0]
    hd2 = HEAD_DIM * HEAD_DIM
    full = lambda *_: (0, 0)
    r_rows = jnp.transpose(r_all, (0, 2, 3, 4, 1)).reshape(-1, b)
    o, rt = pl.pallas_call(
        _ret_step_body,
        grid=(H_D,),
        in_specs=[pl.BlockSpec((b, 1024), full), pl.BlockSpec((b, 256), full), pl.BlockSpec((b, 256), full),
                  pl.BlockSpec((1, 256), full), pl.BlockSpec((hd2, b), lambda h: (layer * H_D + h, 0))],
        out_specs=[pl.BlockSpec((b, 256), full), pl.BlockSpec((hd2, b), lambda h: (h, 0))],
        out_shape=[jax.ShapeDtypeStruct((b, 256), F32), jax.ShapeDtypeStruct((H_D * hd2, b), F32)],
        scratch_shapes=[pltpu.VMEM((3, 256, b), F32), pltpu.VMEM((256, b), F32)],
        compiler_params=_cp("arbitrary"),
        name="ret_step",
    )(colsd, cos, sin, gn.reshape(1, 256), r_rows)
    return o, jnp.transpose(rt.reshape(H_D, HEAD_DIM, HEAD_DIM, b), (3, 0, 1, 2))


def _s5_step_body(u_ref, x0_ref, a1_ref, a2_ref, bt_ref, ct_ref, d_ref, wg_ref, o_ref, xt_ref):
    u = u_ref[...]
    dot_hi = lambda a, b: jnp.dot(a, b, precision=HI, preferred_element_type=F32)
    x0 = x0_ref[...]
    n = x0.shape[0]
    even = (lax.broadcasted_iota(jnp.int32, x0.shape, 0) % 2) == 0
    partner = jnp.where(even, pltpu.roll(x0, n - 1, 0), pltpu.roll(x0, 1, 0))
    x = a1_ref[...] * x0 + a2_ref[...] * partner + dot_hi(bt_ref[...], u.T)
    xt_ref[...] = x
    y = dot_hi(ct_ref[...], x).T + d_ref[...] * u
    z = jax.nn.gelu(y)
    o_ref[...] = z * jax.nn.sigmoid(dot_hi(z, wg_ref[...]))


def _s5_step(u, x_all, layer, lw):
    b = u.shape[0]
    a1, a2, bt, ct, d_row = lw['s5_step']
    n = 2 * G_C * S5_P
    full = lambda *_: (0, 0)
    x_rows = jnp.transpose(x_all, (0, 2, 3, 4, 1)).reshape(-1, b)
    o, xt = pl.pallas_call(
        _s5_step_body,
        grid=(1,),
        in_specs=[pl.BlockSpec((b, C_C), full), pl.BlockSpec((n, b), lambda i: (layer, 0)), pl.BlockSpec((n, 1), full),
                  pl.BlockSpec((n, 1), full), pl.BlockSpec((n, C_C), full), pl.BlockSpec((C_C, n), full),
                  pl.BlockSpec((1, C_C), full), pl.BlockSpec((C_C, C_C), full)],
        out_specs=[pl.BlockSpec((b, C_C), full), pl.BlockSpec((n, b), full)],
        out_shape=[jax.ShapeDtypeStruct((b, C_C), F32), jax.ShapeDtypeStruct((n, b), F32)],
        compiler_params=_cp("arbitrary"),
        name="s5_step",
    )(u, x_rows, a1, a2, bt, ct, d_row, lw['s5_w_glu'])
    return o, jnp.transpose(xt.reshape(G_C, S5_P, 2, b), (3, 0, 1, 2))


def _s5_params(lw):
    lr, li = lw['s5_lambda_re'], lw['s5_lambda_im']
    dt = jnp.exp(lw['s5_log_step'])[:, None]
    mag = jnp.exp(lr * dt)
    ar, ai = mag * jnp.cos(li * dt), mag * jnp.sin(li * dt)
    nr, ni = ar - 1.0, ai
    den = lr * lr + li * li
    fr, fi = (nr * lr + ni * li) / den, (ni * lr - nr * li) / den
    b_re, b_im = lw['s5_b'][0], lw['s5_b'][1]
    bbr = fr[..., None] * b_re - fi[..., None] * b_im
    bbi = fr[..., None] * b_im + fi[..., None] * b_re
    eye = jnp.eye(G_C, dtype=F32)
    bd_in = lambda m: jnp.einsum('gpc,gh->gchp', m, eye).reshape(G_C * S5_CH, G_C * S5_P)
    bd_out = lambda m: jnp.einsum('gcp,gh->gphc', m, eye).reshape(G_C * S5_P, G_C * S5_CH)
    b_big = jnp.concatenate([bd_in(bbr), bd_in(bbi)], axis=1)
    c_big = jnp.concatenate([bd_out(lw['s5_c'][0]), -bd_out(lw['s5_c'][1])], axis=0)
    a_row = jnp.concatenate([ar.reshape(1, -1), ai.reshape(1, -1)], axis=1)
    d_row = lw['s5_d'].reshape(1, C_C)
    n = 2 * G_C * S5_P
    a1 = jnp.stack([ar, ar], axis=-1).reshape(n, 1)
    a2 = jnp.stack([-ai, ai], axis=-1).reshape(n, 1)
    bt = jnp.stack([jnp.einsum('gpc,gh->gphc', bbr, eye), jnp.einsum('gpc,gh->gphc', bbi, eye)], axis=2).reshape(n, G_C * S5_CH)
    ct = jnp.stack([jnp.einsum('gcp,gh->gchp', lw['s5_c'][0], eye), -jnp.einsum('gcp,gh->gchp', lw['s5_c'][1], eye)],
                   axis=-1).reshape(G_C * S5_CH, n)
    return (a_row, b_big, c_big, d_row), (a1, a2, bt, ct, d_row)


def _s5_body(u_ref, x0_ref, a_ref, b_ref, c_ref, d_ref, wg_ref, o_ref, xt_ref, x_scr, bu_scr, xs_scr, *, nb, tt, mm_dtype, prec):
    @pl.when(pl.program_id(0) == 0)
    def _():
        x_scr[...] = x0_ref[...]

    np_ = G_C * S5_P
    ncb = np_ // LANE
    for b in range(nb):
        bu = jnp.dot(u_ref[b].astype(mm_dtype), b_ref[...], precision=prec, preferred_element_type=F32)
        for cb in range(2 * ncb):
            bu_scr[cb, b * tt:(b + 1) * tt, :] = bu[:, cb * LANE:(cb + 1) * LANE]
    a = a_ref[...]

    def step(t, x):
        rows = pl.ds(t, nb, stride=tt)
        new = [None] * (2 * ncb)
        for cb in range(ncb):
            re, im = slice(cb * LANE, (cb + 1) * LANE), slice(np_ + cb * LANE, np_ + (cb + 1) * LANE)
            ar, ai, xr, xi = a[:, re], a[:, im], x[:, re], x[:, im]
            new[cb] = ar * xr - ai * xi + bu_scr[cb, rows, :]
            new[ncb + cb] = ar * xi + ai * xr + bu_scr[ncb + cb, rows, :]
            xs_scr[cb, rows, :] = new[cb]
            xs_scr[ncb + cb, rows, :] = new[ncb + cb]
        return jnp.concatenate(new, axis=1)

    x_last = lax.fori_loop(0, tt, step, x_scr[...], unroll=8)
    x_scr[...] = x_last
    xt_ref[...] = x_last
    for b in range(nb):
        u = u_ref[b]
        xs = jnp.concatenate([xs_scr[cb, b * tt:(b + 1) * tt, :] for cb in range(2 * ncb)], axis=1)
        y = jnp.dot(xs.astype(mm_dtype), c_ref[...], precision=prec, preferred_element_type=F32) + d_ref[...] * u
        z = jax.nn.gelu(y)
        o_ref[b] = z * jax.nn.sigmoid(jnp.dot(z.astype(mm_dtype), wg_ref[...], precision=prec, preferred_element_type=F32))


def _s5_mixer(u, x0, lw, b, t, tt, exact):
    a_row, b_big, c_big, d_row = lw['s5']
    mm_dtype = F32 if exact else BF16
    prec = HI if exact else None
    x0l = jnp.concatenate([x0[..., 0].reshape(b, -1), x0[..., 1].reshape(b, -1)], axis=1)
    np2 = 2 * G_C * S5_P
    fixed = lambda i: (0, 0)
    o, xt = pl.pallas_call(
        functools.partial(_s5_body, nb=b, tt=tt, mm_dtype=mm_dtype, prec=prec),
        grid=(t // tt,),
        in_specs=[pl.BlockSpec((b, tt, C_C), lambda i: (0, i, 0)), pl.BlockSpec((b, np2), fixed),
                  pl.BlockSpec((1, np2), fixed), pl.BlockSpec((C_C, np2), fixed), pl.BlockSpec((np2, C_C), fixed),
                  pl.BlockSpec((1, C_C), fixed), pl.BlockSpec((C_C, C_C), fixed)],
        out_specs=[pl.BlockSpec((b, tt, C_C), lambda i: (0, i, 0)), pl.BlockSpec((b, np2), fixed)],
        out_shape=[jax.ShapeDtypeStruct((b, t, C_C), F32), jax.ShapeDtypeStruct((b, np2), F32)],
        scratch_shapes=[pltpu.VMEM((b, np2), F32), pltpu.VMEM((np2 // LANE, b * tt, LANE), F32),
                        pltpu.VMEM((np2 // LANE, b * tt, LANE), F32)],
        compiler_params=_cp("arbitrary"),
        name="s5",
    )(u.reshape(b, t, C_C), x0l, a_row, b_big.astype(mm_dtype), c_big.astype(mm_dtype), d_row, lw['s5_w_glu'].astype(mm_dtype))
    xt = xt.reshape(b, 2, G_C, S5_P)
    return o.reshape(b * t, C_C), jnp.stack([xt[:, 0], xt[:, 1]], axis=-1)


def _ret_tables(pos, c):
    cos, sin = _rope_tables(pos, HEAD_DIM, RET_THETA, HEAD_DIM, H_D)
    log_g = jnp.log1p(-jnp.exp2(-5.0 - jnp.arange(H_D, dtype=F32)))
    i = jnp.arange(c, dtype=F32)
    diff = i[:, None] - i[None, :]
    dmat = jnp.where(diff >= 0, jnp.exp(jnp.maximum(diff, 0.0)[None] * log_g[:, None, None]), 0.0).reshape(H_D * c, c)
    q_dec = jnp.repeat(jnp.exp((i + 1.0)[None] * log_g[:, None]).T, HEAD_DIM, axis=1)
    k_dec = jnp.repeat(jnp.exp((c - 1.0 - i)[None] * log_g[:, None]).T, HEAD_DIM, axis=1)
    chunk_dec = jnp.repeat(jnp.exp(c * log_g), HEAD_DIM).reshape(256, 1)
    return cos, sin, dmat, q_dec, k_dec, chunk_dec


def _ret_body(c_ref, cos_ref, sin_ref, dmat_ref, qdec_ref, kdec_ref, cdec_ref, r0_ref, gn_ref, o_ref, rt_ref, r_scr, *, c):
    @pl.when(pl.program_id(1) == 0)
    def _():
        r_scr[...] = r0_ref[0]

    x = c_ref[0]
    q, k, v, g = x[:, 0:256], x[:, 256:512], x[:, 512:768], x[:, 768:1024]
    cs, sn = cos_ref[...], sin_ref[...]
    lane = lax.broadcasted_iota(jnp.int32, (c, 256), 1)
    first = (lane % HEAD_DIM) < (HEAD_DIM // 2)

    def rope(z):
        sw = jnp.where(first, pltpu.roll(z, 256 - HEAD_DIM // 2, 1), pltpu.roll(z, HEAD_DIM // 2, 1))
        return z * cs + sw * sn

    q = rope(q)
    k = rope(k) * (HEAD_DIM ** -0.5)
    head = lane // HEAD_DIM
    kb, vb = k.astype(BF16), v.astype(BF16)
    qstack = jnp.concatenate([jnp.where(head == h, q, 0.0) for h in range(H_D)], axis=0).astype(BF16)
    s = lax.dot_general(qstack, kb, (((1,), (1,)), ((), ())), preferred_element_type=F32) * dmat_ref[...]
    pv = jnp.dot(s.astype(BF16), vb, preferred_element_type=F32)
    inner = jnp.zeros((c, 256), F32)
    for h in range(H_D):
        inner = inner + jnp.where(head == h, pv[h * c:(h + 1) * c], 0.0)
    r_old = r_scr[...]
    cross = jnp.dot((q * qdec_ref[...]).astype(BF16), r_old.astype(BF16), preferred_element_type=F32)
    kv = lax.dot_general((k * kdec_ref[...]).astype(BF16), vb, (((0,), (0,)), ((), ())), preferred_element_type=F32)
    bd = _block_ones(256, HEAD_DIM, F32)
    r_new = cdec_ref[...] * r_old + kv * bd
    r_scr[...] = r_new
    rt_ref[0] = r_new
    o = inner + cross
    avg = bd * (1.0 / HEAD_DIM)
    oc = o - jnp.dot(o, avg, precision=HI, preferred_element_type=F32)
    on = oc * lax.rsqrt(jnp.dot(oc * oc, avg, precision=HI, preferred_element_type=F32) + RET_GN_EPS)
    o_ref[0] = jax.nn.silu(g) * (on * gn_ref[...])


def _ret_mixer(colsd, r0, lw, tabs, b, t):
    c = RET_CHUNK if t % RET_CHUNK == 0 else t
    cos, sin, dmat, q_dec, k_dec, chunk_dec = tabs
    eye = jnp.eye(H_D, dtype=F32)
    r0l = jnp.einsum('bhde,hg->bhdge', r0, eye).reshape(b, 256, 256)
    n_t = t // c
    fixed = lambda i, j: (0, 0)
    o, rt = pl.pallas_call(
        functools.partial(_ret_body, c=c),
        grid=(b, n_t),
        in_specs=[pl.BlockSpec((1, c, 1024), lambda i, j: (i, j, 0)),
                  pl.BlockSpec((c, 256), lambda i, j: (j, 0)), pl.BlockSpec((c, 256), lambda i, j: (j, 0)),
                  pl.BlockSpec((H_D * c, c), fixed), pl.BlockSpec((c, 256), fixed), pl.BlockSpec((c, 256), fixed),
                  pl.BlockSpec((256, 1), fixed), pl.BlockSpec((1, 256, 256), lambda i, j: (i, 0, 0)),
                  pl.BlockSpec((1, 256), fixed)],
        out_specs=[pl.BlockSpec((1, c, 256), lambda i, j: (i, j, 0)), pl.BlockSpec((1, 256, 256), lambda i, j: (i, 0, 0))],
        out_shape=[jax.ShapeDtypeStruct((b, t, 256), F32), jax.ShapeDtypeStruct((b, 256, 256), F32)],
        scratch_shapes=[pltpu.VMEM((256, 256), F32)],
        compiler_params=_cp("parallel", "arbitrary"),
        name="retention",
    )(colsd.reshape(b, t, 1024), cos, sin, dmat, q_dec, k_dec, chunk_dec, r0l, lw['ret_gn'].reshape(1, 256))
    rt = jnp.einsum('bhdge,hg->bhde', rt.reshape(b, H_D, HEAD_DIM, H_D, HEAD_DIM), eye)
    return o.reshape(b * t, 256), rt


def _partner(x, d, period):
    pos = lax.broadcasted_iota(jnp.int32, x.shape, 1) % period
    return jnp.where(pos + d < period, pltpu.roll(x, LANE - d, 1), pltpu.roll(x, period - d, 1))


def _out_body(x_ref, oa_ref, ob_ref, oc_ref, od_ref, w_ref, nw_ref, wr_ref, br_ref, x1_ref, h_ref, comb_ref):
    acc = x_ref[...]
    for i, ref in enumerate((oa_ref, ob_ref, oc_ref, od_ref)):
        acc = acc + jnp.dot(ref[...].astype(BF16), w_ref[256 * i:256 * (i + 1), :], preferred_element_type=F32)
    x1_ref[...] = acc
    h = acc * lax.rsqrt(jnp.mean(acc * acc, axis=-1, keepdims=True) + RMS_EPS) * nw_ref[...]
    h_ref[...] = h.astype(BF16)
    logits = jnp.dot(h, wr_ref[...], precision=HI, preferred_element_type=F32) + br_ref[...]
    le, lg = logits[:, :LANE], logits[:, LANE:]
    lane = lax.broadcasted_iota(jnp.int32, le.shape, 1)
    mg = jnp.max(lg, axis=-1, keepdims=True)
    eg = jnp.exp(lg - mg)
    pg = eg / (jnp.sum(eg, axis=-1, keepdims=True) * (1.0 / 32.0))
    gidx = (lane % N_EXPERTS) // EXP_PER_GROUP
    g_rank = jnp.zeros_like(pg)
    for d in range(1, N_GROUPS):
        other = pltpu.roll(pg, LANE - EXP_PER_GROUP * d, 1)
        wrapped = gidx + d >= N_GROUPS
        g_rank = g_rank + jnp.where((other > pg) | ((other == pg) & wrapped), 1.0, 0.0)
    kidx = lane % EXP_PER_GROUP
    others = [_partner(le, d, EXP_PER_GROUP) for d in range(1, EXP_PER_GROUP)]
    me = functools.reduce(jnp.maximum, others, le)
    ee = jnp.exp(le - me)
    se = ee
    for d in range(1, EXP_PER_GROUP):
        se = se + _partner(ee, d, EXP_PER_GROUP)
    pe = ee / se
    e_rank = jnp.zeros_like(pe)
    for d in range(1, EXP_PER_GROUP):
        other = _partner(pe, d, EXP_PER_GROUP)
        wrapped = kidx + d >= EXP_PER_GROUP
        e_rank = e_rank + jnp.where((other > pe) | ((other == pe) & wrapped), 1.0, 0.0)
    top = jnp.where(e_rank < 2.0, pe, 0.0)
    den = top
    for d in range(1, EXP_PER_GROUP):
        den = den + _partner(top, d, EXP_PER_GROUP)
    comb = jnp.where((g_rank < 1.0) & (lane < N_EXPERTS), pg * (top / den), 0.0)
    comb_ref[...] = comb


def _out_router(x, oa, ob, oc, od, lw, tm):
    n = x.shape[0]
    row = lambda i: (i, 0)
    fixed = lambda i: (0, 0)
    mix = pl.BlockSpec((tm, 256), row)
    return pl.pallas_call(
        _out_body,
        grid=(n // tm,),
        in_specs=[pl.BlockSpec((tm, D_MODEL), row), mix, mix, mix, mix,
                  pl.BlockSpec((D_MODEL, D_MODEL), fixed), pl.BlockSpec((1, D_MODEL), fixed),
                  pl.BlockSpec((D_MODEL, 2 * LANE), fixed), pl.BlockSpec((1, 2 * LANE), fixed)],
        out_specs=[pl.BlockSpec((tm, D_MODEL), row), pl.BlockSpec((tm, D_MODEL), row), pl.BlockSpec((tm, LANE), row)],
        out_shape=[jax.ShapeDtypeStruct((n, D_MODEL), F32), jax.ShapeDtypeStruct((n, D_MODEL), BF16),
                   jax.ShapeDtypeStruct((n, LANE), F32)],
        compiler_params=_cp("parallel"),
        name="out_router",
    )(x, oa, ob, oc, od, lw['w_out'], lw['norm_ffn'], lw['w_router'], lw['b_router'])


def _router_weights(w_grp, b_grp, w_exp, b_exp):
    we = jnp.transpose(w_exp, (1, 0, 2)).reshape(D_MODEL, N_EXPERTS)
    wg = jnp.repeat(w_grp, EXP_PER_GROUP, axis=1)
    reps = LANE // N_EXPERTS
    w = jnp.concatenate([jnp.tile(we, (1, reps)), jnp.tile(wg, (1, reps))], axis=1)
    b = jnp.concatenate([jnp.tile(b_exp.reshape(1, N_EXPERTS), (1, reps)),
                         jnp.tile(jnp.repeat(b_grp, EXP_PER_GROUP).reshape(1, N_EXPERTS), (1, reps))], axis=1)
    return w, b


def _moe_body(h_ref, comb_ref, x1_ref, wg_ref, wu_ref, wd_ref, nf_ref, *out_refs, final):
    acc_ref = out_refs[-1]
    e = pl.program_id(1)

    @pl.when(e == 0)
    def _():
        acc_ref[...] = x1_ref[...]

    h = h_ref[...]
    comb = comb_ref[...]
    lane = lax.broadcasted_iota(jnp.int32, comb.shape, 1)
    c = jnp.sum(jnp.where(lane == e, comb, 0.0), axis=-1, keepdims=True)
    hg = jnp.dot(h, wg_ref[0], preferred_element_type=F32)
    hu = jnp.dot(h, wu_ref[0], preferred_element_type=F32)
    act = (jax.nn.silu(hg) * hu * c).astype(BF16)
    acc_ref[...] += jnp.dot(act, wd_ref[0], preferred_element_type=F32)

    @pl.when(e == N_EXPERTS - 1)
    def _():
        x2 = acc_ref[...]
        if final:
            out_refs[0][...] = x2 * lax.rsqrt(jnp.mean(x2 * x2, axis=-1, keepdims=True) + RMS_EPS) * nf_ref[...]
        else:
            out_refs[0][...] = x2


def _moe(h, comb, x1, lw, norm_final, tm, final):
    n = x1.shape[0]
    row = lambda i, e: (i, 0)
    per_e = lambda i, e: (e, 0, 0)
    return pl.pallas_call(
        functools.partial(_moe_body, final=final),
        grid=(n // tm, N_EXPERTS),
        in_specs=[pl.BlockSpec((tm, D_MODEL), row), pl.BlockSpec((tm, LANE), row), pl.BlockSpec((tm, D_MODEL), row),
                  pl.BlockSpec((1, D_MODEL, D_EXPERT), per_e), pl.BlockSpec((1, D_MODEL, D_EXPERT), per_e),
                  pl.BlockSpec((1, D_EXPERT, D_MODEL), per_e), pl.BlockSpec((1, D_MODEL), lambda i, e: (0, 0))],
        out_specs=pl.BlockSpec((tm, D_MODEL), row),
        out_shape=jax.ShapeDtypeStruct((n, D_MODEL), F32),
        scratch_shapes=[pltpu.VMEM((tm, D_MODEL), F32)],
        compiler_params=_cp("parallel", "arbitrary"),
        name="moe",
    )(h, comb, x1, lw['moe_wg'], lw['moe_wu'], lw['moe_wd'], norm_final.reshape(1, D_MODEL))


def _head_norm(y, eps):
    yc = y - jnp.mean(y, -1, keepdims=True)
    return yc * lax.rsqrt(jnp.mean(yc * yc, -1, keepdims=True) + eps)


def _masked_softmax(s, mask):
    s = jnp.where(mask, s.astype(F32), NEG_INF)
    return jnp.where(mask, jax.nn.softmax(s, axis=-1), 0.0)


def _attend(q, k, v, mask):
    s = jnp.einsum('...qgrd,...kgd->...qgrk', q, k) * HEAD_DIM ** -0.5
    p = _masked_softmax(s, mask)
    return jnp.einsum('...qgrk,...kgd->...qgrd', p, v.astype(F32))


def _x_nsa_compress(rows, w1, b1, w2):
    B, Tk = rows.shape[:2]
    m = CMP_BLOCK // CMP_STRIDE
    n_cmp = (Tk - CMP_BLOCK) // CMP_STRIDE + 1
    n_chunks = n_cmp + m - 1
    chunks = rows[:, :n_chunks * CMP_STRIDE].reshape(B, n_chunks, CMP_STRIDE, N_KV_A, HEAD_DIM)
    w1r = w1.reshape(m, CMP_STRIDE, HEAD_DIM, CMP_HIDDEN)
    h = b1.astype(F32)
    for j in range(m):
        h = h + jnp.einsum('bcsgd,sdf->bcgf', chunks[:, j:j + n_cmp], w1r[j])
    return jnp.einsum('bcgf,fd->bcgd', jax.nn.gelu(h), w2)


def _x_nsa_sample(q, kv, gate, pos, cmp_w1, cmp_b1, cmp_w2, past_rows, win_buf):
    B = q.shape[0]
    T = 1
    q = q.reshape(B, T, N_KV_A, R_A, HEAD_DIM)
    kv = kv.reshape(B, T, 6, N_KV_A, HEAD_DIM)
    rows = kv[:, :, 0:4]
    win_rows = kv[:, :, 4:6]
    full = jnp.concatenate([past_rows, rows], axis=1)
    Tk = full.shape[1]
    kc = _x_nsa_compress(full[:, :, 0], cmp_w1[0], cmp_b1[0], cmp_w2[0])
    vc = _x_nsa_compress(full[:, :, 1], cmp_w1[1], cmp_b1[1], cmp_w2[1])
    n_sel = -(-Tk // SEL_BLOCK)
    sel = jnp.pad(full[:, :, 2:4], ((0, 0), (0, n_sel * SEL_BLOCK - Tk), (0, 0), (0, 0), (0, 0)))
    sel = sel.reshape(B, n_sel, SEL_BLOCK, 2, N_KV_A, HEAD_DIM).transpose(3, 0, 4, 1, 2, 5)
    n_cmp = kc.shape[1]
    ov = _cmp_to_sel_t(n_cmp, n_cmp, n_sel).T
    q_pos = jnp.asarray(pos, jnp.int32)
    ks_blk, vs_blk = sel[0], sel[1]
    cmp_end = jnp.arange(n_cmp) * CMP_STRIDE + CMP_BLOCK - 1
    s = jnp.einsum('bqgrd,bngd->bqgrn', q, kc) * HEAD_DIM ** -0.5
    p_cmp = _masked_softmax(s, (cmp_end[None, :] <= q_pos[:, None])[None, :, None, None, :])
    o_c = jnp.einsum('bqgrn,bngd->bqgrd', p_cmp, vc.astype(F32))
    imp = jnp.einsum('bqgrn,nj->bqgj', p_cmp, ov)
    blk = jnp.arange(n_sel)[None, :]
    cur = (q_pos // SEL_BLOCK)[:, None]
    forced = (blk == 0) | (blk == cur) | (blk == cur - 1)
    causal = blk * SEL_BLOCK <= q_pos[:, None]
    imp = jnp.where(forced[None, :, None, :], FORCED_SCORE, jnp.where(causal[None, :, None, :], imp, BLOCKED_SCORE))
    _, idx = lax.top_k(imp, min(TOP_K, n_sel))
    n_k = idx.shape[-1]
    bi = jnp.arange(B)[:, None, None, None]
    gi = jnp.arange(N_KV_A)[None, None, :, None]
    ksel = ks_blk[bi, gi, idx]
    vsel = vs_blk[bi, gi, idx]
    s2 = jnp.einsum('bqgrd,bqgksd->bqgrks', q, ksel) * HEAD_DIM ** -0.5
    kpos = idx[..., None] * SEL_BLOCK + jnp.arange(SEL_BLOCK)
    mask2 = (kpos <= q_pos[None, :, None, None, None]).reshape(B, T, N_KV_A, 1, n_k * SEL_BLOCK)
    p2 = _masked_softmax(s2.reshape(B, T, N_KV_A, R_A, n_k * SEL_BLOCK), mask2)
    o_s = jnp.einsum('bqgrm,bqgmd->bqgrd', p2, vsel.reshape(B, T, N_KV_A, n_k * SEL_BLOCK, HEAD_DIM).astype(F32))
    wb = win_buf.shape[1]
    kw = jnp.concatenate([win_buf, win_rows], axis=1)
    kpos = int(pos[0]) - wb + np.arange(wb + T)
    diff = pos[:, None] - kpos[None, :]
    mask = (diff >= 0) & (diff < WINDOW)
    o_w = _attend(q, kw[:, :, 0], kw[:, :, 1], mask[None, :, None, None, :])
    new_win = kw[:, wb + T - min(WINDOW, wb + T):]
    g = jax.nn.sigmoid(gate.astype(F32)).reshape(B, T, N_KV_A, R_A, 3)
    o = g[..., 0:1] * o_c + g[..., 1:2] * o_s + g[..., 2:3] * o_w
    return o.reshape(B, C_A), rows, new_win


def _x_rwkv_sample(cols, shift0, S0, mu, vec, w_up, a_up, g_up):
    B = cols.shape[0]
    xs = cols + mu * (shift0 - cols)
    r, k, v, wd, ad, gd = jnp.split(xs, _offsets((C_B, C_B, C_B, LORA_W, LORA_A, LORA_G)), axis=-1)
    w0, a0, k_k, k_a, r_k, lnx_w, lnx_b = vec
    w_log = -jax.nn.softplus(-(w0 + jnp.tanh(wd) @ w_up)) - 0.5
    decay = jnp.exp(-jnp.exp(w_log))
    a = jax.nn.sigmoid(a0 + ad @ a_up)
    g = jax.nn.sigmoid(gd) @ g_up
    hs = lambda z: z.reshape(B, H_B, HEAD_DIM)
    kk = hs(k * k_k)
    kk = kk * lax.rsqrt(jnp.sum(kk * kk, -1, keepdims=True) + 1e-12)
    k = k * (1.0 + (a - 1.0) * k_a)
    r_h, w_h, k_h, v_h, a_h = hs(r), hs(decay), hs(k), hs(v), hs(a)
    sk = jnp.einsum('bhij,bhj->bhi', S0, kk)
    S = S0 * w_h[:, :, None, :] - sk[..., None] * (kk * a_h)[:, :, None, :] + v_h[..., None] * k_h[:, :, None, :]
    y = jnp.einsum('bhij,bhj->bhi', S, r_h)
    y = _head_norm(y, RWKV_GN_EPS).reshape(B, C_B) * lnx_w + lnx_b
    bonus = jnp.sum(r_h * k_h * r_k.reshape(H_B, HEAD_DIM), -1, keepdims=True) * v_h
    return (y + bonus.reshape(B, C_B)) * g, S, cols


def _x_s5_sample(u, x0, lw):
    a_row, b_big, c_big, d_row = lw['s5']
    b = u.shape[0]
    np_ = G_C * S5_P
    x0l = jnp.concatenate([x0[..., 0].reshape(b, -1), x0[..., 1].reshape(b, -1)], axis=1)
    bu = jnp.dot(u, b_big, precision=HI)
    ar, ai = a_row[:, :np_], a_row[:, np_:]
    xr = ar * x0l[:, :np_] - ai * x0l[:, np_:] + bu[:, :np_]
    xi = ar * x0l[:, np_:] + ai * x0l[:, :np_] + bu[:, np_:]
    y = jnp.dot(jnp.concatenate([xr, xi], axis=1), c_big, precision=HI) + d_row * u
    z = jax.nn.gelu(y)
    out = z * jax.nn.sigmoid(jnp.dot(z, lw['s5_w_glu'], precision=HI))
    return out, jnp.stack([xr.reshape(b, G_C, S5_P), xi.reshape(b, G_C, S5_P)], axis=-1)


def _x_ret_sample(cols, cos, sin, R0, gn_w):
    B = cols.shape[0]
    q, k, v, g = jnp.split(cols, 4, axis=-1)
    half = HEAD_DIM // 2

    def rope(z):
        z = z.reshape(B, H_D, HEAD_DIM)
        sw = jnp.concatenate([z[..., half:], z[..., :half]], -1)
        return z * cos.reshape(B, H_D, HEAD_DIM) + sw * sin.reshape(B, H_D, HEAD_DIM)

    q = rope(q)
    k = rope(k) * HEAD_DIM ** -0.5
    v = v.reshape(B, H_D, HEAD_DIM)
    gamma = 1.0 - jnp.exp2(-5.0 - jnp.arange(H_D, dtype=F32))
    log_g = jnp.log1p(-jnp.exp2(-5.0 - jnp.arange(H_D, dtype=F32)))
    inner = jnp.einsum('bhd,bhd->bh', q, k)[..., None] * v
    cross = jnp.einsum('bhd,h,bhde->bhe', q, jnp.exp(log_g), R0)
    R = jnp.exp(log_g)[None, :, None, None] * R0 + jnp.einsum('bhd,bhe->bhde', k, v)
    del gamma
    o = _head_norm(inner + cross, RET_GN_EPS).reshape(B, C_D) * gn_w
    return jax.nn.silu(g) * o, R


def _prep_layer(l, p):
    w_in = p['w_in'][l]
    o = _offsets(SPLIT_SIZES)
    segs = jnp.split(w_in, o, axis=1)
    w_all = jnp.concatenate([segs[0], segs[1], _pad_to(segs[2], LANE, 1), _pad_to(segs[3], SHIFT_PAD, 1), segs[4], segs[5]],
                            axis=1).astype(BF16)
    w_prec = w_in[:, :PRECISE_COLS]
    w_lo = (w_prec - w_prec.astype(BF16).astype(F32)).astype(BF16)
    lw = {'layer': l, 'w_all': w_all, 'w_lo': w_lo, 'norm_mix': p['norm_mix'][l]}
    lw['cmp'] = _cmp_weights(p['nsa_cmp_w1'][l], p['nsa_cmp_b1'][l], p['nsa_cmp_w2'][l])
    lw['cmp_raw'] = (p['nsa_cmp_w1'][l], p['nsa_cmp_b1'][l], p['nsa_cmp_w2'][l])
    lw['rwkv_mu'] = _pad_to(p['rwkv_mu'][l].reshape(1, SHIFT_B), SHIFT_PAD, 1)
    lw['rwkv_vec'] = _pad_to(p['rwkv_vec'][l], 8, 0)
    z = lambda a, b: jnp.zeros((a, b), F32)
    lw['rwkv_wup'] = jnp.concatenate([p['rwkv_w_up'][l], z(LANE - LORA_W, C_B)], axis=0)
    lw['rwkv_aup'] = jnp.concatenate([z(LORA_W, C_B), p['rwkv_a_up'][l], z(LANE - LORA_W - LORA_A, C_B)], axis=0)
    lw['rwkv_gup'] = jnp.concatenate([z(LORA_W + LORA_A, C_B), p['rwkv_g_up'][l], z(LANE - LORA_W - LORA_A - LORA_G, C_B)], axis=0)
    lw['rwkv_raw'] = (p['rwkv_mu'][l], p['rwkv_vec'][l], p['rwkv_w_up'][l], p['rwkv_a_up'][l], p['rwkv_g_up'][l])
    for name in ('s5_lambda_re', 's5_lambda_im', 's5_b', 's5_c', 's5_d', 's5_log_step', 's5_w_glu', 'ret_gn'):
        lw[name] = p[name][l]
    lw['s5'], lw['s5_step'] = _s5_params(lw)
    lw['w_out'] = p['w_out'][l].astype(BF16)
    lw['norm_ffn'] = p['norm_ffn'][l].reshape(1, D_MODEL)
    lw['w_router'], lw['b_router'] = _router_weights(p['moe_w_grp'][l], p['moe_b_grp'][l], p['moe_w_exp'][l], p['moe_b_exp'][l])
    lw['moe_wg'] = p['moe_w_gate'][l].astype(BF16)
    lw['moe_wu'] = p['moe_w_up'][l].astype(BF16)
    lw['moe_wd'] = p['moe_w_down'][l].astype(BF16)
    return lw


def _prompt_layer(x, lw, tabs, b, t, norm_final):
    cos_a, sin_a, ret_tabs = tabs
    q, kv, gate, colsb, u, colsd = _proj(x, lw['norm_mix'], lw['w_all'], lw['w_lo'], cos_a, sin_a, 512)
    o_a = _nsa_prompt_mixer(q, kv, gate, lw, b, t)
    o_b, s_rwkv, s_shift = _rwkv_mixer(colsb, jnp.zeros((b, SHIFT_PAD), F32), jnp.zeros((b, H_B, HEAD_DIM, HEAD_DIM), F32),
                                       lw, b, t, 512, b, 256)
    o_c, s_s5 = _s5_mixer(u, jnp.zeros((b, G_C, S5_P, 2), F32), lw, b, t, 256, False)
    o_d, s_ret = _ret_mixer(colsd, jnp.zeros((b, H_D, HEAD_DIM, HEAD_DIM), F32), lw, ret_tabs, b, t)
    x1, h, comb = _out_router(x, o_a, o_b, o_c, o_d, lw, 512)
    x2 = _moe(h, comb, x1, lw, norm_final, 1024, lw['layer'] == DEPTH - 1)
    kv3 = kv.reshape(b, t, 768)
    rows = kv3[:, :, :512].reshape(b, t, 4, N_KV_A, HEAD_DIM)
    win = kv3[:, t - min(WINDOW, t):, 512:].reshape(b, min(WINDOW, t), 2, N_KV_A, HEAD_DIM)
    return x2, (rows, win, s_rwkv, s_shift, s_s5, s_ret)


def _sample_layer(x, lw, tabs, b, pos, cache_kv, page_table, win_buf, s_rwkv, s_shift, s_s5, s_ret, norm_final):
    cos_a, sin_a, ret_cs = tabs
    q, kv, gate, colsb, u, colsd = _proj(x, lw['norm_mix'], lw['w_all'], lw['w_lo'], cos_a, sin_a, b)
    o_a, win = _nsa_sample(q, kv, gate, cache_kv, win_buf, lw['layer'], page_table, lw['cmp'], int(pos[0]))
    rows = kv[:, :512].reshape(b, 1, 4, N_KV_A, HEAD_DIM)
    o_b, s_rwkv = _rwkv_step(colsb, _pad_to(s_shift, SHIFT_PAD, 1), s_rwkv, lw['layer'], lw)
    s_shift = colsb[:, :SHIFT_B]
    o_c, s_s5 = _s5_step(u, s_s5, lw['layer'], lw)
    o_d, s_ret = _ret_step(colsd, ret_cs[0], ret_cs[1], s_ret, lw['layer'], lw['ret_gn'])
    x1, h, comb = _out_router(x, o_a, o_b, o_c, o_d, lw, b)
    x2 = _moe(h, comb, x1, lw, norm_final, b, lw['layer'] == DEPTH - 1)
    return x2, (rows, win, s_rwkv, s_shift, s_s5, s_ret)


def kernel(x_prompt, x_sample, cache_nsa_kv, cache_nsa_win, state_rwkv, state_rwkv_shift, state_s5, state_ret, page_table, norm_mix, w_in, nsa_cmp_w1, nsa_cmp_b1, nsa_cmp_w2, rwkv_mu, rwkv_vec, rwkv_w_up, rwkv_a_up, rwkv_g_up, s5_lambda_re, s5_lambda_im, s5_b, s5_c, s5_d, s5_log_step, s5_w_glu, ret_gn, w_out, norm_ffn, moe_w_grp, moe_b_grp, moe_w_exp, moe_b_exp, moe_w_gate, moe_w_up, moe_w_down, norm_final):
    p = dict(norm_mix=norm_mix, w_in=w_in, nsa_cmp_w1=nsa_cmp_w1, nsa_cmp_b1=nsa_cmp_b1, nsa_cmp_w2=nsa_cmp_w2,
             rwkv_mu=rwkv_mu, rwkv_vec=rwkv_vec, rwkv_w_up=rwkv_w_up, rwkv_a_up=rwkv_a_up, rwkv_g_up=rwkv_g_up,
             s5_lambda_re=s5_lambda_re, s5_lambda_im=s5_lambda_im, s5_b=s5_b, s5_c=s5_c, s5_d=s5_d,
             s5_log_step=s5_log_step, s5_w_glu=s5_w_glu, ret_gn=ret_gn, w_out=w_out, norm_ffn=norm_ffn,
             moe_w_grp=moe_w_grp, moe_b_grp=moe_b_grp, moe_w_exp=moe_w_exp, moe_b_exp=moe_b_exp,
             moe_w_gate=moe_w_gate, moe_w_up=moe_w_up, moe_w_down=moe_w_down)
    bp, tp = x_prompt.shape[:2]
    bs, ts = x_sample.shape[:2]
    past_len = page_table.shape[1] * cache_nsa_kv.shape[2]
    pos_p = np.arange(tp)
    pos_s = past_len + np.arange(ts)
    c = RET_CHUNK if tp % RET_CHUNK == 0 else tp
    tabs_p = _rope_tables(pos_p, ROT_DIM, ROPE_THETA, HEAD_DIM, 2) + (_ret_tables(pos_p, c),)
    pos_rows = np.repeat(pos_s, bs)
    tabs_s = _rope_tables(pos_rows, ROT_DIM, ROPE_THETA, HEAD_DIM, 2) + (_rope_tables(pos_rows, HEAD_DIM, RET_THETA, HEAD_DIM, H_D),)
    xp = x_prompt.reshape(bp * tp, D_MODEL)
    xs = x_sample.reshape(bs * ts, D_MODEL)
    sts_p, sts_s = [], []
    for l in range(DEPTH):
        lw = _prep_layer(l, p)
        xp, st_p = _prompt_layer(xp, lw, tabs_p, bp, tp, norm_final)
        xs, st_s = _sample_layer(xs, lw, tabs_s, bs, pos_s, cache_nsa_kv, page_table, cache_nsa_win, state_rwkv,
                                     state_rwkv_shift[l], state_s5, state_ret, norm_final)
        rows, win, s1, s2, s3, s4 = st_s
        sts_s.append((rows, win, s1, s2, s3, s4))
        sts_p.append(st_p)
    new_p = [jnp.stack([st[i] for st in sts_p]) for i in range(6)]
    new_s = [jnp.stack([st[i] for st in sts_s]) for i in range(6)]
    return (xp.reshape(bp, tp, D_MODEL), xs.reshape(bs, ts, D_MODEL), new_p[0], new_s[0], new_p[1], new_s[1],
            new_p[2], new_s[2], new_p[3], new_s[3], new_p[4], new_s[4], new_p[5], new_s[5])
```

```python
import functools
import math

import numpy as np
import jax
import jax.numpy as jnp
from jax import lax
from jax.experimental import pallas as pl
from jax.experimental.pallas import tpu as pltpu

F32 = jnp.float32
BF16 = jnp.bfloat16
HI = lax.Precision.HIGHEST

D_MODEL = 1024
DEPTH = 2
HEAD_DIM = 64
C_A = C_B = C_C = C_D = 256
H_A = 4
N_KV_A = 2
R_A = 2
ROT_DIM = 16
ROPE_THETA = 500000.0
CMP_BLOCK = 32
CMP_STRIDE = 16
CMP_HIDDEN = 128
SEL_BLOCK = 64
TOP_K = 16
WINDOW = 512
NEG_INF = -1e30
FORCED_SCORE = 1e9
BLOCKED_SCORE = -1e9
H_B = 4
LORA_W = 16
LORA_A = 16
LORA_G = 32
SHIFT_B = 832
SHIFT_PAD = 896
RWKV_GN_EPS = 64e-5
S5_CH = 16
G_C = 16
S5_P = 64
H_D = 4
RET_CHUNK = 128
RET_THETA = 10000.0
RET_GN_EPS = 1e-5
N_GROUPS = 4
EXP_PER_GROUP = 4
N_EXPERTS = 16
D_EXPERT = 256
RMS_EPS = 1e-6
SPLIT_SIZES = (C_A, 6 * N_KV_A * HEAD_DIM, 3 * H_A, SHIFT_B, C_C, 4 * C_D)
LANE = 128
VMEM_LIMIT = 56 * 1024 * 1024


def _cp(*sem):
    return pltpu.CompilerParams(dimension_semantics=sem, vmem_limit_bytes=VMEM_LIMIT)


def _offsets(sizes):
    return [int(s) for s in np.cumsum(sizes)[:-1]]


def _pad_to(a, n, axis):
    pad = [(0, 0)] * a.ndim
    pad[axis] = (0, n - a.shape[axis])
    return jnp.pad(a, pad)


def _block_ones(n, blk, dtype):
    r = lax.broadcasted_iota(jnp.int32, (n, n), 0) // blk
    c = lax.broadcasted_iota(jnp.int32, (n, n), 1) // blk
    return (r == c).astype(dtype)


def _dot2(x, ones_bf16):
    hi = x.astype(BF16)
    lo = (x - hi.astype(F32)).astype(BF16)
    return (jnp.dot(hi, ones_bf16, preferred_element_type=F32)
            + jnp.dot(lo, ones_bf16, preferred_element_type=F32))


def _rope_tables(pos, rot_dim, theta, period, reps):
    half = rot_dim // 2
    inv = theta ** (-jnp.arange(half, dtype=F32) / half)
    ang = jnp.asarray(pos, F32)[:, None] * inv[None, :]
    cos, sin = jnp.cos(ang), jnp.sin(ang)
    n = ang.shape[0]
    rest = period - rot_dim
    c = jnp.concatenate([cos, cos, jnp.ones((n, rest), F32)], -1)
    s = jnp.concatenate([-sin, sin, jnp.zeros((n, rest), F32)], -1)
    return jnp.tile(c, (1, reps)), jnp.tile(s, (1, reps))


PRECISE_COLS = 384


def _proj_body(x_ref, nw_ref, w_ref, wlo_ref, cos_ref, sin_ref, q_ref, kv_ref, g_ref, cb_ref, u_ref, cd_ref):
    x = x_ref[...]
    h = x * lax.rsqrt(jnp.mean(x * x, axis=-1, keepdims=True) + RMS_EPS) * nw_ref[...]
    hb = h.astype(BF16)
    h_lo = (h - hb.astype(F32)).astype(BF16)
    c = cos_ref[...]
    s = sin_ref[...]
    first = (lax.broadcasted_iota(jnp.int32, c.shape, 1) % HEAD_DIM) < (ROT_DIM // 2)

    def rope(z):
        sw = jnp.where(first, pltpu.roll(z, LANE - ROT_DIM // 2, 1), pltpu.roll(z, ROT_DIM // 2, 1))
        return z * c + sw * s

    def dot(a, b):
        z = jnp.dot(hb, w_ref[:, a:b], preferred_element_type=F32)
        if b <= PRECISE_COLS:
            z = z + (jnp.dot(hb, wlo_ref[:, a:b], preferred_element_type=F32)
                     + jnp.dot(h_lo, w_ref[:, a:b], preferred_element_type=F32))
        return z

    for j in range(2):
        q_ref[:, LANE * j:LANE * (j + 1)] = rope(dot(LANE * j, LANE * (j + 1)))
    for j in range(6):
        z = dot(256 + LANE * j, 256 + LANE * (j + 1))
        kv_ref[:, LANE * j:LANE * (j + 1)] = rope(z) if j % 2 == 0 else z
    g_ref[...] = dot(1024, 1152)
    cb_ref[...] = dot(1152, 2048)
    u_ref[...] = dot(2048, 2304)
    cd_ref[...] = dot(2304, 3328)


def _proj(x2d, norm_w, w_all, w_lo, cos_t, sin_t, tm):
    n = x2d.shape[0]
    t_tiles = cos_t.shape[0] // tm
    row = lambda i: (i, 0)
    fixed = lambda i: (0, 0)
    tab = lambda i: (i % t_tiles, 0)
    widths = (256, 768, 128, SHIFT_PAD, 256, 1024)
    return pl.pallas_call(
        _proj_body,
        grid=(n // tm,),
        in_specs=[pl.BlockSpec((tm, D_MODEL), row), pl.BlockSpec((1, D_MODEL), fixed),
                  pl.BlockSpec((D_MODEL, 3328), fixed), pl.BlockSpec((D_MODEL, PRECISE_COLS), fixed),
                  pl.BlockSpec((tm, LANE), tab), pl.BlockSpec((tm, LANE), tab)],
        out_specs=[pl.BlockSpec((tm, w), row) for w in widths],
        out_shape=[jax.ShapeDtypeStruct((n, w), F32) for w in widths],
        compiler_params=_cp("parallel"),
        name="proj",
    )(x2d, norm_w.reshape(1, D_MODEL), w_all, w_lo, cos_t, sin_t)


def _cmp_mlp(xc, kind, w1_ref, b1_ref, w2_ref, w1k_ref, w2k_ref, nck):
    if kind == 0:
        hh = _dot3(xc, w1k_ref[...])
    else:
        hh = jnp.dot(xc.astype(BF16), w1_ref[kind], preferred_element_type=F32)
    hs = []
    for g in range(N_KV_A):
        hg = hh[g * nck:(g + 1) * nck]
        hs.append(jax.nn.gelu(b1_ref[kind] + hg[:, :CMP_HIDDEN] + pltpu.roll(hg[:, CMP_HIDDEN:], nck - 1, 0)))
    act = jnp.concatenate(hs, axis=1)
    if kind == 0:
        return _dot3(act, w2k_ref[...])
    return jnp.dot(act.astype(BF16), w2_ref[kind], preferred_element_type=F32)


def _cmp_body(xk_ref, xv_ref, w1_ref, b1_ref, w2_ref, w1k_ref, w2k_ref, kc_ref, vc_ref, vct_ref, xc_ref, *, n_chunks):
    lane = lax.broadcasted_iota(jnp.int32, (n_chunks, LANE), 1)
    lo = lane < HEAD_DIM
    for pair in range(CMP_STRIDE // 2):
        for kind, x_ref in enumerate((xk_ref, xv_ref)):
            ak = x_ref[0, pl.ds(2 * pair, n_chunks, stride=CMP_STRIDE), :]
            bk = x_ref[0, pl.ds(2 * pair + 1, n_chunks, stride=CMP_STRIDE), :]
            xc_ref[kind, 0:n_chunks, LANE * pair:LANE * (pair + 1)] = jnp.where(lo, ak, pltpu.roll(bk, HEAD_DIM, 1))
            xc_ref[kind, n_chunks:2 * n_chunks, LANE * pair:LANE * (pair + 1)] = jnp.where(lo, pltpu.roll(ak, HEAD_DIM, 1), bk)
    outs = [_cmp_mlp(xc_ref[kind], kind, w1_ref, b1_ref, w2_ref, w1k_ref, w2k_ref, n_chunks) for kind in range(2)]
    kc_ref[0] = outs[0]
    vc_ref[0] = outs[1]
    vct_ref[0] = outs[1].T


def _nsa_compress(rows, w1cat, b1, w2bd, w1k, w2k):
    b, tk = rows.shape[0], rows.shape[1]
    n_chunks = tk // CMP_STRIDE
    fixed3 = lambda i: (0, 0, 0)
    fixed2 = lambda i: (0, 0)
    return pl.pallas_call(
        functools.partial(_cmp_body, n_chunks=n_chunks),
        grid=(b,),
        in_specs=[pl.BlockSpec((1, tk, LANE), lambda i: (i, 0, 0)), pl.BlockSpec((1, tk, LANE), lambda i: (i, 0, 1)),
                  pl.BlockSpec((2, CMP_STRIDE * HEAD_DIM, 2 * CMP_HIDDEN), fixed3),
                  pl.BlockSpec((2, 1, CMP_HIDDEN), fixed3),
                  pl.BlockSpec((2, 2 * CMP_HIDDEN, LANE), fixed3),
                  pl.BlockSpec((CMP_STRIDE * HEAD_DIM, 2 * CMP_HIDDEN), fixed2), pl.BlockSpec((2 * CMP_HIDDEN, LANE), fixed2)],
        out_specs=[pl.BlockSpec((1, n_chunks, LANE), lambda i: (i, 0, 0)),
                   pl.BlockSpec((1, n_chunks, LANE), lambda i: (i, 0, 0)),
                   pl.BlockSpec((1, LANE, n_chunks), lambda i: (i, 0, 0))],
        out_shape=[jax.ShapeDtypeStruct((b, n_chunks, LANE), F32), jax.ShapeDtypeStruct((b, n_chunks, LANE), F32),
                   jax.ShapeDtypeStruct((b, LANE, n_chunks), F32)],
        scratch_shapes=[pltpu.VMEM((2, N_KV_A * n_chunks, CMP_STRIDE * HEAD_DIM), F32)],
        compiler_params=_cp("parallel"),
        name="nsa_compress",
    )(rows, rows, w1cat, b1, w2bd, w1k, w2k)


def _cmp_weights(cmp_w1, cmp_b1, cmp_w2):
    m = CMP_BLOCK // CMP_STRIDE
    w1r = cmp_w1.reshape(2, m, CMP_STRIDE * HEAD_DIM, CMP_HIDDEN)
    w1cat = jnp.concatenate([w1r[:, j] for j in range(m)], axis=-1)
    z = jnp.zeros_like(cmp_w2)
    w2bd = jnp.concatenate([jnp.concatenate([cmp_w2, z], -1), jnp.concatenate([z, cmp_w2], -1)], axis=1)
    return w1cat.astype(BF16), cmp_b1.reshape(2, 1, CMP_HIDDEN), w2bd.astype(BF16), w1cat[0], w2bd[0]


def _cmp_to_sel_t(n_chunks, n_cmp, n_sel):
    starts = np.arange(n_chunks) * CMP_STRIDE
    sel_s = np.arange(n_sel) * SEL_BLOCK
    ov = np.minimum(starts[:, None] + CMP_BLOCK, sel_s[None] + SEL_BLOCK) - np.maximum(starts[:, None], sel_s[None])
    ov = np.clip(ov, 0, None) / CMP_BLOCK
    ov[n_cmp:] = 0.0
    return jnp.asarray(ov.T, dtype=F32)


def _masked_softmax_cols(s, mask):
    m = jnp.max(jnp.where(mask, s, NEG_INF), axis=0, keepdims=True)
    e = jnp.where(mask, jnp.exp(s - m), 0.0)
    den = jnp.sum(e, axis=0, keepdims=True)
    return e * jnp.where(den > 0.0, 1.0 / den, 0.0)


def _nsa_prompt_body(qt_ref, gt_ref, kc_ref, vct_ref, ovt_ref, ks_ref, vst_ref, kw_ref, vwt_ref, o_ref, sel_ref,
                     *, n_cmp, n_sel, qb_size):
    qb = pl.program_id(1)
    tq = qb_size
    n_chunks = kc_ref.shape[1]
    qpos = qb * tq + lax.broadcasted_iota(jnp.int32, (1, tq), 1)
    qpos2 = jnp.concatenate([qpos, qpos], axis=1)
    zeros_q = jnp.zeros((HEAD_DIM, 2 * tq), F32)
    gates = jax.nn.sigmoid(gt_ref[0])
    kc = kc_ref[0]
    n_idx = lax.broadcasted_iota(jnp.int32, (n_chunks, 2 * tq), 0)
    cmp_mask = (n_idx * CMP_STRIDE + (CMP_BLOCK - 1) <= qpos2) & (n_idx < n_cmp)
    blk = lax.broadcasted_iota(jnp.int32, (n_sel, tq), 0)
    cur = qpos // SEL_BLOCK
    forced = (blk == 0) | (blk == cur) | (blk == cur - 1)
    causal_blk = blk * SEL_BLOCK <= qpos
    tk = 2 * tq
    kiota = lax.broadcasted_iota(jnp.int32, (tk, 2 * tq), 0)
    qpads, o_cmps = [], []

    for g in range(N_KV_A):
        q64 = jnp.concatenate([qt_ref[0, (2 * g) * HEAD_DIM:(2 * g + 1) * HEAD_DIM, :],
                               qt_ref[0, (2 * g + 1) * HEAD_DIM:(2 * g + 2) * HEAD_DIM, :]], axis=1) * (HEAD_DIM ** -0.5)
        qpad32 = jnp.concatenate([q64, zeros_q], axis=0) if g == 0 else jnp.concatenate([zeros_q, q64], axis=0)
        qpad = qpad32.astype(BF16)
        qpads.append(qpad)

        p = _masked_softmax_cols(_dot3(kc, qpad32), cmp_mask)
        o_cmps.append(jnp.dot(vct_ref[0, g * HEAD_DIM:(g + 1) * HEAD_DIM, :].astype(BF16), p.astype(BF16),
                              preferred_element_type=F32))
        psum = p[:, :tq] + p[:, tq:]
        imp = jnp.dot(ovt_ref[...], psum, precision=HI, preferred_element_type=F32)
        imp = jnp.where(forced, FORCED_SCORE, jnp.where(causal_blk, imp, BLOCKED_SCORE))
        rank = jnp.zeros((n_sel, tq), F32)
        for i in range(n_sel):
            row = imp[i:i + 1, :]
            rank = rank + jnp.where((row > imp) | ((row == imp) & (blk > i)), 1.0, 0.0)
        sel_ref[g] = jnp.where(rank < float(min(TOP_K, n_sel)), 1.0, 0.0)

    q_all = jnp.concatenate(qpads, axis=1)
    qpos4 = jnp.concatenate([qpos2, qpos2], axis=1)

    def attend(j, carry, k_ref, vt_ref, use_sel, tk, causal=True):
        m, l, acc = carry
        off = pl.multiple_of(j * tk, tk)
        kt = k_ref[0, pl.ds(off, tk), :].astype(BF16)
        s = jnp.dot(kt, q_all, preferred_element_type=F32)
        if causal:
            diff = qpos4 - (off + lax.broadcasted_iota(jnp.int32, (tk, 4 * tq), 0))
        if use_sel:
            per_tile = tk // SEL_BLOCK
            sels = []
            for g in range(N_KV_A):
                rows = [jnp.broadcast_to(sel_ref[g, pl.ds(j * per_tile + a, 1), :], (SEL_BLOCK, tq)) for a in range(per_tile)]
                selm = jnp.concatenate(rows, axis=0)
                sels += [selm, selm]
            mask = jnp.concatenate(sels, axis=1) > 0.0
            if causal:
                mask = mask & (diff >= 0)
        else:
            mask = (diff >= 0) & (diff < WINDOW)
        m_new = jnp.maximum(m, jnp.max(jnp.where(mask, s, NEG_INF), axis=0, keepdims=True))
        alpha = jnp.exp(m - m_new)
        e = jnp.where(mask, jnp.exp(s - m_new), 0.0)
        l_new = alpha * l + jnp.sum(e, axis=0, keepdims=True)
        vt = vt_ref[0, :, pl.ds(off, tk)].astype(BF16)
        return m_new, l_new, alpha * acc + jnp.dot(vt, e.astype(BF16), preferred_element_type=F32)

    init = (jnp.full((1, 4 * tq), NEG_INF, F32), jnp.zeros((1, 4 * tq), F32), jnp.zeros((2 * HEAD_DIM, 4 * tq), F32))
    tk_s, tk_w = 4 * tq, 2 * tq
    last_s = (qb * tq) // tk_s
    slc = functools.partial(attend, k_ref=ks_ref, vt_ref=vst_ref, use_sel=True, tk=tk_s)
    _, l_s, acc_s = slc(last_s, lax.fori_loop(0, last_s, functools.partial(slc, causal=False), init))
    _, l_w, acc_w = lax.fori_loop(jnp.maximum(qb * tq - WINDOW, 0) // tk_w, (qb * tq + tk_w) // tk_w,
                                  functools.partial(attend, k_ref=kw_ref, vt_ref=vwt_ref, use_sel=False, tk=tk_w), init)

    for g in range(N_KV_A):
        blk_g = (slice(g * HEAD_DIM, (g + 1) * HEAD_DIM), slice(g * 2 * tq, (g + 1) * 2 * tq))
        o_slc = acc_s[blk_g] / l_s[:, blk_g[1]]
        o_win = acc_w[blk_g] / l_w[:, blk_g[1]]
        for r in range(R_A):
            h = 2 * g + r
            gr = gates[3 * h:3 * h + 3, :]
            sl = slice(r * tq, (r + 1) * tq)
            o_ref[0, h * HEAD_DIM:(h + 1) * HEAD_DIM, :] = (gr[0:1] * o_cmps[g][:, sl] + gr[1:2] * o_slc[:, sl]
                                                          + gr[2:3] * o_win[:, sl])


def _nsa_prompt(qt, gt, kc, vct, ovt, kv, vst, vwt, n_cmp):
    b, _, t = qt.shape
    tq = 128
    n_sel = t // SEL_BLOCK
    n_chunks = kc.shape[1]
    per_b = lambda i, j: (i, 0, 0)
    return pl.pallas_call(
        functools.partial(_nsa_prompt_body, n_cmp=n_cmp, n_sel=n_sel, qb_size=tq),
        grid=(b, t // tq),
        in_specs=[pl.BlockSpec((1, 256, tq), lambda i, j: (i, 0, j)),
                  pl.BlockSpec((1, 16, tq), lambda i, j: (i, 0, j)),
                  pl.BlockSpec((1, n_chunks, LANE), per_b),
                  pl.BlockSpec((1, LANE, n_chunks), per_b),
                  pl.BlockSpec((n_sel, n_chunks), lambda i, j: (0, 0)),
                  pl.BlockSpec((1, t, LANE), lambda i, j: (i, 0, 2)),
                  pl.BlockSpec((1, LANE, t), per_b),
                  pl.BlockSpec((1, t, LANE), lambda i, j: (i, 0, 4)),
                  pl.BlockSpec((1, LANE, t), per_b)],
        out_specs=pl.BlockSpec((1, 256, tq), lambda i, j: (i, 0, j)),
        out_shape=jax.ShapeDtypeStruct((b, 256, t), F32),
        scratch_shapes=[pltpu.VMEM((N_KV_A, n_sel, tq), F32)],
        compiler_params=_cp("parallel", "arbitrary"),
        name="nsa_prompt",
    )(qt, gt, kc, vct, ovt, kv, vst, kv, vwt)


def _nsa_prompt_mixer(q, kv, gate, lw, b, t):
    kv3 = kv.reshape(b, t, 768)
    n_chunks = t // CMP_STRIDE
    n_cmp = (t - CMP_BLOCK) // CMP_STRIDE + 1
    kc, _, vct = _nsa_compress(kv3, *lw['cmp'])
    ovt = _cmp_to_sel_t(n_chunks, n_cmp, t // SEL_BLOCK)
    qt = jnp.swapaxes(q.reshape(b, t, 256), 1, 2)
    gt = jnp.swapaxes(gate.reshape(b, t, LANE)[:, :, :16], 1, 2)
    vst = jnp.swapaxes(kv3[:, :, 384:512], 1, 2)
    vwt = jnp.swapaxes(kv3[:, :, 640:768], 1, 2)
    ot = _nsa_prompt(qt, gt, kc, vct, ovt, kv3, vst, vwt, n_cmp)
    return jnp.swapaxes(ot, 1, 2).reshape(b * t, 256)


def _softmax_rows_with_extra(s, mask, s_new):
    m = jnp.maximum(jnp.max(jnp.where(mask, s, NEG_INF), axis=-1, keepdims=True), s_new)
    e = jnp.where(mask, jnp.exp(s - m), 0.0)
    e_new = jnp.exp(s_new - m)
    return e, e_new, 1.0 / (jnp.sum(e, axis=-1, keepdims=True) + e_new)


def _nsa_sample_body(pt_ref, *refs, n_pages, page, pos, n_sel, n_cmp, wb):
    del pt_ref
    n_in = n_pages
    pages = refs[:n_in]
    (qbd_ref, new_ref, gate_ref, win_ref, ov_ref, w1_ref, b1_ref, w2_ref, w1k_ref, w2k_ref,
     o_ref, nw_ref, tok_ref, xc_ref) = refs[n_in:]
    pg = lambda p, kind: pages[p].at[0, kind:kind + 1]
    nck = n_pages * page // CMP_STRIDE
    lane8 = lax.broadcasted_iota(jnp.int32, (8, LANE), 1)
    lo8 = lane8 < HEAD_DIM
    nt = lambda a, b: lax.dot_general(a, b, (((1,), (1,)), ((), ())), preferred_element_type=F32)

    lo_c = lax.broadcasted_iota(jnp.int32, (nck, LANE), 1) < HEAD_DIM
    for kind in range(2):
        for p in range(n_pages):
            tok_ref[kind, p * page:(p + 1) * page, :] = pg(p, kind)[0].T
        for pair in range(CMP_STRIDE // 2):
            a = tok_ref[kind, pl.ds(2 * pair, nck, stride=CMP_STRIDE), :]
            b = tok_ref[kind, pl.ds(2 * pair + 1, nck, stride=CMP_STRIDE), :]
            cols = slice(LANE * pair, LANE * (pair + 1))
            xc_ref[kind, 0:nck, cols] = jnp.where(lo_c, a, pltpu.roll(b, HEAD_DIM, 1))
            xc_ref[kind, nck:2 * nck, cols] = jnp.where(lo_c, pltpu.roll(a, HEAD_DIM, 1), b)
    kc, vc = [_cmp_mlp(xc_ref[kind], kind, w1_ref, b1_ref, w2_ref, w1k_ref, w2k_ref, nck) for kind in range(2)]

    q = qbd_ref[0] * (HEAD_DIM ** -0.5)
    qb = q.astype(BF16)
    new = new_ref[0]

    n_idx = lax.broadcasted_iota(jnp.int32, (8, nck), 1)
    cmask = (n_idx * CMP_STRIDE + (CMP_BLOCK - 1) <= pos) & (n_idx < n_cmp)
    s = _dot3(q, kc, _NT)
    m = jnp.max(jnp.where(cmask, s, NEG_INF), axis=-1, keepdims=True)
    e = jnp.where(cmask, jnp.exp(s - m), 0.0)
    den = jnp.sum(e, axis=-1, keepdims=True)
    p_cmp = e * jnp.where(den > 0.0, 1.0 / den, 0.0)
    o_cmp = jnp.dot(p_cmp.astype(BF16), vc.astype(BF16), preferred_element_type=F32)

    row8 = lax.broadcasted_iota(jnp.int32, (8, nck), 0)
    psum = jnp.where(row8 == 0, p_cmp[0:1] + p_cmp[1:2], jnp.where(row8 == 1, p_cmp[2:3] + p_cmp[3:4], 0.0))
    imp = jnp.dot(psum, ov_ref[...], precision=HI, preferred_element_type=F32)
    cur = pos // SEL_BLOCK
    forced = (lane8 == 0) | (lane8 == cur) | (lane8 == cur - 1)
    imp = jnp.where(forced, FORCED_SCORE, jnp.where(lane8 * SEL_BLOCK <= pos, imp, BLOCKED_SCORE))
    imp = jnp.where(lane8 < n_sel, imp, -3e38)
    rank = jnp.zeros((8, LANE), F32)
    for i in range(n_sel):
        col = imp[:, i:i + 1]
        rank = rank + jnp.where((col > imp) | ((col == imp) & (lane8 > i)), 1.0, 0.0)
    sel = jnp.where((rank < float(min(TOP_K, n_sel))) & (lane8 < n_sel), 1.0, 0.0)
    rsel = lax.broadcasted_iota(jnp.int32, (8, LANE), 0)
    selh = jnp.where(rsel < R_A, sel[0:1], jnp.where(rsel < 2 * R_A, sel[1:2], 0.0))

    per_page = page // SEL_BLOCK
    s_t, m_t = [], []
    for p in range(n_pages):
        s_t.append(jnp.dot(qb, pg(p, 2)[0].astype(BF16), preferred_element_type=F32))
        blk_sel = selh[:, per_page * p:per_page * p + 1]
        for a in range(1, per_page):
            blk_sel = jnp.where(lane8 < a * SEL_BLOCK, blk_sel, selh[:, per_page * p + a:per_page * p + a + 1])
        kpos = p * page + lane8
        m_t.append((blk_sel > 0.0) & (kpos <= pos))
    s_all = jnp.concatenate(s_t, axis=1)
    mk_all = jnp.concatenate(m_t, axis=1)
    s_new = jnp.sum(q * new[2:3], axis=-1, keepdims=True)
    e, e_new, inv = _softmax_rows_with_extra(s_all, mk_all, s_new)
    acc = e_new * new[3:4]
    for p in range(n_pages):
        acc = acc + nt(e[:, p * page:(p + 1) * page].astype(BF16), pg(p, 3)[0].astype(BF16))
    o_slc = acc * inv

    kw, vw = win_ref[0], win_ref[1]
    widx = lax.broadcasted_iota(jnp.int32, (8, wb), 1)
    diff = wb - widx
    s_w = jnp.dot(qb, kw.astype(BF16), preferred_element_type=F32)
    s_wnew = jnp.sum(q * new[4:5], axis=-1, keepdims=True)
    e, e_new, inv = _softmax_rows_with_extra(s_w, (diff >= 0) & (diff < WINDOW), s_wnew)
    o_win = (nt(e.astype(BF16), vw.astype(BF16)) + e_new * new[5:6]) * inv

    gts = jax.nn.sigmoid(gate_ref[0])
    o = gts[:, 0:1] * o_cmp + gts[:, 1:2] * o_slc + gts[:, 2:3] * o_win
    lo1 = lo8[0:1]
    o_ref[0] = jnp.concatenate([jnp.where(lo1, o[0:1], pltpu.roll(o[1:2], HEAD_DIM, 1)),
                                jnp.where(lo1, pltpu.roll(o[2:3], HEAD_DIM, 1), o[3:4])], axis=1)
    last = lax.broadcasted_iota(jnp.int32, (LANE, wb), 1) == wb - 1
    new_t = new.T
    nw_ref[0] = jnp.where(last, new_t[:, 4:5], pltpu.roll(kw, wb - 1, 1))
    nw_ref[1] = jnp.where(last, new_t[:, 5:6], pltpu.roll(vw, wb - 1, 1))


def _nsa_sample(q, kv, gate, cache_kv, cache_win, layer, page_table, cmp_w, pos):
    b = q.shape[0]
    n_pool, page = cache_kv.shape[1:3]
    n_pages = page_table.shape[1]
    wb = cache_win.shape[2]
    assert wb == WINDOW and page % CMP_STRIDE == 0 and page % SEL_BLOCK == 0 and page == LANE
    tk = n_pages * page + 1
    n_cmp = (tk - CMP_BLOCK) // CMP_STRIDE + 1
    nck = n_pages * page // CMP_STRIDE
    assert n_cmp <= nck
    n_sel = -(-tk // SEL_BLOCK)
    assert n_sel <= LANE and pos // SEL_BLOCK == n_sel - 1
    ov = _pad_to(_cmp_to_sel_t(nck, n_cmp, n_sel).T, LANE, 1)
    w1cat, b1, w2bd, w1k, w2k = cmp_w
    q4 = q.reshape(b, H_A, HEAD_DIM)
    z = jnp.zeros_like(q4)
    first = (jnp.arange(H_A) // R_A == 0)[None, :, None]
    qbd = jnp.concatenate([jnp.where(first, q4, z), jnp.where(first, z, q4)], axis=-1)
    qbd = _pad_to(qbd, 8, 1)
    new = _pad_to(kv.reshape(b, 6, LANE), 8, 1)
    g8 = _pad_to(_pad_to(gate[:, :3 * H_A].reshape(b, H_A, 3), LANE, 2), 8, 1)
    cache3 = jnp.transpose(cache_kv, (0, 1, 3, 4, 5, 2)).reshape(-1, 4, LANE, page)
    win3 = jnp.transpose(cache_win, (0, 1, 3, 4, 5, 2)).reshape(-1, LANE, wb)
    page_specs = [pl.BlockSpec((1, 4, LANE, page), functools.partial(lambda i, pt, p: (layer * n_pool + pt[i, p], 0, 0, 0), p=p))
                  for p in range(n_pages)]
    per_b = lambda i, pt: (i, 0, 0)
    fixed2 = lambda i, pt: (0, 0)
    fixed3 = lambda i, pt: (0, 0, 0)
    grid_spec = pltpu.PrefetchScalarGridSpec(
        num_scalar_prefetch=1,
        grid=(b,),
        in_specs=page_specs + [pl.BlockSpec((1, 8, LANE), per_b), pl.BlockSpec((1, 8, LANE), per_b), pl.BlockSpec((1, 8, LANE), per_b),
                               pl.BlockSpec((2, LANE, wb), lambda i, pt: (layer * b + i, 0, 0)), pl.BlockSpec((nck, LANE), fixed2),
                               pl.BlockSpec((2, CMP_STRIDE * HEAD_DIM, 2 * CMP_HIDDEN), fixed3),
                               pl.BlockSpec((2, 1, CMP_HIDDEN), fixed3), pl.BlockSpec((2, 2 * CMP_HIDDEN, LANE), fixed3),
                               pl.BlockSpec((CMP_STRIDE * HEAD_DIM, 2 * CMP_HIDDEN), fixed2),
                               pl.BlockSpec((2 * CMP_HIDDEN, LANE), fixed2)],
        out_specs=[pl.BlockSpec((1, 1, 256), per_b), pl.BlockSpec((2, LANE, wb), per_b)],
        scratch_shapes=[pltpu.VMEM((2, n_pages * page, LANE), F32), pltpu.VMEM((2, 2 * nck, CMP_STRIDE * HEAD_DIM), F32)],
    )
    o, nw = pl.pallas_call(
        functools.partial(_nsa_sample_body, n_pages=n_pages, page=page, pos=pos, n_sel=n_sel, n_cmp=n_cmp, wb=wb),
        grid_spec=grid_spec,
        out_shape=[jax.ShapeDtypeStruct((b, 1, 256), F32), jax.ShapeDtypeStruct((2 * b, LANE, wb), F32)],
        compiler_params=_cp("parallel"),
        name="nsa_sample",
    )(page_table, *([cache3] * n_pages), qbd, new, g8, win3, ov, w1cat, b1, w2bd, w1k, w2k)
    return o.reshape(b, 256), jnp.transpose(nw.reshape(b, 2, N_KV_A, HEAD_DIM, wb), (0, 4, 1, 2, 3))


def _rwkv_prep_body(c_ref, s0_ref, mu_ref, vec_ref, wup_ref, aup_ref, gup_ref,
                    r_ref, lw_ref, k_ref, v_ref, kk_ref, ka_ref, g_ref, bonus_ref, carry_ref, *, tiles_per_seq):
    i = pl.program_id(0)
    cols = c_ref[...]
    tm = cols.shape[0]

    @pl.when(i % tiles_per_seq == 0)
    def _():
        carry_ref[...] = s0_ref[0]

    prev = pltpu.roll(cols, 1, 0)
    row0 = lax.broadcasted_iota(jnp.int32, cols.shape, 0) == 0
    prev = jnp.where(row0, carry_ref[...], prev)
    carry_ref[...] = cols[tm - 1:tm, :]
    _rwkv_features(cols, prev, mu_ref, vec_ref, wup_ref, aup_ref, gup_ref,
                   r_ref, lw_ref, k_ref, v_ref, kk_ref, ka_ref, g_ref, bonus_ref)


def _rwkv_step_prep_body(c_ref, s0_ref, mu_ref, vec_ref, wup_ref, aup_ref, gup_ref,
                         r_ref, lw_ref, k_ref, v_ref, kk_ref, ka_ref, g_ref, bonus_ref):
    _rwkv_features(c_ref[...], s0_ref[...], mu_ref, vec_ref, wup_ref, aup_ref, gup_ref,
                   r_ref, lw_ref, k_ref, v_ref, kk_ref, ka_ref, g_ref, bonus_ref)


def _rwkv_features(cols, prev, mu_ref, vec_ref, wup_ref, aup_ref, gup_ref,
                   r_ref, lw_ref, k_ref, v_ref, kk_ref, ka_ref, g_ref, bonus_ref):
    xs = cols + mu_ref[...] * (prev - cols)
    r, k, v, lo = xs[:, 0:256], xs[:, 256:512], xs[:, 512:768], xs[:, 768:896]
    vec = vec_ref[...]
    w0, a0, k_k, k_a, r_k = vec[0:1], vec[1:2], vec[2:3], vec[3:4], vec[4:5]
    dot_hi = lambda x, w: jnp.dot(x, w, precision=HI, preferred_element_type=F32)
    w_log = -jax.nn.softplus(-(w0 + dot_hi(jnp.tanh(lo), wup_ref[...]))) - 0.5
    a = jax.nn.sigmoid(a0 + dot_hi(lo, aup_ref[...]))
    g_ref[...] = dot_hi(jax.nn.sigmoid(lo), gup_ref[...])
    ones = _block_ones(256, HEAD_DIM, F32)
    kk = k * k_k
    kk = kk * lax.rsqrt(dot_hi(kk * kk, ones) + 1e-12)
    k2 = k * (1.0 + (a - 1.0) * k_a)
    r_ref[...] = r
    lw_ref[...] = -jnp.exp(w_log)
    k_ref[...] = k2
    v_ref[...] = v
    kk_ref[...] = kk
    ka_ref[...] = kk * a
    bonus_ref[...] = dot_hi(r * k2 * r_k, ones) * v


def _rwkv_prep(colsb, shift0, lw, t, tm):
    n = colsb.shape[0]
    tiles_per_seq = t // tm
    row = lambda i: (i, 0)
    fixed = lambda i: (0, 0)
    outs = [jax.ShapeDtypeStruct((n, 256), F32)] * 8
    return pl.pallas_call(
        functools.partial(_rwkv_prep_body, tiles_per_seq=tiles_per_seq),
        grid=(n // tm,),
        in_specs=[pl.BlockSpec((tm, SHIFT_PAD), row),
                  pl.BlockSpec((1, 1, SHIFT_PAD), lambda i: (i // tiles_per_seq, 0, 0)),
                  pl.BlockSpec((1, SHIFT_PAD), fixed), pl.BlockSpec((8, 256), fixed),
                  pl.BlockSpec((LANE, 256), fixed), pl.BlockSpec((LANE, 256), fixed), pl.BlockSpec((LANE, 256), fixed)],
        out_specs=[pl.BlockSpec((tm, 256), row)] * 8,
        out_shape=outs,
        scratch_shapes=[pltpu.VMEM((1, SHIFT_PAD), F32)],
        compiler_params=_cp("arbitrary"),
        name="rwkv_prep",
    )(colsb, shift0.reshape(-1, 1, SHIFT_PAD), lw['rwkv_mu'], lw['rwkv_vec'], lw['rwkv_wup'], lw['rwkv_aup'], lw['rwkv_gup'])


def _rwkv_scan_body(r_ref, lw_ref, k_ref, v_ref, kk_ref, ka_ref, g_ref, bonus_ref, s0_ref, vec_ref,
                    o_ref, st_ref, s_scr, y_scr, *, nb, tl):
    @pl.when(pl.program_id(1) == 0)
    def _():
        s_scr[...] = s0_ref[...]

    ones = _block_ones(256, HEAD_DIM, BF16)
    isub = lax.broadcasted_iota(jnp.int32, (HEAD_DIM, 256), 0)
    ilane = lax.broadcasted_iota(jnp.int32, (HEAD_DIM, 256), 1) % HEAD_DIM
    msel = (isub == ilane).astype(F32)

    nr = nb * HEAD_DIM

    def step(t, carry):
        tp = jnp.maximum(t - 1, 0)
        states = [s_scr[b] for b in range(nb)]
        t1 = jnp.concatenate([states[b] * kk_ref[b, pl.ds(t, 1), :] for b in range(nb)], axis=0)
        t1h = t1.astype(BF16)
        t1l = (t1 - t1h.astype(F32)).astype(BF16)
        t2h, t2l = [], []
        for b in range(nb):
            v = v_ref[b, pl.ds(t, 1), :]
            vh = v.astype(BF16).astype(F32)
            t2h.append((msel * vh).astype(BF16))
            t2l.append((msel * (v - vh)).astype(BF16))
        t3 = [(states[b] * r_ref[b, pl.ds(tp, 1), :]).astype(BF16) for b in range(nb)]
        res_h = jnp.dot(jnp.concatenate([t1h] + t2h + t3, axis=0), ones, preferred_element_type=F32)
        res_l = jnp.dot(jnp.concatenate([t1l] + t2l, axis=0), ones, preferred_element_type=F32)
        for b in range(nb):
            rows = slice(b * HEAD_DIM, (b + 1) * HEAD_DIM)
            sk = res_h[rows] + res_l[rows]
            vrep = res_h[nr:2 * nr][rows] + res_l[nr:2 * nr][rows]
            s_scr[b] = (states[b] * jnp.exp(lw_ref[b, pl.ds(t, 1), :]) - sk * ka_ref[b, pl.ds(t, 1), :]
                        + vrep * k_ref[b, pl.ds(t, 1), :])
            y_scr[b, pl.ds(tp, 1), :] = jnp.sum(res_h[2 * nr:][rows] * msel, axis=0, keepdims=True)
        return carry

    lax.fori_loop(0, tl, step, 0)
    for b in range(nb):
        yrep = jnp.dot((s_scr[b] * r_ref[b, tl - 1:tl, :]).astype(BF16), ones, preferred_element_type=F32)
        y_scr[b, tl - 1:tl, :] = jnp.sum(yrep * msel, axis=0, keepdims=True)
    st_ref[...] = s_scr[...]
    vec = vec_ref[...]
    lnx_w, lnx_b = vec[5:6], vec[6:7]
    avg = _block_ones(256, HEAD_DIM, F32) * (1.0 / HEAD_DIM)
    for b in range(nb):
        y = y_scr[b]
        yc = y - jnp.dot(y, avg, precision=HI, preferred_element_type=F32)
        yn = yc * lax.rsqrt(jnp.dot(yc * yc, avg, precision=HI, preferred_element_type=F32) + RWKV_GN_EPS)
        o_ref[b] = (yn * lnx_w + lnx_b + bonus_ref[b]) * g_ref[b]


def _rwkv_scan(prep, s0, vec, b, t, nb, tl):
    r, lw, k, v, kk, ka, g, bonus = [a.reshape(b, t, 256) for a in prep]
    seq = pl.BlockSpec((nb, tl, 256), lambda i, j: (i, j, 0))
    st = pl.BlockSpec((nb, HEAD_DIM, 256), lambda i, j: (i, 0, 0))
    return pl.pallas_call(
        functools.partial(_rwkv_scan_body, nb=nb, tl=tl),
        grid=(b // nb, t // tl),
        in_specs=[seq] * 8 + [st, pl.BlockSpec((8, 256), lambda i, j: (0, 0))],
        out_specs=[seq, st],
        out_shape=[jax.ShapeDtypeStruct((b, t, 256), F32), jax.ShapeDtypeStruct((b, HEAD_DIM, 256), F32)],
        scratch_shapes=[pltpu.VMEM((nb, HEAD_DIM, 256), F32), pltpu.VMEM((nb, tl, 256), F32)],
        compiler_params=_cp("parallel", "arbitrary"),
        name="rwkv_scan",
    )(r, lw, k, v, kk, ka, g, bonus, s0, vec)


RWKV_CHUNK = 64
_PRE_TILES = 9


def _split_bf16(x):
    hi = x.astype(BF16)
    return hi, (x - hi.astype(F32)).astype(BF16)


def _dot3(a, b, dims=(((1,), (0,)), ((), ()))):
    ah, al = _split_bf16(a)
    bh, bl = _split_bf16(b)
    dg = lambda x, y: lax.dot_general(x, y, dims, preferred_element_type=F32)
    return dg(ah, bh) + dg(ah, bl) + dg(al, bh)


_NT = (((1,), (1,)), ((), ()))


def _rwkv_chunk_body(r_ref, lw_ref, k_ref, v_ref, kk_ref, ka_ref, g_ref, bonus_ref, s0_ref, vec_ref,
                     o_ref, st_ref, s_scr, y_scr, *, nb, tl):
    L = RWKV_CHUNK
    nc = tl // L

    @pl.when(pl.program_id(1) == 0)
    def _():
        s_scr[...] = s0_ref[...]

    ri = lax.broadcasted_iota(jnp.int32, (L, L), 0)
    ci = lax.broadcasted_iota(jnp.int32, (L, L), 1)
    strict, incl = ri > ci, ri >= ci
    ltri = incl.astype(F32)
    eye = (ri == ci).astype(F32)

    bnn = (((2,), (1,)), ((0,), (0,)))
    bnt = (((2,), (2,)), ((0,), (0,)))

    def chunk(c, carry):
        rows = pl.ds(pl.multiple_of(c * L, L), L)
        lhs_l, rhs_l, v_l, kw_l, wl_l = [], [], [], [], []
        for b in range(nb):
            r, lw, k, v, kk, ka = [ref[b, rows, :] for ref in (r_ref, lw_ref, k_ref, v_ref, kk_ref, ka_ref)]
            cl = jnp.dot(ltri, lw, precision=HI, preferred_element_type=F32)
            e_neg = jnp.exp(-cl)
            e_rem = jnp.exp(cl[L - 1:L, :] - cl)
            kkd, rd = kk * jnp.exp(cl - lw), r * jnp.exp(cl)
            kinv, kainv, kw, kaw = k * e_neg, ka * e_neg, k * e_rem, ka * e_rem
            w_last = jnp.exp(cl[L - 1:L, :])
            for h in range(H_B):
                sl = slice(h * HEAD_DIM, (h + 1) * HEAD_DIM)
                lhs_l.append(jnp.concatenate([kkd[:, sl], rd[:, sl]], axis=0))
                rhs_l.append(jnp.concatenate([kinv[:, sl], kainv[:, sl]], axis=0))
                v_l.append(v[:, sl])
                kw_l.append(jnp.concatenate([kw[:, sl], kaw[:, sl]], axis=0))
                wl_l.append(w_last[:, sl])
        lhs, rhs, vs, kws, wl = [jnp.stack(x) for x in (lhs_l, rhs_l, v_l, kw_l, wl_l)]
        gm = _dot3(lhs, rhs, bnt)
        a_vk = jnp.where(strict, gm[:, :L, :L], 0.0)
        n1 = jnp.where(strict, -gm[:, :L, L:], 0.0)
        t_inv, pw = eye + n1, n1
        for _ in range(5):
            pw = _dot3(pw, pw, bnn)
            t_inv = _dot3(t_inv, eye + pw, bnn)
        s = s_scr[...].reshape(nb * H_B, HEAD_DIM, HEAD_DIM)
        xs = _dot3(lhs, s, bnt)
        u = _dot3(t_inv, xs[:, :L] + _dot3(a_vk, vs, bnn), bnn)
        b_vk = jnp.where(incl, gm[:, L:, :L], 0.0).astype(BF16)
        b_uk = jnp.where(incl, gm[:, L:, L:], 0.0).astype(BF16)
        y = (xs[:, L:] + lax.dot_general(b_vk, vs.astype(BF16), bnn, preferred_element_type=F32)
             - lax.dot_general(b_uk, u.astype(BF16), bnn, preferred_element_type=F32))
        vu_t = jnp.stack([jnp.concatenate([vs[n], -u[n]], axis=0).T for n in range(nb * H_B)])
        s_new = s * wl + _dot3(vu_t, kws, bnn)
        s_scr[...] = s_new.reshape(nb, H_B, HEAD_DIM, HEAD_DIM)
        for b in range(nb):
            for h in range(H_B):
                y_scr[b, rows, h * HEAD_DIM:(h + 1) * HEAD_DIM] = y[b * H_B + h]
        return carry

    lax.fori_loop(0, nc, chunk, 0)
    st_ref[...] = s_scr[...]
    vec = vec_ref[...]
    for b in range(nb):
        o_ref[b] = (_segment_norm(y_scr[b], RWKV_GN_EPS) * vec[5:6] + vec[6:7] + bonus_ref[b]) * g_ref[b]


def _rwkv_chunked(prep, s0, vec, b, t, nb, tl):
    arrs = [a.reshape(b, t, 256) for a in prep]
    seq = pl.BlockSpec((nb, tl, 256), lambda i, j: (i, j, 0))
    st = pl.BlockSpec((nb, H_B, HEAD_DIM, HEAD_DIM), lambda i, j: (i, 0, 0, 0))
    return pl.pallas_call(
        functools.partial(_rwkv_chunk_body, nb=nb, tl=tl),
        grid=(b // nb, t // tl),
        in_specs=[seq] * 8 + [st, pl.BlockSpec((8, 256), lambda i, j: (0, 0))],
        out_specs=[seq, st],
        out_shape=[jax.ShapeDtypeStruct((b, t, 256), F32), jax.ShapeDtypeStruct((b, H_B, HEAD_DIM, HEAD_DIM), F32)],
        scratch_shapes=[pltpu.VMEM((nb, H_B, HEAD_DIM, HEAD_DIM), F32), pltpu.VMEM((nb, tl, 256), F32)],
        compiler_params=_cp("parallel", "arbitrary"),
        name="rwkv_chunked",
    )(*arrs, s0, vec)


def _rwkv_mixer(colsb, shift0, s0, lw, b, t, tm, nb, tl):
    prep = _rwkv_prep(colsb, shift0, lw, t, tm)
    o, st = _rwkv_chunked(prep, s0, lw['rwkv_vec'], b, t, nb, tl)
    shift = colsb.reshape(b, t, SHIFT_PAD)[:, -1, :SHIFT_B]
    return o.reshape(b * t, 256), st, shift


def _segment_norm(y, eps):
    avg = _block_ones(256, HEAD_DIM, F32) * (1.0 / HEAD_DIM)
    yc = y - jnp.dot(y, avg, precision=HI, preferred_element_type=F32)
    return yc * lax.rsqrt(jnp.dot(yc * yc, avg, precision=HI, preferred_element_type=F32) + eps)


def _rwkv_step_body(r_ref, lw_ref, k_ref, v_ref, kk_ref, ka_ref, g_ref, bonus_ref, vec_ref, s_ref, o_ref, st_ref, ft_scr, y_scr):
    h = pl.program_id(0)

    @pl.when(h == 0)
    def _():
        for n, ref in enumerate((r_ref, lw_ref, k_ref, v_ref, kk_ref, ka_ref)):
            ft_scr[n] = ref[...].T

    base = pl.multiple_of(h * HEAD_DIM, HEAD_DIM)
    head = lambda n: ft_scr[n, pl.ds(base, HEAD_DIM), :]
    r_t, w_t, k_t, kk_t, ka_t = head(0), jnp.exp(head(1)), head(2), head(4), head(5)

    def body(i, carry):
        rows = pl.ds(pl.multiple_of(i * HEAD_DIM, HEAD_DIM), HEAD_DIM)
        s = s_ref[rows, :]
        sk = jnp.sum(s * kk_t, axis=0, keepdims=True)
        s = s * w_t - sk * ka_t + ft_scr[3, pl.ds(base + i, 1), :] * k_t
        st_ref[rows, :] = s
        y_scr[pl.ds(base + i, 1), :] = jnp.sum(s * r_t, axis=0, keepdims=True)
        return carry

    lax.fori_loop(0, HEAD_DIM, body, 0)

    @pl.when(h == H_B - 1)
    def _():
        vec = vec_ref[...]
        o_ref[...] = (_segment_norm(y_scr[...].T, RWKV_GN_EPS) * vec[5:6] + vec[6:7] + bonus_ref[...]) * g_ref[...]


def _rwkv_step(colsb, shift0, s_all, layer, lw):
    b = colsb.shape[0]
    hd2 = HEAD_DIM * HEAD_DIM
    full = lambda *_: (0, 0)
    feat = pl.BlockSpec((b, 256), full)
    prep = pl.pallas_call(
        _rwkv_step_prep_body,
        grid=(1,),
        in_specs=[pl.BlockSpec((b, SHIFT_PAD), full), pl.BlockSpec((b, SHIFT_PAD), full), pl.BlockSpec((1, SHIFT_PAD), full),
                  pl.BlockSpec((8, 256), full)] + [pl.BlockSpec((LANE, 256), full)] * 3,
        out_specs=[feat] * 8,
        out_shape=[jax.ShapeDtypeStruct((b, 256), F32)] * 8,
        compiler_params=_cp("arbitrary"),
        name="rwkv_step_prep",
    )(colsb, shift0, lw['rwkv_mu'], lw['rwkv_vec'], lw['rwkv_wup'], lw['rwkv_aup'], lw['rwkv_gup'])
    s_rows = jnp.transpose(s_all, (0, 2, 3, 4, 1)).reshape(-1, b)
    o, st = pl.pallas_call(
        _rwkv_step_body,
        grid=(H_B,),
        in_specs=[feat] * 8 + [pl.BlockSpec((8, 256), full), pl.BlockSpec((hd2, b), lambda h: (layer * H_B + h, 0))],
        out_specs=[feat, pl.BlockSpec((hd2, b), lambda h: (h, 0))],
        out_shape=[jax.ShapeDtypeStruct((b, 256), F32), jax.ShapeDtypeStruct((H_B * hd2, b), F32)],
        scratch_shapes=[pltpu.VMEM((6, 256, b), F32), pltpu.VMEM((256, b), F32)],
        compiler_params=_cp("arbitrary"),
        name="rwkv_step",
    )(*prep, lw['rwkv_vec'], s_rows)
    return o, jnp.transpose(st.reshape(H_B, HEAD_DIM, HEAD_DIM, b), (3, 0, 1, 2))


def _ret_step_body(c_ref, cos_ref, sin_ref, gn_ref, r0_ref, o_ref, rt_ref, ft_scr, acc_scr):
    h = pl.program_id(0)

    @pl.when(h == 0)
    def _():
        x = c_ref[...]
        cs, sn = cos_ref[...], sin_ref[...]
        first = (lax.broadcasted_iota(jnp.int32, cs.shape, 1) % HEAD_DIM) < (HEAD_DIM // 2)

        def rope(z):
            sw = jnp.where(first, pltpu.roll(z, 256 - HEAD_DIM // 2, 1), pltpu.roll(z, HEAD_DIM // 2, 1))
            return z * cs + sw * sn

        ft_scr[0] = rope(x[:, 0:256]).T
        ft_scr[1] = (rope(x[:, 256:512]) * (HEAD_DIM ** -0.5)).T
        ft_scr[2] = x[:, 512:768].T

    base = pl.multiple_of(h * HEAD_DIM, HEAD_DIM)
    nseq = rt_ref.shape[1]
    gamma = jnp.exp(jnp.log1p(-jnp.exp2(-5.0 - jnp.full((1, nseq), h, jnp.int32).astype(F32))))
    q_t, k_t, v_t = [ft_scr[n, pl.ds(base, HEAD_DIM), :] for n in range(3)]
    qk = jnp.sum(q_t * k_t, axis=0, keepdims=True)

    def body(d, cross):
        rows = pl.ds(pl.multiple_of(d * HEAD_DIM, HEAD_DIM), HEAD_DIM)
        r_old = r0_ref[rows, :]
        rt_ref[rows, :] = gamma * r_old + ft_scr[1, pl.ds(base + d, 1), :] * v_t
        return cross + ft_scr[0, pl.ds(base + d, 1), :] * r_old

    cross = lax.fori_loop(0, HEAD_DIM, body, jnp.zeros((HEAD_DIM, nseq), F32))
    acc_scr[pl.ds(base, HEAD_DIM), :] = qk * v_t + gamma * cross

    @pl.when(h == H_D - 1)
    def _():
        o_ref[...] = jax.nn.silu(c_ref[:, 768:1024]) * (_segment_norm(acc_scr[...].T, RET_GN_EPS) * gn_ref[...])


def _ret_step(colsd, cos, sin, r_all, layer, gn):
    b = colsd.shape[0]
    hd2 = HEAD_DIM * HEAD_DIM
    full = lambda *_: (0, 0)
    r_rows = jnp.transpose(r_all, (0, 2, 3, 4, 1)).reshape(-1, b)
    o, rt = pl.pallas_call(
        _ret_step_body,
        grid=(H_D,),
        in_specs=[pl.BlockSpec((b, 1024), full), pl.BlockSpec((b, 256), full), pl.BlockSpec((b, 256), full),
                  pl.BlockSpec((1, 256), full), pl.BlockSpec((hd2, b), lambda h: (layer * H_D + h, 0))],
        out_specs=[pl.BlockSpec((b, 256), full), pl.BlockSpec((hd2, b), lambda h: (h, 0))],
        out_shape=[jax.ShapeDtypeStruct((b, 256), F32), jax.ShapeDtypeStruct((H_D * hd2, b), F32)],
        scratch_shapes=[pltpu.VMEM((3, 256, b), F32), pltpu.VMEM((256, b), F32)],
        compiler_params=_cp("arbitrary"),
        name="ret_step",
    )(colsd, cos, sin, gn.reshape(1, 256), r_rows)
    return o, jnp.transpose(rt.reshape(H_D, HEAD_DIM, HEAD_DIM, b), (3, 0, 1, 2))


def _s5_step_body(u_ref, x0_ref, a1_ref, a2_ref, bt_ref, ct_ref, d_ref, wg_ref, o_ref, xt_ref):
    u = u_ref[...]
    dot_hi = lambda a, b: jnp.dot(a, b, precision=HI, preferred_element_type=F32)
    x0 = x0_ref[...]
    n = x0.shape[0]
    even = (lax.broadcasted_iota(jnp.int32, x0.shape, 0) % 2) == 0
    partner = jnp.where(even, pltpu.roll(x0, n - 1, 0), pltpu.roll(x0, 1, 0))
    x = a1_ref[...] * x0 + a2_ref[...] * partner + dot_hi(bt_ref[...], u.T)
    xt_ref[...] = x
    y = dot_hi(ct_ref[...], x).T + d_ref[...] * u
    z = jax.nn.gelu(y)
    o_ref[...] = z * jax.nn.sigmoid(dot_hi(z, wg_ref[...]))


def _s5_step(u, x_all, layer, lw):
    b = u.shape[0]
    a1, a2, bt, ct, d_row = lw['s5_step']
    n = 2 * G_C * S5_P
    full = lambda *_: (0, 0)
    x_rows = jnp.transpose(x_all, (0, 2, 3, 4, 1)).reshape(-1, b)
    o, xt = pl.pallas_call(
        _s5_step_body,
        grid=(1,),
        in_specs=[pl.BlockSpec((b, C_C), full), pl.BlockSpec((n, b), lambda i: (layer, 0)), pl.BlockSpec((n, 1), full),
                  pl.BlockSpec((n, 1), full), pl.BlockSpec((n, C_C), full), pl.BlockSpec((C_C, n), full),
                  pl.BlockSpec((1, C_C), full), pl.BlockSpec((C_C, C_C), full)],
        out_specs=[pl.BlockSpec((b, C_C), full), pl.BlockSpec((n, b), full)],
        out_shape=[jax.ShapeDtypeStruct((b, C_C), F32), jax.ShapeDtypeStruct((n, b), F32)],
        compiler_params=_cp("arbitrary"),
        name="s5_step",
    )(u, x_rows, a1, a2, bt, ct, d_row, lw['s5_w_glu'])
    return o, jnp.transpose(xt.reshape(G_C, S5_P, 2, b), (3, 0, 1, 2))


def _s5_params(lw):
    lr, li = lw['s5_lambda_re'], lw['s5_lambda_im']
    dt = jnp.exp(lw['s5_log_step'])[:, None]
    mag = jnp.exp(lr * dt)
    ar, ai = mag * jnp.cos(li * dt), mag * jnp.sin(li * dt)
    nr, ni = ar - 1.0, ai
    den = lr * lr + li * li
    fr, fi = (nr * lr + ni * li) / den, (ni * lr - nr * li) / den
    b_re, b_im = lw['s5_b'][0], lw['s5_b'][1]
    bbr = fr[..., None] * b_re - fi[..., None] * b_im
    bbi = fr[..., None] * b_im + fi[..., None] * b_re
    eye = jnp.eye(G_C, dtype=F32)
    bd_in = lambda m: jnp.einsum('gpc,gh->gchp', m, eye).reshape(G_C * S5_CH, G_C * S5_P)
    bd_out = lambda m: jnp.einsum('gcp,gh->gphc', m, eye).reshape(G_C * S5_P, G_C * S5_CH)
    b_big = jnp.concatenate([bd_in(bbr), bd_in(bbi)], axis=1)
    c_big = jnp.concatenate([bd_out(lw['s5_c'][0]), -bd_out(lw['s5_c'][1])], axis=0)
    a_row = jnp.concatenate([ar.reshape(1, -1), ai.reshape(1, -1)], axis=1)
    d_row = lw['s5_d'].reshape(1, C_C)
    n = 2 * G_C * S5_P
    a1 = jnp.stack([ar, ar], axis=-1).reshape(n, 1)
    a2 = jnp.stack([-ai, ai], axis=-1).reshape(n, 1)
    bt = jnp.stack([jnp.einsum('gpc,gh->gphc', bbr, eye), jnp.einsum('gpc,gh->gphc', bbi, eye)], axis=2).reshape(n, G_C * S5_CH)
    ct = jnp.stack([jnp.einsum('gcp,gh->gchp', lw['s5_c'][0], eye), -jnp.einsum('gcp,gh->gchp', lw['s5_c'][1], eye)],
                   axis=-1).reshape(G_C * S5_CH, n)
    return (a_row, b_big, c_big, d_row), (a1, a2, bt, ct, d_row)


def _s5_body(u_ref, x0_ref, a_ref, b_ref, c_ref, d_ref, wg_ref, o_ref, xt_ref, x_scr, bu_scr, xs_scr, *, nb, tt, mm_dtype, prec):
    @pl.when(pl.program_id(0) == 0)
    def _():
        x_scr[...] = x0_ref[...]

    np_ = G_C * S5_P
    ncb = np_ // LANE
    for b in range(nb):
        bu = jnp.dot(u_ref[b].astype(mm_dtype), b_ref[...], precision=prec, preferred_element_type=F32)
        for cb in range(2 * ncb):
            bu_scr[cb, b * tt:(b + 1) * tt, :] = bu[:, cb * LANE:(cb + 1) * LANE]
    a = a_ref[...]

    def step(t, x):
        rows = pl.ds(t, nb, stride=tt)
        new = [None] * (2 * ncb)
        for cb in range(ncb):
            re, im = slice(cb * LANE, (cb + 1) * LANE), slice(np_ + cb * LANE, np_ + (cb + 1) * LANE)
            ar, ai, xr, xi = a[:, re], a[:, im], x[:, re], x[:, im]
            new[cb] = ar * xr - ai * xi + bu_scr[cb, rows, :]
            new[ncb + cb] = ar * xi + ai * xr + bu_scr[ncb + cb, rows, :]
            xs_scr[cb, rows, :] = new[cb]
            xs_scr[ncb + cb, rows, :] = new[ncb + cb]
        return jnp.concatenate(new, axis=1)

    x_last = lax.fori_loop(0, tt, step, x_scr[...], unroll=8)
    x_scr[...] = x_last
    xt_ref[...] = x_last
    for b in range(nb):
        u = u_ref[b]
        xs = jnp.concatenate([xs_scr[cb, b * tt:(b + 1) * tt, :] for cb in range(2 * ncb)], axis=1)
        y = jnp.dot(xs.astype(mm_dtype), c_ref[...], precision=prec, preferred_element_type=F32) + d_ref[...] * u
        z = jax.nn.gelu(y)
        o_ref[b] = z * jax.nn.sigmoid(jnp.dot(z.astype(mm_dtype), wg_ref[...], precision=prec, preferred_element_type=F32))


def _s5_mixer(u, x0, lw, b, t, tt, exact):
    a_row, b_big, c_big, d_row = lw['s5']
    mm_dtype = F32 if exact else BF16
    prec = HI if exact else None
    x0l = jnp.concatenate([x0[..., 0].reshape(b, -1), x0[..., 1].reshape(b, -1)], axis=1)
    np2 = 2 * G_C * S5_P
    fixed = lambda i: (0, 0)
    o, xt = pl.pallas_call(
        functools.partial(_s5_body, nb=b, tt=tt, mm_dtype=mm_dtype, prec=prec),
        grid=(t // tt,),
        in_specs=[pl.BlockSpec((b, tt, C_C), lambda i: (0, i, 0)), pl.BlockSpec((b, np2), fixed),
                  pl.BlockSpec((1, np2), fixed), pl.BlockSpec((C_C, np2), fixed), pl.BlockSpec((np2, C_C), fixed),
                  pl.BlockSpec((1, C_C), fixed), pl.BlockSpec((C_C, C_C), fixed)],
        out_specs=[pl.BlockSpec((b, tt, C_C), lambda i: (0, i, 0)), pl.BlockSpec((b, np2), fixed)],
        out_shape=[jax.ShapeDtypeStruct((b, t, C_C), F32), jax.ShapeDtypeStruct((b, np2), F32)],
        scratch_shapes=[pltpu.VMEM((b, np2), F32), pltpu.VMEM((np2 // LANE, b * tt, LANE), F32),
                        pltpu.VMEM((np2 // LANE, b * tt, LANE), F32)],
        compiler_params=_cp("arbitrary"),
        name="s5",
    )(u.reshape(b, t, C_C), x0l, a_row, b_big.astype(mm_dtype), c_big.astype(mm_dtype), d_row, lw['s5_w_glu'].astype(mm_dtype))
    xt = xt.reshape(b, 2, G_C, S5_P)
    return o.reshape(b * t, C_C), jnp.stack([xt[:, 0], xt[:, 1]], axis=-1)


def _ret_tables(pos, c):
    cos, sin = _rope_tables(pos, HEAD_DIM, RET_THETA, HEAD_DIM, H_D)
    log_g = jnp.log1p(-jnp.exp2(-5.0 - jnp.arange(H_D, dtype=F32)))
    i = jnp.arange(c, dtype=F32)
    diff = i[:, None] - i[None, :]
    dmat = jnp.where(diff >= 0, jnp.exp(jnp.maximum(diff, 0.0)[None] * log_g[:, None, None]), 0.0).reshape(H_D * c, c)
    q_dec = jnp.repeat(jnp.exp((i + 1.0)[None] * log_g[:, None]).T, HEAD_DIM, axis=1)
    k_dec = jnp.repeat(jnp.exp((c - 1.0 - i)[None] * log_g[:, None]).T, HEAD_DIM, axis=1)
    chunk_dec = jnp.repeat(jnp.exp(c * log_g), HEAD_DIM).reshape(256, 1)
    return cos, sin, dmat, q_dec, k_dec, chunk_dec


def _ret_body(c_ref, cos_ref, sin_ref, dmat_ref, qdec_ref, kdec_ref, cdec_ref, r0_ref, gn_ref, o_ref, rt_ref, r_scr, *, c):
    @pl.when(pl.program_id(1) == 0)
    def _():
        r_scr[...] = r0_ref[0]

    x = c_ref[0]
    q, k, v, g = x[:, 0:256], x[:, 256:512], x[:, 512:768], x[:, 768:1024]
    cs, sn = cos_ref[...], sin_ref[...]
    lane = lax.broadcasted_iota(jnp.int32, (c, 256), 1)
    first = (lane % HEAD_DIM) < (HEAD_DIM // 2)

    def rope(z):
        sw = jnp.where(first, pltpu.roll(z, 256 - HEAD_DIM // 2, 1), pltpu.roll(z, HEAD_DIM // 2, 1))
        return z * cs + sw * sn

    q = rope(q)
    k = rope(k) * (HEAD_DIM ** -0.5)
    head = lane // HEAD_DIM
    kb, vb = k.astype(BF16), v.astype(BF16)
    qstack = jnp.concatenate([jnp.where(head == h, q, 0.0) for h in range(H_D)], axis=0).astype(BF16)
    s = lax.dot_general(qstack, kb, (((1,), (1,)), ((), ())), preferred_element_type=F32) * dmat_ref[...]
    pv = jnp.dot(s.astype(BF16), vb, preferred_element_type=F32)
    inner = jnp.zeros((c, 256), F32)
    for h in range(H_D):
        inner = inner + jnp.where(head == h, pv[h * c:(h + 1) * c], 0.0)
    r_old = r_scr[...]
    cross = jnp.dot((q * qdec_ref[...]).astype(BF16), r_old.astype(BF16), preferred_element_type=F32)
    kv = lax.dot_general((k * kdec_ref[...]).astype(BF16), vb, (((0,), (0,)), ((), ())), preferred_element_type=F32)
    bd = _block_ones(256, HEAD_DIM, F32)
    r_new = cdec_ref[...] * r_old + kv * bd
    r_scr[...] = r_new
    rt_ref[0] = r_new
    o = inner + cross
    avg = bd * (1.0 / HEAD_DIM)
    oc = o - jnp.dot(o, avg, precision=HI, preferred_element_type=F32)
    on = oc * lax.rsqrt(jnp.dot(oc * oc, avg, precision=HI, preferred_element_type=F32) + RET_GN_EPS)
    o_ref[0] = jax.nn.silu(g) * (on * gn_ref[...])


def _ret_mixer(colsd, r0, lw, tabs, b, t):
    c = RET_CHUNK if t % RET_CHUNK == 0 else t
    cos, sin, dmat, q_dec, k_dec, chunk_dec = tabs
    eye = jnp.eye(H_D, dtype=F32)
    r0l = jnp.einsum('bhde,hg->bhdge', r0, eye).reshape(b, 256, 256)
    n_t = t // c
    fixed = lambda i, j: (0, 0)
    o, rt = pl.pallas_call(
        functools.partial(_ret_body, c=c),
        grid=(b, n_t),
        in_specs=[pl.BlockSpec((1, c, 1024), lambda i, j: (i, j, 0)),
                  pl.BlockSpec((c, 256), lambda i, j: (j, 0)), pl.BlockSpec((c, 256), lambda i, j: (j, 0)),
                  pl.BlockSpec((H_D * c, c), fixed), pl.BlockSpec((c, 256), fixed), pl.BlockSpec((c, 256), fixed),
                  pl.BlockSpec((256, 1), fixed), pl.BlockSpec((1, 256, 256), lambda i, j: (i, 0, 0)),
                  pl.BlockSpec((1, 256), fixed)],
        out_specs=[pl.BlockSpec((1, c, 256), lambda i, j: (i, j, 0)), pl.BlockSpec((1, 256, 256), lambda i, j: (i, 0, 0))],
        out_shape=[jax.ShapeDtypeStruct((b, t, 256), F32), jax.ShapeDtypeStruct((b, 256, 256), F32)],
        scratch_shapes=[pltpu.VMEM((256, 256), F32)],
        compiler_params=_cp("parallel", "arbitrary"),
        name="retention",
    )(colsd.reshape(b, t, 1024), cos, sin, dmat, q_dec, k_dec, chunk_dec, r0l, lw['ret_gn'].reshape(1, 256))
    rt = jnp.einsum('bhdge,hg->bhde', rt.reshape(b, H_D, HEAD_DIM, H_D, HEAD_DIM), eye)
    return o.reshape(b * t, 256), rt


def _partner(x, d, period):
    pos = lax.broadcasted_iota(jnp.int32, x.shape, 1) % period
    return jnp.where(pos + d < period, pltpu.roll(x, LANE - d, 1), pltpu.roll(x, period - d, 1))


def _out_body(x_ref, oa_ref, ob_ref, oc_ref, od_ref, w_ref, nw_ref, wr_ref, br_ref, x1_ref, h_ref, comb_ref):
    acc = x_ref[...]
    for i, ref in enumerate((oa_ref, ob_ref, oc_ref, od_ref)):
        acc = acc + jnp.dot(ref[...].astype(BF16), w_ref[256 * i:256 * (i + 1), :], preferred_element_type=F32)
    x1_ref[...] = acc
    h = acc * lax.rsqrt(jnp.mean(acc * acc, axis=-1, keepdims=True) + RMS_EPS) * nw_ref[...]
    hb = h.astype(BF16)
    h_ref[...] = hb
    h_lo = (h - hb.astype(F32)).astype(BF16)
    logits = (jnp.dot(hb, wr_ref[0], preferred_element_type=F32) + jnp.dot(hb, wr_ref[1], preferred_element_type=F32)
              + jnp.dot(h_lo, wr_ref[0], preferred_element_type=F32) + br_ref[...])
    le, lg = logits[:, :LANE], logits[:, LANE:]
    lane = lax.broadcasted_iota(jnp.int32, le.shape, 1)
    mg = jnp.max(lg, axis=-1, keepdims=True)
    eg = jnp.exp(lg - mg)
    pg = eg / (jnp.sum(eg, axis=-1, keepdims=True) * (1.0 / 32.0))
    gidx = (lane % N_EXPERTS) // EXP_PER_GROUP
    g_rank = jnp.zeros_like(pg)
    for d in range(1, N_GROUPS):
        other = pltpu.roll(pg, LANE - EXP_PER_GROUP * d, 1)
        wrapped = gidx + d >= N_GROUPS
        g_rank = g_rank + jnp.where((other > pg) | ((other == pg) & wrapped), 1.0, 0.0)
    kidx = lane % EXP_PER_GROUP
    others = [_partner(le, d, EXP_PER_GROUP) for d in range(1, EXP_PER_GROUP)]
    me = functools.reduce(jnp.maximum, others, le)
    ee = jnp.exp(le - me)
    se = ee
    for d in range(1, EXP_PER_GROUP):
        se = se + _partner(ee, d, EXP_PER_GROUP)
    pe = ee / se
    e_rank = jnp.zeros_like(pe)
    for d in range(1, EXP_PER_GROUP):
        other = _partner(pe, d, EXP_PER_GROUP)
        wrapped = kidx + d >= EXP_PER_GROUP
        e_rank = e_rank + jnp.where((other > pe) | ((other == pe) & wrapped), 1.0, 0.0)
    top = jnp.where(e_rank < 2.0, pe, 0.0)
    den = top
    for d in range(1, EXP_PER_GROUP):
        den = den + _partner(top, d, EXP_PER_GROUP)
    comb = jnp.where((g_rank < 1.0) & (lane < N_EXPERTS), pg * (top / den), 0.0)
    comb_ref[...] = comb


def _out_router(x, oa, ob, oc, od, lw, tm):
    n = x.shape[0]
    row = lambda i: (i, 0)
    fixed = lambda i: (0, 0)
    mix = pl.BlockSpec((tm, 256), row)
    return pl.pallas_call(
        _out_body,
        grid=(n // tm,),
        in_specs=[pl.BlockSpec((tm, D_MODEL), row), mix, mix, mix, mix,
                  pl.BlockSpec((D_MODEL, D_MODEL), fixed), pl.BlockSpec((1, D_MODEL), fixed),
                  pl.BlockSpec((2, D_MODEL, 2 * LANE), lambda i: (0, 0, 0)), pl.BlockSpec((1, 2 * LANE), fixed)],
        out_specs=[pl.BlockSpec((tm, D_MODEL), row), pl.BlockSpec((tm, D_MODEL), row), pl.BlockSpec((tm, LANE), row)],
        out_shape=[jax.ShapeDtypeStruct((n, D_MODEL), F32), jax.ShapeDtypeStruct((n, D_MODEL), BF16),
                   jax.ShapeDtypeStruct((n, LANE), F32)],
        compiler_params=_cp("parallel"),
        name="out_router",
    )(x, oa, ob, oc, od, lw['w_out'], lw['norm_ffn'], lw['w_router'], lw['b_router'])


def _router_weights(w_grp, b_grp, w_exp, b_exp):
    we = jnp.transpose(w_exp, (1, 0, 2)).reshape(D_MODEL, N_EXPERTS)
    wg = jnp.repeat(w_grp, EXP_PER_GROUP, axis=1)
    reps = LANE // N_EXPERTS
    w = jnp.concatenate([jnp.tile(we, (1, reps)), jnp.tile(wg, (1, reps))], axis=1)
    b = jnp.concatenate([jnp.tile(b_exp.reshape(1, N_EXPERTS), (1, reps)),
                         jnp.tile(jnp.repeat(b_grp, EXP_PER_GROUP).reshape(1, N_EXPERTS), (1, reps))], axis=1)
    w_hi = w.astype(BF16)
    return jnp.stack([w_hi, (w - w_hi.astype(F32)).astype(BF16)]), b


def _moe_body(h_ref, comb_ref, x1_ref, wg_ref, wu_ref, wd_ref, nf_ref, *out_refs, final):
    acc_ref = out_refs[-1]
    e = pl.program_id(1)

    @pl.when(e == 0)
    def _():
        acc_ref[...] = x1_ref[...]

    h = h_ref[...]
    comb = comb_ref[...]
    lane = lax.broadcasted_iota(jnp.int32, comb.shape, 1)
    c = jnp.sum(jnp.where(lane == e, comb, 0.0), axis=-1, keepdims=True)
    hg = jnp.dot(h, wg_ref[0], preferred_element_type=F32)
    hu = jnp.dot(h, wu_ref[0], preferred_element_type=F32)
    act = (jax.nn.silu(hg) * hu * c).astype(BF16)
    acc_ref[...] += jnp.dot(act, wd_ref[0], preferred_element_type=F32)

    @pl.when(e == N_EXPERTS - 1)
    def _():
        x2 = acc_ref[...]
        if final:
            out_refs[0][...] = x2 * lax.rsqrt(jnp.mean(x2 * x2, axis=-1, keepdims=True) + RMS_EPS) * nf_ref[...]
        else:
            out_refs[0][...] = x2


def _moe(h, comb, x1, lw, norm_final, tm, final):
    n = x1.shape[0]
    row = lambda i, e: (i, 0)
    per_e = lambda i, e: (e, 0, 0)
    return pl.pallas_call(
        functools.partial(_moe_body, final=final),
        grid=(n // tm, N_EXPERTS),
        in_specs=[pl.BlockSpec((tm, D_MODEL), row), pl.BlockSpec((tm, LANE), row), pl.BlockSpec((tm, D_MODEL), row),
                  pl.BlockSpec((1, D_MODEL, D_EXPERT), per_e), pl.BlockSpec((1, D_MODEL, D_EXPERT), per_e),
                  pl.BlockSpec((1, D_EXPERT, D_MODEL), per_e), pl.BlockSpec((1, D_MODEL), lambda i, e: (0, 0))],
        out_specs=pl.BlockSpec((tm, D_MODEL), row),
        out_shape=jax.ShapeDtypeStruct((n, D_MODEL), F32),
        scratch_shapes=[pltpu.VMEM((tm, D_MODEL), F32)],
        compiler_params=_cp("parallel", "arbitrary"),
        name="moe",
    )(h, comb, x1, lw['moe_wg'], lw['moe_wu'], lw['moe_wd'], norm_final.reshape(1, D_MODEL))


def _head_norm(y, eps):
    yc = y - jnp.mean(y, -1, keepdims=True)
    return yc * lax.rsqrt(jnp.mean(yc * yc, -1, keepdims=True) + eps)


def _masked_softmax(s, mask):
    s = jnp.where(mask, s.astype(F32), NEG_INF)
    return jnp.where(mask, jax.nn.softmax(s, axis=-1), 0.0)


def _attend(q, k, v, mask):
    s = jnp.einsum('...qgrd,...kgd->...qgrk', q, k) * HEAD_DIM ** -0.5
    p = _masked_softmax(s, mask)
    return jnp.einsum('...qgrk,...kgd->...qgrd', p, v.astype(F32))


def _x_nsa_compress(rows, w1, b1, w2):
    B, Tk = rows.shape[:2]
    m = CMP_BLOCK // CMP_STRIDE
    n_cmp = (Tk - CMP_BLOCK) // CMP_STRIDE + 1
    n_chunks = n_cmp + m - 1
    chunks = rows[:, :n_chunks * CMP_STRIDE].reshape(B, n_chunks, CMP_STRIDE, N_KV_A, HEAD_DIM)
    w1r = w1.reshape(m, CMP_STRIDE, HEAD_DIM, CMP_HIDDEN)
    h = b1.astype(F32)
    for j in range(m):
        h = h + jnp.einsum('bcsgd,sdf->bcgf', chunks[:, j:j + n_cmp], w1r[j])
    return jnp.einsum('bcgf,fd->bcgd', jax.nn.gelu(h), w2)


def _x_nsa_sample(q, kv, gate, pos, cmp_w1, cmp_b1, cmp_w2, past_rows, win_buf):
    B = q.shape[0]
    T = 1
    q = q.reshape(B, T, N_KV_A, R_A, HEAD_DIM)
    kv = kv.reshape(B, T, 6, N_KV_A, HEAD_DIM)
    rows = kv[:, :, 0:4]
    win_rows = kv[:, :, 4:6]
    full = jnp.concatenate([past_rows, rows], axis=1)
    Tk = full.shape[1]
    kc = _x_nsa_compress(full[:, :, 0], cmp_w1[0], cmp_b1[0], cmp_w2[0])
    vc = _x_nsa_compress(full[:, :, 1], cmp_w1[1], cmp_b1[1], cmp_w2[1])
    n_sel = -(-Tk // SEL_BLOCK)
    sel = jnp.pad(full[:, :, 2:4], ((0, 0), (0, n_sel * SEL_BLOCK - Tk), (0, 0), (0, 0), (0, 0)))
    sel = sel.reshape(B, n_sel, SEL_BLOCK, 2, N_KV_A, HEAD_DIM).transpose(3, 0, 4, 1, 2, 5)
    n_cmp = kc.shape[1]
    ov = _cmp_to_sel_t(n_cmp, n_cmp, n_sel).T
    q_pos = jnp.asarray(pos, jnp.int32)
    ks_blk, vs_blk = sel[0], sel[1]
    cmp_end = jnp.arange(n_cmp) * CMP_STRIDE + CMP_BLOCK - 1
    s = jnp.einsum('bqgrd,bngd->bqgrn', q, kc) * HEAD_DIM ** -0.5
    p_cmp = _masked_softmax(s, (cmp_end[None, :] <= q_pos[:, None])[None, :, None, None, :])
    o_c = jnp.einsum('bqgrn,bngd->bqgrd', p_cmp, vc.astype(F32))
    imp = jnp.einsum('bqgrn,nj->bqgj', p_cmp, ov)
    blk = jnp.arange(n_sel)[None, :]
    cur = (q_pos // SEL_BLOCK)[:, None]
    forced = (blk == 0) | (blk == cur) | (blk == cur - 1)
    causal = blk * SEL_BLOCK <= q_pos[:, None]
    imp = jnp.where(forced[None, :, None, :], FORCED_SCORE, jnp.where(causal[None, :, None, :], imp, BLOCKED_SCORE))
    _, idx = lax.top_k(imp, min(TOP_K, n_sel))
    n_k = idx.shape[-1]
    bi = jnp.arange(B)[:, None, None, None]
    gi = jnp.arange(N_KV_A)[None, None, :, None]
    ksel = ks_blk[bi, gi, idx]
    vsel = vs_blk[bi, gi, idx]
    s2 = jnp.einsum('bqgrd,bqgksd->bqgrks', q, ksel) * HEAD_DIM ** -0.5
    kpos = idx[..., None] * SEL_BLOCK + jnp.arange(SEL_BLOCK)
    mask2 = (kpos <= q_pos[None, :, None, None, None]).reshape(B, T, N_KV_A, 1, n_k * SEL_BLOCK)
    p2 = _masked_softmax(s2.reshape(B, T, N_KV_A, R_A, n_k * SEL_BLOCK), mask2)
    o_s = jnp.einsum('bqgrm,bqgmd->bqgrd', p2, vsel.reshape(B, T, N_KV_A, n_k * SEL_BLOCK, HEAD_DIM).astype(F32))
    wb = win_buf.shape[1]
    kw = jnp.concatenate([win_buf, win_rows], axis=1)
    kpos = int(pos[0]) - wb + np.arange(wb + T)
    diff = pos[:, None] - kpos[None, :]
    mask = (diff >= 0) & (diff < WINDOW)
    o_w = _attend(q, kw[:, :, 0], kw[:, :, 1], mask[None, :, None, None, :])
    new_win = kw[:, wb + T - min(WINDOW, wb + T):]
    g = jax.nn.sigmoid(gate.astype(F32)).reshape(B, T, N_KV_A, R_A, 3)
    o = g[..., 0:1] * o_c + g[..., 1:2] * o_s + g[..., 2:3] * o_w
    return o.reshape(B, C_A), rows, new_win


def _x_rwkv_sample(cols, shift0, S0, mu, vec, w_up, a_up, g_up):
    B = cols.shape[0]
    xs = cols + mu * (shift0 - cols)
    r, k, v, wd, ad, gd = jnp.split(xs, _offsets((C_B, C_B, C_B, LORA_W, LORA_A, LORA_G)), axis=-1)
    w0, a0, k_k, k_a, r_k, lnx_w, lnx_b = vec
    w_log = -jax.nn.softplus(-(w0 + jnp.tanh(wd) @ w_up)) - 0.5
    decay = jnp.exp(-jnp.exp(w_log))
    a = jax.nn.sigmoid(a0 + ad @ a_up)
    g = jax.nn.sigmoid(gd) @ g_up
    hs = lambda z: z.reshape(B, H_B, HEAD_DIM)
    kk = hs(k * k_k)
    kk = kk * lax.rsqrt(jnp.sum(kk * kk, -1, keepdims=True) + 1e-12)
    k = k * (1.0 + (a - 1.0) * k_a)
    r_h, w_h, k_h, v_h, a_h = hs(r), hs(decay), hs(k), hs(v), hs(a)
    sk = jnp.einsum('bhij,bhj->bhi', S0, kk)
    S = S0 * w_h[:, :, None, :] - sk[..., None] * (kk * a_h)[:, :, None, :] + v_h[..., None] * k_h[:, :, None, :]
    y = jnp.einsum('bhij,bhj->bhi', S, r_h)
    y = _head_norm(y, RWKV_GN_EPS).reshape(B, C_B) * lnx_w + lnx_b
    bonus = jnp.sum(r_h * k_h * r_k.reshape(H_B, HEAD_DIM), -1, keepdims=True) * v_h
    return (y + bonus.reshape(B, C_B)) * g, S, cols


def _x_s5_sample(u, x0, lw):
    a_row, b_big, c_big, d_row = lw['s5']
    b = u.shape[0]
    np_ = G_C * S5_P
    x0l = jnp.concatenate([x0[..., 0].reshape(b, -1), x0[..., 1].reshape(b, -1)], axis=1)
    bu = jnp.dot(u, b_big, precision=HI)
    ar, ai = a_row[:, :np_], a_row[:, np_:]
    xr = ar * x0l[:, :np_] - ai * x0l[:, np_:] + bu[:, :np_]
    xi = ar * x0l[:, np_:] + ai * x0l[:, :np_] + bu[:, np_:]
    y = jnp.dot(jnp.concatenate([xr, xi], axis=1), c_big, precision=HI) + d_row * u
    z = jax.nn.gelu(y)
    out = z * jax.nn.sigmoid(jnp.dot(z, lw['s5_w_glu'], precision=HI))
    return out, jnp.stack([xr.reshape(b, G_C, S5_P), xi.reshape(b, G_C, S5_P)], axis=-1)


def _x_ret_sample(cols, cos, sin, R0, gn_w):
    B = cols.shape[0]
    q, k, v, g = jnp.split(cols, 4, axis=-1)
    half = HEAD_DIM // 2

    def rope(z):
        z = z.reshape(B, H_D, HEAD_DIM)
        sw = jnp.concatenate([z[..., half:], z[..., :half]], -1)
        return z * cos.reshape(B, H_D, HEAD_DIM) + sw * sin.reshape(B, H_D, HEAD_DIM)

    q = rope(q)
    k = rope(k) * HEAD_DIM ** -0.5
    v = v.reshape(B, H_D, HEAD_DIM)
    gamma = 1.0 - jnp.exp2(-5.0 - jnp.arange(H_D, dtype=F32))
    log_g = jnp.log1p(-jnp.exp2(-5.0 - jnp.arange(H_D, dtype=F32)))
    inner = jnp.einsum('bhd,bhd->bh', q, k)[..., None] * v
    cross = jnp.einsum('bhd,h,bhde->bhe', q, jnp.exp(log_g), R0)
    R = jnp.exp(log_g)[None, :, None, None] * R0 + jnp.einsum('bhd,bhe->bhde', k, v)
    del gamma
    o = _head_norm(inner + cross, RET_GN_EPS).reshape(B, C_D) * gn_w
    return jax.nn.silu(g) * o, R


def _prep_layer(l, p):
    w_in = p['w_in'][l]
    o = _offsets(SPLIT_SIZES)
    segs = jnp.split(w_in, o, axis=1)
    w_all = jnp.concatenate([segs[0], segs[1], _pad_to(segs[2], LANE, 1), _pad_to(segs[3], SHIFT_PAD, 1), segs[4], segs[5]],
                            axis=1).astype(BF16)
    w_prec = w_in[:, :PRECISE_COLS]
    w_lo = (w_prec - w_prec.astype(BF16).astype(F32)).astype(BF16)
    lw = {'layer': l, 'w_all': w_all, 'w_lo': w_lo, 'norm_mix': p['norm_mix'][l]}
    lw['cmp'] = _cmp_weights(p['nsa_cmp_w1'][l], p['nsa_cmp_b1'][l], p['nsa_cmp_w2'][l])
    lw['cmp_raw'] = (p['nsa_cmp_w1'][l], p['nsa_cmp_b1'][l], p['nsa_cmp_w2'][l])
    lw['rwkv_mu'] = _pad_to(p['rwkv_mu'][l].reshape(1, SHIFT_B), SHIFT_PAD, 1)
    lw['rwkv_vec'] = _pad_to(p['rwkv_vec'][l], 8, 0)
    z = lambda a, b: jnp.zeros((a, b), F32)
    lw['rwkv_wup'] = jnp.concatenate([p['rwkv_w_up'][l], z(LANE - LORA_W, C_B)], axis=0)
    lw['rwkv_aup'] = jnp.concatenate([z(LORA_W, C_B), p['rwkv_a_up'][l], z(LANE - LORA_W - LORA_A, C_B)], axis=0)
    lw['rwkv_gup'] = jnp.concatenate([z(LORA_W + LORA_A, C_B), p['rwkv_g_up'][l], z(LANE - LORA_W - LORA_A - LORA_G, C_B)], axis=0)
    lw['rwkv_raw'] = (p['rwkv_mu'][l], p['rwkv_vec'][l], p['rwkv_w_up'][l], p['rwkv_a_up'][l], p['rwkv_g_up'][l])
    for name in ('s5_lambda_re', 's5_lambda_im', 's5_b', 's5_c', 's5_d', 's5_log_step', 's5_w_glu', 'ret_gn'):
        lw[name] = p[name][l]
    lw['s5'], lw['s5_step'] = _s5_params(lw)
    lw['w_out'] = p['w_out'][l].astype(BF16)
    lw['norm_ffn'] = p['norm_ffn'][l].reshape(1, D_MODEL)
    lw['w_router'], lw['b_router'] = _router_weights(p['moe_w_grp'][l], p['moe_b_grp'][l], p['moe_w_exp'][l], p['moe_b_exp'][l])
    lw['moe_wg'] = p['moe_w_gate'][l].astype(BF16)
    lw['moe_wu'] = p['moe_w_up'][l].astype(BF16)
    lw['moe_wd'] = p['moe_w_down'][l].astype(BF16)
    return lw


def _prompt_layer(x, lw, tabs, b, t, norm_final):
    cos_a, sin_a, ret_tabs = tabs
    q, kv, gate, colsb, u, colsd = _proj(x, lw['norm_mix'], lw['w_all'], lw['w_lo'], cos_a, sin_a, 512)
    o_a = _nsa_prompt_mixer(q, kv, gate, lw, b, t)
    o_b, s_rwkv, s_shift = _rwkv_mixer(colsb, jnp.zeros((b, SHIFT_PAD), F32), jnp.zeros((b, H_B, HEAD_DIM, HEAD_DIM), F32),
                                       lw, b, t, 512, b, 256)
    o_c, s_s5 = _s5_mixer(u, jnp.zeros((b, G_C, S5_P, 2), F32), lw, b, t, 256, False)
    o_d, s_ret = _ret_mixer(colsd, jnp.zeros((b, H_D, HEAD_DIM, HEAD_DIM), F32), lw, ret_tabs, b, t)
    x1, h, comb = _out_router(x, o_a, o_b, o_c, o_d, lw, 512)
    x2 = _moe(h, comb, x1, lw, norm_final, 1024, lw['layer'] == DEPTH - 1)
    kv3 = kv.reshape(b, t, 768)
    rows = kv3[:, :, :512].reshape(b, t, 4, N_KV_A, HEAD_DIM)
    win = kv3[:, t - min(WINDOW, t):, 512:].reshape(b, min(WINDOW, t), 2, N_KV_A, HEAD_DIM)
    return x2, (rows, win, s_rwkv, s_shift, s_s5, s_ret)


def _sample_layer(x, lw, tabs, b, pos, cache_kv, page_table, win_buf, s_rwkv, s_shift, s_s5, s_ret, norm_final):
    cos_a, sin_a, ret_cs = tabs
    q, kv, gate, colsb, u, colsd = _proj(x, lw['norm_mix'], lw['w_all'], lw['w_lo'], cos_a, sin_a, b)
    o_a, win = _nsa_sample(q, kv, gate, cache_kv, win_buf, lw['layer'], page_table, lw['cmp'], int(pos[0]))
    rows = kv[:, :512].reshape(b, 1, 4, N_KV_A, HEAD_DIM)
    o_b, s_rwkv = _rwkv_step(colsb, _pad_to(s_shift, SHIFT_PAD, 1), s_rwkv, lw['layer'], lw)
    s_shift = colsb[:, :SHIFT_B]
    o_c, s_s5 = _s5_step(u, s_s5, lw['layer'], lw)
    o_d, s_ret = _ret_step(colsd, ret_cs[0], ret_cs[1], s_ret, lw['layer'], lw['ret_gn'])
    x1, h, comb = _out_router(x, o_a, o_b, o_c, o_d, lw, b)
    x2 = _moe(h, comb, x1, lw, norm_final, b, lw['layer'] == DEPTH - 1)
    return x2, (rows, win, s_rwkv, s_shift, s_s5, s_ret)


def kernel(x_prompt, x_sample, cache_nsa_kv, cache_nsa_win, state_rwkv, state_rwkv_shift, state_s5, state_ret, page_table, norm_mix, w_in, nsa_cmp_w1, nsa_cmp_b1, nsa_cmp_w2, rwkv_mu, rwkv_vec, rwkv_w_up, rwkv_a_up, rwkv_g_up, s5_lambda_re, s5_lambda_im, s5_b, s5_c, s5_d, s5_log_step, s5_w_glu, ret_gn, w_out, norm_ffn, moe_w_grp, moe_b_grp, moe_w_exp, moe_b_exp, moe_w_gate, moe_w_up, moe_w_down, norm_final):
    p = dict(norm_mix=norm_mix, w_in=w_in, nsa_cmp_w1=nsa_cmp_w1, nsa_cmp_b1=nsa_cmp_b1, nsa_cmp_w2=nsa_cmp_w2,
             rwkv_mu=rwkv_mu, rwkv_vec=rwkv_vec, rwkv_w_up=rwkv_w_up, rwkv_a_up=rwkv_a_up, rwkv_g_up=rwkv_g_up,
             s5_lambda_re=s5_lambda_re, s5_lambda_im=s5_lambda_im, s5_b=s5_b, s5_c=s5_c, s5_d=s5_d,
             s5_log_step=s5_log_step, s5_w_glu=s5_w_glu, ret_gn=ret_gn, w_out=w_out, norm_ffn=norm_ffn,
             moe_w_grp=moe_w_grp, moe_b_grp=moe_b_grp, moe_w_exp=moe_w_exp, moe_b_exp=moe_b_exp,
             moe_w_gate=moe_w_gate, moe_w_up=moe_w_up, moe_w_down=moe_w_down)
    bp, tp = x_prompt.shape[:2]
    bs, ts = x_sample.shape[:2]
    past_len = page_table.shape[1] * cache_nsa_kv.shape[2]
    pos_p = np.arange(tp)
    pos_s = past_len + np.arange(ts)
    c = RET_CHUNK if tp % RET_CHUNK == 0 else tp
    tabs_p = _rope_tables(pos_p, ROT_DIM, ROPE_THETA, HEAD_DIM, 2) + (_ret_tables(pos_p, c),)
    pos_rows = np.repeat(pos_s, bs)
    tabs_s = _rope_tables(pos_rows, ROT_DIM, ROPE_THETA, HEAD_DIM, 2) + (_rope_tables(pos_rows, HEAD_DIM, RET_THETA, HEAD_DIM, H_D),)
    xp = x_prompt.reshape(bp * tp, D_MODEL)
    xs = x_sample.reshape(bs * ts, D_MODEL)
    sts_p, sts_s = [], []
    for l in range(DEPTH):
        lw = _prep_layer(l, p)
        xp, st_p = _prompt_layer(xp, lw, tabs_p, bp, tp, norm_final)
        xs, st_s = _sample_layer(xs, lw, tabs_s, bs, pos_s, cache_nsa_kv, page_table, cache_nsa_win, state_rwkv,
                                     state_rwkv_shift[l], state_s5, state_ret, norm_final)
        rows, win, s1, s2, s3, s4 = st_s
        sts_s.append((rows, win, s1, s2, s3, s4))
        sts_p.append(st_p)
    new_p = [jnp.stack([st[i] for st in sts_p]) for i in range(6)]
    new_s = [jnp.stack([st[i] for st in sts_s]) for i in range(6)]
    return (xp.reshape(bp, tp, D_MODEL), xs.reshape(bs, ts, D_MODEL), new_p[0], new_s[0], new_p[1], new_s[1],
            new_p[2], new_s[2], new_p[3], new_s[3], new_p[4], new_s[4], new_p[5], new_s[5])
```

```python
import functools

import numpy as np
import jax
import jax.numpy as jnp
from jax import lax
from jax.experimental import pallas as pl
from jax.experimental.pallas import tpu as pltpu

F32 = jnp.float32
BF16 = jnp.bfloat16
HI = lax.Precision.HIGHEST

D_MODEL = 1024
DEPTH = 2
HEAD_DIM = 64
C_A = C_B = C_C = C_D = 256
H_A = 4
N_KV_A = 2
R_A = 2
ROT_DIM = 16
ROPE_THETA = 500000.0
CMP_BLOCK = 32
CMP_STRIDE = 16
CMP_HIDDEN = 128
SEL_BLOCK = 64
TOP_K = 16
WINDOW = 512
NEG_INF = -1e30
FORCED_SCORE = 1e9
BLOCKED_SCORE = -1e9
H_B = 4
LORA_W = 16
LORA_A = 16
LORA_G = 32
SHIFT_B = 832
SHIFT_PAD = 896
RWKV_GN_EPS = 64e-5
S5_CH = 16
G_C = 16
S5_P = 64
H_D = 4
RET_CHUNK = 128
RET_THETA = 10000.0
RET_GN_EPS = 1e-5
N_GROUPS = 4
EXP_PER_GROUP = 4
N_EXPERTS = 16
D_EXPERT = 256
RMS_EPS = 1e-6
SPLIT_SIZES = (C_A, 6 * N_KV_A * HEAD_DIM, 3 * H_A, SHIFT_B, C_C, 4 * C_D)
LANE = 128
VMEM_LIMIT = 56 * 1024 * 1024


def _cp(*sem):
    return pltpu.CompilerParams(dimension_semantics=sem, vmem_limit_bytes=VMEM_LIMIT)


def _offsets(sizes):
    return [int(s) for s in np.cumsum(sizes)[:-1]]


def _pad_to(a, n, axis):
    pad = [(0, 0)] * a.ndim
    pad[axis] = (0, n - a.shape[axis])
    return jnp.pad(a, pad)


def _block_ones(n, blk, dtype):
    r = lax.broadcasted_iota(jnp.int32, (n, n), 0) // blk
    c = lax.broadcasted_iota(jnp.int32, (n, n), 1) // blk
    return (r == c).astype(dtype)


def _rope_tables(pos, rot_dim, theta, period, reps):
    half = rot_dim // 2
    inv = theta ** (-jnp.arange(half, dtype=F32) / half)
    ang = jnp.asarray(pos, F32)[:, None] * inv[None, :]
    cos, sin = jnp.cos(ang), jnp.sin(ang)
    n = ang.shape[0]
    rest = period - rot_dim
    c = jnp.concatenate([cos, cos, jnp.ones((n, rest), F32)], -1)
    s = jnp.concatenate([-sin, sin, jnp.zeros((n, rest), F32)], -1)
    return jnp.tile(c, (1, reps)), jnp.tile(s, (1, reps))


PRECISE_COLS = 384


def _proj_body(x_ref, nw_ref, w_ref, wlo_ref, cos_ref, sin_ref, q_ref, rows_ref, wrows_ref, g_ref, cb_ref, u_ref, cd_ref):
    x = x_ref[...]
    h = x * lax.rsqrt(jnp.mean(x * x, axis=-1, keepdims=True) + RMS_EPS) * nw_ref[...]
    hb = h.astype(BF16)
    h_lo = (h - hb.astype(F32)).astype(BF16)
    c = cos_ref[...]
    s = sin_ref[...]
    first = (lax.broadcasted_iota(jnp.int32, c.shape, 1) % HEAD_DIM) < (ROT_DIM // 2)

    def rope(z):
        sw = jnp.where(first, pltpu.roll(z, LANE - ROT_DIM // 2, 1), pltpu.roll(z, ROT_DIM // 2, 1))
        return z * c + sw * s

    def dot(a, b):
        z = jnp.dot(hb, w_ref[:, a:b], preferred_element_type=F32)
        if b <= PRECISE_COLS:
            z = z + (jnp.dot(hb, wlo_ref[:, a:b], preferred_element_type=F32)
                     + jnp.dot(h_lo, w_ref[:, a:b], preferred_element_type=F32))
        return z

    for j in range(2):
        q_ref[:, LANE * j:LANE * (j + 1)] = rope(dot(LANE * j, LANE * (j + 1)))
    for j in range(6):
        z = dot(256 + LANE * j, 256 + LANE * (j + 1))
        dst, jj = (rows_ref, j) if j < 4 else (wrows_ref, j - 4)
        dst[:, LANE * jj:LANE * (jj + 1)] = rope(z) if j % 2 == 0 else z
    g_ref[...] = dot(1024, 1152)
    cb_ref[...] = dot(1152, 2048)
    u_ref[...] = dot(2048, 2304)
    cd_ref[...] = dot(2304, 3328)


def _proj(x2d, norm_w, w_all, w_lo, cos_t, sin_t, tm):
    n = x2d.shape[0]
    t_tiles = cos_t.shape[0] // tm
    row = lambda i: (i, 0)
    fixed = lambda i: (0, 0)
    tab = lambda i: (i % t_tiles, 0)
    widths = (256, 512, 256, 128, SHIFT_PAD, 256, 1024)
    return pl.pallas_call(
        _proj_body,
        grid=(n // tm,),
        in_specs=[pl.BlockSpec((tm, D_MODEL), row), pl.BlockSpec((1, D_MODEL), fixed),
                  pl.BlockSpec((D_MODEL, 3328), fixed), pl.BlockSpec((D_MODEL, PRECISE_COLS), fixed),
                  pl.BlockSpec((tm, LANE), tab), pl.BlockSpec((tm, LANE), tab)],
        out_specs=[pl.BlockSpec((tm, w), row) for w in widths],
        out_shape=[jax.ShapeDtypeStruct((n, w), F32) for w in widths],
        compiler_params=_cp("parallel"),
        name="proj",
    )(x2d, norm_w.reshape(1, D_MODEL), w_all, w_lo, cos_t, sin_t)


def _cmp_mlp(xc, kind, w1_ref, b1_ref, w2_ref, w1k_ref, w2k_ref, nck):
    if kind == 0:
        hh = _dot3(xc, w1k_ref[...])
    else:
        hh = jnp.dot(xc.astype(BF16), w1_ref[kind], preferred_element_type=F32)
    hs = []
    for g in range(N_KV_A):
        hg = hh[g * nck:(g + 1) * nck]
        hs.append(jax.nn.gelu(b1_ref[kind] + hg[:, :CMP_HIDDEN] + pltpu.roll(hg[:, CMP_HIDDEN:], nck - 1, 0)))
    act = jnp.concatenate(hs, axis=1)
    if kind == 0:
        return _dot3(act, w2k_ref[...])
    return jnp.dot(act.astype(BF16), w2_ref[kind], preferred_element_type=F32)


def _cmp_body(xk_ref, xv_ref, w1_ref, b1_ref, w2_ref, w1k_ref, w2k_ref, kc_ref, vc_ref, vct_ref, xc_ref, *, n_chunks):
    lane = lax.broadcasted_iota(jnp.int32, (n_chunks, LANE), 1)
    lo = lane < HEAD_DIM
    for pair in range(CMP_STRIDE // 2):
        for kind, x_ref in enumerate((xk_ref, xv_ref)):
            ak = x_ref[0, pl.ds(2 * pair, n_chunks, stride=CMP_STRIDE), :]
            bk = x_ref[0, pl.ds(2 * pair + 1, n_chunks, stride=CMP_STRIDE), :]
            xc_ref[kind, 0:n_chunks, LANE * pair:LANE * (pair + 1)] = jnp.where(lo, ak, pltpu.roll(bk, HEAD_DIM, 1))
            xc_ref[kind, n_chunks:2 * n_chunks, LANE * pair:LANE * (pair + 1)] = jnp.where(lo, pltpu.roll(ak, HEAD_DIM, 1), bk)
    outs = [_cmp_mlp(xc_ref[kind], kind, w1_ref, b1_ref, w2_ref, w1k_ref, w2k_ref, n_chunks) for kind in range(2)]
    kc_ref[0] = outs[0]
    vc_ref[0] = outs[1]
    vct_ref[0] = outs[1].T


def _nsa_compress(rows, w1cat, b1, w2bd, w1k, w2k):
    b, tk = rows.shape[0], rows.shape[1]
    n_chunks = tk // CMP_STRIDE
    fixed3 = lambda i: (0, 0, 0)
    fixed2 = lambda i: (0, 0)
    return pl.pallas_call(
        functools.partial(_cmp_body, n_chunks=n_chunks),
        grid=(b,),
        in_specs=[pl.BlockSpec((1, tk, LANE), lambda i: (i, 0, 0)), pl.BlockSpec((1, tk, LANE), lambda i: (i, 0, 1)),
                  pl.BlockSpec((2, CMP_STRIDE * HEAD_DIM, 2 * CMP_HIDDEN), fixed3),
                  pl.BlockSpec((2, 1, CMP_HIDDEN), fixed3),
                  pl.BlockSpec((2, 2 * CMP_HIDDEN, LANE), fixed3),
                  pl.BlockSpec((CMP_STRIDE * HEAD_DIM, 2 * CMP_HIDDEN), fixed2), pl.BlockSpec((2 * CMP_HIDDEN, LANE), fixed2)],
        out_specs=[pl.BlockSpec((1, n_chunks, LANE), lambda i: (i, 0, 0)),
                   pl.BlockSpec((1, n_chunks, LANE), lambda i: (i, 0, 0)),
                   pl.BlockSpec((1, LANE, n_chunks), lambda i: (i, 0, 0))],
        out_shape=[jax.ShapeDtypeStruct((b, n_chunks, LANE), F32), jax.ShapeDtypeStruct((b, n_chunks, LANE), F32),
                   jax.ShapeDtypeStruct((b, LANE, n_chunks), F32)],
        scratch_shapes=[pltpu.VMEM((2, N_KV_A * n_chunks, CMP_STRIDE * HEAD_DIM), F32)],
        compiler_params=_cp("parallel"),
        name="nsa_compress",
    )(rows, rows, w1cat, b1, w2bd, w1k, w2k)


def _cmp_weights(cmp_w1, cmp_b1, cmp_w2):
    m = CMP_BLOCK // CMP_STRIDE
    w1r = cmp_w1.reshape(2, m, CMP_STRIDE * HEAD_DIM, CMP_HIDDEN)
    w1cat = jnp.concatenate([w1r[:, j] for j in range(m)], axis=-1)
    z = jnp.zeros_like(cmp_w2)
    w2bd = jnp.concatenate([jnp.concatenate([cmp_w2, z], -1), jnp.concatenate([z, cmp_w2], -1)], axis=1)
    return w1cat.astype(BF16), cmp_b1.reshape(2, 1, CMP_HIDDEN), w2bd.astype(BF16), w1cat[0], w2bd[0]


def _cmp_to_sel_t(n_chunks, n_cmp, n_sel):
    starts = np.arange(n_chunks) * CMP_STRIDE
    sel_s = np.arange(n_sel) * SEL_BLOCK
    ov = np.minimum(starts[:, None] + CMP_BLOCK, sel_s[None] + SEL_BLOCK) - np.maximum(starts[:, None], sel_s[None])
    ov = np.clip(ov, 0, None) / CMP_BLOCK
    ov[n_cmp:] = 0.0
    return jnp.asarray(ov.T, dtype=F32)


def _masked_softmax_cols(s, mask):
    m = jnp.max(jnp.where(mask, s, NEG_INF), axis=0, keepdims=True)
    e = jnp.where(mask, jnp.exp(s - m), 0.0)
    den = jnp.sum(e, axis=0, keepdims=True)
    return e * jnp.where(den > 0.0, 1.0 / den, 0.0)


def _nsa_prompt_body(qt_ref, gt_ref, kc_ref, vct_ref, ovt_ref, ks_ref, vst_ref, kw_ref, vwt_ref, o_ref, sel_ref,
                     *, n_cmp, n_sel, qb_size):
    qb = pl.program_id(1)
    tq = qb_size
    n_chunks = kc_ref.shape[1]
    qpos = qb * tq + lax.broadcasted_iota(jnp.int32, (1, tq), 1)
    qpos2 = jnp.concatenate([qpos, qpos], axis=1)
    zeros_q = jnp.zeros((HEAD_DIM, 2 * tq), F32)
    gates = jax.nn.sigmoid(gt_ref[0])
    kc = kc_ref[0]
    n_idx = lax.broadcasted_iota(jnp.int32, (n_chunks, 2 * tq), 0)
    cmp_mask = (n_idx * CMP_STRIDE + (CMP_BLOCK - 1) <= qpos2) & (n_idx < n_cmp)
    blk = lax.broadcasted_iota(jnp.int32, (n_sel, tq), 0)
    cur = qpos // SEL_BLOCK
    forced = (blk == 0) | (blk == cur) | (blk == cur - 1)
    causal_blk = blk * SEL_BLOCK <= qpos
    qpads, o_cmps = [], []

    for g in range(N_KV_A):
        q64 = jnp.concatenate([qt_ref[0, (2 * g) * HEAD_DIM:(2 * g + 1) * HEAD_DIM, :],
                               qt_ref[0, (2 * g + 1) * HEAD_DIM:(2 * g + 2) * HEAD_DIM, :]], axis=1) * (HEAD_DIM ** -0.5)
        qpad32 = jnp.concatenate([q64, zeros_q], axis=0) if g == 0 else jnp.concatenate([zeros_q, q64], axis=0)
        qpad = qpad32.astype(BF16)
        qpads.append(qpad)

        p = _masked_softmax_cols(_dot3(kc, qpad32), cmp_mask)
        o_cmps.append(jnp.dot(vct_ref[0, g * HEAD_DIM:(g + 1) * HEAD_DIM, :].astype(BF16), p.astype(BF16),
                              preferred_element_type=F32))
        psum = p[:, :tq] + p[:, tq:]
        imp = jnp.dot(ovt_ref[...], psum, precision=HI, preferred_element_type=F32)
        imp = jnp.where(forced, FORCED_SCORE, jnp.where(causal_blk, imp, BLOCKED_SCORE))
        rank = jnp.zeros((n_sel, tq), F32)
        for i in range(n_sel):
            row = imp[i:i + 1, :]
            rank = rank + jnp.where((row > imp) | ((row == imp) & (blk > i)), 1.0, 0.0)
        sel_ref[g] = jnp.where(rank < float(min(TOP_K, n_sel)), 1.0, 0.0)

    q_all = jnp.concatenate(qpads, axis=1)
    qpos4 = jnp.concatenate([qpos2, qpos2], axis=1)

    def attend(j, carry, k_ref, vt_ref, use_sel, tk, causal=True):
        m, l, acc = carry
        off = pl.multiple_of(j * tk, tk)
        kt = k_ref[0, pl.ds(off, tk), :].astype(BF16)
        s = jnp.dot(kt, q_all, preferred_element_type=F32)
        if causal:
            diff = qpos4 - (off + lax.broadcasted_iota(jnp.int32, (tk, 4 * tq), 0))
        if use_sel:
            per_tile = tk // SEL_BLOCK
            sels = []
            for g in range(N_KV_A):
                rows = [jnp.broadcast_to(sel_ref[g, pl.ds(j * per_tile + a, 1), :], (SEL_BLOCK, tq)) for a in range(per_tile)]
                selm = jnp.concatenate(rows, axis=0)
                sels += [selm, selm]
            mask = jnp.concatenate(sels, axis=1) > 0.0
            if causal:
                mask = mask & (diff >= 0)
        else:
            mask = (diff >= 0) & (diff < WINDOW)
        m_new = jnp.maximum(m, jnp.max(jnp.where(mask, s, NEG_INF), axis=0, keepdims=True))
        alpha = jnp.exp(m - m_new)
        e = jnp.where(mask, jnp.exp(s - m_new), 0.0)
        l_new = alpha * l + jnp.sum(e, axis=0, keepdims=True)
        vt = vt_ref[0, :, pl.ds(off, tk)].astype(BF16)
        return m_new, l_new, alpha * acc + jnp.dot(vt, e.astype(BF16), preferred_element_type=F32)

    init = (jnp.full((1, 4 * tq), NEG_INF, F32), jnp.zeros((1, 4 * tq), F32), jnp.zeros((2 * HEAD_DIM, 4 * tq), F32))
    tk_s, tk_w = 4 * tq, 2 * tq
    last_s = (qb * tq) // tk_s
    slc = functools.partial(attend, k_ref=ks_ref, vt_ref=vst_ref, use_sel=True, tk=tk_s)
    _, l_s, acc_s = slc(last_s, lax.fori_loop(0, last_s, functools.partial(slc, causal=False), init))
    _, l_w, acc_w = lax.fori_loop(jnp.maximum(qb * tq - WINDOW, 0) // tk_w, (qb * tq + tk_w) // tk_w,
                                  functools.partial(attend, k_ref=kw_ref, vt_ref=vwt_ref, use_sel=False, tk=tk_w), init)

    for g in range(N_KV_A):
        blk_g = (slice(g * HEAD_DIM, (g + 1) * HEAD_DIM), slice(g * 2 * tq, (g + 1) * 2 * tq))
        o_slc = acc_s[blk_g] / l_s[:, blk_g[1]]
        o_win = acc_w[blk_g] / l_w[:, blk_g[1]]
        for r in range(R_A):
            h = 2 * g + r
            gr = gates[3 * h:3 * h + 3, :]
            sl = slice(r * tq, (r + 1) * tq)
            o_ref[0, h * HEAD_DIM:(h + 1) * HEAD_DIM, :] = (gr[0:1] * o_cmps[g][:, sl] + gr[1:2] * o_slc[:, sl]
                                                          + gr[2:3] * o_win[:, sl])


def _nsa_prompt(qt, gt, kc, vct, ovt, rows, vst, wrows, vwt, n_cmp):
    b, _, t = qt.shape
    tq = 128
    n_sel = t // SEL_BLOCK
    n_chunks = kc.shape[1]
    per_b = lambda i, j: (i, 0, 0)
    return pl.pallas_call(
        functools.partial(_nsa_prompt_body, n_cmp=n_cmp, n_sel=n_sel, qb_size=tq),
        grid=(b, t // tq),
        in_specs=[pl.BlockSpec((1, 256, tq), lambda i, j: (i, 0, j)),
                  pl.BlockSpec((1, 16, tq), lambda i, j: (i, 0, j)),
                  pl.BlockSpec((1, n_chunks, LANE), per_b),
                  pl.BlockSpec((1, LANE, n_chunks), per_b),
                  pl.BlockSpec((n_sel, n_chunks), lambda i, j: (0, 0)),
                  pl.BlockSpec((1, t, LANE), lambda i, j: (i, 0, 2)),
                  pl.BlockSpec((1, LANE, t), per_b),
                  pl.BlockSpec((1, t, LANE), per_b),
                  pl.BlockSpec((1, LANE, t), per_b)],
        out_specs=pl.BlockSpec((1, 256, tq), lambda i, j: (i, 0, j)),
        out_shape=jax.ShapeDtypeStruct((b, 256, t), F32),
        scratch_shapes=[pltpu.VMEM((N_KV_A, n_sel, tq), F32)],
        compiler_params=_cp("parallel", "arbitrary"),
        name="nsa_prompt",
    )(qt, gt, kc, vct, ovt, rows, vst, wrows, vwt)


def _nsa_prompt_mixer(q, rows, wrows, gate, lw, b, t):
    rows3 = rows.reshape(b, t, 4 * LANE)
    wrows3 = wrows.reshape(b, t, 2 * LANE)
    n_chunks = t // CMP_STRIDE
    n_cmp = (t - CMP_BLOCK) // CMP_STRIDE + 1
    kc, _, vct = _nsa_compress(rows3, *lw['cmp'])
    ovt = _cmp_to_sel_t(n_chunks, n_cmp, t // SEL_BLOCK)
    qt = jnp.swapaxes(q.reshape(b, t, 256), 1, 2)
    gt = jnp.swapaxes(gate.reshape(b, t, LANE)[:, :, :16], 1, 2)
    vst = jnp.swapaxes(rows3[:, :, 3 * LANE:], 1, 2)
    vwt = jnp.swapaxes(wrows3[:, :, LANE:], 1, 2)
    ot = _nsa_prompt(qt, gt, kc, vct, ovt, rows3, vst, wrows3, vwt, n_cmp)
    return jnp.swapaxes(ot, 1, 2).reshape(b * t, 256)


def _softmax_rows_with_extra(s, mask, s_new):
    m = jnp.maximum(jnp.max(jnp.where(mask, s, NEG_INF), axis=-1, keepdims=True), s_new)
    e = jnp.where(mask, jnp.exp(s - m), 0.0)
    e_new = jnp.exp(s_new - m)
    return e, e_new, 1.0 / (jnp.sum(e, axis=-1, keepdims=True) + e_new)


def _nsa_sample_body(pt_ref, *refs, n_pages, page, pos, n_sel, n_cmp, wb):
    del pt_ref
    n_in = n_pages
    pages = refs[:n_in]
    (qbd_ref, new_ref, gate_ref, win_ref, ov_ref, w1_ref, b1_ref, w2_ref, w1k_ref, w2k_ref,
     o_ref, nw_ref, tok_ref, xc_ref) = refs[n_in:]
    pg = lambda p, kind: pages[p].at[0, kind:kind + 1]
    nck = n_pages * page // CMP_STRIDE
    lane8 = lax.broadcasted_iota(jnp.int32, (8, LANE), 1)
    lo8 = lane8 < HEAD_DIM
    nt = lambda a, b: lax.dot_general(a, b, (((1,), (1,)), ((), ())), preferred_element_type=F32)

    lo_c = lax.broadcasted_iota(jnp.int32, (nck, LANE), 1) < HEAD_DIM
    for kind in range(2):
        for p in range(n_pages):
            tok_ref[kind, p * page:(p + 1) * page, :] = pg(p, kind)[0].T
        for pair in range(CMP_STRIDE // 2):
            a = tok_ref[kind, pl.ds(2 * pair, nck, stride=CMP_STRIDE), :]
            b = tok_ref[kind, pl.ds(2 * pair + 1, nck, stride=CMP_STRIDE), :]
            cols = slice(LANE * pair, LANE * (pair + 1))
            xc_ref[kind, 0:nck, cols] = jnp.where(lo_c, a, pltpu.roll(b, HEAD_DIM, 1))
            xc_ref[kind, nck:2 * nck, cols] = jnp.where(lo_c, pltpu.roll(a, HEAD_DIM, 1), b)
    kc, vc = [_cmp_mlp(xc_ref[kind], kind, w1_ref, b1_ref, w2_ref, w1k_ref, w2k_ref, nck) for kind in range(2)]

    q = qbd_ref[0] * (HEAD_DIM ** -0.5)
    qb = q.astype(BF16)
    new = new_ref[0]

    n_idx = lax.broadcasted_iota(jnp.int32, (8, nck), 1)
    cmask = (n_idx * CMP_STRIDE + (CMP_BLOCK - 1) <= pos) & (n_idx < n_cmp)
    s = _dot3(q, kc, _NT)
    m = jnp.max(jnp.where(cmask, s, NEG_INF), axis=-1, keepdims=True)
    e = jnp.where(cmask, jnp.exp(s - m), 0.0)
    den = jnp.sum(e, axis=-1, keepdims=True)
    p_cmp = e * jnp.where(den > 0.0, 1.0 / den, 0.0)
    o_cmp = jnp.dot(p_cmp.astype(BF16), vc.astype(BF16), preferred_element_type=F32)

    row8 = lax.broadcasted_iota(jnp.int32, (8, nck), 0)
    psum = jnp.where(row8 == 0, p_cmp[0:1] + p_cmp[1:2], jnp.where(row8 == 1, p_cmp[2:3] + p_cmp[3:4], 0.0))
    imp = jnp.dot(psum, ov_ref[...], precision=HI, preferred_element_type=F32)
    cur = pos // SEL_BLOCK
    forced = (lane8 == 0) | (lane8 == cur) | (lane8 == cur - 1)
    imp = jnp.where(forced, FORCED_SCORE, jnp.where(lane8 * SEL_BLOCK <= pos, imp, BLOCKED_SCORE))
    imp = jnp.where(lane8 < n_sel, imp, -3e38)
    rank = jnp.zeros((8, LANE), F32)
    for i in range(n_sel):
        col = imp[:, i:i + 1]
        rank = rank + jnp.where((col > imp) | ((col == imp) & (lane8 > i)), 1.0, 0.0)
    sel = jnp.where((rank < float(min(TOP_K, n_sel))) & (lane8 < n_sel), 1.0, 0.0)
    rsel = lax.broadcasted_iota(jnp.int32, (8, LANE), 0)
    selh = jnp.where(rsel < R_A, sel[0:1], jnp.where(rsel < 2 * R_A, sel[1:2], 0.0))

    per_page = page // SEL_BLOCK
    s_t, m_t = [], []
    for p in range(n_pages):
        s_t.append(jnp.dot(qb, pg(p, 2)[0].astype(BF16), preferred_element_type=F32))
        blk_sel = selh[:, per_page * p:per_page * p + 1]
        for a in range(1, per_page):
            blk_sel = jnp.where(lane8 < a * SEL_BLOCK, blk_sel, selh[:, per_page * p + a:per_page * p + a + 1])
        kpos = p * page + lane8
        m_t.append((blk_sel > 0.0) & (kpos <= pos))
    s_all = jnp.concatenate(s_t, axis=1)
    mk_all = jnp.concatenate(m_t, axis=1)
    s_new = jnp.sum(q * new[2:3], axis=-1, keepdims=True)
    e, e_new, inv = _softmax_rows_with_extra(s_all, mk_all, s_new)
    acc = e_new * new[3:4]
    for p in range(n_pages):
        acc = acc + nt(e[:, p * page:(p + 1) * page].astype(BF16), pg(p, 3)[0].astype(BF16))
    o_slc = acc * inv

    kw, vw = win_ref[0], win_ref[1]
    widx = lax.broadcasted_iota(jnp.int32, (8, wb), 1)
    diff = wb - widx
    s_w = jnp.dot(qb, kw.astype(BF16), preferred_element_type=F32)
    s_wnew = jnp.sum(q * new[4:5], axis=-1, keepdims=True)
    e, e_new, inv = _softmax_rows_with_extra(s_w, (diff >= 0) & (diff < WINDOW), s_wnew)
    o_win = (nt(e.astype(BF16), vw.astype(BF16)) + e_new * new[5:6]) * inv

    gts = jax.nn.sigmoid(gate_ref[0])
    o = gts[:, 0:1] * o_cmp + gts[:, 1:2] * o_slc + gts[:, 2:3] * o_win
    lo1 = lo8[0:1]
    o_ref[0] = jnp.concatenate([jnp.where(lo1, o[0:1], pltpu.roll(o[1:2], HEAD_DIM, 1)),
                                jnp.where(lo1, pltpu.roll(o[2:3], HEAD_DIM, 1), o[3:4])], axis=1)
    last = lax.broadcasted_iota(jnp.int32, (LANE, wb), 1) == wb - 1
    new_t = new.T
    nw_ref[0] = jnp.where(last, new_t[:, 4:5], pltpu.roll(kw, wb - 1, 1))
    nw_ref[1] = jnp.where(last, new_t[:, 5:6], pltpu.roll(vw, wb - 1, 1))


def _nsa_sample(q, rows, wrows, gate, cache_kv, cache_win, layer, page_table, cmp_w, pos):
    b = q.shape[0]
    n_pool, page = cache_kv.shape[1:3]
    n_pages = page_table.shape[1]
    wb = cache_win.shape[2]
    assert wb == WINDOW and page % CMP_STRIDE == 0 and page % SEL_BLOCK == 0 and page == LANE
    tk = n_pages * page + 1
    n_cmp = (tk - CMP_BLOCK) // CMP_STRIDE + 1
    nck = n_pages * page // CMP_STRIDE
    assert n_cmp <= nck
    n_sel = -(-tk // SEL_BLOCK)
    assert n_sel <= LANE and pos // SEL_BLOCK == n_sel - 1
    ov = _pad_to(_cmp_to_sel_t(nck, n_cmp, n_sel).T, LANE, 1)
    w1cat, b1, w2bd, w1k, w2k = cmp_w
    q4 = q.reshape(b, H_A, HEAD_DIM)
    z = jnp.zeros_like(q4)
    first = (jnp.arange(H_A) // R_A == 0)[None, :, None]
    qbd = jnp.concatenate([jnp.where(first, q4, z), jnp.where(first, z, q4)], axis=-1)
    qbd = _pad_to(qbd, 8, 1)
    new = _pad_to(jnp.concatenate([rows, wrows], axis=1).reshape(b, 6, LANE), 8, 1)
    g8 = _pad_to(_pad_to(gate[:, :3 * H_A].reshape(b, H_A, 3), LANE, 2), 8, 1)
    cache3 = jnp.transpose(cache_kv, (0, 1, 3, 4, 5, 2)).reshape(-1, 4, LANE, page)
    win3 = jnp.transpose(cache_win, (0, 1, 3, 4, 5, 2)).reshape(-1, LANE, wb)
    page_specs = [pl.BlockSpec((1, 4, LANE, page), functools.partial(lambda i, pt, p: (layer * n_pool + pt[i, p], 0, 0, 0), p=p))
                  for p in range(n_pages)]
    per_b = lambda i, pt: (i, 0, 0)
    fixed2 = lambda i, pt: (0, 0)
    fixed3 = lambda i, pt: (0, 0, 0)
    grid_spec = pltpu.PrefetchScalarGridSpec(
        num_scalar_prefetch=1,
        grid=(b,),
        in_specs=page_specs + [pl.BlockSpec((1, 8, LANE), per_b), pl.BlockSpec((1, 8, LANE), per_b), pl.BlockSpec((1, 8, LANE), per_b),
                               pl.BlockSpec((2, LANE, wb), lambda i, pt: (layer * b + i, 0, 0)), pl.BlockSpec((nck, LANE), fixed2),
                               pl.BlockSpec((2, CMP_STRIDE * HEAD_DIM, 2 * CMP_HIDDEN), fixed3),
                               pl.BlockSpec((2, 1, CMP_HIDDEN), fixed3), pl.BlockSpec((2, 2 * CMP_HIDDEN, LANE), fixed3),
                               pl.BlockSpec((CMP_STRIDE * HEAD_DIM, 2 * CMP_HIDDEN), fixed2),
                               pl.BlockSpec((2 * CMP_HIDDEN, LANE), fixed2)],
        out_specs=[pl.BlockSpec((1, 1, 256), per_b), pl.BlockSpec((2, LANE, wb), per_b)],
        scratch_shapes=[pltpu.VMEM((2, n_pages * page, LANE), F32), pltpu.VMEM((2, 2 * nck, CMP_STRIDE * HEAD_DIM), F32)],
    )
    o, nw = pl.pallas_call(
        functools.partial(_nsa_sample_body, n_pages=n_pages, page=page, pos=pos, n_sel=n_sel, n_cmp=n_cmp, wb=wb),
        grid_spec=grid_spec,
        out_shape=[jax.ShapeDtypeStruct((b, 1, 256), F32), jax.ShapeDtypeStruct((2 * b, LANE, wb), F32)],
        compiler_params=_cp("parallel"),
        name="nsa_sample",
    )(page_table, *([cache3] * n_pages), qbd, new, g8, win3, ov, w1cat, b1, w2bd, w1k, w2k)
    return o.reshape(b, 256), jnp.transpose(nw.reshape(b, 2, N_KV_A, HEAD_DIM, wb), (0, 4, 1, 2, 3))


def _rwkv_prep_body(c_ref, s0_ref, mu_ref, vec_ref, wup_ref, aup_ref, gup_ref,
                    r_ref, lw_ref, k_ref, v_ref, kk_ref, ka_ref, g_ref, bonus_ref, carry_ref, *, tiles_per_seq):
    i = pl.program_id(0)
    cols = c_ref[...]
    tm = cols.shape[0]

    @pl.when(i % tiles_per_seq == 0)
    def _():
        carry_ref[...] = s0_ref[0]

    prev = pltpu.roll(cols, 1, 0)
    row0 = lax.broadcasted_iota(jnp.int32, cols.shape, 0) == 0
    prev = jnp.where(row0, carry_ref[...], prev)
    carry_ref[...] = cols[tm - 1:tm, :]
    _rwkv_features(cols, prev, mu_ref, vec_ref, wup_ref, aup_ref, gup_ref,
                   r_ref, lw_ref, k_ref, v_ref, kk_ref, ka_ref, g_ref, bonus_ref)


def _rwkv_step_prep_body(c_ref, s0_ref, mu_ref, vec_ref, wup_ref, aup_ref, gup_ref,
                         r_ref, lw_ref, k_ref, v_ref, kk_ref, ka_ref, g_ref, bonus_ref):
    _rwkv_features(c_ref[...], s0_ref[...], mu_ref, vec_ref, wup_ref, aup_ref, gup_ref,
                   r_ref, lw_ref, k_ref, v_ref, kk_ref, ka_ref, g_ref, bonus_ref)


def _rwkv_features(cols, prev, mu_ref, vec_ref, wup_ref, aup_ref, gup_ref,
                   r_ref, lw_ref, k_ref, v_ref, kk_ref, ka_ref, g_ref, bonus_ref):
    xs = cols + mu_ref[...] * (prev - cols)
    r, k, v, lo = xs[:, 0:256], xs[:, 256:512], xs[:, 512:768], xs[:, 768:896]
    vec = vec_ref[...]
    w0, a0, k_k, k_a, r_k = vec[0:1], vec[1:2], vec[2:3], vec[3:4], vec[4:5]
    dot_hi = lambda x, w: jnp.dot(x, w, precision=HI, preferred_element_type=F32)
    w_log = -jax.nn.softplus(-(w0 + dot_hi(jnp.tanh(lo), wup_ref[...]))) - 0.5
    a = jax.nn.sigmoid(a0 + dot_hi(lo, aup_ref[...]))
    g_ref[...] = dot_hi(jax.nn.sigmoid(lo), gup_ref[...])
    ones = _block_ones(256, HEAD_DIM, F32)
    kk = k * k_k
    kk = kk * lax.rsqrt(dot_hi(kk * kk, ones) + 1e-12)
    k2 = k * (1.0 + (a - 1.0) * k_a)
    r_ref[...] = r
    lw_ref[...] = -jnp.exp(w_log)
    k_ref[...] = k2
    v_ref[...] = v
    kk_ref[...] = kk
    ka_ref[...] = kk * a
    bonus_ref[...] = dot_hi(r * k2 * r_k, ones) * v


def _rwkv_prep(colsb, shift0, lw, t, tm):
    n = colsb.shape[0]
    tiles_per_seq = t // tm
    row = lambda i: (i, 0)
    fixed = lambda i: (0, 0)
    outs = [jax.ShapeDtypeStruct((n, 256), F32)] * 8
    return pl.pallas_call(
        functools.partial(_rwkv_prep_body, tiles_per_seq=tiles_per_seq),
        grid=(n // tm,),
        in_specs=[pl.BlockSpec((tm, SHIFT_PAD), row),
                  pl.BlockSpec((1, 1, SHIFT_PAD), lambda i: (i // tiles_per_seq, 0, 0)),
                  pl.BlockSpec((1, SHIFT_PAD), fixed), pl.BlockSpec((8, 256), fixed),
                  pl.BlockSpec((LANE, 256), fixed), pl.BlockSpec((LANE, 256), fixed), pl.BlockSpec((LANE, 256), fixed)],
        out_specs=[pl.BlockSpec((tm, 256), row)] * 8,
        out_shape=outs,
        scratch_shapes=[pltpu.VMEM((1, SHIFT_PAD), F32)],
        compiler_params=_cp("arbitrary"),
        name="rwkv_prep",
    )(colsb, shift0.reshape(-1, 1, SHIFT_PAD), lw['rwkv_mu'], lw['rwkv_vec'], lw['rwkv_wup'], lw['rwkv_aup'], lw['rwkv_gup'])


RWKV_CHUNK = 64


def _split_bf16(x):
    hi = x.astype(BF16)
    return hi, (x - hi.astype(F32)).astype(BF16)


def _dot3(a, b, dims=(((1,), (0,)), ((), ()))):
    ah, al = _split_bf16(a)
    bh, bl = _split_bf16(b)
    dg = lambda x, y: lax.dot_general(x, y, dims, preferred_element_type=F32)
    return dg(ah, bh) + dg(ah, bl) + dg(al, bh)


_NT = (((1,), (1,)), ((), ()))


def _rwkv_chunk_body(r_ref, lw_ref, k_ref, v_ref, kk_ref, ka_ref, g_ref, bonus_ref, s0_ref, vec_ref,
                     o_ref, st_ref, s_scr, y_scr, *, nb, tl):
    L = RWKV_CHUNK
    nc = tl // L

    @pl.when(pl.program_id(1) == 0)
    def _():
        s_scr[...] = s0_ref[...]

    ri = lax.broadcasted_iota(jnp.int32, (L, L), 0)
    ci = lax.broadcasted_iota(jnp.int32, (L, L), 1)
    strict, incl = ri > ci, ri >= ci
    ltri = incl.astype(F32)
    eye = (ri == ci).astype(F32)

    bnn = (((2,), (1,)), ((0,), (0,)))
    bnt = (((2,), (2,)), ((0,), (0,)))

    def chunk(c, carry):
        rows = pl.ds(pl.multiple_of(c * L, L), L)
        lhs_l, rhs_l, v_l, kw_l, wl_l = [], [], [], [], []
        for b in range(nb):
            r, lw, k, v, kk, ka = [ref[b, rows, :] for ref in (r_ref, lw_ref, k_ref, v_ref, kk_ref, ka_ref)]
            cl = jnp.dot(ltri, lw, precision=HI, preferred_element_type=F32)
            e_neg = jnp.exp(-cl)
            e_rem = jnp.exp(cl[L - 1:L, :] - cl)
            kkd, rd = kk * jnp.exp(cl - lw), r * jnp.exp(cl)
            kinv, kainv, kw, kaw = k * e_neg, ka * e_neg, k * e_rem, ka * e_rem
            w_last = jnp.exp(cl[L - 1:L, :])
            for h in range(H_B):
                sl = slice(h * HEAD_DIM, (h + 1) * HEAD_DIM)
                lhs_l.append(jnp.concatenate([kkd[:, sl], rd[:, sl]], axis=0))
                rhs_l.append(jnp.concatenate([kinv[:, sl], kainv[:, sl]], axis=0))
                v_l.append(v[:, sl])
                kw_l.append(jnp.concatenate([kw[:, sl], kaw[:, sl]], axis=0))
                wl_l.append(w_last[:, sl])
        lhs, rhs, vs, kws, wl = [jnp.stack(x) for x in (lhs_l, rhs_l, v_l, kw_l, wl_l)]
        gm = _dot3(lhs, rhs, bnt)
        a_vk = jnp.where(strict, gm[:, :L, :L], 0.0)
        n1 = jnp.where(strict, -gm[:, :L, L:], 0.0)
        t_inv, pw = eye + n1, n1
        for _ in range(5):
            pw = _dot3(pw, pw, bnn)
            t_inv = _dot3(t_inv, eye + pw, bnn)
        s = s_scr[...].reshape(nb * H_B, HEAD_DIM, HEAD_DIM)
        xs = _dot3(lhs, s, bnt)
        u = _dot3(t_inv, xs[:, :L] + _dot3(a_vk, vs, bnn), bnn)
        b_vk = jnp.where(incl, gm[:, L:, :L], 0.0).astype(BF16)
        b_uk = jnp.where(incl, gm[:, L:, L:], 0.0).astype(BF16)
        y = (xs[:, L:] + lax.dot_general(b_vk, vs.astype(BF16), bnn, preferred_element_type=F32)
             - lax.dot_general(b_uk, u.astype(BF16), bnn, preferred_element_type=F32))
        vu_t = jnp.stack([jnp.concatenate([vs[n], -u[n]], axis=0).T for n in range(nb * H_B)])
        s_new = s * wl + _dot3(vu_t, kws, bnn)
        s_scr[...] = s_new.reshape(nb, H_B, HEAD_DIM, HEAD_DIM)
        for b in range(nb):
            for h in range(H_B):
                y_scr[b, rows, h * HEAD_DIM:(h + 1) * HEAD_DIM] = y[b * H_B + h]
        return carry

    lax.fori_loop(0, nc, chunk, 0)
    st_ref[...] = s_scr[...]
    vec = vec_ref[...]
    for b in range(nb):
        o_ref[b] = (_segment_norm(y_scr[b], RWKV_GN_EPS) * vec[5:6] + vec[6:7] + bonus_ref[b]) * g_ref[b]


def _rwkv_chunked(prep, s0, vec, b, t, nb, tl):
    arrs = [a.reshape(b, t, 256) for a in prep]
    seq = pl.BlockSpec((nb, tl, 256), lambda i, j: (i, j, 0))
    st = pl.BlockSpec((nb, H_B, HEAD_DIM, HEAD_DIM), lambda i, j: (i, 0, 0, 0))
    return pl.pallas_call(
        functools.partial(_rwkv_chunk_body, nb=nb, tl=tl),
        grid=(b // nb, t // tl),
        in_specs=[seq] * 8 + [st, pl.BlockSpec((8, 256), lambda i, j: (0, 0))],
        out_specs=[seq, st],
        out_shape=[jax.ShapeDtypeStruct((b, t, 256), F32), jax.ShapeDtypeStruct((b, H_B, HEAD_DIM, HEAD_DIM), F32)],
        scratch_shapes=[pltpu.VMEM((nb, H_B, HEAD_DIM, HEAD_DIM), F32), pltpu.VMEM((nb, tl, 256), F32)],
        compiler_params=_cp("parallel", "arbitrary"),
        name="rwkv_chunked",
    )(*arrs, s0, vec)


def _rwkv_mixer(colsb, shift0, s0, lw, b, t, tm, nb, tl):
    prep = _rwkv_prep(colsb, shift0, lw, t, tm)
    o, st = _rwkv_chunked(prep, s0, lw['rwkv_vec'], b, t, nb, tl)
    shift = colsb.reshape(b, t, SHIFT_PAD)[:, -1, :SHIFT_B]
    return o.reshape(b * t, 256), st, shift


def _segment_norm(y, eps):
    avg = _block_ones(256, HEAD_DIM, F32) * (1.0 / HEAD_DIM)
    yc = y - jnp.dot(y, avg, precision=HI, preferred_element_type=F32)
    return yc * lax.rsqrt(jnp.dot(yc * yc, avg, precision=HI, preferred_element_type=F32) + eps)


def _rwkv_step_body(r_ref, lw_ref, k_ref, v_ref, kk_ref, ka_ref, g_ref, bonus_ref, vec_ref, s_ref, o_ref, st_ref, ft_scr, y_scr):
    h = pl.program_id(0)

    @pl.when(h == 0)
    def _():
        for n, ref in enumerate((r_ref, lw_ref, k_ref, v_ref, kk_ref, ka_ref)):
            ft_scr[n] = ref[...].T

    base = pl.multiple_of(h * HEAD_DIM, HEAD_DIM)
    head = lambda n: ft_scr[n, pl.ds(base, HEAD_DIM), :]
    r_t, w_t, k_t, kk_t, ka_t = head(0), jnp.exp(head(1)), head(2), head(4), head(5)

    def body(i, carry):
        rows = pl.ds(pl.multiple_of(i * HEAD_DIM, HEAD_DIM), HEAD_DIM)
        s = s_ref[rows, :]
        sk = jnp.sum(s * kk_t, axis=0, keepdims=True)
        s = s * w_t - sk * ka_t + ft_scr[3, pl.ds(base + i, 1), :] * k_t
        st_ref[rows, :] = s
        y_scr[pl.ds(base + i, 1), :] = jnp.sum(s * r_t, axis=0, keepdims=True)
        return carry

    lax.fori_loop(0, HEAD_DIM, body, 0)

    @pl.when(h == H_B - 1)
    def _():
        vec = vec_ref[...]
        o_ref[...] = (_segment_norm(y_scr[...].T, RWKV_GN_EPS) * vec[5:6] + vec[6:7] + bonus_ref[...]) * g_ref[...]


def _rwkv_step(colsb, shift0, s_all, layer, lw):
    b = colsb.shape[0]
    hd2 = HEAD_DIM * HEAD_DIM
    full = lambda *_: (0, 0)
    feat = pl.BlockSpec((b, 256), full)
    prep = pl.pallas_call(
        _rwkv_step_prep_body,
        grid=(1,),
        in_specs=[pl.BlockSpec((b, SHIFT_PAD), full), pl.BlockSpec((b, SHIFT_PAD), full), pl.BlockSpec((1, SHIFT_PAD), full),
                  pl.BlockSpec((8, 256), full)] + [pl.BlockSpec((LANE, 256), full)] * 3,
        out_specs=[feat] * 8,
        out_shape=[jax.ShapeDtypeStruct((b, 256), F32)] * 8,
        compiler_params=_cp("arbitrary"),
        name="rwkv_step_prep",
    )(colsb, shift0, lw['rwkv_mu'], lw['rwkv_vec'], lw['rwkv_wup'], lw['rwkv_aup'], lw['rwkv_gup'])
    s_rows = jnp.transpose(s_all, (0, 2, 3, 4, 1)).reshape(-1, b)
    o, st = pl.pallas_call(
        _rwkv_step_body,
        grid=(H_B,),
        in_specs=[feat] * 8 + [pl.BlockSpec((8, 256), full), pl.BlockSpec((hd2, b), lambda h: (layer * H_B + h, 0))],
        out_specs=[feat, pl.BlockSpec((hd2, b), lambda h: (h, 0))],
        out_shape=[jax.ShapeDtypeStruct((b, 256), F32), jax.ShapeDtypeStruct((H_B * hd2, b), F32)],
        scratch_shapes=[pltpu.VMEM((6, 256, b), F32), pltpu.VMEM((256, b), F32)],
        compiler_params=_cp("arbitrary"),
        name="rwkv_step",
    )(*prep, lw['rwkv_vec'], s_rows)
    return o, jnp.transpose(st.reshape(H_B, HEAD_DIM, HEAD_DIM, b), (3, 0, 1, 2))


def _ret_step_body(c_ref, cos_ref, sin_ref, gn_ref, r0_ref, o_ref, rt_ref, ft_scr, acc_scr):
    h = pl.program_id(0)

    @pl.when(h == 0)
    def _():
        x = c_ref[...]
        cs, sn = cos_ref[...], sin_ref[...]
        first = (lax.broadcasted_iota(jnp.int32, cs.shape, 1) % HEAD_DIM) < (HEAD_DIM // 2)

        def rope(z):
            sw = jnp.where(first, pltpu.roll(z, 256 - HEAD_DIM // 2, 1), pltpu.roll(z, HEAD_DIM // 2, 1))
            return z * cs + sw * sn

        ft_scr[0] = rope(x[:, 0:256]).T
        ft_scr[1] = (rope(x[:, 256:512]) * (HEAD_DIM ** -0.5)).T
        ft_scr[2] = x[:, 512:768].T

    base = pl.multiple_of(h * HEAD_DIM, HEAD_DIM)
    nseq = rt_ref.shape[1]
    gamma = jnp.exp(jnp.log1p(-jnp.exp2(-5.0 - jnp.full((1, nseq), h, jnp.int32).astype(F32))))
    q_t, k_t, v_t = [ft_scr[n, pl.ds(base, HEAD_DIM), :] for n in range(3)]
    qk = jnp.sum(q_t * k_t, axis=0, keepdims=True)

    def body(d, cross):
        rows = pl.ds(pl.multiple_of(d * HEAD_DIM, HEAD_DIM), HEAD_DIM)
        r_old = r0_ref[rows, :]
        rt_ref[rows, :] = gamma * r_old + ft_scr[1, pl.ds(base + d, 1), :] * v_t
        return cross + ft_scr[0, pl.ds(base + d, 1), :] * r_old

    cross = lax.fori_loop(0, HEAD_DIM, body, jnp.zeros((HEAD_DIM, nseq), F32))
    acc_scr[pl.ds(base, HEAD_DIM), :] = qk * v_t + gamma * cross

    @pl.when(h == H_D - 1)
    def _():
        o_ref[...] = jax.nn.silu(c_ref[:, 768:1024]) * (_segment_norm(acc_scr[...].T, RET_GN_EPS) * gn_ref[...])


def _ret_step(colsd, cos, sin, r_all, layer, gn):
    b = colsd.shape[0]
    hd2 = HEAD_DIM * HEAD_DIM
    full = lambda *_: (0, 0)
    r_rows = jnp.transpose(r_all, (0, 2, 3, 4, 1)).reshape(-1, b)
    o, rt = pl.pallas_call(
        _ret_step_body,
        grid=(H_D,),
        in_specs=[pl.BlockSpec((b, 1024), full), pl.BlockSpec((b, 256), full), pl.BlockSpec((b, 256), full),
                  pl.BlockSpec((1, 256), full), pl.BlockSpec((hd2, b), lambda h: (layer * H_D + h, 0))],
        out_specs=[pl.BlockSpec((b, 256), full), pl.BlockSpec((hd2, b), lambda h: (h, 0))],
        out_shape=[jax.ShapeDtypeStruct((b, 256), F32), jax.ShapeDtypeStruct((H_D * hd2, b), F32)],
        scratch_shapes=[pltpu.VMEM((3, 256, b), F32), pltpu.VMEM((256, b), F32)],
        compiler_params=_cp("arbitrary"),
        name="ret_step",
    )(colsd, cos, sin, gn.reshape(1, 256), r_rows)
    return o, jnp.transpose(rt.reshape(H_D, HEAD_DIM, HEAD_DIM, b), (3, 0, 1, 2))


def _s5_step_body(u_ref, x0_ref, a1_ref, a2_ref, bt_ref, ct_ref, d_ref, wg_ref, o_ref, xt_ref):
    u = u_ref[...]
    dot_hi = lambda a, b: jnp.dot(a, b, precision=HI, preferred_element_type=F32)
    x0 = x0_ref[...]
    n = x0.shape[0]
    even = (lax.broadcasted_iota(jnp.int32, x0.shape, 0) % 2) == 0
    partner = jnp.where(even, pltpu.roll(x0, n - 1, 0), pltpu.roll(x0, 1, 0))
    x = a1_ref[...] * x0 + a2_ref[...] * partner + dot_hi(bt_ref[...], u.T)
    xt_ref[...] = x
    y = dot_hi(ct_ref[...], x).T + d_ref[...] * u
    z = jax.nn.gelu(y)
    o_ref[...] = z * jax.nn.sigmoid(dot_hi(z, wg_ref[...]))


def _s5_step(u, x_all, layer, lw):
    b = u.shape[0]
    a1, a2, bt, ct, d_row = lw['s5_step']
    n = 2 * G_C * S5_P
    full = lambda *_: (0, 0)
    x_rows = jnp.transpose(x_all, (0, 2, 3, 4, 1)).reshape(-1, b)
    o, xt = pl.pallas_call(
        _s5_step_body,
        grid=(1,),
        in_specs=[pl.BlockSpec((b, C_C), full), pl.BlockSpec((n, b), lambda i: (layer, 0)), pl.BlockSpec((n, 1), full),
                  pl.BlockSpec((n, 1), full), pl.BlockSpec((n, C_C), full), pl.BlockSpec((C_C, n), full),
                  pl.BlockSpec((1, C_C), full), pl.BlockSpec((C_C, C_C), full)],
        out_specs=[pl.BlockSpec((b, C_C), full), pl.BlockSpec((n, b), full)],
        out_shape=[jax.ShapeDtypeStruct((b, C_C), F32), jax.ShapeDtypeStruct((n, b), F32)],
        compiler_params=_cp("arbitrary"),
        name="s5_step",
    )(u, x_rows, a1, a2, bt, ct, d_row, lw['s5_w_glu'])
    return o, jnp.transpose(xt.reshape(G_C, S5_P, 2, b), (3, 0, 1, 2))


def _s5_params(lw):
    lr, li = lw['s5_lambda_re'], lw['s5_lambda_im']
    dt = jnp.exp(lw['s5_log_step'])[:, None]
    mag = jnp.exp(lr * dt)
    ar, ai = mag * jnp.cos(li * dt), mag * jnp.sin(li * dt)
    nr, ni = ar - 1.0, ai
    den = lr * lr + li * li
    fr, fi = (nr * lr + ni * li) / den, (ni * lr - nr * li) / den
    b_re, b_im = lw['s5_b'][0], lw['s5_b'][1]
    bbr = fr[..., None] * b_re - fi[..., None] * b_im
    bbi = fr[..., None] * b_im + fi[..., None] * b_re
    eye = jnp.eye(G_C, dtype=F32)
    bd_in = lambda m: jnp.einsum('gpc,gh->gchp', m, eye).reshape(G_C * S5_CH, G_C * S5_P)
    bd_out = lambda m: jnp.einsum('gcp,gh->gphc', m, eye).reshape(G_C * S5_P, G_C * S5_CH)
    b_big = jnp.concatenate([bd_in(bbr), bd_in(bbi)], axis=1)
    c_big = jnp.concatenate([bd_out(lw['s5_c'][0]), -bd_out(lw['s5_c'][1])], axis=0)
    a_row = jnp.concatenate([ar.reshape(1, -1), ai.reshape(1, -1)], axis=1)
    d_row = lw['s5_d'].reshape(1, C_C)
    n = 2 * G_C * S5_P
    a1 = jnp.stack([ar, ar], axis=-1).reshape(n, 1)
    a2 = jnp.stack([-ai, ai], axis=-1).reshape(n, 1)
    bt = jnp.stack([jnp.einsum('gpc,gh->gphc', bbr, eye), jnp.einsum('gpc,gh->gphc', bbi, eye)], axis=2).reshape(n, G_C * S5_CH)
    ct = jnp.stack([jnp.einsum('gcp,gh->gchp', lw['s5_c'][0], eye), -jnp.einsum('gcp,gh->gchp', lw['s5_c'][1], eye)],
                   axis=-1).reshape(G_C * S5_CH, n)
    return (a_row, b_big, c_big, d_row), (a1, a2, bt, ct, d_row)


def _s5_body(u_ref, x0_ref, a_ref, b_ref, c_ref, d_ref, wg_ref, o_ref, xt_ref, x_scr, bu_scr, xs_scr, *, nb, tt, mm_dtype, prec):
    @pl.when(pl.program_id(0) == 0)
    def _():
        x_scr[...] = x0_ref[...]

    np_ = G_C * S5_P
    ncb = np_ // LANE
    for b in range(nb):
        bu = jnp.dot(u_ref[b].astype(mm_dtype), b_ref[...], precision=prec, preferred_element_type=F32)
        for cb in range(2 * ncb):
            bu_scr[cb, b * tt:(b + 1) * tt, :] = bu[:, cb * LANE:(cb + 1) * LANE]
    a = a_ref[...]

    def step(t, x):
        rows = pl.ds(t, nb, stride=tt)
        new = [None] * (2 * ncb)
        for cb in range(ncb):
            re, im = slice(cb * LANE, (cb + 1) * LANE), slice(np_ + cb * LANE, np_ + (cb + 1) * LANE)
            ar, ai, xr, xi = a[:, re], a[:, im], x[:, re], x[:, im]
            new[cb] = ar * xr - ai * xi + bu_scr[cb, rows, :]
            new[ncb + cb] = ar * xi + ai * xr + bu_scr[ncb + cb, rows, :]
            xs_scr[cb, rows, :] = new[cb]
            xs_scr[ncb + cb, rows, :] = new[ncb + cb]
        return jnp.concatenate(new, axis=1)

    x_last = lax.fori_loop(0, tt, step, x_scr[...], unroll=8)
    x_scr[...] = x_last
    xt_ref[...] = x_last
    for b in range(nb):
        u = u_ref[b]
        xs = jnp.concatenate([xs_scr[cb, b * tt:(b + 1) * tt, :] for cb in range(2 * ncb)], axis=1)
        y = jnp.dot(xs.astype(mm_dtype), c_ref[...], precision=prec, preferred_element_type=F32) + d_ref[...] * u
        z = jax.nn.gelu(y)
        o_ref[b] = z * jax.nn.sigmoid(jnp.dot(z.astype(mm_dtype), wg_ref[...], precision=prec, preferred_element_type=F32))


def _s5_mixer(u, x0, lw, b, t, tt, exact):
    a_row, b_big, c_big, d_row = lw['s5']
    mm_dtype = F32 if exact else BF16
    prec = HI if exact else None
    x0l = jnp.concatenate([x0[..., 0].reshape(b, -1), x0[..., 1].reshape(b, -1)], axis=1)
    np2 = 2 * G_C * S5_P
    fixed = lambda i: (0, 0)
    o, xt = pl.pallas_call(
        functools.partial(_s5_body, nb=b, tt=tt, mm_dtype=mm_dtype, prec=prec),
        grid=(t // tt,),
        in_specs=[pl.BlockSpec((b, tt, C_C), lambda i: (0, i, 0)), pl.BlockSpec((b, np2), fixed),
                  pl.BlockSpec((1, np2), fixed), pl.BlockSpec((C_C, np2), fixed), pl.BlockSpec((np2, C_C), fixed),
                  pl.BlockSpec((1, C_C), fixed), pl.BlockSpec((C_C, C_C), fixed)],
        out_specs=[pl.BlockSpec((b, tt, C_C), lambda i: (0, i, 0)), pl.BlockSpec((b, np2), fixed)],
        out_shape=[jax.ShapeDtypeStruct((b, t, C_C), F32), jax.ShapeDtypeStruct((b, np2), F32)],
        scratch_shapes=[pltpu.VMEM((b, np2), F32), pltpu.VMEM((np2 // LANE, b * tt, LANE), F32),
                        pltpu.VMEM((np2 // LANE, b * tt, LANE), F32)],
        compiler_params=_cp("arbitrary"),
        name="s5",
    )(u.reshape(b, t, C_C), x0l, a_row, b_big.astype(mm_dtype), c_big.astype(mm_dtype), d_row, lw['s5_w_glu'].astype(mm_dtype))
    xt = xt.reshape(b, 2, G_C, S5_P)
    return o.reshape(b * t, C_C), jnp.stack([xt[:, 0], xt[:, 1]], axis=-1)


def _ret_tables(pos, c):
    cos, sin = _rope_tables(pos, HEAD_DIM, RET_THETA, HEAD_DIM, H_D)
    log_g = jnp.log1p(-jnp.exp2(-5.0 - jnp.arange(H_D, dtype=F32)))
    i = jnp.arange(c, dtype=F32)
    diff = i[:, None] - i[None, :]
    dmat = jnp.where(diff >= 0, jnp.exp(jnp.maximum(diff, 0.0)[None] * log_g[:, None, None]), 0.0).reshape(H_D * c, c)
    q_dec = jnp.repeat(jnp.exp((i + 1.0)[None] * log_g[:, None]).T, HEAD_DIM, axis=1)
    k_dec = jnp.repeat(jnp.exp((c - 1.0 - i)[None] * log_g[:, None]).T, HEAD_DIM, axis=1)
    chunk_dec = jnp.repeat(jnp.exp(c * log_g), HEAD_DIM).reshape(256, 1)
    return cos, sin, dmat, q_dec, k_dec, chunk_dec


def _ret_body(c_ref, cos_ref, sin_ref, dmat_ref, qdec_ref, kdec_ref, cdec_ref, r0_ref, gn_ref, o_ref, rt_ref, r_scr, *, c):
    @pl.when(pl.program_id(1) == 0)
    def _():
        r_scr[...] = r0_ref[0]

    x = c_ref[0]
    q, k, v, g = x[:, 0:256], x[:, 256:512], x[:, 512:768], x[:, 768:1024]
    cs, sn = cos_ref[...], sin_ref[...]
    lane = lax.broadcasted_iota(jnp.int32, (c, 256), 1)
    first = (lane % HEAD_DIM) < (HEAD_DIM // 2)

    def rope(z):
        sw = jnp.where(first, pltpu.roll(z, 256 - HEAD_DIM // 2, 1), pltpu.roll(z, HEAD_DIM // 2, 1))
        return z * cs + sw * sn

    q = rope(q)
    k = rope(k) * (HEAD_DIM ** -0.5)
    head = lane // HEAD_DIM
    kb, vb = k.astype(BF16), v.astype(BF16)
    qstack = jnp.concatenate([jnp.where(head == h, q, 0.0) for h in range(H_D)], axis=0).astype(BF16)
    s = lax.dot_general(qstack, kb, (((1,), (1,)), ((), ())), preferred_element_type=F32) * dmat_ref[...]
    pv = jnp.dot(s.astype(BF16), vb, preferred_element_type=F32)
    inner = jnp.zeros((c, 256), F32)
    for h in range(H_D):
        inner = inner + jnp.where(head == h, pv[h * c:(h + 1) * c], 0.0)
    r_old = r_scr[...]
    cross = jnp.dot((q * qdec_ref[...]).astype(BF16), r_old.astype(BF16), preferred_element_type=F32)
    kv = lax.dot_general((k * kdec_ref[...]).astype(BF16), vb, (((0,), (0,)), ((), ())), preferred_element_type=F32)
    bd = _block_ones(256, HEAD_DIM, F32)
    r_new = cdec_ref[...] * r_old + kv * bd
    r_scr[...] = r_new
    rt_ref[0] = r_new
    o = inner + cross
    avg = bd * (1.0 / HEAD_DIM)
    oc = o - jnp.dot(o, avg, precision=HI, preferred_element_type=F32)
    on = oc * lax.rsqrt(jnp.dot(oc * oc, avg, precision=HI, preferred_element_type=F32) + RET_GN_EPS)
    o_ref[0] = jax.nn.silu(g) * (on * gn_ref[...])


def _ret_mixer(colsd, r0, lw, tabs, b, t):
    c = RET_CHUNK if t % RET_CHUNK == 0 else t
    cos, sin, dmat, q_dec, k_dec, chunk_dec = tabs
    eye = jnp.eye(H_D, dtype=F32)
    r0l = jnp.einsum('bhde,hg->bhdge', r0, eye).reshape(b, 256, 256)
    n_t = t // c
    fixed = lambda i, j: (0, 0)
    o, rt = pl.pallas_call(
        functools.partial(_ret_body, c=c),
        grid=(b, n_t),
        in_specs=[pl.BlockSpec((1, c, 1024), lambda i, j: (i, j, 0)),
                  pl.BlockSpec((c, 256), lambda i, j: (j, 0)), pl.BlockSpec((c, 256), lambda i, j: (j, 0)),
                  pl.BlockSpec((H_D * c, c), fixed), pl.BlockSpec((c, 256), fixed), pl.BlockSpec((c, 256), fixed),
                  pl.BlockSpec((256, 1), fixed), pl.BlockSpec((1, 256, 256), lambda i, j: (i, 0, 0)),
                  pl.BlockSpec((1, 256), fixed)],
        out_specs=[pl.BlockSpec((1, c, 256), lambda i, j: (i, j, 0)), pl.BlockSpec((1, 256, 256), lambda i, j: (i, 0, 0))],
        out_shape=[jax.ShapeDtypeStruct((b, t, 256), F32), jax.ShapeDtypeStruct((b, 256, 256), F32)],
        scratch_shapes=[pltpu.VMEM((256, 256), F32)],
        compiler_params=_cp("parallel", "arbitrary"),
        name="retention",
    )(colsd.reshape(b, t, 1024), cos, sin, dmat, q_dec, k_dec, chunk_dec, r0l, lw['ret_gn'].reshape(1, 256))
    rt = jnp.einsum('bhdge,hg->bhde', rt.reshape(b, H_D, HEAD_DIM, H_D, HEAD_DIM), eye)
    return o.reshape(b * t, 256), rt


def _partner(x, d, period):
    pos = lax.broadcasted_iota(jnp.int32, x.shape, 1) % period
    return jnp.where(pos + d < period, pltpu.roll(x, LANE - d, 1), pltpu.roll(x, period - d, 1))


def _out_body(x_ref, oa_ref, ob_ref, oc_ref, od_ref, w_ref, nw_ref, wr_ref, br_ref, x1_ref, h_ref, comb_ref):
    acc = x_ref[...]
    for i, ref in enumerate((oa_ref, ob_ref, oc_ref, od_ref)):
        acc = acc + jnp.dot(ref[...].astype(BF16), w_ref[256 * i:256 * (i + 1), :], preferred_element_type=F32)
    x1_ref[...] = acc
    h = acc * lax.rsqrt(jnp.mean(acc * acc, axis=-1, keepdims=True) + RMS_EPS) * nw_ref[...]
    hb = h.astype(BF16)
    h_ref[...] = hb
    h_lo = (h - hb.astype(F32)).astype(BF16)
    logits = (jnp.dot(hb, wr_ref[0], preferred_element_type=F32) + jnp.dot(hb, wr_ref[1], preferred_element_type=F32)
              + jnp.dot(h_lo, wr_ref[0], preferred_element_type=F32) + br_ref[...])
    le, lg = logits[:, :LANE], logits[:, LANE:]
    lane = lax.broadcasted_iota(jnp.int32, le.shape, 1)
    mg = jnp.max(lg, axis=-1, keepdims=True)
    eg = jnp.exp(lg - mg)
    pg = eg / (jnp.sum(eg, axis=-1, keepdims=True) * (1.0 / 32.0))
    gidx = (lane % N_EXPERTS) // EXP_PER_GROUP
    g_rank = jnp.zeros_like(pg)
    for d in range(1, N_GROUPS):
        other = pltpu.roll(pg, LANE - EXP_PER_GROUP * d, 1)
        wrapped = gidx + d >= N_GROUPS
        g_rank = g_rank + jnp.where((other > pg) | ((other == pg) & wrapped), 1.0, 0.0)
    kidx = lane % EXP_PER_GROUP
    others = [_partner(le, d, EXP_PER_GROUP) for d in range(1, EXP_PER_GROUP)]
    me = functools.reduce(jnp.maximum, others, le)
    ee = jnp.exp(le - me)
    se = ee
    for d in range(1, EXP_PER_GROUP):
        se = se + _partner(ee, d, EXP_PER_GROUP)
    pe = ee / se
    e_rank = jnp.zeros_like(pe)
    for d in range(1, EXP_PER_GROUP):
        other = _partner(pe, d, EXP_PER_GROUP)
        wrapped = kidx + d >= EXP_PER_GROUP
        e_rank = e_rank + jnp.where((other > pe) | ((other == pe) & wrapped), 1.0, 0.0)
    top = jnp.where(e_rank < 2.0, pe, 0.0)
    den = top
    for d in range(1, EXP_PER_GROUP):
        den = den + _partner(top, d, EXP_PER_GROUP)
    comb = jnp.where((g_rank < 1.0) & (lane < N_EXPERTS), pg * (top / den), 0.0)
    comb_ref[...] = comb


def _out_router(x, oa, ob, oc, od, lw, tm):
    n = x.shape[0]
    row = lambda i: (i, 0)
    fixed = lambda i: (0, 0)
    mix = pl.BlockSpec((tm, 256), row)
    return pl.pallas_call(
        _out_body,
        grid=(n // tm,),
        in_specs=[pl.BlockSpec((tm, D_MODEL), row), mix, mix, mix, mix,
                  pl.BlockSpec((D_MODEL, D_MODEL), fixed), pl.BlockSpec((1, D_MODEL), fixed),
                  pl.BlockSpec((2, D_MODEL, 2 * LANE), lambda i: (0, 0, 0)), pl.BlockSpec((1, 2 * LANE), fixed)],
        out_specs=[pl.BlockSpec((tm, D_MODEL), row), pl.BlockSpec((tm, D_MODEL), row), pl.BlockSpec((tm, LANE), row)],
        out_shape=[jax.ShapeDtypeStruct((n, D_MODEL), F32), jax.ShapeDtypeStruct((n, D_MODEL), BF16),
                   jax.ShapeDtypeStruct((n, LANE), F32)],
        compiler_params=_cp("parallel"),
        name="out_router",
    )(x, oa, ob, oc, od, lw['w_out'], lw['norm_ffn'], lw['w_router'], lw['b_router'])


def _router_weights(w_grp, b_grp, w_exp, b_exp):
    we = jnp.transpose(w_exp, (1, 0, 2)).reshape(D_MODEL, N_EXPERTS)
    wg = jnp.repeat(w_grp, EXP_PER_GROUP, axis=1)
    reps = LANE // N_EXPERTS
    w = jnp.concatenate([jnp.tile(we, (1, reps)), jnp.tile(wg, (1, reps))], axis=1)
    b = jnp.concatenate([jnp.tile(b_exp.reshape(1, N_EXPERTS), (1, reps)),
                         jnp.tile(jnp.repeat(b_grp, EXP_PER_GROUP).reshape(1, N_EXPERTS), (1, reps))], axis=1)
    w_hi = w.astype(BF16)
    return jnp.stack([w_hi, (w - w_hi.astype(F32)).astype(BF16)]), b


def _moe_body(h_ref, comb_ref, x1_ref, wg_ref, wu_ref, wd_ref, nf_ref, *out_refs, final):
    acc_ref = out_refs[-1]
    e = pl.program_id(1)

    @pl.when(e == 0)
    def _():
        acc_ref[...] = x1_ref[...]

    h = h_ref[...]
    comb = comb_ref[...]
    lane = lax.broadcasted_iota(jnp.int32, comb.shape, 1)
    c = jnp.sum(jnp.where(lane == e, comb, 0.0), axis=-1, keepdims=True)
    hg = jnp.dot(h, wg_ref[0], preferred_element_type=F32)
    hu = jnp.dot(h, wu_ref[0], preferred_element_type=F32)
    act = (jax.nn.silu(hg) * hu * c).astype(BF16)
    acc_ref[...] += jnp.dot(act, wd_ref[0], preferred_element_type=F32)

    @pl.when(e == N_EXPERTS - 1)
    def _():
        x2 = acc_ref[...]
        if final:
            out_refs[0][...] = x2 * lax.rsqrt(jnp.mean(x2 * x2, axis=-1, keepdims=True) + RMS_EPS) * nf_ref[...]
        else:
            out_refs[0][...] = x2


def _moe(h, comb, x1, lw, norm_final, tm, final):
    n = x1.shape[0]
    row = lambda i, e: (i, 0)
    per_e = lambda i, e: (e, 0, 0)
    return pl.pallas_call(
        functools.partial(_moe_body, final=final),
        grid=(n // tm, N_EXPERTS),
        in_specs=[pl.BlockSpec((tm, D_MODEL), row), pl.BlockSpec((tm, LANE), row), pl.BlockSpec((tm, D_MODEL), row),
                  pl.BlockSpec((1, D_MODEL, D_EXPERT), per_e), pl.BlockSpec((1, D_MODEL, D_EXPERT), per_e),
                  pl.BlockSpec((1, D_EXPERT, D_MODEL), per_e), pl.BlockSpec((1, D_MODEL), lambda i, e: (0, 0))],
        out_specs=pl.BlockSpec((tm, D_MODEL), row),
        out_shape=jax.ShapeDtypeStruct((n, D_MODEL), F32),
        scratch_shapes=[pltpu.VMEM((tm, D_MODEL), F32)],
        compiler_params=_cp("parallel", "arbitrary"),
        name="moe",
    )(h, comb, x1, lw['moe_wg'], lw['moe_wu'], lw['moe_wd'], norm_final.reshape(1, D_MODEL))


def _prep_layer(l, p):
    w_in = p['w_in'][l]
    o = _offsets(SPLIT_SIZES)
    segs = jnp.split(w_in, o, axis=1)
    w_all = jnp.concatenate([segs[0], segs[1], _pad_to(segs[2], LANE, 1), _pad_to(segs[3], SHIFT_PAD, 1), segs[4], segs[5]],
                            axis=1).astype(BF16)
    w_prec = w_in[:, :PRECISE_COLS]
    w_lo = (w_prec - w_prec.astype(BF16).astype(F32)).astype(BF16)
    lw = {'layer': l, 'w_all': w_all, 'w_lo': w_lo, 'norm_mix': p['norm_mix'][l]}
    lw['cmp'] = _cmp_weights(p['nsa_cmp_w1'][l], p['nsa_cmp_b1'][l], p['nsa_cmp_w2'][l])
    lw['rwkv_mu'] = _pad_to(p['rwkv_mu'][l].reshape(1, SHIFT_B), SHIFT_PAD, 1)
    lw['rwkv_vec'] = _pad_to(p['rwkv_vec'][l], 8, 0)
    z = lambda a, b: jnp.zeros((a, b), F32)
    lw['rwkv_wup'] = jnp.concatenate([p['rwkv_w_up'][l], z(LANE - LORA_W, C_B)], axis=0)
    lw['rwkv_aup'] = jnp.concatenate([z(LORA_W, C_B), p['rwkv_a_up'][l], z(LANE - LORA_W - LORA_A, C_B)], axis=0)
    lw['rwkv_gup'] = jnp.concatenate([z(LORA_W + LORA_A, C_B), p['rwkv_g_up'][l], z(LANE - LORA_W - LORA_A - LORA_G, C_B)], axis=0)
    for name in ('s5_lambda_re', 's5_lambda_im', 's5_b', 's5_c', 's5_d', 's5_log_step', 's5_w_glu', 'ret_gn'):
        lw[name] = p[name][l]
    lw['s5'], lw['s5_step'] = _s5_params(lw)
    lw['w_out'] = p['w_out'][l].astype(BF16)
    lw['norm_ffn'] = p['norm_ffn'][l].reshape(1, D_MODEL)
    lw['w_router'], lw['b_router'] = _router_weights(p['moe_w_grp'][l], p['moe_b_grp'][l], p['moe_w_exp'][l], p['moe_b_exp'][l])
    lw['moe_wg'] = p['moe_w_gate'][l].astype(BF16)
    lw['moe_wu'] = p['moe_w_up'][l].astype(BF16)
    lw['moe_wd'] = p['moe_w_down'][l].astype(BF16)
    return lw


ROW_TILE = 512
MOE_ROW_TILE = 1024
SCAN_TILE = 256


def _prompt_layer(x, lw, tabs, b, t, norm_final):
    cos_a, sin_a, ret_tabs = tabs
    q, rows, wrows, gate, colsb, u, colsd = _proj(x, lw['norm_mix'], lw['w_all'], lw['w_lo'], cos_a, sin_a, ROW_TILE)
    o_a = _nsa_prompt_mixer(q, rows, wrows, gate, lw, b, t)
    o_b, s_rwkv, s_shift = _rwkv_mixer(colsb, jnp.zeros((b, SHIFT_PAD), F32), jnp.zeros((b, H_B, HEAD_DIM, HEAD_DIM), F32),
                                       lw, b, t, ROW_TILE, b, SCAN_TILE)
    o_c, s_s5 = _s5_mixer(u, jnp.zeros((b, G_C, S5_P, 2), F32), lw, b, t, SCAN_TILE, False)
    o_d, s_ret = _ret_mixer(colsd, jnp.zeros((b, H_D, HEAD_DIM, HEAD_DIM), F32), lw, ret_tabs, b, t)
    x1, h, comb = _out_router(x, o_a, o_b, o_c, o_d, lw, ROW_TILE)
    x2 = _moe(h, comb, x1, lw, norm_final, MOE_ROW_TILE, lw['layer'] == DEPTH - 1)
    rows = rows.reshape(b, t, 4, N_KV_A, HEAD_DIM)
    win = wrows.reshape(b, t, 2, N_KV_A, HEAD_DIM)[:, t - min(WINDOW, t):]
    return x2, (rows, win, s_rwkv, s_shift, s_s5, s_ret)


def _sample_layer(x, lw, tabs, b, pos, cache_kv, page_table, win_buf, s_rwkv, s_shift, s_s5, s_ret, norm_final):
    cos_a, sin_a, ret_cs = tabs
    q, rows, wrows, gate, colsb, u, colsd = _proj(x, lw['norm_mix'], lw['w_all'], lw['w_lo'], cos_a, sin_a, b)
    o_a, win = _nsa_sample(q, rows, wrows, gate, cache_kv, win_buf, lw['layer'], page_table, lw['cmp'], int(pos[0]))
    rows = rows.reshape(b, 1, 4, N_KV_A, HEAD_DIM)
    o_b, s_rwkv = _rwkv_step(colsb, _pad_to(s_shift, SHIFT_PAD, 1), s_rwkv, lw['layer'], lw)
    s_shift = colsb[:, :SHIFT_B]
    o_c, s_s5 = _s5_step(u, s_s5, lw['layer'], lw)
    o_d, s_ret = _ret_step(colsd, ret_cs[0], ret_cs[1], s_ret, lw['layer'], lw['ret_gn'])
    x1, h, comb = _out_router(x, o_a, o_b, o_c, o_d, lw, b)
    x2 = _moe(h, comb, x1, lw, norm_final, b, lw['layer'] == DEPTH - 1)
    return x2, (rows, win, s_rwkv, s_shift, s_s5, s_ret)


def kernel(x_prompt, x_sample, cache_nsa_kv, cache_nsa_win, state_rwkv, state_rwkv_shift, state_s5, state_ret, page_table, norm_mix, w_in, nsa_cmp_w1, nsa_cmp_b1, nsa_cmp_w2, rwkv_mu, rwkv_vec, rwkv_w_up, rwkv_a_up, rwkv_g_up, s5_lambda_re, s5_lambda_im, s5_b, s5_c, s5_d, s5_log_step, s5_w_glu, ret_gn, w_out, norm_ffn, moe_w_grp, moe_b_grp, moe_w_exp, moe_b_exp, moe_w_gate, moe_w_up, moe_w_down, norm_final):
    p = dict(norm_mix=norm_mix, w_in=w_in, nsa_cmp_w1=nsa_cmp_w1, nsa_cmp_b1=nsa_cmp_b1, nsa_cmp_w2=nsa_cmp_w2,
             rwkv_mu=rwkv_mu, rwkv_vec=rwkv_vec, rwkv_w_up=rwkv_w_up, rwkv_a_up=rwkv_a_up, rwkv_g_up=rwkv_g_up,
             s5_lambda_re=s5_lambda_re, s5_lambda_im=s5_lambda_im, s5_b=s5_b, s5_c=s5_c, s5_d=s5_d,
             s5_log_step=s5_log_step, s5_w_glu=s5_w_glu, ret_gn=ret_gn, w_out=w_out, norm_ffn=norm_ffn,
             moe_w_grp=moe_w_grp, moe_b_grp=moe_b_grp, moe_w_exp=moe_w_exp, moe_b_exp=moe_b_exp,
             moe_w_gate=moe_w_gate, moe_w_up=moe_w_up, moe_w_down=moe_w_down)
    bp, tp = x_prompt.shape[:2]
    bs, ts = x_sample.shape[:2]
    assert ts == 1 and tp % ROW_TILE == 0 and (bp * tp) % MOE_ROW_TILE == 0 and tp % (4 * LANE) == 0 and bs % 8 == 0
    past_len = page_table.shape[1] * cache_nsa_kv.shape[2]
    pos_p = np.arange(tp)
    pos_s = past_len + np.arange(ts)
    c = RET_CHUNK if tp % RET_CHUNK == 0 else tp
    tabs_p = _rope_tables(pos_p, ROT_DIM, ROPE_THETA, HEAD_DIM, 2) + (_ret_tables(pos_p, c),)
    pos_rows = np.repeat(pos_s, bs)
    tabs_s = _rope_tables(pos_rows, ROT_DIM, ROPE_THETA, HEAD_DIM, 2) + (_rope_tables(pos_rows, HEAD_DIM, RET_THETA, HEAD_DIM, H_D),)
    xp = x_prompt.reshape(bp * tp, D_MODEL)
    xs = x_sample.reshape(bs * ts, D_MODEL)
    sts_p, sts_s = [], []
    for l in range(DEPTH):
        lw = _prep_layer(l, p)
        xp, st_p = _prompt_layer(xp, lw, tabs_p, bp, tp, norm_final)
        xs, st_s = _sample_layer(xs, lw, tabs_s, bs, pos_s, cache_nsa_kv, page_table, cache_nsa_win, state_rwkv,
                                     state_rwkv_shift[l], state_s5, state_ret, norm_final)
        rows, win, s1, s2, s3, s4 = st_s
        sts_s.append((rows, win, s1, s2, s3, s4))
        sts_p.append(st_p)
    new_p = [jnp.stack([st[i] for st in sts_p]) for i in range(6)]
    new_s = [jnp.stack([st[i] for st in sts_s]) for i in range(6)]
    return (xp.reshape(bp, tp, D_MODEL), xs.reshape(bs, ts, D_MODEL), new_p[0], new_s[0], new_p[1], new_s[1],
            new_p[2], new_s[2], new_p[3], new_s[3], new_p[4], new_s[4], new_p[5], new_s[5])
```

```python
import functools

import numpy as np
import jax
import jax.numpy as jnp
from jax import lax
from jax.experimental import pallas as pl
from jax.experimental.pallas import tpu as pltpu

F32 = jnp.float32
BF16 = jnp.bfloat16
HI = lax.Precision.HIGHEST

D_MODEL = 1024
DEPTH = 2
HEAD_DIM = 64
C_A = C_B = C_C = C_D = 256
H_A = 4
N_KV_A = 2
R_A = 2
ROT_DIM = 16
ROPE_THETA = 500000.0
CMP_BLOCK = 32
CMP_STRIDE = 16
CMP_HIDDEN = 128
SEL_BLOCK = 64
TOP_K = 16
WINDOW = 512
NEG_INF = -1e30
FORCED_SCORE = 1e9
BLOCKED_SCORE = -1e9
H_B = 4
LORA_W = 16
LORA_A = 16
LORA_G = 32
SHIFT_B = 832
SHIFT_PAD = 896
RWKV_GN_EPS = 64e-5
S5_CH = 16
G_C = 16
S5_P = 64
H_D = 4
RET_CHUNK = 128
RET_THETA = 10000.0
RET_GN_EPS = 1e-5
N_GROUPS = 4
EXP_PER_GROUP = 4
N_EXPERTS = 16
D_EXPERT = 256
RMS_EPS = 1e-6
SPLIT_SIZES = (C_A, 6 * N_KV_A * HEAD_DIM, 3 * H_A, SHIFT_B, C_C, 4 * C_D)
LANE = 128
VMEM_LIMIT = 56 * 1024 * 1024


def _cp(*sem):
    return pltpu.CompilerParams(dimension_semantics=sem, vmem_limit_bytes=VMEM_LIMIT)


def _offsets(sizes):
    return [int(s) for s in np.cumsum(sizes)[:-1]]


def _pad_to(a, n, axis):
    pad = [(0, 0)] * a.ndim
    pad[axis] = (0, n - a.shape[axis])
    return jnp.pad(a, pad)


def _block_ones(n, blk, dtype):
    r = lax.broadcasted_iota(jnp.int32, (n, n), 0) // blk
    c = lax.broadcasted_iota(jnp.int32, (n, n), 1) // blk
    return (r == c).astype(dtype)


def _rope_tables(pos, rot_dim, theta, period, reps):
    half = rot_dim // 2
    inv = theta ** (-jnp.arange(half, dtype=F32) / half)
    ang = jnp.asarray(pos, F32)[:, None] * inv[None, :]
    cos, sin = jnp.cos(ang), jnp.sin(ang)
    n = ang.shape[0]
    rest = period - rot_dim
    c = jnp.concatenate([cos, cos, jnp.ones((n, rest), F32)], -1)
    s = jnp.concatenate([-sin, sin, jnp.zeros((n, rest), F32)], -1)
    return jnp.tile(c, (1, reps)), jnp.tile(s, (1, reps))


PRECISE_COLS = 384


def _proj_body(x_ref, nw_ref, w_ref, wlo_ref, cos_ref, sin_ref, q_ref, kv_ref, g_ref, cb_ref, u_ref, cd_ref):
    x = x_ref[...]
    h = x * lax.rsqrt(jnp.mean(x * x, axis=-1, keepdims=True) + RMS_EPS) * nw_ref[...]
    hb = h.astype(BF16)
    h_lo = (h - hb.astype(F32)).astype(BF16)
    c = cos_ref[...]
    s = sin_ref[...]
    first = (lax.broadcasted_iota(jnp.int32, c.shape, 1) % HEAD_DIM) < (ROT_DIM // 2)

    def rope(z):
        sw = jnp.where(first, pltpu.roll(z, LANE - ROT_DIM // 2, 1), pltpu.roll(z, ROT_DIM // 2, 1))
        return z * c + sw * s

    def dot(a, b):
        z = jnp.dot(hb, w_ref[:, a:b], preferred_element_type=F32)
        if b <= PRECISE_COLS:
            z = z + (jnp.dot(hb, wlo_ref[:, a:b], preferred_element_type=F32)
                     + jnp.dot(h_lo, w_ref[:, a:b], preferred_element_type=F32))
        return z

    for j in range(2):
        q_ref[:, LANE * j:LANE * (j + 1)] = rope(dot(LANE * j, LANE * (j + 1)))
    for j in range(6):
        z = dot(256 + LANE * j, 256 + LANE * (j + 1))
        kv_ref[:, LANE * j:LANE * (j + 1)] = rope(z) if j % 2 == 0 else z
    g_ref[...] = dot(1024, 1152)
    cb_ref[...] = dot(1152, 2048)
    u_ref[...] = dot(2048, 2304)
    cd_ref[...] = dot(2304, 3328)


def _proj(x2d, norm_w, w_all, w_lo, cos_t, sin_t, tm):
    n = x2d.shape[0]
    t_tiles = cos_t.shape[0] // tm
    row = lambda i: (i, 0)
    fixed = lambda i: (0, 0)
    tab = lambda i: (i % t_tiles, 0)
    widths = (256, 768, 128, SHIFT_PAD, 256, 1024)
    return pl.pallas_call(
        _proj_body,
        grid=(n // tm,),
        in_specs=[pl.BlockSpec((tm, D_MODEL), row), pl.BlockSpec((1, D_MODEL), fixed),
                  pl.BlockSpec((D_MODEL, 3328), fixed), pl.BlockSpec((D_MODEL, PRECISE_COLS), fixed),
                  pl.BlockSpec((tm, LANE), tab), pl.BlockSpec((tm, LANE), tab)],
        out_specs=[pl.BlockSpec((tm, w), row) for w in widths],
        out_shape=[jax.ShapeDtypeStruct((n, w), F32) for w in widths],
        compiler_params=_cp("parallel"),
        name="proj",
    )(x2d, norm_w.reshape(1, D_MODEL), w_all, w_lo, cos_t, sin_t)


def _cmp_mlp(xc, kind, w1_ref, b1_ref, w2_ref, w1k_ref, w2k_ref, nck):
    if kind == 0:
        hh = _dot3(xc, w1k_ref[...])
    else:
        hh = jnp.dot(xc.astype(BF16), w1_ref[kind], preferred_element_type=F32)
    hs = []
    for g in range(N_KV_A):
        hg = hh[g * nck:(g + 1) * nck]
        hs.append(jax.nn.gelu(b1_ref[kind] + hg[:, :CMP_HIDDEN] + pltpu.roll(hg[:, CMP_HIDDEN:], nck - 1, 0)))
    act = jnp.concatenate(hs, axis=1)
    if kind == 0:
        return _dot3(act, w2k_ref[...])
    return jnp.dot(act.astype(BF16), w2_ref[kind], preferred_element_type=F32)


def _cmp_body(xk_ref, xv_ref, w1_ref, b1_ref, w2_ref, w1k_ref, w2k_ref, kc_ref, vc_ref, vct_ref, xc_ref, *, n_chunks):
    lane = lax.broadcasted_iota(jnp.int32, (n_chunks, LANE), 1)
    lo = lane < HEAD_DIM
    for pair in range(CMP_STRIDE // 2):
        for kind, x_ref in enumerate((xk_ref, xv_ref)):
            ak = x_ref[0, pl.ds(2 * pair, n_chunks, stride=CMP_STRIDE), :]
            bk = x_ref[0, pl.ds(2 * pair + 1, n_chunks, stride=CMP_STRIDE), :]
            xc_ref[kind, 0:n_chunks, LANE * pair:LANE * (pair + 1)] = jnp.where(lo, ak, pltpu.roll(bk, HEAD_DIM, 1))
            xc_ref[kind, n_chunks:2 * n_chunks, LANE * pair:LANE * (pair + 1)] = jnp.where(lo, pltpu.roll(ak, HEAD_DIM, 1), bk)
    outs = [_cmp_mlp(xc_ref[kind], kind, w1_ref, b1_ref, w2_ref, w1k_ref, w2k_ref, n_chunks) for kind in range(2)]
    kc_ref[0] = outs[0]
    vc_ref[0] = outs[1]
    vct_ref[0] = outs[1].T


def _nsa_compress(rows, w1cat, b1, w2bd, w1k, w2k):
    b, tk = rows.shape[0], rows.shape[1]
    n_chunks = tk // CMP_STRIDE
    fixed3 = lambda i: (0, 0, 0)
    fixed2 = lambda i: (0, 0)
    return pl.pallas_call(
        functools.partial(_cmp_body, n_chunks=n_chunks),
        grid=(b,),
        in_specs=[pl.BlockSpec((1, tk, LANE), lambda i: (i, 0, 0)), pl.BlockSpec((1, tk, LANE), lambda i: (i, 0, 1)),
                  pl.BlockSpec((2, CMP_STRIDE * HEAD_DIM, 2 * CMP_HIDDEN), fixed3),
                  pl.BlockSpec((2, 1, CMP_HIDDEN), fixed3),
                  pl.BlockSpec((2, 2 * CMP_HIDDEN, LANE), fixed3),
                  pl.BlockSpec((CMP_STRIDE * HEAD_DIM, 2 * CMP_HIDDEN), fixed2), pl.BlockSpec((2 * CMP_HIDDEN, LANE), fixed2)],
        out_specs=[pl.BlockSpec((1, n_chunks, LANE), lambda i: (i, 0, 0)),
                   pl.BlockSpec((1, n_chunks, LANE), lambda i: (i, 0, 0)),
                   pl.BlockSpec((1, LANE, n_chunks), lambda i: (i, 0, 0))],
        out_shape=[jax.ShapeDtypeStruct((b, n_chunks, LANE), F32), jax.ShapeDtypeStruct((b, n_chunks, LANE), F32),
                   jax.ShapeDtypeStruct((b, LANE, n_chunks), F32)],
        scratch_shapes=[pltpu.VMEM((2, N_KV_A * n_chunks, CMP_STRIDE * HEAD_DIM), F32)],
        compiler_params=_cp("parallel"),
        name="nsa_compress",
    )(rows, rows, w1cat, b1, w2bd, w1k, w2k)


def _cmp_weights(cmp_w1, cmp_b1, cmp_w2):
    m = CMP_BLOCK // CMP_STRIDE
    w1r = cmp_w1.reshape(2, m, CMP_STRIDE * HEAD_DIM, CMP_HIDDEN)
    w1cat = jnp.concatenate([w1r[:, j] for j in range(m)], axis=-1)
    z = jnp.zeros_like(cmp_w2)
    w2bd = jnp.concatenate([jnp.concatenate([cmp_w2, z], -1), jnp.concatenate([z, cmp_w2], -1)], axis=1)
    return w1cat.astype(BF16), cmp_b1.reshape(2, 1, CMP_HIDDEN), w2bd.astype(BF16), w1cat[0], w2bd[0]


def _cmp_to_sel_t(n_chunks, n_cmp, n_sel):
    starts = np.arange(n_chunks) * CMP_STRIDE
    sel_s = np.arange(n_sel) * SEL_BLOCK
    ov = np.minimum(starts[:, None] + CMP_BLOCK, sel_s[None] + SEL_BLOCK) - np.maximum(starts[:, None], sel_s[None])
    ov = np.clip(ov, 0, None) / CMP_BLOCK
    ov[n_cmp:] = 0.0
    return jnp.asarray(ov.T, dtype=F32)


def _masked_softmax_cols(s, mask):
    m = jnp.max(jnp.where(mask, s, NEG_INF), axis=0, keepdims=True)
    e = jnp.where(mask, jnp.exp(s - m), 0.0)
    den = jnp.sum(e, axis=0, keepdims=True)
    return e * jnp.where(den > 0.0, 1.0 / den, 0.0)


def _nsa_prompt_body(qt_ref, gt_ref, kc_ref, vct_ref, ovt_ref, ks_ref, vst_ref, kw_ref, vwt_ref, o_ref, sel_ref,
                     *, n_cmp, n_sel, qb_size):
    qb = pl.program_id(1)
    tq = qb_size
    n_chunks = kc_ref.shape[1]
    qpos = qb * tq + lax.broadcasted_iota(jnp.int32, (1, tq), 1)
    qpos2 = jnp.concatenate([qpos, qpos], axis=1)
    zeros_q = jnp.zeros((HEAD_DIM, 2 * tq), F32)
    gates = jax.nn.sigmoid(gt_ref[0])
    kc = kc_ref[0]
    n_idx = lax.broadcasted_iota(jnp.int32, (n_chunks, 2 * tq), 0)
    cmp_mask = (n_idx * CMP_STRIDE + (CMP_BLOCK - 1) <= qpos2) & (n_idx < n_cmp)
    blk = lax.broadcasted_iota(jnp.int32, (n_sel, tq), 0)
    cur = qpos // SEL_BLOCK
    forced = (blk == 0) | (blk == cur) | (blk == cur - 1)
    causal_blk = blk * SEL_BLOCK <= qpos
    qpads, o_cmps = [], []

    for g in range(N_KV_A):
        q64 = jnp.concatenate([qt_ref[0, (2 * g) * HEAD_DIM:(2 * g + 1) * HEAD_DIM, :],
                               qt_ref[0, (2 * g + 1) * HEAD_DIM:(2 * g + 2) * HEAD_DIM, :]], axis=1) * (HEAD_DIM ** -0.5)
        qpad32 = jnp.concatenate([q64, zeros_q], axis=0) if g == 0 else jnp.concatenate([zeros_q, q64], axis=0)
        qpad = qpad32.astype(BF16)
        qpads.append(qpad)

        p = _masked_softmax_cols(_dot3(kc, qpad32), cmp_mask)
        o_cmps.append(jnp.dot(vct_ref[0, g * HEAD_DIM:(g + 1) * HEAD_DIM, :].astype(BF16), p.astype(BF16),
                              preferred_element_type=F32))
        psum = p[:, :tq] + p[:, tq:]
        imp = jnp.dot(ovt_ref[...], psum, precision=HI, preferred_element_type=F32)
        imp = jnp.where(forced, FORCED_SCORE, jnp.where(causal_blk, imp, BLOCKED_SCORE))
        rank = jnp.zeros((n_sel, tq), F32)
        for i in range(n_sel):
            row = imp[i:i + 1, :]
            rank = rank + jnp.where((row > imp) | ((row == imp) & (blk > i)), 1.0, 0.0)
        sel_ref[g] = jnp.where(rank < float(min(TOP_K, n_sel)), 1.0, 0.0)

    q_all = jnp.concatenate(qpads, axis=1)
    qpos4 = jnp.concatenate([qpos2, qpos2], axis=1)

    def attend(j, carry, k_ref, vt_ref, use_sel, tk, causal=True):
        m, l, acc = carry
        off = pl.multiple_of(j * tk, tk)
        kt = k_ref[0, pl.ds(off, tk), :].astype(BF16)
        s = jnp.dot(kt, q_all, preferred_element_type=F32)
        if causal:
            diff = qpos4 - (off + lax.broadcasted_iota(jnp.int32, (tk, 4 * tq), 0))
        if use_sel:
            per_tile = tk // SEL_BLOCK
            sels = []
            for g in range(N_KV_A):
                rows = [jnp.broadcast_to(sel_ref[g, pl.ds(j * per_tile + a, 1), :], (SEL_BLOCK, tq)) for a in range(per_tile)]
                selm = jnp.concatenate(rows, axis=0)
                sels += [selm, selm]
            mask = jnp.concatenate(sels, axis=1) > 0.0
            if causal:
                mask = mask & (diff >= 0)
        else:
            mask = (diff >= 0) & (diff < WINDOW)
        m_new = jnp.maximum(m, jnp.max(jnp.where(mask, s, NEG_INF), axis=0, keepdims=True))
        alpha = jnp.exp(m - m_new)
        e = jnp.where(mask, jnp.exp(s - m_new), 0.0)
        l_new = alpha * l + jnp.sum(e, axis=0, keepdims=True)
        vt = vt_ref[0, :, pl.ds(off, tk)].astype(BF16)
        return m_new, l_new, alpha * acc + jnp.dot(vt, e.astype(BF16), preferred_element_type=F32)

    init = (jnp.full((1, 4 * tq), NEG_INF, F32), jnp.zeros((1, 4 * tq), F32), jnp.zeros((2 * HEAD_DIM, 4 * tq), F32))
    tk_s, tk_w = 4 * tq, 2 * tq
    last_s = (qb * tq) // tk_s
    slc = functools.partial(attend, k_ref=ks_ref, vt_ref=vst_ref, use_sel=True, tk=tk_s)
    _, l_s, acc_s = slc(last_s, lax.fori_loop(0, last_s, functools.partial(slc, causal=False), init))
    _, l_w, acc_w = lax.fori_loop(jnp.maximum(qb * tq - WINDOW, 0) // tk_w, (qb * tq + tk_w) // tk_w,
                                  functools.partial(attend, k_ref=kw_ref, vt_ref=vwt_ref, use_sel=False, tk=tk_w), init)

    for g in range(N_KV_A):
        blk_g = (slice(g * HEAD_DIM, (g + 1) * HEAD_DIM), slice(g * 2 * tq, (g + 1) * 2 * tq))
        o_slc = acc_s[blk_g] / l_s[:, blk_g[1]]
        o_win = acc_w[blk_g] / l_w[:, blk_g[1]]
        for r in range(R_A):
            h = 2 * g + r
            gr = gates[3 * h:3 * h + 3, :]
            sl = slice(r * tq, (r + 1) * tq)
            o_ref[0, h * HEAD_DIM:(h + 1) * HEAD_DIM, :] = (gr[0:1] * o_cmps[g][:, sl] + gr[1:2] * o_slc[:, sl]
                                                          + gr[2:3] * o_win[:, sl])


def _nsa_prompt(qt, gt, kc, vct, ovt, kv, vst, vwt, n_cmp):
    b, _, t = qt.shape
    tq = 128
    n_sel = t // SEL_BLOCK
    n_chunks = kc.shape[1]
    per_b = lambda i, j: (i, 0, 0)
    return pl.pallas_call(
        functools.partial(_nsa_prompt_body, n_cmp=n_cmp, n_sel=n_sel, qb_size=tq),
        grid=(b, t // tq),
        in_specs=[pl.BlockSpec((1, 256, tq), lambda i, j: (i, 0, j)),
                  pl.BlockSpec((1, 16, tq), lambda i, j: (i, 0, j)),
                  pl.BlockSpec((1, n_chunks, LANE), per_b),
                  pl.BlockSpec((1, LANE, n_chunks), per_b),
                  pl.BlockSpec((n_sel, n_chunks), lambda i, j: (0, 0)),
                  pl.BlockSpec((1, t, LANE), lambda i, j: (i, 0, 2)),
                  pl.BlockSpec((1, LANE, t), per_b),
                  pl.BlockSpec((1, t, LANE), lambda i, j: (i, 0, 4)),
                  pl.BlockSpec((1, LANE, t), per_b)],
        out_specs=pl.BlockSpec((1, 256, tq), lambda i, j: (i, 0, j)),
        out_shape=jax.ShapeDtypeStruct((b, 256, t), F32),
        scratch_shapes=[pltpu.VMEM((N_KV_A, n_sel, tq), F32)],
        compiler_params=_cp("parallel", "arbitrary"),
        name="nsa_prompt",
    )(qt, gt, kc, vct, ovt, kv, vst, kv, vwt)


def _nsa_prompt_mixer(q, kv, gate, lw, b, t):
    kv3 = kv.reshape(b, t, 6 * LANE)
    n_chunks = t // CMP_STRIDE
    n_cmp = (t - CMP_BLOCK) // CMP_STRIDE + 1
    kc, _, vct = _nsa_compress(kv3, *lw['cmp'])
    ovt = _cmp_to_sel_t(n_chunks, n_cmp, t // SEL_BLOCK)
    qt = jnp.swapaxes(q.reshape(b, t, 256), 1, 2)
    gt = jnp.swapaxes(gate.reshape(b, t, LANE)[:, :, :16], 1, 2)
    vst = jnp.swapaxes(kv3[:, :, 3 * LANE:4 * LANE], 1, 2)
    vwt = jnp.swapaxes(kv3[:, :, 5 * LANE:], 1, 2)
    ot = _nsa_prompt(qt, gt, kc, vct, ovt, kv3, vst, vwt, n_cmp)
    return jnp.swapaxes(ot, 1, 2).reshape(b * t, 256)


def _softmax_rows_with_extra(s, mask, s_new):
    m = jnp.maximum(jnp.max(jnp.where(mask, s, NEG_INF), axis=-1, keepdims=True), s_new)
    e = jnp.where(mask, jnp.exp(s - m), 0.0)
    e_new = jnp.exp(s_new - m)
    return e, e_new, 1.0 / (jnp.sum(e, axis=-1, keepdims=True) + e_new)


def _nsa_sample_body(pt_ref, *refs, n_pages, page, pos, n_sel, n_cmp, wb):
    del pt_ref
    n_in = n_pages
    pages = refs[:n_in]
    (qbd_ref, new_ref, gate_ref, win_ref, ov_ref, w1_ref, b1_ref, w2_ref, w1k_ref, w2k_ref,
     o_ref, nw_ref, tok_ref, xc_ref) = refs[n_in:]
    pg = lambda p, kind: pages[p].at[0, kind:kind + 1]
    nck = n_pages * page // CMP_STRIDE
    lane8 = lax.broadcasted_iota(jnp.int32, (8, LANE), 1)
    lo8 = lane8 < HEAD_DIM
    nt = lambda a, b: lax.dot_general(a, b, (((1,), (1,)), ((), ())), preferred_element_type=F32)

    lo_c = lax.broadcasted_iota(jnp.int32, (nck, LANE), 1) < HEAD_DIM
    for kind in range(2):
        for p in range(n_pages):
            tok_ref[kind, p * page:(p + 1) * page, :] = pg(p, kind)[0].T
        for pair in range(CMP_STRIDE // 2):
            a = tok_ref[kind, pl.ds(2 * pair, nck, stride=CMP_STRIDE), :]
            b = tok_ref[kind, pl.ds(2 * pair + 1, nck, stride=CMP_STRIDE), :]
            cols = slice(LANE * pair, LANE * (pair + 1))
            xc_ref[kind, 0:nck, cols] = jnp.where(lo_c, a, pltpu.roll(b, HEAD_DIM, 1))
            xc_ref[kind, nck:2 * nck, cols] = jnp.where(lo_c, pltpu.roll(a, HEAD_DIM, 1), b)
    kc, vc = [_cmp_mlp(xc_ref[kind], kind, w1_ref, b1_ref, w2_ref, w1k_ref, w2k_ref, nck) for kind in range(2)]

    q = qbd_ref[0] * (HEAD_DIM ** -0.5)
    qb = q.astype(BF16)
    new = new_ref[0]

    n_idx = lax.broadcasted_iota(jnp.int32, (8, nck), 1)
    cmask = (n_idx * CMP_STRIDE + (CMP_BLOCK - 1) <= pos) & (n_idx < n_cmp)
    s = _dot3(q, kc, _NT)
    m = jnp.max(jnp.where(cmask, s, NEG_INF), axis=-1, keepdims=True)
    e = jnp.where(cmask, jnp.exp(s - m), 0.0)
    den = jnp.sum(e, axis=-1, keepdims=True)
    p_cmp = e * jnp.where(den > 0.0, 1.0 / den, 0.0)
    o_cmp = jnp.dot(p_cmp.astype(BF16), vc.astype(BF16), preferred_element_type=F32)

    row8 = lax.broadcasted_iota(jnp.int32, (8, nck), 0)
    psum = jnp.where(row8 == 0, p_cmp[0:1] + p_cmp[1:2], jnp.where(row8 == 1, p_cmp[2:3] + p_cmp[3:4], 0.0))
    imp = jnp.dot(psum, ov_ref[...], precision=HI, preferred_element_type=F32)
    cur = pos // SEL_BLOCK
    forced = (lane8 == 0) | (lane8 == cur) | (lane8 == cur - 1)
    imp = jnp.where(forced, FORCED_SCORE, jnp.where(lane8 * SEL_BLOCK <= pos, imp, BLOCKED_SCORE))
    imp = jnp.where(lane8 < n_sel, imp, -3e38)
    rank = jnp.zeros((8, LANE), F32)
    for i in range(n_sel):
        col = imp[:, i:i + 1]
        rank = rank + jnp.where((col > imp) | ((col == imp) & (lane8 > i)), 1.0, 0.0)
    sel = jnp.where((rank < float(min(TOP_K, n_sel))) & (lane8 < n_sel), 1.0, 0.0)
    rsel = lax.broadcasted_iota(jnp.int32, (8, LANE), 0)
    selh = jnp.where(rsel < R_A, sel[0:1], jnp.where(rsel < 2 * R_A, sel[1:2], 0.0))

    per_page = page // SEL_BLOCK
    s_t, m_t = [], []
    for p in range(n_pages):
        s_t.append(jnp.dot(qb, pg(p, 2)[0].astype(BF16), preferred_element_type=F32))
        blk_sel = selh[:, per_page * p:per_page * p + 1]
        for a in range(1, per_page):
            blk_sel = jnp.where(lane8 < a * SEL_BLOCK, blk_sel, selh[:, per_page * p + a:per_page * p + a + 1])
        kpos = p * page + lane8
        m_t.append((blk_sel > 0.0) & (kpos <= pos))
    s_all = jnp.concatenate(s_t, axis=1)
    mk_all = jnp.concatenate(m_t, axis=1)
    s_new = jnp.sum(q * new[2:3], axis=-1, keepdims=True)
    e, e_new, inv = _softmax_rows_with_extra(s_all, mk_all, s_new)
    acc = e_new * new[3:4]
    for p in range(n_pages):
        acc = acc + nt(e[:, p * page:(p + 1) * page].astype(BF16), pg(p, 3)[0].astype(BF16))
    o_slc = acc * inv

    kw, vw = win_ref[0], win_ref[1]
    widx = lax.broadcasted_iota(jnp.int32, (8, wb), 1)
    diff = wb - widx
    s_w = jnp.dot(qb, kw.astype(BF16), preferred_element_type=F32)
    s_wnew = jnp.sum(q * new[4:5], axis=-1, keepdims=True)
    e, e_new, inv = _softmax_rows_with_extra(s_w, (diff >= 0) & (diff < WINDOW), s_wnew)
    o_win = (nt(e.astype(BF16), vw.astype(BF16)) + e_new * new[5:6]) * inv

    gts = jax.nn.sigmoid(gate_ref[0])
    o = gts[:, 0:1] * o_cmp + gts[:, 1:2] * o_slc + gts[:, 2:3] * o_win
    lo1 = lo8[0:1]
    o_ref[0] = jnp.concatenate([jnp.where(lo1, o[0:1], pltpu.roll(o[1:2], HEAD_DIM, 1)),
                                jnp.where(lo1, pltpu.roll(o[2:3], HEAD_DIM, 1), o[3:4])], axis=1)
    last = lax.broadcasted_iota(jnp.int32, (LANE, wb), 1) == wb - 1
    new_t = new.T
    nw_ref[0] = jnp.where(last, new_t[:, 4:5], pltpu.roll(kw, wb - 1, 1))
    nw_ref[1] = jnp.where(last, new_t[:, 5:6], pltpu.roll(vw, wb - 1, 1))


def _nsa_sample(q, kv, gate, cache_kv, cache_win, layer, page_table, cmp_w, pos):
    b = q.shape[0]
    n_pool, page = cache_kv.shape[1:3]
    n_pages = page_table.shape[1]
    wb = cache_win.shape[2]
    assert wb == WINDOW and page % CMP_STRIDE == 0 and page % SEL_BLOCK == 0 and page == LANE
    tk = n_pages * page + 1
    n_cmp = (tk - CMP_BLOCK) // CMP_STRIDE + 1
    nck = n_pages * page // CMP_STRIDE
    assert n_cmp <= nck
    n_sel = -(-tk // SEL_BLOCK)
    assert n_sel <= LANE and pos // SEL_BLOCK == n_sel - 1
    ov = _pad_to(_cmp_to_sel_t(nck, n_cmp, n_sel).T, LANE, 1)
    w1cat, b1, w2bd, w1k, w2k = cmp_w
    q4 = q.reshape(b, H_A, HEAD_DIM)
    z = jnp.zeros_like(q4)
    first = (jnp.arange(H_A) // R_A == 0)[None, :, None]
    qbd = jnp.concatenate([jnp.where(first, q4, z), jnp.where(first, z, q4)], axis=-1)
    qbd = _pad_to(qbd, 8, 1)
    new = _pad_to(kv.reshape(b, 6, LANE), 8, 1)
    g8 = _pad_to(_pad_to(gate[:, :3 * H_A].reshape(b, H_A, 3), LANE, 2), 8, 1)
    cache3 = jnp.transpose(cache_kv, (0, 1, 3, 4, 5, 2)).reshape(-1, 4, LANE, page)
    win3 = jnp.transpose(cache_win, (0, 1, 3, 4, 5, 2)).reshape(-1, LANE, wb)
    page_specs = [pl.BlockSpec((1, 4, LANE, page), functools.partial(lambda i, pt, p: (layer * n_pool + pt[i, p], 0, 0, 0), p=p))
                  for p in range(n_pages)]
    per_b = lambda i, pt: (i, 0, 0)
    fixed2 = lambda i, pt: (0, 0)
    fixed3 = lambda i, pt: (0, 0, 0)
    grid_spec = pltpu.PrefetchScalarGridSpec(
        num_scalar_prefetch=1,
        grid=(b,),
        in_specs=page_specs + [pl.BlockSpec((1, 8, LANE), per_b), pl.BlockSpec((1, 8, LANE), per_b), pl.BlockSpec((1, 8, LANE), per_b),
                               pl.BlockSpec((2, LANE, wb), lambda i, pt: (layer * b + i, 0, 0)), pl.BlockSpec((nck, LANE), fixed2),
                               pl.BlockSpec((2, CMP_STRIDE * HEAD_DIM, 2 * CMP_HIDDEN), fixed3),
                               pl.BlockSpec((2, 1, CMP_HIDDEN), fixed3), pl.BlockSpec((2, 2 * CMP_HIDDEN, LANE), fixed3),
                               pl.BlockSpec((CMP_STRIDE * HEAD_DIM, 2 * CMP_HIDDEN), fixed2),
                               pl.BlockSpec((2 * CMP_HIDDEN, LANE), fixed2)],
        out_specs=[pl.BlockSpec((1, 1, 256), per_b), pl.BlockSpec((2, LANE, wb), per_b)],
        scratch_shapes=[pltpu.VMEM((2, n_pages * page, LANE), F32), pltpu.VMEM((2, 2 * nck, CMP_STRIDE * HEAD_DIM), F32)],
    )
    o, nw = pl.pallas_call(
        functools.partial(_nsa_sample_body, n_pages=n_pages, page=page, pos=pos, n_sel=n_sel, n_cmp=n_cmp, wb=wb),
        grid_spec=grid_spec,
        out_shape=[jax.ShapeDtypeStruct((b, 1, 256), F32), jax.ShapeDtypeStruct((2 * b, LANE, wb), F32)],
        compiler_params=_cp("parallel"),
        name="nsa_sample",
    )(page_table, *([cache3] * n_pages), qbd, new, g8, win3, ov, w1cat, b1, w2bd, w1k, w2k)
    return o.reshape(b, 256), jnp.transpose(nw.reshape(b, 2, N_KV_A, HEAD_DIM, wb), (0, 4, 1, 2, 3))


def _rwkv_prep_body(c_ref, s0_ref, mu_ref, vec_ref, wup_ref, aup_ref, gup_ref,
                    r_ref, lw_ref, k_ref, v_ref, kk_ref, ka_ref, g_ref, bonus_ref, carry_ref, *, tiles_per_seq):
    i = pl.program_id(0)
    cols = c_ref[...]
    tm = cols.shape[0]

    @pl.when(i % tiles_per_seq == 0)
    def _():
        carry_ref[...] = s0_ref[0]

    prev = pltpu.roll(cols, 1, 0)
    row0 = lax.broadcasted_iota(jnp.int32, cols.shape, 0) == 0
    prev = jnp.where(row0, carry_ref[...], prev)
    carry_ref[...] = cols[tm - 1:tm, :]
    _rwkv_features(cols, prev, mu_ref, vec_ref, wup_ref, aup_ref, gup_ref,
                   r_ref, lw_ref, k_ref, v_ref, kk_ref, ka_ref, g_ref, bonus_ref)


def _rwkv_step_prep_body(c_ref, s0_ref, mu_ref, vec_ref, wup_ref, aup_ref, gup_ref,
                         r_ref, lw_ref, k_ref, v_ref, kk_ref, ka_ref, g_ref, bonus_ref):
    _rwkv_features(c_ref[...], s0_ref[...], mu_ref, vec_ref, wup_ref, aup_ref, gup_ref,
                   r_ref, lw_ref, k_ref, v_ref, kk_ref, ka_ref, g_ref, bonus_ref)


def _rwkv_features(cols, prev, mu_ref, vec_ref, wup_ref, aup_ref, gup_ref,
                   r_ref, lw_ref, k_ref, v_ref, kk_ref, ka_ref, g_ref, bonus_ref):
    xs = cols + mu_ref[...] * (prev - cols)
    r, k, v, lo = xs[:, 0:256], xs[:, 256:512], xs[:, 512:768], xs[:, 768:896]
    vec = vec_ref[...]
    w0, a0, k_k, k_a, r_k = vec[0:1], vec[1:2], vec[2:3], vec[3:4], vec[4:5]
    dot_hi = lambda x, w: jnp.dot(x, w, precision=HI, preferred_element_type=F32)
    w_log = -jax.nn.softplus(-(w0 + dot_hi(jnp.tanh(lo), wup_ref[...]))) - 0.5
    a = jax.nn.sigmoid(a0 + dot_hi(lo, aup_ref[...]))
    g_ref[...] = dot_hi(jax.nn.sigmoid(lo), gup_ref[...])
    ones = _block_ones(256, HEAD_DIM, F32)
    kk = k * k_k
    kk = kk * lax.rsqrt(dot_hi(kk * kk, ones) + 1e-12)
    k2 = k * (1.0 + (a - 1.0) * k_a)
    r_ref[...] = r
    lw_ref[...] = -jnp.exp(w_log)
    k_ref[...] = k2
    v_ref[...] = v
    kk_ref[...] = kk
    ka_ref[...] = kk * a
    bonus_ref[...] = dot_hi(r * k2 * r_k, ones) * v


def _rwkv_prep(colsb, shift0, lw, t, tm):
    n = colsb.shape[0]
    tiles_per_seq = t // tm
    row = lambda i: (i, 0)
    fixed = lambda i: (0, 0)
    outs = [jax.ShapeDtypeStruct((n, 256), F32)] * 8
    return pl.pallas_call(
        functools.partial(_rwkv_prep_body, tiles_per_seq=tiles_per_seq),
        grid=(n // tm,),
        in_specs=[pl.BlockSpec((tm, SHIFT_PAD), row),
                  pl.BlockSpec((1, 1, SHIFT_PAD), lambda i: (i // tiles_per_seq, 0, 0)),
                  pl.BlockSpec((1, SHIFT_PAD), fixed), pl.BlockSpec((8, 256), fixed),
                  pl.BlockSpec((LANE, 256), fixed), pl.BlockSpec((LANE, 256), fixed), pl.BlockSpec((LANE, 256), fixed)],
        out_specs=[pl.BlockSpec((tm, 256), row)] * 8,
        out_shape=outs,
        scratch_shapes=[pltpu.VMEM((1, SHIFT_PAD), F32)],
        compiler_params=_cp("arbitrary"),
        name="rwkv_prep",
    )(colsb, shift0.reshape(-1, 1, SHIFT_PAD), lw['rwkv_mu'], lw['rwkv_vec'], lw['rwkv_wup'], lw['rwkv_aup'], lw['rwkv_gup'])


RWKV_CHUNK = 64


def _split_bf16(x):
    hi = x.astype(BF16)
    return hi, (x - hi.astype(F32)).astype(BF16)


def _dot3(a, b, dims=(((1,), (0,)), ((), ()))):
    ah, al = _split_bf16(a)
    bh, bl = _split_bf16(b)
    dg = lambda x, y: lax.dot_general(x, y, dims, preferred_element_type=F32)
    return dg(ah, bh) + dg(ah, bl) + dg(al, bh)


_NT = (((1,), (1,)), ((), ()))


def _rwkv_chunk_body(r_ref, lw_ref, k_ref, v_ref, kk_ref, ka_ref, g_ref, bonus_ref, s0_ref, vec_ref,
                     o_ref, st_ref, s_scr, y_scr, *, nb, tl):
    L = RWKV_CHUNK
    nc = tl // L

    @pl.when(pl.program_id(1) == 0)
    def _():
        s_scr[...] = s0_ref[...]

    ri = lax.broadcasted_iota(jnp.int32, (L, L), 0)
    ci = lax.broadcasted_iota(jnp.int32, (L, L), 1)
    strict, incl = ri > ci, ri >= ci
    ltri = incl.astype(F32)
    eye = (ri == ci).astype(F32)

    bnn = (((2,), (1,)), ((0,), (0,)))
    bnt = (((2,), (2,)), ((0,), (0,)))

    def chunk(c, carry):
        rows = pl.ds(pl.multiple_of(c * L, L), L)
        lhs_l, rhs_l, v_l, kw_l, wl_l = [], [], [], [], []
        for b in range(nb):
            r, lw, k, v, kk, ka = [ref[b, rows, :] for ref in (r_ref, lw_ref, k_ref, v_ref, kk_ref, ka_ref)]
            cl = jnp.dot(ltri, lw, precision=HI, preferred_element_type=F32)
            e_neg = jnp.exp(-cl)
            e_rem = jnp.exp(cl[L - 1:L, :] - cl)
            kkd, rd = kk * jnp.exp(cl - lw), r * jnp.exp(cl)
            kinv, kainv, kw, kaw = k * e_neg, ka * e_neg, k * e_rem, ka * e_rem
            w_last = jnp.exp(cl[L - 1:L, :])
            for h in range(H_B):
                sl = slice(h * HEAD_DIM, (h + 1) * HEAD_DIM)
                lhs_l.append(jnp.concatenate([kkd[:, sl], rd[:, sl]], axis=0))
                rhs_l.append(jnp.concatenate([kinv[:, sl], kainv[:, sl]], axis=0))
                v_l.append(v[:, sl])
                kw_l.append(jnp.concatenate([kw[:, sl], kaw[:, sl]], axis=0))
                wl_l.append(w_last[:, sl])
        lhs, rhs, vs, kws, wl = [jnp.stack(x) for x in (lhs_l, rhs_l, v_l, kw_l, wl_l)]
        gm = _dot3(lhs, rhs, bnt)
        a_vk = jnp.where(strict, gm[:, :L, :L], 0.0)
        n1 = jnp.where(strict, -gm[:, :L, L:], 0.0)
        t_inv, pw = eye + n1, n1
        for _ in range(5):
            pw = _dot3(pw, pw, bnn)
            t_inv = _dot3(t_inv, eye + pw, bnn)
        s = s_scr[...].reshape(nb * H_B, HEAD_DIM, HEAD_DIM)
        xs = _dot3(lhs, s, bnt)
        u = _dot3(t_inv, xs[:, :L] + _dot3(a_vk, vs, bnn), bnn)
        b_vk = jnp.where(incl, gm[:, L:, :L], 0.0).astype(BF16)
        b_uk = jnp.where(incl, gm[:, L:, L:], 0.0).astype(BF16)
        y = (xs[:, L:] + lax.dot_general(b_vk, vs.astype(BF16), bnn, preferred_element_type=F32)
             - lax.dot_general(b_uk, u.astype(BF16), bnn, preferred_element_type=F32))
        vu_t = jnp.stack([jnp.concatenate([vs[n], -u[n]], axis=0).T for n in range(nb * H_B)])
        s_new = s * wl + _dot3(vu_t, kws, bnn)
        s_scr[...] = s_new.reshape(nb, H_B, HEAD_DIM, HEAD_DIM)
        for b in range(nb):
            for h in range(H_B):
                y_scr[b, rows, h * HEAD_DIM:(h + 1) * HEAD_DIM] = y[b * H_B + h]
        return carry

    lax.fori_loop(0, nc, chunk, 0)
    st_ref[...] = s_scr[...]
    vec = vec_ref[...]
    for b in range(nb):
        o_ref[b] = (_segment_norm(y_scr[b], RWKV_GN_EPS) * vec[5:6] + vec[6:7] + bonus_ref[b]) * g_ref[b]


def _rwkv_chunked(prep, s0, vec, b, t, nb, tl):
    arrs = [a.reshape(b, t, 256) for a in prep]
    seq = pl.BlockSpec((nb, tl, 256), lambda i, j: (i, j, 0))
    st = pl.BlockSpec((nb, H_B, HEAD_DIM, HEAD_DIM), lambda i, j: (i, 0, 0, 0))
    return pl.pallas_call(
        functools.partial(_rwkv_chunk_body, nb=nb, tl=tl),
        grid=(b // nb, t // tl),
        in_specs=[seq] * 8 + [st, pl.BlockSpec((8, 256), lambda i, j: (0, 0))],
        out_specs=[seq, st],
        out_shape=[jax.ShapeDtypeStruct((b, t, 256), F32), jax.ShapeDtypeStruct((b, H_B, HEAD_DIM, HEAD_DIM), F32)],
        scratch_shapes=[pltpu.VMEM((nb, H_B, HEAD_DIM, HEAD_DIM), F32), pltpu.VMEM((nb, tl, 256), F32)],
        compiler_params=_cp("parallel", "arbitrary"),
        name="rwkv_chunked",
    )(*arrs, s0, vec)


def _rwkv_mixer(colsb, shift0, s0, lw, b, t, tm, nb, tl):
    prep = _rwkv_prep(colsb, shift0, lw, t, tm)
    o, st = _rwkv_chunked(prep, s0, lw['rwkv_vec'], b, t, nb, tl)
    shift = colsb.reshape(b, t, SHIFT_PAD)[:, -1, :SHIFT_B]
    return o.reshape(b * t, 256), st, shift


def _segment_norm(y, eps):
    avg = _block_ones(256, HEAD_DIM, F32) * (1.0 / HEAD_DIM)
    yc = y - jnp.dot(y, avg, precision=HI, preferred_element_type=F32)
    return yc * lax.rsqrt(jnp.dot(yc * yc, avg, precision=HI, preferred_element_type=F32) + eps)


def _rwkv_step_body(r_ref, lw_ref, k_ref, v_ref, kk_ref, ka_ref, g_ref, bonus_ref, vec_ref, s_ref, o_ref, st_ref, ft_scr, y_scr):
    h = pl.program_id(0)

    @pl.when(h == 0)
    def _():
        for n, ref in enumerate((r_ref, lw_ref, k_ref, v_ref, kk_ref, ka_ref)):
            ft_scr[n] = ref[...].T

    base = pl.multiple_of(h * HEAD_DIM, HEAD_DIM)
    head = lambda n: ft_scr[n, pl.ds(base, HEAD_DIM), :]
    r_t, w_t, k_t, kk_t, ka_t = head(0), jnp.exp(head(1)), head(2), head(4), head(5)

    def body(i, carry):
        rows = pl.ds(pl.multiple_of(i * HEAD_DIM, HEAD_DIM), HEAD_DIM)
        s = s_ref[rows, :]
        sk = jnp.sum(s * kk_t, axis=0, keepdims=True)
        s = s * w_t - sk * ka_t + ft_scr[3, pl.ds(base + i, 1), :] * k_t
        st_ref[rows, :] = s
        y_scr[pl.ds(base + i, 1), :] = jnp.sum(s * r_t, axis=0, keepdims=True)
        return carry

    lax.fori_loop(0, HEAD_DIM, body, 0)

    @pl.when(h == H_B - 1)
    def _():
        vec = vec_ref[...]
        o_ref[...] = (_segment_norm(y_scr[...].T, RWKV_GN_EPS) * vec[5:6] + vec[6:7] + bonus_ref[...]) * g_ref[...]


def _rwkv_step(colsb, shift0, s_all, layer, lw):
    b = colsb.shape[0]
    hd2 = HEAD_DIM * HEAD_DIM
    full = lambda *_: (0, 0)
    feat = pl.BlockSpec((b, 256), full)
    prep = pl.pallas_call(
        _rwkv_step_prep_body,
        grid=(1,),
        in_specs=[pl.BlockSpec((b, SHIFT_PAD), full), pl.BlockSpec((b, SHIFT_PAD), full), pl.BlockSpec((1, SHIFT_PAD), full),
                  pl.BlockSpec((8, 256), full)] + [pl.BlockSpec((LANE, 256), full)] * 3,
        out_specs=[feat] * 8,
        out_shape=[jax.ShapeDtypeStruct((b, 256), F32)] * 8,
        compiler_params=_cp("arbitrary"),
        name="rwkv_step_prep",
    )(colsb, shift0, lw['rwkv_mu'], lw['rwkv_vec'], lw['rwkv_wup'], lw['rwkv_aup'], lw['rwkv_gup'])
    s_rows = jnp.transpose(s_all, (0, 2, 3, 4, 1)).reshape(-1, b)
    o, st = pl.pallas_call(
        _rwkv_step_body,
        grid=(H_B,),
        in_specs=[feat] * 8 + [pl.BlockSpec((8, 256), full), pl.BlockSpec((hd2, b), lambda h: (layer * H_B + h, 0))],
        out_specs=[feat, pl.BlockSpec((hd2, b), lambda h: (h, 0))],
        out_shape=[jax.ShapeDtypeStruct((b, 256), F32), jax.ShapeDtypeStruct((H_B * hd2, b), F32)],
        scratch_shapes=[pltpu.VMEM((6, 256, b), F32), pltpu.VMEM((256, b), F32)],
        compiler_params=_cp("arbitrary"),
        name="rwkv_step",
    )(*prep, lw['rwkv_vec'], s_rows)
    return o, jnp.transpose(st.reshape(H_B, HEAD_DIM, HEAD_DIM, b), (3, 0, 1, 2))


def _ret_step_body(c_ref, cos_ref, sin_ref, gn_ref, r0_ref, o_ref, rt_ref, ft_scr, acc_scr):
    h = pl.program_id(0)

    @pl.when(h == 0)
    def _():
        x = c_ref[...]
        cs, sn = cos_ref[...], sin_ref[...]
        first = (lax.broadcasted_iota(jnp.int32, cs.shape, 1) % HEAD_DIM) < (HEAD_DIM // 2)

        def rope(z):
            sw = jnp.where(first, pltpu.roll(z, 256 - HEAD_DIM // 2, 1), pltpu.roll(z, HEAD_DIM // 2, 1))
            return z * cs + sw * sn

        ft_scr[0] = rope(x[:, 0:256]).T
        ft_scr[1] = (rope(x[:, 256:512]) * (HEAD_DIM ** -0.5)).T
        ft_scr[2] = x[:, 512:768].T

    base = pl.multiple_of(h * HEAD_DIM, HEAD_DIM)
    nseq = rt_ref.shape[1]
    gamma = jnp.exp(jnp.log1p(-jnp.exp2(-5.0 - jnp.full((1, nseq), h, jnp.int32).astype(F32))))
    q_t, k_t, v_t = [ft_scr[n, pl.ds(base, HEAD_DIM), :] for n in range(3)]
    qk = jnp.sum(q_t * k_t, axis=0, keepdims=True)

    def body(d, cross):
        rows = pl.ds(pl.multiple_of(d * HEAD_DIM, HEAD_DIM), HEAD_DIM)
        r_old = r0_ref[rows, :]
        rt_ref[rows, :] = gamma * r_old + ft_scr[1, pl.ds(base + d, 1), :] * v_t
        return cross + ft_scr[0, pl.ds(base + d, 1), :] * r_old

    cross = lax.fori_loop(0, HEAD_DIM, body, jnp.zeros((HEAD_DIM, nseq), F32))
    acc_scr[pl.ds(base, HEAD_DIM), :] = qk * v_t + gamma * cross

    @pl.when(h == H_D - 1)
    def _():
        o_ref[...] = jax.nn.silu(c_ref[:, 768:1024]) * (_segment_norm(acc_scr[...].T, RET_GN_EPS) * gn_ref[...])


def _ret_step(colsd, cos, sin, r_all, layer, gn):
    b = colsd.shape[0]
    hd2 = HEAD_DIM * HEAD_DIM
    full = lambda *_: (0, 0)
    r_rows = jnp.transpose(r_all, (0, 2, 3, 4, 1)).reshape(-1, b)
    o, rt = pl.pallas_call(
        _ret_step_body,
        grid=(H_D,),
        in_specs=[pl.BlockSpec((b, 1024), full), pl.BlockSpec((b, 256), full), pl.BlockSpec((b, 256), full),
                  pl.BlockSpec((1, 256), full), pl.BlockSpec((hd2, b), lambda h: (layer * H_D + h, 0))],
        out_specs=[pl.BlockSpec((b, 256), full), pl.BlockSpec((hd2, b), lambda h: (h, 0))],
        out_shape=[jax.ShapeDtypeStruct((b, 256), F32), jax.ShapeDtypeStruct((H_D * hd2, b), F32)],
        scratch_shapes=[pltpu.VMEM((3, 256, b), F32), pltpu.VMEM((256, b), F32)],
        compiler_params=_cp("arbitrary"),
        name="ret_step",
    )(colsd, cos, sin, gn.reshape(1, 256), r_rows)
    return o, jnp.transpose(rt.reshape(H_D, HEAD_DIM, HEAD_DIM, b), (3, 0, 1, 2))


def _s5_step_body(u_ref, x0_ref, a1_ref, a2_ref, bt_ref, ct_ref, d_ref, wg_ref, o_ref, xt_ref):
    u = u_ref[...]
    dot_hi = lambda a, b: jnp.dot(a, b, precision=HI, preferred_element_type=F32)
    x0 = x0_ref[...]
    n = x0.shape[0]
    even = (lax.broadcasted_iota(jnp.int32, x0.shape, 0) % 2) == 0
    partner = jnp.where(even, pltpu.roll(x0, n - 1, 0), pltpu.roll(x0, 1, 0))
    x = a1_ref[...] * x0 + a2_ref[...] * partner + dot_hi(bt_ref[...], u.T)
    xt_ref[...] = x
    y = dot_hi(ct_ref[...], x).T + d_ref[...] * u
    z = jax.nn.gelu(y)
    o_ref[...] = z * jax.nn.sigmoid(dot_hi(z, wg_ref[...]))


def _s5_step(u, x_all, layer, lw):
    b = u.shape[0]
    a1, a2, bt, ct, d_row = lw['s5_step']
    n = 2 * G_C * S5_P
    full = lambda *_: (0, 0)
    x_rows = jnp.transpose(x_all, (0, 2, 3, 4, 1)).reshape(-1, b)
    o, xt = pl.pallas_call(
        _s5_step_body,
        grid=(1,),
        in_specs=[pl.BlockSpec((b, C_C), full), pl.BlockSpec((n, b), lambda i: (layer, 0)), pl.BlockSpec((n, 1), full),
                  pl.BlockSpec((n, 1), full), pl.BlockSpec((n, C_C), full), pl.BlockSpec((C_C, n), full),
                  pl.BlockSpec((1, C_C), full), pl.BlockSpec((C_C, C_C), full)],
        out_specs=[pl.BlockSpec((b, C_C), full), pl.BlockSpec((n, b), full)],
        out_shape=[jax.ShapeDtypeStruct((b, C_C), F32), jax.ShapeDtypeStruct((n, b), F32)],
        compiler_params=_cp("arbitrary"),
        name="s5_step",
    )(u, x_rows, a1, a2, bt, ct, d_row, lw['s5_w_glu'])
    return o, jnp.transpose(xt.reshape(G_C, S5_P, 2, b), (3, 0, 1, 2))


def _s5_params(lw):
    lr, li = lw['s5_lambda_re'], lw['s5_lambda_im']
    dt = jnp.exp(lw['s5_log_step'])[:, None]
    mag = jnp.exp(lr * dt)
    ar, ai = mag * jnp.cos(li * dt), mag * jnp.sin(li * dt)
    nr, ni = ar - 1.0, ai
    den = lr * lr + li * li
    fr, fi = (nr * lr + ni * li) / den, (ni * lr - nr * li) / den
    b_re, b_im = lw['s5_b'][0], lw['s5_b'][1]
    bbr = fr[..., None] * b_re - fi[..., None] * b_im
    bbi = fr[..., None] * b_im + fi[..., None] * b_re
    eye = jnp.eye(G_C, dtype=F32)
    bd_in = lambda m: jnp.einsum('gpc,gh->gchp', m, eye).reshape(G_C * S5_CH, G_C * S5_P)
    bd_out = lambda m: jnp.einsum('gcp,gh->gphc', m, eye).reshape(G_C * S5_P, G_C * S5_CH)
    b_big = jnp.concatenate([bd_in(bbr), bd_in(bbi)], axis=1)
    c_big = jnp.concatenate([bd_out(lw['s5_c'][0]), -bd_out(lw['s5_c'][1])], axis=0)
    a_row = jnp.concatenate([ar.reshape(1, -1), ai.reshape(1, -1)], axis=1)
    d_row = lw['s5_d'].reshape(1, C_C)
    n = 2 * G_C * S5_P
    a1 = jnp.stack([ar, ar], axis=-1).reshape(n, 1)
    a2 = jnp.stack([-ai, ai], axis=-1).reshape(n, 1)
    bt = jnp.stack([jnp.einsum('gpc,gh->gphc', bbr, eye), jnp.einsum('gpc,gh->gphc', bbi, eye)], axis=2).reshape(n, G_C * S5_CH)
    ct = jnp.stack([jnp.einsum('gcp,gh->gchp', lw['s5_c'][0], eye), -jnp.einsum('gcp,gh->gchp', lw['s5_c'][1], eye)],
                   axis=-1).reshape(G_C * S5_CH, n)
    return (a_row, b_big, c_big, d_row), (a1, a2, bt, ct, d_row)


def _s5_body(u_ref, x0_ref, a_ref, b_ref, c_ref, d_ref, wg_ref, o_ref, xt_ref, x_scr, bu_scr, xs_scr, *, nb, tt, mm_dtype, prec):
    sub = 8

    @pl.when(pl.program_id(0) == 0)
    def _():
        x_scr[...] = jnp.zeros_like(x_scr)
        x_scr[0:nb, :] = x0_ref[...]
        bu_scr[...] = jnp.zeros_like(bu_scr)

    np_ = G_C * S5_P
    ncb = np_ // LANE
    for b in range(nb):
        bu = jnp.dot(u_ref[b].astype(mm_dtype), b_ref[...], precision=prec, preferred_element_type=F32)
        for cb in range(2 * ncb):
            bu_scr[cb, pl.ds(b, tt, stride=sub), :] = bu[:, cb * LANE:(cb + 1) * LANE]
    a = a_ref[...]

    def step(t, x):
        rows = pl.ds(pl.multiple_of(t * sub, sub), sub)
        new = [None] * (2 * ncb)
        for cb in range(ncb):
            re, im = slice(cb * LANE, (cb + 1) * LANE), slice(np_ + cb * LANE, np_ + (cb + 1) * LANE)
            ar, ai, xr, xi = a[:, re], a[:, im], x[:, re], x[:, im]
            new[cb] = ar * xr - ai * xi + bu_scr[cb, rows, :]
            new[ncb + cb] = ar * xi + ai * xr + bu_scr[ncb + cb, rows, :]
            xs_scr[cb, rows, :] = new[cb]
            xs_scr[ncb + cb, rows, :] = new[ncb + cb]
        return jnp.concatenate(new, axis=1)

    x_last = lax.fori_loop(0, tt, step, x_scr[...], unroll=8)
    x_scr[...] = x_last
    xt_ref[...] = x_last[0:nb, :]
    for b in range(nb):
        u = u_ref[b]
        xs = jnp.concatenate([xs_scr[cb, pl.ds(b, tt, stride=sub), :] for cb in range(2 * ncb)], axis=1)
        y = jnp.dot(xs.astype(mm_dtype), c_ref[...], precision=prec, preferred_element_type=F32) + d_ref[...] * u
        z = jax.nn.gelu(y)
        o_ref[b] = z * jax.nn.sigmoid(jnp.dot(z.astype(mm_dtype), wg_ref[...], precision=prec, preferred_element_type=F32))


def _s5_mixer(u, x0, lw, b, t, tt, exact):
    a_row, b_big, c_big, d_row = lw['s5']
    mm_dtype = F32 if exact else BF16
    prec = HI if exact else None
    x0l = jnp.concatenate([x0[..., 0].reshape(b, -1), x0[..., 1].reshape(b, -1)], axis=1)
    np2 = 2 * G_C * S5_P
    fixed = lambda i: (0, 0)
    o, xt = pl.pallas_call(
        functools.partial(_s5_body, nb=b, tt=tt, mm_dtype=mm_dtype, prec=prec),
        grid=(t // tt,),
        in_specs=[pl.BlockSpec((b, tt, C_C), lambda i: (0, i, 0)), pl.BlockSpec((b, np2), fixed),
                  pl.BlockSpec((1, np2), fixed), pl.BlockSpec((C_C, np2), fixed), pl.BlockSpec((np2, C_C), fixed),
                  pl.BlockSpec((1, C_C), fixed), pl.BlockSpec((C_C, C_C), fixed)],
        out_specs=[pl.BlockSpec((b, tt, C_C), lambda i: (0, i, 0)), pl.BlockSpec((b, np2), fixed)],
        out_shape=[jax.ShapeDtypeStruct((b, t, C_C), F32), jax.ShapeDtypeStruct((b, np2), F32)],
        scratch_shapes=[pltpu.VMEM((8, np2), F32), pltpu.VMEM((np2 // LANE, 8 * tt, LANE), F32),
                        pltpu.VMEM((np2 // LANE, 8 * tt, LANE), F32)],
        compiler_params=_cp("arbitrary"),
        name="s5",
    )(u.reshape(b, t, C_C), x0l, a_row, b_big.astype(mm_dtype), c_big.astype(mm_dtype), d_row, lw['s5_w_glu'].astype(mm_dtype))
    xt = xt.reshape(b, 2, G_C, S5_P)
    return o.reshape(b * t, C_C), jnp.stack([xt[:, 0], xt[:, 1]], axis=-1)


def _ret_tables(pos, c):
    cos, sin = _rope_tables(pos, HEAD_DIM, RET_THETA, HEAD_DIM, H_D)
    log_g = jnp.log1p(-jnp.exp2(-5.0 - jnp.arange(H_D, dtype=F32)))
    i = jnp.arange(c, dtype=F32)
    diff = i[:, None] - i[None, :]
    dmat = jnp.where(diff >= 0, jnp.exp(jnp.maximum(diff, 0.0)[None] * log_g[:, None, None]), 0.0).reshape(H_D * c, c)
    q_dec = jnp.repeat(jnp.exp((i + 1.0)[None] * log_g[:, None]).T, HEAD_DIM, axis=1)
    k_dec = jnp.repeat(jnp.exp((c - 1.0 - i)[None] * log_g[:, None]).T, HEAD_DIM, axis=1)
    chunk_dec = jnp.repeat(jnp.exp(c * log_g), HEAD_DIM).reshape(256, 1)
    return cos, sin, dmat, q_dec, k_dec, chunk_dec


def _ret_body(c_ref, cos_ref, sin_ref, dmat_ref, qdec_ref, kdec_ref, cdec_ref, r0_ref, gn_ref, o_ref, rt_ref, r_scr, *, c):
    @pl.when(pl.program_id(1) == 0)
    def _():
        r_scr[...] = r0_ref[0]

    x = c_ref[0]
    q, k, v, g = x[:, 0:256], x[:, 256:512], x[:, 512:768], x[:, 768:1024]
    cs, sn = cos_ref[...], sin_ref[...]
    lane = lax.broadcasted_iota(jnp.int32, (c, 256), 1)
    first = (lane % HEAD_DIM) < (HEAD_DIM // 2)

    def rope(z):
        sw = jnp.where(first, pltpu.roll(z, 256 - HEAD_DIM // 2, 1), pltpu.roll(z, HEAD_DIM // 2, 1))
        return z * cs + sw * sn

    q = rope(q)
    k = rope(k) * (HEAD_DIM ** -0.5)
    head = lane // HEAD_DIM
    kb, vb = k.astype(BF16), v.astype(BF16)
    qstack = jnp.concatenate([jnp.where(head == h, q, 0.0) for h in range(H_D)], axis=0).astype(BF16)
    s = lax.dot_general(qstack, kb, (((1,), (1,)), ((), ())), preferred_element_type=F32) * dmat_ref[...]
    pv = jnp.dot(s.astype(BF16), vb, preferred_element_type=F32)
    inner = jnp.zeros((c, 256), F32)
    for h in range(H_D):
        inner = inner + jnp.where(head == h, pv[h * c:(h + 1) * c], 0.0)
    r_old = r_scr[...]
    cross = jnp.dot((q * qdec_ref[...]).astype(BF16), r_old.astype(BF16), preferred_element_type=F32)
    kv = lax.dot_general((k * kdec_ref[...]).astype(BF16), vb, (((0,), (0,)), ((), ())), preferred_element_type=F32)
    bd = _block_ones(256, HEAD_DIM, F32)
    r_new = cdec_ref[...] * r_old + kv * bd
    r_scr[...] = r_new
    rt_ref[0] = r_new
    o = inner + cross
    avg = bd * (1.0 / HEAD_DIM)
    oc = o - jnp.dot(o, avg, precision=HI, preferred_element_type=F32)
    on = oc * lax.rsqrt(jnp.dot(oc * oc, avg, precision=HI, preferred_element_type=F32) + RET_GN_EPS)
    o_ref[0] = jax.nn.silu(g) * (on * gn_ref[...])


def _ret_mixer(colsd, r0, lw, tabs, b, t):
    c = RET_CHUNK if t % RET_CHUNK == 0 else t
    cos, sin, dmat, q_dec, k_dec, chunk_dec = tabs
    eye = jnp.eye(H_D, dtype=F32)
    r0l = jnp.einsum('bhde,hg->bhdge', r0, eye).reshape(b, 256, 256)
    n_t = t // c
    fixed = lambda i, j: (0, 0)
    o, rt = pl.pallas_call(
        functools.partial(_ret_body, c=c),
        grid=(b, n_t),
        in_specs=[pl.BlockSpec((1, c, 1024), lambda i, j: (i, j, 0)),
                  pl.BlockSpec((c, 256), lambda i, j: (j, 0)), pl.BlockSpec((c, 256), lambda i, j: (j, 0)),
                  pl.BlockSpec((H_D * c, c), fixed), pl.BlockSpec((c, 256), fixed), pl.BlockSpec((c, 256), fixed),
                  pl.BlockSpec((256, 1), fixed), pl.BlockSpec((1, 256, 256), lambda i, j: (i, 0, 0)),
                  pl.BlockSpec((1, 256), fixed)],
        out_specs=[pl.BlockSpec((1, c, 256), lambda i, j: (i, j, 0)), pl.BlockSpec((1, 256, 256), lambda i, j: (i, 0, 0))],
        out_shape=[jax.ShapeDtypeStruct((b, t, 256), F32), jax.ShapeDtypeStruct((b, 256, 256), F32)],
        scratch_shapes=[pltpu.VMEM((256, 256), F32)],
        compiler_params=_cp("parallel", "arbitrary"),
        name="retention",
    )(colsd.reshape(b, t, 1024), cos, sin, dmat, q_dec, k_dec, chunk_dec, r0l, lw['ret_gn'].reshape(1, 256))
    rt = jnp.einsum('bhdge,hg->bhde', rt.reshape(b, H_D, HEAD_DIM, H_D, HEAD_DIM), eye)
    return o.reshape(b * t, 256), rt


def _partner(x, d, period):
    pos = lax.broadcasted_iota(jnp.int32, x.shape, 1) % period
    return jnp.where(pos + d < period, pltpu.roll(x, LANE - d, 1), pltpu.roll(x, period - d, 1))


def _out_body(x_ref, oa_ref, ob_ref, oc_ref, od_ref, w_ref, nw_ref, wr_ref, br_ref, x1_ref, h_ref, comb_ref):
    acc = x_ref[...]
    for i, ref in enumerate((oa_ref, ob_ref, oc_ref, od_ref)):
        acc = acc + jnp.dot(ref[...].astype(BF16), w_ref[256 * i:256 * (i + 1), :], preferred_element_type=F32)
    x1_ref[...] = acc
    h = acc * lax.rsqrt(jnp.mean(acc * acc, axis=-1, keepdims=True) + RMS_EPS) * nw_ref[...]
    hb = h.astype(BF16)
    h_ref[...] = hb
    h_lo = (h - hb.astype(F32)).astype(BF16)
    logits = (jnp.dot(hb, wr_ref[0], preferred_element_type=F32) + jnp.dot(hb, wr_ref[1], preferred_element_type=F32)
              + jnp.dot(h_lo, wr_ref[0], preferred_element_type=F32) + br_ref[...])
    le, lg = logits[:, :LANE], logits[:, LANE:]
    lane = lax.broadcasted_iota(jnp.int32, le.shape, 1)
    mg = jnp.max(lg, axis=-1, keepdims=True)
    eg = jnp.exp(lg - mg)
    pg = eg / (jnp.sum(eg, axis=-1, keepdims=True) * (1.0 / 32.0))
    gidx = (lane % N_EXPERTS) // EXP_PER_GROUP
    g_rank = jnp.zeros_like(pg)
    for d in range(1, N_GROUPS):
        other = pltpu.roll(pg, LANE - EXP_PER_GROUP * d, 1)
        wrapped = gidx + d >= N_GROUPS
        g_rank = g_rank + jnp.where((other > pg) | ((other == pg) & wrapped), 1.0, 0.0)
    kidx = lane % EXP_PER_GROUP
    others = [_partner(le, d, EXP_PER_GROUP) for d in range(1, EXP_PER_GROUP)]
    me = functools.reduce(jnp.maximum, others, le)
    ee = jnp.exp(le - me)
    se = ee
    for d in range(1, EXP_PER_GROUP):
        se = se + _partner(ee, d, EXP_PER_GROUP)
    pe = ee / se
    e_rank = jnp.zeros_like(pe)
    for d in range(1, EXP_PER_GROUP):
        other = _partner(pe, d, EXP_PER_GROUP)
        wrapped = kidx + d >= EXP_PER_GROUP
        e_rank = e_rank + jnp.where((other > pe) | ((other == pe) & wrapped), 1.0, 0.0)
    top = jnp.where(e_rank < 2.0, pe, 0.0)
    den = top
    for d in range(1, EXP_PER_GROUP):
        den = den + _partner(top, d, EXP_PER_GROUP)
    comb = jnp.where((g_rank < 1.0) & (lane < N_EXPERTS), pg * (top / den), 0.0)
    comb_ref[...] = comb


def _out_router(x, oa, ob, oc, od, lw, tm):
    n = x.shape[0]
    row = lambda i: (i, 0)
    fixed = lambda i: (0, 0)
    mix = pl.BlockSpec((tm, 256), row)
    return pl.pallas_call(
        _out_body,
        grid=(n // tm,),
        in_specs=[pl.BlockSpec((tm, D_MODEL), row), mix, mix, mix, mix,
                  pl.BlockSpec((D_MODEL, D_MODEL), fixed), pl.BlockSpec((1, D_MODEL), fixed),
                  pl.BlockSpec((2, D_MODEL, 2 * LANE), lambda i: (0, 0, 0)), pl.BlockSpec((1, 2 * LANE), fixed)],
        out_specs=[pl.BlockSpec((tm, D_MODEL), row), pl.BlockSpec((tm, D_MODEL), row), pl.BlockSpec((tm, LANE), row)],
        out_shape=[jax.ShapeDtypeStruct((n, D_MODEL), F32), jax.ShapeDtypeStruct((n, D_MODEL), BF16),
                   jax.ShapeDtypeStruct((n, LANE), F32)],
        compiler_params=_cp("parallel"),
        name="out_router",
    )(x, oa, ob, oc, od, lw['w_out'], lw['norm_ffn'], lw['w_router'], lw['b_router'])


def _router_weights(w_grp, b_grp, w_exp, b_exp):
    we = jnp.transpose(w_exp, (1, 0, 2)).reshape(D_MODEL, N_EXPERTS)
    wg = jnp.repeat(w_grp, EXP_PER_GROUP, axis=1)
    reps = LANE // N_EXPERTS
    w = jnp.concatenate([jnp.tile(we, (1, reps)), jnp.tile(wg, (1, reps))], axis=1)
    b = jnp.concatenate([jnp.tile(b_exp.reshape(1, N_EXPERTS), (1, reps)),
                         jnp.tile(jnp.repeat(b_grp, EXP_PER_GROUP).reshape(1, N_EXPERTS), (1, reps))], axis=1)
    w_hi = w.astype(BF16)
    return jnp.stack([w_hi, (w - w_hi.astype(F32)).astype(BF16)]), b


def _moe_body(h_ref, comb_ref, x1_ref, wg_ref, wu_ref, wd_ref, nf_ref, *out_refs, final):
    acc_ref = out_refs[-1]
    e = pl.program_id(1)

    @pl.when(e == 0)
    def _():
        acc_ref[...] = x1_ref[...]

    h = h_ref[...]
    comb = comb_ref[...]
    lane = lax.broadcasted_iota(jnp.int32, comb.shape, 1)
    c = jnp.sum(jnp.where(lane == e, comb, 0.0), axis=-1, keepdims=True)
    hg = jnp.dot(h, wg_ref[0], preferred_element_type=F32)
    hu = jnp.dot(h, wu_ref[0], preferred_element_type=F32)
    act = (jax.nn.silu(hg) * hu * c).astype(BF16)
    acc_ref[...] += jnp.dot(act, wd_ref[0], preferred_element_type=F32)

    @pl.when(e == N_EXPERTS - 1)
    def _():
        x2 = acc_ref[...]
        if final:
            out_refs[0][...] = x2 * lax.rsqrt(jnp.mean(x2 * x2, axis=-1, keepdims=True) + RMS_EPS) * nf_ref[...]
        else:
            out_refs[0][...] = x2


def _moe(h, comb, x1, lw, norm_final, tm, final):
    n = x1.shape[0]
    row = lambda i, e: (i, 0)
    per_e = lambda i, e: (e, 0, 0)
    return pl.pallas_call(
        functools.partial(_moe_body, final=final),
        grid=(n // tm, N_EXPERTS),
        in_specs=[pl.BlockSpec((tm, D_MODEL), row), pl.BlockSpec((tm, LANE), row), pl.BlockSpec((tm, D_MODEL), row),
                  pl.BlockSpec((1, D_MODEL, D_EXPERT), per_e), pl.BlockSpec((1, D_MODEL, D_EXPERT), per_e),
                  pl.BlockSpec((1, D_EXPERT, D_MODEL), per_e), pl.BlockSpec((1, D_MODEL), lambda i, e: (0, 0))],
        out_specs=pl.BlockSpec((tm, D_MODEL), row),
        out_shape=jax.ShapeDtypeStruct((n, D_MODEL), F32),
        scratch_shapes=[pltpu.VMEM((tm, D_MODEL), F32)],
        compiler_params=_cp("parallel", "arbitrary"),
        name="moe",
    )(h, comb, x1, lw['moe_wg'], lw['moe_wu'], lw['moe_wd'], norm_final.reshape(1, D_MODEL))


def _prep_layer(l, p):
    w_in = p['w_in'][l]
    o = _offsets(SPLIT_SIZES)
    segs = jnp.split(w_in, o, axis=1)
    w_all = jnp.concatenate([segs[0], segs[1], _pad_to(segs[2], LANE, 1), _pad_to(segs[3], SHIFT_PAD, 1), segs[4], segs[5]],
                            axis=1).astype(BF16)
    w_prec = w_in[:, :PRECISE_COLS]
    w_lo = (w_prec - w_prec.astype(BF16).astype(F32)).astype(BF16)
    lw = {'layer': l, 'w_all': w_all, 'w_lo': w_lo, 'norm_mix': p['norm_mix'][l]}
    lw['cmp'] = _cmp_weights(p['nsa_cmp_w1'][l], p['nsa_cmp_b1'][l], p['nsa_cmp_w2'][l])
    lw['rwkv_mu'] = _pad_to(p['rwkv_mu'][l].reshape(1, SHIFT_B), SHIFT_PAD, 1)
    lw['rwkv_vec'] = _pad_to(p['rwkv_vec'][l], 8, 0)
    z = lambda a, b: jnp.zeros((a, b), F32)
    lw['rwkv_wup'] = jnp.concatenate([p['rwkv_w_up'][l], z(LANE - LORA_W, C_B)], axis=0)
    lw['rwkv_aup'] = jnp.concatenate([z(LORA_W, C_B), p['rwkv_a_up'][l], z(LANE - LORA_W - LORA_A, C_B)], axis=0)
    lw['rwkv_gup'] = jnp.concatenate([z(LORA_W + LORA_A, C_B), p['rwkv_g_up'][l], z(LANE - LORA_W - LORA_A - LORA_G, C_B)], axis=0)
    for name in ('s5_lambda_re', 's5_lambda_im', 's5_b', 's5_c', 's5_d', 's5_log_step', 's5_w_glu', 'ret_gn'):
        lw[name] = p[name][l]
    lw['s5'], lw['s5_step'] = _s5_params(lw)
    lw['w_out'] = p['w_out'][l].astype(BF16)
    lw['norm_ffn'] = p['norm_ffn'][l].reshape(1, D_MODEL)
    lw['w_router'], lw['b_router'] = _router_weights(p['moe_w_grp'][l], p['moe_b_grp'][l], p['moe_w_exp'][l], p['moe_b_exp'][l])
    lw['moe_wg'] = p['moe_w_gate'][l].astype(BF16)
    lw['moe_wu'] = p['moe_w_up'][l].astype(BF16)
    lw['moe_wd'] = p['moe_w_down'][l].astype(BF16)
    return lw


ROW_TILE = 512
MOE_ROW_TILE = 1024
SCAN_TILE = 256


def _prompt_layer(x, lw, tabs, b, t, norm_final):
    cos_a, sin_a, ret_tabs = tabs
    q, kv, gate, colsb, u, colsd = _proj(x, lw['norm_mix'], lw['w_all'], lw['w_lo'], cos_a, sin_a, ROW_TILE)
    o_a = _nsa_prompt_mixer(q, kv, gate, lw, b, t)
    o_b, s_rwkv, s_shift = _rwkv_mixer(colsb, jnp.zeros((b, SHIFT_PAD), F32), jnp.zeros((b, H_B, HEAD_DIM, HEAD_DIM), F32),
                                       lw, b, t, ROW_TILE, b, SCAN_TILE)
    o_c, s_s5 = _s5_mixer(u, jnp.zeros((b, G_C, S5_P, 2), F32), lw, b, t, SCAN_TILE, False)
    o_d, s_ret = _ret_mixer(colsd, jnp.zeros((b, H_D, HEAD_DIM, HEAD_DIM), F32), lw, ret_tabs, b, t)
    x1, h, comb = _out_router(x, o_a, o_b, o_c, o_d, lw, ROW_TILE)
    x2 = _moe(h, comb, x1, lw, norm_final, MOE_ROW_TILE, lw['layer'] == DEPTH - 1)
    kv3 = kv.reshape(b, t, 6 * LANE)
    rows = kv3[:, :, :4 * LANE].reshape(b, t, 4, N_KV_A, HEAD_DIM)
    win = kv3[:, t - min(WINDOW, t):, 4 * LANE:].reshape(b, min(WINDOW, t), 2, N_KV_A, HEAD_DIM)
    return x2, (rows, win, s_rwkv, s_shift, s_s5, s_ret)


def _sample_layer(x, lw, tabs, b, pos, cache_kv, page_table, win_buf, s_rwkv, s_shift, s_s5, s_ret, norm_final):
    cos_a, sin_a, ret_cs = tabs
    q, kv, gate, colsb, u, colsd = _proj(x, lw['norm_mix'], lw['w_all'], lw['w_lo'], cos_a, sin_a, b)
    o_a, win = _nsa_sample(q, kv, gate, cache_kv, win_buf, lw['layer'], page_table, lw['cmp'], int(pos[0]))
    rows = kv[:, :4 * LANE].reshape(b, 1, 4, N_KV_A, HEAD_DIM)
    o_b, s_rwkv = _rwkv_step(colsb, _pad_to(s_shift, SHIFT_PAD, 1), s_rwkv, lw['layer'], lw)
    s_shift = colsb[:, :SHIFT_B]
    o_c, s_s5 = _s5_step(u, s_s5, lw['layer'], lw)
    o_d, s_ret = _ret_step(colsd, ret_cs[0], ret_cs[1], s_ret, lw['layer'], lw['ret_gn'])
    x1, h, comb = _out_router(x, o_a, o_b, o_c, o_d, lw, b)
    x2 = _moe(h, comb, x1, lw, norm_final, b, lw['layer'] == DEPTH - 1)
    return x2, (rows, win, s_rwkv, s_shift, s_s5, s_ret)


def kernel(x_prompt, x_sample, cache_nsa_kv, cache_nsa_win, state_rwkv, state_rwkv_shift, state_s5, state_ret, page_table, norm_mix, w_in, nsa_cmp_w1, nsa_cmp_b1, nsa_cmp_w2, rwkv_mu, rwkv_vec, rwkv_w_up, rwkv_a_up, rwkv_g_up, s5_lambda_re, s5_lambda_im, s5_b, s5_c, s5_d, s5_log_step, s5_w_glu, ret_gn, w_out, norm_ffn, moe_w_grp, moe_b_grp, moe_w_exp, moe_b_exp, moe_w_gate, moe_w_up, moe_w_down, norm_final):
    p = dict(norm_mix=norm_mix, w_in=w_in, nsa_cmp_w1=nsa_cmp_w1, nsa_cmp_b1=nsa_cmp_b1, nsa_cmp_w2=nsa_cmp_w2,
             rwkv_mu=rwkv_mu, rwkv_vec=rwkv_vec, rwkv_w_up=rwkv_w_up, rwkv_a_up=rwkv_a_up, rwkv_g_up=rwkv_g_up,
             s5_lambda_re=s5_lambda_re, s5_lambda_im=s5_lambda_im, s5_b=s5_b, s5_c=s5_c, s5_d=s5_d,
             s5_log_step=s5_log_step, s5_w_glu=s5_w_glu, ret_gn=ret_gn, w_out=w_out, norm_ffn=norm_ffn,
             moe_w_grp=moe_w_grp, moe_b_grp=moe_b_grp, moe_w_exp=moe_w_exp, moe_b_exp=moe_b_exp,
             moe_w_gate=moe_w_gate, moe_w_up=moe_w_up, moe_w_down=moe_w_down)
    bp, tp = x_prompt.shape[:2]
    bs, ts = x_sample.shape[:2]
    assert ts == 1 and tp % ROW_TILE == 0 and (bp * tp) % MOE_ROW_TILE == 0 and tp % (4 * LANE) == 0 and bs % 8 == 0
    past_len = page_table.shape[1] * cache_nsa_kv.shape[2]
    pos_p = np.arange(tp)
    pos_s = past_len + np.arange(ts)
    c = RET_CHUNK if tp % RET_CHUNK == 0 else tp
    tabs_p = _rope_tables(pos_p, ROT_DIM, ROPE_THETA, HEAD_DIM, 2) + (_ret_tables(pos_p, c),)
    pos_rows = np.repeat(pos_s, bs)
    tabs_s = _rope_tables(pos_rows, ROT_DIM, ROPE_THETA, HEAD_DIM, 2) + (_rope_tables(pos_rows, HEAD_DIM, RET_THETA, HEAD_DIM, H_D),)
    xp = x_prompt.reshape(bp * tp, D_MODEL)
    xs = x_sample.reshape(bs * ts, D_MODEL)
    sts_p, sts_s = [], []
    for l in range(DEPTH):
        lw = _prep_layer(l, p)
        xp, st_p = _prompt_layer(xp, lw, tabs_p, bp, tp, norm_final)
        xs, st_s = _sample_layer(xs, lw, tabs_s, bs, pos_s, cache_nsa_kv, page_table, cache_nsa_win, state_rwkv,
                                     state_rwkv_shift[l], state_s5, state_ret, norm_final)
        rows, win, s1, s2, s3, s4 = st_s
        sts_s.append((rows, win, s1, s2, s3, s4))
        sts_p.append(st_p)
    new_p = [jnp.stack([st[i] for st in sts_p]) for i in range(6)]
    new_s = [jnp.stack([st[i] for st in sts_s]) for i in range(6)]
    return (xp.reshape(bp, tp, D_MODEL), xs.reshape(bs, ts, D_MODEL), new_p[0], new_s[0], new_p[1], new_s[1],
            new_p[2], new_s[2], new_p[3], new_s[3], new_p[4], new_s[4], new_p[5], new_s[5])
```

```python
import functools

import numpy as np
import jax
import jax.numpy as jnp
from jax import lax
from jax.experimental import pallas as pl
from jax.experimental.pallas import tpu as pltpu

F32 = jnp.float32
BF16 = jnp.bfloat16
HI = lax.Precision.HIGHEST

D_MODEL = 1024
DEPTH = 2
HEAD_DIM = 64
C_A = C_B = C_C = C_D = 256
H_A = 4
N_KV_A = 2
R_A = 2
ROT_DIM = 16
ROPE_THETA = 500000.0
CMP_BLOCK = 32
CMP_STRIDE = 16
CMP_HIDDEN = 128
SEL_BLOCK = 64
TOP_K = 16
WINDOW = 512
NEG_INF = -1e30
FORCED_SCORE = 1e9
BLOCKED_SCORE = -1e9
H_B = 4
LORA_W = 16
LORA_A = 16
LORA_G = 32
SHIFT_B = 832
SHIFT_PAD = 896
RWKV_GN_EPS = 64e-5
S5_CH = 16
G_C = 16
S5_P = 64
H_D = 4
RET_CHUNK = 128
RET_THETA = 10000.0
RET_GN_EPS = 1e-5
N_GROUPS = 4
EXP_PER_GROUP = 4
N_EXPERTS = 16
D_EXPERT = 256
RMS_EPS = 1e-6
SPLIT_SIZES = (C_A, 6 * N_KV_A * HEAD_DIM, 3 * H_A, SHIFT_B, C_C, 4 * C_D)
LANE = 128
VMEM_LIMIT = 56 * 1024 * 1024


def _cp(*sem):
    return pltpu.CompilerParams(dimension_semantics=sem, vmem_limit_bytes=VMEM_LIMIT)


def _offsets(sizes):
    return [int(s) for s in np.cumsum(sizes)[:-1]]


def _pad_to(a, n, axis):
    pad = [(0, 0)] * a.ndim
    pad[axis] = (0, n - a.shape[axis])
    return jnp.pad(a, pad)


def _block_ones(n, blk, dtype):
    r = lax.broadcasted_iota(jnp.int32, (n, n), 0) // blk
    c = lax.broadcasted_iota(jnp.int32, (n, n), 1) // blk
    return (r == c).astype(dtype)


def _rope_tables(pos, rot_dim, theta, period, reps):
    half = rot_dim // 2
    inv = theta ** (-jnp.arange(half, dtype=F32) / half)
    ang = jnp.asarray(pos, F32)[:, None] * inv[None, :]
    cos, sin = jnp.cos(ang), jnp.sin(ang)
    n = ang.shape[0]
    rest = period - rot_dim
    c = jnp.concatenate([cos, cos, jnp.ones((n, rest), F32)], -1)
    s = jnp.concatenate([-sin, sin, jnp.zeros((n, rest), F32)], -1)
    return jnp.tile(c, (1, reps)), jnp.tile(s, (1, reps))


PRECISE_COLS = 384


def _proj_body(x_ref, nw_ref, w_ref, wlo_ref, cos_ref, sin_ref, q_ref, kv_ref, g_ref, cb_ref, u_ref, cd_ref):
    x = x_ref[...]
    h = x * lax.rsqrt(jnp.mean(x * x, axis=-1, keepdims=True) + RMS_EPS) * nw_ref[...]
    hb = h.astype(BF16)
    h_lo = (h - hb.astype(F32)).astype(BF16)
    c = cos_ref[...]
    s = sin_ref[...]
    first = (lax.broadcasted_iota(jnp.int32, c.shape, 1) % HEAD_DIM) < (ROT_DIM // 2)

    def rope(z):
        sw = jnp.where(first, pltpu.roll(z, LANE - ROT_DIM // 2, 1), pltpu.roll(z, ROT_DIM // 2, 1))
        return z * c + sw * s

    def dot(a, b):
        z = jnp.dot(hb, w_ref[:, a:b], preferred_element_type=F32)
        if b <= PRECISE_COLS:
            z = z + (jnp.dot(hb, wlo_ref[:, a:b], preferred_element_type=F32)
                     + jnp.dot(h_lo, w_ref[:, a:b], preferred_element_type=F32))
        return z

    for j in range(2):
        q_ref[:, LANE * j:LANE * (j + 1)] = rope(dot(LANE * j, LANE * (j + 1)))
    for j in range(6):
        z = dot(256 + LANE * j, 256 + LANE * (j + 1))
        kv_ref[:, LANE * j:LANE * (j + 1)] = rope(z) if j % 2 == 0 else z
    g_ref[...] = dot(1024, 1152)
    cb_ref[...] = dot(1152, 2048)
    u_ref[...] = dot(2048, 2304)
    cd_ref[...] = dot(2304, 3328)


def _proj(x2d, norm_w, w_all, w_lo, cos_t, sin_t, tm):
    n = x2d.shape[0]
    t_tiles = cos_t.shape[0] // tm
    row = lambda i: (i, 0)
    fixed = lambda i: (0, 0)
    tab = lambda i: (i % t_tiles, 0)
    widths = (256, 768, 128, SHIFT_PAD, 256, 1024)
    return pl.pallas_call(
        _proj_body,
        grid=(n // tm,),
        in_specs=[pl.BlockSpec((tm, D_MODEL), row), pl.BlockSpec((1, D_MODEL), fixed),
                  pl.BlockSpec((D_MODEL, 3328), fixed), pl.BlockSpec((D_MODEL, PRECISE_COLS), fixed),
                  pl.BlockSpec((tm, LANE), tab), pl.BlockSpec((tm, LANE), tab)],
        out_specs=[pl.BlockSpec((tm, w), row) for w in widths],
        out_shape=[jax.ShapeDtypeStruct((n, w), F32) for w in widths],
        compiler_params=_cp("parallel"),
        name="proj",
    )(x2d, norm_w.reshape(1, D_MODEL), w_all, w_lo, cos_t, sin_t)


def _cmp_mlp(xc, kind, w1_ref, b1_ref, w2_ref, w1k_ref, w2k_ref, nck):
    if kind == 0:
        hh = _dot3_presplit(xc, w1k_ref)
    else:
        hh = jnp.dot(xc.astype(BF16), w1_ref[kind], preferred_element_type=F32)
    hs = []
    for g in range(N_KV_A):
        hg = hh[g * nck:(g + 1) * nck]
        hs.append(jax.nn.gelu(b1_ref[kind] + hg[:, :CMP_HIDDEN] + pltpu.roll(hg[:, CMP_HIDDEN:], nck - 1, 0)))
    act = jnp.concatenate(hs, axis=1)
    if kind == 0:
        return _dot3_presplit(act, w2k_ref)
    return jnp.dot(act.astype(BF16), w2_ref[kind], preferred_element_type=F32)


def _cmp_body(xk_ref, xv_ref, w1_ref, b1_ref, w2_ref, w1k_ref, w2k_ref, kc_ref, vc_ref, vct_ref, xc_ref, *, n_chunks):
    lane = lax.broadcasted_iota(jnp.int32, (n_chunks, LANE), 1)
    lo = lane < HEAD_DIM
    for pair in range(CMP_STRIDE // 2):
        for kind, x_ref in enumerate((xk_ref, xv_ref)):
            ak = x_ref[0, pl.ds(2 * pair, n_chunks, stride=CMP_STRIDE), :]
            bk = x_ref[0, pl.ds(2 * pair + 1, n_chunks, stride=CMP_STRIDE), :]
            xc_ref[kind, 0:n_chunks, LANE * pair:LANE * (pair + 1)] = jnp.where(lo, ak, pltpu.roll(bk, HEAD_DIM, 1))
            xc_ref[kind, n_chunks:2 * n_chunks, LANE * pair:LANE * (pair + 1)] = jnp.where(lo, pltpu.roll(ak, HEAD_DIM, 1), bk)
    outs = [_cmp_mlp(xc_ref[kind], kind, w1_ref, b1_ref, w2_ref, w1k_ref, w2k_ref, n_chunks) for kind in range(2)]
    kc_ref[0] = outs[0]
    vc_ref[0] = outs[1]
    vct_ref[0] = outs[1].T


def _nsa_compress(rows, w1cat, b1, w2bd, w1k, w2k):
    b, tk = rows.shape[0], rows.shape[1]
    n_chunks = tk // CMP_STRIDE
    fixed3 = lambda i: (0, 0, 0)
    fixed2 = lambda i: (0, 0)
    return pl.pallas_call(
        functools.partial(_cmp_body, n_chunks=n_chunks),
        grid=(b,),
        in_specs=[pl.BlockSpec((1, tk, LANE), lambda i: (i, 0, 0)), pl.BlockSpec((1, tk, LANE), lambda i: (i, 0, 1)),
                  pl.BlockSpec((2, CMP_STRIDE * HEAD_DIM, 2 * CMP_HIDDEN), fixed3),
                  pl.BlockSpec((2, 1, CMP_HIDDEN), fixed3),
                  pl.BlockSpec((2, 2 * CMP_HIDDEN, LANE), fixed3),
                  pl.BlockSpec((2, CMP_STRIDE * HEAD_DIM, 2 * CMP_HIDDEN), fixed3), pl.BlockSpec((2, 2 * CMP_HIDDEN, LANE), fixed3)],
        out_specs=[pl.BlockSpec((1, n_chunks, LANE), lambda i: (i, 0, 0)),
                   pl.BlockSpec((1, n_chunks, LANE), lambda i: (i, 0, 0)),
                   pl.BlockSpec((1, LANE, n_chunks), lambda i: (i, 0, 0))],
        out_shape=[jax.ShapeDtypeStruct((b, n_chunks, LANE), F32), jax.ShapeDtypeStruct((b, n_chunks, LANE), F32),
                   jax.ShapeDtypeStruct((b, LANE, n_chunks), F32)],
        scratch_shapes=[pltpu.VMEM((2, N_KV_A * n_chunks, CMP_STRIDE * HEAD_DIM), F32)],
        compiler_params=_cp("parallel"),
        name="nsa_compress",
    )(rows, rows, w1cat, b1, w2bd, w1k, w2k)


def _cmp_weights(cmp_w1, cmp_b1, cmp_w2):
    m = CMP_BLOCK // CMP_STRIDE
    w1r = cmp_w1.reshape(2, m, CMP_STRIDE * HEAD_DIM, CMP_HIDDEN)
    w1cat = jnp.concatenate([w1r[:, j] for j in range(m)], axis=-1)
    z = jnp.zeros_like(cmp_w2)
    w2bd = jnp.concatenate([jnp.concatenate([cmp_w2, z], -1), jnp.concatenate([z, cmp_w2], -1)], axis=1)
    split = lambda w: jnp.stack([w.astype(BF16), (w - w.astype(BF16).astype(F32)).astype(BF16)])
    return w1cat.astype(BF16), cmp_b1.reshape(2, 1, CMP_HIDDEN), w2bd.astype(BF16), split(w1cat[0]), split(w2bd[0])


def _cmp_to_sel_t(n_chunks, n_cmp, n_sel):
    starts = np.arange(n_chunks) * CMP_STRIDE
    sel_s = np.arange(n_sel) * SEL_BLOCK
    ov = np.minimum(starts[:, None] + CMP_BLOCK, sel_s[None] + SEL_BLOCK) - np.maximum(starts[:, None], sel_s[None])
    ov = np.clip(ov, 0, None) / CMP_BLOCK
    ov[n_cmp:] = 0.0
    return jnp.asarray(ov.T, dtype=F32)


def _masked_softmax_cols(s, mask):
    m = jnp.max(jnp.where(mask, s, NEG_INF), axis=0, keepdims=True)
    e = jnp.where(mask, jnp.exp(s - m), 0.0)
    den = jnp.sum(e, axis=0, keepdims=True)
    return e * jnp.where(den > 0.0, 1.0 / den, 0.0)


def _nsa_prompt_body(qt_ref, gt_ref, kc_ref, vct_ref, ovt_ref, ks_ref, vst_ref, kw_ref, vwt_ref, o_ref, sel_ref,
                     *, n_cmp, n_sel, qb_size):
    qb = pl.program_id(1)
    tq = qb_size
    n_chunks = kc_ref.shape[1]
    qpos = qb * tq + lax.broadcasted_iota(jnp.int32, (1, tq), 1)
    qpos2 = jnp.concatenate([qpos, qpos], axis=1)
    zeros_q = jnp.zeros((HEAD_DIM, 2 * tq), F32)
    gates = jax.nn.sigmoid(gt_ref[0])
    kc = kc_ref[0]
    n_idx = lax.broadcasted_iota(jnp.int32, (n_chunks, 2 * tq), 0)
    cmp_mask = (n_idx * CMP_STRIDE + (CMP_BLOCK - 1) <= qpos2) & (n_idx < n_cmp)
    blk = lax.broadcasted_iota(jnp.int32, (n_sel, tq), 0)
    cur = qpos // SEL_BLOCK
    forced = (blk == 0) | (blk == cur) | (blk == cur - 1)
    causal_blk = blk * SEL_BLOCK <= qpos
    qpads, o_cmps = [], []

    for g in range(N_KV_A):
        q64 = jnp.concatenate([qt_ref[0, (2 * g) * HEAD_DIM:(2 * g + 1) * HEAD_DIM, :],
                               qt_ref[0, (2 * g + 1) * HEAD_DIM:(2 * g + 2) * HEAD_DIM, :]], axis=1) * (HEAD_DIM ** -0.5)
        qpad32 = jnp.concatenate([q64, zeros_q], axis=0) if g == 0 else jnp.concatenate([zeros_q, q64], axis=0)
        qpad = qpad32.astype(BF16)
        qpads.append(qpad)

        p = _masked_softmax_cols(_dot3(kc, qpad32), cmp_mask)
        o_cmps.append(jnp.dot(vct_ref[0, g * HEAD_DIM:(g + 1) * HEAD_DIM, :].astype(BF16), p.astype(BF16),
                              preferred_element_type=F32))
        psum = p[:, :tq] + p[:, tq:]
        imp = jnp.dot(ovt_ref[...], psum, precision=HI, preferred_element_type=F32)
        imp = jnp.where(forced, FORCED_SCORE, jnp.where(causal_blk, imp, BLOCKED_SCORE))
        rank = jnp.zeros((n_sel, tq), F32)
        for i in range(n_sel):
            row = imp[i:i + 1, :]
            rank = rank + jnp.where((row > imp) | ((row == imp) & (blk > i)), 1.0, 0.0)
        sel_ref[g] = jnp.where(rank < float(min(TOP_K, n_sel)), 1.0, 0.0)

    q_all = jnp.concatenate(qpads, axis=1)
    qpos4 = jnp.concatenate([qpos2, qpos2], axis=1)

    def attend(j, carry, k_ref, vt_ref, use_sel, tk, causal=True):
        m, l, acc = carry
        off = pl.multiple_of(j * tk, tk)
        kt = k_ref[0, pl.ds(off, tk), :].astype(BF16)
        s = jnp.dot(kt, q_all, preferred_element_type=F32)
        if causal:
            diff = qpos4 - (off + lax.broadcasted_iota(jnp.int32, (tk, 4 * tq), 0))
        if use_sel:
            per_tile = tk // SEL_BLOCK
            sels = []
            for g in range(N_KV_A):
                rows = [jnp.broadcast_to(sel_ref[g, pl.ds(j * per_tile + a, 1), :], (SEL_BLOCK, tq)) for a in range(per_tile)]
                selm = jnp.concatenate(rows, axis=0)
                sels += [selm, selm]
            mask = jnp.concatenate(sels, axis=1) > 0.0
            if causal:
                mask = mask & (diff >= 0)
        else:
            mask = (diff >= 0) & (diff < WINDOW)
        m_new = jnp.maximum(m, jnp.max(jnp.where(mask, s, NEG_INF), axis=0, keepdims=True))
        alpha = jnp.exp(m - m_new)
        e = jnp.where(mask, jnp.exp(s - m_new), 0.0)
        l_new = alpha * l + jnp.sum(e, axis=0, keepdims=True)
        vt = vt_ref[0, :, pl.ds(off, tk)].astype(BF16)
        return m_new, l_new, alpha * acc + jnp.dot(vt, e.astype(BF16), preferred_element_type=F32)

    init = (jnp.full((1, 4 * tq), NEG_INF, F32), jnp.zeros((1, 4 * tq), F32), jnp.zeros((2 * HEAD_DIM, 4 * tq), F32))
    tk_s, tk_w = 4 * tq, 2 * tq
    last_s = (qb * tq) // tk_s
    slc = functools.partial(attend, k_ref=ks_ref, vt_ref=vst_ref, use_sel=True, tk=tk_s)
    _, l_s, acc_s = slc(last_s, lax.fori_loop(0, last_s, functools.partial(slc, causal=False), init))
    _, l_w, acc_w = lax.fori_loop(jnp.maximum(qb * tq - WINDOW, 0) // tk_w, (qb * tq + tk_w) // tk_w,
                                  functools.partial(attend, k_ref=kw_ref, vt_ref=vwt_ref, use_sel=False, tk=tk_w), init)

    for g in range(N_KV_A):
        blk_g = (slice(g * HEAD_DIM, (g + 1) * HEAD_DIM), slice(g * 2 * tq, (g + 1) * 2 * tq))
        o_slc = acc_s[blk_g] / l_s[:, blk_g[1]]
        o_win = acc_w[blk_g] / l_w[:, blk_g[1]]
        for r in range(R_A):
            h = 2 * g + r
            gr = gates[3 * h:3 * h + 3, :]
            sl = slice(r * tq, (r + 1) * tq)
            o_ref[0, h * HEAD_DIM:(h + 1) * HEAD_DIM, :] = (gr[0:1] * o_cmps[g][:, sl] + gr[1:2] * o_slc[:, sl]
                                                          + gr[2:3] * o_win[:, sl])


def _nsa_prompt(qt, gt, kc, vct, ovt, kv, vst, vwt, n_cmp):
    b, _, t = qt.shape
    tq = 128
    n_sel = t // SEL_BLOCK
    n_chunks = kc.shape[1]
    per_b = lambda i, j: (i, 0, 0)
    return pl.pallas_call(
        functools.partial(_nsa_prompt_body, n_cmp=n_cmp, n_sel=n_sel, qb_size=tq),
        grid=(b, t // tq),
        in_specs=[pl.BlockSpec((1, 256, tq), lambda i, j: (i, 0, j)),
                  pl.BlockSpec((1, 16, tq), lambda i, j: (i, 0, j)),
                  pl.BlockSpec((1, n_chunks, LANE), per_b),
                  pl.BlockSpec((1, LANE, n_chunks), per_b),
                  pl.BlockSpec((n_sel, n_chunks), lambda i, j: (0, 0)),
                  pl.BlockSpec((1, t, LANE), lambda i, j: (i, 0, 2)),
                  pl.BlockSpec((1, LANE, t), per_b),
                  pl.BlockSpec((1, t, LANE), lambda i, j: (i, 0, 4)),
                  pl.BlockSpec((1, LANE, t), per_b)],
        out_specs=pl.BlockSpec((1, 256, tq), lambda i, j: (i, 0, j)),
        out_shape=jax.ShapeDtypeStruct((b, 256, t), F32),
        scratch_shapes=[pltpu.VMEM((N_KV_A, n_sel, tq), F32)],
        compiler_params=_cp("parallel", "arbitrary"),
        name="nsa_prompt",
    )(qt, gt, kc, vct, ovt, kv, vst, kv, vwt)


def _nsa_prompt_mixer(q, kv, gate, lw, b, t):
    kv3 = kv.reshape(b, t, 6 * LANE)
    n_chunks = t // CMP_STRIDE
    n_cmp = (t - CMP_BLOCK) // CMP_STRIDE + 1
    kc, _, vct = _nsa_compress(kv3, *lw['cmp'])
    ovt = _cmp_to_sel_t(n_chunks, n_cmp, t // SEL_BLOCK)
    qt = jnp.swapaxes(q.reshape(b, t, 256), 1, 2)
    gt = jnp.swapaxes(gate.reshape(b, t, LANE)[:, :, :16], 1, 2)
    vst = jnp.swapaxes(kv3[:, :, 3 * LANE:4 * LANE], 1, 2)
    vwt = jnp.swapaxes(kv3[:, :, 5 * LANE:], 1, 2)
    ot = _nsa_prompt(qt, gt, kc, vct, ovt, kv3, vst, vwt, n_cmp)
    return jnp.swapaxes(ot, 1, 2).reshape(b * t, 256)


def _softmax_rows_with_extra(s, mask, s_new):
    m = jnp.maximum(jnp.max(jnp.where(mask, s, NEG_INF), axis=-1, keepdims=True), s_new)
    e = jnp.where(mask, jnp.exp(s - m), 0.0)
    e_new = jnp.exp(s_new - m)
    return e, e_new, 1.0 / (jnp.sum(e, axis=-1, keepdims=True) + e_new)


def _nsa_sample_body(pt_ref, *refs, n_pages, page, pos, n_sel, n_cmp, wb):
    del pt_ref
    n_in = n_pages
    pages = refs[:n_in]
    (qbd_ref, new_ref, gate_ref, win_ref, ov_ref, w1_ref, b1_ref, w2_ref, w1k_ref, w2k_ref,
     o_ref, nw_ref, tok_ref, xc_ref) = refs[n_in:]
    pg = lambda p, kind: pages[p].at[0, kind:kind + 1]
    nck = n_pages * page // CMP_STRIDE
    lane8 = lax.broadcasted_iota(jnp.int32, (8, LANE), 1)
    lo8 = lane8 < HEAD_DIM
    nt = lambda a, b: lax.dot_general(a, b, (((1,), (1,)), ((), ())), preferred_element_type=F32)

    lo_c = lax.broadcasted_iota(jnp.int32, (nck, LANE), 1) < HEAD_DIM
    for kind in range(2):
        for p in range(n_pages):
            tok_ref[kind, p * page:(p + 1) * page, :] = pg(p, kind)[0].T
        for pair in range(CMP_STRIDE // 2):
            a = tok_ref[kind, pl.ds(2 * pair, nck, stride=CMP_STRIDE), :]
            b = tok_ref[kind, pl.ds(2 * pair + 1, nck, stride=CMP_STRIDE), :]
            cols = slice(LANE * pair, LANE * (pair + 1))
            xc_ref[kind, 0:nck, cols] = jnp.where(lo_c, a, pltpu.roll(b, HEAD_DIM, 1))
            xc_ref[kind, nck:2 * nck, cols] = jnp.where(lo_c, pltpu.roll(a, HEAD_DIM, 1), b)
    kc, vc = [_cmp_mlp(xc_ref[kind], kind, w1_ref, b1_ref, w2_ref, w1k_ref, w2k_ref, nck) for kind in range(2)]

    q = qbd_ref[0] * (HEAD_DIM ** -0.5)
    qb = q.astype(BF16)
    new = new_ref[0]

    n_idx = lax.broadcasted_iota(jnp.int32, (8, nck), 1)
    cmask = (n_idx * CMP_STRIDE + (CMP_BLOCK - 1) <= pos) & (n_idx < n_cmp)
    s = _dot3(q, kc, _NT)
    m = jnp.max(jnp.where(cmask, s, NEG_INF), axis=-1, keepdims=True)
    e = jnp.where(cmask, jnp.exp(s - m), 0.0)
    den = jnp.sum(e, axis=-1, keepdims=True)
    p_cmp = e * jnp.where(den > 0.0, 1.0 / den, 0.0)
    o_cmp = jnp.dot(p_cmp.astype(BF16), vc.astype(BF16), preferred_element_type=F32)

    row8 = lax.broadcasted_iota(jnp.int32, (8, nck), 0)
    psum = jnp.where(row8 == 0, p_cmp[0:1] + p_cmp[1:2], jnp.where(row8 == 1, p_cmp[2:3] + p_cmp[3:4], 0.0))
    imp = jnp.dot(psum, ov_ref[...], precision=HI, preferred_element_type=F32)
    cur = pos // SEL_BLOCK
    forced = (lane8 == 0) | (lane8 == cur) | (lane8 == cur - 1)
    imp = jnp.where(forced, FORCED_SCORE, jnp.where(lane8 * SEL_BLOCK <= pos, imp, BLOCKED_SCORE))
    imp = jnp.where(lane8 < n_sel, imp, -3e38)
    rank = jnp.zeros((8, LANE), F32)
    for i in range(n_sel):
        col = imp[:, i:i + 1]
        rank = rank + jnp.where((col > imp) | ((col == imp) & (lane8 > i)), 1.0, 0.0)
    sel = jnp.where((rank < float(min(TOP_K, n_sel))) & (lane8 < n_sel), 1.0, 0.0)
    rsel = lax.broadcasted_iota(jnp.int32, (8, LANE), 0)
    selh = jnp.where(rsel < R_A, sel[0:1], jnp.where(rsel < 2 * R_A, sel[1:2], 0.0))

    per_page = page // SEL_BLOCK
    s_t, m_t = [], []
    for p in range(n_pages):
        s_t.append(jnp.dot(qb, pg(p, 2)[0].astype(BF16), preferred_element_type=F32))
        blk_sel = selh[:, per_page * p:per_page * p + 1]
        for a in range(1, per_page):
            blk_sel = jnp.where(lane8 < a * SEL_BLOCK, blk_sel, selh[:, per_page * p + a:per_page * p + a + 1])
        kpos = p * page + lane8
        m_t.append((blk_sel > 0.0) & (kpos <= pos))
    s_all = jnp.concatenate(s_t, axis=1)
    mk_all = jnp.concatenate(m_t, axis=1)
    s_new = jnp.sum(q * new[2:3], axis=-1, keepdims=True)
    e, e_new, inv = _softmax_rows_with_extra(s_all, mk_all, s_new)
    acc = e_new * new[3:4]
    for p in range(n_pages):
        acc = acc + nt(e[:, p * page:(p + 1) * page].astype(BF16), pg(p, 3)[0].astype(BF16))
    o_slc = acc * inv

    kw, vw = win_ref[0], win_ref[1]
    widx = lax.broadcasted_iota(jnp.int32, (8, wb), 1)
    diff = wb - widx
    s_w = jnp.dot(qb, kw.astype(BF16), preferred_element_type=F32)
    s_wnew = jnp.sum(q * new[4:5], axis=-1, keepdims=True)
    e, e_new, inv = _softmax_rows_with_extra(s_w, (diff >= 0) & (diff < WINDOW), s_wnew)
    o_win = (nt(e.astype(BF16), vw.astype(BF16)) + e_new * new[5:6]) * inv

    gts = jax.nn.sigmoid(gate_ref[0])
    o = gts[:, 0:1] * o_cmp + gts[:, 1:2] * o_slc + gts[:, 2:3] * o_win
    lo1 = lo8[0:1]
    o_ref[0] = jnp.concatenate([jnp.where(lo1, o[0:1], pltpu.roll(o[1:2], HEAD_DIM, 1)),
                                jnp.where(lo1, pltpu.roll(o[2:3], HEAD_DIM, 1), o[3:4])], axis=1)
    last = lax.broadcasted_iota(jnp.int32, (LANE, wb), 1) == wb - 1
    new_t = new.T
    nw_ref[0] = jnp.where(last, new_t[:, 4:5], pltpu.roll(kw, wb - 1, 1))
    nw_ref[1] = jnp.where(last, new_t[:, 5:6], pltpu.roll(vw, wb - 1, 1))


def _nsa_sample(q, kv, gate, cache_kv, cache_win, layer, page_table, cmp_w, pos):
    b = q.shape[0]
    n_pool, page = cache_kv.shape[1:3]
    n_pages = page_table.shape[1]
    wb = cache_win.shape[2]
    assert wb == WINDOW and page % CMP_STRIDE == 0 and page % SEL_BLOCK == 0 and page == LANE
    tk = n_pages * page + 1
    n_cmp = (tk - CMP_BLOCK) // CMP_STRIDE + 1
    nck = n_pages * page // CMP_STRIDE
    assert n_cmp <= nck
    n_sel = -(-tk // SEL_BLOCK)
    assert n_sel <= LANE and pos // SEL_BLOCK == n_sel - 1
    ov = _pad_to(_cmp_to_sel_t(nck, n_cmp, n_sel).T, LANE, 1)
    w1cat, b1, w2bd, w1k, w2k = cmp_w
    q4 = q.reshape(b, H_A, HEAD_DIM)
    z = jnp.zeros_like(q4)
    first = (jnp.arange(H_A) // R_A == 0)[None, :, None]
    qbd = jnp.concatenate([jnp.where(first, q4, z), jnp.where(first, z, q4)], axis=-1)
    qbd = _pad_to(qbd, 8, 1)
    new = _pad_to(kv.reshape(b, 6, LANE), 8, 1)
    g8 = _pad_to(_pad_to(gate[:, :3 * H_A].reshape(b, H_A, 3), LANE, 2), 8, 1)
    cache3 = jnp.transpose(cache_kv, (0, 1, 3, 4, 5, 2)).reshape(-1, 4, LANE, page)
    win3 = jnp.transpose(cache_win, (0, 1, 3, 4, 5, 2)).reshape(-1, LANE, wb)
    page_specs = [pl.BlockSpec((1, 4, LANE, page), functools.partial(lambda i, pt, p: (layer * n_pool + pt[i, p], 0, 0, 0), p=p))
                  for p in range(n_pages)]
    per_b = lambda i, pt: (i, 0, 0)
    fixed2 = lambda i, pt: (0, 0)
    fixed3 = lambda i, pt: (0, 0, 0)
    grid_spec = pltpu.PrefetchScalarGridSpec(
        num_scalar_prefetch=1,
        grid=(b,),
        in_specs=page_specs + [pl.BlockSpec((1, 8, LANE), per_b), pl.BlockSpec((1, 8, LANE), per_b), pl.BlockSpec((1, 8, LANE), per_b),
                               pl.BlockSpec((2, LANE, wb), lambda i, pt: (layer * b + i, 0, 0)), pl.BlockSpec((nck, LANE), fixed2),
                               pl.BlockSpec((2, CMP_STRIDE * HEAD_DIM, 2 * CMP_HIDDEN), fixed3),
                               pl.BlockSpec((2, 1, CMP_HIDDEN), fixed3), pl.BlockSpec((2, 2 * CMP_HIDDEN, LANE), fixed3),
                               pl.BlockSpec((2, CMP_STRIDE * HEAD_DIM, 2 * CMP_HIDDEN), fixed3),
                               pl.BlockSpec((2, 2 * CMP_HIDDEN, LANE), fixed3)],
        out_specs=[pl.BlockSpec((1, 1, 256), per_b), pl.BlockSpec((2, LANE, wb), per_b)],
        scratch_shapes=[pltpu.VMEM((2, n_pages * page, LANE), F32), pltpu.VMEM((2, 2 * nck, CMP_STRIDE * HEAD_DIM), F32)],
    )
    o, nw = pl.pallas_call(
        functools.partial(_nsa_sample_body, n_pages=n_pages, page=page, pos=pos, n_sel=n_sel, n_cmp=n_cmp, wb=wb),
        grid_spec=grid_spec,
        out_shape=[jax.ShapeDtypeStruct((b, 1, 256), F32), jax.ShapeDtypeStruct((2 * b, LANE, wb), F32)],
        compiler_params=_cp("parallel"),
        name="nsa_sample",
    )(page_table, *([cache3] * n_pages), qbd, new, g8, win3, ov, w1cat, b1, w2bd, w1k, w2k)
    return o.reshape(b, 256), jnp.transpose(nw.reshape(b, 2, N_KV_A, HEAD_DIM, wb), (0, 4, 1, 2, 3))


def _rwkv_prep_body(c_ref, s0_ref, mu_ref, vec_ref, wup_ref, aup_ref, gup_ref,
                    r_ref, lw_ref, k_ref, v_ref, kk_ref, ka_ref, g_ref, bonus_ref, carry_ref, *, tiles_per_seq):
    i = pl.program_id(0)
    cols = c_ref[...]
    tm = cols.shape[0]

    @pl.when(i % tiles_per_seq == 0)
    def _():
        carry_ref[...] = s0_ref[0]

    prev = pltpu.roll(cols, 1, 0)
    row0 = lax.broadcasted_iota(jnp.int32, cols.shape, 0) == 0
    prev = jnp.where(row0, carry_ref[...], prev)
    carry_ref[...] = cols[tm - 1:tm, :]
    _rwkv_features(cols, prev, mu_ref, vec_ref, wup_ref, aup_ref, gup_ref,
                   r_ref, lw_ref, k_ref, v_ref, kk_ref, ka_ref, g_ref, bonus_ref)


def _rwkv_step_prep_body(c_ref, s0_ref, mu_ref, vec_ref, wup_ref, aup_ref, gup_ref,
                         r_ref, lw_ref, k_ref, v_ref, kk_ref, ka_ref, g_ref, bonus_ref):
    _rwkv_features(c_ref[...], s0_ref[...], mu_ref, vec_ref, wup_ref, aup_ref, gup_ref,
                   r_ref, lw_ref, k_ref, v_ref, kk_ref, ka_ref, g_ref, bonus_ref)


def _rwkv_features(cols, prev, mu_ref, vec_ref, wup_ref, aup_ref, gup_ref,
                   r_ref, lw_ref, k_ref, v_ref, kk_ref, ka_ref, g_ref, bonus_ref):
    xs = cols + mu_ref[...] * (prev - cols)
    r, k, v, lo = xs[:, 0:256], xs[:, 256:512], xs[:, 512:768], xs[:, 768:896]
    vec = vec_ref[...]
    w0, a0, k_k, k_a, r_k = vec[0:1], vec[1:2], vec[2:3], vec[3:4], vec[4:5]
    w_log = -jax.nn.softplus(-(w0 + _dot3(jnp.tanh(lo), wup_ref[...]))) - 0.5
    a = jax.nn.sigmoid(a0 + _dot3(lo, aup_ref[...]))
    g_ref[...] = _dot3(jax.nn.sigmoid(lo), gup_ref[...])
    ones = _block_ones(256, HEAD_DIM, BF16)
    kk = k * k_k
    kk = kk * lax.rsqrt(_dot_bf16_exact_rhs(kk * kk, ones) + 1e-12)
    k2 = k * (1.0 + (a - 1.0) * k_a)
    r_ref[...] = r
    lw_ref[...] = -jnp.exp(w_log)
    k_ref[...] = k2
    v_ref[...] = v
    kk_ref[...] = kk
    ka_ref[...] = kk * a
    bonus_ref[...] = _dot_bf16_exact_rhs(r * k2 * r_k, ones) * v


def _rwkv_prep(colsb, shift0, lw, t, tm):
    n = colsb.shape[0]
    tiles_per_seq = t // tm
    row = lambda i: (i, 0)
    fixed = lambda i: (0, 0)
    outs = [jax.ShapeDtypeStruct((n, 256), F32)] * 8
    return pl.pallas_call(
        functools.partial(_rwkv_prep_body, tiles_per_seq=tiles_per_seq),
        grid=(n // tm,),
        in_specs=[pl.BlockSpec((tm, SHIFT_PAD), row),
                  pl.BlockSpec((1, 1, SHIFT_PAD), lambda i: (i // tiles_per_seq, 0, 0)),
                  pl.BlockSpec((1, SHIFT_PAD), fixed), pl.BlockSpec((8, 256), fixed),
                  pl.BlockSpec((LANE, 256), fixed), pl.BlockSpec((LANE, 256), fixed), pl.BlockSpec((LANE, 256), fixed)],
        out_specs=[pl.BlockSpec((tm, 256), row)] * 8,
        out_shape=outs,
        scratch_shapes=[pltpu.VMEM((1, SHIFT_PAD), F32)],
        compiler_params=_cp("arbitrary"),
        name="rwkv_prep",
    )(colsb, shift0.reshape(-1, 1, SHIFT_PAD), lw['rwkv_mu'], lw['rwkv_vec'], lw['rwkv_wup'], lw['rwkv_aup'], lw['rwkv_gup'])


RWKV_CHUNK = 64


def _split_bf16(x):
    hi = x.astype(BF16)
    return hi, (x - hi.astype(F32)).astype(BF16)


def _dot3(a, b, dims=(((1,), (0,)), ((), ()))):
    ah, al = _split_bf16(a)
    bh, bl = _split_bf16(b)
    dg = lambda x, y: lax.dot_general(x, y, dims, preferred_element_type=F32)
    return dg(ah, bh) + dg(ah, bl) + dg(al, bh)


def _dot3_presplit(x, w_ref):
    xh, xl = _split_bf16(x)
    return (jnp.dot(xh, w_ref[0], preferred_element_type=F32) + jnp.dot(xh, w_ref[1], preferred_element_type=F32)
            + jnp.dot(xl, w_ref[0], preferred_element_type=F32))


def _dot_bf16_exact_rhs(x, m_bf16):
    hi, lo = _split_bf16(x)
    return jnp.dot(hi, m_bf16, preferred_element_type=F32) + jnp.dot(lo, m_bf16, preferred_element_type=F32)


_NT = (((1,), (1,)), ((), ()))


def _rwkv_chunk_body(r_ref, lw_ref, k_ref, v_ref, kk_ref, ka_ref, g_ref, bonus_ref, s0_ref, vec_ref,
                     o_ref, st_ref, s_scr, y_scr, *, nb, tl):
    L = RWKV_CHUNK
    nc = tl // L

    @pl.when(pl.program_id(1) == 0)
    def _():
        s_scr[...] = s0_ref[...]

    ri = lax.broadcasted_iota(jnp.int32, (L, L), 0)
    ci = lax.broadcasted_iota(jnp.int32, (L, L), 1)
    strict, incl = ri > ci, ri >= ci
    ltri = incl.astype(F32)
    eye = (ri == ci).astype(F32)

    bnn = (((2,), (1,)), ((0,), (0,)))
    bnt = (((2,), (2,)), ((0,), (0,)))

    def chunk(c, carry):
        rows = pl.ds(pl.multiple_of(c * L, L), L)
        lhs_l, rhs_l, v_l, kw_l, wl_l = [], [], [], [], []
        for b in range(nb):
            r, lw, k, v, kk, ka = [ref[b, rows, :] for ref in (r_ref, lw_ref, k_ref, v_ref, kk_ref, ka_ref)]
            cl = jnp.dot(ltri, lw, precision=HI, preferred_element_type=F32)
            e_neg = jnp.exp(-cl)
            e_rem = jnp.exp(cl[L - 1:L, :] - cl)
            kkd, rd = kk * jnp.exp(cl - lw), r * jnp.exp(cl)
            kinv, kainv, kw, kaw = k * e_neg, ka * e_neg, k * e_rem, ka * e_rem
            w_last = jnp.exp(cl[L - 1:L, :])
            for h in range(H_B):
                sl = slice(h * HEAD_DIM, (h + 1) * HEAD_DIM)
                lhs_l.append(jnp.concatenate([kkd[:, sl], rd[:, sl]], axis=0))
                rhs_l.append(jnp.concatenate([kinv[:, sl], kainv[:, sl]], axis=0))
                v_l.append(v[:, sl])
                kw_l.append(jnp.concatenate([kw[:, sl], kaw[:, sl]], axis=0))
                wl_l.append(w_last[:, sl])
        lhs, rhs, vs, kws, wl = [jnp.stack(x) for x in (lhs_l, rhs_l, v_l, kw_l, wl_l)]
        gm = _dot3(lhs, rhs, bnt)
        a_vk = jnp.where(strict, gm[:, :L, :L], 0.0)
        n1 = jnp.where(strict, -gm[:, :L, L:], 0.0)
        t_inv, pw = eye + n1, n1
        for _ in range(5):
            pw = _dot3(pw, pw, bnn)
            t_inv = _dot3(t_inv, eye + pw, bnn)
        s = s_scr[...].reshape(nb * H_B, HEAD_DIM, HEAD_DIM)
        xs = _dot3(lhs, s, bnt)
        u = _dot3(t_inv, xs[:, :L] + _dot3(a_vk, vs, bnn), bnn)
        b_vk = jnp.where(incl, gm[:, L:, :L], 0.0).astype(BF16)
        b_uk = jnp.where(incl, gm[:, L:, L:], 0.0).astype(BF16)
        y = (xs[:, L:] + lax.dot_general(b_vk, vs.astype(BF16), bnn, preferred_element_type=F32)
             - lax.dot_general(b_uk, u.astype(BF16), bnn, preferred_element_type=F32))
        vu_t = jnp.stack([jnp.concatenate([vs[n], -u[n]], axis=0).T for n in range(nb * H_B)])
        s_new = s * wl + _dot3(vu_t, kws, bnn)
        s_scr[...] = s_new.reshape(nb, H_B, HEAD_DIM, HEAD_DIM)
        for b in range(nb):
            for h in range(H_B):
                y_scr[b, rows, h * HEAD_DIM:(h + 1) * HEAD_DIM] = y[b * H_B + h]
        return carry

    lax.fori_loop(0, nc, chunk, 0)
    st_ref[...] = s_scr[...]
    vec = vec_ref[...]
    for b in range(nb):
        o_ref[b] = (_segment_norm(y_scr[b], RWKV_GN_EPS) * vec[5:6] + vec[6:7] + bonus_ref[b]) * g_ref[b]


def _rwkv_chunked(prep, s0, vec, b, t, nb, tl):
    arrs = [a.reshape(b, t, 256) for a in prep]
    seq = pl.BlockSpec((nb, tl, 256), lambda i, j: (i, j, 0))
    st = pl.BlockSpec((nb, H_B, HEAD_DIM, HEAD_DIM), lambda i, j: (i, 0, 0, 0))
    return pl.pallas_call(
        functools.partial(_rwkv_chunk_body, nb=nb, tl=tl),
        grid=(b // nb, t // tl),
        in_specs=[seq] * 8 + [st, pl.BlockSpec((8, 256), lambda i, j: (0, 0))],
        out_specs=[seq, st],
        out_shape=[jax.ShapeDtypeStruct((b, t, 256), F32), jax.ShapeDtypeStruct((b, H_B, HEAD_DIM, HEAD_DIM), F32)],
        scratch_shapes=[pltpu.VMEM((nb, H_B, HEAD_DIM, HEAD_DIM), F32), pltpu.VMEM((nb, tl, 256), F32)],
        compiler_params=_cp("parallel", "arbitrary"),
        name="rwkv_chunked",
    )(*arrs, s0, vec)


def _rwkv_mixer(colsb, shift0, s0, lw, b, t, tm, nb, tl):
    prep = _rwkv_prep(colsb, shift0, lw, t, tm)
    o, st = _rwkv_chunked(prep, s0, lw['rwkv_vec'], b, t, nb, tl)
    shift = colsb.reshape(b, t, SHIFT_PAD)[:, -1, :SHIFT_B]
    return o.reshape(b * t, 256), st, shift


def _segment_norm(y, eps):
    avg = _block_ones(256, HEAD_DIM, BF16) * (1.0 / HEAD_DIM)
    yc = y - _dot_bf16_exact_rhs(y, avg)
    return yc * lax.rsqrt(_dot_bf16_exact_rhs(yc * yc, avg) + eps)


def _rwkv_step_body(r_ref, lw_ref, k_ref, v_ref, kk_ref, ka_ref, g_ref, bonus_ref, vec_ref, s_ref, o_ref, st_ref, ft_scr, y_scr):
    h = pl.program_id(0)

    @pl.when(h == 0)
    def _():
        for n, ref in enumerate((r_ref, lw_ref, k_ref, v_ref, kk_ref, ka_ref)):
            ft_scr[n] = ref[...].T

    base = pl.multiple_of(h * HEAD_DIM, HEAD_DIM)
    head = lambda n: ft_scr[n, pl.ds(base, HEAD_DIM), :]
    r_t, w_t, k_t, kk_t, ka_t = head(0), jnp.exp(head(1)), head(2), head(4), head(5)

    def body(i, carry):
        rows = pl.ds(pl.multiple_of(i * HEAD_DIM, HEAD_DIM), HEAD_DIM)
        s = s_ref[rows, :]
        sk = jnp.sum(s * kk_t, axis=0, keepdims=True)
        s = s * w_t - sk * ka_t + ft_scr[3, pl.ds(base + i, 1), :] * k_t
        st_ref[rows, :] = s
        y_scr[pl.ds(base + i, 1), :] = jnp.sum(s * r_t, axis=0, keepdims=True)
        return carry

    lax.fori_loop(0, HEAD_DIM, body, 0)

    @pl.when(h == H_B - 1)
    def _():
        vec = vec_ref[...]
        o_ref[...] = (_segment_norm(y_scr[...].T, RWKV_GN_EPS) * vec[5:6] + vec[6:7] + bonus_ref[...]) * g_ref[...]


def _rwkv_step(colsb, shift0, s_all, layer, lw):
    b = colsb.shape[0]
    hd2 = HEAD_DIM * HEAD_DIM
    full = lambda *_: (0, 0)
    feat = pl.BlockSpec((b, 256), full)
    prep = pl.pallas_call(
        _rwkv_step_prep_body,
        grid=(1,),
        in_specs=[pl.BlockSpec((b, SHIFT_PAD), full), pl.BlockSpec((b, SHIFT_PAD), full), pl.BlockSpec((1, SHIFT_PAD), full),
                  pl.BlockSpec((8, 256), full)] + [pl.BlockSpec((LANE, 256), full)] * 3,
        out_specs=[feat] * 8,
        out_shape=[jax.ShapeDtypeStruct((b, 256), F32)] * 8,
        compiler_params=_cp("arbitrary"),
        name="rwkv_step_prep",
    )(colsb, shift0, lw['rwkv_mu'], lw['rwkv_vec'], lw['rwkv_wup'], lw['rwkv_aup'], lw['rwkv_gup'])
    s_rows = jnp.transpose(s_all, (0, 2, 3, 4, 1)).reshape(-1, b)
    o, st = pl.pallas_call(
        _rwkv_step_body,
        grid=(H_B,),
        in_specs=[feat] * 8 + [pl.BlockSpec((8, 256), full), pl.BlockSpec((hd2, b), lambda h: (layer * H_B + h, 0))],
        out_specs=[feat, pl.BlockSpec((hd2, b), lambda h: (h, 0))],
        out_shape=[jax.ShapeDtypeStruct((b, 256), F32), jax.ShapeDtypeStruct((H_B * hd2, b), F32)],
        scratch_shapes=[pltpu.VMEM((6, 256, b), F32), pltpu.VMEM((256, b), F32)],
        compiler_params=_cp("arbitrary"),
        name="rwkv_step",
    )(*prep, lw['rwkv_vec'], s_rows)
    return o, jnp.transpose(st.reshape(H_B, HEAD_DIM, HEAD_DIM, b), (3, 0, 1, 2))


def _ret_step_body(c_ref, cos_ref, sin_ref, gn_ref, r0_ref, o_ref, rt_ref, ft_scr, acc_scr):
    h = pl.program_id(0)

    @pl.when(h == 0)
    def _():
        x = c_ref[...]
        cs, sn = cos_ref[...], sin_ref[...]
        first = (lax.broadcasted_iota(jnp.int32, cs.shape, 1) % HEAD_DIM) < (HEAD_DIM // 2)

        def rope(z):
            sw = jnp.where(first, pltpu.roll(z, 256 - HEAD_DIM // 2, 1), pltpu.roll(z, HEAD_DIM // 2, 1))
            return z * cs + sw * sn

        ft_scr[0] = rope(x[:, 0:256]).T
        ft_scr[1] = (rope(x[:, 256:512]) * (HEAD_DIM ** -0.5)).T
        ft_scr[2] = x[:, 512:768].T

    base = pl.multiple_of(h * HEAD_DIM, HEAD_DIM)
    nseq = rt_ref.shape[1]
    gamma = jnp.exp(jnp.log1p(-jnp.exp2(-5.0 - jnp.full((1, nseq), h, jnp.int32).astype(F32))))
    q_t, k_t, v_t = [ft_scr[n, pl.ds(base, HEAD_DIM), :] for n in range(3)]
    qk = jnp.sum(q_t * k_t, axis=0, keepdims=True)

    def body(d, cross):
        rows = pl.ds(pl.multiple_of(d * HEAD_DIM, HEAD_DIM), HEAD_DIM)
        r_old = r0_ref[rows, :]
        rt_ref[rows, :] = gamma * r_old + ft_scr[1, pl.ds(base + d, 1), :] * v_t
        return cross + ft_scr[0, pl.ds(base + d, 1), :] * r_old

    cross = lax.fori_loop(0, HEAD_DIM, body, jnp.zeros((HEAD_DIM, nseq), F32))
    acc_scr[pl.ds(base, HEAD_DIM), :] = qk * v_t + gamma * cross

    @pl.when(h == H_D - 1)
    def _():
        o_ref[...] = jax.nn.silu(c_ref[:, 768:1024]) * (_segment_norm(acc_scr[...].T, RET_GN_EPS) * gn_ref[...])


def _ret_step(colsd, cos, sin, r_all, layer, gn):
    b = colsd.shape[0]
    hd2 = HEAD_DIM * HEAD_DIM
    full = lambda *_: (0, 0)
    r_rows = jnp.transpose(r_all, (0, 2, 3, 4, 1)).reshape(-1, b)
    o, rt = pl.pallas_call(
        _ret_step_body,
        grid=(H_D,),
        in_specs=[pl.BlockSpec((b, 1024), full), pl.BlockSpec((b, 256), full), pl.BlockSpec((b, 256), full),
                  pl.BlockSpec((1, 256), full), pl.BlockSpec((hd2, b), lambda h: (layer * H_D + h, 0))],
        out_specs=[pl.BlockSpec((b, 256), full), pl.BlockSpec((hd2, b), lambda h: (h, 0))],
        out_shape=[jax.ShapeDtypeStruct((b, 256), F32), jax.ShapeDtypeStruct((H_D * hd2, b), F32)],
        scratch_shapes=[pltpu.VMEM((3, 256, b), F32), pltpu.VMEM((256, b), F32)],
        compiler_params=_cp("arbitrary"),
        name="ret_step",
    )(colsd, cos, sin, gn.reshape(1, 256), r_rows)
    return o, jnp.transpose(rt.reshape(H_D, HEAD_DIM, HEAD_DIM, b), (3, 0, 1, 2))


def _s5_step_body(u_ref, x0_ref, a1_ref, a2_ref, bt_ref, ct_ref, d_ref, wg_ref, o_ref, xt_ref):
    u = u_ref[...]
    dot_hi = lambda a, b: jnp.dot(a, b, precision=HI, preferred_element_type=F32)
    x0 = x0_ref[...]
    n = x0.shape[0]
    even = (lax.broadcasted_iota(jnp.int32, x0.shape, 0) % 2) == 0
    partner = jnp.where(even, pltpu.roll(x0, n - 1, 0), pltpu.roll(x0, 1, 0))
    x = a1_ref[...] * x0 + a2_ref[...] * partner + dot_hi(bt_ref[...], u.T)
    xt_ref[...] = x
    y = dot_hi(ct_ref[...], x).T + d_ref[...] * u
    z = jax.nn.gelu(y)
    o_ref[...] = z * jax.nn.sigmoid(dot_hi(z, wg_ref[...]))


def _s5_step(u, x_all, layer, lw):
    b = u.shape[0]
    a1, a2, bt, ct, d_row = lw['s5_step']
    n = 2 * G_C * S5_P
    full = lambda *_: (0, 0)
    x_rows = jnp.transpose(x_all, (0, 2, 3, 4, 1)).reshape(-1, b)
    o, xt = pl.pallas_call(
        _s5_step_body,
        grid=(1,),
        in_specs=[pl.BlockSpec((b, C_C), full), pl.BlockSpec((n, b), lambda i: (layer, 0)), pl.BlockSpec((n, 1), full),
                  pl.BlockSpec((n, 1), full), pl.BlockSpec((n, C_C), full), pl.BlockSpec((C_C, n), full),
                  pl.BlockSpec((1, C_C), full), pl.BlockSpec((C_C, C_C), full)],
        out_specs=[pl.BlockSpec((b, C_C), full), pl.BlockSpec((n, b), full)],
        out_shape=[jax.ShapeDtypeStruct((b, C_C), F32), jax.ShapeDtypeStruct((n, b), F32)],
        compiler_params=_cp("arbitrary"),
        name="s5_step",
    )(u, x_rows, a1, a2, bt, ct, d_row, lw['s5_w_glu'])
    return o, jnp.transpose(xt.reshape(G_C, S5_P, 2, b), (3, 0, 1, 2))


def _s5_params(lw):
    lr, li = lw['s5_lambda_re'], lw['s5_lambda_im']
    dt = jnp.exp(lw['s5_log_step'])[:, None]
    mag = jnp.exp(lr * dt)
    ar, ai = mag * jnp.cos(li * dt), mag * jnp.sin(li * dt)
    nr, ni = ar - 1.0, ai
    den = lr * lr + li * li
    fr, fi = (nr * lr + ni * li) / den, (ni * lr - nr * li) / den
    b_re, b_im = lw['s5_b'][0], lw['s5_b'][1]
    bbr = fr[..., None] * b_re - fi[..., None] * b_im
    bbi = fr[..., None] * b_im + fi[..., None] * b_re
    eye = jnp.eye(G_C, dtype=F32)
    bd_in = lambda m: jnp.einsum('gpc,gh->gchp', m, eye).reshape(G_C * S5_CH, G_C * S5_P)
    bd_out = lambda m: jnp.einsum('gcp,gh->gphc', m, eye).reshape(G_C * S5_P, G_C * S5_CH)
    b_big = jnp.concatenate([bd_in(bbr), bd_in(bbi)], axis=1)
    c_big = jnp.concatenate([bd_out(lw['s5_c'][0]), -bd_out(lw['s5_c'][1])], axis=0)
    a_row = jnp.concatenate([ar.reshape(1, -1), ai.reshape(1, -1)], axis=1)
    d_row = lw['s5_d'].reshape(1, C_C)
    n = 2 * G_C * S5_P
    a1 = jnp.stack([ar, ar], axis=-1).reshape(n, 1)
    a2 = jnp.stack([-ai, ai], axis=-1).reshape(n, 1)
    bt = jnp.stack([jnp.einsum('gpc,gh->gphc', bbr, eye), jnp.einsum('gpc,gh->gphc', bbi, eye)], axis=2).reshape(n, G_C * S5_CH)
    ct = jnp.stack([jnp.einsum('gcp,gh->gchp', lw['s5_c'][0], eye), -jnp.einsum('gcp,gh->gchp', lw['s5_c'][1], eye)],
                   axis=-1).reshape(G_C * S5_CH, n)
    return (a_row, b_big, c_big, d_row), (a1, a2, bt, ct, d_row)


def _s5_body(u_ref, x0_ref, a_ref, b_ref, c_ref, d_ref, wg_ref, o_ref, xt_ref, x_scr, bu_scr, xs_scr, *, nb, tt, mm_dtype, prec):
    sub = 8

    @pl.when(pl.program_id(0) == 0)
    def _():
        x_scr[...] = jnp.zeros_like(x_scr)
        x_scr[0:nb, :] = x0_ref[...]
        bu_scr[...] = jnp.zeros_like(bu_scr)

    np_ = G_C * S5_P
    ncb = np_ // LANE
    for b in range(nb):
        bu = jnp.dot(u_ref[b].astype(mm_dtype), b_ref[...], precision=prec, preferred_element_type=F32)
        for cb in range(2 * ncb):
            bu_scr[cb, pl.ds(b, tt, stride=sub), :] = bu[:, cb * LANE:(cb + 1) * LANE]
    a = a_ref[...]

    def step(t, x):
        rows = pl.ds(pl.multiple_of(t * sub, sub), sub)
        new = [None] * (2 * ncb)
        for cb in range(ncb):
            re, im = slice(cb * LANE, (cb + 1) * LANE), slice(np_ + cb * LANE, np_ + (cb + 1) * LANE)
            ar, ai, xr, xi = a[:, re], a[:, im], x[:, re], x[:, im]
            new[cb] = ar * xr - ai * xi + bu_scr[cb, rows, :]
            new[ncb + cb] = ar * xi + ai * xr + bu_scr[ncb + cb, rows, :]
            xs_scr[cb, rows, :] = new[cb]
            xs_scr[ncb + cb, rows, :] = new[ncb + cb]
        return jnp.concatenate(new, axis=1)

    x_last = lax.fori_loop(0, tt, step, x_scr[...], unroll=8)
    x_scr[...] = x_last
    xt_ref[...] = x_last[0:nb, :]
    for b in range(nb):
        u = u_ref[b]
        xs = jnp.concatenate([xs_scr[cb, pl.ds(b, tt, stride=sub), :] for cb in range(2 * ncb)], axis=1)
        y = jnp.dot(xs.astype(mm_dtype), c_ref[...], precision=prec, preferred_element_type=F32) + d_ref[...] * u
        z = jax.nn.gelu(y)
        o_ref[b] = z * jax.nn.sigmoid(jnp.dot(z.astype(mm_dtype), wg_ref[...], precision=prec, preferred_element_type=F32))


def _s5_mixer(u, x0, lw, b, t, tt, exact):
    a_row, b_big, c_big, d_row = lw['s5']
    mm_dtype = F32 if exact else BF16
    prec = HI if exact else None
    x0l = jnp.concatenate([x0[..., 0].reshape(b, -1), x0[..., 1].reshape(b, -1)], axis=1)
    np2 = 2 * G_C * S5_P
    fixed = lambda i: (0, 0)
    o, xt = pl.pallas_call(
        functools.partial(_s5_body, nb=b, tt=tt, mm_dtype=mm_dtype, prec=prec),
        grid=(t // tt,),
        in_specs=[pl.BlockSpec((b, tt, C_C), lambda i: (0, i, 0)), pl.BlockSpec((b, np2), fixed),
                  pl.BlockSpec((1, np2), fixed), pl.BlockSpec((C_C, np2), fixed), pl.BlockSpec((np2, C_C), fixed),
                  pl.BlockSpec((1, C_C), fixed), pl.BlockSpec((C_C, C_C), fixed)],
        out_specs=[pl.BlockSpec((b, tt, C_C), lambda i: (0, i, 0)), pl.BlockSpec((b, np2), fixed)],
        out_shape=[jax.ShapeDtypeStruct((b, t, C_C), F32), jax.ShapeDtypeStruct((b, np2), F32)],
        scratch_shapes=[pltpu.VMEM((8, np2), F32), pltpu.VMEM((np2 // LANE, 8 * tt, LANE), F32),
                        pltpu.VMEM((np2 // LANE, 8 * tt, LANE), F32)],
        compiler_params=_cp("arbitrary"),
        name="s5",
    )(u.reshape(b, t, C_C), x0l, a_row, b_big.astype(mm_dtype), c_big.astype(mm_dtype), d_row, lw['s5_w_glu'].astype(mm_dtype))
    xt = xt.reshape(b, 2, G_C, S5_P)
    return o.reshape(b * t, C_C), jnp.stack([xt[:, 0], xt[:, 1]], axis=-1)


def _ret_tables(pos, c):
    cos, sin = _rope_tables(pos, HEAD_DIM, RET_THETA, HEAD_DIM, H_D)
    log_g = jnp.log1p(-jnp.exp2(-5.0 - jnp.arange(H_D, dtype=F32)))
    i = jnp.arange(c, dtype=F32)
    diff = i[:, None] - i[None, :]
    dmat = jnp.where(diff >= 0, jnp.exp(jnp.maximum(diff, 0.0)[None] * log_g[:, None, None]), 0.0).reshape(H_D * c, c)
    q_dec = jnp.repeat(jnp.exp((i + 1.0)[None] * log_g[:, None]).T, HEAD_DIM, axis=1)
    k_dec = jnp.repeat(jnp.exp((c - 1.0 - i)[None] * log_g[:, None]).T, HEAD_DIM, axis=1)
    chunk_dec = jnp.repeat(jnp.exp(c * log_g), HEAD_DIM).reshape(256, 1)
    return cos, sin, dmat, q_dec, k_dec, chunk_dec


def _ret_body(c_ref, cos_ref, sin_ref, dmat_ref, qdec_ref, kdec_ref, cdec_ref, r0_ref, gn_ref, o_ref, rt_ref, r_scr, *, c):
    @pl.when(pl.program_id(1) == 0)
    def _():
        r_scr[...] = r0_ref[0]

    x = c_ref[0]
    q, k, v, g = x[:, 0:256], x[:, 256:512], x[:, 512:768], x[:, 768:1024]
    cs, sn = cos_ref[...], sin_ref[...]
    lane = lax.broadcasted_iota(jnp.int32, (c, 256), 1)
    first = (lane % HEAD_DIM) < (HEAD_DIM // 2)

    def rope(z):
        sw = jnp.where(first, pltpu.roll(z, 256 - HEAD_DIM // 2, 1), pltpu.roll(z, HEAD_DIM // 2, 1))
        return z * cs + sw * sn

    q = rope(q)
    k = rope(k) * (HEAD_DIM ** -0.5)
    head = lane // HEAD_DIM
    kb, vb = k.astype(BF16), v.astype(BF16)
    qstack = jnp.concatenate([jnp.where(head == h, q, 0.0) for h in range(H_D)], axis=0).astype(BF16)
    s = lax.dot_general(qstack, kb, (((1,), (1,)), ((), ())), preferred_element_type=F32) * dmat_ref[...]
    pv = jnp.dot(s.astype(BF16), vb, preferred_element_type=F32)
    inner = jnp.zeros((c, 256), F32)
    for h in range(H_D):
        inner = inner + jnp.where(head == h, pv[h * c:(h + 1) * c], 0.0)
    r_old = r_scr[...]
    cross = jnp.dot((q * qdec_ref[...]).astype(BF16), r_old.astype(BF16), preferred_element_type=F32)
    kv = lax.dot_general((k * kdec_ref[...]).astype(BF16), vb, (((0,), (0,)), ((), ())), preferred_element_type=F32)
    bd = _block_ones(256, HEAD_DIM, F32)
    r_new = cdec_ref[...] * r_old + kv * bd
    r_scr[...] = r_new
    rt_ref[0] = r_new
    o_ref[0] = jax.nn.silu(g) * (_segment_norm(inner + cross, RET_GN_EPS) * gn_ref[...])


def _ret_mixer(colsd, r0, lw, tabs, b, t):
    c = RET_CHUNK if t % RET_CHUNK == 0 else t
    cos, sin, dmat, q_dec, k_dec, chunk_dec = tabs
    eye = jnp.eye(H_D, dtype=F32)
    r0l = jnp.einsum('bhde,hg->bhdge', r0, eye).reshape(b, 256, 256)
    n_t = t // c
    fixed = lambda i, j: (0, 0)
    o, rt = pl.pallas_call(
        functools.partial(_ret_body, c=c),
        grid=(b, n_t),
        in_specs=[pl.BlockSpec((1, c, 1024), lambda i, j: (i, j, 0)),
                  pl.BlockSpec((c, 256), lambda i, j: (j, 0)), pl.BlockSpec((c, 256), lambda i, j: (j, 0)),
                  pl.BlockSpec((H_D * c, c), fixed), pl.BlockSpec((c, 256), fixed), pl.BlockSpec((c, 256), fixed),
                  pl.BlockSpec((256, 1), fixed), pl.BlockSpec((1, 256, 256), lambda i, j: (i, 0, 0)),
                  pl.BlockSpec((1, 256), fixed)],
        out_specs=[pl.BlockSpec((1, c, 256), lambda i, j: (i, j, 0)), pl.BlockSpec((1, 256, 256), lambda i, j: (i, 0, 0))],
        out_shape=[jax.ShapeDtypeStruct((b, t, 256), F32), jax.ShapeDtypeStruct((b, 256, 256), F32)],
        scratch_shapes=[pltpu.VMEM((256, 256), F32)],
        compiler_params=_cp("parallel", "arbitrary"),
        name="retention",
    )(colsd.reshape(b, t, 1024), cos, sin, dmat, q_dec, k_dec, chunk_dec, r0l, lw['ret_gn'].reshape(1, 256))
    rt = jnp.einsum('bhdge,hg->bhde', rt.reshape(b, H_D, HEAD_DIM, H_D, HEAD_DIM), eye)
    return o.reshape(b * t, 256), rt


def _partner(x, d, period):
    pos = lax.broadcasted_iota(jnp.int32, x.shape, 1) % period
    return jnp.where(pos + d < period, pltpu.roll(x, LANE - d, 1), pltpu.roll(x, period - d, 1))


def _out_body(x_ref, oa_ref, ob_ref, oc_ref, od_ref, w_ref, nw_ref, wr_ref, br_ref, x1_ref, h_ref, comb_ref):
    acc = x_ref[...]
    for i, ref in enumerate((oa_ref, ob_ref, oc_ref, od_ref)):
        acc = acc + jnp.dot(ref[...].astype(BF16), w_ref[256 * i:256 * (i + 1), :], preferred_element_type=F32)
    x1_ref[...] = acc
    h = acc * lax.rsqrt(jnp.mean(acc * acc, axis=-1, keepdims=True) + RMS_EPS) * nw_ref[...]
    hb = h.astype(BF16)
    h_ref[...] = hb
    h_lo = (h - hb.astype(F32)).astype(BF16)
    logits = (jnp.dot(hb, wr_ref[0], preferred_element_type=F32) + jnp.dot(hb, wr_ref[1], preferred_element_type=F32)
              + jnp.dot(h_lo, wr_ref[0], preferred_element_type=F32) + br_ref[...])
    le, lg = logits[:, :LANE], logits[:, LANE:]
    lane = lax.broadcasted_iota(jnp.int32, le.shape, 1)
    mg = jnp.max(lg, axis=-1, keepdims=True)
    eg = jnp.exp(lg - mg)
    pg = eg / (jnp.sum(eg, axis=-1, keepdims=True) * (1.0 / 32.0))
    gidx = (lane % N_EXPERTS) // EXP_PER_GROUP
    g_rank = jnp.zeros_like(pg)
    for d in range(1, N_GROUPS):
        other = pltpu.roll(pg, LANE - EXP_PER_GROUP * d, 1)
        wrapped = gidx + d >= N_GROUPS
        g_rank = g_rank + jnp.where((other > pg) | ((other == pg) & wrapped), 1.0, 0.0)
    kidx = lane % EXP_PER_GROUP
    others = [_partner(le, d, EXP_PER_GROUP) for d in range(1, EXP_PER_GROUP)]
    me = functools.reduce(jnp.maximum, others, le)
    ee = jnp.exp(le - me)
    se = ee
    for d in range(1, EXP_PER_GROUP):
        se = se + _partner(ee, d, EXP_PER_GROUP)
    pe = ee / se
    e_rank = jnp.zeros_like(pe)
    for d in range(1, EXP_PER_GROUP):
        other = _partner(pe, d, EXP_PER_GROUP)
        wrapped = kidx + d >= EXP_PER_GROUP
        e_rank = e_rank + jnp.where((other > pe) | ((other == pe) & wrapped), 1.0, 0.0)
    top = jnp.where(e_rank < 2.0, pe, 0.0)
    den = top
    for d in range(1, EXP_PER_GROUP):
        den = den + _partner(top, d, EXP_PER_GROUP)
    comb = jnp.where((g_rank < 1.0) & (lane < N_EXPERTS), pg * (top / den), 0.0)
    comb_ref[...] = comb


def _out_router(x, oa, ob, oc, od, lw, tm):
    n = x.shape[0]
    row = lambda i: (i, 0)
    fixed = lambda i: (0, 0)
    mix = pl.BlockSpec((tm, 256), row)
    return pl.pallas_call(
        _out_body,
        grid=(n // tm,),
        in_specs=[pl.BlockSpec((tm, D_MODEL), row), mix, mix, mix, mix,
                  pl.BlockSpec((D_MODEL, D_MODEL), fixed), pl.BlockSpec((1, D_MODEL), fixed),
                  pl.BlockSpec((2, D_MODEL, 2 * LANE), lambda i: (0, 0, 0)), pl.BlockSpec((1, 2 * LANE), fixed)],
        out_specs=[pl.BlockSpec((tm, D_MODEL), row), pl.BlockSpec((tm, D_MODEL), row), pl.BlockSpec((tm, LANE), row)],
        out_shape=[jax.ShapeDtypeStruct((n, D_MODEL), F32), jax.ShapeDtypeStruct((n, D_MODEL), BF16),
                   jax.ShapeDtypeStruct((n, LANE), F32)],
        compiler_params=_cp("parallel"),
        name="out_router",
    )(x, oa, ob, oc, od, lw['w_out'], lw['norm_ffn'], lw['w_router'], lw['b_router'])


def _router_weights(w_grp, b_grp, w_exp, b_exp):
    we = jnp.transpose(w_exp, (1, 0, 2)).reshape(D_MODEL, N_EXPERTS)
    wg = jnp.repeat(w_grp, EXP_PER_GROUP, axis=1)
    reps = LANE // N_EXPERTS
    w = jnp.concatenate([jnp.tile(we, (1, reps)), jnp.tile(wg, (1, reps))], axis=1)
    b = jnp.concatenate([jnp.tile(b_exp.reshape(1, N_EXPERTS), (1, reps)),
                         jnp.tile(jnp.repeat(b_grp, EXP_PER_GROUP).reshape(1, N_EXPERTS), (1, reps))], axis=1)
    w_hi = w.astype(BF16)
    return jnp.stack([w_hi, (w - w_hi.astype(F32)).astype(BF16)]), b


def _moe_body(h_ref, comb_ref, x1_ref, wg_ref, wu_ref, wd_ref, nf_ref, *out_refs, final):
    acc_ref = out_refs[-1]
    e = pl.program_id(1)

    @pl.when(e == 0)
    def _():
        acc_ref[...] = x1_ref[...]

    h = h_ref[...]
    comb = comb_ref[...]
    lane = lax.broadcasted_iota(jnp.int32, comb.shape, 1)
    c = jnp.sum(jnp.where(lane == e, comb, 0.0), axis=-1, keepdims=True)
    hg = jnp.dot(h, wg_ref[0], preferred_element_type=F32)
    hu = jnp.dot(h, wu_ref[0], preferred_element_type=F32)
    act = (jax.nn.silu(hg) * hu * c).astype(BF16)
    acc_ref[...] += jnp.dot(act, wd_ref[0], preferred_element_type=F32)

    @pl.when(e == N_EXPERTS - 1)
    def _():
        x2 = acc_ref[...]
        if final:
            out_refs[0][...] = x2 * lax.rsqrt(jnp.mean(x2 * x2, axis=-1, keepdims=True) + RMS_EPS) * nf_ref[...]
        else:
            out_refs[0][...] = x2


def _moe(h, comb, x1, lw, norm_final, tm, final):
    n = x1.shape[0]
    row = lambda i, e: (i, 0)
    per_e = lambda i, e: (e, 0, 0)
    return pl.pallas_call(
        functools.partial(_moe_body, final=final),
        grid=(n // tm, N_EXPERTS),
        in_specs=[pl.BlockSpec((tm, D_MODEL), row), pl.BlockSpec((tm, LANE), row), pl.BlockSpec((tm, D_MODEL), row),
                  pl.BlockSpec((1, D_MODEL, D_EXPERT), per_e), pl.BlockSpec((1, D_MODEL, D_EXPERT), per_e),
                  pl.BlockSpec((1, D_EXPERT, D_MODEL), per_e), pl.BlockSpec((1, D_MODEL), lambda i, e: (0, 0))],
        out_specs=pl.BlockSpec((tm, D_MODEL), row),
        out_shape=jax.ShapeDtypeStruct((n, D_MODEL), F32),
        scratch_shapes=[pltpu.VMEM((tm, D_MODEL), F32)],
        compiler_params=_cp("parallel", "arbitrary"),
        name="moe",
    )(h, comb, x1, lw['moe_wg'], lw['moe_wu'], lw['moe_wd'], norm_final.reshape(1, D_MODEL))


def _prep_layer(l, p):
    w_in = p['w_in'][l]
    o = _offsets(SPLIT_SIZES)
    segs = jnp.split(w_in, o, axis=1)
    w_all = jnp.concatenate([segs[0], segs[1], _pad_to(segs[2], LANE, 1), _pad_to(segs[3], SHIFT_PAD, 1), segs[4], segs[5]],
                            axis=1).astype(BF16)
    w_prec = w_in[:, :PRECISE_COLS]
    w_lo = (w_prec - w_prec.astype(BF16).astype(F32)).astype(BF16)
    lw = {'layer': l, 'w_all': w_all, 'w_lo': w_lo, 'norm_mix': p['norm_mix'][l]}
    lw['cmp'] = _cmp_weights(p['nsa_cmp_w1'][l], p['nsa_cmp_b1'][l], p['nsa_cmp_w2'][l])
    lw['rwkv_mu'] = _pad_to(p['rwkv_mu'][l].reshape(1, SHIFT_B), SHIFT_PAD, 1)
    lw['rwkv_vec'] = _pad_to(p['rwkv_vec'][l], 8, 0)
    z = lambda a, b: jnp.zeros((a, b), F32)
    lw['rwkv_wup'] = jnp.concatenate([p['rwkv_w_up'][l], z(LANE - LORA_W, C_B)], axis=0)
    lw['rwkv_aup'] = jnp.concatenate([z(LORA_W, C_B), p['rwkv_a_up'][l], z(LANE - LORA_W - LORA_A, C_B)], axis=0)
    lw['rwkv_gup'] = jnp.concatenate([z(LORA_W + LORA_A, C_B), p['rwkv_g_up'][l], z(LANE - LORA_W - LORA_A - LORA_G, C_B)], axis=0)
    for name in ('s5_lambda_re', 's5_lambda_im', 's5_b', 's5_c', 's5_d', 's5_log_step', 's5_w_glu', 'ret_gn'):
        lw[name] = p[name][l]
    lw['s5'], lw['s5_step'] = _s5_params(lw)
    lw['w_out'] = p['w_out'][l].astype(BF16)
    lw['norm_ffn'] = p['norm_ffn'][l].reshape(1, D_MODEL)
    lw['w_router'], lw['b_router'] = _router_weights(p['moe_w_grp'][l], p['moe_b_grp'][l], p['moe_w_exp'][l], p['moe_b_exp'][l])
    lw['moe_wg'] = p['moe_w_gate'][l].astype(BF16)
    lw['moe_wu'] = p['moe_w_up'][l].astype(BF16)
    lw['moe_wd'] = p['moe_w_down'][l].astype(BF16)
    return lw


ROW_TILE = 512
MOE_ROW_TILE = 1024
SCAN_TILE = 256


def _prompt_layer(x, lw, tabs, b, t, norm_final):
    cos_a, sin_a, ret_tabs = tabs
    q, kv, gate, colsb, u, colsd = _proj(x, lw['norm_mix'], lw['w_all'], lw['w_lo'], cos_a, sin_a, ROW_TILE)
    o_a = _nsa_prompt_mixer(q, kv, gate, lw, b, t)
    o_b, s_rwkv, s_shift = _rwkv_mixer(colsb, jnp.zeros((b, SHIFT_PAD), F32), jnp.zeros((b, H_B, HEAD_DIM, HEAD_DIM), F32),
                                       lw, b, t, ROW_TILE, b, SCAN_TILE)
    o_c, s_s5 = _s5_mixer(u, jnp.zeros((b, G_C, S5_P, 2), F32), lw, b, t, SCAN_TILE, False)
    o_d, s_ret = _ret_mixer(colsd, jnp.zeros((b, H_D, HEAD_DIM, HEAD_DIM), F32), lw, ret_tabs, b, t)
    x1, h, comb = _out_router(x, o_a, o_b, o_c, o_d, lw, ROW_TILE)
    x2 = _moe(h, comb, x1, lw, norm_final, MOE_ROW_TILE, lw['layer'] == DEPTH - 1)
    kv3 = kv.reshape(b, t, 6 * LANE)
    rows = kv3[:, :, :4 * LANE].reshape(b, t, 4, N_KV_A, HEAD_DIM)
    win = kv3[:, t - min(WINDOW, t):, 4 * LANE:].reshape(b, min(WINDOW, t), 2, N_KV_A, HEAD_DIM)
    return x2, (rows, win, s_rwkv, s_shift, s_s5, s_ret)


def _sample_layer(x, lw, tabs, b, pos, cache_kv, page_table, win_buf, s_rwkv, s_shift, s_s5, s_ret, norm_final):
    cos_a, sin_a, ret_cs = tabs
    q, kv, gate, colsb, u, colsd = _proj(x, lw['norm_mix'], lw['w_all'], lw['w_lo'], cos_a, sin_a, b)
    o_a, win = _nsa_sample(q, kv, gate, cache_kv, win_buf, lw['layer'], page_table, lw['cmp'], int(pos[0]))
    rows = kv[:, :4 * LANE].reshape(b, 1, 4, N_KV_A, HEAD_DIM)
    o_b, s_rwkv = _rwkv_step(colsb, _pad_to(s_shift, SHIFT_PAD, 1), s_rwkv, lw['layer'], lw)
    s_shift = colsb[:, :SHIFT_B]
    o_c, s_s5 = _s5_step(u, s_s5, lw['layer'], lw)
    o_d, s_ret = _ret_step(colsd, ret_cs[0], ret_cs[1], s_ret, lw['layer'], lw['ret_gn'])
    x1, h, comb = _out_router(x, o_a, o_b, o_c, o_d, lw, b)
    x2 = _moe(h, comb, x1, lw, norm_final, b, lw['layer'] == DEPTH - 1)
    return x2, (rows, win, s_rwkv, s_shift, s_s5, s_ret)


def kernel(x_prompt, x_sample, cache_nsa_kv, cache_nsa_win, state_rwkv, state_rwkv_shift, state_s5, state_ret, page_table, norm_mix, w_in, nsa_cmp_w1, nsa_cmp_b1, nsa_cmp_w2, rwkv_mu, rwkv_vec, rwkv_w_up, rwkv_a_up, rwkv_g_up, s5_lambda_re, s5_lambda_im, s5_b, s5_c, s5_d, s5_log_step, s5_w_glu, ret_gn, w_out, norm_ffn, moe_w_grp, moe_b_grp, moe_w_exp, moe_b_exp, moe_w_gate, moe_w_up, moe_w_down, norm_final):
    p = dict(norm_mix=norm_mix, w_in=w_in, nsa_cmp_w1=nsa_cmp_w1, nsa_cmp_b1=nsa_cmp_b1, nsa_cmp_w2=nsa_cmp_w2,
             rwkv_mu=rwkv_mu, rwkv_vec=rwkv_vec, rwkv_w_up=rwkv_w_up, rwkv_a_up=rwkv_a_up, rwkv_g_up=rwkv_g_up,
             s5_lambda_re=s5_lambda_re, s5_lambda_im=s5_lambda_im, s5_b=s5_b, s5_c=s5_c, s5_d=s5_d,
             s5_log_step=s5_log_step, s5_w_glu=s5_w_glu, ret_gn=ret_gn, w_out=w_out, norm_ffn=norm_ffn,
             moe_w_grp=moe_w_grp, moe_b_grp=moe_b_grp, moe_w_exp=moe_w_exp, moe_b_exp=moe_b_exp,
             moe_w_gate=moe_w_gate, moe_w_up=moe_w_up, moe_w_down=moe_w_down)
    bp, tp = x_prompt.shape[:2]
    bs, ts = x_sample.shape[:2]
    assert ts == 1 and tp % ROW_TILE == 0 and (bp * tp) % MOE_ROW_TILE == 0 and tp % (4 * LANE) == 0 and bs % 8 == 0
    past_len = page_table.shape[1] * cache_nsa_kv.shape[2]
    pos_p = np.arange(tp)
    pos_s = past_len + np.arange(ts)
    c = RET_CHUNK if tp % RET_CHUNK == 0 else tp
    tabs_p = _rope_tables(pos_p, ROT_DIM, ROPE_THETA, HEAD_DIM, 2) + (_ret_tables(pos_p, c),)
    pos_rows = np.repeat(pos_s, bs)
    tabs_s = _rope_tables(pos_rows, ROT_DIM, ROPE_THETA, HEAD_DIM, 2) + (_rope_tables(pos_rows, HEAD_DIM, RET_THETA, HEAD_DIM, H_D),)
    xp = x_prompt.reshape(bp * tp, D_MODEL)
    xs = x_sample.reshape(bs * ts, D_MODEL)
    sts_p, sts_s = [], []
    for l in range(DEPTH):
        lw = _prep_layer(l, p)
        xp, st_p = _prompt_layer(xp, lw, tabs_p, bp, tp, norm_final)
        xs, st_s = _sample_layer(xs, lw, tabs_s, bs, pos_s, cache_nsa_kv, page_table, cache_nsa_win, state_rwkv,
                                     state_rwkv_shift[l], state_s5, state_ret, norm_final)
        rows, win, s1, s2, s3, s4 = st_s
        sts_s.append((rows, win, s1, s2, s3, s4))
        sts_p.append(st_p)
    new_p = [jnp.stack([st[i] for st in sts_p]) for i in range(6)]
    new_s = [jnp.stack([st[i] for st in sts_s]) for i in range(6)]
    return (xp.reshape(bp, tp, D_MODEL), xs.reshape(bs, ts, D_MODEL), new_p[0], new_s[0], new_p[1], new_s[1],
            new_p[2], new_s[2], new_p[3], new_s[3], new_p[4], new_s[4], new_p[5], new_s[5])
```

```python
import functools

import numpy as np
import jax
import jax.numpy as jnp
from jax import lax
from jax.experimental import pallas as pl
from jax.experimental.pallas import tpu as pltpu

F32 = jnp.float32
BF16 = jnp.bfloat16
HI = lax.Precision.HIGHEST

D_MODEL = 1024
DEPTH = 2
HEAD_DIM = 64
C_A = C_B = C_C = C_D = 256
H_A = 4
N_KV_A = 2
R_A = 2
ROT_DIM = 16
ROPE_THETA = 500000.0
CMP_BLOCK = 32
CMP_STRIDE = 16
CMP_HIDDEN = 128
SEL_BLOCK = 64
TOP_K = 16
WINDOW = 512
NEG_INF = -1e30
FORCED_SCORE = 1e9
BLOCKED_SCORE = -1e9
H_B = 4
LORA_W = 16
LORA_A = 16
LORA_G = 32
SHIFT_B = 832
SHIFT_PAD = 896
RWKV_GN_EPS = 64e-5
S5_CH = 16
G_C = 16
S5_P = 64
H_D = 4
RET_CHUNK = 128
RET_THETA = 10000.0
RET_GN_EPS = 1e-5
N_GROUPS = 4
EXP_PER_GROUP = 4
N_EXPERTS = 16
D_EXPERT = 256
RMS_EPS = 1e-6
SPLIT_SIZES = (C_A, 6 * N_KV_A * HEAD_DIM, 3 * H_A, SHIFT_B, C_C, 4 * C_D)
LANE = 128
VMEM_LIMIT = 56 * 1024 * 1024


def _cp(*sem):
    return pltpu.CompilerParams(dimension_semantics=sem, vmem_limit_bytes=VMEM_LIMIT)


def _offsets(sizes):
    return [int(s) for s in np.cumsum(sizes)[:-1]]


def _pad_to(a, n, axis):
    pad = [(0, 0)] * a.ndim
    pad[axis] = (0, n - a.shape[axis])
    return jnp.pad(a, pad)


def _block_ones(n, blk, dtype):
    r = lax.broadcasted_iota(jnp.int32, (n, n), 0) // blk
    c = lax.broadcasted_iota(jnp.int32, (n, n), 1) // blk
    return (r == c).astype(dtype)


def _rope_tables(pos, rot_dim, theta, period, reps):
    half = rot_dim // 2
    inv = theta ** (-jnp.arange(half, dtype=F32) / half)
    ang = jnp.asarray(pos, F32)[:, None] * inv[None, :]
    cos, sin = jnp.cos(ang), jnp.sin(ang)
    n = ang.shape[0]
    rest = period - rot_dim
    c = jnp.concatenate([cos, cos, jnp.ones((n, rest), F32)], -1)
    s = jnp.concatenate([-sin, sin, jnp.zeros((n, rest), F32)], -1)
    return jnp.tile(c, (1, reps)), jnp.tile(s, (1, reps))


PRECISE_COLS = 384


def _proj_body(x_ref, nw_ref, w_ref, wlo_ref, cos_ref, sin_ref, q_ref, kv_ref, g_ref, cb_ref, u_ref, cd_ref):
    x = x_ref[...]
    h = x * lax.rsqrt(jnp.mean(x * x, axis=-1, keepdims=True) + RMS_EPS) * nw_ref[...]
    hb = h.astype(BF16)
    h_lo = (h - hb.astype(F32)).astype(BF16)
    c = cos_ref[...]
    s = sin_ref[...]
    first = (lax.broadcasted_iota(jnp.int32, c.shape, 1) % HEAD_DIM) < (ROT_DIM // 2)

    def rope(z):
        sw = jnp.where(first, pltpu.roll(z, LANE - ROT_DIM // 2, 1), pltpu.roll(z, ROT_DIM // 2, 1))
        return z * c + sw * s

    def dot(a, b):
        z = jnp.dot(hb, w_ref[:, a:b], preferred_element_type=F32)
        if b <= PRECISE_COLS:
            z = z + (jnp.dot(hb, wlo_ref[:, a:b], preferred_element_type=F32)
                     + jnp.dot(h_lo, w_ref[:, a:b], preferred_element_type=F32))
        return z

    for j in range(2):
        q_ref[:, LANE * j:LANE * (j + 1)] = rope(dot(LANE * j, LANE * (j + 1)))
    for j in range(6):
        z = dot(256 + LANE * j, 256 + LANE * (j + 1))
        kv_ref[:, LANE * j:LANE * (j + 1)] = rope(z) if j % 2 == 0 else z
    g_ref[...] = dot(1024, 1152)
    cb_ref[...] = dot(1152, 2048)
    u_ref[...] = dot(2048, 2304)
    cd_ref[...] = dot(2304, 3328)


def _proj(x2d, norm_w, w_all, w_lo, cos_t, sin_t, tm):
    n = x2d.shape[0]
    t_tiles = cos_t.shape[0] // tm
    row = lambda i: (i, 0)
    fixed = lambda i: (0, 0)
    tab = lambda i: (i % t_tiles, 0)
    widths = (256, 768, 128, SHIFT_PAD, 256, 1024)
    return pl.pallas_call(
        _proj_body,
        grid=(n // tm,),
        in_specs=[pl.BlockSpec((tm, D_MODEL), row), pl.BlockSpec((1, D_MODEL), fixed),
                  pl.BlockSpec((D_MODEL, 3328), fixed), pl.BlockSpec((D_MODEL, PRECISE_COLS), fixed),
                  pl.BlockSpec((tm, LANE), tab), pl.BlockSpec((tm, LANE), tab)],
        out_specs=[pl.BlockSpec((tm, w), row) for w in widths],
        out_shape=[jax.ShapeDtypeStruct((n, w), F32) for w in widths],
        compiler_params=_cp("parallel"),
        name="proj",
    )(x2d, norm_w.reshape(1, D_MODEL), w_all, w_lo, cos_t, sin_t)


def _cmp_mlp(xc, kind, w1_ref, b1_ref, w2_ref, w1k_ref, w2k_ref, nck):
    if kind == 0:
        hh = _dot3_presplit(xc, w1k_ref)
    else:
        hh = jnp.dot(xc.astype(BF16), w1_ref[kind], preferred_element_type=F32)
    hs = []
    for g in range(N_KV_A):
        hg = hh[g * nck:(g + 1) * nck]
        hs.append(jax.nn.gelu(b1_ref[kind] + hg[:, :CMP_HIDDEN] + pltpu.roll(hg[:, CMP_HIDDEN:], nck - 1, 0)))
    act = jnp.concatenate(hs, axis=1)
    if kind == 0:
        return _dot3_presplit(act, w2k_ref)
    return jnp.dot(act.astype(BF16), w2_ref[kind], preferred_element_type=F32)


def _cmp_body(xk_ref, xv_ref, w1_ref, b1_ref, w2_ref, w1k_ref, w2k_ref, kc_ref, vc_ref, vct_ref, xc_ref, *, n_chunks):
    lane = lax.broadcasted_iota(jnp.int32, (n_chunks, LANE), 1)
    lo = lane < HEAD_DIM
    for pair in range(CMP_STRIDE // 2):
        for kind, x_ref in enumerate((xk_ref, xv_ref)):
            ak = x_ref[0, pl.ds(2 * pair, n_chunks, stride=CMP_STRIDE), :]
            bk = x_ref[0, pl.ds(2 * pair + 1, n_chunks, stride=CMP_STRIDE), :]
            xc_ref[kind, 0:n_chunks, LANE * pair:LANE * (pair + 1)] = jnp.where(lo, ak, pltpu.roll(bk, HEAD_DIM, 1))
            xc_ref[kind, n_chunks:2 * n_chunks, LANE * pair:LANE * (pair + 1)] = jnp.where(lo, pltpu.roll(ak, HEAD_DIM, 1), bk)
    outs = [_cmp_mlp(xc_ref[kind], kind, w1_ref, b1_ref, w2_ref, w1k_ref, w2k_ref, n_chunks) for kind in range(2)]
    kc_ref[0] = outs[0]
    vc_ref[0] = outs[1]
    vct_ref[0] = outs[1].T


def _nsa_compress(rows, w1cat, b1, w2bd, w1k, w2k):
    b, tk = rows.shape[0], rows.shape[1]
    n_chunks = tk // CMP_STRIDE
    fixed3 = lambda i: (0, 0, 0)
    fixed2 = lambda i: (0, 0)
    return pl.pallas_call(
        functools.partial(_cmp_body, n_chunks=n_chunks),
        grid=(b,),
        in_specs=[pl.BlockSpec((1, tk, LANE), lambda i: (i, 0, 0)), pl.BlockSpec((1, tk, LANE), lambda i: (i, 0, 1)),
                  pl.BlockSpec((2, CMP_STRIDE * HEAD_DIM, 2 * CMP_HIDDEN), fixed3),
                  pl.BlockSpec((2, 1, CMP_HIDDEN), fixed3),
                  pl.BlockSpec((2, 2 * CMP_HIDDEN, LANE), fixed3),
                  pl.BlockSpec((2, CMP_STRIDE * HEAD_DIM, 2 * CMP_HIDDEN), fixed3), pl.BlockSpec((2, 2 * CMP_HIDDEN, LANE), fixed3)],
        out_specs=[pl.BlockSpec((1, n_chunks, LANE), lambda i: (i, 0, 0)),
                   pl.BlockSpec((1, n_chunks, LANE), lambda i: (i, 0, 0)),
                   pl.BlockSpec((1, LANE, n_chunks), lambda i: (i, 0, 0))],
        out_shape=[jax.ShapeDtypeStruct((b, n_chunks, LANE), F32), jax.ShapeDtypeStruct((b, n_chunks, LANE), F32),
                   jax.ShapeDtypeStruct((b, LANE, n_chunks), F32)],
        scratch_shapes=[pltpu.VMEM((2, N_KV_A * n_chunks, CMP_STRIDE * HEAD_DIM), F32)],
        compiler_params=_cp("parallel"),
        name="nsa_compress",
    )(rows, rows, w1cat, b1, w2bd, w1k, w2k)


def _cmp_weights(cmp_w1, cmp_b1, cmp_w2):
    m = CMP_BLOCK // CMP_STRIDE
    w1r = cmp_w1.reshape(2, m, CMP_STRIDE * HEAD_DIM, CMP_HIDDEN)
    w1cat = jnp.concatenate([w1r[:, j] for j in range(m)], axis=-1)
    z = jnp.zeros_like(cmp_w2)
    w2bd = jnp.concatenate([jnp.concatenate([cmp_w2, z], -1), jnp.concatenate([z, cmp_w2], -1)], axis=1)
    split = lambda w: jnp.stack([w.astype(BF16), (w - w.astype(BF16).astype(F32)).astype(BF16)])
    return w1cat.astype(BF16), cmp_b1.reshape(2, 1, CMP_HIDDEN), w2bd.astype(BF16), split(w1cat[0]), split(w2bd[0])


def _cmp_to_sel_t(n_chunks, n_cmp, n_sel):
    starts = np.arange(n_chunks) * CMP_STRIDE
    sel_s = np.arange(n_sel) * SEL_BLOCK
    ov = np.minimum(starts[:, None] + CMP_BLOCK, sel_s[None] + SEL_BLOCK) - np.maximum(starts[:, None], sel_s[None])
    ov = np.clip(ov, 0, None) / CMP_BLOCK
    ov[n_cmp:] = 0.0
    return jnp.asarray(ov.T, dtype=F32)


def _masked_softmax_cols(s, mask):
    m = jnp.max(jnp.where(mask, s, NEG_INF), axis=0, keepdims=True)
    e = jnp.where(mask, jnp.exp(s - m), 0.0)
    den = jnp.sum(e, axis=0, keepdims=True)
    return e * jnp.where(den > 0.0, 1.0 / den, 0.0)


def _nsa_prompt_body(qt_ref, gt_ref, kc_ref, vct_ref, ovt_ref, ks_ref, vst_ref, kw_ref, vwt_ref, o_ref, sel_ref,
                     *, n_cmp, n_sel, qb_size):
    qb = pl.program_id(1)
    tq = qb_size
    n_chunks = kc_ref.shape[1]
    qpos = qb * tq + lax.broadcasted_iota(jnp.int32, (1, tq), 1)
    qpos2 = jnp.concatenate([qpos, qpos], axis=1)
    zeros_q = jnp.zeros((HEAD_DIM, 2 * tq), F32)
    gates = jax.nn.sigmoid(gt_ref[0])
    kc = kc_ref[0]
    n_idx = lax.broadcasted_iota(jnp.int32, (n_chunks, 2 * tq), 0)
    cmp_mask = (n_idx * CMP_STRIDE + (CMP_BLOCK - 1) <= qpos2) & (n_idx < n_cmp)
    blk = lax.broadcasted_iota(jnp.int32, (n_sel, tq), 0)
    cur = qpos // SEL_BLOCK
    forced = (blk == 0) | (blk == cur) | (blk == cur - 1)
    causal_blk = blk * SEL_BLOCK <= qpos
    qpads, o_cmps = [], []

    for g in range(N_KV_A):
        q64 = jnp.concatenate([qt_ref[0, (2 * g) * HEAD_DIM:(2 * g + 1) * HEAD_DIM, :],
                               qt_ref[0, (2 * g + 1) * HEAD_DIM:(2 * g + 2) * HEAD_DIM, :]], axis=1) * (HEAD_DIM ** -0.5)
        qpad32 = jnp.concatenate([q64, zeros_q], axis=0) if g == 0 else jnp.concatenate([zeros_q, q64], axis=0)
        qpad = qpad32.astype(BF16)
        qpads.append(qpad)

        p = _masked_softmax_cols(_dot3(kc, qpad32), cmp_mask)
        o_cmps.append(jnp.dot(vct_ref[0, g * HEAD_DIM:(g + 1) * HEAD_DIM, :].astype(BF16), p.astype(BF16),
                              preferred_element_type=F32))
        psum = p[:, :tq] + p[:, tq:]
        imp = jnp.dot(ovt_ref[...], psum, precision=HI, preferred_element_type=F32)
        imp = jnp.where(forced, FORCED_SCORE, jnp.where(causal_blk, imp, BLOCKED_SCORE))
        rank = jnp.zeros((n_sel, tq), F32)
        for i in range(n_sel):
            row = imp[i:i + 1, :]
            rank = rank + jnp.where((row > imp) | ((row == imp) & (blk > i)), 1.0, 0.0)
        sel_ref[g] = jnp.where(rank < float(min(TOP_K, n_sel)), 0.0, NEG_INF)

    q_all = jnp.concatenate(qpads, axis=1)
    qpos4 = jnp.concatenate([qpos2, qpos2], axis=1)

    def attend(j, carry, k_ref, vt_ref, use_sel, tk, causal=True):
        m, l, acc = carry
        off = pl.multiple_of(j * tk, tk)
        s = jnp.dot(k_ref[0, pl.ds(off, tk), :].astype(BF16), q_all, preferred_element_type=F32)
        if causal:
            diff = qpos4 - (off + lax.broadcasted_iota(jnp.int32, (tk, 4 * tq), 0))
        if use_sel:
            per_tile = tk // SEL_BLOCK
            biases = []
            for g in range(N_KV_A):
                rows = [jnp.broadcast_to(sel_ref[g, pl.ds(j * per_tile + a, 1), :], (SEL_BLOCK, tq)) for a in range(per_tile)]
                bias = jnp.concatenate(rows, axis=0)
                biases += [bias, bias]
            sm = s + jnp.concatenate(biases, axis=1)
            if causal:
                sm = jnp.where(diff >= 0, sm, NEG_INF)
            m_new = jnp.maximum(m, jnp.max(sm, axis=0, keepdims=True))
            e = jnp.exp(sm - m_new)
        else:
            mask = (diff >= 0) & (diff < WINDOW)
            m_new = jnp.maximum(m, jnp.max(jnp.where(mask, s, NEG_INF), axis=0, keepdims=True))
            e = jnp.where(mask, jnp.exp(s - m_new), 0.0)
        alpha = jnp.exp(m - m_new)
        l_new = alpha * l + jnp.sum(e, axis=0, keepdims=True)
        vt = vt_ref[0, :, pl.ds(off, tk)].astype(BF16)
        return m_new, l_new, alpha * acc + jnp.dot(vt, e.astype(BF16), preferred_element_type=F32)

    init = (jnp.full((1, 4 * tq), NEG_INF, F32), jnp.zeros((1, 4 * tq), F32), jnp.zeros((2 * HEAD_DIM, 4 * tq), F32))
    tk_s, tk_w = 4 * tq, 2 * tq
    last_s = (qb * tq) // tk_s
    slc = functools.partial(attend, k_ref=ks_ref, vt_ref=vst_ref, use_sel=True, tk=tk_s)
    _, l_s, acc_s = slc(last_s, lax.fori_loop(0, last_s, functools.partial(slc, causal=False), init))
    _, l_w, acc_w = lax.fori_loop(jnp.maximum(qb * tq - WINDOW, 0) // tk_w, (qb * tq + tk_w) // tk_w,
                                  functools.partial(attend, k_ref=kw_ref, vt_ref=vwt_ref, use_sel=False, tk=tk_w), init)

    for g in range(N_KV_A):
        blk_g = (slice(g * HEAD_DIM, (g + 1) * HEAD_DIM), slice(g * 2 * tq, (g + 1) * 2 * tq))
        o_slc = acc_s[blk_g] / l_s[:, blk_g[1]]
        o_win = acc_w[blk_g] / l_w[:, blk_g[1]]
        for r in range(R_A):
            h = 2 * g + r
            gr = gates[3 * h:3 * h + 3, :]
            sl = slice(r * tq, (r + 1) * tq)
            o_ref[0, h * HEAD_DIM:(h + 1) * HEAD_DIM, :] = (gr[0:1] * o_cmps[g][:, sl] + gr[1:2] * o_slc[:, sl]
                                                          + gr[2:3] * o_win[:, sl])


def _nsa_prompt(qt, gt, kc, vct, ovt, kv, vst, vwt, n_cmp):
    b, _, t = qt.shape
    tq = 128
    n_sel = t // SEL_BLOCK
    n_chunks = kc.shape[1]
    per_b = lambda i, j: (i, 0, 0)
    return pl.pallas_call(
        functools.partial(_nsa_prompt_body, n_cmp=n_cmp, n_sel=n_sel, qb_size=tq),
        grid=(b, t // tq),
        in_specs=[pl.BlockSpec((1, 256, tq), lambda i, j: (i, 0, j)),
                  pl.BlockSpec((1, 16, tq), lambda i, j: (i, 0, j)),
                  pl.BlockSpec((1, n_chunks, LANE), per_b),
                  pl.BlockSpec((1, LANE, n_chunks), per_b),
                  pl.BlockSpec((n_sel, n_chunks), lambda i, j: (0, 0)),
                  pl.BlockSpec((1, t, LANE), lambda i, j: (i, 0, 2)),
                  pl.BlockSpec((1, LANE, t), per_b),
                  pl.BlockSpec((1, t, LANE), lambda i, j: (i, 0, 4)),
                  pl.BlockSpec((1, LANE, t), per_b)],
        out_specs=pl.BlockSpec((1, 256, tq), lambda i, j: (i, 0, j)),
        out_shape=jax.ShapeDtypeStruct((b, 256, t), F32),
        scratch_shapes=[pltpu.VMEM((N_KV_A, n_sel, tq), F32)],
        compiler_params=_cp("parallel", "arbitrary"),
        name="nsa_prompt",
    )(qt, gt, kc, vct, ovt, kv, vst, kv, vwt)


def _nsa_prompt_mixer(q, kv, gate, lw, b, t):
    kv3 = kv.reshape(b, t, 6 * LANE)
    n_chunks = t // CMP_STRIDE
    n_cmp = (t - CMP_BLOCK) // CMP_STRIDE + 1
    kc, _, vct = _nsa_compress(kv3, *lw['cmp'])
    ovt = _cmp_to_sel_t(n_chunks, n_cmp, t // SEL_BLOCK)
    qt = jnp.swapaxes(q.reshape(b, t, 256), 1, 2)
    gt = jnp.swapaxes(gate.reshape(b, t, LANE)[:, :, :16], 1, 2)
    vst = jnp.swapaxes(kv3[:, :, 3 * LANE:4 * LANE], 1, 2)
    vwt = jnp.swapaxes(kv3[:, :, 5 * LANE:], 1, 2)
    ot = _nsa_prompt(qt, gt, kc, vct, ovt, kv3, vst, vwt, n_cmp)
    return jnp.swapaxes(ot, 1, 2).reshape(b * t, 256)


def _softmax_rows_with_extra(s, mask, s_new):
    m = jnp.maximum(jnp.max(jnp.where(mask, s, NEG_INF), axis=-1, keepdims=True), s_new)
    e = jnp.where(mask, jnp.exp(s - m), 0.0)
    e_new = jnp.exp(s_new - m)
    return e, e_new, 1.0 / (jnp.sum(e, axis=-1, keepdims=True) + e_new)


def _nsa_sample_body(pt_ref, *refs, n_pages, page, pos, n_sel, n_cmp, wb):
    del pt_ref
    n_in = n_pages
    pages = refs[:n_in]
    (qbd_ref, new_ref, gate_ref, win_ref, ov_ref, w1_ref, b1_ref, w2_ref, w1k_ref, w2k_ref,
     o_ref, nw_ref, tok_ref, xc_ref) = refs[n_in:]
    pg = lambda p, kind: pages[p].at[0, kind:kind + 1]
    nck = n_pages * page // CMP_STRIDE
    lane8 = lax.broadcasted_iota(jnp.int32, (8, LANE), 1)
    lo8 = lane8 < HEAD_DIM
    nt = lambda a, b: lax.dot_general(a, b, (((1,), (1,)), ((), ())), preferred_element_type=F32)

    lo_c = lax.broadcasted_iota(jnp.int32, (nck, LANE), 1) < HEAD_DIM
    for kind in range(2):
        for p in range(n_pages):
            tok_ref[kind, p * page:(p + 1) * page, :] = pg(p, kind)[0].T
        for pair in range(CMP_STRIDE // 2):
            a = tok_ref[kind, pl.ds(2 * pair, nck, stride=CMP_STRIDE), :]
            b = tok_ref[kind, pl.ds(2 * pair + 1, nck, stride=CMP_STRIDE), :]
            cols = slice(LANE * pair, LANE * (pair + 1))
            xc_ref[kind, 0:nck, cols] = jnp.where(lo_c, a, pltpu.roll(b, HEAD_DIM, 1))
            xc_ref[kind, nck:2 * nck, cols] = jnp.where(lo_c, pltpu.roll(a, HEAD_DIM, 1), b)
    kc, vc = [_cmp_mlp(xc_ref[kind], kind, w1_ref, b1_ref, w2_ref, w1k_ref, w2k_ref, nck) for kind in range(2)]

    q = qbd_ref[0] * (HEAD_DIM ** -0.5)
    qb = q.astype(BF16)
    new = new_ref[0]

    n_idx = lax.broadcasted_iota(jnp.int32, (8, nck), 1)
    cmask = (n_idx * CMP_STRIDE + (CMP_BLOCK - 1) <= pos) & (n_idx < n_cmp)
    s = _dot3(q, kc, _NT)
    m = jnp.max(jnp.where(cmask, s, NEG_INF), axis=-1, keepdims=True)
    e = jnp.where(cmask, jnp.exp(s - m), 0.0)
    den = jnp.sum(e, axis=-1, keepdims=True)
    p_cmp = e * jnp.where(den > 0.0, 1.0 / den, 0.0)
    o_cmp = jnp.dot(p_cmp.astype(BF16), vc.astype(BF16), preferred_element_type=F32)

    row8 = lax.broadcasted_iota(jnp.int32, (8, nck), 0)
    psum = jnp.where(row8 == 0, p_cmp[0:1] + p_cmp[1:2], jnp.where(row8 == 1, p_cmp[2:3] + p_cmp[3:4], 0.0))
    imp = jnp.dot(psum, ov_ref[...], precision=HI, preferred_element_type=F32)
    cur = pos // SEL_BLOCK
    forced = (lane8 == 0) | (lane8 == cur) | (lane8 == cur - 1)
    imp = jnp.where(forced, FORCED_SCORE, jnp.where(lane8 * SEL_BLOCK <= pos, imp, BLOCKED_SCORE))
    imp = jnp.where(lane8 < n_sel, imp, -3e38)
    rank = jnp.zeros((8, LANE), F32)
    for i in range(n_sel):
        col = imp[:, i:i + 1]
        rank = rank + jnp.where((col > imp) | ((col == imp) & (lane8 > i)), 1.0, 0.0)
    sel = jnp.where((rank < float(min(TOP_K, n_sel))) & (lane8 < n_sel), 1.0, 0.0)
    rsel = lax.broadcasted_iota(jnp.int32, (8, LANE), 0)
    selh = jnp.where(rsel < R_A, sel[0:1], jnp.where(rsel < 2 * R_A, sel[1:2], 0.0))

    per_page = page // SEL_BLOCK
    s_t, m_t = [], []
    for p in range(n_pages):
        s_t.append(jnp.dot(qb, pg(p, 2)[0].astype(BF16), preferred_element_type=F32))
        blk_sel = selh[:, per_page * p:per_page * p + 1]
        for a in range(1, per_page):
            blk_sel = jnp.where(lane8 < a * SEL_BLOCK, blk_sel, selh[:, per_page * p + a:per_page * p + a + 1])
        kpos = p * page + lane8
        m_t.append((blk_sel > 0.0) & (kpos <= pos))
    s_all = jnp.concatenate(s_t, axis=1)
    mk_all = jnp.concatenate(m_t, axis=1)
    s_new = jnp.sum(q * new[2:3], axis=-1, keepdims=True)
    e, e_new, inv = _softmax_rows_with_extra(s_all, mk_all, s_new)
    acc = e_new * new[3:4]
    for p in range(n_pages):
        acc = acc + nt(e[:, p * page:(p + 1) * page].astype(BF16), pg(p, 3)[0].astype(BF16))
    o_slc = acc * inv

    kw, vw = win_ref[0], win_ref[1]
    widx = lax.broadcasted_iota(jnp.int32, (8, wb), 1)
    diff = wb - widx
    s_w = jnp.dot(qb, kw.astype(BF16), preferred_element_type=F32)
    s_wnew = jnp.sum(q * new[4:5], axis=-1, keepdims=True)
    e, e_new, inv = _softmax_rows_with_extra(s_w, (diff >= 0) & (diff < WINDOW), s_wnew)
    o_win = (nt(e.astype(BF16), vw.astype(BF16)) + e_new * new[5:6]) * inv

    gts = jax.nn.sigmoid(gate_ref[0])
    o = gts[:, 0:1] * o_cmp + gts[:, 1:2] * o_slc + gts[:, 2:3] * o_win
    lo1 = lo8[0:1]
    o_ref[0] = jnp.concatenate([jnp.where(lo1, o[0:1], pltpu.roll(o[1:2], HEAD_DIM, 1)),
                                jnp.where(lo1, pltpu.roll(o[2:3], HEAD_DIM, 1), o[3:4])], axis=1)
    last = lax.broadcasted_iota(jnp.int32, (LANE, wb), 1) == wb - 1
    new_t = new.T
    nw_ref[0] = jnp.where(last, new_t[:, 4:5], pltpu.roll(kw, wb - 1, 1))
    nw_ref[1] = jnp.where(last, new_t[:, 5:6], pltpu.roll(vw, wb - 1, 1))


def _nsa_sample(q, kv, gate, cache_kv, cache_win, layer, page_table, cmp_w, pos):
    b = q.shape[0]
    n_pool, page = cache_kv.shape[1:3]
    n_pages = page_table.shape[1]
    wb = cache_win.shape[2]
    assert wb == WINDOW and page % CMP_STRIDE == 0 and page % SEL_BLOCK == 0 and page == LANE
    tk = n_pages * page + 1
    n_cmp = (tk - CMP_BLOCK) // CMP_STRIDE + 1
    nck = n_pages * page // CMP_STRIDE
    assert n_cmp <= nck
    n_sel = -(-tk // SEL_BLOCK)
    assert n_sel <= LANE and pos // SEL_BLOCK == n_sel - 1
    ov = _pad_to(_cmp_to_sel_t(nck, n_cmp, n_sel).T, LANE, 1)
    w1cat, b1, w2bd, w1k, w2k = cmp_w
    q4 = q.reshape(b, H_A, HEAD_DIM)
    z = jnp.zeros_like(q4)
    first = (jnp.arange(H_A) // R_A == 0)[None, :, None]
    qbd = jnp.concatenate([jnp.where(first, q4, z), jnp.where(first, z, q4)], axis=-1)
    qbd = _pad_to(qbd, 8, 1)
    new = _pad_to(kv.reshape(b, 6, LANE), 8, 1)
    g8 = _pad_to(_pad_to(gate[:, :3 * H_A].reshape(b, H_A, 3), LANE, 2), 8, 1)
    cache3 = jnp.transpose(cache_kv, (0, 1, 3, 4, 5, 2)).reshape(-1, 4, LANE, page)
    win3 = jnp.transpose(cache_win, (0, 1, 3, 4, 5, 2)).reshape(-1, LANE, wb)
    page_specs = [pl.BlockSpec((1, 4, LANE, page), functools.partial(lambda i, pt, p: (layer * n_pool + pt[i, p], 0, 0, 0), p=p))
                  for p in range(n_pages)]
    per_b = lambda i, pt: (i, 0, 0)
    fixed2 = lambda i, pt: (0, 0)
    fixed3 = lambda i, pt: (0, 0, 0)
    grid_spec = pltpu.PrefetchScalarGridSpec(
        num_scalar_prefetch=1,
        grid=(b,),
        in_specs=page_specs + [pl.BlockSpec((1, 8, LANE), per_b), pl.BlockSpec((1, 8, LANE), per_b), pl.BlockSpec((1, 8, LANE), per_b),
                               pl.BlockSpec((2, LANE, wb), lambda i, pt: (layer * b + i, 0, 0)), pl.BlockSpec((nck, LANE), fixed2),
                               pl.BlockSpec((2, CMP_STRIDE * HEAD_DIM, 2 * CMP_HIDDEN), fixed3),
                               pl.BlockSpec((2, 1, CMP_HIDDEN), fixed3), pl.BlockSpec((2, 2 * CMP_HIDDEN, LANE), fixed3),
                               pl.BlockSpec((2, CMP_STRIDE * HEAD_DIM, 2 * CMP_HIDDEN), fixed3),
                               pl.BlockSpec((2, 2 * CMP_HIDDEN, LANE), fixed3)],
        out_specs=[pl.BlockSpec((1, 1, 256), per_b), pl.BlockSpec((2, LANE, wb), per_b)],
        scratch_shapes=[pltpu.VMEM((2, n_pages * page, LANE), F32), pltpu.VMEM((2, 2 * nck, CMP_STRIDE * HEAD_DIM), F32)],
    )
    o, nw = pl.pallas_call(
        functools.partial(_nsa_sample_body, n_pages=n_pages, page=page, pos=pos, n_sel=n_sel, n_cmp=n_cmp, wb=wb),
        grid_spec=grid_spec,
        out_shape=[jax.ShapeDtypeStruct((b, 1, 256), F32), jax.ShapeDtypeStruct((2 * b, LANE, wb), F32)],
        compiler_params=_cp("parallel"),
        name="nsa_sample",
    )(page_table, *([cache3] * n_pages), qbd, new, g8, win3, ov, w1cat, b1, w2bd, w1k, w2k)
    return o.reshape(b, 256), jnp.transpose(nw.reshape(b, 2, N_KV_A, HEAD_DIM, wb), (0, 4, 1, 2, 3))


def _rwkv_prep_body(c_ref, s0_ref, mu_ref, vec_ref, wup_ref, aup_ref, gup_ref,
                    r_ref, lw_ref, k_ref, v_ref, kk_ref, ka_ref, g_ref, bonus_ref, carry_ref, *, tiles_per_seq):
    i = pl.program_id(0)
    cols = c_ref[...]
    tm = cols.shape[0]

    @pl.when(i % tiles_per_seq == 0)
    def _():
        carry_ref[...] = s0_ref[0]

    prev = pltpu.roll(cols, 1, 0)
    row0 = lax.broadcasted_iota(jnp.int32, cols.shape, 0) == 0
    prev = jnp.where(row0, carry_ref[...], prev)
    carry_ref[...] = cols[tm - 1:tm, :]
    _rwkv_features(cols, prev, mu_ref, vec_ref, wup_ref, aup_ref, gup_ref,
                   r_ref, lw_ref, k_ref, v_ref, kk_ref, ka_ref, g_ref, bonus_ref)


def _rwkv_step_prep_body(c_ref, s0_ref, mu_ref, vec_ref, wup_ref, aup_ref, gup_ref,
                         r_ref, lw_ref, k_ref, v_ref, kk_ref, ka_ref, g_ref, bonus_ref):
    _rwkv_features(c_ref[...], s0_ref[...], mu_ref, vec_ref, wup_ref, aup_ref, gup_ref,
                   r_ref, lw_ref, k_ref, v_ref, kk_ref, ka_ref, g_ref, bonus_ref)


def _rwkv_features(cols, prev, mu_ref, vec_ref, wup_ref, aup_ref, gup_ref,
                   r_ref, lw_ref, k_ref, v_ref, kk_ref, ka_ref, g_ref, bonus_ref):
    xs = cols + mu_ref[...] * (prev - cols)
    r, k, v, lo = xs[:, 0:256], xs[:, 256:512], xs[:, 512:768], xs[:, 768:896]
    vec = vec_ref[...]
    w0, a0, k_k, k_a, r_k = vec[0:1], vec[1:2], vec[2:3], vec[3:4], vec[4:5]
    w_log = -jax.nn.softplus(-(w0 + _dot3(jnp.tanh(lo), wup_ref[...]))) - 0.5
    a = jax.nn.sigmoid(a0 + _dot3(lo, aup_ref[...]))
    g_ref[...] = _dot3(jax.nn.sigmoid(lo), gup_ref[...])
    ones = _block_ones(256, HEAD_DIM, BF16)
    kk = k * k_k
    kk = kk * lax.rsqrt(_dot_bf16_exact_rhs(kk * kk, ones) + 1e-12)
    k2 = k * (1.0 + (a - 1.0) * k_a)
    r_ref[...] = r
    lw_ref[...] = -jnp.exp(w_log)
    k_ref[...] = k2
    v_ref[...] = v
    kk_ref[...] = kk
    ka_ref[...] = kk * a
    bonus_ref[...] = _dot_bf16_exact_rhs(r * k2 * r_k, ones) * v


def _rwkv_prep(colsb, shift0, lw, t, tm):
    n = colsb.shape[0]
    tiles_per_seq = t // tm
    row = lambda i: (i, 0)
    fixed = lambda i: (0, 0)
    outs = [jax.ShapeDtypeStruct((n, 256), F32)] * 8
    return pl.pallas_call(
        functools.partial(_rwkv_prep_body, tiles_per_seq=tiles_per_seq),
        grid=(n // tm,),
        in_specs=[pl.BlockSpec((tm, SHIFT_PAD), row),
                  pl.BlockSpec((1, 1, SHIFT_PAD), lambda i: (i // tiles_per_seq, 0, 0)),
                  pl.BlockSpec((1, SHIFT_PAD), fixed), pl.BlockSpec((8, 256), fixed),
                  pl.BlockSpec((LANE, 256), fixed), pl.BlockSpec((LANE, 256), fixed), pl.BlockSpec((LANE, 256), fixed)],
        out_specs=[pl.BlockSpec((tm, 256), row)] * 8,
        out_shape=outs,
        scratch_shapes=[pltpu.VMEM((1, SHIFT_PAD), F32)],
        compiler_params=_cp("arbitrary"),
        name="rwkv_prep",
    )(colsb, shift0.reshape(-1, 1, SHIFT_PAD), lw['rwkv_mu'], lw['rwkv_vec'], lw['rwkv_wup'], lw['rwkv_aup'], lw['rwkv_gup'])


RWKV_CHUNK = 64


def _split_bf16(x):
    hi = x.astype(BF16)
    return hi, (x - hi.astype(F32)).astype(BF16)


def _dot3(a, b, dims=(((1,), (0,)), ((), ()))):
    ah, al = _split_bf16(a)
    bh, bl = _split_bf16(b)
    dg = lambda x, y: lax.dot_general(x, y, dims, preferred_element_type=F32)
    return dg(ah, bh) + dg(ah, bl) + dg(al, bh)


def _dot3_presplit(x, w_ref):
    xh, xl = _split_bf16(x)
    return (jnp.dot(xh, w_ref[0], preferred_element_type=F32) + jnp.dot(xh, w_ref[1], preferred_element_type=F32)
            + jnp.dot(xl, w_ref[0], preferred_element_type=F32))


def _dot_bf16_exact_rhs(x, m_bf16):
    hi, lo = _split_bf16(x)
    return jnp.dot(hi, m_bf16, preferred_element_type=F32) + jnp.dot(lo, m_bf16, preferred_element_type=F32)


_NT = (((1,), (1,)), ((), ()))


def _rwkv_chunk_body(r_ref, lw_ref, k_ref, v_ref, kk_ref, ka_ref, g_ref, bonus_ref, s0_ref, vec_ref,
                     o_ref, st_ref, s_scr, y_scr, *, nb, tl):
    L = RWKV_CHUNK
    nc = tl // L

    @pl.when(pl.program_id(1) == 0)
    def _():
        s_scr[...] = s0_ref[...]

    ri = lax.broadcasted_iota(jnp.int32, (L, L), 0)
    ci = lax.broadcasted_iota(jnp.int32, (L, L), 1)
    strict, incl = ri > ci, ri >= ci
    ltri = incl.astype(F32)
    eye = (ri == ci).astype(F32)

    bnn = (((2,), (1,)), ((0,), (0,)))
    bnt = (((2,), (2,)), ((0,), (0,)))

    def chunk(c, carry):
        rows = pl.ds(pl.multiple_of(c * L, L), L)
        lhs_l, rhs_l, v_l, kw_l, wl_l = [], [], [], [], []
        for b in range(nb):
            r, lw, k, v, kk, ka = [ref[b, rows, :] for ref in (r_ref, lw_ref, k_ref, v_ref, kk_ref, ka_ref)]
            cl = jnp.dot(ltri, lw, precision=HI, preferred_element_type=F32)
            e_neg = jnp.exp(-cl)
            e_rem = jnp.exp(cl[L - 1:L, :] - cl)
            kkd, rd = kk * jnp.exp(cl - lw), r * jnp.exp(cl)
            kinv, kainv, kw, kaw = k * e_neg, ka * e_neg, k * e_rem, ka * e_rem
            w_last = jnp.exp(cl[L - 1:L, :])
            for h in range(H_B):
                sl = slice(h * HEAD_DIM, (h + 1) * HEAD_DIM)
                lhs_l.append(jnp.concatenate([kkd[:, sl], rd[:, sl]], axis=0))
                rhs_l.append(jnp.concatenate([kinv[:, sl], kainv[:, sl]], axis=0))
                v_l.append(v[:, sl])
                kw_l.append(jnp.concatenate([kw[:, sl], kaw[:, sl]], axis=0))
                wl_l.append(w_last[:, sl])
        lhs, rhs, vs, kws, wl = [jnp.stack(x) for x in (lhs_l, rhs_l, v_l, kw_l, wl_l)]
        gm = _dot3(lhs, rhs, bnt)
        a_vk = jnp.where(strict, gm[:, :L, :L], 0.0)
        n1 = jnp.where(strict, -gm[:, :L, L:], 0.0)
        t_inv, pw = eye + n1, n1
        for _ in range(L.bit_length() - 2):
            pw = _dot3(pw, pw, bnn)
            t_inv = _dot3(t_inv, eye + pw, bnn)
        s = s_scr[...].reshape(nb * H_B, HEAD_DIM, HEAD_DIM)
        xs = _dot3(lhs, s, bnt)
        u = _dot3(t_inv, xs[:, :L] + _dot3(a_vk, vs, bnn), bnn)
        b_vk = jnp.where(incl, gm[:, L:, :L], 0.0).astype(BF16)
        b_uk = jnp.where(incl, gm[:, L:, L:], 0.0).astype(BF16)
        y = (xs[:, L:] + lax.dot_general(b_vk, vs.astype(BF16), bnn, preferred_element_type=F32)
             - lax.dot_general(b_uk, u.astype(BF16), bnn, preferred_element_type=F32))
        vu_t = jnp.stack([jnp.concatenate([vs[n], -u[n]], axis=0).T for n in range(nb * H_B)])
        s_new = s * wl + _dot3(vu_t, kws, bnn)
        s_scr[...] = s_new.reshape(nb, H_B, HEAD_DIM, HEAD_DIM)
        for b in range(nb):
            for h in range(H_B):
                y_scr[b, rows, h * HEAD_DIM:(h + 1) * HEAD_DIM] = y[b * H_B + h]
        return carry

    lax.fori_loop(0, nc, chunk, 0)
    st_ref[...] = s_scr[...]
    vec = vec_ref[...]
    for b in range(nb):
        o_ref[b] = (_segment_norm(y_scr[b], RWKV_GN_EPS) * vec[5:6] + vec[6:7] + bonus_ref[b]) * g_ref[b]


def _rwkv_chunked(prep, s0, vec, b, t, nb, tl):
    arrs = [a.reshape(b, t, 256) for a in prep]
    seq = pl.BlockSpec((nb, tl, 256), lambda i, j: (i, j, 0))
    st = pl.BlockSpec((nb, H_B, HEAD_DIM, HEAD_DIM), lambda i, j: (i, 0, 0, 0))
    return pl.pallas_call(
        functools.partial(_rwkv_chunk_body, nb=nb, tl=tl),
        grid=(b // nb, t // tl),
        in_specs=[seq] * 8 + [st, pl.BlockSpec((8, 256), lambda i, j: (0, 0))],
        out_specs=[seq, st],
        out_shape=[jax.ShapeDtypeStruct((b, t, 256), F32), jax.ShapeDtypeStruct((b, H_B, HEAD_DIM, HEAD_DIM), F32)],
        scratch_shapes=[pltpu.VMEM((nb, H_B, HEAD_DIM, HEAD_DIM), F32), pltpu.VMEM((nb, tl, 256), F32)],
        compiler_params=_cp("parallel", "arbitrary"),
        name="rwkv_chunked",
    )(*arrs, s0, vec)


def _rwkv_mixer(colsb, shift0, s0, lw, b, t, tm, nb, tl):
    prep = _rwkv_prep(colsb, shift0, lw, t, tm)
    o, st = _rwkv_chunked(prep, s0, lw['rwkv_vec'], b, t, nb, tl)
    shift = colsb.reshape(b, t, SHIFT_PAD)[:, -1, :SHIFT_B]
    return o.reshape(b * t, 256), st, shift


def _segment_norm(y, eps):
    avg = _block_ones(256, HEAD_DIM, BF16) * (1.0 / HEAD_DIM)
    yc = y - _dot_bf16_exact_rhs(y, avg)
    return yc * lax.rsqrt(_dot_bf16_exact_rhs(yc * yc, avg) + eps)


def _rwkv_step_body(r_ref, lw_ref, k_ref, v_ref, kk_ref, ka_ref, g_ref, bonus_ref, vec_ref, s_ref, o_ref, st_ref, ft_scr, y_scr):
    h = pl.program_id(0)

    @pl.when(h == 0)
    def _():
        for n, ref in enumerate((r_ref, lw_ref, k_ref, v_ref, kk_ref, ka_ref)):
            ft_scr[n] = ref[...].T

    base = pl.multiple_of(h * HEAD_DIM, HEAD_DIM)
    head = lambda n: ft_scr[n, pl.ds(base, HEAD_DIM), :]
    r_t, w_t, k_t, kk_t, ka_t = head(0), jnp.exp(head(1)), head(2), head(4), head(5)

    def body(i, carry):
        rows = pl.ds(pl.multiple_of(i * HEAD_DIM, HEAD_DIM), HEAD_DIM)
        s = s_ref[rows, :]
        sk = jnp.sum(s * kk_t, axis=0, keepdims=True)
        s = s * w_t - sk * ka_t + ft_scr[3, pl.ds(base + i, 1), :] * k_t
        st_ref[rows, :] = s
        y_scr[pl.ds(base + i, 1), :] = jnp.sum(s * r_t, axis=0, keepdims=True)
        return carry

    lax.fori_loop(0, HEAD_DIM, body, 0)

    @pl.when(h == H_B - 1)
    def _():
        vec = vec_ref[...]
        o_ref[...] = (_segment_norm(y_scr[...].T, RWKV_GN_EPS) * vec[5:6] + vec[6:7] + bonus_ref[...]) * g_ref[...]


def _rwkv_step(colsb, shift0, s_all, layer, lw):
    b = colsb.shape[0]
    hd2 = HEAD_DIM * HEAD_DIM
    full = lambda *_: (0, 0)
    feat = pl.BlockSpec((b, 256), full)
    prep = pl.pallas_call(
        _rwkv_step_prep_body,
        grid=(1,),
        in_specs=[pl.BlockSpec((b, SHIFT_PAD), full), pl.BlockSpec((b, SHIFT_PAD), full), pl.BlockSpec((1, SHIFT_PAD), full),
                  pl.BlockSpec((8, 256), full)] + [pl.BlockSpec((LANE, 256), full)] * 3,
        out_specs=[feat] * 8,
        out_shape=[jax.ShapeDtypeStruct((b, 256), F32)] * 8,
        compiler_params=_cp("arbitrary"),
        name="rwkv_step_prep",
    )(colsb, shift0, lw['rwkv_mu'], lw['rwkv_vec'], lw['rwkv_wup'], lw['rwkv_aup'], lw['rwkv_gup'])
    s_rows = jnp.transpose(s_all, (0, 2, 3, 4, 1)).reshape(-1, b)
    o, st = pl.pallas_call(
        _rwkv_step_body,
        grid=(H_B,),
        in_specs=[feat] * 8 + [pl.BlockSpec((8, 256), full), pl.BlockSpec((hd2, b), lambda h: (layer * H_B + h, 0))],
        out_specs=[feat, pl.BlockSpec((hd2, b), lambda h: (h, 0))],
        out_shape=[jax.ShapeDtypeStruct((b, 256), F32), jax.ShapeDtypeStruct((H_B * hd2, b), F32)],
        scratch_shapes=[pltpu.VMEM((6, 256, b), F32), pltpu.VMEM((256, b), F32)],
        compiler_params=_cp("arbitrary"),
        name="rwkv_step",
    )(*prep, lw['rwkv_vec'], s_rows)
    return o, jnp.transpose(st.reshape(H_B, HEAD_DIM, HEAD_DIM, b), (3, 0, 1, 2))


def _ret_step_body(c_ref, cos_ref, sin_ref, gn_ref, r0_ref, o_ref, rt_ref, ft_scr, acc_scr):
    h = pl.program_id(0)

    @pl.when(h == 0)
    def _():
        x = c_ref[...]
        cs, sn = cos_ref[...], sin_ref[...]
        first = (lax.broadcasted_iota(jnp.int32, cs.shape, 1) % HEAD_DIM) < (HEAD_DIM // 2)

        def rope(z):
            sw = jnp.where(first, pltpu.roll(z, 256 - HEAD_DIM // 2, 1), pltpu.roll(z, HEAD_DIM // 2, 1))
            return z * cs + sw * sn

        ft_scr[0] = rope(x[:, 0:256]).T
        ft_scr[1] = (rope(x[:, 256:512]) * (HEAD_DIM ** -0.5)).T
        ft_scr[2] = x[:, 512:768].T

    base = pl.multiple_of(h * HEAD_DIM, HEAD_DIM)
    nseq = rt_ref.shape[1]
    gamma = jnp.exp(jnp.log1p(-jnp.exp2(-5.0 - jnp.full((1, nseq), h, jnp.int32).astype(F32))))
    q_t, k_t, v_t = [ft_scr[n, pl.ds(base, HEAD_DIM), :] for n in range(3)]
    qk = jnp.sum(q_t * k_t, axis=0, keepdims=True)

    def body(d, cross):
        rows = pl.ds(pl.multiple_of(d * HEAD_DIM, HEAD_DIM), HEAD_DIM)
        r_old = r0_ref[rows, :]
        rt_ref[rows, :] = gamma * r_old + ft_scr[1, pl.ds(base + d, 1), :] * v_t
        return cross + ft_scr[0, pl.ds(base + d, 1), :] * r_old

    cross = lax.fori_loop(0, HEAD_DIM, body, jnp.zeros((HEAD_DIM, nseq), F32))
    acc_scr[pl.ds(base, HEAD_DIM), :] = qk * v_t + gamma * cross

    @pl.when(h == H_D - 1)
    def _():
        o_ref[...] = jax.nn.silu(c_ref[:, 768:1024]) * (_segment_norm(acc_scr[...].T, RET_GN_EPS) * gn_ref[...])


def _ret_step(colsd, cos, sin, r_all, layer, gn):
    b = colsd.shape[0]
    hd2 = HEAD_DIM * HEAD_DIM
    full = lambda *_: (0, 0)
    r_rows = jnp.transpose(r_all, (0, 2, 3, 4, 1)).reshape(-1, b)
    o, rt = pl.pallas_call(
        _ret_step_body,
        grid=(H_D,),
        in_specs=[pl.BlockSpec((b, 1024), full), pl.BlockSpec((b, 256), full), pl.BlockSpec((b, 256), full),
                  pl.BlockSpec((1, 256), full), pl.BlockSpec((hd2, b), lambda h: (layer * H_D + h, 0))],
        out_specs=[pl.BlockSpec((b, 256), full), pl.BlockSpec((hd2, b), lambda h: (h, 0))],
        out_shape=[jax.ShapeDtypeStruct((b, 256), F32), jax.ShapeDtypeStruct((H_D * hd2, b), F32)],
        scratch_shapes=[pltpu.VMEM((3, 256, b), F32), pltpu.VMEM((256, b), F32)],
        compiler_params=_cp("arbitrary"),
        name="ret_step",
    )(colsd, cos, sin, gn.reshape(1, 256), r_rows)
    return o, jnp.transpose(rt.reshape(H_D, HEAD_DIM, HEAD_DIM, b), (3, 0, 1, 2))


def _s5_step_body(u_ref, x0_ref, a1_ref, a2_ref, bt_ref, ct_ref, d_ref, wg_ref, o_ref, xt_ref):
    u = u_ref[...]
    dot_hi = lambda a, b: jnp.dot(a, b, precision=HI, preferred_element_type=F32)
    x0 = x0_ref[...]
    n = x0.shape[0]
    even = (lax.broadcasted_iota(jnp.int32, x0.shape, 0) % 2) == 0
    partner = jnp.where(even, pltpu.roll(x0, n - 1, 0), pltpu.roll(x0, 1, 0))
    x = a1_ref[...] * x0 + a2_ref[...] * partner + dot_hi(bt_ref[...], u.T)
    xt_ref[...] = x
    y = dot_hi(ct_ref[...], x).T + d_ref[...] * u
    z = jax.nn.gelu(y)
    o_ref[...] = z * jax.nn.sigmoid(dot_hi(z, wg_ref[...]))


def _s5_step(u, x_all, layer, lw):
    b = u.shape[0]
    a1, a2, bt, ct, d_row = lw['s5_step']
    n = 2 * G_C * S5_P
    full = lambda *_: (0, 0)
    x_rows = jnp.transpose(x_all, (0, 2, 3, 4, 1)).reshape(-1, b)
    o, xt = pl.pallas_call(
        _s5_step_body,
        grid=(1,),
        in_specs=[pl.BlockSpec((b, C_C), full), pl.BlockSpec((n, b), lambda i: (layer, 0)), pl.BlockSpec((n, 1), full),
                  pl.BlockSpec((n, 1), full), pl.BlockSpec((n, C_C), full), pl.BlockSpec((C_C, n), full),
                  pl.BlockSpec((1, C_C), full), pl.BlockSpec((C_C, C_C), full)],
        out_specs=[pl.BlockSpec((b, C_C), full), pl.BlockSpec((n, b), full)],
        out_shape=[jax.ShapeDtypeStruct((b, C_C), F32), jax.ShapeDtypeStruct((n, b), F32)],
        compiler_params=_cp("arbitrary"),
        name="s5_step",
    )(u, x_rows, a1, a2, bt, ct, d_row, lw['s5_w_glu'])
    return o, jnp.transpose(xt.reshape(G_C, S5_P, 2, b), (3, 0, 1, 2))


def _s5_params(lw):
    lr, li = lw['s5_lambda_re'], lw['s5_lambda_im']
    dt = jnp.exp(lw['s5_log_step'])[:, None]
    mag = jnp.exp(lr * dt)
    ar, ai = mag * jnp.cos(li * dt), mag * jnp.sin(li * dt)
    nr, ni = ar - 1.0, ai
    den = lr * lr + li * li
    fr, fi = (nr * lr + ni * li) / den, (ni * lr - nr * li) / den
    b_re, b_im = lw['s5_b'][0], lw['s5_b'][1]
    bbr = fr[..., None] * b_re - fi[..., None] * b_im
    bbi = fr[..., None] * b_im + fi[..., None] * b_re
    eye = jnp.eye(G_C, dtype=F32)
    bd_in = lambda m: jnp.einsum('gpc,gh->gchp', m, eye).reshape(G_C * S5_CH, G_C * S5_P)
    bd_out = lambda m: jnp.einsum('gcp,gh->gphc', m, eye).reshape(G_C * S5_P, G_C * S5_CH)
    b_big = jnp.concatenate([bd_in(bbr), bd_in(bbi)], axis=1)
    c_big = jnp.concatenate([bd_out(lw['s5_c'][0]), -bd_out(lw['s5_c'][1])], axis=0)
    a_row = jnp.concatenate([ar.reshape(1, -1), ai.reshape(1, -1)], axis=1)
    d_row = lw['s5_d'].reshape(1, C_C)
    n = 2 * G_C * S5_P
    a1 = jnp.stack([ar, ar], axis=-1).reshape(n, 1)
    a2 = jnp.stack([-ai, ai], axis=-1).reshape(n, 1)
    bt = jnp.stack([jnp.einsum('gpc,gh->gphc', bbr, eye), jnp.einsum('gpc,gh->gphc', bbi, eye)], axis=2).reshape(n, G_C * S5_CH)
    ct = jnp.stack([jnp.einsum('gcp,gh->gchp', lw['s5_c'][0], eye), -jnp.einsum('gcp,gh->gchp', lw['s5_c'][1], eye)],
                   axis=-1).reshape(G_C * S5_CH, n)
    return (a_row, b_big, c_big, d_row), (a1, a2, bt, ct, d_row)


def _s5_body(u_ref, x0_ref, a_ref, b_ref, c_ref, d_ref, wg_ref, o_ref, xt_ref, x_scr, bu_scr, xs_scr, *, nb, tt, mm_dtype, prec):
    sub = 8

    @pl.when(pl.program_id(0) == 0)
    def _():
        x_scr[...] = jnp.zeros_like(x_scr)
        x_scr[0:nb, :] = x0_ref[...]
        bu_scr[...] = jnp.zeros_like(bu_scr)

    np_ = G_C * S5_P
    ncb = np_ // LANE
    for b in range(nb):
        bu = jnp.dot(u_ref[b].astype(mm_dtype), b_ref[...], precision=prec, preferred_element_type=F32)
        for cb in range(2 * ncb):
            bu_scr[cb, pl.ds(b, tt, stride=sub), :] = bu[:, cb * LANE:(cb + 1) * LANE]
    a = a_ref[...]

    def step(t, x):
        rows = pl.ds(pl.multiple_of(t * sub, sub), sub)
        new = [None] * (2 * ncb)
        for cb in range(ncb):
            re, im = slice(cb * LANE, (cb + 1) * LANE), slice(np_ + cb * LANE, np_ + (cb + 1) * LANE)
            ar, ai, xr, xi = a[:, re], a[:, im], x[:, re], x[:, im]
            new[cb] = ar * xr - ai * xi + bu_scr[cb, rows, :]
            new[ncb + cb] = ar * xi + ai * xr + bu_scr[ncb + cb, rows, :]
            xs_scr[cb, rows, :] = new[cb]
            xs_scr[ncb + cb, rows, :] = new[ncb + cb]
        return jnp.concatenate(new, axis=1)

    x_last = lax.fori_loop(0, tt, step, x_scr[...], unroll=8)
    x_scr[...] = x_last
    xt_ref[...] = x_last[0:nb, :]
    for b in range(nb):
        u = u_ref[b]
        xs = jnp.concatenate([xs_scr[cb, pl.ds(b, tt, stride=sub), :] for cb in range(2 * ncb)], axis=1)
        y = jnp.dot(xs.astype(mm_dtype), c_ref[...], precision=prec, preferred_element_type=F32) + d_ref[...] * u
        z = jax.nn.gelu(y)
        o_ref[b] = z * jax.nn.sigmoid(jnp.dot(z.astype(mm_dtype), wg_ref[...], precision=prec, preferred_element_type=F32))


def _s5_mixer(u, x0, lw, b, t, tt, exact):
    a_row, b_big, c_big, d_row = lw['s5']
    mm_dtype = F32 if exact else BF16
    prec = HI if exact else None
    x0l = jnp.concatenate([x0[..., 0].reshape(b, -1), x0[..., 1].reshape(b, -1)], axis=1)
    np2 = 2 * G_C * S5_P
    fixed = lambda i: (0, 0)
    o, xt = pl.pallas_call(
        functools.partial(_s5_body, nb=b, tt=tt, mm_dtype=mm_dtype, prec=prec),
        grid=(t // tt,),
        in_specs=[pl.BlockSpec((b, tt, C_C), lambda i: (0, i, 0)), pl.BlockSpec((b, np2), fixed),
                  pl.BlockSpec((1, np2), fixed), pl.BlockSpec((C_C, np2), fixed), pl.BlockSpec((np2, C_C), fixed),
                  pl.BlockSpec((1, C_C), fixed), pl.BlockSpec((C_C, C_C), fixed)],
        out_specs=[pl.BlockSpec((b, tt, C_C), lambda i: (0, i, 0)), pl.BlockSpec((b, np2), fixed)],
        out_shape=[jax.ShapeDtypeStruct((b, t, C_C), F32), jax.ShapeDtypeStruct((b, np2), F32)],
        scratch_shapes=[pltpu.VMEM((8, np2), F32), pltpu.VMEM((np2 // LANE, 8 * tt, LANE), F32),
                        pltpu.VMEM((np2 // LANE, 8 * tt, LANE), F32)],
        compiler_params=_cp("arbitrary"),
        name="s5",
    )(u.reshape(b, t, C_C), x0l, a_row, b_big.astype(mm_dtype), c_big.astype(mm_dtype), d_row, lw['s5_w_glu'].astype(mm_dtype))
    xt = xt.reshape(b, 2, G_C, S5_P)
    return o.reshape(b * t, C_C), jnp.stack([xt[:, 0], xt[:, 1]], axis=-1)


def _ret_tables(pos, c):
    cos, sin = _rope_tables(pos, HEAD_DIM, RET_THETA, HEAD_DIM, H_D)
    log_g = jnp.log1p(-jnp.exp2(-5.0 - jnp.arange(H_D, dtype=F32)))
    i = jnp.arange(c, dtype=F32)
    diff = i[:, None] - i[None, :]
    dmat = jnp.where(diff >= 0, jnp.exp(jnp.maximum(diff, 0.0)[None] * log_g[:, None, None]), 0.0).reshape(H_D * c, c)
    q_dec = jnp.repeat(jnp.exp((i + 1.0)[None] * log_g[:, None]).T, HEAD_DIM, axis=1)
    k_dec = jnp.repeat(jnp.exp((c - 1.0 - i)[None] * log_g[:, None]).T, HEAD_DIM, axis=1)
    chunk_dec = jnp.repeat(jnp.exp(c * log_g), HEAD_DIM).reshape(256, 1)
    return cos, sin, dmat, q_dec, k_dec, chunk_dec


def _ret_body(c_ref, cos_ref, sin_ref, dmat_ref, qdec_ref, kdec_ref, cdec_ref, r0_ref, gn_ref, o_ref, rt_ref, r_scr, *, c):
    @pl.when(pl.program_id(1) == 0)
    def _():
        r_scr[...] = r0_ref[0]

    x = c_ref[0]
    q, k, v, g = x[:, 0:256], x[:, 256:512], x[:, 512:768], x[:, 768:1024]
    cs, sn = cos_ref[...], sin_ref[...]
    lane = lax.broadcasted_iota(jnp.int32, (c, 256), 1)
    first = (lane % HEAD_DIM) < (HEAD_DIM // 2)

    def rope(z):
        sw = jnp.where(first, pltpu.roll(z, 256 - HEAD_DIM // 2, 1), pltpu.roll(z, HEAD_DIM // 2, 1))
        return z * cs + sw * sn

    q = rope(q)
    k = rope(k) * (HEAD_DIM ** -0.5)
    head = lane // HEAD_DIM
    kb, vb = k.astype(BF16), v.astype(BF16)
    qstack = jnp.concatenate([jnp.where(head == h, q, 0.0) for h in range(H_D)], axis=0).astype(BF16)
    s = lax.dot_general(qstack, kb, (((1,), (1,)), ((), ())), preferred_element_type=F32) * dmat_ref[...]
    pv = jnp.dot(s.astype(BF16), vb, preferred_element_type=F32)
    inner = jnp.zeros((c, 256), F32)
    for h in range(H_D):
        inner = inner + jnp.where(head == h, pv[h * c:(h + 1) * c], 0.0)
    r_old = r_scr[...]
    cross = jnp.dot((q * qdec_ref[...]).astype(BF16), r_old.astype(BF16), preferred_element_type=F32)
    kv = lax.dot_general((k * kdec_ref[...]).astype(BF16), vb, (((0,), (0,)), ((), ())), preferred_element_type=F32)
    bd = _block_ones(256, HEAD_DIM, F32)
    r_new = cdec_ref[...] * r_old + kv * bd
    r_scr[...] = r_new
    rt_ref[0] = r_new
    o_ref[0] = jax.nn.silu(g) * (_segment_norm(inner + cross, RET_GN_EPS) * gn_ref[...])


def _ret_mixer(colsd, r0, lw, tabs, b, t):
    c = RET_CHUNK if t % RET_CHUNK == 0 else t
    cos, sin, dmat, q_dec, k_dec, chunk_dec = tabs
    eye = jnp.eye(H_D, dtype=F32)
    r0l = jnp.einsum('bhde,hg->bhdge', r0, eye).reshape(b, 256, 256)
    n_t = t // c
    fixed = lambda i, j: (0, 0)
    o, rt = pl.pallas_call(
        functools.partial(_ret_body, c=c),
        grid=(b, n_t),
        in_specs=[pl.BlockSpec((1, c, 1024), lambda i, j: (i, j, 0)),
                  pl.BlockSpec((c, 256), lambda i, j: (j, 0)), pl.BlockSpec((c, 256), lambda i, j: (j, 0)),
                  pl.BlockSpec((H_D * c, c), fixed), pl.BlockSpec((c, 256), fixed), pl.BlockSpec((c, 256), fixed),
                  pl.BlockSpec((256, 1), fixed), pl.BlockSpec((1, 256, 256), lambda i, j: (i, 0, 0)),
                  pl.BlockSpec((1, 256), fixed)],
        out_specs=[pl.BlockSpec((1, c, 256), lambda i, j: (i, j, 0)), pl.BlockSpec((1, 256, 256), lambda i, j: (i, 0, 0))],
        out_shape=[jax.ShapeDtypeStruct((b, t, 256), F32), jax.ShapeDtypeStruct((b, 256, 256), F32)],
        scratch_shapes=[pltpu.VMEM((256, 256), F32)],
        compiler_params=_cp("parallel", "arbitrary"),
        name="retention",
    )(colsd.reshape(b, t, 1024), cos, sin, dmat, q_dec, k_dec, chunk_dec, r0l, lw['ret_gn'].reshape(1, 256))
    rt = jnp.einsum('bhdge,hg->bhde', rt.reshape(b, H_D, HEAD_DIM, H_D, HEAD_DIM), eye)
    return o.reshape(b * t, 256), rt


def _partner(x, d, period):
    pos = lax.broadcasted_iota(jnp.int32, x.shape, 1) % period
    return jnp.where(pos + d < period, pltpu.roll(x, LANE - d, 1), pltpu.roll(x, period - d, 1))


def _out_body(x_ref, oa_ref, ob_ref, oc_ref, od_ref, w_ref, nw_ref, wr_ref, br_ref, x1_ref, h_ref, comb_ref):
    acc = x_ref[...]
    for i, ref in enumerate((oa_ref, ob_ref, oc_ref, od_ref)):
        acc = acc + jnp.dot(ref[...].astype(BF16), w_ref[256 * i:256 * (i + 1), :], preferred_element_type=F32)
    x1_ref[...] = acc
    h = acc * lax.rsqrt(jnp.mean(acc * acc, axis=-1, keepdims=True) + RMS_EPS) * nw_ref[...]
    hb = h.astype(BF16)
    h_ref[...] = hb
    h_lo = (h - hb.astype(F32)).astype(BF16)
    logits = (jnp.dot(hb, wr_ref[0], preferred_element_type=F32) + jnp.dot(hb, wr_ref[1], preferred_element_type=F32)
              + jnp.dot(h_lo, wr_ref[0], preferred_element_type=F32) + br_ref[...])
    le, lg = logits[:, :LANE], logits[:, LANE:]
    lane = lax.broadcasted_iota(jnp.int32, le.shape, 1)
    mg = jnp.max(lg, axis=-1, keepdims=True)
    eg = jnp.exp(lg - mg)
    pg = eg / (jnp.sum(eg, axis=-1, keepdims=True) * (1.0 / 32.0))
    gidx = (lane % N_EXPERTS) // EXP_PER_GROUP
    g_rank = jnp.zeros_like(pg)
    for d in range(1, N_GROUPS):
        other = pltpu.roll(pg, LANE - EXP_PER_GROUP * d, 1)
        wrapped = gidx + d >= N_GROUPS
        g_rank = g_rank + jnp.where((other > pg) | ((other == pg) & wrapped), 1.0, 0.0)
    kidx = lane % EXP_PER_GROUP
    others = [_partner(le, d, EXP_PER_GROUP) for d in range(1, EXP_PER_GROUP)]
    me = functools.reduce(jnp.maximum, others, le)
    ee = jnp.exp(le - me)
    se = ee
    for d in range(1, EXP_PER_GROUP):
        se = se + _partner(ee, d, EXP_PER_GROUP)
    pe = ee / se
    e_rank = jnp.zeros_like(pe)
    for d in range(1, EXP_PER_GROUP):
        other = _partner(pe, d, EXP_PER_GROUP)
        wrapped = kidx + d >= EXP_PER_GROUP
        e_rank = e_rank + jnp.where((other > pe) | ((other == pe) & wrapped), 1.0, 0.0)
    top = jnp.where(e_rank < 2.0, pe, 0.0)
    den = top
    for d in range(1, EXP_PER_GROUP):
        den = den + _partner(top, d, EXP_PER_GROUP)
    comb = jnp.where((g_rank < 1.0) & (lane < N_EXPERTS), pg * (top / den), 0.0)
    comb_ref[...] = comb


def _out_router(x, oa, ob, oc, od, lw, tm):
    n = x.shape[0]
    row = lambda i: (i, 0)
    fixed = lambda i: (0, 0)
    mix = pl.BlockSpec((tm, 256), row)
    return pl.pallas_call(
        _out_body,
        grid=(n // tm,),
        in_specs=[pl.BlockSpec((tm, D_MODEL), row), mix, mix, mix, mix,
                  pl.BlockSpec((D_MODEL, D_MODEL), fixed), pl.BlockSpec((1, D_MODEL), fixed),
                  pl.BlockSpec((2, D_MODEL, 2 * LANE), lambda i: (0, 0, 0)), pl.BlockSpec((1, 2 * LANE), fixed)],
        out_specs=[pl.BlockSpec((tm, D_MODEL), row), pl.BlockSpec((tm, D_MODEL), row), pl.BlockSpec((tm, LANE), row)],
        out_shape=[jax.ShapeDtypeStruct((n, D_MODEL), F32), jax.ShapeDtypeStruct((n, D_MODEL), BF16),
                   jax.ShapeDtypeStruct((n, LANE), F32)],
        compiler_params=_cp("parallel"),
        name="out_router",
    )(x, oa, ob, oc, od, lw['w_out'], lw['norm_ffn'], lw['w_router'], lw['b_router'])


def _router_weights(w_grp, b_grp, w_exp, b_exp):
    we = jnp.transpose(w_exp, (1, 0, 2)).reshape(D_MODEL, N_EXPERTS)
    wg = jnp.repeat(w_grp, EXP_PER_GROUP, axis=1)
    reps = LANE // N_EXPERTS
    w = jnp.concatenate([jnp.tile(we, (1, reps)), jnp.tile(wg, (1, reps))], axis=1)
    b = jnp.concatenate([jnp.tile(b_exp.reshape(1, N_EXPERTS), (1, reps)),
                         jnp.tile(jnp.repeat(b_grp, EXP_PER_GROUP).reshape(1, N_EXPERTS), (1, reps))], axis=1)
    w_hi = w.astype(BF16)
    return jnp.stack([w_hi, (w - w_hi.astype(F32)).astype(BF16)]), b


def _moe_body(h_ref, comb_ref, x1_ref, wg_ref, wu_ref, wd_ref, nf_ref, *out_refs, final):
    acc_ref = out_refs[-1]
    e = pl.program_id(1)

    @pl.when(e == 0)
    def _():
        acc_ref[...] = x1_ref[...]

    h = h_ref[...]
    comb = comb_ref[...]
    lane = lax.broadcasted_iota(jnp.int32, comb.shape, 1)
    c = jnp.sum(jnp.where(lane == e, comb, 0.0), axis=-1, keepdims=True)
    hg = jnp.dot(h, wg_ref[0], preferred_element_type=F32)
    hu = jnp.dot(h, wu_ref[0], preferred_element_type=F32)
    act = (jax.nn.silu(hg) * hu * c).astype(BF16)
    acc_ref[...] += jnp.dot(act, wd_ref[0], preferred_element_type=F32)

    @pl.when(e == N_EXPERTS - 1)
    def _():
        x2 = acc_ref[...]
        if final:
            out_refs[0][...] = x2 * lax.rsqrt(jnp.mean(x2 * x2, axis=-1, keepdims=True) + RMS_EPS) * nf_ref[...]
        else:
            out_refs[0][...] = x2


def _moe(h, comb, x1, lw, norm_final, tm, final):
    n = x1.shape[0]
    row = lambda i, e: (i, 0)
    per_e = lambda i, e: (e, 0, 0)
    return pl.pallas_call(
        functools.partial(_moe_body, final=final),
        grid=(n // tm, N_EXPERTS),
        in_specs=[pl.BlockSpec((tm, D_MODEL), row), pl.BlockSpec((tm, LANE), row), pl.BlockSpec((tm, D_MODEL), row),
                  pl.BlockSpec((1, D_MODEL, D_EXPERT), per_e), pl.BlockSpec((1, D_MODEL, D_EXPERT), per_e),
                  pl.BlockSpec((1, D_EXPERT, D_MODEL), per_e), pl.BlockSpec((1, D_MODEL), lambda i, e: (0, 0))],
        out_specs=pl.BlockSpec((tm, D_MODEL), row),
        out_shape=jax.ShapeDtypeStruct((n, D_MODEL), F32),
        scratch_shapes=[pltpu.VMEM((tm, D_MODEL), F32)],
        compiler_params=_cp("parallel", "arbitrary"),
        name="moe",
    )(h, comb, x1, lw['moe_wg'], lw['moe_wu'], lw['moe_wd'], norm_final.reshape(1, D_MODEL))


def _prep_layer(l, p):
    w_in = p['w_in'][l]
    o = _offsets(SPLIT_SIZES)
    segs = jnp.split(w_in, o, axis=1)
    w_all = jnp.concatenate([segs[0], segs[1], _pad_to(segs[2], LANE, 1), _pad_to(segs[3], SHIFT_PAD, 1), segs[4], segs[5]],
                            axis=1).astype(BF16)
    w_prec = w_in[:, :PRECISE_COLS]
    w_lo = (w_prec - w_prec.astype(BF16).astype(F32)).astype(BF16)
    lw = {'layer': l, 'w_all': w_all, 'w_lo': w_lo, 'norm_mix': p['norm_mix'][l]}
    lw['cmp'] = _cmp_weights(p['nsa_cmp_w1'][l], p['nsa_cmp_b1'][l], p['nsa_cmp_w2'][l])
    lw['rwkv_mu'] = _pad_to(p['rwkv_mu'][l].reshape(1, SHIFT_B), SHIFT_PAD, 1)
    lw['rwkv_vec'] = _pad_to(p['rwkv_vec'][l], 8, 0)
    z = lambda a, b: jnp.zeros((a, b), F32)
    lw['rwkv_wup'] = jnp.concatenate([p['rwkv_w_up'][l], z(LANE - LORA_W, C_B)], axis=0)
    lw['rwkv_aup'] = jnp.concatenate([z(LORA_W, C_B), p['rwkv_a_up'][l], z(LANE - LORA_W - LORA_A, C_B)], axis=0)
    lw['rwkv_gup'] = jnp.concatenate([z(LORA_W + LORA_A, C_B), p['rwkv_g_up'][l], z(LANE - LORA_W - LORA_A - LORA_G, C_B)], axis=0)
    for name in ('s5_lambda_re', 's5_lambda_im', 's5_b', 's5_c', 's5_d', 's5_log_step', 's5_w_glu', 'ret_gn'):
        lw[name] = p[name][l]
    lw['s5'], lw['s5_step'] = _s5_params(lw)
    lw['w_out'] = p['w_out'][l].astype(BF16)
    lw['norm_ffn'] = p['norm_ffn'][l].reshape(1, D_MODEL)
    lw['w_router'], lw['b_router'] = _router_weights(p['moe_w_grp'][l], p['moe_b_grp'][l], p['moe_w_exp'][l], p['moe_b_exp'][l])
    lw['moe_wg'] = p['moe_w_gate'][l].astype(BF16)
    lw['moe_wu'] = p['moe_w_up'][l].astype(BF16)
    lw['moe_wd'] = p['moe_w_down'][l].astype(BF16)
    return lw


ROW_TILE = 512
MOE_ROW_TILE = 1024
SCAN_TILE = 256


def _prompt_layer(x, lw, tabs, b, t, norm_final):
    cos_a, sin_a, ret_tabs = tabs
    q, kv, gate, colsb, u, colsd = _proj(x, lw['norm_mix'], lw['w_all'], lw['w_lo'], cos_a, sin_a, ROW_TILE)
    o_a = _nsa_prompt_mixer(q, kv, gate, lw, b, t)
    o_b, s_rwkv, s_shift = _rwkv_mixer(colsb, jnp.zeros((b, SHIFT_PAD), F32), jnp.zeros((b, H_B, HEAD_DIM, HEAD_DIM), F32),
                                       lw, b, t, ROW_TILE, b, SCAN_TILE)
    o_c, s_s5 = _s5_mixer(u, jnp.zeros((b, G_C, S5_P, 2), F32), lw, b, t, SCAN_TILE, False)
    o_d, s_ret = _ret_mixer(colsd, jnp.zeros((b, H_D, HEAD_DIM, HEAD_DIM), F32), lw, ret_tabs, b, t)
    x1, h, comb = _out_router(x, o_a, o_b, o_c, o_d, lw, ROW_TILE)
    x2 = _moe(h, comb, x1, lw, norm_final, MOE_ROW_TILE, lw['layer'] == DEPTH - 1)
    kv3 = kv.reshape(b, t, 6 * LANE)
    rows = kv3[:, :, :4 * LANE].reshape(b, t, 4, N_KV_A, HEAD_DIM)
    win = kv3[:, t - min(WINDOW, t):, 4 * LANE:].reshape(b, min(WINDOW, t), 2, N_KV_A, HEAD_DIM)
    return x2, (rows, win, s_rwkv, s_shift, s_s5, s_ret)


def _sample_layer(x, lw, tabs, b, pos, cache_kv, page_table, win_buf, s_rwkv, s_shift, s_s5, s_ret, norm_final):
    cos_a, sin_a, ret_cs = tabs
    q, kv, gate, colsb, u, colsd = _proj(x, lw['norm_mix'], lw['w_all'], lw['w_lo'], cos_a, sin_a, b)
    o_a, win = _nsa_sample(q, kv, gate, cache_kv, win_buf, lw['layer'], page_table, lw['cmp'], int(pos[0]))
    rows = kv[:, :4 * LANE].reshape(b, 1, 4, N_KV_A, HEAD_DIM)
    o_b, s_rwkv = _rwkv_step(colsb, _pad_to(s_shift, SHIFT_PAD, 1), s_rwkv, lw['layer'], lw)
    s_shift = colsb[:, :SHIFT_B]
    o_c, s_s5 = _s5_step(u, s_s5, lw['layer'], lw)
    o_d, s_ret = _ret_step(colsd, ret_cs[0], ret_cs[1], s_ret, lw['layer'], lw['ret_gn'])
    x1, h, comb = _out_router(x, o_a, o_b, o_c, o_d, lw, b)
    x2 = _moe(h, comb, x1, lw, norm_final, b, lw['layer'] == DEPTH - 1)
    return x2, (rows, win, s_rwkv, s_shift, s_s5, s_ret)


def kernel(x_prompt, x_sample, cache_nsa_kv, cache_nsa_win, state_rwkv, state_rwkv_shift, state_s5, state_ret, page_table, norm_mix, w_in, nsa_cmp_w1, nsa_cmp_b1, nsa_cmp_w2, rwkv_mu, rwkv_vec, rwkv_w_up, rwkv_a_up, rwkv_g_up, s5_lambda_re, s5_lambda_im, s5_b, s5_c, s5_d, s5_log_step, s5_w_glu, ret_gn, w_out, norm_ffn, moe_w_grp, moe_b_grp, moe_w_exp, moe_b_exp, moe_w_gate, moe_w_up, moe_w_down, norm_final):
    p = dict(norm_mix=norm_mix, w_in=w_in, nsa_cmp_w1=nsa_cmp_w1, nsa_cmp_b1=nsa_cmp_b1, nsa_cmp_w2=nsa_cmp_w2,
             rwkv_mu=rwkv_mu, rwkv_vec=rwkv_vec, rwkv_w_up=rwkv_w_up, rwkv_a_up=rwkv_a_up, rwkv_g_up=rwkv_g_up,
             s5_lambda_re=s5_lambda_re, s5_lambda_im=s5_lambda_im, s5_b=s5_b, s5_c=s5_c, s5_d=s5_d,
             s5_log_step=s5_log_step, s5_w_glu=s5_w_glu, ret_gn=ret_gn, w_out=w_out, norm_ffn=norm_ffn,
             moe_w_grp=moe_w_grp, moe_b_grp=moe_b_grp, moe_w_exp=moe_w_exp, moe_b_exp=moe_b_exp,
             moe_w_gate=moe_w_gate, moe_w_up=moe_w_up, moe_w_down=moe_w_down)
    bp, tp = x_prompt.shape[:2]
    bs, ts = x_sample.shape[:2]
    assert ts == 1 and tp % ROW_TILE == 0 and (bp * tp) % MOE_ROW_TILE == 0 and tp % (4 * LANE) == 0 and bs % 8 == 0
    past_len = page_table.shape[1] * cache_nsa_kv.shape[2]
    pos_p = np.arange(tp)
    pos_s = past_len + np.arange(ts)
    c = RET_CHUNK if tp % RET_CHUNK == 0 else tp
    tabs_p = _rope_tables(pos_p, ROT_DIM, ROPE_THETA, HEAD_DIM, 2) + (_ret_tables(pos_p, c),)
    pos_rows = np.repeat(pos_s, bs)
    tabs_s = _rope_tables(pos_rows, ROT_DIM, ROPE_THETA, HEAD_DIM, 2) + (_rope_tables(pos_rows, HEAD_DIM, RET_THETA, HEAD_DIM, H_D),)
    xp = x_prompt.reshape(bp * tp, D_MODEL)
    xs = x_sample.reshape(bs * ts, D_MODEL)
    sts_p, sts_s = [], []
    for l in range(DEPTH):
        lw = _prep_layer(l, p)
        xp, st_p = _prompt_layer(xp, lw, tabs_p, bp, tp, norm_final)
        xs, st_s = _sample_layer(xs, lw, tabs_s, bs, pos_s, cache_nsa_kv, page_table, cache_nsa_win, state_rwkv,
                                     state_rwkv_shift[l], state_s5, state_ret, norm_final)
        rows, win, s1, s2, s3, s4 = st_s
        sts_s.append((rows, win, s1, s2, s3, s4))
        sts_p.append(st_p)
    new_p = [jnp.stack([st[i] for st in sts_p]) for i in range(6)]
    new_s = [jnp.stack([st[i] for st in sts_s]) for i in range(6)]
    return (xp.reshape(bp, tp, D_MODEL), xs.reshape(bs, ts, D_MODEL), new_p[0], new_s[0], new_p[1], new_s[1],
            new_p[2], new_s[2], new_p[3], new_s[3], new_p[4], new_s[4], new_p[5], new_s[5])
```

```python
import functools

import numpy as np
import jax
import jax.numpy as jnp
from jax import lax
from jax.experimental import pallas as pl
from jax.experimental.pallas import tpu as pltpu

F32 = jnp.float32
BF16 = jnp.bfloat16
HI = lax.Precision.HIGHEST

D_MODEL = 1024
DEPTH = 2
HEAD_DIM = 64
C_A = C_B = C_C = C_D = 256
H_A = 4
N_KV_A = 2
R_A = 2
ROT_DIM = 16
ROPE_THETA = 500000.0
CMP_BLOCK = 32
CMP_STRIDE = 16
CMP_HIDDEN = 128
SEL_BLOCK = 64
TOP_K = 16
WINDOW = 512
NEG_INF = -1e30
FORCED_SCORE = 1e9
BLOCKED_SCORE = -1e9
H_B = 4
LORA_W = 16
LORA_A = 16
LORA_G = 32
SHIFT_B = 832
SHIFT_PAD = 896
RWKV_GN_EPS = 64e-5
S5_CH = 16
G_C = 16
S5_P = 64
H_D = 4
RET_CHUNK = 128
RET_THETA = 10000.0
RET_GN_EPS = 1e-5
N_GROUPS = 4
EXP_PER_GROUP = 4
N_EXPERTS = 16
D_EXPERT = 256
RMS_EPS = 1e-6
SPLIT_SIZES = (C_A, 6 * N_KV_A * HEAD_DIM, 3 * H_A, SHIFT_B, C_C, 4 * C_D)
LANE = 128
VMEM_LIMIT = 56 * 1024 * 1024


def _cp(*sem):
    return pltpu.CompilerParams(dimension_semantics=sem, vmem_limit_bytes=VMEM_LIMIT)


def _offsets(sizes):
    return [int(s) for s in np.cumsum(sizes)[:-1]]


def _pad_to(a, n, axis):
    pad = [(0, 0)] * a.ndim
    pad[axis] = (0, n - a.shape[axis])
    return jnp.pad(a, pad)


def _block_ones(n, blk, dtype):
    r = lax.broadcasted_iota(jnp.int32, (n, n), 0) // blk
    c = lax.broadcasted_iota(jnp.int32, (n, n), 1) // blk
    return (r == c).astype(dtype)


def _rope_tables(pos, rot_dim, theta, period, reps):
    half = rot_dim // 2
    inv = theta ** (-jnp.arange(half, dtype=F32) / half)
    ang = jnp.asarray(pos, F32)[:, None] * inv[None, :]
    cos, sin = jnp.cos(ang), jnp.sin(ang)
    n = ang.shape[0]
    rest = period - rot_dim
    c = jnp.concatenate([cos, cos, jnp.ones((n, rest), F32)], -1)
    s = jnp.concatenate([-sin, sin, jnp.zeros((n, rest), F32)], -1)
    return jnp.tile(c, (1, reps)), jnp.tile(s, (1, reps))


PRECISE_COLS = 384


def _proj_body(x_ref, nw_ref, w_ref, wlo_ref, cos_ref, sin_ref, q_ref, kv_ref, g_ref, cb_ref, u_ref, cd_ref):
    x = x_ref[...]
    h = x * lax.rsqrt(jnp.mean(x * x, axis=-1, keepdims=True) + RMS_EPS) * nw_ref[...]
    hb = h.astype(BF16)
    h_lo = (h - hb.astype(F32)).astype(BF16)
    c = cos_ref[...]
    s = sin_ref[...]
    first = (lax.broadcasted_iota(jnp.int32, c.shape, 1) % HEAD_DIM) < (ROT_DIM // 2)

    def rope(z):
        sw = jnp.where(first, pltpu.roll(z, LANE - ROT_DIM // 2, 1), pltpu.roll(z, ROT_DIM // 2, 1))
        return z * c + sw * s

    def dot(a, b):
        z = jnp.dot(hb, w_ref[:, a:b], preferred_element_type=F32)
        if b <= PRECISE_COLS:
            z = z + (jnp.dot(hb, wlo_ref[:, a:b], preferred_element_type=F32)
                     + jnp.dot(h_lo, w_ref[:, a:b], preferred_element_type=F32))
        return z

    for j in range(2):
        q_ref[:, LANE * j:LANE * (j + 1)] = rope(dot(LANE * j, LANE * (j + 1)))
    for j in range(6):
        z = dot(256 + LANE * j, 256 + LANE * (j + 1))
        kv_ref[:, LANE * j:LANE * (j + 1)] = rope(z) if j % 2 == 0 else z
    g_ref[...] = dot(1024, 1152)
    cb_ref[...] = dot(1152, 2048)
    u_ref[...] = dot(2048, 2304)
    cd_ref[...] = dot(2304, 3328)


def _proj(x2d, norm_w, w_all, w_lo, cos_t, sin_t, tm):
    n = x2d.shape[0]
    t_tiles = cos_t.shape[0] // tm
    row = lambda i: (i, 0)
    fixed = lambda i: (0, 0)
    tab = lambda i: (i % t_tiles, 0)
    widths = (256, 768, 128, SHIFT_PAD, 256, 1024)
    return pl.pallas_call(
        _proj_body,
        grid=(n // tm,),
        in_specs=[pl.BlockSpec((tm, D_MODEL), row), pl.BlockSpec((1, D_MODEL), fixed),
                  pl.BlockSpec((D_MODEL, 3328), fixed), pl.BlockSpec((D_MODEL, PRECISE_COLS), fixed),
                  pl.BlockSpec((tm, LANE), tab), pl.BlockSpec((tm, LANE), tab)],
        out_specs=[pl.BlockSpec((tm, w), row) for w in widths],
        out_shape=[jax.ShapeDtypeStruct((n, w), F32) for w in widths],
        compiler_params=_cp("parallel"),
        name="proj",
    )(x2d, norm_w.reshape(1, D_MODEL), w_all, w_lo, cos_t, sin_t)


def _cmp_mlp(xc, kind, w1_ref, b1_ref, w2_ref, w1k_ref, w2k_ref, nck):
    if kind == 0:
        hh = _dot3_presplit(xc, w1k_ref)
    else:
        hh = jnp.dot(xc.astype(BF16), w1_ref[kind], preferred_element_type=F32)
    hs = []
    for g in range(N_KV_A):
        hg = hh[g * nck:(g + 1) * nck]
        hs.append(jax.nn.gelu(b1_ref[kind] + hg[:, :CMP_HIDDEN] + pltpu.roll(hg[:, CMP_HIDDEN:], nck - 1, 0)))
    act = jnp.concatenate(hs, axis=1)
    if kind == 0:
        return _dot3_presplit(act, w2k_ref)
    return jnp.dot(act.astype(BF16), w2_ref[kind], preferred_element_type=F32)


def _cmp_body(xk_ref, xv_ref, w1_ref, b1_ref, w2_ref, w1k_ref, w2k_ref, kc_ref, vc_ref, vct_ref, xc_ref, *, n_chunks):
    lane = lax.broadcasted_iota(jnp.int32, (n_chunks, LANE), 1)
    lo = lane < HEAD_DIM
    for pair in range(CMP_STRIDE // 2):
        for kind, x_ref in enumerate((xk_ref, xv_ref)):
            ak = x_ref[0, pl.ds(2 * pair, n_chunks, stride=CMP_STRIDE), :]
            bk = x_ref[0, pl.ds(2 * pair + 1, n_chunks, stride=CMP_STRIDE), :]
            xc_ref[kind, 0:n_chunks, LANE * pair:LANE * (pair + 1)] = jnp.where(lo, ak, pltpu.roll(bk, HEAD_DIM, 1))
            xc_ref[kind, n_chunks:2 * n_chunks, LANE * pair:LANE * (pair + 1)] = jnp.where(lo, pltpu.roll(ak, HEAD_DIM, 1), bk)
    outs = [_cmp_mlp(xc_ref[kind], kind, w1_ref, b1_ref, w2_ref, w1k_ref, w2k_ref, n_chunks) for kind in range(2)]
    kc_ref[0] = outs[0]
    vc_ref[0] = outs[1]
    vct_ref[0] = outs[1].T


def _nsa_compress(rows, w1cat, b1, w2bd, w1k, w2k):
    b, tk = rows.shape[0], rows.shape[1]
    n_chunks = tk // CMP_STRIDE
    fixed3 = lambda i: (0, 0, 0)
    fixed2 = lambda i: (0, 0)
    return pl.pallas_call(
        functools.partial(_cmp_body, n_chunks=n_chunks),
        grid=(b,),
        in_specs=[pl.BlockSpec((1, tk, LANE), lambda i: (i, 0, 0)), pl.BlockSpec((1, tk, LANE), lambda i: (i, 0, 1)),
                  pl.BlockSpec((2, CMP_STRIDE * HEAD_DIM, 2 * CMP_HIDDEN), fixed3),
                  pl.BlockSpec((2, 1, CMP_HIDDEN), fixed3),
                  pl.BlockSpec((2, 2 * CMP_HIDDEN, LANE), fixed3),
                  pl.BlockSpec((2, CMP_STRIDE * HEAD_DIM, 2 * CMP_HIDDEN), fixed3), pl.BlockSpec((2, 2 * CMP_HIDDEN, LANE), fixed3)],
        out_specs=[pl.BlockSpec((1, n_chunks, LANE), lambda i: (i, 0, 0)),
                   pl.BlockSpec((1, n_chunks, LANE), lambda i: (i, 0, 0)),
                   pl.BlockSpec((1, LANE, n_chunks), lambda i: (i, 0, 0))],
        out_shape=[jax.ShapeDtypeStruct((b, n_chunks, LANE), F32), jax.ShapeDtypeStruct((b, n_chunks, LANE), F32),
                   jax.ShapeDtypeStruct((b, LANE, n_chunks), F32)],
        scratch_shapes=[pltpu.VMEM((2, N_KV_A * n_chunks, CMP_STRIDE * HEAD_DIM), F32)],
        compiler_params=_cp("parallel"),
        name="nsa_compress",
    )(rows, rows, w1cat, b1, w2bd, w1k, w2k)


def _cmp_weights(cmp_w1, cmp_b1, cmp_w2):
    m = CMP_BLOCK // CMP_STRIDE
    w1r = cmp_w1.reshape(2, m, CMP_STRIDE * HEAD_DIM, CMP_HIDDEN)
    w1cat = jnp.concatenate([w1r[:, j] for j in range(m)], axis=-1)
    z = jnp.zeros_like(cmp_w2)
    w2bd = jnp.concatenate([jnp.concatenate([cmp_w2, z], -1), jnp.concatenate([z, cmp_w2], -1)], axis=1)
    split = lambda w: jnp.stack([w.astype(BF16), (w - w.astype(BF16).astype(F32)).astype(BF16)])
    return w1cat.astype(BF16), cmp_b1.reshape(2, 1, CMP_HIDDEN), w2bd.astype(BF16), split(w1cat[0]), split(w2bd[0])


def _cmp_to_sel_t(n_chunks, n_cmp, n_sel):
    starts = np.arange(n_chunks) * CMP_STRIDE
    sel_s = np.arange(n_sel) * SEL_BLOCK
    ov = np.minimum(starts[:, None] + CMP_BLOCK, sel_s[None] + SEL_BLOCK) - np.maximum(starts[:, None], sel_s[None])
    ov = np.clip(ov, 0, None) / CMP_BLOCK
    ov[n_cmp:] = 0.0
    return jnp.asarray(ov.T, dtype=F32)


def _masked_softmax_cols(s, mask):
    m = jnp.max(jnp.where(mask, s, NEG_INF), axis=0, keepdims=True)
    e = jnp.where(mask, jnp.exp(s - m), 0.0)
    den = jnp.sum(e, axis=0, keepdims=True)
    return e * jnp.where(den > 0.0, 1.0 / den, 0.0)


def _nsa_prompt_body(qt_ref, gt_ref, kc_ref, vct_ref, ovt_ref, ks_ref, vst_ref, kw_ref, vwt_ref, o_ref, sel_ref,
                     *, n_cmp, n_sel, qb_size):
    qb = pl.program_id(1)
    tq = qb_size
    n_chunks = kc_ref.shape[1]
    qpos = qb * tq + lax.broadcasted_iota(jnp.int32, (1, tq), 1)
    qpos2 = jnp.concatenate([qpos, qpos], axis=1)
    zeros_q = jnp.zeros((HEAD_DIM, 2 * tq), F32)
    gates = jax.nn.sigmoid(gt_ref[0])
    kc = kc_ref[0]
    qpos4 = jnp.concatenate([qpos2, qpos2], axis=1)
    n_idx = lax.broadcasted_iota(jnp.int32, (n_chunks, 4 * tq), 0)
    cmp_mask = (n_idx * CMP_STRIDE + (CMP_BLOCK - 1) <= qpos4) & (n_idx < n_cmp)
    blk = lax.broadcasted_iota(jnp.int32, (n_sel, tq), 0)
    cur = qpos // SEL_BLOCK
    forced = (blk == 0) | (blk == cur) | (blk == cur - 1)
    causal_blk = blk * SEL_BLOCK <= qpos

    qpads32 = []
    for g in range(N_KV_A):
        q64 = jnp.concatenate([qt_ref[0, (2 * g) * HEAD_DIM:(2 * g + 1) * HEAD_DIM, :],
                               qt_ref[0, (2 * g + 1) * HEAD_DIM:(2 * g + 2) * HEAD_DIM, :]], axis=1) * (HEAD_DIM ** -0.5)
        qpads32.append(jnp.concatenate([q64, zeros_q], axis=0) if g == 0 else jnp.concatenate([zeros_q, q64], axis=0))
    q_all32 = jnp.concatenate(qpads32, axis=1)
    q_all = q_all32.astype(BF16)

    p_all = _masked_softmax_cols(_dot3(kc, q_all32), cmp_mask)
    o_cmp_all = jnp.dot(vct_ref[0].astype(BF16), p_all.astype(BF16), preferred_element_type=F32)
    o_cmps = [o_cmp_all[g * HEAD_DIM:(g + 1) * HEAD_DIM, g * 2 * tq:(g + 1) * 2 * tq] for g in range(N_KV_A)]

    for g in range(N_KV_A):
        psum = p_all[:, g * 2 * tq:g * 2 * tq + tq] + p_all[:, g * 2 * tq + tq:(g + 1) * 2 * tq]
        p1 = psum.astype(BF16)
        p2, p3 = _split_bf16(psum - p1.astype(F32))
        ovb = ovt_ref[...].astype(BF16)
        imp = (jnp.dot(ovb, p1, preferred_element_type=F32) + jnp.dot(ovb, p2, preferred_element_type=F32)
               + jnp.dot(ovb, p3, preferred_element_type=F32))
        imp = jnp.where(forced, FORCED_SCORE, jnp.where(causal_blk, imp, BLOCKED_SCORE))
        ngrp = n_sel // 8
        imp_g = [imp[8 * k:8 * k + 8, :] for k in range(ngrp)]
        rank_g = [jnp.zeros((8, tq), F32) for _ in range(ngrp)]
        for i in range(n_sel):
            row = imp[i:i + 1, :]
            for k in range(ngrp):
                if k > i // 8:
                    beats = row >= imp_g[k]
                elif k < i // 8:
                    beats = row > imp_g[k]
                else:
                    beats = (row > imp_g[k]) | ((row == imp_g[k]) & (blk[8 * k:8 * k + 8, :] > i))
                rank_g[k] = rank_g[k] + jnp.where(beats, 1.0, 0.0)
        rank = jnp.concatenate(rank_g, axis=0)
        sel_ref[g] = jnp.where(rank < float(min(TOP_K, n_sel)), 0.0, NEG_INF)


    def attend(j, carry, k_ref, vt_ref, use_sel, tk, causal=True):
        m, l, acc = carry
        off = pl.multiple_of(j * tk, tk)
        s = jnp.dot(k_ref[0, pl.ds(off, tk), :].astype(BF16), q_all, preferred_element_type=F32)
        if causal:
            diff = qpos4 - (off + lax.broadcasted_iota(jnp.int32, (tk, 4 * tq), 0))
        if use_sel:
            per_tile = tk // SEL_BLOCK
            biases = []
            for g in range(N_KV_A):
                rows = [jnp.broadcast_to(sel_ref[g, pl.ds(j * per_tile + a, 1), :], (SEL_BLOCK, tq)) for a in range(per_tile)]
                bias = jnp.concatenate(rows, axis=0)
                biases += [bias, bias]
            sm = s + jnp.concatenate(biases, axis=1)
            if causal:
                sm = jnp.where(diff >= 0, sm, NEG_INF)
            m_new = jnp.maximum(m, jnp.max(sm, axis=0, keepdims=True))
            e = jnp.exp(sm - m_new)
        else:
            mask = (diff >= 0) & (diff < WINDOW)
            m_new = jnp.maximum(m, jnp.max(jnp.where(mask, s, NEG_INF), axis=0, keepdims=True))
            e = jnp.where(mask, jnp.exp(s - m_new), 0.0)
        alpha = jnp.exp(m - m_new)
        l_new = alpha * l + jnp.sum(e, axis=0, keepdims=True)
        vt = vt_ref[0, :, pl.ds(off, tk)].astype(BF16)
        return m_new, l_new, alpha * acc + jnp.dot(vt, e.astype(BF16), preferred_element_type=F32)

    init = (jnp.full((1, 4 * tq), NEG_INF, F32), jnp.zeros((1, 4 * tq), F32), jnp.zeros((2 * HEAD_DIM, 4 * tq), F32))
    tk_s, tk_w = 4 * tq, 2 * tq
    last_s = (qb * tq) // tk_s
    slc = functools.partial(attend, k_ref=ks_ref, vt_ref=vst_ref, use_sel=True, tk=tk_s)
    _, l_s, acc_s = slc(last_s, lax.fori_loop(0, last_s, functools.partial(slc, causal=False), init))
    _, l_w, acc_w = lax.fori_loop(jnp.maximum(qb * tq - WINDOW, 0) // tk_w, (qb * tq + tk_w) // tk_w,
                                  functools.partial(attend, k_ref=kw_ref, vt_ref=vwt_ref, use_sel=False, tk=tk_w), init)

    for g in range(N_KV_A):
        blk_g = (slice(g * HEAD_DIM, (g + 1) * HEAD_DIM), slice(g * 2 * tq, (g + 1) * 2 * tq))
        o_slc = acc_s[blk_g] / l_s[:, blk_g[1]]
        o_win = acc_w[blk_g] / l_w[:, blk_g[1]]
        for r in range(R_A):
            h = 2 * g + r
            gr = gates[3 * h:3 * h + 3, :]
            sl = slice(r * tq, (r + 1) * tq)
            o_ref[0, h * HEAD_DIM:(h + 1) * HEAD_DIM, :] = (gr[0:1] * o_cmps[g][:, sl] + gr[1:2] * o_slc[:, sl]
                                                          + gr[2:3] * o_win[:, sl])


def _nsa_prompt(qt, gt, kc, vct, ovt, kv, vst, vwt, n_cmp):
    b, _, t = qt.shape
    tq = 128
    n_sel = t // SEL_BLOCK
    n_chunks = kc.shape[1]
    per_b = lambda i, j: (i, 0, 0)
    return pl.pallas_call(
        functools.partial(_nsa_prompt_body, n_cmp=n_cmp, n_sel=n_sel, qb_size=tq),
        grid=(b, t // tq),
        in_specs=[pl.BlockSpec((1, 256, tq), lambda i, j: (i, 0, j)),
                  pl.BlockSpec((1, 16, tq), lambda i, j: (i, 0, j)),
                  pl.BlockSpec((1, n_chunks, LANE), per_b),
                  pl.BlockSpec((1, LANE, n_chunks), per_b),
                  pl.BlockSpec((n_sel, n_chunks), lambda i, j: (0, 0)),
                  pl.BlockSpec((1, t, LANE), lambda i, j: (i, 0, 2)),
                  pl.BlockSpec((1, LANE, t), per_b),
                  pl.BlockSpec((1, t, LANE), lambda i, j: (i, 0, 4)),
                  pl.BlockSpec((1, LANE, t), per_b)],
        out_specs=pl.BlockSpec((1, 256, tq), lambda i, j: (i, 0, j)),
        out_shape=jax.ShapeDtypeStruct((b, 256, t), F32),
        scratch_shapes=[pltpu.VMEM((N_KV_A, n_sel, tq), F32)],
        compiler_params=_cp("parallel", "arbitrary"),
        name="nsa_prompt",
    )(qt, gt, kc, vct, ovt, kv, vst, kv, vwt)


def _nsa_prompt_mixer(q, kv, gate, lw, b, t):
    kv3 = kv.reshape(b, t, 6 * LANE)
    n_chunks = t // CMP_STRIDE
    n_cmp = (t - CMP_BLOCK) // CMP_STRIDE + 1
    kc, _, vct = _nsa_compress(kv3, *lw['cmp'])
    ovt = _cmp_to_sel_t(n_chunks, n_cmp, t // SEL_BLOCK)
    qt = jnp.swapaxes(q.reshape(b, t, 256), 1, 2)
    gt = jnp.swapaxes(gate.reshape(b, t, LANE)[:, :, :16], 1, 2)
    vst = jnp.swapaxes(kv3[:, :, 3 * LANE:4 * LANE], 1, 2)
    vwt = jnp.swapaxes(kv3[:, :, 5 * LANE:], 1, 2)
    ot = _nsa_prompt(qt, gt, kc, vct, ovt, kv3, vst, vwt, n_cmp)
    return jnp.swapaxes(ot, 1, 2).reshape(b * t, 256)


def _softmax_rows_with_extra(s, mask, s_new):
    m = jnp.maximum(jnp.max(jnp.where(mask, s, NEG_INF), axis=-1, keepdims=True), s_new)
    e = jnp.where(mask, jnp.exp(s - m), 0.0)
    e_new = jnp.exp(s_new - m)
    return e, e_new, 1.0 / (jnp.sum(e, axis=-1, keepdims=True) + e_new)


def _nsa_sample_body(pt_ref, *refs, n_pages, page, pos, n_sel, n_cmp, wb):
    del pt_ref
    n_in = n_pages
    pages = refs[:n_in]
    (qbd_ref, new_ref, gate_ref, win_ref, ov_ref, w1_ref, b1_ref, w2_ref, w1k_ref, w2k_ref,
     o_ref, nw_ref, tok_ref, xc_ref) = refs[n_in:]
    pg = lambda p, kind: pages[p].at[0, kind:kind + 1]
    nck = n_pages * page // CMP_STRIDE
    lane8 = lax.broadcasted_iota(jnp.int32, (8, LANE), 1)
    lo8 = lane8 < HEAD_DIM
    nt = lambda a, b: lax.dot_general(a, b, (((1,), (1,)), ((), ())), preferred_element_type=F32)

    lo_c = lax.broadcasted_iota(jnp.int32, (nck, LANE), 1) < HEAD_DIM
    for kind in range(2):
        for p in range(n_pages):
            tok_ref[kind, p * page:(p + 1) * page, :] = pg(p, kind)[0].T
        for pair in range(CMP_STRIDE // 2):
            a = tok_ref[kind, pl.ds(2 * pair, nck, stride=CMP_STRIDE), :]
            b = tok_ref[kind, pl.ds(2 * pair + 1, nck, stride=CMP_STRIDE), :]
            cols = slice(LANE * pair, LANE * (pair + 1))
            xc_ref[kind, 0:nck, cols] = jnp.where(lo_c, a, pltpu.roll(b, HEAD_DIM, 1))
            xc_ref[kind, nck:2 * nck, cols] = jnp.where(lo_c, pltpu.roll(a, HEAD_DIM, 1), b)
    kc, vc = [_cmp_mlp(xc_ref[kind], kind, w1_ref, b1_ref, w2_ref, w1k_ref, w2k_ref, nck) for kind in range(2)]

    q = qbd_ref[0] * (HEAD_DIM ** -0.5)
    qb = q.astype(BF16)
    new = new_ref[0]

    n_idx = lax.broadcasted_iota(jnp.int32, (8, nck), 1)
    cmask = (n_idx * CMP_STRIDE + (CMP_BLOCK - 1) <= pos) & (n_idx < n_cmp)
    s = _dot3(q, kc, _NT)
    m = jnp.max(jnp.where(cmask, s, NEG_INF), axis=-1, keepdims=True)
    e = jnp.where(cmask, jnp.exp(s - m), 0.0)
    den = jnp.sum(e, axis=-1, keepdims=True)
    p_cmp = e * jnp.where(den > 0.0, 1.0 / den, 0.0)
    o_cmp = jnp.dot(p_cmp.astype(BF16), vc.astype(BF16), preferred_element_type=F32)

    row8 = lax.broadcasted_iota(jnp.int32, (8, nck), 0)
    psum = jnp.where(row8 == 0, p_cmp[0:1] + p_cmp[1:2], jnp.where(row8 == 1, p_cmp[2:3] + p_cmp[3:4], 0.0))
    imp = jnp.dot(psum, ov_ref[...], precision=HI, preferred_element_type=F32)
    cur = pos // SEL_BLOCK
    forced = (lane8 == 0) | (lane8 == cur) | (lane8 == cur - 1)
    imp = jnp.where(forced, FORCED_SCORE, jnp.where(lane8 * SEL_BLOCK <= pos, imp, BLOCKED_SCORE))
    imp = jnp.where(lane8 < n_sel, imp, -3e38)
    rank = jnp.zeros((8, LANE), F32)
    for i in range(n_sel):
        col = imp[:, i:i + 1]
        rank = rank + jnp.where((col > imp) | ((col == imp) & (lane8 > i)), 1.0, 0.0)
    sel = jnp.where((rank < float(min(TOP_K, n_sel))) & (lane8 < n_sel), 1.0, 0.0)
    rsel = lax.broadcasted_iota(jnp.int32, (8, LANE), 0)
    selh = jnp.where(rsel < R_A, sel[0:1], jnp.where(rsel < 2 * R_A, sel[1:2], 0.0))

    per_page = page // SEL_BLOCK
    s_t, m_t = [], []
    for p in range(n_pages):
        s_t.append(jnp.dot(qb, pg(p, 2)[0].astype(BF16), preferred_element_type=F32))
        blk_sel = selh[:, per_page * p:per_page * p + 1]
        for a in range(1, per_page):
            blk_sel = jnp.where(lane8 < a * SEL_BLOCK, blk_sel, selh[:, per_page * p + a:per_page * p + a + 1])
        kpos = p * page + lane8
        m_t.append((blk_sel > 0.0) & (kpos <= pos))
    s_all = jnp.concatenate(s_t, axis=1)
    mk_all = jnp.concatenate(m_t, axis=1)
    s_new = jnp.sum(q * new[2:3], axis=-1, keepdims=True)
    e, e_new, inv = _softmax_rows_with_extra(s_all, mk_all, s_new)
    acc = e_new * new[3:4]
    for p in range(n_pages):
        acc = acc + nt(e[:, p * page:(p + 1) * page].astype(BF16), pg(p, 3)[0].astype(BF16))
    o_slc = acc * inv

    kw, vw = win_ref[0], win_ref[1]
    widx = lax.broadcasted_iota(jnp.int32, (8, wb), 1)
    diff = wb - widx
    s_w = jnp.dot(qb, kw.astype(BF16), preferred_element_type=F32)
    s_wnew = jnp.sum(q * new[4:5], axis=-1, keepdims=True)
    e, e_new, inv = _softmax_rows_with_extra(s_w, (diff >= 0) & (diff < WINDOW), s_wnew)
    o_win = (nt(e.astype(BF16), vw.astype(BF16)) + e_new * new[5:6]) * inv

    gts = jax.nn.sigmoid(gate_ref[0])
    o = gts[:, 0:1] * o_cmp + gts[:, 1:2] * o_slc + gts[:, 2:3] * o_win
    lo1 = lo8[0:1]
    o_ref[0] = jnp.concatenate([jnp.where(lo1, o[0:1], pltpu.roll(o[1:2], HEAD_DIM, 1)),
                                jnp.where(lo1, pltpu.roll(o[2:3], HEAD_DIM, 1), o[3:4])], axis=1)
    last = lax.broadcasted_iota(jnp.int32, (LANE, wb), 1) == wb - 1
    new_t = new.T
    nw_ref[0] = jnp.where(last, new_t[:, 4:5], pltpu.roll(kw, wb - 1, 1))
    nw_ref[1] = jnp.where(last, new_t[:, 5:6], pltpu.roll(vw, wb - 1, 1))


def _nsa_sample(q, kv, gate, cache_kv, cache_win, layer, page_table, cmp_w, pos):
    b = q.shape[0]
    n_pool, page = cache_kv.shape[1:3]
    n_pages = page_table.shape[1]
    wb = cache_win.shape[2]
    assert wb == WINDOW and page % CMP_STRIDE == 0 and page % SEL_BLOCK == 0 and page == LANE
    tk = n_pages * page + 1
    n_cmp = (tk - CMP_BLOCK) // CMP_STRIDE + 1
    nck = n_pages * page // CMP_STRIDE
    assert n_cmp <= nck
    n_sel = -(-tk // SEL_BLOCK)
    assert n_sel <= LANE and pos // SEL_BLOCK == n_sel - 1
    ov = _pad_to(_cmp_to_sel_t(nck, n_cmp, n_sel).T, LANE, 1)
    w1cat, b1, w2bd, w1k, w2k = cmp_w
    q4 = q.reshape(b, H_A, HEAD_DIM)
    z = jnp.zeros_like(q4)
    first = (jnp.arange(H_A) // R_A == 0)[None, :, None]
    qbd = jnp.concatenate([jnp.where(first, q4, z), jnp.where(first, z, q4)], axis=-1)
    qbd = _pad_to(qbd, 8, 1)
    new = _pad_to(kv.reshape(b, 6, LANE), 8, 1)
    g8 = _pad_to(_pad_to(gate[:, :3 * H_A].reshape(b, H_A, 3), LANE, 2), 8, 1)
    cache3 = jnp.transpose(cache_kv, (0, 1, 3, 4, 5, 2)).reshape(-1, 4, LANE, page)
    win3 = jnp.transpose(cache_win, (0, 1, 3, 4, 5, 2)).reshape(-1, LANE, wb)
    page_specs = [pl.BlockSpec((1, 4, LANE, page), functools.partial(lambda i, pt, p: (layer * n_pool + pt[i, p], 0, 0, 0), p=p))
                  for p in range(n_pages)]
    per_b = lambda i, pt: (i, 0, 0)
    fixed2 = lambda i, pt: (0, 0)
    fixed3 = lambda i, pt: (0, 0, 0)
    grid_spec = pltpu.PrefetchScalarGridSpec(
        num_scalar_prefetch=1,
        grid=(b,),
        in_specs=page_specs + [pl.BlockSpec((1, 8, LANE), per_b), pl.BlockSpec((1, 8, LANE), per_b), pl.BlockSpec((1, 8, LANE), per_b),
                               pl.BlockSpec((2, LANE, wb), lambda i, pt: (layer * b + i, 0, 0)), pl.BlockSpec((nck, LANE), fixed2),
                               pl.BlockSpec((2, CMP_STRIDE * HEAD_DIM, 2 * CMP_HIDDEN), fixed3),
                               pl.BlockSpec((2, 1, CMP_HIDDEN), fixed3), pl.BlockSpec((2, 2 * CMP_HIDDEN, LANE), fixed3),
                               pl.BlockSpec((2, CMP_STRIDE * HEAD_DIM, 2 * CMP_HIDDEN), fixed3),
                               pl.BlockSpec((2, 2 * CMP_HIDDEN, LANE), fixed3)],
        out_specs=[pl.BlockSpec((1, 1, 256), per_b), pl.BlockSpec((2, LANE, wb), per_b)],
        scratch_shapes=[pltpu.VMEM((2, n_pages * page, LANE), F32), pltpu.VMEM((2, 2 * nck, CMP_STRIDE * HEAD_DIM), F32)],
    )
    o, nw = pl.pallas_call(
        functools.partial(_nsa_sample_body, n_pages=n_pages, page=page, pos=pos, n_sel=n_sel, n_cmp=n_cmp, wb=wb),
        grid_spec=grid_spec,
        out_shape=[jax.ShapeDtypeStruct((b, 1, 256), F32), jax.ShapeDtypeStruct((2 * b, LANE, wb), F32)],
        compiler_params=_cp("parallel"),
        name="nsa_sample",
    )(page_table, *([cache3] * n_pages), qbd, new, g8, win3, ov, w1cat, b1, w2bd, w1k, w2k)
    return o.reshape(b, 256), jnp.transpose(nw.reshape(b, 2, N_KV_A, HEAD_DIM, wb), (0, 4, 1, 2, 3))


def _rwkv_prep_body(c_ref, s0_ref, mu_ref, vec_ref, wup_ref, aup_ref, gup_ref,
                    r_ref, lw_ref, k_ref, v_ref, kk_ref, ka_ref, g_ref, bonus_ref, carry_ref, *, tiles_per_seq):
    i = pl.program_id(0)
    cols = c_ref[...]
    tm = cols.shape[0]

    @pl.when(i % tiles_per_seq == 0)
    def _():
        carry_ref[...] = s0_ref[0]

    prev = pltpu.roll(cols, 1, 0)
    row0 = lax.broadcasted_iota(jnp.int32, cols.shape, 0) == 0
    prev = jnp.where(row0, carry_ref[...], prev)
    carry_ref[...] = cols[tm - 1:tm, :]
    _rwkv_features(cols, prev, mu_ref, vec_ref, wup_ref, aup_ref, gup_ref,
                   r_ref, lw_ref, k_ref, v_ref, kk_ref, ka_ref, g_ref, bonus_ref)


def _rwkv_step_prep_body(c_ref, s0_ref, mu_ref, vec_ref, wup_ref, aup_ref, gup_ref,
                         r_ref, lw_ref, k_ref, v_ref, kk_ref, ka_ref, g_ref, bonus_ref):
    _rwkv_features(c_ref[...], s0_ref[...], mu_ref, vec_ref, wup_ref, aup_ref, gup_ref,
                   r_ref, lw_ref, k_ref, v_ref, kk_ref, ka_ref, g_ref, bonus_ref)


def _rwkv_features(cols, prev, mu_ref, vec_ref, wup_ref, aup_ref, gup_ref,
                   r_ref, lw_ref, k_ref, v_ref, kk_ref, ka_ref, g_ref, bonus_ref):
    xs = cols + mu_ref[...] * (prev - cols)
    r, k, v, lo = xs[:, 0:256], xs[:, 256:512], xs[:, 512:768], xs[:, 768:896]
    vec = vec_ref[...]
    w0, a0, k_k, k_a, r_k = vec[0:1], vec[1:2], vec[2:3], vec[3:4], vec[4:5]
    w_log = -jax.nn.softplus(-(w0 + _dot3(jnp.tanh(lo), wup_ref[...]))) - 0.5
    a = jax.nn.sigmoid(a0 + _dot3(lo, aup_ref[...]))
    g_ref[...] = _dot3(jax.nn.sigmoid(lo), gup_ref[...])
    ones = _block_ones(256, HEAD_DIM, BF16)
    kk = k * k_k
    kk = kk * lax.rsqrt(_dot_bf16_exact_rhs(kk * kk, ones) + 1e-12)
    k2 = k * (1.0 + (a - 1.0) * k_a)
    r_ref[...] = r
    lw_ref[...] = -jnp.exp(w_log)
    k_ref[...] = k2
    v_ref[...] = v
    kk_ref[...] = kk
    ka_ref[...] = kk * a
    bonus_ref[...] = _dot_bf16_exact_rhs(r * k2 * r_k, ones) * v


def _rwkv_prep(colsb, shift0, lw, t, tm):
    n = colsb.shape[0]
    tiles_per_seq = t // tm
    row = lambda i: (i, 0)
    fixed = lambda i: (0, 0)
    outs = [jax.ShapeDtypeStruct((n, 256), F32)] * 8
    return pl.pallas_call(
        functools.partial(_rwkv_prep_body, tiles_per_seq=tiles_per_seq),
        grid=(n // tm,),
        in_specs=[pl.BlockSpec((tm, SHIFT_PAD), row),
                  pl.BlockSpec((1, 1, SHIFT_PAD), lambda i: (i // tiles_per_seq, 0, 0)),
                  pl.BlockSpec((1, SHIFT_PAD), fixed), pl.BlockSpec((8, 256), fixed),
                  pl.BlockSpec((LANE, 256), fixed), pl.BlockSpec((LANE, 256), fixed), pl.BlockSpec((LANE, 256), fixed)],
        out_specs=[pl.BlockSpec((tm, 256), row)] * 8,
        out_shape=outs,
        scratch_shapes=[pltpu.VMEM((1, SHIFT_PAD), F32)],
        compiler_params=_cp("arbitrary"),
        name="rwkv_prep",
    )(colsb, shift0.reshape(-1, 1, SHIFT_PAD), lw['rwkv_mu'], lw['rwkv_vec'], lw['rwkv_wup'], lw['rwkv_aup'], lw['rwkv_gup'])


RWKV_CHUNK = 64


def _split_bf16(x):
    hi = x.astype(BF16)
    return hi, (x - hi.astype(F32)).astype(BF16)


def _dot3(a, b, dims=(((1,), (0,)), ((), ()))):
    ah, al = _split_bf16(a)
    bh, bl = _split_bf16(b)
    dg = lambda x, y: lax.dot_general(x, y, dims, preferred_element_type=F32)
    return dg(ah, bh) + dg(ah, bl) + dg(al, bh)


def _dot3_presplit(x, w_ref):
    xh, xl = _split_bf16(x)
    return (jnp.dot(xh, w_ref[0], preferred_element_type=F32) + jnp.dot(xh, w_ref[1], preferred_element_type=F32)
            + jnp.dot(xl, w_ref[0], preferred_element_type=F32))


def _dot_bf16_exact_rhs(x, m_bf16):
    hi, lo = _split_bf16(x)
    return jnp.dot(hi, m_bf16, preferred_element_type=F32) + jnp.dot(lo, m_bf16, preferred_element_type=F32)


_NT = (((1,), (1,)), ((), ()))


def _rwkv_chunk_body(r_ref, lw_ref, k_ref, v_ref, kk_ref, ka_ref, g_ref, bonus_ref, s0_ref, vec_ref,
                     o_ref, st_ref, s_scr, y_scr, *, nb, tl):
    L = RWKV_CHUNK
    nc = tl // L

    @pl.when(pl.program_id(1) == 0)
    def _():
        s_scr[...] = s0_ref[...]

    ri = lax.broadcasted_iota(jnp.int32, (L, L), 0)
    ci = lax.broadcasted_iota(jnp.int32, (L, L), 1)
    strict, incl = ri > ci, ri >= ci
    ltri = incl.astype(F32)
    eye = (ri == ci).astype(F32)

    bnn = (((2,), (1,)), ((0,), (0,)))
    bnt = (((2,), (2,)), ((0,), (0,)))

    def chunk(c, carry):
        rows = pl.ds(pl.multiple_of(c * L, L), L)
        lhs_l, rhs_l, v_l, kw_l, wl_l = [], [], [], [], []
        for b in range(nb):
            r, lw, k, v, kk, ka = [ref[b, rows, :] for ref in (r_ref, lw_ref, k_ref, v_ref, kk_ref, ka_ref)]
            cl = jnp.dot(ltri, lw, precision=HI, preferred_element_type=F32)
            e_neg = jnp.exp(-cl)
            e_rem = jnp.exp(cl[L - 1:L, :] - cl)
            kkd, rd = kk * jnp.exp(cl - lw), r * jnp.exp(cl)
            kinv, kainv, kw, kaw = k * e_neg, ka * e_neg, k * e_rem, ka * e_rem
            w_last = jnp.exp(cl[L - 1:L, :])
            for h in range(H_B):
                sl = slice(h * HEAD_DIM, (h + 1) * HEAD_DIM)
                lhs_l.append(jnp.concatenate([kkd[:, sl], rd[:, sl]], axis=0))
                rhs_l.append(jnp.concatenate([kinv[:, sl], kainv[:, sl]], axis=0))
                v_l.append(v[:, sl])
                kw_l.append(jnp.concatenate([kw[:, sl], kaw[:, sl]], axis=0))
                wl_l.append(w_last[:, sl])
        lhs, rhs, vs, kws, wl = [jnp.stack(x) for x in (lhs_l, rhs_l, v_l, kw_l, wl_l)]
        gm = _dot3(lhs, rhs, bnt)
        a_vk = jnp.where(strict, gm[:, :L, :L], 0.0)
        n1 = jnp.where(strict, -gm[:, :L, L:], 0.0)
        t_inv, pw = eye + n1, n1
        for _ in range(L.bit_length() - 2):
            pw = _dot3(pw, pw, bnn)
            t_inv = _dot3(t_inv, eye + pw, bnn)
        s = s_scr[...].reshape(nb * H_B, HEAD_DIM, HEAD_DIM)
        xs = _dot3(lhs, s, bnt)
        u = _dot3(t_inv, xs[:, :L] + _dot3(a_vk, vs, bnn), bnn)
        b_vk = jnp.where(incl, gm[:, L:, :L], 0.0).astype(BF16)
        b_uk = jnp.where(incl, gm[:, L:, L:], 0.0).astype(BF16)
        y = (xs[:, L:] + lax.dot_general(b_vk, vs.astype(BF16), bnn, preferred_element_type=F32)
             - lax.dot_general(b_uk, u.astype(BF16), bnn, preferred_element_type=F32))
        vu_t = jnp.stack([jnp.concatenate([vs[n], -u[n]], axis=0).T for n in range(nb * H_B)])
        s_new = s * wl + _dot3(vu_t, kws, bnn)
        s_scr[...] = s_new.reshape(nb, H_B, HEAD_DIM, HEAD_DIM)
        for b in range(nb):
            for h in range(H_B):
                y_scr[b, rows, h * HEAD_DIM:(h + 1) * HEAD_DIM] = y[b * H_B + h]
        return carry

    lax.fori_loop(0, nc, chunk, 0)
    st_ref[...] = s_scr[...]
    vec = vec_ref[...]
    for b in range(nb):
        o_ref[b] = (_segment_norm(y_scr[b], RWKV_GN_EPS) * vec[5:6] + vec[6:7] + bonus_ref[b]) * g_ref[b]


def _rwkv_chunked(prep, s0, vec, b, t, nb, tl):
    arrs = [a.reshape(b, t, 256) for a in prep]
    seq = pl.BlockSpec((nb, tl, 256), lambda i, j: (i, j, 0))
    st = pl.BlockSpec((nb, H_B, HEAD_DIM, HEAD_DIM), lambda i, j: (i, 0, 0, 0))
    return pl.pallas_call(
        functools.partial(_rwkv_chunk_body, nb=nb, tl=tl),
        grid=(b // nb, t // tl),
        in_specs=[seq] * 8 + [st, pl.BlockSpec((8, 256), lambda i, j: (0, 0))],
        out_specs=[seq, st],
        out_shape=[jax.ShapeDtypeStruct((b, t, 256), F32), jax.ShapeDtypeStruct((b, H_B, HEAD_DIM, HEAD_DIM), F32)],
        scratch_shapes=[pltpu.VMEM((nb, H_B, HEAD_DIM, HEAD_DIM), F32), pltpu.VMEM((nb, tl, 256), F32)],
        compiler_params=_cp("parallel", "arbitrary"),
        name="rwkv_chunked",
    )(*arrs, s0, vec)


def _rwkv_mixer(colsb, shift0, s0, lw, b, t, tm, nb, tl):
    prep = _rwkv_prep(colsb, shift0, lw, t, tm)
    o, st = _rwkv_chunked(prep, s0, lw['rwkv_vec'], b, t, nb, tl)
    shift = colsb.reshape(b, t, SHIFT_PAD)[:, -1, :SHIFT_B]
    return o.reshape(b * t, 256), st, shift


def _segment_norm(y, eps):
    avg = _block_ones(256, HEAD_DIM, BF16) * (1.0 / HEAD_DIM)
    yc = y - _dot_bf16_exact_rhs(y, avg)
    return yc * lax.rsqrt(_dot_bf16_exact_rhs(yc * yc, avg) + eps)


def _rwkv_step_body(r_ref, lw_ref, k_ref, v_ref, kk_ref, ka_ref, g_ref, bonus_ref, vec_ref, s_ref, o_ref, st_ref, ft_scr, y_scr):
    h = pl.program_id(0)

    @pl.when(h == 0)
    def _():
        for n, ref in enumerate((r_ref, lw_ref, k_ref, v_ref, kk_ref, ka_ref)):
            ft_scr[n] = ref[...].T

    base = pl.multiple_of(h * HEAD_DIM, HEAD_DIM)
    head = lambda n: ft_scr[n, pl.ds(base, HEAD_DIM), :]
    r_t, w_t, k_t, kk_t, ka_t = head(0), jnp.exp(head(1)), head(2), head(4), head(5)

    def body(i, carry):
        rows = pl.ds(pl.multiple_of(i * HEAD_DIM, HEAD_DIM), HEAD_DIM)
        s = s_ref[rows, :]
        sk = jnp.sum(s * kk_t, axis=0, keepdims=True)
        s = s * w_t - sk * ka_t + ft_scr[3, pl.ds(base + i, 1), :] * k_t
        st_ref[rows, :] = s
        y_scr[pl.ds(base + i, 1), :] = jnp.sum(s * r_t, axis=0, keepdims=True)
        return carry

    lax.fori_loop(0, HEAD_DIM, body, 0)

    @pl.when(h == H_B - 1)
    def _():
        vec = vec_ref[...]
        o_ref[...] = (_segment_norm(y_scr[...].T, RWKV_GN_EPS) * vec[5:6] + vec[6:7] + bonus_ref[...]) * g_ref[...]


def _rwkv_step(colsb, shift0, s_all, layer, lw):
    b = colsb.shape[0]
    hd2 = HEAD_DIM * HEAD_DIM
    full = lambda *_: (0, 0)
    feat = pl.BlockSpec((b, 256), full)
    prep = pl.pallas_call(
        _rwkv_step_prep_body,
        grid=(1,),
        in_specs=[pl.BlockSpec((b, SHIFT_PAD), full), pl.BlockSpec((b, SHIFT_PAD), full), pl.BlockSpec((1, SHIFT_PAD), full),
                  pl.BlockSpec((8, 256), full)] + [pl.BlockSpec((LANE, 256), full)] * 3,
        out_specs=[feat] * 8,
        out_shape=[jax.ShapeDtypeStruct((b, 256), F32)] * 8,
        compiler_params=_cp("arbitrary"),
        name="rwkv_step_prep",
    )(colsb, shift0, lw['rwkv_mu'], lw['rwkv_vec'], lw['rwkv_wup'], lw['rwkv_aup'], lw['rwkv_gup'])
    s_rows = jnp.transpose(s_all, (0, 2, 3, 4, 1)).reshape(-1, b)
    o, st = pl.pallas_call(
        _rwkv_step_body,
        grid=(H_B,),
        in_specs=[feat] * 8 + [pl.BlockSpec((8, 256), full), pl.BlockSpec((hd2, b), lambda h: (layer * H_B + h, 0))],
        out_specs=[feat, pl.BlockSpec((hd2, b), lambda h: (h, 0))],
        out_shape=[jax.ShapeDtypeStruct((b, 256), F32), jax.ShapeDtypeStruct((H_B * hd2, b), F32)],
        scratch_shapes=[pltpu.VMEM((6, 256, b), F32), pltpu.VMEM((256, b), F32)],
        compiler_params=_cp("arbitrary"),
        name="rwkv_step",
    )(*prep, lw['rwkv_vec'], s_rows)
    return o, jnp.transpose(st.reshape(H_B, HEAD_DIM, HEAD_DIM, b), (3, 0, 1, 2))


def _ret_step_body(c_ref, cos_ref, sin_ref, gn_ref, r0_ref, o_ref, rt_ref, ft_scr, acc_scr):
    h = pl.program_id(0)

    @pl.when(h == 0)
    def _():
        x = c_ref[...]
        cs, sn = cos_ref[...], sin_ref[...]
        first = (lax.broadcasted_iota(jnp.int32, cs.shape, 1) % HEAD_DIM) < (HEAD_DIM // 2)

        def rope(z):
            sw = jnp.where(first, pltpu.roll(z, 256 - HEAD_DIM // 2, 1), pltpu.roll(z, HEAD_DIM // 2, 1))
            return z * cs + sw * sn

        ft_scr[0] = rope(x[:, 0:256]).T
        ft_scr[1] = (rope(x[:, 256:512]) * (HEAD_DIM ** -0.5)).T
        ft_scr[2] = x[:, 512:768].T

    base = pl.multiple_of(h * HEAD_DIM, HEAD_DIM)
    nseq = rt_ref.shape[1]
    gamma = jnp.exp(jnp.log1p(-jnp.exp2(-5.0 - jnp.full((1, nseq), h, jnp.int32).astype(F32))))
    q_t, k_t, v_t = [ft_scr[n, pl.ds(base, HEAD_DIM), :] for n in range(3)]
    qk = jnp.sum(q_t * k_t, axis=0, keepdims=True)

    def body(d, cross):
        rows = pl.ds(pl.multiple_of(d * HEAD_DIM, HEAD_DIM), HEAD_DIM)
        r_old = r0_ref[rows, :]
        rt_ref[rows, :] = gamma * r_old + ft_scr[1, pl.ds(base + d, 1), :] * v_t
        return cross + ft_scr[0, pl.ds(base + d, 1), :] * r_old

    cross = lax.fori_loop(0, HEAD_DIM, body, jnp.zeros((HEAD_DIM, nseq), F32))
    acc_scr[pl.ds(base, HEAD_DIM), :] = qk * v_t + gamma * cross

    @pl.when(h == H_D - 1)
    def _():
        o_ref[...] = jax.nn.silu(c_ref[:, 768:1024]) * (_segment_norm(acc_scr[...].T, RET_GN_EPS) * gn_ref[...])


def _ret_step(colsd, cos, sin, r_all, layer, gn):
    b = colsd.shape[0]
    hd2 = HEAD_DIM * HEAD_DIM
    full = lambda *_: (0, 0)
    r_rows = jnp.transpose(r_all, (0, 2, 3, 4, 1)).reshape(-1, b)
    o, rt = pl.pallas_call(
        _ret_step_body,
        grid=(H_D,),
        in_specs=[pl.BlockSpec((b, 1024), full), pl.BlockSpec((b, 256), full), pl.BlockSpec((b, 256), full),
                  pl.BlockSpec((1, 256), full), pl.BlockSpec((hd2, b), lambda h: (layer * H_D + h, 0))],
        out_specs=[pl.BlockSpec((b, 256), full), pl.BlockSpec((hd2, b), lambda h: (h, 0))],
        out_shape=[jax.ShapeDtypeStruct((b, 256), F32), jax.ShapeDtypeStruct((H_D * hd2, b), F32)],
        scratch_shapes=[pltpu.VMEM((3, 256, b), F32), pltpu.VMEM((256, b), F32)],
        compiler_params=_cp("arbitrary"),
        name="ret_step",
    )(colsd, cos, sin, gn.reshape(1, 256), r_rows)
    return o, jnp.transpose(rt.reshape(H_D, HEAD_DIM, HEAD_DIM, b), (3, 0, 1, 2))


def _s5_step_body(u_ref, x0_ref, a1_ref, a2_ref, bt_ref, ct_ref, d_ref, wg_ref, o_ref, xt_ref):
    u = u_ref[...]
    dot_hi = lambda a, b: jnp.dot(a, b, precision=HI, preferred_element_type=F32)
    x0 = x0_ref[...]
    n = x0.shape[0]
    even = (lax.broadcasted_iota(jnp.int32, x0.shape, 0) % 2) == 0
    partner = jnp.where(even, pltpu.roll(x0, n - 1, 0), pltpu.roll(x0, 1, 0))
    x = a1_ref[...] * x0 + a2_ref[...] * partner + dot_hi(bt_ref[...], u.T)
    xt_ref[...] = x
    y = dot_hi(ct_ref[...], x).T + d_ref[...] * u
    z = jax.nn.gelu(y)
    o_ref[...] = z * jax.nn.sigmoid(dot_hi(z, wg_ref[...]))


def _s5_step(u, x_all, layer, lw):
    b = u.shape[0]
    a1, a2, bt, ct, d_row = lw['s5_step']
    n = 2 * G_C * S5_P
    full = lambda *_: (0, 0)
    x_rows = jnp.transpose(x_all, (0, 2, 3, 4, 1)).reshape(-1, b)
    o, xt = pl.pallas_call(
        _s5_step_body,
        grid=(1,),
        in_specs=[pl.BlockSpec((b, C_C), full), pl.BlockSpec((n, b), lambda i: (layer, 0)), pl.BlockSpec((n, 1), full),
                  pl.BlockSpec((n, 1), full), pl.BlockSpec((n, C_C), full), pl.BlockSpec((C_C, n), full),
                  pl.BlockSpec((1, C_C), full), pl.BlockSpec((C_C, C_C), full)],
        out_specs=[pl.BlockSpec((b, C_C), full), pl.BlockSpec((n, b), full)],
        out_shape=[jax.ShapeDtypeStruct((b, C_C), F32), jax.ShapeDtypeStruct((n, b), F32)],
        compiler_params=_cp("arbitrary"),
        name="s5_step",
    )(u, x_rows, a1, a2, bt, ct, d_row, lw['s5_w_glu'])
    return o, jnp.transpose(xt.reshape(G_C, S5_P, 2, b), (3, 0, 1, 2))


def _s5_params(lw):
    lr, li = lw['s5_lambda_re'], lw['s5_lambda_im']
    dt = jnp.exp(lw['s5_log_step'])[:, None]
    mag = jnp.exp(lr * dt)
    ar, ai = mag * jnp.cos(li * dt), mag * jnp.sin(li * dt)
    nr, ni = ar - 1.0, ai
    den = lr * lr + li * li
    fr, fi = (nr * lr + ni * li) / den, (ni * lr - nr * li) / den
    b_re, b_im = lw['s5_b'][0], lw['s5_b'][1]
    bbr = fr[..., None] * b_re - fi[..., None] * b_im
    bbi = fr[..., None] * b_im + fi[..., None] * b_re
    eye = jnp.eye(G_C, dtype=F32)
    bd_in = lambda m: jnp.einsum('gpc,gh->gchp', m, eye).reshape(G_C * S5_CH, G_C * S5_P)
    bd_out = lambda m: jnp.einsum('gcp,gh->gphc', m, eye).reshape(G_C * S5_P, G_C * S5_CH)
    b_big = jnp.concatenate([bd_in(bbr), bd_in(bbi)], axis=1)
    c_big = jnp.concatenate([bd_out(lw['s5_c'][0]), -bd_out(lw['s5_c'][1])], axis=0)
    a_row = jnp.concatenate([ar.reshape(1, -1), ai.reshape(1, -1)], axis=1)
    d_row = lw['s5_d'].reshape(1, C_C)
    n = 2 * G_C * S5_P
    a1 = jnp.stack([ar, ar], axis=-1).reshape(n, 1)
    a2 = jnp.stack([-ai, ai], axis=-1).reshape(n, 1)
    bt = jnp.stack([jnp.einsum('gpc,gh->gphc', bbr, eye), jnp.einsum('gpc,gh->gphc', bbi, eye)], axis=2).reshape(n, G_C * S5_CH)
    ct = jnp.stack([jnp.einsum('gcp,gh->gchp', lw['s5_c'][0], eye), -jnp.einsum('gcp,gh->gchp', lw['s5_c'][1], eye)],
                   axis=-1).reshape(G_C * S5_CH, n)
    return (a_row, b_big, c_big, d_row), (a1, a2, bt, ct, d_row)


def _s5_body(u_ref, x0_ref, a_ref, b_ref, c_ref, d_ref, wg_ref, o_ref, xt_ref, x_scr, bu_scr, xs_scr, *, nb, tt, mm_dtype, prec):
    sub = 8

    @pl.when(pl.program_id(0) == 0)
    def _():
        x_scr[...] = jnp.zeros_like(x_scr)
        x_scr[0:nb, :] = x0_ref[...]
        bu_scr[...] = jnp.zeros_like(bu_scr)

    np_ = G_C * S5_P
    ncb = np_ // LANE
    for b in range(nb):
        bu = jnp.dot(u_ref[b].astype(mm_dtype), b_ref[...], precision=prec, preferred_element_type=F32)
        for cb in range(2 * ncb):
            bu_scr[cb, pl.ds(b, tt, stride=sub), :] = bu[:, cb * LANE:(cb + 1) * LANE]
    a = a_ref[...]

    def step(t, x):
        rows = pl.ds(pl.multiple_of(t * sub, sub), sub)
        new = [None] * (2 * ncb)
        for cb in range(ncb):
            re, im = slice(cb * LANE, (cb + 1) * LANE), slice(np_ + cb * LANE, np_ + (cb + 1) * LANE)
            ar, ai, xr, xi = a[:, re], a[:, im], x[:, re], x[:, im]
            new[cb] = ar * xr - ai * xi + bu_scr[cb, rows, :]
            new[ncb + cb] = ar * xi + ai * xr + bu_scr[ncb + cb, rows, :]
            xs_scr[cb, rows, :] = new[cb]
            xs_scr[ncb + cb, rows, :] = new[ncb + cb]
        return jnp.concatenate(new, axis=1)

    x_last = lax.fori_loop(0, tt, step, x_scr[...], unroll=8)
    x_scr[...] = x_last
    xt_ref[...] = x_last[0:nb, :]
    for b in range(nb):
        u = u_ref[b]
        xs = jnp.concatenate([xs_scr[cb, pl.ds(b, tt, stride=sub), :] for cb in range(2 * ncb)], axis=1)
        y = jnp.dot(xs.astype(mm_dtype), c_ref[...], precision=prec, preferred_element_type=F32) + d_ref[...] * u
        z = jax.nn.gelu(y)
        o_ref[b] = z * jax.nn.sigmoid(jnp.dot(z.astype(mm_dtype), wg_ref[...], precision=prec, preferred_element_type=F32))


def _s5_mixer(u, x0, lw, b, t, tt, exact):
    a_row, b_big, c_big, d_row = lw['s5']
    mm_dtype = F32 if exact else BF16
    prec = HI if exact else None
    x0l = jnp.concatenate([x0[..., 0].reshape(b, -1), x0[..., 1].reshape(b, -1)], axis=1)
    np2 = 2 * G_C * S5_P
    fixed = lambda i: (0, 0)
    o, xt = pl.pallas_call(
        functools.partial(_s5_body, nb=b, tt=tt, mm_dtype=mm_dtype, prec=prec),
        grid=(t // tt,),
        in_specs=[pl.BlockSpec((b, tt, C_C), lambda i: (0, i, 0)), pl.BlockSpec((b, np2), fixed),
                  pl.BlockSpec((1, np2), fixed), pl.BlockSpec((C_C, np2), fixed), pl.BlockSpec((np2, C_C), fixed),
                  pl.BlockSpec((1, C_C), fixed), pl.BlockSpec((C_C, C_C), fixed)],
        out_specs=[pl.BlockSpec((b, tt, C_C), lambda i: (0, i, 0)), pl.BlockSpec((b, np2), fixed)],
        out_shape=[jax.ShapeDtypeStruct((b, t, C_C), F32), jax.ShapeDtypeStruct((b, np2), F32)],
        scratch_shapes=[pltpu.VMEM((8, np2), F32), pltpu.VMEM((np2 // LANE, 8 * tt, LANE), F32),
                        pltpu.VMEM((np2 // LANE, 8 * tt, LANE), F32)],
        compiler_params=_cp("arbitrary"),
        name="s5",
    )(u.reshape(b, t, C_C), x0l, a_row, b_big.astype(mm_dtype), c_big.astype(mm_dtype), d_row, lw['s5_w_glu'].astype(mm_dtype))
    xt = xt.reshape(b, 2, G_C, S5_P)
    return o.reshape(b * t, C_C), jnp.stack([xt[:, 0], xt[:, 1]], axis=-1)


def _ret_tables(pos, c):
    cos, sin = _rope_tables(pos, HEAD_DIM, RET_THETA, HEAD_DIM, H_D)
    log_g = jnp.log1p(-jnp.exp2(-5.0 - jnp.arange(H_D, dtype=F32)))
    i = jnp.arange(c, dtype=F32)
    diff = i[:, None] - i[None, :]
    dmat = jnp.where(diff >= 0, jnp.exp(jnp.maximum(diff, 0.0)[None] * log_g[:, None, None]), 0.0).reshape(H_D * c, c)
    q_dec = jnp.repeat(jnp.exp((i + 1.0)[None] * log_g[:, None]).T, HEAD_DIM, axis=1)
    k_dec = jnp.repeat(jnp.exp((c - 1.0 - i)[None] * log_g[:, None]).T, HEAD_DIM, axis=1)
    chunk_dec = jnp.repeat(jnp.exp(c * log_g), HEAD_DIM).reshape(256, 1)
    return cos, sin, dmat, q_dec, k_dec, chunk_dec


def _ret_body(c_ref, cos_ref, sin_ref, dmat_ref, qdec_ref, kdec_ref, cdec_ref, r0_ref, gn_ref, o_ref, rt_ref, r_scr, *, c):
    @pl.when(pl.program_id(1) == 0)
    def _():
        r_scr[...] = r0_ref[0]

    x = c_ref[0]
    q, k, v, g = x[:, 0:256], x[:, 256:512], x[:, 512:768], x[:, 768:1024]
    cs, sn = cos_ref[...], sin_ref[...]
    lane = lax.broadcasted_iota(jnp.int32, (c, 256), 1)
    first = (lane % HEAD_DIM) < (HEAD_DIM // 2)

    def rope(z):
        sw = jnp.where(first, pltpu.roll(z, 256 - HEAD_DIM // 2, 1), pltpu.roll(z, HEAD_DIM // 2, 1))
        return z * cs + sw * sn

    q = rope(q)
    k = rope(k) * (HEAD_DIM ** -0.5)
    head = lane // HEAD_DIM
    kb, vb = k.astype(BF16), v.astype(BF16)
    qstack = jnp.concatenate([jnp.where(head == h, q, 0.0) for h in range(H_D)], axis=0).astype(BF16)
    s = lax.dot_general(qstack, kb, (((1,), (1,)), ((), ())), preferred_element_type=F32) * dmat_ref[...]
    pv = jnp.dot(s.astype(BF16), vb, preferred_element_type=F32)
    inner = jnp.zeros((c, 256), F32)
    for h in range(H_D):
        inner = inner + jnp.where(head == h, pv[h * c:(h + 1) * c], 0.0)
    r_old = r_scr[...]
    cross = jnp.dot((q * qdec_ref[...]).astype(BF16), r_old.astype(BF16), preferred_element_type=F32)
    kv = lax.dot_general((k * kdec_ref[...]).astype(BF16), vb, (((0,), (0,)), ((), ())), preferred_element_type=F32)
    bd = _block_ones(256, HEAD_DIM, F32)
    r_new = cdec_ref[...] * r_old + kv * bd
    r_scr[...] = r_new
    rt_ref[0] = r_new
    o_ref[0] = jax.nn.silu(g) * (_segment_norm(inner + cross, RET_GN_EPS) * gn_ref[...])


def _ret_mixer(colsd, r0, lw, tabs, b, t):
    c = RET_CHUNK if t % RET_CHUNK == 0 else t
    cos, sin, dmat, q_dec, k_dec, chunk_dec = tabs
    eye = jnp.eye(H_D, dtype=F32)
    r0l = jnp.einsum('bhde,hg->bhdge', r0, eye).reshape(b, 256, 256)
    n_t = t // c
    fixed = lambda i, j: (0, 0)
    o, rt = pl.pallas_call(
        functools.partial(_ret_body, c=c),
        grid=(b, n_t),
        in_specs=[pl.BlockSpec((1, c, 1024), lambda i, j: (i, j, 0)),
                  pl.BlockSpec((c, 256), lambda i, j: (j, 0)), pl.BlockSpec((c, 256), lambda i, j: (j, 0)),
                  pl.BlockSpec((H_D * c, c), fixed), pl.BlockSpec((c, 256), fixed), pl.BlockSpec((c, 256), fixed),
                  pl.BlockSpec((256, 1), fixed), pl.BlockSpec((1, 256, 256), lambda i, j: (i, 0, 0)),
                  pl.BlockSpec((1, 256), fixed)],
        out_specs=[pl.BlockSpec((1, c, 256), lambda i, j: (i, j, 0)), pl.BlockSpec((1, 256, 256), lambda i, j: (i, 0, 0))],
        out_shape=[jax.ShapeDtypeStruct((b, t, 256), F32), jax.ShapeDtypeStruct((b, 256, 256), F32)],
        scratch_shapes=[pltpu.VMEM((256, 256), F32)],
        compiler_params=_cp("parallel", "arbitrary"),
        name="retention",
    )(colsd.reshape(b, t, 1024), cos, sin, dmat, q_dec, k_dec, chunk_dec, r0l, lw['ret_gn'].reshape(1, 256))
    rt = jnp.einsum('bhdge,hg->bhde', rt.reshape(b, H_D, HEAD_DIM, H_D, HEAD_DIM), eye)
    return o.reshape(b * t, 256), rt


def _partner(x, d, period):
    pos = lax.broadcasted_iota(jnp.int32, x.shape, 1) % period
    return jnp.where(pos + d < period, pltpu.roll(x, LANE - d, 1), pltpu.roll(x, period - d, 1))


def _out_body(x_ref, oa_ref, ob_ref, oc_ref, od_ref, w_ref, nw_ref, wr_ref, br_ref, x1_ref, h_ref, comb_ref):
    acc = x_ref[...]
    for i, ref in enumerate((oa_ref, ob_ref, oc_ref, od_ref)):
        acc = acc + jnp.dot(ref[...].astype(BF16), w_ref[256 * i:256 * (i + 1), :], preferred_element_type=F32)
    x1_ref[...] = acc
    h = acc * lax.rsqrt(jnp.mean(acc * acc, axis=-1, keepdims=True) + RMS_EPS) * nw_ref[...]
    hb = h.astype(BF16)
    h_ref[...] = hb
    h_lo = (h - hb.astype(F32)).astype(BF16)
    logits = (jnp.dot(hb, wr_ref[0], preferred_element_type=F32) + jnp.dot(hb, wr_ref[1], preferred_element_type=F32)
              + jnp.dot(h_lo, wr_ref[0], preferred_element_type=F32) + br_ref[...])
    le, lg = logits[:, :LANE], logits[:, LANE:]
    lane = lax.broadcasted_iota(jnp.int32, le.shape, 1)
    mg = jnp.max(lg, axis=-1, keepdims=True)
    eg = jnp.exp(lg - mg)
    pg = eg / (jnp.sum(eg, axis=-1, keepdims=True) * (1.0 / 32.0))
    gidx = (lane % N_EXPERTS) // EXP_PER_GROUP
    g_rank = jnp.zeros_like(pg)
    for d in range(1, N_GROUPS):
        other = pltpu.roll(pg, LANE - EXP_PER_GROUP * d, 1)
        wrapped = gidx + d >= N_GROUPS
        g_rank = g_rank + jnp.where((other > pg) | ((other == pg) & wrapped), 1.0, 0.0)
    kidx = lane % EXP_PER_GROUP
    others = [_partner(le, d, EXP_PER_GROUP) for d in range(1, EXP_PER_GROUP)]
    me = functools.reduce(jnp.maximum, others, le)
    ee = jnp.exp(le - me)
    se = ee
    for d in range(1, EXP_PER_GROUP):
        se = se + _partner(ee, d, EXP_PER_GROUP)
    pe = ee / se
    e_rank = jnp.zeros_like(pe)
    for d in range(1, EXP_PER_GROUP):
        other = _partner(pe, d, EXP_PER_GROUP)
        wrapped = kidx + d >= EXP_PER_GROUP
        e_rank = e_rank + jnp.where((other > pe) | ((other == pe) & wrapped), 1.0, 0.0)
    top = jnp.where(e_rank < 2.0, pe, 0.0)
    den = top
    for d in range(1, EXP_PER_GROUP):
        den = den + _partner(top, d, EXP_PER_GROUP)
    comb = jnp.where((g_rank < 1.0) & (lane < N_EXPERTS), pg * (top / den), 0.0)
    comb_ref[...] = comb


def _out_router(x, oa, ob, oc, od, lw, tm):
    n = x.shape[0]
    row = lambda i: (i, 0)
    fixed = lambda i: (0, 0)
    mix = pl.BlockSpec((tm, 256), row)
    return pl.pallas_call(
        _out_body,
        grid=(n // tm,),
        in_specs=[pl.BlockSpec((tm, D_MODEL), row), mix, mix, mix, mix,
                  pl.BlockSpec((D_MODEL, D_MODEL), fixed), pl.BlockSpec((1, D_MODEL), fixed),
                  pl.BlockSpec((2, D_MODEL, 2 * LANE), lambda i: (0, 0, 0)), pl.BlockSpec((1, 2 * LANE), fixed)],
        out_specs=[pl.BlockSpec((tm, D_MODEL), row), pl.BlockSpec((tm, D_MODEL), row), pl.BlockSpec((tm, LANE), row)],
        out_shape=[jax.ShapeDtypeStruct((n, D_MODEL), F32), jax.ShapeDtypeStruct((n, D_MODEL), BF16),
                   jax.ShapeDtypeStruct((n, LANE), F32)],
        compiler_params=_cp("parallel"),
        name="out_router",
    )(x, oa, ob, oc, od, lw['w_out'], lw['norm_ffn'], lw['w_router'], lw['b_router'])


def _router_weights(w_grp, b_grp, w_exp, b_exp):
    we = jnp.transpose(w_exp, (1, 0, 2)).reshape(D_MODEL, N_EXPERTS)
    wg = jnp.repeat(w_grp, EXP_PER_GROUP, axis=1)
    reps = LANE // N_EXPERTS
    w = jnp.concatenate([jnp.tile(we, (1, reps)), jnp.tile(wg, (1, reps))], axis=1)
    b = jnp.concatenate([jnp.tile(b_exp.reshape(1, N_EXPERTS), (1, reps)),
                         jnp.tile(jnp.repeat(b_grp, EXP_PER_GROUP).reshape(1, N_EXPERTS), (1, reps))], axis=1)
    w_hi = w.astype(BF16)
    return jnp.stack([w_hi, (w - w_hi.astype(F32)).astype(BF16)]), b


def _moe_body(h_ref, comb_ref, x1_ref, wg_ref, wu_ref, wd_ref, nf_ref, *out_refs, final):
    acc_ref = out_refs[-1]
    e = pl.program_id(1)

    @pl.when(e == 0)
    def _():
        acc_ref[...] = x1_ref[...]

    h = h_ref[...]
    comb = comb_ref[...]
    lane = lax.broadcasted_iota(jnp.int32, comb.shape, 1)
    c = jnp.sum(jnp.where(lane == e, comb, 0.0), axis=-1, keepdims=True)
    hg = jnp.dot(h, wg_ref[0], preferred_element_type=F32)
    hu = jnp.dot(h, wu_ref[0], preferred_element_type=F32)
    act = (jax.nn.silu(hg) * hu * c).astype(BF16)
    acc_ref[...] += jnp.dot(act, wd_ref[0], preferred_element_type=F32)

    @pl.when(e == N_EXPERTS - 1)
    def _():
        x2 = acc_ref[...]
        if final:
            out_refs[0][...] = x2 * lax.rsqrt(jnp.mean(x2 * x2, axis=-1, keepdims=True) + RMS_EPS) * nf_ref[...]
        else:
            out_refs[0][...] = x2


def _moe(h, comb, x1, lw, norm_final, tm, final):
    n = x1.shape[0]
    row = lambda i, e: (i, 0)
    per_e = lambda i, e: (e, 0, 0)
    return pl.pallas_call(
        functools.partial(_moe_body, final=final),
        grid=(n // tm, N_EXPERTS),
        in_specs=[pl.BlockSpec((tm, D_MODEL), row), pl.BlockSpec((tm, LANE), row), pl.BlockSpec((tm, D_MODEL), row),
                  pl.BlockSpec((1, D_MODEL, D_EXPERT), per_e), pl.BlockSpec((1, D_MODEL, D_EXPERT), per_e),
                  pl.BlockSpec((1, D_EXPERT, D_MODEL), per_e), pl.BlockSpec((1, D_MODEL), lambda i, e: (0, 0))],
        out_specs=pl.BlockSpec((tm, D_MODEL), row),
        out_shape=jax.ShapeDtypeStruct((n, D_MODEL), F32),
        scratch_shapes=[pltpu.VMEM((tm, D_MODEL), F32)],
        compiler_params=_cp("parallel", "arbitrary"),
        name="moe",
    )(h, comb, x1, lw['moe_wg'], lw['moe_wu'], lw['moe_wd'], norm_final.reshape(1, D_MODEL))


def _prep_layer(l, p):
    w_in = p['w_in'][l]
    o = _offsets(SPLIT_SIZES)
    segs = jnp.split(w_in, o, axis=1)
    w_all = jnp.concatenate([segs[0], segs[1], _pad_to(segs[2], LANE, 1), _pad_to(segs[3], SHIFT_PAD, 1), segs[4], segs[5]],
                            axis=1).astype(BF16)
    w_prec = w_in[:, :PRECISE_COLS]
    w_lo = (w_prec - w_prec.astype(BF16).astype(F32)).astype(BF16)
    lw = {'layer': l, 'w_all': w_all, 'w_lo': w_lo, 'norm_mix': p['norm_mix'][l]}
    lw['cmp'] = _cmp_weights(p['nsa_cmp_w1'][l], p['nsa_cmp_b1'][l], p['nsa_cmp_w2'][l])
    lw['rwkv_mu'] = _pad_to(p['rwkv_mu'][l].reshape(1, SHIFT_B), SHIFT_PAD, 1)
    lw['rwkv_vec'] = _pad_to(p['rwkv_vec'][l], 8, 0)
    z = lambda a, b: jnp.zeros((a, b), F32)
    lw['rwkv_wup'] = jnp.concatenate([p['rwkv_w_up'][l], z(LANE - LORA_W, C_B)], axis=0)
    lw['rwkv_aup'] = jnp.concatenate([z(LORA_W, C_B), p['rwkv_a_up'][l], z(LANE - LORA_W - LORA_A, C_B)], axis=0)
    lw['rwkv_gup'] = jnp.concatenate([z(LORA_W + LORA_A, C_B), p['rwkv_g_up'][l], z(LANE - LORA_W - LORA_A - LORA_G, C_B)], axis=0)
    for name in ('s5_lambda_re', 's5_lambda_im', 's5_b', 's5_c', 's5_d', 's5_log_step', 's5_w_glu', 'ret_gn'):
        lw[name] = p[name][l]
    lw['s5'], lw['s5_step'] = _s5_params(lw)
    lw['w_out'] = p['w_out'][l].astype(BF16)
    lw['norm_ffn'] = p['norm_ffn'][l].reshape(1, D_MODEL)
    lw['w_router'], lw['b_router'] = _router_weights(p['moe_w_grp'][l], p['moe_b_grp'][l], p['moe_w_exp'][l], p['moe_b_exp'][l])
    lw['moe_wg'] = p['moe_w_gate'][l].astype(BF16)
    lw['moe_wu'] = p['moe_w_up'][l].astype(BF16)
    lw['moe_wd'] = p['moe_w_down'][l].astype(BF16)
    return lw


ROW_TILE = 512
MOE_ROW_TILE = 1024
SCAN_TILE = 256


def _prompt_layer(x, lw, tabs, b, t, norm_final):
    cos_a, sin_a, ret_tabs = tabs
    q, kv, gate, colsb, u, colsd = _proj(x, lw['norm_mix'], lw['w_all'], lw['w_lo'], cos_a, sin_a, ROW_TILE)
    o_a = _nsa_prompt_mixer(q, kv, gate, lw, b, t)
    o_b, s_rwkv, s_shift = _rwkv_mixer(colsb, jnp.zeros((b, SHIFT_PAD), F32), jnp.zeros((b, H_B, HEAD_DIM, HEAD_DIM), F32),
                                       lw, b, t, ROW_TILE, b, SCAN_TILE)
    o_c, s_s5 = _s5_mixer(u, jnp.zeros((b, G_C, S5_P, 2), F32), lw, b, t, SCAN_TILE, False)
    o_d, s_ret = _ret_mixer(colsd, jnp.zeros((b, H_D, HEAD_DIM, HEAD_DIM), F32), lw, ret_tabs, b, t)
    x1, h, comb = _out_router(x, o_a, o_b, o_c, o_d, lw, ROW_TILE)
    x2 = _moe(h, comb, x1, lw, norm_final, MOE_ROW_TILE, lw['layer'] == DEPTH - 1)
    kv3 = kv.reshape(b, t, 6 * LANE)
    rows = kv3[:, :, :4 * LANE].reshape(b, t, 4, N_KV_A, HEAD_DIM)
    win = kv3[:, t - min(WINDOW, t):, 4 * LANE:].reshape(b, min(WINDOW, t), 2, N_KV_A, HEAD_DIM)
    return x2, (rows, win, s_rwkv, s_shift, s_s5, s_ret)


def _sample_layer(x, lw, tabs, b, pos, cache_kv, page_table, win_buf, s_rwkv, s_shift, s_s5, s_ret, norm_final):
    cos_a, sin_a, ret_cs = tabs
    q, kv, gate, colsb, u, colsd = _proj(x, lw['norm_mix'], lw['w_all'], lw['w_lo'], cos_a, sin_a, b)
    o_a, win = _nsa_sample(q, kv, gate, cache_kv, win_buf, lw['layer'], page_table, lw['cmp'], int(pos[0]))
    rows = kv[:, :4 * LANE].reshape(b, 1, 4, N_KV_A, HEAD_DIM)
    o_b, s_rwkv = _rwkv_step(colsb, _pad_to(s_shift, SHIFT_PAD, 1), s_rwkv, lw['layer'], lw)
    s_shift = colsb[:, :SHIFT_B]
    o_c, s_s5 = _s5_step(u, s_s5, lw['layer'], lw)
    o_d, s_ret = _ret_step(colsd, ret_cs[0], ret_cs[1], s_ret, lw['layer'], lw['ret_gn'])
    x1, h, comb = _out_router(x, o_a, o_b, o_c, o_d, lw, b)
    x2 = _moe(h, comb, x1, lw, norm_final, b, lw['layer'] == DEPTH - 1)
    return x2, (rows, win, s_rwkv, s_shift, s_s5, s_ret)


def kernel(x_prompt, x_sample, cache_nsa_kv, cache_nsa_win, state_rwkv, state_rwkv_shift, state_s5, state_ret, page_table, norm_mix, w_in, nsa_cmp_w1, nsa_cmp_b1, nsa_cmp_w2, rwkv_mu, rwkv_vec, rwkv_w_up, rwkv_a_up, rwkv_g_up, s5_lambda_re, s5_lambda_im, s5_b, s5_c, s5_d, s5_log_step, s5_w_glu, ret_gn, w_out, norm_ffn, moe_w_grp, moe_b_grp, moe_w_exp, moe_b_exp, moe_w_gate, moe_w_up, moe_w_down, norm_final):
    p = dict(norm_mix=norm_mix, w_in=w_in, nsa_cmp_w1=nsa_cmp_w1, nsa_cmp_b1=nsa_cmp_b1, nsa_cmp_w2=nsa_cmp_w2,
             rwkv_mu=rwkv_mu, rwkv_vec=rwkv_vec, rwkv_w_up=rwkv_w_up, rwkv_a_up=rwkv_a_up, rwkv_g_up=rwkv_g_up,
             s5_lambda_re=s5_lambda_re, s5_lambda_im=s5_lambda_im, s5_b=s5_b, s5_c=s5_c, s5_d=s5_d,
             s5_log_step=s5_log_step, s5_w_glu=s5_w_glu, ret_gn=ret_gn, w_out=w_out, norm_ffn=norm_ffn,
             moe_w_grp=moe_w_grp, moe_b_grp=moe_b_grp, moe_w_exp=moe_w_exp, moe_b_exp=moe_b_exp,
             moe_w_gate=moe_w_gate, moe_w_up=moe_w_up, moe_w_down=moe_w_down)
    bp, tp = x_prompt.shape[:2]
    bs, ts = x_sample.shape[:2]
    assert ts == 1 and tp % ROW_TILE == 0 and (bp * tp) % MOE_ROW_TILE == 0 and tp % (4 * LANE) == 0 and bs % 8 == 0
    past_len = page_table.shape[1] * cache_nsa_kv.shape[2]
    pos_p = np.arange(tp)
    pos_s = past_len + np.arange(ts)
    c = RET_CHUNK if tp % RET_CHUNK == 0 else tp
    tabs_p = _rope_tables(pos_p, ROT_DIM, ROPE_THETA, HEAD_DIM, 2) + (_ret_tables(pos_p, c),)
    pos_rows = np.repeat(pos_s, bs)
    tabs_s = _rope_tables(pos_rows, ROT_DIM, ROPE_THETA, HEAD_DIM, 2) + (_rope_tables(pos_rows, HEAD_DIM, RET_THETA, HEAD_DIM, H_D),)
    xp = x_prompt.reshape(bp * tp, D_MODEL)
    xs = x_sample.reshape(bs * ts, D_MODEL)
    sts_p, sts_s = [], []
    for l in range(DEPTH):
        lw = _prep_layer(l, p)
        xp, st_p = _prompt_layer(xp, lw, tabs_p, bp, tp, norm_final)
        xs, st_s = _sample_layer(xs, lw, tabs_s, bs, pos_s, cache_nsa_kv, page_table, cache_nsa_win, state_rwkv,
                                     state_rwkv_shift[l], state_s5, state_ret, norm_final)
        rows, win, s1, s2, s3, s4 = st_s
        sts_s.append((rows, win, s1, s2, s3, s4))
        sts_p.append(st_p)
    new_p = [jnp.stack([st[i] for st in sts_p]) for i in range(6)]
    new_s = [jnp.stack([st[i] for st in sts_s]) for i in range(6)]
    return (xp.reshape(bp, tp, D_MODEL), xs.reshape(bs, ts, D_MODEL), new_p[0], new_s[0], new_p[1], new_s[1],
            new_p[2], new_s[2], new_p[3], new_s[3], new_p[4], new_s[4], new_p[5], new_s[5])
```

```python
import functools

import numpy as np
import jax
import jax.numpy as jnp
from jax import lax
from jax.experimental import pallas as pl
from jax.experimental.pallas import tpu as pltpu

F32 = jnp.float32
BF16 = jnp.bfloat16
HI = lax.Precision.HIGHEST

D_MODEL = 1024
DEPTH = 2
HEAD_DIM = 64
C_A = C_B = C_C = C_D = 256
H_A = 4
N_KV_A = 2
R_A = 2
ROT_DIM = 16
ROPE_THETA = 500000.0
CMP_BLOCK = 32
CMP_STRIDE = 16
CMP_HIDDEN = 128
SEL_BLOCK = 64
TOP_K = 16
WINDOW = 512
NEG_INF = -1e30
FORCED_SCORE = 1e9
BLOCKED_SCORE = -1e9
H_B = 4
LORA_W = 16
LORA_A = 16
LORA_G = 32
SHIFT_B = 832
SHIFT_PAD = 896
RWKV_GN_EPS = 64e-5
S5_CH = 16
G_C = 16
S5_P = 64
H_D = 4
RET_CHUNK = 128
RET_THETA = 10000.0
RET_GN_EPS = 1e-5
N_GROUPS = 4
EXP_PER_GROUP = 4
N_EXPERTS = 16
D_EXPERT = 256
RMS_EPS = 1e-6
SPLIT_SIZES = (C_A, 6 * N_KV_A * HEAD_DIM, 3 * H_A, SHIFT_B, C_C, 4 * C_D)
LANE = 128
VMEM_LIMIT = 56 * 1024 * 1024


def _cp(*sem):
    return pltpu.CompilerParams(dimension_semantics=sem, vmem_limit_bytes=VMEM_LIMIT)


def _offsets(sizes):
    return [int(s) for s in np.cumsum(sizes)[:-1]]


def _pad_to(a, n, axis):
    pad = [(0, 0)] * a.ndim
    pad[axis] = (0, n - a.shape[axis])
    return jnp.pad(a, pad)


def _block_ones(n, blk, dtype):
    r = lax.broadcasted_iota(jnp.int32, (n, n), 0) // blk
    c = lax.broadcasted_iota(jnp.int32, (n, n), 1) // blk
    return (r == c).astype(dtype)


def _rope_tables(pos, rot_dim, theta, period, reps):
    half = rot_dim // 2
    inv = theta ** (-jnp.arange(half, dtype=F32) / half)
    ang = jnp.asarray(pos, F32)[:, None] * inv[None, :]
    cos, sin = jnp.cos(ang), jnp.sin(ang)
    n = ang.shape[0]
    rest = period - rot_dim
    c = jnp.concatenate([cos, cos, jnp.ones((n, rest), F32)], -1)
    s = jnp.concatenate([-sin, sin, jnp.zeros((n, rest), F32)], -1)
    return jnp.tile(c, (1, reps)), jnp.tile(s, (1, reps))


PRECISE_COLS = 384


def _proj_body(x_ref, nw_ref, w_ref, wlo_ref, cos_ref, sin_ref, q_ref, kv_ref, g_ref, cb_ref, u_ref, cd_ref):
    x = x_ref[...]
    h = x * lax.rsqrt(jnp.mean(x * x, axis=-1, keepdims=True) + RMS_EPS) * nw_ref[...]
    hb = h.astype(BF16)
    h_lo = (h - hb.astype(F32)).astype(BF16)
    c = cos_ref[...]
    s = sin_ref[...]
    first = (lax.broadcasted_iota(jnp.int32, c.shape, 1) % HEAD_DIM) < (ROT_DIM // 2)

    def rope(z):
        sw = jnp.where(first, pltpu.roll(z, LANE - ROT_DIM // 2, 1), pltpu.roll(z, ROT_DIM // 2, 1))
        return z * c + sw * s

    def dot(a, b):
        z = jnp.dot(hb, w_ref[:, a:b], preferred_element_type=F32)
        if b <= PRECISE_COLS:
            z = z + (jnp.dot(hb, wlo_ref[:, a:b], preferred_element_type=F32)
                     + jnp.dot(h_lo, w_ref[:, a:b], preferred_element_type=F32))
        return z

    for j in range(2):
        q_ref[:, LANE * j:LANE * (j + 1)] = rope(dot(LANE * j, LANE * (j + 1)))
    for j in range(6):
        z = dot(256 + LANE * j, 256 + LANE * (j + 1))
        kv_ref[:, LANE * j:LANE * (j + 1)] = rope(z) if j % 2 == 0 else z
    g_ref[...] = dot(1024, 1152)
    cb_ref[...] = dot(1152, 2048)
    u_ref[...] = dot(2048, 2304)
    cd_ref[...] = dot(2304, 3328)


def _proj(x2d, norm_w, w_all, w_lo, cos_t, sin_t, tm):
    n = x2d.shape[0]
    t_tiles = cos_t.shape[0] // tm
    row = lambda i: (i, 0)
    fixed = lambda i: (0, 0)
    tab = lambda i: (i % t_tiles, 0)
    widths = (256, 768, 128, SHIFT_PAD, 256, 1024)
    return pl.pallas_call(
        _proj_body,
        grid=(n // tm,),
        in_specs=[pl.BlockSpec((tm, D_MODEL), row), pl.BlockSpec((1, D_MODEL), fixed),
                  pl.BlockSpec((D_MODEL, 3328), fixed), pl.BlockSpec((D_MODEL, PRECISE_COLS), fixed),
                  pl.BlockSpec((tm, LANE), tab), pl.BlockSpec((tm, LANE), tab)],
        out_specs=[pl.BlockSpec((tm, w), row) for w in widths],
        out_shape=[jax.ShapeDtypeStruct((n, w), F32) for w in widths],
        compiler_params=_cp("parallel"),
        name="proj",
    )(x2d, norm_w.reshape(1, D_MODEL), w_all, w_lo, cos_t, sin_t)


def _cmp_mlp(xc, kind, w1_ref, b1_ref, w2_ref, w1k_ref, w2k_ref, nck):
    if kind == 0:
        hh = _dot3_presplit(xc, w1k_ref)
    else:
        hh = jnp.dot(xc.astype(BF16), w1_ref[kind], preferred_element_type=F32)
    hs = []
    for g in range(N_KV_A):
        hg = hh[g * nck:(g + 1) * nck]
        hs.append(jax.nn.gelu(b1_ref[kind] + hg[:, :CMP_HIDDEN] + pltpu.roll(hg[:, CMP_HIDDEN:], nck - 1, 0)))
    act = jnp.concatenate(hs, axis=1)
    if kind == 0:
        return _dot3_presplit(act, w2k_ref)
    return jnp.dot(act.astype(BF16), w2_ref[kind], preferred_element_type=F32)


def _cmp_body(xk_ref, xv_ref, w1_ref, b1_ref, w2_ref, w1k_ref, w2k_ref, kc_ref, vc_ref, vct_ref, xc_ref, *, n_chunks):
    lane = lax.broadcasted_iota(jnp.int32, (n_chunks, LANE), 1)
    lo = lane < HEAD_DIM
    for pair in range(CMP_STRIDE // 2):
        for kind, x_ref in enumerate((xk_ref, xv_ref)):
            ak = x_ref[0, pl.ds(2 * pair, n_chunks, stride=CMP_STRIDE), :]
            bk = x_ref[0, pl.ds(2 * pair + 1, n_chunks, stride=CMP_STRIDE), :]
            xc_ref[kind, 0:n_chunks, LANE * pair:LANE * (pair + 1)] = jnp.where(lo, ak, pltpu.roll(bk, HEAD_DIM, 1))
            xc_ref[kind, n_chunks:2 * n_chunks, LANE * pair:LANE * (pair + 1)] = jnp.where(lo, pltpu.roll(ak, HEAD_DIM, 1), bk)
    outs = [_cmp_mlp(xc_ref[kind], kind, w1_ref, b1_ref, w2_ref, w1k_ref, w2k_ref, n_chunks) for kind in range(2)]
    kc_ref[0] = outs[0]
    vc_ref[0] = outs[1]
    vct_ref[0] = outs[1].T


def _nsa_compress(rows, w1cat, b1, w2bd, w1k, w2k):
    b, tk = rows.shape[0], rows.shape[1]
    n_chunks = tk // CMP_STRIDE
    fixed3 = lambda i: (0, 0, 0)
    fixed2 = lambda i: (0, 0)
    return pl.pallas_call(
        functools.partial(_cmp_body, n_chunks=n_chunks),
        grid=(b,),
        in_specs=[pl.BlockSpec((1, tk, LANE), lambda i: (i, 0, 0)), pl.BlockSpec((1, tk, LANE), lambda i: (i, 0, 1)),
                  pl.BlockSpec((2, CMP_STRIDE * HEAD_DIM, 2 * CMP_HIDDEN), fixed3),
                  pl.BlockSpec((2, 1, CMP_HIDDEN), fixed3),
                  pl.BlockSpec((2, 2 * CMP_HIDDEN, LANE), fixed3),
                  pl.BlockSpec((2, CMP_STRIDE * HEAD_DIM, 2 * CMP_HIDDEN), fixed3), pl.BlockSpec((2, 2 * CMP_HIDDEN, LANE), fixed3)],
        out_specs=[pl.BlockSpec((1, n_chunks, LANE), lambda i: (i, 0, 0)),
                   pl.BlockSpec((1, n_chunks, LANE), lambda i: (i, 0, 0)),
                   pl.BlockSpec((1, LANE, n_chunks), lambda i: (i, 0, 0))],
        out_shape=[jax.ShapeDtypeStruct((b, n_chunks, LANE), F32), jax.ShapeDtypeStruct((b, n_chunks, LANE), F32),
                   jax.ShapeDtypeStruct((b, LANE, n_chunks), F32)],
        scratch_shapes=[pltpu.VMEM((2, N_KV_A * n_chunks, CMP_STRIDE * HEAD_DIM), F32)],
        compiler_params=_cp("parallel"),
        name="nsa_compress",
    )(rows, rows, w1cat, b1, w2bd, w1k, w2k)


def _cmp_weights(cmp_w1, cmp_b1, cmp_w2):
    m = CMP_BLOCK // CMP_STRIDE
    w1r = cmp_w1.reshape(2, m, CMP_STRIDE * HEAD_DIM, CMP_HIDDEN)
    w1cat = jnp.concatenate([w1r[:, j] for j in range(m)], axis=-1)
    z = jnp.zeros_like(cmp_w2)
    w2bd = jnp.concatenate([jnp.concatenate([cmp_w2, z], -1), jnp.concatenate([z, cmp_w2], -1)], axis=1)
    split = lambda w: jnp.stack([w.astype(BF16), (w - w.astype(BF16).astype(F32)).astype(BF16)])
    return w1cat.astype(BF16), cmp_b1.reshape(2, 1, CMP_HIDDEN), w2bd.astype(BF16), split(w1cat[0]), split(w2bd[0])


def _cmp_to_sel_t(n_chunks, n_cmp, n_sel):
    starts = np.arange(n_chunks) * CMP_STRIDE
    sel_s = np.arange(n_sel) * SEL_BLOCK
    ov = np.minimum(starts[:, None] + CMP_BLOCK, sel_s[None] + SEL_BLOCK) - np.maximum(starts[:, None], sel_s[None])
    ov = np.clip(ov, 0, None) / CMP_BLOCK
    ov[n_cmp:] = 0.0
    return jnp.asarray(ov.T, dtype=F32)


def _masked_softmax_cols(s, mask):
    m = jnp.max(jnp.where(mask, s, NEG_INF), axis=0, keepdims=True)
    e = jnp.where(mask, jnp.exp(s - m), 0.0)
    den = jnp.sum(e, axis=0, keepdims=True)
    return e * jnp.where(den > 0.0, 1.0 / den, 0.0)


def _nsa_prompt_body(qt_ref, gt_ref, kc_ref, vct_ref, ovt_ref, ks_ref, vst_ref, kw_ref, vwt_ref, o_ref, sel_ref,
                     *, n_cmp, n_sel, qb_size):
    qb = pl.program_id(1)
    tq = qb_size
    n_chunks = kc_ref.shape[1]
    qpos = qb * tq + lax.broadcasted_iota(jnp.int32, (1, tq), 1)
    qpos2 = jnp.concatenate([qpos, qpos], axis=1)
    zeros_q = jnp.zeros((HEAD_DIM, 2 * tq), F32)
    gates = jax.nn.sigmoid(gt_ref[0])
    kc = kc_ref[0]
    qpos4 = jnp.concatenate([qpos2, qpos2], axis=1)
    n_idx = lax.broadcasted_iota(jnp.int32, (n_chunks, 4 * tq), 0)
    cmp_mask = (n_idx * CMP_STRIDE + (CMP_BLOCK - 1) <= qpos4) & (n_idx < n_cmp)
    blk = lax.broadcasted_iota(jnp.int32, (n_sel, tq), 0)
    cur = qpos // SEL_BLOCK
    forced = (blk == 0) | (blk == cur) | (blk == cur - 1)
    causal_blk = blk * SEL_BLOCK <= qpos

    qpads32 = []
    for g in range(N_KV_A):
        q64 = jnp.concatenate([qt_ref[0, (2 * g) * HEAD_DIM:(2 * g + 1) * HEAD_DIM, :],
                               qt_ref[0, (2 * g + 1) * HEAD_DIM:(2 * g + 2) * HEAD_DIM, :]], axis=1) * (HEAD_DIM ** -0.5)
        qpads32.append(jnp.concatenate([q64, zeros_q], axis=0) if g == 0 else jnp.concatenate([zeros_q, q64], axis=0))
    q_all32 = jnp.concatenate(qpads32, axis=1)
    q_all = q_all32.astype(BF16)

    p_all = _masked_softmax_cols(_dot3(kc, q_all32), cmp_mask)
    o_cmp_all = jnp.dot(vct_ref[0].astype(BF16), p_all.astype(BF16), preferred_element_type=F32)
    o_cmps = [o_cmp_all[g * HEAD_DIM:(g + 1) * HEAD_DIM, g * 2 * tq:(g + 1) * 2 * tq] for g in range(N_KV_A)]

    for g in range(N_KV_A):
        psum = p_all[:, g * 2 * tq:g * 2 * tq + tq] + p_all[:, g * 2 * tq + tq:(g + 1) * 2 * tq]
        p1 = psum.astype(BF16)
        p2, p3 = _split_bf16(psum - p1.astype(F32))
        ovb = ovt_ref[...].astype(BF16)
        imp = (jnp.dot(ovb, p1, preferred_element_type=F32) + jnp.dot(ovb, p2, preferred_element_type=F32)
               + jnp.dot(ovb, p3, preferred_element_type=F32))
        imp = jnp.where(forced, FORCED_SCORE, jnp.where(causal_blk, imp, BLOCKED_SCORE))
        ngrp = n_sel // 8
        imp_g = [imp[8 * k:8 * k + 8, :] for k in range(ngrp)]
        rank_g = [jnp.zeros((8, tq), F32) for _ in range(ngrp)]
        for i in range(n_sel):
            row = imp[i:i + 1, :]
            for k in range(ngrp):
                if k > i // 8:
                    beats = row >= imp_g[k]
                elif k < i // 8:
                    beats = row > imp_g[k]
                else:
                    beats = (row > imp_g[k]) | ((row == imp_g[k]) & (blk[8 * k:8 * k + 8, :] > i))
                rank_g[k] = rank_g[k] + jnp.where(beats, 1.0, 0.0)
        rank = jnp.concatenate(rank_g, axis=0)
        sel_ref[g] = jnp.where(rank < float(min(TOP_K, n_sel)), 0.0, NEG_INF)


    def attend(j, carry, k_ref, vt_ref, use_sel, tk, causal=True):
        m, l, acc = carry
        off = pl.multiple_of(j * tk, tk)
        s = jnp.dot(k_ref[0, pl.ds(off, tk), :].astype(BF16), q_all, preferred_element_type=F32)
        if causal:
            diff = qpos4 - (off + lax.broadcasted_iota(jnp.int32, (tk, 4 * tq), 0))
        if use_sel:
            per_tile = tk // SEL_BLOCK
            biases = []
            for g in range(N_KV_A):
                rows = [jnp.broadcast_to(sel_ref[g, pl.ds(j * per_tile + a, 1), :], (SEL_BLOCK, tq)) for a in range(per_tile)]
                bias = jnp.concatenate(rows, axis=0)
                biases += [bias, bias]
            sm = s + jnp.concatenate(biases, axis=1)
            if causal:
                sm = jnp.where(diff >= 0, sm, NEG_INF)
            m_new = jnp.maximum(m, jnp.max(sm, axis=0, keepdims=True))
            e = jnp.exp(sm - m_new)
        else:
            mask = (diff >= 0) & (diff < WINDOW)
            m_new = jnp.maximum(m, jnp.max(jnp.where(mask, s, NEG_INF), axis=0, keepdims=True))
            e = jnp.where(mask, jnp.exp(s - m_new), 0.0)
        alpha = jnp.exp(m - m_new)
        l_new = alpha * l + jnp.sum(e, axis=0, keepdims=True)
        vt = vt_ref[0, :, pl.ds(off, tk)].astype(BF16)
        return m_new, l_new, alpha * acc + jnp.dot(vt, e.astype(BF16), preferred_element_type=F32)

    init = (jnp.full((1, 4 * tq), NEG_INF, F32), jnp.zeros((1, 4 * tq), F32), jnp.zeros((2 * HEAD_DIM, 4 * tq), F32))
    tk_s, tk_w = 4 * tq, 2 * tq
    last_s = (qb * tq) // tk_s
    slc = functools.partial(attend, k_ref=ks_ref, vt_ref=vst_ref, use_sel=True, tk=tk_s)
    _, l_s, acc_s = slc(last_s, lax.fori_loop(0, last_s, functools.partial(slc, causal=False), init))
    t_keys = kw_ref.shape[1]
    tk_w = min(WINDOW + tq, t_keys)
    start = pl.multiple_of(jnp.clip(qb * tq - WINDOW, 0, t_keys - tk_w), tq)
    s_w = jnp.dot(kw_ref[0, pl.ds(start, tk_w), :].astype(BF16), q_all, preferred_element_type=F32)
    diff_w = qpos4 - (start + lax.broadcasted_iota(jnp.int32, (tk_w, 4 * tq), 0))
    mask_w = (diff_w >= 0) & (diff_w < WINDOW)
    m_w = jnp.max(jnp.where(mask_w, s_w, NEG_INF), axis=0, keepdims=True)
    e_w = jnp.where(mask_w, jnp.exp(s_w - m_w), 0.0)
    l_w = jnp.sum(e_w, axis=0, keepdims=True)
    acc_w = jnp.dot(vwt_ref[0, :, pl.ds(start, tk_w)].astype(BF16), e_w.astype(BF16), preferred_element_type=F32)

    for g in range(N_KV_A):
        blk_g = (slice(g * HEAD_DIM, (g + 1) * HEAD_DIM), slice(g * 2 * tq, (g + 1) * 2 * tq))
        o_slc = acc_s[blk_g] / l_s[:, blk_g[1]]
        o_win = acc_w[blk_g] / l_w[:, blk_g[1]]
        for r in range(R_A):
            h = 2 * g + r
            gr = gates[3 * h:3 * h + 3, :]
            sl = slice(r * tq, (r + 1) * tq)
            o_ref[0, h * HEAD_DIM:(h + 1) * HEAD_DIM, :] = (gr[0:1] * o_cmps[g][:, sl] + gr[1:2] * o_slc[:, sl]
                                                          + gr[2:3] * o_win[:, sl])


def _nsa_prompt(qt, gt, kc, vct, ovt, kv, vst, vwt, n_cmp):
    b, _, t = qt.shape
    tq = 128
    n_sel = t // SEL_BLOCK
    n_chunks = kc.shape[1]
    per_b = lambda i, j: (i, 0, 0)
    return pl.pallas_call(
        functools.partial(_nsa_prompt_body, n_cmp=n_cmp, n_sel=n_sel, qb_size=tq),
        grid=(b, t // tq),
        in_specs=[pl.BlockSpec((1, 256, tq), lambda i, j: (i, 0, j)),
                  pl.BlockSpec((1, 16, tq), lambda i, j: (i, 0, j)),
                  pl.BlockSpec((1, n_chunks, LANE), per_b),
                  pl.BlockSpec((1, LANE, n_chunks), per_b),
                  pl.BlockSpec((n_sel, n_chunks), lambda i, j: (0, 0)),
                  pl.BlockSpec((1, t, LANE), lambda i, j: (i, 0, 2)),
                  pl.BlockSpec((1, LANE, t), per_b),
                  pl.BlockSpec((1, t, LANE), lambda i, j: (i, 0, 4)),
                  pl.BlockSpec((1, LANE, t), per_b)],
        out_specs=pl.BlockSpec((1, 256, tq), lambda i, j: (i, 0, j)),
        out_shape=jax.ShapeDtypeStruct((b, 256, t), F32),
        scratch_shapes=[pltpu.VMEM((N_KV_A, n_sel, tq), F32)],
        compiler_params=_cp("parallel", "arbitrary"),
        name="nsa_prompt",
    )(qt, gt, kc, vct, ovt, kv, vst, kv, vwt)


def _nsa_prompt_mixer(q, kv, gate, lw, b, t):
    kv3 = kv.reshape(b, t, 6 * LANE)
    n_chunks = t // CMP_STRIDE
    n_cmp = (t - CMP_BLOCK) // CMP_STRIDE + 1
    kc, _, vct = _nsa_compress(kv3, *lw['cmp'])
    ovt = _cmp_to_sel_t(n_chunks, n_cmp, t // SEL_BLOCK)
    qt = jnp.swapaxes(q.reshape(b, t, 256), 1, 2)
    gt = jnp.swapaxes(gate.reshape(b, t, LANE)[:, :, :16], 1, 2)
    vst = jnp.swapaxes(kv3[:, :, 3 * LANE:4 * LANE], 1, 2)
    vwt = jnp.swapaxes(kv3[:, :, 5 * LANE:], 1, 2)
    ot = _nsa_prompt(qt, gt, kc, vct, ovt, kv3, vst, vwt, n_cmp)
    return jnp.swapaxes(ot, 1, 2).reshape(b * t, 256)


def _softmax_rows_with_extra(s, mask, s_new):
    m = jnp.maximum(jnp.max(jnp.where(mask, s, NEG_INF), axis=-1, keepdims=True), s_new)
    e = jnp.where(mask, jnp.exp(s - m), 0.0)
    e_new = jnp.exp(s_new - m)
    return e, e_new, 1.0 / (jnp.sum(e, axis=-1, keepdims=True) + e_new)


def _nsa_sample_body(pt_ref, *refs, n_pages, page, pos, n_sel, n_cmp, wb):
    del pt_ref
    n_in = n_pages
    pages = refs[:n_in]
    (qbd_ref, new_ref, gate_ref, win_ref, ov_ref, w1_ref, b1_ref, w2_ref, w1k_ref, w2k_ref,
     o_ref, nw_ref, tok_ref, xc_ref) = refs[n_in:]
    pg = lambda p, kind: pages[p].at[0, kind:kind + 1]
    nck = n_pages * page // CMP_STRIDE
    lane8 = lax.broadcasted_iota(jnp.int32, (8, LANE), 1)
    lo8 = lane8 < HEAD_DIM
    nt = lambda a, b: lax.dot_general(a, b, (((1,), (1,)), ((), ())), preferred_element_type=F32)

    lo_c = lax.broadcasted_iota(jnp.int32, (nck, LANE), 1) < HEAD_DIM
    for kind in range(2):
        for p in range(n_pages):
            tok_ref[kind, p * page:(p + 1) * page, :] = pg(p, kind)[0].T
        for pair in range(CMP_STRIDE // 2):
            a = tok_ref[kind, pl.ds(2 * pair, nck, stride=CMP_STRIDE), :]
            b = tok_ref[kind, pl.ds(2 * pair + 1, nck, stride=CMP_STRIDE), :]
            cols = slice(LANE * pair, LANE * (pair + 1))
            xc_ref[kind, 0:nck, cols] = jnp.where(lo_c, a, pltpu.roll(b, HEAD_DIM, 1))
            xc_ref[kind, nck:2 * nck, cols] = jnp.where(lo_c, pltpu.roll(a, HEAD_DIM, 1), b)
    kc, vc = [_cmp_mlp(xc_ref[kind], kind, w1_ref, b1_ref, w2_ref, w1k_ref, w2k_ref, nck) for kind in range(2)]

    q = qbd_ref[0] * (HEAD_DIM ** -0.5)
    qb = q.astype(BF16)
    new = new_ref[0]

    n_idx = lax.broadcasted_iota(jnp.int32, (8, nck), 1)
    cmask = (n_idx * CMP_STRIDE + (CMP_BLOCK - 1) <= pos) & (n_idx < n_cmp)
    s = _dot3(q, kc, _NT)
    m = jnp.max(jnp.where(cmask, s, NEG_INF), axis=-1, keepdims=True)
    e = jnp.where(cmask, jnp.exp(s - m), 0.0)
    den = jnp.sum(e, axis=-1, keepdims=True)
    p_cmp = e * jnp.where(den > 0.0, 1.0 / den, 0.0)
    o_cmp = jnp.dot(p_cmp.astype(BF16), vc.astype(BF16), preferred_element_type=F32)

    row8 = lax.broadcasted_iota(jnp.int32, (8, nck), 0)
    psum = jnp.where(row8 == 0, p_cmp[0:1] + p_cmp[1:2], jnp.where(row8 == 1, p_cmp[2:3] + p_cmp[3:4], 0.0))
    imp = jnp.dot(psum, ov_ref[...], precision=HI, preferred_element_type=F32)
    cur = pos // SEL_BLOCK
    forced = (lane8 == 0) | (lane8 == cur) | (lane8 == cur - 1)
    imp = jnp.where(forced, FORCED_SCORE, jnp.where(lane8 * SEL_BLOCK <= pos, imp, BLOCKED_SCORE))
    imp = jnp.where(lane8 < n_sel, imp, -3e38)
    rank = jnp.zeros((8, LANE), F32)
    for i in range(n_sel):
        col = imp[:, i:i + 1]
        rank = rank + jnp.where((col > imp) | ((col == imp) & (lane8 > i)), 1.0, 0.0)
    sel = jnp.where((rank < float(min(TOP_K, n_sel))) & (lane8 < n_sel), 1.0, 0.0)
    rsel = lax.broadcasted_iota(jnp.int32, (8, LANE), 0)
    selh = jnp.where(rsel < R_A, sel[0:1], jnp.where(rsel < 2 * R_A, sel[1:2], 0.0))

    per_page = page // SEL_BLOCK
    s_t, m_t = [], []
    for p in range(n_pages):
        s_t.append(jnp.dot(qb, pg(p, 2)[0].astype(BF16), preferred_element_type=F32))
        blk_sel = selh[:, per_page * p:per_page * p + 1]
        for a in range(1, per_page):
            blk_sel = jnp.where(lane8 < a * SEL_BLOCK, blk_sel, selh[:, per_page * p + a:per_page * p + a + 1])
        kpos = p * page + lane8
        m_t.append((blk_sel > 0.0) & (kpos <= pos))
    s_all = jnp.concatenate(s_t, axis=1)
    mk_all = jnp.concatenate(m_t, axis=1)
    s_new = jnp.sum(q * new[2:3], axis=-1, keepdims=True)
    e, e_new, inv = _softmax_rows_with_extra(s_all, mk_all, s_new)
    acc = e_new * new[3:4]
    for p in range(n_pages):
        acc = acc + nt(e[:, p * page:(p + 1) * page].astype(BF16), pg(p, 3)[0].astype(BF16))
    o_slc = acc * inv

    kw, vw = win_ref[0], win_ref[1]
    widx = lax.broadcasted_iota(jnp.int32, (8, wb), 1)
    diff = wb - widx
    s_w = jnp.dot(qb, kw.astype(BF16), preferred_element_type=F32)
    s_wnew = jnp.sum(q * new[4:5], axis=-1, keepdims=True)
    e, e_new, inv = _softmax_rows_with_extra(s_w, (diff >= 0) & (diff < WINDOW), s_wnew)
    o_win = (nt(e.astype(BF16), vw.astype(BF16)) + e_new * new[5:6]) * inv

    gts = jax.nn.sigmoid(gate_ref[0])
    o = gts[:, 0:1] * o_cmp + gts[:, 1:2] * o_slc + gts[:, 2:3] * o_win
    lo1 = lo8[0:1]
    o_ref[0] = jnp.concatenate([jnp.where(lo1, o[0:1], pltpu.roll(o[1:2], HEAD_DIM, 1)),
                                jnp.where(lo1, pltpu.roll(o[2:3], HEAD_DIM, 1), o[3:4])], axis=1)
    last = lax.broadcasted_iota(jnp.int32, (LANE, wb), 1) == wb - 1
    new_t = new.T
    nw_ref[0] = jnp.where(last, new_t[:, 4:5], pltpu.roll(kw, wb - 1, 1))
    nw_ref[1] = jnp.where(last, new_t[:, 5:6], pltpu.roll(vw, wb - 1, 1))


def _nsa_sample(q, kv, gate, cache_kv, cache_win, layer, page_table, cmp_w, pos):
    b = q.shape[0]
    n_pool, page = cache_kv.shape[1:3]
    n_pages = page_table.shape[1]
    wb = cache_win.shape[2]
    assert wb == WINDOW and page % CMP_STRIDE == 0 and page % SEL_BLOCK == 0 and page == LANE
    tk = n_pages * page + 1
    n_cmp = (tk - CMP_BLOCK) // CMP_STRIDE + 1
    nck = n_pages * page // CMP_STRIDE
    assert n_cmp <= nck
    n_sel = -(-tk // SEL_BLOCK)
    assert n_sel <= LANE and pos // SEL_BLOCK == n_sel - 1
    ov = _pad_to(_cmp_to_sel_t(nck, n_cmp, n_sel).T, LANE, 1)
    w1cat, b1, w2bd, w1k, w2k = cmp_w
    q4 = q.reshape(b, H_A, HEAD_DIM)
    z = jnp.zeros_like(q4)
    first = (jnp.arange(H_A) // R_A == 0)[None, :, None]
    qbd = jnp.concatenate([jnp.where(first, q4, z), jnp.where(first, z, q4)], axis=-1)
    qbd = _pad_to(qbd, 8, 1)
    new = _pad_to(kv.reshape(b, 6, LANE), 8, 1)
    g8 = _pad_to(_pad_to(gate[:, :3 * H_A].reshape(b, H_A, 3), LANE, 2), 8, 1)
    cache3 = jnp.transpose(cache_kv, (0, 1, 3, 4, 5, 2)).reshape(-1, 4, LANE, page)
    win3 = jnp.transpose(cache_win, (0, 1, 3, 4, 5, 2)).reshape(-1, LANE, wb)
    page_specs = [pl.BlockSpec((1, 4, LANE, page), functools.partial(lambda i, pt, p: (layer * n_pool + pt[i, p], 0, 0, 0), p=p))
                  for p in range(n_pages)]
    per_b = lambda i, pt: (i, 0, 0)
    fixed2 = lambda i, pt: (0, 0)
    fixed3 = lambda i, pt: (0, 0, 0)
    grid_spec = pltpu.PrefetchScalarGridSpec(
        num_scalar_prefetch=1,
        grid=(b,),
        in_specs=page_specs + [pl.BlockSpec((1, 8, LANE), per_b), pl.BlockSpec((1, 8, LANE), per_b), pl.BlockSpec((1, 8, LANE), per_b),
                               pl.BlockSpec((2, LANE, wb), lambda i, pt: (layer * b + i, 0, 0)), pl.BlockSpec((nck, LANE), fixed2),
                               pl.BlockSpec((2, CMP_STRIDE * HEAD_DIM, 2 * CMP_HIDDEN), fixed3),
                               pl.BlockSpec((2, 1, CMP_HIDDEN), fixed3), pl.BlockSpec((2, 2 * CMP_HIDDEN, LANE), fixed3),
                               pl.BlockSpec((2, CMP_STRIDE * HEAD_DIM, 2 * CMP_HIDDEN), fixed3),
                               pl.BlockSpec((2, 2 * CMP_HIDDEN, LANE), fixed3)],
        out_specs=[pl.BlockSpec((1, 1, 256), per_b), pl.BlockSpec((2, LANE, wb), per_b)],
        scratch_shapes=[pltpu.VMEM((2, n_pages * page, LANE), F32), pltpu.VMEM((2, 2 * nck, CMP_STRIDE * HEAD_DIM), F32)],
    )
    o, nw = pl.pallas_call(
        functools.partial(_nsa_sample_body, n_pages=n_pages, page=page, pos=pos, n_sel=n_sel, n_cmp=n_cmp, wb=wb),
        grid_spec=grid_spec,
        out_shape=[jax.ShapeDtypeStruct((b, 1, 256), F32), jax.ShapeDtypeStruct((2 * b, LANE, wb), F32)],
        compiler_params=_cp("parallel"),
        name="nsa_sample",
    )(page_table, *([cache3] * n_pages), qbd, new, g8, win3, ov, w1cat, b1, w2bd, w1k, w2k)
    return o.reshape(b, 256), jnp.transpose(nw.reshape(b, 2, N_KV_A, HEAD_DIM, wb), (0, 4, 1, 2, 3))


def _rwkv_prep_body(c_ref, s0_ref, mu_ref, vec_ref, wup_ref, aup_ref, gup_ref,
                    r_ref, lw_ref, k_ref, v_ref, kk_ref, ka_ref, g_ref, bonus_ref, carry_ref, *, tiles_per_seq):
    i = pl.program_id(0)
    cols = c_ref[...]
    tm = cols.shape[0]

    @pl.when(i % tiles_per_seq == 0)
    def _():
        carry_ref[...] = s0_ref[0]

    prev = pltpu.roll(cols, 1, 0)
    row0 = lax.broadcasted_iota(jnp.int32, cols.shape, 0) == 0
    prev = jnp.where(row0, carry_ref[...], prev)
    carry_ref[...] = cols[tm - 1:tm, :]
    _rwkv_features(cols, prev, mu_ref, vec_ref, wup_ref, aup_ref, gup_ref,
                   r_ref, lw_ref, k_ref, v_ref, kk_ref, ka_ref, g_ref, bonus_ref)


def _rwkv_step_prep_body(c_ref, s0_ref, mu_ref, vec_ref, wup_ref, aup_ref, gup_ref,
                         r_ref, lw_ref, k_ref, v_ref, kk_ref, ka_ref, g_ref, bonus_ref):
    _rwkv_features(c_ref[...], s0_ref[...], mu_ref, vec_ref, wup_ref, aup_ref, gup_ref,
                   r_ref, lw_ref, k_ref, v_ref, kk_ref, ka_ref, g_ref, bonus_ref)


def _rwkv_features(cols, prev, mu_ref, vec_ref, wup_ref, aup_ref, gup_ref,
                   r_ref, lw_ref, k_ref, v_ref, kk_ref, ka_ref, g_ref, bonus_ref):
    xs = cols + mu_ref[...] * (prev - cols)
    r, k, v, lo = xs[:, 0:256], xs[:, 256:512], xs[:, 512:768], xs[:, 768:896]
    vec = vec_ref[...]
    w0, a0, k_k, k_a, r_k = vec[0:1], vec[1:2], vec[2:3], vec[3:4], vec[4:5]
    w_log = -jax.nn.softplus(-(w0 + _dot3(jnp.tanh(lo), wup_ref[...]))) - 0.5
    a = jax.nn.sigmoid(a0 + _dot3(lo, aup_ref[...]))
    g_ref[...] = _dot3(jax.nn.sigmoid(lo), gup_ref[...])
    ones = _block_ones(256, HEAD_DIM, BF16)
    kk = k * k_k
    kk = kk * lax.rsqrt(_dot_bf16_exact_rhs(kk * kk, ones) + 1e-12)
    k2 = k * (1.0 + (a - 1.0) * k_a)
    r_ref[...] = r
    lw_ref[...] = -jnp.exp(w_log)
    k_ref[...] = k2
    v_ref[...] = v
    kk_ref[...] = kk
    ka_ref[...] = kk * a
    bonus_ref[...] = _dot_bf16_exact_rhs(r * k2 * r_k, ones) * v


def _rwkv_prep(colsb, shift0, lw, t, tm):
    n = colsb.shape[0]
    tiles_per_seq = t // tm
    row = lambda i: (i, 0)
    fixed = lambda i: (0, 0)
    outs = [jax.ShapeDtypeStruct((n, 256), F32)] * 8
    return pl.pallas_call(
        functools.partial(_rwkv_prep_body, tiles_per_seq=tiles_per_seq),
        grid=(n // tm,),
        in_specs=[pl.BlockSpec((tm, SHIFT_PAD), row),
                  pl.BlockSpec((1, 1, SHIFT_PAD), lambda i: (i // tiles_per_seq, 0, 0)),
                  pl.BlockSpec((1, SHIFT_PAD), fixed), pl.BlockSpec((8, 256), fixed),
                  pl.BlockSpec((LANE, 256), fixed), pl.BlockSpec((LANE, 256), fixed), pl.BlockSpec((LANE, 256), fixed)],
        out_specs=[pl.BlockSpec((tm, 256), row)] * 8,
        out_shape=outs,
        scratch_shapes=[pltpu.VMEM((1, SHIFT_PAD), F32)],
        compiler_params=_cp("arbitrary"),
        name="rwkv_prep",
    )(colsb, shift0.reshape(-1, 1, SHIFT_PAD), lw['rwkv_mu'], lw['rwkv_vec'], lw['rwkv_wup'], lw['rwkv_aup'], lw['rwkv_gup'])


RWKV_CHUNK = 64


def _split_bf16(x):
    hi = x.astype(BF16)
    return hi, (x - hi.astype(F32)).astype(BF16)


def _dot3(a, b, dims=(((1,), (0,)), ((), ()))):
    ah, al = _split_bf16(a)
    bh, bl = _split_bf16(b)
    dg = lambda x, y: lax.dot_general(x, y, dims, preferred_element_type=F32)
    return dg(ah, bh) + dg(ah, bl) + dg(al, bh)


def _dot3_presplit(x, w_ref):
    xh, xl = _split_bf16(x)
    return (jnp.dot(xh, w_ref[0], preferred_element_type=F32) + jnp.dot(xh, w_ref[1], preferred_element_type=F32)
            + jnp.dot(xl, w_ref[0], preferred_element_type=F32))


def _dot_bf16_exact_rhs(x, m_bf16):
    hi, lo = _split_bf16(x)
    return jnp.dot(hi, m_bf16, preferred_element_type=F32) + jnp.dot(lo, m_bf16, preferred_element_type=F32)


_NT = (((1,), (1,)), ((), ()))


def _rwkv_chunk_body(r_ref, lw_ref, k_ref, v_ref, kk_ref, ka_ref, g_ref, bonus_ref, s0_ref, vec_ref,
                     o_ref, st_ref, s_scr, y_scr, *, nb, tl):
    L = RWKV_CHUNK
    nc = tl // L

    @pl.when(pl.program_id(1) == 0)
    def _():
        s_scr[...] = s0_ref[...]

    ri = lax.broadcasted_iota(jnp.int32, (L, L), 0)
    ci = lax.broadcasted_iota(jnp.int32, (L, L), 1)
    strict, incl = ri > ci, ri >= ci
    ltri = incl.astype(F32)
    eye = (ri == ci).astype(F32)

    bnn = (((2,), (1,)), ((0,), (0,)))
    bnt = (((2,), (2,)), ((0,), (0,)))

    def chunk(c, carry):
        rows = pl.ds(pl.multiple_of(c * L, L), L)
        lhs_l, rhs_l, v_l, kw_l, wl_l = [], [], [], [], []
        for b in range(nb):
            r, lw, k, v, kk, ka = [ref[b, rows, :] for ref in (r_ref, lw_ref, k_ref, v_ref, kk_ref, ka_ref)]
            cl = jnp.dot(ltri, lw, precision=HI, preferred_element_type=F32)
            e_neg = jnp.exp(-cl)
            e_rem = jnp.exp(cl[L - 1:L, :] - cl)
            kkd, rd = kk * jnp.exp(cl - lw), r * jnp.exp(cl)
            kinv, kainv, kw, kaw = k * e_neg, ka * e_neg, k * e_rem, ka * e_rem
            w_last = jnp.exp(cl[L - 1:L, :])
            for h in range(H_B):
                sl = slice(h * HEAD_DIM, (h + 1) * HEAD_DIM)
                lhs_l.append(jnp.concatenate([kkd[:, sl], rd[:, sl]], axis=0))
                rhs_l.append(jnp.concatenate([kinv[:, sl], kainv[:, sl]], axis=0))
                v_l.append(v[:, sl])
                kw_l.append(jnp.concatenate([kw[:, sl], kaw[:, sl]], axis=0))
                wl_l.append(w_last[:, sl])
        lhs, rhs, vs, kws, wl = [jnp.stack(x) for x in (lhs_l, rhs_l, v_l, kw_l, wl_l)]
        gm = _dot3(lhs, rhs, bnt)
        a_vk = jnp.where(strict, gm[:, :L, :L], 0.0)
        n1 = jnp.where(strict, -gm[:, :L, L:], 0.0)
        t_inv, pw = eye + n1, n1
        for _ in range(L.bit_length() - 2):
            pw = _dot3(pw, pw, bnn)
            t_inv = _dot3(t_inv, eye + pw, bnn)
        s = s_scr[...].reshape(nb * H_B, HEAD_DIM, HEAD_DIM)
        xs = _dot3(lhs, s, bnt)
        u = _dot3(t_inv, xs[:, :L] + _dot3(a_vk, vs, bnn), bnn)
        b_vk = jnp.where(incl, gm[:, L:, :L], 0.0).astype(BF16)
        b_uk = jnp.where(incl, gm[:, L:, L:], 0.0).astype(BF16)
        y = (xs[:, L:] + lax.dot_general(b_vk, vs.astype(BF16), bnn, preferred_element_type=F32)
             - lax.dot_general(b_uk, u.astype(BF16), bnn, preferred_element_type=F32))
        vu_t = jnp.stack([jnp.concatenate([vs[n], -u[n]], axis=0).T for n in range(nb * H_B)])
        s_new = s * wl + _dot3(vu_t, kws, bnn)
        s_scr[...] = s_new.reshape(nb, H_B, HEAD_DIM, HEAD_DIM)
        for b in range(nb):
            for h in range(H_B):
                y_scr[b, rows, h * HEAD_DIM:(h + 1) * HEAD_DIM] = y[b * H_B + h]
        return carry

    lax.fori_loop(0, nc, chunk, 0)
    st_ref[...] = s_scr[...]
    vec = vec_ref[...]
    for b in range(nb):
        o_ref[b] = (_segment_norm(y_scr[b], RWKV_GN_EPS) * vec[5:6] + vec[6:7] + bonus_ref[b]) * g_ref[b]


def _rwkv_chunked(prep, s0, vec, b, t, nb, tl):
    arrs = [a.reshape(b, t, 256) for a in prep]
    seq = pl.BlockSpec((nb, tl, 256), lambda i, j: (i, j, 0))
    st = pl.BlockSpec((nb, H_B, HEAD_DIM, HEAD_DIM), lambda i, j: (i, 0, 0, 0))
    return pl.pallas_call(
        functools.partial(_rwkv_chunk_body, nb=nb, tl=tl),
        grid=(b // nb, t // tl),
        in_specs=[seq] * 8 + [st, pl.BlockSpec((8, 256), lambda i, j: (0, 0))],
        out_specs=[seq, st],
        out_shape=[jax.ShapeDtypeStruct((b, t, 256), F32), jax.ShapeDtypeStruct((b, H_B, HEAD_DIM, HEAD_DIM), F32)],
        scratch_shapes=[pltpu.VMEM((nb, H_B, HEAD_DIM, HEAD_DIM), F32), pltpu.VMEM((nb, tl, 256), F32)],
        compiler_params=_cp("parallel", "arbitrary"),
        name="rwkv_chunked",
    )(*arrs, s0, vec)


def _rwkv_mixer(colsb, shift0, s0, lw, b, t, tm, nb, tl):
    prep = _rwkv_prep(colsb, shift0, lw, t, tm)
    o, st = _rwkv_chunked(prep, s0, lw['rwkv_vec'], b, t, nb, tl)
    shift = colsb.reshape(b, t, SHIFT_PAD)[:, -1, :SHIFT_B]
    return o.reshape(b * t, 256), st, shift


def _segment_norm(y, eps):
    avg = _block_ones(256, HEAD_DIM, BF16) * (1.0 / HEAD_DIM)
    yc = y - _dot_bf16_exact_rhs(y, avg)
    return yc * lax.rsqrt(_dot_bf16_exact_rhs(yc * yc, avg) + eps)


def _rwkv_step_body(r_ref, lw_ref, k_ref, v_ref, kk_ref, ka_ref, g_ref, bonus_ref, vec_ref, s_ref, o_ref, st_ref, ft_scr, y_scr):
    h = pl.program_id(0)

    @pl.when(h == 0)
    def _():
        for n, ref in enumerate((r_ref, lw_ref, k_ref, v_ref, kk_ref, ka_ref)):
            ft_scr[n] = ref[...].T

    base = pl.multiple_of(h * HEAD_DIM, HEAD_DIM)
    head = lambda n: ft_scr[n, pl.ds(base, HEAD_DIM), :]
    r_t, w_t, k_t, kk_t, ka_t = head(0), jnp.exp(head(1)), head(2), head(4), head(5)

    def body(i, carry):
        rows = pl.ds(pl.multiple_of(i * HEAD_DIM, HEAD_DIM), HEAD_DIM)
        s = s_ref[rows, :]
        sk = jnp.sum(s * kk_t, axis=0, keepdims=True)
        s = s * w_t - sk * ka_t + ft_scr[3, pl.ds(base + i, 1), :] * k_t
        st_ref[rows, :] = s
        y_scr[pl.ds(base + i, 1), :] = jnp.sum(s * r_t, axis=0, keepdims=True)
        return carry

    lax.fori_loop(0, HEAD_DIM, body, 0)

    @pl.when(h == H_B - 1)
    def _():
        vec = vec_ref[...]
        o_ref[...] = (_segment_norm(y_scr[...].T, RWKV_GN_EPS) * vec[5:6] + vec[6:7] + bonus_ref[...]) * g_ref[...]


def _rwkv_step(colsb, shift0, s_all, layer, lw):
    b = colsb.shape[0]
    hd2 = HEAD_DIM * HEAD_DIM
    full = lambda *_: (0, 0)
    feat = pl.BlockSpec((b, 256), full)
    prep = pl.pallas_call(
        _rwkv_step_prep_body,
        grid=(1,),
        in_specs=[pl.BlockSpec((b, SHIFT_PAD), full), pl.BlockSpec((b, SHIFT_PAD), full), pl.BlockSpec((1, SHIFT_PAD), full),
                  pl.BlockSpec((8, 256), full)] + [pl.BlockSpec((LANE, 256), full)] * 3,
        out_specs=[feat] * 8,
        out_shape=[jax.ShapeDtypeStruct((b, 256), F32)] * 8,
        compiler_params=_cp("arbitrary"),
        name="rwkv_step_prep",
    )(colsb, shift0, lw['rwkv_mu'], lw['rwkv_vec'], lw['rwkv_wup'], lw['rwkv_aup'], lw['rwkv_gup'])
    s_rows = jnp.transpose(s_all, (0, 2, 3, 4, 1)).reshape(-1, b)
    o, st = pl.pallas_call(
        _rwkv_step_body,
        grid=(H_B,),
        in_specs=[feat] * 8 + [pl.BlockSpec((8, 256), full), pl.BlockSpec((hd2, b), lambda h: (layer * H_B + h, 0))],
        out_specs=[feat, pl.BlockSpec((hd2, b), lambda h: (h, 0))],
        out_shape=[jax.ShapeDtypeStruct((b, 256), F32), jax.ShapeDtypeStruct((H_B * hd2, b), F32)],
        scratch_shapes=[pltpu.VMEM((6, 256, b), F32), pltpu.VMEM((256, b), F32)],
        compiler_params=_cp("arbitrary"),
        name="rwkv_step",
    )(*prep, lw['rwkv_vec'], s_rows)
    return o, jnp.transpose(st.reshape(H_B, HEAD_DIM, HEAD_DIM, b), (3, 0, 1, 2))


def _ret_step_body(c_ref, cos_ref, sin_ref, gn_ref, r0_ref, o_ref, rt_ref, ft_scr, acc_scr):
    h = pl.program_id(0)

    @pl.when(h == 0)
    def _():
        x = c_ref[...]
        cs, sn = cos_ref[...], sin_ref[...]
        first = (lax.broadcasted_iota(jnp.int32, cs.shape, 1) % HEAD_DIM) < (HEAD_DIM // 2)

        def rope(z):
            sw = jnp.where(first, pltpu.roll(z, 256 - HEAD_DIM // 2, 1), pltpu.roll(z, HEAD_DIM // 2, 1))
            return z * cs + sw * sn

        ft_scr[0] = rope(x[:, 0:256]).T
        ft_scr[1] = (rope(x[:, 256:512]) * (HEAD_DIM ** -0.5)).T
        ft_scr[2] = x[:, 512:768].T

    base = pl.multiple_of(h * HEAD_DIM, HEAD_DIM)
    nseq = rt_ref.shape[1]
    gamma = jnp.exp(jnp.log1p(-jnp.exp2(-5.0 - jnp.full((1, nseq), h, jnp.int32).astype(F32))))
    q_t, k_t, v_t = [ft_scr[n, pl.ds(base, HEAD_DIM), :] for n in range(3)]
    qk = jnp.sum(q_t * k_t, axis=0, keepdims=True)

    def body(d, cross):
        rows = pl.ds(pl.multiple_of(d * HEAD_DIM, HEAD_DIM), HEAD_DIM)
        r_old = r0_ref[rows, :]
        rt_ref[rows, :] = gamma * r_old + ft_scr[1, pl.ds(base + d, 1), :] * v_t
        return cross + ft_scr[0, pl.ds(base + d, 1), :] * r_old

    cross = lax.fori_loop(0, HEAD_DIM, body, jnp.zeros((HEAD_DIM, nseq), F32))
    acc_scr[pl.ds(base, HEAD_DIM), :] = qk * v_t + gamma * cross

    @pl.when(h == H_D - 1)
    def _():
        o_ref[...] = jax.nn.silu(c_ref[:, 768:1024]) * (_segment_norm(acc_scr[...].T, RET_GN_EPS) * gn_ref[...])


def _ret_step(colsd, cos, sin, r_all, layer, gn):
    b = colsd.shape[0]
    hd2 = HEAD_DIM * HEAD_DIM
    full = lambda *_: (0, 0)
    r_rows = jnp.transpose(r_all, (0, 2, 3, 4, 1)).reshape(-1, b)
    o, rt = pl.pallas_call(
        _ret_step_body,
        grid=(H_D,),
        in_specs=[pl.BlockSpec((b, 1024), full), pl.BlockSpec((b, 256), full), pl.BlockSpec((b, 256), full),
                  pl.BlockSpec((1, 256), full), pl.BlockSpec((hd2, b), lambda h: (layer * H_D + h, 0))],
        out_specs=[pl.BlockSpec((b, 256), full), pl.BlockSpec((hd2, b), lambda h: (h, 0))],
        out_shape=[jax.ShapeDtypeStruct((b, 256), F32), jax.ShapeDtypeStruct((H_D * hd2, b), F32)],
        scratch_shapes=[pltpu.VMEM((3, 256, b), F32), pltpu.VMEM((256, b), F32)],
        compiler_params=_cp("arbitrary"),
        name="ret_step",
    )(colsd, cos, sin, gn.reshape(1, 256), r_rows)
    return o, jnp.transpose(rt.reshape(H_D, HEAD_DIM, HEAD_DIM, b), (3, 0, 1, 2))


def _s5_step_body(u_ref, x0_ref, a1_ref, a2_ref, bt_ref, ct_ref, d_ref, wg_ref, o_ref, xt_ref):
    u = u_ref[...]
    dot_hi = lambda a, b: jnp.dot(a, b, precision=HI, preferred_element_type=F32)
    x0 = x0_ref[...]
    n = x0.shape[0]
    even = (lax.broadcasted_iota(jnp.int32, x0.shape, 0) % 2) == 0
    partner = jnp.where(even, pltpu.roll(x0, n - 1, 0), pltpu.roll(x0, 1, 0))
    x = a1_ref[...] * x0 + a2_ref[...] * partner + dot_hi(bt_ref[...], u.T)
    xt_ref[...] = x
    y = dot_hi(ct_ref[...], x).T + d_ref[...] * u
    z = jax.nn.gelu(y)
    o_ref[...] = z * jax.nn.sigmoid(dot_hi(z, wg_ref[...]))


def _s5_step(u, x_all, layer, lw):
    b = u.shape[0]
    a1, a2, bt, ct, d_row = lw['s5_step']
    n = 2 * G_C * S5_P
    full = lambda *_: (0, 0)
    x_rows = jnp.transpose(x_all, (0, 2, 3, 4, 1)).reshape(-1, b)
    o, xt = pl.pallas_call(
        _s5_step_body,
        grid=(1,),
        in_specs=[pl.BlockSpec((b, C_C), full), pl.BlockSpec((n, b), lambda i: (layer, 0)), pl.BlockSpec((n, 1), full),
                  pl.BlockSpec((n, 1), full), pl.BlockSpec((n, C_C), full), pl.BlockSpec((C_C, n), full),
                  pl.BlockSpec((1, C_C), full), pl.BlockSpec((C_C, C_C), full)],
        out_specs=[pl.BlockSpec((b, C_C), full), pl.BlockSpec((n, b), full)],
        out_shape=[jax.ShapeDtypeStruct((b, C_C), F32), jax.ShapeDtypeStruct((n, b), F32)],
        compiler_params=_cp("arbitrary"),
        name="s5_step",
    )(u, x_rows, a1, a2, bt, ct, d_row, lw['s5_w_glu'])
    return o, jnp.transpose(xt.reshape(G_C, S5_P, 2, b), (3, 0, 1, 2))


def _s5_params(lw):
    lr, li = lw['s5_lambda_re'], lw['s5_lambda_im']
    dt = jnp.exp(lw['s5_log_step'])[:, None]
    mag = jnp.exp(lr * dt)
    ar, ai = mag * jnp.cos(li * dt), mag * jnp.sin(li * dt)
    nr, ni = ar - 1.0, ai
    den = lr * lr + li * li
    fr, fi = (nr * lr + ni * li) / den, (ni * lr - nr * li) / den
    b_re, b_im = lw['s5_b'][0], lw['s5_b'][1]
    bbr = fr[..., None] * b_re - fi[..., None] * b_im
    bbi = fr[..., None] * b_im + fi[..., None] * b_re
    eye = jnp.eye(G_C, dtype=F32)
    bd_in = lambda m: jnp.einsum('gpc,gh->gchp', m, eye).reshape(G_C * S5_CH, G_C * S5_P)
    bd_out = lambda m: jnp.einsum('gcp,gh->gphc', m, eye).reshape(G_C * S5_P, G_C * S5_CH)
    b_big = jnp.concatenate([bd_in(bbr), bd_in(bbi)], axis=1)
    c_big = jnp.concatenate([bd_out(lw['s5_c'][0]), -bd_out(lw['s5_c'][1])], axis=0)
    a_row = jnp.concatenate([ar.reshape(1, -1), ai.reshape(1, -1)], axis=1)
    d_row = lw['s5_d'].reshape(1, C_C)
    n = 2 * G_C * S5_P
    a1 = jnp.stack([ar, ar], axis=-1).reshape(n, 1)
    a2 = jnp.stack([-ai, ai], axis=-1).reshape(n, 1)
    bt = jnp.stack([jnp.einsum('gpc,gh->gphc', bbr, eye), jnp.einsum('gpc,gh->gphc', bbi, eye)], axis=2).reshape(n, G_C * S5_CH)
    ct = jnp.stack([jnp.einsum('gcp,gh->gchp', lw['s5_c'][0], eye), -jnp.einsum('gcp,gh->gchp', lw['s5_c'][1], eye)],
                   axis=-1).reshape(G_C * S5_CH, n)
    return (a_row, b_big, c_big, d_row), (a1, a2, bt, ct, d_row)


def _s5_body(u_ref, x0_ref, a_ref, b_ref, c_ref, d_ref, wg_ref, o_ref, xt_ref, x_scr, bu_scr, xs_scr, *, nb, tt, mm_dtype, prec):
    sub = 8

    @pl.when(pl.program_id(0) == 0)
    def _():
        x_scr[...] = jnp.zeros_like(x_scr)
        x_scr[0:nb, :] = x0_ref[...]
        bu_scr[...] = jnp.zeros_like(bu_scr)

    np_ = G_C * S5_P
    ncb = np_ // LANE
    for b in range(nb):
        bu = jnp.dot(u_ref[b].astype(mm_dtype), b_ref[...], precision=prec, preferred_element_type=F32)
        for cb in range(2 * ncb):
            bu_scr[cb, pl.ds(b, tt, stride=sub), :] = bu[:, cb * LANE:(cb + 1) * LANE]
    a = a_ref[...]

    def step(t, x):
        rows = pl.ds(pl.multiple_of(t * sub, sub), sub)
        new = [None] * (2 * ncb)
        for cb in range(ncb):
            re, im = slice(cb * LANE, (cb + 1) * LANE), slice(np_ + cb * LANE, np_ + (cb + 1) * LANE)
            ar, ai, xr, xi = a[:, re], a[:, im], x[:, re], x[:, im]
            new[cb] = ar * xr - ai * xi + bu_scr[cb, rows, :]
            new[ncb + cb] = ar * xi + ai * xr + bu_scr[ncb + cb, rows, :]
            xs_scr[cb, rows, :] = new[cb]
            xs_scr[ncb + cb, rows, :] = new[ncb + cb]
        return jnp.concatenate(new, axis=1)

    x_last = lax.fori_loop(0, tt, step, x_scr[...], unroll=8)
    x_scr[...] = x_last
    xt_ref[...] = x_last[0:nb, :]
    for b in range(nb):
        u = u_ref[b]
        xs = jnp.concatenate([xs_scr[cb, pl.ds(b, tt, stride=sub), :] for cb in range(2 * ncb)], axis=1)
        y = jnp.dot(xs.astype(mm_dtype), c_ref[...], precision=prec, preferred_element_type=F32) + d_ref[...] * u
        z = jax.nn.gelu(y)
        o_ref[b] = z * jax.nn.sigmoid(jnp.dot(z.astype(mm_dtype), wg_ref[...], precision=prec, preferred_element_type=F32))


def _s5_mixer(u, x0, lw, b, t, tt, exact):
    a_row, b_big, c_big, d_row = lw['s5']
    mm_dtype = F32 if exact else BF16
    prec = HI if exact else None
    x0l = jnp.concatenate([x0[..., 0].reshape(b, -1), x0[..., 1].reshape(b, -1)], axis=1)
    np2 = 2 * G_C * S5_P
    fixed = lambda i: (0, 0)
    o, xt = pl.pallas_call(
        functools.partial(_s5_body, nb=b, tt=tt, mm_dtype=mm_dtype, prec=prec),
        grid=(t // tt,),
        in_specs=[pl.BlockSpec((b, tt, C_C), lambda i: (0, i, 0)), pl.BlockSpec((b, np2), fixed),
                  pl.BlockSpec((1, np2), fixed), pl.BlockSpec((C_C, np2), fixed), pl.BlockSpec((np2, C_C), fixed),
                  pl.BlockSpec((1, C_C), fixed), pl.BlockSpec((C_C, C_C), fixed)],
        out_specs=[pl.BlockSpec((b, tt, C_C), lambda i: (0, i, 0)), pl.BlockSpec((b, np2), fixed)],
        out_shape=[jax.ShapeDtypeStruct((b, t, C_C), F32), jax.ShapeDtypeStruct((b, np2), F32)],
        scratch_shapes=[pltpu.VMEM((8, np2), F32), pltpu.VMEM((np2 // LANE, 8 * tt, LANE), F32),
                        pltpu.VMEM((np2 // LANE, 8 * tt, LANE), F32)],
        compiler_params=_cp("arbitrary"),
        name="s5",
    )(u.reshape(b, t, C_C), x0l, a_row, b_big.astype(mm_dtype), c_big.astype(mm_dtype), d_row, lw['s5_w_glu'].astype(mm_dtype))
    xt = xt.reshape(b, 2, G_C, S5_P)
    return o.reshape(b * t, C_C), jnp.stack([xt[:, 0], xt[:, 1]], axis=-1)


def _ret_tables(pos, c):
    cos, sin = _rope_tables(pos, HEAD_DIM, RET_THETA, HEAD_DIM, H_D)
    log_g = jnp.log1p(-jnp.exp2(-5.0 - jnp.arange(H_D, dtype=F32)))
    i = jnp.arange(c, dtype=F32)
    diff = i[:, None] - i[None, :]
    dmat = jnp.where(diff >= 0, jnp.exp(jnp.maximum(diff, 0.0)[None] * log_g[:, None, None]), 0.0).reshape(H_D * c, c)
    q_dec = jnp.repeat(jnp.exp((i + 1.0)[None] * log_g[:, None]).T, HEAD_DIM, axis=1)
    k_dec = jnp.repeat(jnp.exp((c - 1.0 - i)[None] * log_g[:, None]).T, HEAD_DIM, axis=1)
    chunk_dec = jnp.repeat(jnp.exp(c * log_g), HEAD_DIM).reshape(256, 1)
    return cos, sin, dmat, q_dec, k_dec, chunk_dec


def _ret_body(c_ref, cos_ref, sin_ref, dmat_ref, qdec_ref, kdec_ref, cdec_ref, r0_ref, gn_ref, o_ref, rt_ref, r_scr, *, c):
    @pl.when(pl.program_id(1) == 0)
    def _():
        r_scr[...] = r0_ref[0]

    x = c_ref[0]
    q, k, v, g = x[:, 0:256], x[:, 256:512], x[:, 512:768], x[:, 768:1024]
    cs, sn = cos_ref[...], sin_ref[...]
    lane = lax.broadcasted_iota(jnp.int32, (c, 256), 1)
    first = (lane % HEAD_DIM) < (HEAD_DIM // 2)

    def rope(z):
        sw = jnp.where(first, pltpu.roll(z, 256 - HEAD_DIM // 2, 1), pltpu.roll(z, HEAD_DIM // 2, 1))
        return z * cs + sw * sn

    q = rope(q)
    k = rope(k) * (HEAD_DIM ** -0.5)
    head = lane // HEAD_DIM
    kb, vb = k.astype(BF16), v.astype(BF16)
    qstack = jnp.concatenate([jnp.where(head == h, q, 0.0) for h in range(H_D)], axis=0).astype(BF16)
    s = lax.dot_general(qstack, kb, (((1,), (1,)), ((), ())), preferred_element_type=F32) * dmat_ref[...]
    pv = jnp.dot(s.astype(BF16), vb, preferred_element_type=F32)
    inner = jnp.zeros((c, 256), F32)
    for h in range(H_D):
        inner = inner + jnp.where(head == h, pv[h * c:(h + 1) * c], 0.0)
    r_old = r_scr[...]
    cross = jnp.dot((q * qdec_ref[...]).astype(BF16), r_old.astype(BF16), preferred_element_type=F32)
    kv = lax.dot_general((k * kdec_ref[...]).astype(BF16), vb, (((0,), (0,)), ((), ())), preferred_element_type=F32)
    bd = _block_ones(256, HEAD_DIM, F32)
    r_new = cdec_ref[...] * r_old + kv * bd
    r_scr[...] = r_new
    rt_ref[0] = r_new
    o_ref[0] = jax.nn.silu(g) * (_segment_norm(inner + cross, RET_GN_EPS) * gn_ref[...])


def _ret_mixer(colsd, r0, lw, tabs, b, t):
    c = RET_CHUNK if t % RET_CHUNK == 0 else t
    cos, sin, dmat, q_dec, k_dec, chunk_dec = tabs
    eye = jnp.eye(H_D, dtype=F32)
    r0l = jnp.einsum('bhde,hg->bhdge', r0, eye).reshape(b, 256, 256)
    n_t = t // c
    fixed = lambda i, j: (0, 0)
    o, rt = pl.pallas_call(
        functools.partial(_ret_body, c=c),
        grid=(b, n_t),
        in_specs=[pl.BlockSpec((1, c, 1024), lambda i, j: (i, j, 0)),
                  pl.BlockSpec((c, 256), lambda i, j: (j, 0)), pl.BlockSpec((c, 256), lambda i, j: (j, 0)),
                  pl.BlockSpec((H_D * c, c), fixed), pl.BlockSpec((c, 256), fixed), pl.BlockSpec((c, 256), fixed),
                  pl.BlockSpec((256, 1), fixed), pl.BlockSpec((1, 256, 256), lambda i, j: (i, 0, 0)),
                  pl.BlockSpec((1, 256), fixed)],
        out_specs=[pl.BlockSpec((1, c, 256), lambda i, j: (i, j, 0)), pl.BlockSpec((1, 256, 256), lambda i, j: (i, 0, 0))],
        out_shape=[jax.ShapeDtypeStruct((b, t, 256), F32), jax.ShapeDtypeStruct((b, 256, 256), F32)],
        scratch_shapes=[pltpu.VMEM((256, 256), F32)],
        compiler_params=_cp("parallel", "arbitrary"),
        name="retention",
    )(colsd.reshape(b, t, 1024), cos, sin, dmat, q_dec, k_dec, chunk_dec, r0l, lw['ret_gn'].reshape(1, 256))
    rt = jnp.einsum('bhdge,hg->bhde', rt.reshape(b, H_D, HEAD_DIM, H_D, HEAD_DIM), eye)
    return o.reshape(b * t, 256), rt


def _partner(x, d, period):
    pos = lax.broadcasted_iota(jnp.int32, x.shape, 1) % period
    return jnp.where(pos + d < period, pltpu.roll(x, LANE - d, 1), pltpu.roll(x, period - d, 1))


def _out_body(x_ref, oa_ref, ob_ref, oc_ref, od_ref, w_ref, nw_ref, wr_ref, br_ref, x1_ref, h_ref, comb_ref):
    acc = x_ref[...]
    for i, ref in enumerate((oa_ref, ob_ref, oc_ref, od_ref)):
        acc = acc + jnp.dot(ref[...].astype(BF16), w_ref[256 * i:256 * (i + 1), :], preferred_element_type=F32)
    x1_ref[...] = acc
    h = acc * lax.rsqrt(jnp.mean(acc * acc, axis=-1, keepdims=True) + RMS_EPS) * nw_ref[...]
    hb = h.astype(BF16)
    h_ref[...] = hb
    h_lo = (h - hb.astype(F32)).astype(BF16)
    logits = (jnp.dot(hb, wr_ref[0], preferred_element_type=F32) + jnp.dot(hb, wr_ref[1], preferred_element_type=F32)
              + jnp.dot(h_lo, wr_ref[0], preferred_element_type=F32) + br_ref[...])
    le, lg = logits[:, :LANE], logits[:, LANE:]
    lane = lax.broadcasted_iota(jnp.int32, le.shape, 1)
    mg = jnp.max(lg, axis=-1, keepdims=True)
    eg = jnp.exp(lg - mg)
    pg = eg / (jnp.sum(eg, axis=-1, keepdims=True) * (1.0 / 32.0))
    gidx = (lane % N_EXPERTS) // EXP_PER_GROUP
    g_rank = jnp.zeros_like(pg)
    for d in range(1, N_GROUPS):
        other = pltpu.roll(pg, LANE - EXP_PER_GROUP * d, 1)
        wrapped = gidx + d >= N_GROUPS
        g_rank = g_rank + jnp.where((other > pg) | ((other == pg) & wrapped), 1.0, 0.0)
    kidx = lane % EXP_PER_GROUP
    others = [_partner(le, d, EXP_PER_GROUP) for d in range(1, EXP_PER_GROUP)]
    me = functools.reduce(jnp.maximum, others, le)
    ee = jnp.exp(le - me)
    se = ee
    for d in range(1, EXP_PER_GROUP):
        se = se + _partner(ee, d, EXP_PER_GROUP)
    pe = ee / se
    e_rank = jnp.zeros_like(pe)
    for d in range(1, EXP_PER_GROUP):
        other = _partner(pe, d, EXP_PER_GROUP)
        wrapped = kidx + d >= EXP_PER_GROUP
        e_rank = e_rank + jnp.where((other > pe) | ((other == pe) & wrapped), 1.0, 0.0)
    top = jnp.where(e_rank < 2.0, pe, 0.0)
    den = top
    for d in range(1, EXP_PER_GROUP):
        den = den + _partner(top, d, EXP_PER_GROUP)
    comb = jnp.where((g_rank < 1.0) & (lane < N_EXPERTS), pg * (top / den), 0.0)
    comb_ref[...] = comb


def _out_router(x, oa, ob, oc, od, lw, tm):
    n = x.shape[0]
    row = lambda i: (i, 0)
    fixed = lambda i: (0, 0)
    mix = pl.BlockSpec((tm, 256), row)
    return pl.pallas_call(
        _out_body,
        grid=(n // tm,),
        in_specs=[pl.BlockSpec((tm, D_MODEL), row), mix, mix, mix, mix,
                  pl.BlockSpec((D_MODEL, D_MODEL), fixed), pl.BlockSpec((1, D_MODEL), fixed),
                  pl.BlockSpec((2, D_MODEL, 2 * LANE), lambda i: (0, 0, 0)), pl.BlockSpec((1, 2 * LANE), fixed)],
        out_specs=[pl.BlockSpec((tm, D_MODEL), row), pl.BlockSpec((tm, D_MODEL), row), pl.BlockSpec((tm, LANE), row)],
        out_shape=[jax.ShapeDtypeStruct((n, D_MODEL), F32), jax.ShapeDtypeStruct((n, D_MODEL), BF16),
                   jax.ShapeDtypeStruct((n, LANE), F32)],
        compiler_params=_cp("parallel"),
        name="out_router",
    )(x, oa, ob, oc, od, lw['w_out'], lw['norm_ffn'], lw['w_router'], lw['b_router'])


def _router_weights(w_grp, b_grp, w_exp, b_exp):
    we = jnp.transpose(w_exp, (1, 0, 2)).reshape(D_MODEL, N_EXPERTS)
    wg = jnp.repeat(w_grp, EXP_PER_GROUP, axis=1)
    reps = LANE // N_EXPERTS
    w = jnp.concatenate([jnp.tile(we, (1, reps)), jnp.tile(wg, (1, reps))], axis=1)
    b = jnp.concatenate([jnp.tile(b_exp.reshape(1, N_EXPERTS), (1, reps)),
                         jnp.tile(jnp.repeat(b_grp, EXP_PER_GROUP).reshape(1, N_EXPERTS), (1, reps))], axis=1)
    w_hi = w.astype(BF16)
    return jnp.stack([w_hi, (w - w_hi.astype(F32)).astype(BF16)]), b


def _moe_body(h_ref, comb_ref, x1_ref, wg_ref, wu_ref, wd_ref, nf_ref, *out_refs, final):
    acc_ref = out_refs[-1]
    e = pl.program_id(1)

    @pl.when(e == 0)
    def _():
        acc_ref[...] = x1_ref[...]

    h = h_ref[...]
    comb = comb_ref[...]
    lane = lax.broadcasted_iota(jnp.int32, comb.shape, 1)
    c = jnp.sum(jnp.where(lane == e, comb, 0.0), axis=-1, keepdims=True)
    hg = jnp.dot(h, wg_ref[0], preferred_element_type=F32)
    hu = jnp.dot(h, wu_ref[0], preferred_element_type=F32)
    act = (jax.nn.silu(hg) * hu * c).astype(BF16)
    acc_ref[...] += jnp.dot(act, wd_ref[0], preferred_element_type=F32)

    @pl.when(e == N_EXPERTS - 1)
    def _():
        x2 = acc_ref[...]
        if final:
            out_refs[0][...] = x2 * lax.rsqrt(jnp.mean(x2 * x2, axis=-1, keepdims=True) + RMS_EPS) * nf_ref[...]
        else:
            out_refs[0][...] = x2


def _moe(h, comb, x1, lw, norm_final, tm, final):
    n = x1.shape[0]
    row = lambda i, e: (i, 0)
    per_e = lambda i, e: (e, 0, 0)
    return pl.pallas_call(
        functools.partial(_moe_body, final=final),
        grid=(n // tm, N_EXPERTS),
        in_specs=[pl.BlockSpec((tm, D_MODEL), row), pl.BlockSpec((tm, LANE), row), pl.BlockSpec((tm, D_MODEL), row),
                  pl.BlockSpec((1, D_MODEL, D_EXPERT), per_e), pl.BlockSpec((1, D_MODEL, D_EXPERT), per_e),
                  pl.BlockSpec((1, D_EXPERT, D_MODEL), per_e), pl.BlockSpec((1, D_MODEL), lambda i, e: (0, 0))],
        out_specs=pl.BlockSpec((tm, D_MODEL), row),
        out_shape=jax.ShapeDtypeStruct((n, D_MODEL), F32),
        scratch_shapes=[pltpu.VMEM((tm, D_MODEL), F32)],
        compiler_params=_cp("parallel", "arbitrary"),
        name="moe",
    )(h, comb, x1, lw['moe_wg'], lw['moe_wu'], lw['moe_wd'], norm_final.reshape(1, D_MODEL))


def _prep_layer(l, p):
    w_in = p['w_in'][l]
    o = _offsets(SPLIT_SIZES)
    segs = jnp.split(w_in, o, axis=1)
    w_all = jnp.concatenate([segs[0], segs[1], _pad_to(segs[2], LANE, 1), _pad_to(segs[3], SHIFT_PAD, 1), segs[4], segs[5]],
                            axis=1).astype(BF16)
    w_prec = w_in[:, :PRECISE_COLS]
    w_lo = (w_prec - w_prec.astype(BF16).astype(F32)).astype(BF16)
    lw = {'layer': l, 'w_all': w_all, 'w_lo': w_lo, 'norm_mix': p['norm_mix'][l]}
    lw['cmp'] = _cmp_weights(p['nsa_cmp_w1'][l], p['nsa_cmp_b1'][l], p['nsa_cmp_w2'][l])
    lw['rwkv_mu'] = _pad_to(p['rwkv_mu'][l].reshape(1, SHIFT_B), SHIFT_PAD, 1)
    lw['rwkv_vec'] = _pad_to(p['rwkv_vec'][l], 8, 0)
    z = lambda a, b: jnp.zeros((a, b), F32)
    lw['rwkv_wup'] = jnp.concatenate([p['rwkv_w_up'][l], z(LANE - LORA_W, C_B)], axis=0)
    lw['rwkv_aup'] = jnp.concatenate([z(LORA_W, C_B), p['rwkv_a_up'][l], z(LANE - LORA_W - LORA_A, C_B)], axis=0)
    lw['rwkv_gup'] = jnp.concatenate([z(LORA_W + LORA_A, C_B), p['rwkv_g_up'][l], z(LANE - LORA_W - LORA_A - LORA_G, C_B)], axis=0)
    for name in ('s5_lambda_re', 's5_lambda_im', 's5_b', 's5_c', 's5_d', 's5_log_step', 's5_w_glu', 'ret_gn'):
        lw[name] = p[name][l]
    lw['s5'], lw['s5_step'] = _s5_params(lw)
    lw['w_out'] = p['w_out'][l].astype(BF16)
    lw['norm_ffn'] = p['norm_ffn'][l].reshape(1, D_MODEL)
    lw['w_router'], lw['b_router'] = _router_weights(p['moe_w_grp'][l], p['moe_b_grp'][l], p['moe_w_exp'][l], p['moe_b_exp'][l])
    lw['moe_wg'] = p['moe_w_gate'][l].astype(BF16)
    lw['moe_wu'] = p['moe_w_up'][l].astype(BF16)
    lw['moe_wd'] = p['moe_w_down'][l].astype(BF16)
    return lw


ROW_TILE = 512
MOE_ROW_TILE = 1024
SCAN_TILE = 256


def _prompt_layer(x, lw, tabs, b, t, norm_final):
    cos_a, sin_a, ret_tabs = tabs
    q, kv, gate, colsb, u, colsd = _proj(x, lw['norm_mix'], lw['w_all'], lw['w_lo'], cos_a, sin_a, ROW_TILE)
    o_a = _nsa_prompt_mixer(q, kv, gate, lw, b, t)
    o_b, s_rwkv, s_shift = _rwkv_mixer(colsb, jnp.zeros((b, SHIFT_PAD), F32), jnp.zeros((b, H_B, HEAD_DIM, HEAD_DIM), F32),
                                       lw, b, t, ROW_TILE, b, SCAN_TILE)
    o_c, s_s5 = _s5_mixer(u, jnp.zeros((b, G_C, S5_P, 2), F32), lw, b, t, SCAN_TILE, False)
    o_d, s_ret = _ret_mixer(colsd, jnp.zeros((b, H_D, HEAD_DIM, HEAD_DIM), F32), lw, ret_tabs, b, t)
    x1, h, comb = _out_router(x, o_a, o_b, o_c, o_d, lw, ROW_TILE)
    x2 = _moe(h, comb, x1, lw, norm_final, MOE_ROW_TILE, lw['layer'] == DEPTH - 1)
    kv3 = kv.reshape(b, t, 6 * LANE)
    rows = kv3[:, :, :4 * LANE].reshape(b, t, 4, N_KV_A, HEAD_DIM)
    win = kv3[:, t - min(WINDOW, t):, 4 * LANE:].reshape(b, min(WINDOW, t), 2, N_KV_A, HEAD_DIM)
    return x2, (rows, win, s_rwkv, s_shift, s_s5, s_ret)


def _sample_layer(x, lw, tabs, b, pos, cache_kv, page_table, win_buf, s_rwkv, s_shift, s_s5, s_ret, norm_final):
    cos_a, sin_a, ret_cs = tabs
    q, kv, gate, colsb, u, colsd = _proj(x, lw['norm_mix'], lw['w_all'], lw['w_lo'], cos_a, sin_a, b)
    o_a, win = _nsa_sample(q, kv, gate, cache_kv, win_buf, lw['layer'], page_table, lw['cmp'], int(pos[0]))
    rows = kv[:, :4 * LANE].reshape(b, 1, 4, N_KV_A, HEAD_DIM)
    o_b, s_rwkv = _rwkv_step(colsb, _pad_to(s_shift, SHIFT_PAD, 1), s_rwkv, lw['layer'], lw)
    s_shift = colsb[:, :SHIFT_B]
    o_c, s_s5 = _s5_step(u, s_s5, lw['layer'], lw)
    o_d, s_ret = _ret_step(colsd, ret_cs[0], ret_cs[1], s_ret, lw['layer'], lw['ret_gn'])
    x1, h, comb = _out_router(x, o_a, o_b, o_c, o_d, lw, b)
    x2 = _moe(h, comb, x1, lw, norm_final, b, lw['layer'] == DEPTH - 1)
    return x2, (rows, win, s_rwkv, s_shift, s_s5, s_ret)


def kernel(x_prompt, x_sample, cache_nsa_kv, cache_nsa_win, state_rwkv, state_rwkv_shift, state_s5, state_ret, page_table, norm_mix, w_in, nsa_cmp_w1, nsa_cmp_b1, nsa_cmp_w2, rwkv_mu, rwkv_vec, rwkv_w_up, rwkv_a_up, rwkv_g_up, s5_lambda_re, s5_lambda_im, s5_b, s5_c, s5_d, s5_log_step, s5_w_glu, ret_gn, w_out, norm_ffn, moe_w_grp, moe_b_grp, moe_w_exp, moe_b_exp, moe_w_gate, moe_w_up, moe_w_down, norm_final):
    p = dict(norm_mix=norm_mix, w_in=w_in, nsa_cmp_w1=nsa_cmp_w1, nsa_cmp_b1=nsa_cmp_b1, nsa_cmp_w2=nsa_cmp_w2,
             rwkv_mu=rwkv_mu, rwkv_vec=rwkv_vec, rwkv_w_up=rwkv_w_up, rwkv_a_up=rwkv_a_up, rwkv_g_up=rwkv_g_up,
             s5_lambda_re=s5_lambda_re, s5_lambda_im=s5_lambda_im, s5_b=s5_b, s5_c=s5_c, s5_d=s5_d,
             s5_log_step=s5_log_step, s5_w_glu=s5_w_glu, ret_gn=ret_gn, w_out=w_out, norm_ffn=norm_ffn,
             moe_w_grp=moe_w_grp, moe_b_grp=moe_b_grp, moe_w_exp=moe_w_exp, moe_b_exp=moe_b_exp,
             moe_w_gate=moe_w_gate, moe_w_up=moe_w_up, moe_w_down=moe_w_down)
    bp, tp = x_prompt.shape[:2]
    bs, ts = x_sample.shape[:2]
    assert ts == 1 and tp % ROW_TILE == 0 and (bp * tp) % MOE_ROW_TILE == 0 and tp % (4 * LANE) == 0 and bs % 8 == 0
    past_len = page_table.shape[1] * cache_nsa_kv.shape[2]
    pos_p = np.arange(tp)
    pos_s = past_len + np.arange(ts)
    c = RET_CHUNK if tp % RET_CHUNK == 0 else tp
    tabs_p = _rope_tables(pos_p, ROT_DIM, ROPE_THETA, HEAD_DIM, 2) + (_ret_tables(pos_p, c),)
    pos_rows = np.repeat(pos_s, bs)
    tabs_s = _rope_tables(pos_rows, ROT_DIM, ROPE_THETA, HEAD_DIM, 2) + (_rope_tables(pos_rows, HEAD_DIM, RET_THETA, HEAD_DIM, H_D),)
    xp = x_prompt.reshape(bp * tp, D_MODEL)
    xs = x_sample.reshape(bs * ts, D_MODEL)
    sts_p, sts_s = [], []
    for l in range(DEPTH):
        lw = _prep_layer(l, p)
        xp, st_p = _prompt_layer(xp, lw, tabs_p, bp, tp, norm_final)
        xs, st_s = _sample_layer(xs, lw, tabs_s, bs, pos_s, cache_nsa_kv, page_table, cache_nsa_win, state_rwkv,
                                     state_rwkv_shift[l], state_s5, state_ret, norm_final)
        rows, win, s1, s2, s3, s4 = st_s
        sts_s.append((rows, win, s1, s2, s3, s4))
        sts_p.append(st_p)
    new_p = [jnp.stack([st[i] for st in sts_p]) for i in range(6)]
    new_s = [jnp.stack([st[i] for st in sts_s]) for i in range(6)]
    return (xp.reshape(bp, tp, D_MODEL), xs.reshape(bs, ts, D_MODEL), new_p[0], new_s[0], new_p[1], new_s[1],
            new_p[2], new_s[2], new_p[3], new_s[3], new_p[4], new_s[4], new_p[5], new_s[5])
```

```python
import functools

import numpy as np
import jax
import jax.numpy as jnp
from jax import lax
from jax.experimental import pallas as pl
from jax.experimental.pallas import tpu as pltpu

F32 = jnp.float32
BF16 = jnp.bfloat16
HI = lax.Precision.HIGHEST

D_MODEL = 1024
DEPTH = 2
HEAD_DIM = 64
C_A = C_B = C_C = C_D = 256
H_A = 4
N_KV_A = 2
R_A = 2
ROT_DIM = 16
ROPE_THETA = 500000.0
CMP_BLOCK = 32
CMP_STRIDE = 16
CMP_HIDDEN = 128
SEL_BLOCK = 64
TOP_K = 16
WINDOW = 512
NEG_INF = -1e30
FORCED_SCORE = 1e9
BLOCKED_SCORE = -1e9
H_B = 4
LORA_W = 16
LORA_A = 16
LORA_G = 32
SHIFT_B = 832
SHIFT_PAD = 896
RWKV_GN_EPS = 64e-5
S5_CH = 16
G_C = 16
S5_P = 64
H_D = 4
RET_CHUNK = 128
RET_THETA = 10000.0
RET_GN_EPS = 1e-5
N_GROUPS = 4
EXP_PER_GROUP = 4
N_EXPERTS = 16
D_EXPERT = 256
RMS_EPS = 1e-6
SPLIT_SIZES = (C_A, 6 * N_KV_A * HEAD_DIM, 3 * H_A, SHIFT_B, C_C, 4 * C_D)
LANE = 128
VMEM_LIMIT = 56 * 1024 * 1024


def _cp(*sem):
    return pltpu.CompilerParams(dimension_semantics=sem, vmem_limit_bytes=VMEM_LIMIT)


def _offsets(sizes):
    return [int(s) for s in np.cumsum(sizes)[:-1]]


def _pad_to(a, n, axis):
    pad = [(0, 0)] * a.ndim
    pad[axis] = (0, n - a.shape[axis])
    return jnp.pad(a, pad)


def _block_ones(n, blk, dtype):
    r = lax.broadcasted_iota(jnp.int32, (n, n), 0) // blk
    c = lax.broadcasted_iota(jnp.int32, (n, n), 1) // blk
    return (r == c).astype(dtype)


def _rope_tables(pos, rot_dim, theta, period, reps):
    half = rot_dim // 2
    inv = theta ** (-jnp.arange(half, dtype=F32) / half)
    ang = jnp.asarray(pos, F32)[:, None] * inv[None, :]
    cos, sin = jnp.cos(ang), jnp.sin(ang)
    n = ang.shape[0]
    rest = period - rot_dim
    c = jnp.concatenate([cos, cos, jnp.ones((n, rest), F32)], -1)
    s = jnp.concatenate([-sin, sin, jnp.zeros((n, rest), F32)], -1)
    return jnp.tile(c, (1, reps)), jnp.tile(s, (1, reps))


PRECISE_COLS = 384


def _proj_body(x_ref, nw_ref, w_ref, wlo_ref, cos_ref, sin_ref, q_ref, kv_ref, g_ref, cb_ref, u_ref, cd_ref):
    x = x_ref[...]
    h = x * lax.rsqrt(jnp.mean(x * x, axis=-1, keepdims=True) + RMS_EPS) * nw_ref[...]
    hb = h.astype(BF16)
    h_lo = (h - hb.astype(F32)).astype(BF16)
    c = cos_ref[...]
    s = sin_ref[...]
    first = (lax.broadcasted_iota(jnp.int32, c.shape, 1) % HEAD_DIM) < (ROT_DIM // 2)

    def rope(z):
        sw = jnp.where(first, pltpu.roll(z, LANE - ROT_DIM // 2, 1), pltpu.roll(z, ROT_DIM // 2, 1))
        return z * c + sw * s

    def dot(a, b):
        z = jnp.dot(hb, w_ref[:, a:b], preferred_element_type=F32)
        if b <= PRECISE_COLS:
            z = z + (jnp.dot(hb, wlo_ref[:, a:b], preferred_element_type=F32)
                     + jnp.dot(h_lo, w_ref[:, a:b], preferred_element_type=F32))
        return z

    for j in range(2):
        q_ref[:, LANE * j:LANE * (j + 1)] = rope(dot(LANE * j, LANE * (j + 1)))
    for j in range(6):
        z = dot(256 + LANE * j, 256 + LANE * (j + 1))
        kv_ref[:, LANE * j:LANE * (j + 1)] = rope(z) if j % 2 == 0 else z
    g_ref[...] = dot(1024, 1152)
    cb_ref[...] = dot(1152, 2048)
    u_ref[...] = dot(2048, 2304)
    cd_ref[...] = dot(2304, 3328)


def _proj(x2d, norm_w, w_all, w_lo, cos_t, sin_t, tm):
    n = x2d.shape[0]
    t_tiles = cos_t.shape[0] // tm
    row = lambda i: (i, 0)
    fixed = lambda i: (0, 0)
    tab = lambda i: (i % t_tiles, 0)
    widths = (256, 768, 128, SHIFT_PAD, 256, 1024)
    return pl.pallas_call(
        _proj_body,
        grid=(n // tm,),
        in_specs=[pl.BlockSpec((tm, D_MODEL), row), pl.BlockSpec((1, D_MODEL), fixed),
                  pl.BlockSpec((D_MODEL, 3328), fixed), pl.BlockSpec((D_MODEL, PRECISE_COLS), fixed),
                  pl.BlockSpec((tm, LANE), tab), pl.BlockSpec((tm, LANE), tab)],
        out_specs=[pl.BlockSpec((tm, w), row) for w in widths],
        out_shape=[jax.ShapeDtypeStruct((n, w), F32) for w in widths],
        compiler_params=_cp("parallel"),
        name="proj",
    )(x2d, norm_w.reshape(1, D_MODEL), w_all, w_lo, cos_t, sin_t)


def _cmp_mlp(xc, kind, w1_ref, b1_ref, w2_ref, w1k_ref, w2k_ref, nck):
    if kind == 0:
        hh = _dot3_presplit(xc, w1k_ref)
    else:
        hh = jnp.dot(xc.astype(BF16), w1_ref[kind], preferred_element_type=F32)
    hs = []
    for g in range(N_KV_A):
        hg = hh[g * nck:(g + 1) * nck]
        hs.append(jax.nn.gelu(b1_ref[kind] + hg[:, :CMP_HIDDEN] + pltpu.roll(hg[:, CMP_HIDDEN:], nck - 1, 0)))
    act = jnp.concatenate(hs, axis=1)
    if kind == 0:
        return _dot3_presplit(act, w2k_ref)
    return jnp.dot(act.astype(BF16), w2_ref[kind], preferred_element_type=F32)


def _cmp_body(xk_ref, xv_ref, w1_ref, b1_ref, w2_ref, w1k_ref, w2k_ref, kc_ref, vc_ref, vct_ref, xc_ref, *, n_chunks):
    lane = lax.broadcasted_iota(jnp.int32, (n_chunks, LANE), 1)
    lo = lane < HEAD_DIM
    for pair in range(CMP_STRIDE // 2):
        for kind, x_ref in enumerate((xk_ref, xv_ref)):
            ak = x_ref[0, pl.ds(2 * pair, n_chunks, stride=CMP_STRIDE), :]
            bk = x_ref[0, pl.ds(2 * pair + 1, n_chunks, stride=CMP_STRIDE), :]
            xc_ref[kind, 0:n_chunks, LANE * pair:LANE * (pair + 1)] = jnp.where(lo, ak, pltpu.roll(bk, HEAD_DIM, 1))
            xc_ref[kind, n_chunks:2 * n_chunks, LANE * pair:LANE * (pair + 1)] = jnp.where(lo, pltpu.roll(ak, HEAD_DIM, 1), bk)
    outs = [_cmp_mlp(xc_ref[kind], kind, w1_ref, b1_ref, w2_ref, w1k_ref, w2k_ref, n_chunks) for kind in range(2)]
    kc_ref[0] = outs[0]
    vc_ref[0] = outs[1]
    vct_ref[0] = outs[1].T


def _nsa_compress(rows, w1cat, b1, w2bd, w1k, w2k):
    b, tk = rows.shape[0], rows.shape[1]
    n_chunks = tk // CMP_STRIDE
    fixed3 = lambda i: (0, 0, 0)
    fixed2 = lambda i: (0, 0)
    return pl.pallas_call(
        functools.partial(_cmp_body, n_chunks=n_chunks),
        grid=(b,),
        in_specs=[pl.BlockSpec((1, tk, LANE), lambda i: (i, 0, 0)), pl.BlockSpec((1, tk, LANE), lambda i: (i, 0, 1)),
                  pl.BlockSpec((2, CMP_STRIDE * HEAD_DIM, 2 * CMP_HIDDEN), fixed3),
                  pl.BlockSpec((2, 1, CMP_HIDDEN), fixed3),
                  pl.BlockSpec((2, 2 * CMP_HIDDEN, LANE), fixed3),
                  pl.BlockSpec((2, CMP_STRIDE * HEAD_DIM, 2 * CMP_HIDDEN), fixed3), pl.BlockSpec((2, 2 * CMP_HIDDEN, LANE), fixed3)],
        out_specs=[pl.BlockSpec((1, n_chunks, LANE), lambda i: (i, 0, 0)),
                   pl.BlockSpec((1, n_chunks, LANE), lambda i: (i, 0, 0)),
                   pl.BlockSpec((1, LANE, n_chunks), lambda i: (i, 0, 0))],
        out_shape=[jax.ShapeDtypeStruct((b, n_chunks, LANE), F32), jax.ShapeDtypeStruct((b, n_chunks, LANE), F32),
                   jax.ShapeDtypeStruct((b, LANE, n_chunks), F32)],
        scratch_shapes=[pltpu.VMEM((2, N_KV_A * n_chunks, CMP_STRIDE * HEAD_DIM), F32)],
        compiler_params=_cp("parallel"),
        name="nsa_compress",
    )(rows, rows, w1cat, b1, w2bd, w1k, w2k)


def _cmp_weights(cmp_w1, cmp_b1, cmp_w2):
    m = CMP_BLOCK // CMP_STRIDE
    w1r = cmp_w1.reshape(2, m, CMP_STRIDE * HEAD_DIM, CMP_HIDDEN)
    w1cat = jnp.concatenate([w1r[:, j] for j in range(m)], axis=-1)
    z = jnp.zeros_like(cmp_w2)
    w2bd = jnp.concatenate([jnp.concatenate([cmp_w2, z], -1), jnp.concatenate([z, cmp_w2], -1)], axis=1)
    split = lambda w: jnp.stack([w.astype(BF16), (w - w.astype(BF16).astype(F32)).astype(BF16)])
    return w1cat.astype(BF16), cmp_b1.reshape(2, 1, CMP_HIDDEN), w2bd.astype(BF16), split(w1cat[0]), split(w2bd[0])


def _cmp_to_sel_t(n_chunks, n_cmp, n_sel):
    starts = np.arange(n_chunks) * CMP_STRIDE
    sel_s = np.arange(n_sel) * SEL_BLOCK
    ov = np.minimum(starts[:, None] + CMP_BLOCK, sel_s[None] + SEL_BLOCK) - np.maximum(starts[:, None], sel_s[None])
    ov = np.clip(ov, 0, None) / CMP_BLOCK
    ov[n_cmp:] = 0.0
    return jnp.asarray(ov.T, dtype=F32)


def _masked_softmax_cols(s, mask):
    m = jnp.max(jnp.where(mask, s, NEG_INF), axis=0, keepdims=True)
    e = jnp.where(mask, jnp.exp(s - m), 0.0)
    den = jnp.sum(e, axis=0, keepdims=True)
    return e * jnp.where(den > 0.0, 1.0 / den, 0.0)


def _nsa_prompt_body(qt_ref, gt_ref, kc_ref, vct_ref, ovt_ref, ks_ref, vst_ref, kw_ref, vwt_ref, o_ref, sel_ref,
                     *, n_cmp, n_sel, qb_size):
    qb = pl.program_id(1)
    tq = qb_size
    n_chunks = kc_ref.shape[1]
    qpos = qb * tq + lax.broadcasted_iota(jnp.int32, (1, tq), 1)
    qpos2 = jnp.concatenate([qpos, qpos], axis=1)
    zeros_q = jnp.zeros((HEAD_DIM, 2 * tq), F32)
    gates = jax.nn.sigmoid(gt_ref[0])
    kc = kc_ref[0]
    qpos4 = jnp.concatenate([qpos2, qpos2], axis=1)
    n_idx = lax.broadcasted_iota(jnp.int32, (n_chunks, 4 * tq), 0)
    cmp_mask = (n_idx * CMP_STRIDE + (CMP_BLOCK - 1) <= qpos4) & (n_idx < n_cmp)
    blk = lax.broadcasted_iota(jnp.int32, (n_sel, tq), 0)
    cur = qpos // SEL_BLOCK
    forced = (blk == 0) | (blk == cur) | (blk == cur - 1)
    causal_blk = blk * SEL_BLOCK <= qpos

    qpads32 = []
    for g in range(N_KV_A):
        q64 = jnp.concatenate([qt_ref[0, (2 * g) * HEAD_DIM:(2 * g + 1) * HEAD_DIM, :],
                               qt_ref[0, (2 * g + 1) * HEAD_DIM:(2 * g + 2) * HEAD_DIM, :]], axis=1) * (HEAD_DIM ** -0.5)
        qpads32.append(jnp.concatenate([q64, zeros_q], axis=0) if g == 0 else jnp.concatenate([zeros_q, q64], axis=0))
    q_all32 = jnp.concatenate(qpads32, axis=1)
    q_all = q_all32.astype(BF16)

    p_all = _masked_softmax_cols(_dot3(kc, q_all32), cmp_mask)
    o_cmp_all = jnp.dot(vct_ref[0].astype(BF16), p_all.astype(BF16), preferred_element_type=F32)
    o_cmps = [o_cmp_all[g * HEAD_DIM:(g + 1) * HEAD_DIM, g * 2 * tq:(g + 1) * 2 * tq] for g in range(N_KV_A)]

    for g in range(N_KV_A):
        psum = p_all[:, g * 2 * tq:g * 2 * tq + tq] + p_all[:, g * 2 * tq + tq:(g + 1) * 2 * tq]
        p1 = psum.astype(BF16)
        p2, p3 = _split_bf16(psum - p1.astype(F32))
        ovb = ovt_ref[...].astype(BF16)
        imp = (jnp.dot(ovb, p1, preferred_element_type=F32) + jnp.dot(ovb, p2, preferred_element_type=F32)
               + jnp.dot(ovb, p3, preferred_element_type=F32))
        imp = jnp.where(forced, FORCED_SCORE, jnp.where(causal_blk, imp, BLOCKED_SCORE))
        ngrp = n_sel // 8
        imp_g = [imp[8 * k:8 * k + 8, :] for k in range(ngrp)]
        rank_g = [jnp.zeros((8, tq), F32) for _ in range(ngrp)]
        for i in range(n_sel):
            row = imp[i:i + 1, :]
            for k in range(ngrp):
                if k > i // 8:
                    beats = row >= imp_g[k]
                elif k < i // 8:
                    beats = row > imp_g[k]
                else:
                    beats = (row > imp_g[k]) | ((row == imp_g[k]) & (blk[8 * k:8 * k + 8, :] > i))
                rank_g[k] = rank_g[k] + jnp.where(beats, 1.0, 0.0)
        rank = jnp.concatenate(rank_g, axis=0)
        sel_ref[g] = jnp.where(rank < float(min(TOP_K, n_sel)), 0.0, NEG_INF)

    tk_s = 4 * tq

    def slc(j, carry, causal=True):
        m, l, acc = carry
        off = pl.multiple_of(j * tk_s, tk_s)
        s = jnp.dot(ks_ref[0, pl.ds(off, tk_s), :].astype(BF16), q_all, preferred_element_type=F32)
        per_tile = tk_s // SEL_BLOCK
        biases = []
        for g in range(N_KV_A):
            rows = [jnp.broadcast_to(sel_ref[g, pl.ds(j * per_tile + a, 1), :], (SEL_BLOCK, tq)) for a in range(per_tile)]
            bias = jnp.concatenate(rows, axis=0)
            biases += [bias, bias]
        sm = s + jnp.concatenate(biases, axis=1)
        if causal:
            diff = qpos4 - (off + lax.broadcasted_iota(jnp.int32, (tk_s, 4 * tq), 0))
            sm = jnp.where(diff >= 0, sm, NEG_INF)
        m_new = jnp.maximum(m, jnp.max(sm, axis=0, keepdims=True))
        e = jnp.exp(sm - m_new)
        alpha = jnp.exp(m - m_new)
        l_new = alpha * l + jnp.sum(e, axis=0, keepdims=True)
        vt = vst_ref[0, :, pl.ds(off, tk_s)].astype(BF16)
        return m_new, l_new, alpha * acc + jnp.dot(vt, e.astype(BF16), preferred_element_type=F32)

    init = (jnp.full((1, 4 * tq), NEG_INF, F32), jnp.zeros((1, 4 * tq), F32), jnp.zeros((2 * HEAD_DIM, 4 * tq), F32))
    last_s = (qb * tq) // tk_s
    _, l_s, acc_s = slc(last_s, lax.fori_loop(0, last_s, functools.partial(slc, causal=False), init))
    t_keys = kw_ref.shape[1]
    tk_w = min(WINDOW + tq, t_keys)
    start = pl.multiple_of(jnp.clip(qb * tq - WINDOW, 0, t_keys - tk_w), tq)
    s_w = jnp.dot(kw_ref[0, pl.ds(start, tk_w), :].astype(BF16), q_all, preferred_element_type=F32)
    diff_w = qpos4 - (start + lax.broadcasted_iota(jnp.int32, (tk_w, 4 * tq), 0))
    mask_w = (diff_w >= 0) & (diff_w < WINDOW)
    m_w = jnp.max(jnp.where(mask_w, s_w, NEG_INF), axis=0, keepdims=True)
    e_w = jnp.where(mask_w, jnp.exp(s_w - m_w), 0.0)
    l_w = jnp.sum(e_w, axis=0, keepdims=True)
    acc_w = jnp.dot(vwt_ref[0, :, pl.ds(start, tk_w)].astype(BF16), e_w.astype(BF16), preferred_element_type=F32)

    for g in range(N_KV_A):
        blk_g = (slice(g * HEAD_DIM, (g + 1) * HEAD_DIM), slice(g * 2 * tq, (g + 1) * 2 * tq))
        o_slc = acc_s[blk_g] / l_s[:, blk_g[1]]
        o_win = acc_w[blk_g] / l_w[:, blk_g[1]]
        for r in range(R_A):
            h = 2 * g + r
            gr = gates[3 * h:3 * h + 3, :]
            sl = slice(r * tq, (r + 1) * tq)
            o_ref[0, h * HEAD_DIM:(h + 1) * HEAD_DIM, :] = (gr[0:1] * o_cmps[g][:, sl] + gr[1:2] * o_slc[:, sl]
                                                          + gr[2:3] * o_win[:, sl])


def _nsa_prompt(qt, gt, kc, vct, ovt, kv, vst, vwt, n_cmp):
    b, _, t = qt.shape
    tq = 128
    n_sel = t // SEL_BLOCK
    n_chunks = kc.shape[1]
    per_b = lambda i, j: (i, 0, 0)
    return pl.pallas_call(
        functools.partial(_nsa_prompt_body, n_cmp=n_cmp, n_sel=n_sel, qb_size=tq),
        grid=(b, t // tq),
        in_specs=[pl.BlockSpec((1, 256, tq), lambda i, j: (i, 0, j)),
                  pl.BlockSpec((1, 16, tq), lambda i, j: (i, 0, j)),
                  pl.BlockSpec((1, n_chunks, LANE), per_b),
                  pl.BlockSpec((1, LANE, n_chunks), per_b),
                  pl.BlockSpec((n_sel, n_chunks), lambda i, j: (0, 0)),
                  pl.BlockSpec((1, t, LANE), lambda i, j: (i, 0, 2)),
                  pl.BlockSpec((1, LANE, t), per_b),
                  pl.BlockSpec((1, t, LANE), lambda i, j: (i, 0, 4)),
                  pl.BlockSpec((1, LANE, t), per_b)],
        out_specs=pl.BlockSpec((1, 256, tq), lambda i, j: (i, 0, j)),
        out_shape=jax.ShapeDtypeStruct((b, 256, t), F32),
        scratch_shapes=[pltpu.VMEM((N_KV_A, n_sel, tq), F32)],
        compiler_params=_cp("parallel", "arbitrary"),
        name="nsa_prompt",
    )(qt, gt, kc, vct, ovt, kv, vst, kv, vwt)


def _nsa_prompt_mixer(q, kv, gate, lw, b, t):
    kv3 = kv.reshape(b, t, 6 * LANE)
    n_chunks = t // CMP_STRIDE
    n_cmp = (t - CMP_BLOCK) // CMP_STRIDE + 1
    kc, _, vct = _nsa_compress(kv3, *lw['cmp'])
    ovt = _cmp_to_sel_t(n_chunks, n_cmp, t // SEL_BLOCK)
    qt = jnp.swapaxes(q.reshape(b, t, 256), 1, 2)
    gt = jnp.swapaxes(gate.reshape(b, t, LANE)[:, :, :16], 1, 2)
    vst = jnp.swapaxes(kv3[:, :, 3 * LANE:4 * LANE], 1, 2)
    vwt = jnp.swapaxes(kv3[:, :, 5 * LANE:], 1, 2)
    ot = _nsa_prompt(qt, gt, kc, vct, ovt, kv3, vst, vwt, n_cmp)
    return jnp.swapaxes(ot, 1, 2).reshape(b * t, 256)


def _softmax_rows_with_extra(s, mask, s_new):
    m = jnp.maximum(jnp.max(jnp.where(mask, s, NEG_INF), axis=-1, keepdims=True), s_new)
    e = jnp.where(mask, jnp.exp(s - m), 0.0)
    e_new = jnp.exp(s_new - m)
    return e, e_new, 1.0 / (jnp.sum(e, axis=-1, keepdims=True) + e_new)


def _nsa_sample_body(pt_ref, *refs, n_pages, page, pos, n_sel, n_cmp, wb):
    del pt_ref
    n_in = n_pages
    pages = refs[:n_in]
    (qbd_ref, new_ref, gate_ref, win_ref, ov_ref, w1_ref, b1_ref, w2_ref, w1k_ref, w2k_ref,
     o_ref, nw_ref, tok_ref, xc_ref) = refs[n_in:]
    pg = lambda p, kind: pages[p].at[0, kind:kind + 1]
    nck = n_pages * page // CMP_STRIDE
    lane8 = lax.broadcasted_iota(jnp.int32, (8, LANE), 1)
    lo8 = lane8 < HEAD_DIM
    nt = lambda a, b: lax.dot_general(a, b, (((1,), (1,)), ((), ())), preferred_element_type=F32)

    lo_c = lax.broadcasted_iota(jnp.int32, (nck, LANE), 1) < HEAD_DIM
    for kind in range(2):
        for p in range(n_pages):
            tok_ref[kind, p * page:(p + 1) * page, :] = pg(p, kind)[0].T
        for pair in range(CMP_STRIDE // 2):
            a = tok_ref[kind, pl.ds(2 * pair, nck, stride=CMP_STRIDE), :]
            b = tok_ref[kind, pl.ds(2 * pair + 1, nck, stride=CMP_STRIDE), :]
            cols = slice(LANE * pair, LANE * (pair + 1))
            xc_ref[kind, 0:nck, cols] = jnp.where(lo_c, a, pltpu.roll(b, HEAD_DIM, 1))
            xc_ref[kind, nck:2 * nck, cols] = jnp.where(lo_c, pltpu.roll(a, HEAD_DIM, 1), b)
    kc, vc = [_cmp_mlp(xc_ref[kind], kind, w1_ref, b1_ref, w2_ref, w1k_ref, w2k_ref, nck) for kind in range(2)]

    q = qbd_ref[0] * (HEAD_DIM ** -0.5)
    qb = q.astype(BF16)
    new = new_ref[0]

    n_idx = lax.broadcasted_iota(jnp.int32, (8, nck), 1)
    cmask = (n_idx * CMP_STRIDE + (CMP_BLOCK - 1) <= pos) & (n_idx < n_cmp)
    s = _dot3(q, kc, _NT)
    m = jnp.max(jnp.where(cmask, s, NEG_INF), axis=-1, keepdims=True)
    e = jnp.where(cmask, jnp.exp(s - m), 0.0)
    den = jnp.sum(e, axis=-1, keepdims=True)
    p_cmp = e * jnp.where(den > 0.0, 1.0 / den, 0.0)
    o_cmp = jnp.dot(p_cmp.astype(BF16), vc.astype(BF16), preferred_element_type=F32)

    row8 = lax.broadcasted_iota(jnp.int32, (8, nck), 0)
    psum = jnp.where(row8 == 0, p_cmp[0:1] + p_cmp[1:2], jnp.where(row8 == 1, p_cmp[2:3] + p_cmp[3:4], 0.0))
    imp = jnp.dot(psum, ov_ref[...], precision=HI, preferred_element_type=F32)
    cur = pos // SEL_BLOCK
    forced = (lane8 == 0) | (lane8 == cur) | (lane8 == cur - 1)
    imp = jnp.where(forced, FORCED_SCORE, jnp.where(lane8 * SEL_BLOCK <= pos, imp, BLOCKED_SCORE))
    imp = jnp.where(lane8 < n_sel, imp, -3e38)
    rank = jnp.zeros((8, LANE), F32)
    for i in range(n_sel):
        col = imp[:, i:i + 1]
        rank = rank + jnp.where((col > imp) | ((col == imp) & (lane8 > i)), 1.0, 0.0)
    sel = jnp.where((rank < float(min(TOP_K, n_sel))) & (lane8 < n_sel), 1.0, 0.0)
    rsel = lax.broadcasted_iota(jnp.int32, (8, LANE), 0)
    selh = jnp.where(rsel < R_A, sel[0:1], jnp.where(rsel < 2 * R_A, sel[1:2], 0.0))

    per_page = page // SEL_BLOCK
    s_t, m_t = [], []
    for p in range(n_pages):
        s_t.append(jnp.dot(qb, pg(p, 2)[0].astype(BF16), preferred_element_type=F32))
        blk_sel = selh[:, per_page * p:per_page * p + 1]
        for a in range(1, per_page):
            blk_sel = jnp.where(lane8 < a * SEL_BLOCK, blk_sel, selh[:, per_page * p + a:per_page * p + a + 1])
        kpos = p * page + lane8
        m_t.append((blk_sel > 0.0) & (kpos <= pos))
    s_all = jnp.concatenate(s_t, axis=1)
    mk_all = jnp.concatenate(m_t, axis=1)
    s_new = jnp.sum(q * new[2:3], axis=-1, keepdims=True)
    e, e_new, inv = _softmax_rows_with_extra(s_all, mk_all, s_new)
    acc = e_new * new[3:4]
    for p in range(n_pages):
        acc = acc + nt(e[:, p * page:(p + 1) * page].astype(BF16), pg(p, 3)[0].astype(BF16))
    o_slc = acc * inv

    kw, vw = win_ref[0], win_ref[1]
    widx = lax.broadcasted_iota(jnp.int32, (8, wb), 1)
    diff = wb - widx
    s_w = jnp.dot(qb, kw.astype(BF16), preferred_element_type=F32)
    s_wnew = jnp.sum(q * new[4:5], axis=-1, keepdims=True)
    e, e_new, inv = _softmax_rows_with_extra(s_w, (diff >= 0) & (diff < WINDOW), s_wnew)
    o_win = (nt(e.astype(BF16), vw.astype(BF16)) + e_new * new[5:6]) * inv

    gts = jax.nn.sigmoid(gate_ref[0])
    o = gts[:, 0:1] * o_cmp + gts[:, 1:2] * o_slc + gts[:, 2:3] * o_win
    lo1 = lo8[0:1]
    o_ref[0] = jnp.concatenate([jnp.where(lo1, o[0:1], pltpu.roll(o[1:2], HEAD_DIM, 1)),
                                jnp.where(lo1, pltpu.roll(o[2:3], HEAD_DIM, 1), o[3:4])], axis=1)
    last = lax.broadcasted_iota(jnp.int32, (LANE, wb), 1) == wb - 1
    new_t = new.T
    nw_ref[0] = jnp.where(last, new_t[:, 4:5], pltpu.roll(kw, wb - 1, 1))
    nw_ref[1] = jnp.where(last, new_t[:, 5:6], pltpu.roll(vw, wb - 1, 1))


def _nsa_sample(q, kv, gate, cache_kv, cache_win, layer, page_table, cmp_w, pos):
    b = q.shape[0]
    n_pool, page = cache_kv.shape[1:3]
    n_pages = page_table.shape[1]
    wb = cache_win.shape[2]
    assert wb == WINDOW and page % CMP_STRIDE == 0 and page % SEL_BLOCK == 0 and page == LANE
    tk = n_pages * page + 1
    n_cmp = (tk - CMP_BLOCK) // CMP_STRIDE + 1
    nck = n_pages * page // CMP_STRIDE
    assert n_cmp <= nck
    n_sel = -(-tk // SEL_BLOCK)
    assert n_sel <= LANE and pos // SEL_BLOCK == n_sel - 1
    ov = _pad_to(_cmp_to_sel_t(nck, n_cmp, n_sel).T, LANE, 1)
    w1cat, b1, w2bd, w1k, w2k = cmp_w
    q4 = q.reshape(b, H_A, HEAD_DIM)
    z = jnp.zeros_like(q4)
    first = (jnp.arange(H_A) // R_A == 0)[None, :, None]
    qbd = jnp.concatenate([jnp.where(first, q4, z), jnp.where(first, z, q4)], axis=-1)
    qbd = _pad_to(qbd, 8, 1)
    new = _pad_to(kv.reshape(b, 6, LANE), 8, 1)
    g8 = _pad_to(_pad_to(gate[:, :3 * H_A].reshape(b, H_A, 3), LANE, 2), 8, 1)
    cache3 = jnp.transpose(cache_kv, (0, 1, 3, 4, 5, 2)).reshape(-1, 4, LANE, page)
    win3 = jnp.transpose(cache_win, (0, 1, 3, 4, 5, 2)).reshape(-1, LANE, wb)
    page_specs = [pl.BlockSpec((1, 4, LANE, page), functools.partial(lambda i, pt, p: (layer * n_pool + pt[i, p], 0, 0, 0), p=p))
                  for p in range(n_pages)]
    per_b = lambda i, pt: (i, 0, 0)
    fixed2 = lambda i, pt: (0, 0)
    fixed3 = lambda i, pt: (0, 0, 0)
    grid_spec = pltpu.PrefetchScalarGridSpec(
        num_scalar_prefetch=1,
        grid=(b,),
        in_specs=page_specs + [pl.BlockSpec((1, 8, LANE), per_b), pl.BlockSpec((1, 8, LANE), per_b), pl.BlockSpec((1, 8, LANE), per_b),
                               pl.BlockSpec((2, LANE, wb), lambda i, pt: (layer * b + i, 0, 0)), pl.BlockSpec((nck, LANE), fixed2),
                               pl.BlockSpec((2, CMP_STRIDE * HEAD_DIM, 2 * CMP_HIDDEN), fixed3),
                               pl.BlockSpec((2, 1, CMP_HIDDEN), fixed3), pl.BlockSpec((2, 2 * CMP_HIDDEN, LANE), fixed3),
                               pl.BlockSpec((2, CMP_STRIDE * HEAD_DIM, 2 * CMP_HIDDEN), fixed3),
                               pl.BlockSpec((2, 2 * CMP_HIDDEN, LANE), fixed3)],
        out_specs=[pl.BlockSpec((1, 1, 256), per_b), pl.BlockSpec((2, LANE, wb), per_b)],
        scratch_shapes=[pltpu.VMEM((2, n_pages * page, LANE), F32), pltpu.VMEM((2, 2 * nck, CMP_STRIDE * HEAD_DIM), F32)],
    )
    o, nw = pl.pallas_call(
        functools.partial(_nsa_sample_body, n_pages=n_pages, page=page, pos=pos, n_sel=n_sel, n_cmp=n_cmp, wb=wb),
        grid_spec=grid_spec,
        out_shape=[jax.ShapeDtypeStruct((b, 1, 256), F32), jax.ShapeDtypeStruct((2 * b, LANE, wb), F32)],
        compiler_params=_cp("parallel"),
        name="nsa_sample",
    )(page_table, *([cache3] * n_pages), qbd, new, g8, win3, ov, w1cat, b1, w2bd, w1k, w2k)
    return o.reshape(b, 256), jnp.transpose(nw.reshape(b, 2, N_KV_A, HEAD_DIM, wb), (0, 4, 1, 2, 3))


def _rwkv_prep_body(c_ref, s0_ref, mu_ref, vec_ref, wup_ref, aup_ref, gup_ref,
                    r_ref, lw_ref, k_ref, v_ref, kk_ref, ka_ref, g_ref, bonus_ref, carry_ref, *, tiles_per_seq):
    i = pl.program_id(0)
    cols = c_ref[...]
    tm = cols.shape[0]

    @pl.when(i % tiles_per_seq == 0)
    def _():
        carry_ref[...] = s0_ref[0]

    prev = pltpu.roll(cols, 1, 0)
    row0 = lax.broadcasted_iota(jnp.int32, cols.shape, 0) == 0
    prev = jnp.where(row0, carry_ref[...], prev)
    carry_ref[...] = cols[tm - 1:tm, :]
    _rwkv_features(cols, prev, mu_ref, vec_ref, wup_ref, aup_ref, gup_ref,
                   r_ref, lw_ref, k_ref, v_ref, kk_ref, ka_ref, g_ref, bonus_ref)


def _rwkv_step_prep_body(c_ref, s0_ref, mu_ref, vec_ref, wup_ref, aup_ref, gup_ref,
                         r_ref, lw_ref, k_ref, v_ref, kk_ref, ka_ref, g_ref, bonus_ref):
    _rwkv_features(c_ref[...], s0_ref[...], mu_ref, vec_ref, wup_ref, aup_ref, gup_ref,
                   r_ref, lw_ref, k_ref, v_ref, kk_ref, ka_ref, g_ref, bonus_ref)


def _rwkv_features(cols, prev, mu_ref, vec_ref, wup_ref, aup_ref, gup_ref,
                   r_ref, lw_ref, k_ref, v_ref, kk_ref, ka_ref, g_ref, bonus_ref):
    xs = cols + mu_ref[...] * (prev - cols)
    r, k, v, lo = xs[:, 0:256], xs[:, 256:512], xs[:, 512:768], xs[:, 768:896]
    vec = vec_ref[...]
    w0, a0, k_k, k_a, r_k = vec[0:1], vec[1:2], vec[2:3], vec[3:4], vec[4:5]
    w_log = -jax.nn.softplus(-(w0 + _dot3(jnp.tanh(lo), wup_ref[...]))) - 0.5
    a = jax.nn.sigmoid(a0 + _dot3(lo, aup_ref[...]))
    g_ref[...] = _dot3(jax.nn.sigmoid(lo), gup_ref[...])
    ones = _block_ones(256, HEAD_DIM, BF16)
    kk = k * k_k
    kk = kk * lax.rsqrt(_dot_bf16_exact_rhs(kk * kk, ones) + 1e-12)
    k2 = k * (1.0 + (a - 1.0) * k_a)
    r_ref[...] = r
    lw_ref[...] = -jnp.exp(w_log)
    k_ref[...] = k2
    v_ref[...] = v
    kk_ref[...] = kk
    ka_ref[...] = kk * a
    bonus_ref[...] = _dot_bf16_exact_rhs(r * k2 * r_k, ones) * v


def _rwkv_prep(colsb, shift0, lw, t, tm):
    n = colsb.shape[0]
    tiles_per_seq = t // tm
    row = lambda i: (i, 0)
    fixed = lambda i: (0, 0)
    outs = [jax.ShapeDtypeStruct((n, 256), F32)] * 8
    return pl.pallas_call(
        functools.partial(_rwkv_prep_body, tiles_per_seq=tiles_per_seq),
        grid=(n // tm,),
        in_specs=[pl.BlockSpec((tm, SHIFT_PAD), row),
                  pl.BlockSpec((1, 1, SHIFT_PAD), lambda i: (i // tiles_per_seq, 0, 0)),
                  pl.BlockSpec((1, SHIFT_PAD), fixed), pl.BlockSpec((8, 256), fixed),
                  pl.BlockSpec((LANE, 256), fixed), pl.BlockSpec((LANE, 256), fixed), pl.BlockSpec((LANE, 256), fixed)],
        out_specs=[pl.BlockSpec((tm, 256), row)] * 8,
        out_shape=outs,
        scratch_shapes=[pltpu.VMEM((1, SHIFT_PAD), F32)],
        compiler_params=_cp("arbitrary"),
        name="rwkv_prep",
    )(colsb, shift0.reshape(-1, 1, SHIFT_PAD), lw['rwkv_mu'], lw['rwkv_vec'], lw['rwkv_wup'], lw['rwkv_aup'], lw['rwkv_gup'])


RWKV_CHUNK = 64


def _split_bf16(x):
    hi = x.astype(BF16)
    return hi, (x - hi.astype(F32)).astype(BF16)


def _dot3(a, b, dims=(((1,), (0,)), ((), ()))):
    ah, al = _split_bf16(a)
    bh, bl = _split_bf16(b)
    dg = lambda x, y: lax.dot_general(x, y, dims, preferred_element_type=F32)
    return dg(ah, bh) + dg(ah, bl) + dg(al, bh)


def _dot3_presplit(x, w_ref):
    xh, xl = _split_bf16(x)
    return (jnp.dot(xh, w_ref[0], preferred_element_type=F32) + jnp.dot(xh, w_ref[1], preferred_element_type=F32)
            + jnp.dot(xl, w_ref[0], preferred_element_type=F32))


def _dot_bf16_exact_rhs(x, m_bf16):
    hi, lo = _split_bf16(x)
    return jnp.dot(hi, m_bf16, preferred_element_type=F32) + jnp.dot(lo, m_bf16, preferred_element_type=F32)


_NT = (((1,), (1,)), ((), ()))


def _rwkv_chunk_body(r_ref, lw_ref, k_ref, v_ref, kk_ref, ka_ref, g_ref, bonus_ref, s0_ref, vec_ref,
                     o_ref, st_ref, s_scr, y_scr, *, nb, tl):
    L = RWKV_CHUNK
    nc = tl // L

    @pl.when(pl.program_id(1) == 0)
    def _():
        s_scr[...] = s0_ref[...]

    ri = lax.broadcasted_iota(jnp.int32, (L, L), 0)
    ci = lax.broadcasted_iota(jnp.int32, (L, L), 1)
    strict, incl = ri > ci, ri >= ci
    ltri = incl.astype(F32)
    eye = (ri == ci).astype(F32)

    bnn = (((2,), (1,)), ((0,), (0,)))
    bnt = (((2,), (2,)), ((0,), (0,)))

    def chunk(c, carry):
        rows = pl.ds(pl.multiple_of(c * L, L), L)
        lhs_l, rhs_l, v_l, kw_l, wl_l = [], [], [], [], []
        for b in range(nb):
            r, lw, k, v, kk, ka = [ref[b, rows, :] for ref in (r_ref, lw_ref, k_ref, v_ref, kk_ref, ka_ref)]
            cl = jnp.dot(ltri, lw, precision=HI, preferred_element_type=F32)
            e_neg = jnp.exp(-cl)
            e_rem = jnp.exp(cl[L - 1:L, :] - cl)
            kkd, rd = kk * jnp.exp(cl - lw), r * jnp.exp(cl)
            kinv, kainv, kw, kaw = k * e_neg, ka * e_neg, k * e_rem, ka * e_rem
            w_last = jnp.exp(cl[L - 1:L, :])
            for h in range(H_B):
                sl = slice(h * HEAD_DIM, (h + 1) * HEAD_DIM)
                lhs_l.append(jnp.concatenate([kkd[:, sl], rd[:, sl]], axis=0))
                rhs_l.append(jnp.concatenate([kinv[:, sl], kainv[:, sl]], axis=0))
                v_l.append(v[:, sl])
                kw_l.append(jnp.concatenate([kw[:, sl], kaw[:, sl]], axis=0))
                wl_l.append(w_last[:, sl])
        lhs, rhs, vs, kws, wl = [jnp.stack(x) for x in (lhs_l, rhs_l, v_l, kw_l, wl_l)]
        gm = _dot3(lhs, rhs, bnt)
        a_vk = jnp.where(strict, gm[:, :L, :L], 0.0)
        n1 = jnp.where(strict, -gm[:, :L, L:], 0.0)
        t_inv, pw = eye + n1, n1
        for _ in range(L.bit_length() - 2):
            pw = _dot3(pw, pw, bnn)
            t_inv = _dot3(t_inv, eye + pw, bnn)
        s = s_scr[...].reshape(nb * H_B, HEAD_DIM, HEAD_DIM)
        xs = _dot3(lhs, s, bnt)
        u = _dot3(t_inv, xs[:, :L] + _dot3(a_vk, vs, bnn), bnn)
        b_vk = jnp.where(incl, gm[:, L:, :L], 0.0).astype(BF16)
        b_uk = jnp.where(incl, gm[:, L:, L:], 0.0).astype(BF16)
        y = (xs[:, L:] + lax.dot_general(b_vk, vs.astype(BF16), bnn, preferred_element_type=F32)
             - lax.dot_general(b_uk, u.astype(BF16), bnn, preferred_element_type=F32))
        vu_t = jnp.stack([jnp.concatenate([vs[n], -u[n]], axis=0).T for n in range(nb * H_B)])
        s_new = s * wl + _dot3(vu_t, kws, bnn)
        s_scr[...] = s_new.reshape(nb, H_B, HEAD_DIM, HEAD_DIM)
        for b in range(nb):
            for h in range(H_B):
                y_scr[b, rows, h * HEAD_DIM:(h + 1) * HEAD_DIM] = y[b * H_B + h]
        return carry

    lax.fori_loop(0, nc, chunk, 0)
    st_ref[...] = s_scr[...]
    vec = vec_ref[...]
    for b in range(nb):
        o_ref[b] = (_segment_norm(y_scr[b], RWKV_GN_EPS) * vec[5:6] + vec[6:7] + bonus_ref[b]) * g_ref[b]


def _rwkv_chunked(prep, s0, vec, b, t, nb, tl):
    arrs = [a.reshape(b, t, 256) for a in prep]
    seq = pl.BlockSpec((nb, tl, 256), lambda i, j: (i, j, 0))
    st = pl.BlockSpec((nb, H_B, HEAD_DIM, HEAD_DIM), lambda i, j: (i, 0, 0, 0))
    return pl.pallas_call(
        functools.partial(_rwkv_chunk_body, nb=nb, tl=tl),
        grid=(b // nb, t // tl),
        in_specs=[seq] * 8 + [st, pl.BlockSpec((8, 256), lambda i, j: (0, 0))],
        out_specs=[seq, st],
        out_shape=[jax.ShapeDtypeStruct((b, t, 256), F32), jax.ShapeDtypeStruct((b, H_B, HEAD_DIM, HEAD_DIM), F32)],
        scratch_shapes=[pltpu.VMEM((nb, H_B, HEAD_DIM, HEAD_DIM), F32), pltpu.VMEM((nb, tl, 256), F32)],
        compiler_params=_cp("parallel", "arbitrary"),
        name="rwkv_chunked",
    )(*arrs, s0, vec)


def _rwkv_mixer(colsb, shift0, s0, lw, b, t, tm, nb, tl):
    prep = _rwkv_prep(colsb, shift0, lw, t, tm)
    o, st = _rwkv_chunked(prep, s0, lw['rwkv_vec'], b, t, nb, tl)
    shift = colsb.reshape(b, t, SHIFT_PAD)[:, -1, :SHIFT_B]
    return o.reshape(b * t, 256), st, shift


def _segment_norm(y, eps):
    avg = _block_ones(256, HEAD_DIM, BF16) * (1.0 / HEAD_DIM)
    yc = y - _dot_bf16_exact_rhs(y, avg)
    return yc * lax.rsqrt(_dot_bf16_exact_rhs(yc * yc, avg) + eps)


def _rwkv_step_body(r_ref, lw_ref, k_ref, v_ref, kk_ref, ka_ref, g_ref, bonus_ref, vec_ref, s_ref, o_ref, st_ref, ft_scr, y_scr):
    h = pl.program_id(0)

    @pl.when(h == 0)
    def _():
        for n, ref in enumerate((r_ref, lw_ref, k_ref, v_ref, kk_ref, ka_ref)):
            ft_scr[n] = ref[...].T

    base = pl.multiple_of(h * HEAD_DIM, HEAD_DIM)
    head = lambda n: ft_scr[n, pl.ds(base, HEAD_DIM), :]
    r_t, w_t, k_t, kk_t, ka_t = head(0), jnp.exp(head(1)), head(2), head(4), head(5)

    def body(i, carry):
        rows = pl.ds(pl.multiple_of(i * HEAD_DIM, HEAD_DIM), HEAD_DIM)
        s = s_ref[rows, :]
        sk = jnp.sum(s * kk_t, axis=0, keepdims=True)
        s = s * w_t - sk * ka_t + ft_scr[3, pl.ds(base + i, 1), :] * k_t
        st_ref[rows, :] = s
        y_scr[pl.ds(base + i, 1), :] = jnp.sum(s * r_t, axis=0, keepdims=True)
        return carry

    lax.fori_loop(0, HEAD_DIM, body, 0)

    @pl.when(h == H_B - 1)
    def _():
        vec = vec_ref[...]
        o_ref[...] = (_segment_norm(y_scr[...].T, RWKV_GN_EPS) * vec[5:6] + vec[6:7] + bonus_ref[...]) * g_ref[...]


def _rwkv_step(colsb, shift0, s_all, layer, lw):
    b = colsb.shape[0]
    hd2 = HEAD_DIM * HEAD_DIM
    full = lambda *_: (0, 0)
    feat = pl.BlockSpec((b, 256), full)
    prep = pl.pallas_call(
        _rwkv_step_prep_body,
        grid=(1,),
        in_specs=[pl.BlockSpec((b, SHIFT_PAD), full), pl.BlockSpec((b, SHIFT_PAD), full), pl.BlockSpec((1, SHIFT_PAD), full),
                  pl.BlockSpec((8, 256), full)] + [pl.BlockSpec((LANE, 256), full)] * 3,
        out_specs=[feat] * 8,
        out_shape=[jax.ShapeDtypeStruct((b, 256), F32)] * 8,
        compiler_params=_cp("arbitrary"),
        name="rwkv_step_prep",
    )(colsb, shift0, lw['rwkv_mu'], lw['rwkv_vec'], lw['rwkv_wup'], lw['rwkv_aup'], lw['rwkv_gup'])
    s_rows = jnp.transpose(s_all, (0, 2, 3, 4, 1)).reshape(-1, b)
    o, st = pl.pallas_call(
        _rwkv_step_body,
        grid=(H_B,),
        in_specs=[feat] * 8 + [pl.BlockSpec((8, 256), full), pl.BlockSpec((hd2, b), lambda h: (layer * H_B + h, 0))],
        out_specs=[feat, pl.BlockSpec((hd2, b), lambda h: (h, 0))],
        out_shape=[jax.ShapeDtypeStruct((b, 256), F32), jax.ShapeDtypeStruct((H_B * hd2, b), F32)],
        scratch_shapes=[pltpu.VMEM((6, 256, b), F32), pltpu.VMEM((256, b), F32)],
        compiler_params=_cp("arbitrary"),
        name="rwkv_step",
    )(*prep, lw['rwkv_vec'], s_rows)
    return o, jnp.transpose(st.reshape(H_B, HEAD_DIM, HEAD_DIM, b), (3, 0, 1, 2))


def _ret_step_body(c_ref, cos_ref, sin_ref, gn_ref, r0_ref, o_ref, rt_ref, ft_scr, acc_scr):
    h = pl.program_id(0)

    @pl.when(h == 0)
    def _():
        x = c_ref[...]
        cs, sn = cos_ref[...], sin_ref[...]
        first = (lax.broadcasted_iota(jnp.int32, cs.shape, 1) % HEAD_DIM) < (HEAD_DIM // 2)

        def rope(z):
            sw = jnp.where(first, pltpu.roll(z, 256 - HEAD_DIM // 2, 1), pltpu.roll(z, HEAD_DIM // 2, 1))
            return z * cs + sw * sn

        ft_scr[0] = rope(x[:, 0:256]).T
        ft_scr[1] = (rope(x[:, 256:512]) * (HEAD_DIM ** -0.5)).T
        ft_scr[2] = x[:, 512:768].T

    base = pl.multiple_of(h * HEAD_DIM, HEAD_DIM)
    nseq = rt_ref.shape[1]
    gamma = jnp.exp(jnp.log1p(-jnp.exp2(-5.0 - jnp.full((1, nseq), h, jnp.int32).astype(F32))))
    q_t, k_t, v_t = [ft_scr[n, pl.ds(base, HEAD_DIM), :] for n in range(3)]
    qk = jnp.sum(q_t * k_t, axis=0, keepdims=True)

    def body(d, cross):
        rows = pl.ds(pl.multiple_of(d * HEAD_DIM, HEAD_DIM), HEAD_DIM)
        r_old = r0_ref[rows, :]
        rt_ref[rows, :] = gamma * r_old + ft_scr[1, pl.ds(base + d, 1), :] * v_t
        return cross + ft_scr[0, pl.ds(base + d, 1), :] * r_old

    cross = lax.fori_loop(0, HEAD_DIM, body, jnp.zeros((HEAD_DIM, nseq), F32))
    acc_scr[pl.ds(base, HEAD_DIM), :] = qk * v_t + gamma * cross

    @pl.when(h == H_D - 1)
    def _():
        o_ref[...] = jax.nn.silu(c_ref[:, 768:1024]) * (_segment_norm(acc_scr[...].T, RET_GN_EPS) * gn_ref[...])


def _ret_step(colsd, cos, sin, r_all, layer, gn):
    b = colsd.shape[0]
    hd2 = HEAD_DIM * HEAD_DIM
    full = lambda *_: (0, 0)
    r_rows = jnp.transpose(r_all, (0, 2, 3, 4, 1)).reshape(-1, b)
    o, rt = pl.pallas_call(
        _ret_step_body,
        grid=(H_D,),
        in_specs=[pl.BlockSpec((b, 1024), full), pl.BlockSpec((b, 256), full), pl.BlockSpec((b, 256), full),
                  pl.BlockSpec((1, 256), full), pl.BlockSpec((hd2, b), lambda h: (layer * H_D + h, 0))],
        out_specs=[pl.BlockSpec((b, 256), full), pl.BlockSpec((hd2, b), lambda h: (h, 0))],
        out_shape=[jax.ShapeDtypeStruct((b, 256), F32), jax.ShapeDtypeStruct((H_D * hd2, b), F32)],
        scratch_shapes=[pltpu.VMEM((3, 256, b), F32), pltpu.VMEM((256, b), F32)],
        compiler_params=_cp("arbitrary"),
        name="ret_step",
    )(colsd, cos, sin, gn.reshape(1, 256), r_rows)
    return o, jnp.transpose(rt.reshape(H_D, HEAD_DIM, HEAD_DIM, b), (3, 0, 1, 2))


def _s5_step_body(u_ref, x0_ref, a1_ref, a2_ref, bt_ref, ct_ref, d_ref, wg_ref, o_ref, xt_ref):
    u = u_ref[...]
    dot_hi = lambda a, b: jnp.dot(a, b, precision=HI, preferred_element_type=F32)
    x0 = x0_ref[...]
    n = x0.shape[0]
    even = (lax.broadcasted_iota(jnp.int32, x0.shape, 0) % 2) == 0
    partner = jnp.where(even, pltpu.roll(x0, n - 1, 0), pltpu.roll(x0, 1, 0))
    x = a1_ref[...] * x0 + a2_ref[...] * partner + dot_hi(bt_ref[...], u.T)
    xt_ref[...] = x
    y = dot_hi(ct_ref[...], x).T + d_ref[...] * u
    z = jax.nn.gelu(y)
    o_ref[...] = z * jax.nn.sigmoid(dot_hi(z, wg_ref[...]))


def _s5_step(u, x_all, layer, lw):
    b = u.shape[0]
    a1, a2, bt, ct, d_row = lw['s5_step']
    n = 2 * G_C * S5_P
    full = lambda *_: (0, 0)
    x_rows = jnp.transpose(x_all, (0, 2, 3, 4, 1)).reshape(-1, b)
    o, xt = pl.pallas_call(
        _s5_step_body,
        grid=(1,),
        in_specs=[pl.BlockSpec((b, C_C), full), pl.BlockSpec((n, b), lambda i: (layer, 0)), pl.BlockSpec((n, 1), full),
                  pl.BlockSpec((n, 1), full), pl.BlockSpec((n, C_C), full), pl.BlockSpec((C_C, n), full),
                  pl.BlockSpec((1, C_C), full), pl.BlockSpec((C_C, C_C), full)],
        out_specs=[pl.BlockSpec((b, C_C), full), pl.BlockSpec((n, b), full)],
        out_shape=[jax.ShapeDtypeStruct((b, C_C), F32), jax.ShapeDtypeStruct((n, b), F32)],
        compiler_params=_cp("arbitrary"),
        name="s5_step",
    )(u, x_rows, a1, a2, bt, ct, d_row, lw['s5_w_glu'])
    return o, jnp.transpose(xt.reshape(G_C, S5_P, 2, b), (3, 0, 1, 2))


def _s5_params(lw):
    lr, li = lw['s5_lambda_re'], lw['s5_lambda_im']
    dt = jnp.exp(lw['s5_log_step'])[:, None]
    mag = jnp.exp(lr * dt)
    ar, ai = mag * jnp.cos(li * dt), mag * jnp.sin(li * dt)
    nr, ni = ar - 1.0, ai
    den = lr * lr + li * li
    fr, fi = (nr * lr + ni * li) / den, (ni * lr - nr * li) / den
    b_re, b_im = lw['s5_b'][0], lw['s5_b'][1]
    bbr = fr[..., None] * b_re - fi[..., None] * b_im
    bbi = fr[..., None] * b_im + fi[..., None] * b_re
    eye = jnp.eye(G_C, dtype=F32)
    bd_in = lambda m: jnp.einsum('gpc,gh->gchp', m, eye).reshape(G_C * S5_CH, G_C * S5_P)
    bd_out = lambda m: jnp.einsum('gcp,gh->gphc', m, eye).reshape(G_C * S5_P, G_C * S5_CH)
    b_big = jnp.concatenate([bd_in(bbr), bd_in(bbi)], axis=1)
    c_big = jnp.concatenate([bd_out(lw['s5_c'][0]), -bd_out(lw['s5_c'][1])], axis=0)
    a_row = jnp.concatenate([ar.reshape(1, -1), ai.reshape(1, -1)], axis=1)
    d_row = lw['s5_d'].reshape(1, C_C)
    n = 2 * G_C * S5_P
    a1 = jnp.stack([ar, ar], axis=-1).reshape(n, 1)
    a2 = jnp.stack([-ai, ai], axis=-1).reshape(n, 1)
    bt = jnp.stack([jnp.einsum('gpc,gh->gphc', bbr, eye), jnp.einsum('gpc,gh->gphc', bbi, eye)], axis=2).reshape(n, G_C * S5_CH)
    ct = jnp.stack([jnp.einsum('gcp,gh->gchp', lw['s5_c'][0], eye), -jnp.einsum('gcp,gh->gchp', lw['s5_c'][1], eye)],
                   axis=-1).reshape(G_C * S5_CH, n)
    return (a_row, b_big, c_big, d_row), (a1, a2, bt, ct, d_row)


def _s5_body(u_ref, x0_ref, a_ref, b_ref, c_ref, d_ref, wg_ref, o_ref, xt_ref, x_scr, bu_scr, xs_scr, *, nb, tt, mm_dtype, prec):
    sub = 8

    @pl.when(pl.program_id(0) == 0)
    def _():
        x_scr[...] = jnp.zeros_like(x_scr)
        x_scr[0:nb, :] = x0_ref[...]
        bu_scr[...] = jnp.zeros_like(bu_scr)

    np_ = G_C * S5_P
    ncb = np_ // LANE
    for b in range(nb):
        bu = jnp.dot(u_ref[b].astype(mm_dtype), b_ref[...], precision=prec, preferred_element_type=F32)
        for cb in range(2 * ncb):
            bu_scr[cb, pl.ds(b, tt, stride=sub), :] = bu[:, cb * LANE:(cb + 1) * LANE]
    a = a_ref[...]

    def step(t, x):
        rows = pl.ds(pl.multiple_of(t * sub, sub), sub)
        new = [None] * (2 * ncb)
        for cb in range(ncb):
            re, im = slice(cb * LANE, (cb + 1) * LANE), slice(np_ + cb * LANE, np_ + (cb + 1) * LANE)
            ar, ai, xr, xi = a[:, re], a[:, im], x[:, re], x[:, im]
            new[cb] = ar * xr - ai * xi + bu_scr[cb, rows, :]
            new[ncb + cb] = ar * xi + ai * xr + bu_scr[ncb + cb, rows, :]
            xs_scr[cb, rows, :] = new[cb]
            xs_scr[ncb + cb, rows, :] = new[ncb + cb]
        return jnp.concatenate(new, axis=1)

    x_last = lax.fori_loop(0, tt, step, x_scr[...], unroll=8)
    x_scr[...] = x_last
    xt_ref[...] = x_last[0:nb, :]
    for b in range(nb):
        u = u_ref[b]
        xs = jnp.concatenate([xs_scr[cb, pl.ds(b, tt, stride=sub), :] for cb in range(2 * ncb)], axis=1)
        y = jnp.dot(xs.astype(mm_dtype), c_ref[...], precision=prec, preferred_element_type=F32) + d_ref[...] * u
        z = jax.nn.gelu(y)
        o_ref[b] = z * jax.nn.sigmoid(jnp.dot(z.astype(mm_dtype), wg_ref[...], precision=prec, preferred_element_type=F32))


def _s5_mixer(u, x0, lw, b, t, tt, exact):
    a_row, b_big, c_big, d_row = lw['s5']
    mm_dtype = F32 if exact else BF16
    prec = HI if exact else None
    x0l = jnp.concatenate([x0[..., 0].reshape(b, -1), x0[..., 1].reshape(b, -1)], axis=1)
    np2 = 2 * G_C * S5_P
    fixed = lambda i: (0, 0)
    o, xt = pl.pallas_call(
        functools.partial(_s5_body, nb=b, tt=tt, mm_dtype=mm_dtype, prec=prec),
        grid=(t // tt,),
        in_specs=[pl.BlockSpec((b, tt, C_C), lambda i: (0, i, 0)), pl.BlockSpec((b, np2), fixed),
                  pl.BlockSpec((1, np2), fixed), pl.BlockSpec((C_C, np2), fixed), pl.BlockSpec((np2, C_C), fixed),
                  pl.BlockSpec((1, C_C), fixed), pl.BlockSpec((C_C, C_C), fixed)],
        out_specs=[pl.BlockSpec((b, tt, C_C), lambda i: (0, i, 0)), pl.BlockSpec((b, np2), fixed)],
        out_shape=[jax.ShapeDtypeStruct((b, t, C_C), F32), jax.ShapeDtypeStruct((b, np2), F32)],
        scratch_shapes=[pltpu.VMEM((8, np2), F32), pltpu.VMEM((np2 // LANE, 8 * tt, LANE), F32),
                        pltpu.VMEM((np2 // LANE, 8 * tt, LANE), F32)],
        compiler_params=_cp("arbitrary"),
        name="s5",
    )(u.reshape(b, t, C_C), x0l, a_row, b_big.astype(mm_dtype), c_big.astype(mm_dtype), d_row, lw['s5_w_glu'].astype(mm_dtype))
    xt = xt.reshape(b, 2, G_C, S5_P)
    return o.reshape(b * t, C_C), jnp.stack([xt[:, 0], xt[:, 1]], axis=-1)


def _ret_tables(pos, c):
    cos, sin = _rope_tables(pos, HEAD_DIM, RET_THETA, HEAD_DIM, H_D)
    log_g = jnp.log1p(-jnp.exp2(-5.0 - jnp.arange(H_D, dtype=F32)))
    i = jnp.arange(c, dtype=F32)
    diff = i[:, None] - i[None, :]
    dmat = jnp.where(diff >= 0, jnp.exp(jnp.maximum(diff, 0.0)[None] * log_g[:, None, None]), 0.0).reshape(H_D * c, c)
    q_dec = jnp.repeat(jnp.exp((i + 1.0)[None] * log_g[:, None]).T, HEAD_DIM, axis=1)
    k_dec = jnp.repeat(jnp.exp((c - 1.0 - i)[None] * log_g[:, None]).T, HEAD_DIM, axis=1)
    chunk_dec = jnp.repeat(jnp.exp(c * log_g), HEAD_DIM).reshape(256, 1)
    return cos, sin, dmat, q_dec, k_dec, chunk_dec


def _ret_body(c_ref, cos_ref, sin_ref, dmat_ref, qdec_ref, kdec_ref, cdec_ref, r0_ref, gn_ref, o_ref, rt_ref, r_scr, *, c):
    @pl.when(pl.program_id(1) == 0)
    def _():
        r_scr[...] = r0_ref[0]

    x = c_ref[0]
    q, k, v, g = x[:, 0:256], x[:, 256:512], x[:, 512:768], x[:, 768:1024]
    cs, sn = cos_ref[...], sin_ref[...]
    lane = lax.broadcasted_iota(jnp.int32, (c, 256), 1)
    first = (lane % HEAD_DIM) < (HEAD_DIM // 2)

    def rope(z):
        sw = jnp.where(first, pltpu.roll(z, 256 - HEAD_DIM // 2, 1), pltpu.roll(z, HEAD_DIM // 2, 1))
        return z * cs + sw * sn

    q = rope(q)
    k = rope(k) * (HEAD_DIM ** -0.5)
    head = lane // HEAD_DIM
    kb, vb = k.astype(BF16), v.astype(BF16)
    qstack = jnp.concatenate([jnp.where(head == h, q, 0.0) for h in range(H_D)], axis=0).astype(BF16)
    s = lax.dot_general(qstack, kb, (((1,), (1,)), ((), ())), preferred_element_type=F32) * dmat_ref[...]
    pv = jnp.dot(s.astype(BF16), vb, preferred_element_type=F32)
    inner = jnp.zeros((c, 256), F32)
    for h in range(H_D):
        inner = inner + jnp.where(head == h, pv[h * c:(h + 1) * c], 0.0)
    r_old = r_scr[...]
    cross = jnp.dot((q * qdec_ref[...]).astype(BF16), r_old.astype(BF16), preferred_element_type=F32)
    kv = lax.dot_general((k * kdec_ref[...]).astype(BF16), vb, (((0,), (0,)), ((), ())), preferred_element_type=F32)
    bd = _block_ones(256, HEAD_DIM, F32)
    r_new = cdec_ref[...] * r_old + kv * bd
    r_scr[...] = r_new
    rt_ref[0] = r_new
    o_ref[0] = jax.nn.silu(g) * (_segment_norm(inner + cross, RET_GN_EPS) * gn_ref[...])


def _ret_mixer(colsd, r0, lw, tabs, b, t):
    c = RET_CHUNK if t % RET_CHUNK == 0 else t
    cos, sin, dmat, q_dec, k_dec, chunk_dec = tabs
    eye = jnp.eye(H_D, dtype=F32)
    r0l = jnp.einsum('bhde,hg->bhdge', r0, eye).reshape(b, 256, 256)
    n_t = t // c
    fixed = lambda i, j: (0, 0)
    o, rt = pl.pallas_call(
        functools.partial(_ret_body, c=c),
        grid=(b, n_t),
        in_specs=[pl.BlockSpec((1, c, 1024), lambda i, j: (i, j, 0)),
                  pl.BlockSpec((c, 256), lambda i, j: (j, 0)), pl.BlockSpec((c, 256), lambda i, j: (j, 0)),
                  pl.BlockSpec((H_D * c, c), fixed), pl.BlockSpec((c, 256), fixed), pl.BlockSpec((c, 256), fixed),
                  pl.BlockSpec((256, 1), fixed), pl.BlockSpec((1, 256, 256), lambda i, j: (i, 0, 0)),
                  pl.BlockSpec((1, 256), fixed)],
        out_specs=[pl.BlockSpec((1, c, 256), lambda i, j: (i, j, 0)), pl.BlockSpec((1, 256, 256), lambda i, j: (i, 0, 0))],
        out_shape=[jax.ShapeDtypeStruct((b, t, 256), F32), jax.ShapeDtypeStruct((b, 256, 256), F32)],
        scratch_shapes=[pltpu.VMEM((256, 256), F32)],
        compiler_params=_cp("parallel", "arbitrary"),
        name="retention",
    )(colsd.reshape(b, t, 1024), cos, sin, dmat, q_dec, k_dec, chunk_dec, r0l, lw['ret_gn'].reshape(1, 256))
    rt = jnp.einsum('bhdge,hg->bhde', rt.reshape(b, H_D, HEAD_DIM, H_D, HEAD_DIM), eye)
    return o.reshape(b * t, 256), rt


def _partner(x, d, period):
    pos = lax.broadcasted_iota(jnp.int32, x.shape, 1) % period
    return jnp.where(pos + d < period, pltpu.roll(x, LANE - d, 1), pltpu.roll(x, period - d, 1))


def _out_body(x_ref, oa_ref, ob_ref, oc_ref, od_ref, w_ref, nw_ref, wr_ref, br_ref, x1_ref, h_ref, comb_ref):
    acc = x_ref[...]
    for i, ref in enumerate((oa_ref, ob_ref, oc_ref, od_ref)):
        acc = acc + jnp.dot(ref[...].astype(BF16), w_ref[256 * i:256 * (i + 1), :], preferred_element_type=F32)
    x1_ref[...] = acc
    h = acc * lax.rsqrt(jnp.mean(acc * acc, axis=-1, keepdims=True) + RMS_EPS) * nw_ref[...]
    hb = h.astype(BF16)
    h_ref[...] = hb
    h_lo = (h - hb.astype(F32)).astype(BF16)
    logits = (jnp.dot(hb, wr_ref[0], preferred_element_type=F32) + jnp.dot(hb, wr_ref[1], preferred_element_type=F32)
              + jnp.dot(h_lo, wr_ref[0], preferred_element_type=F32) + br_ref[...])
    le, lg = logits[:, :LANE], logits[:, LANE:]
    lane = lax.broadcasted_iota(jnp.int32, le.shape, 1)
    mg = jnp.max(lg, axis=-1, keepdims=True)
    eg = jnp.exp(lg - mg)
    pg = eg / (jnp.sum(eg, axis=-1, keepdims=True) * (1.0 / 32.0))
    gidx = (lane % N_EXPERTS) // EXP_PER_GROUP
    g_rank = jnp.zeros_like(pg)
    for d in range(1, N_GROUPS):
        other = pltpu.roll(pg, LANE - EXP_PER_GROUP * d, 1)
        wrapped = gidx + d >= N_GROUPS
        g_rank = g_rank + jnp.where((other > pg) | ((other == pg) & wrapped), 1.0, 0.0)
    kidx = lane % EXP_PER_GROUP
    others = [_partner(le, d, EXP_PER_GROUP) for d in range(1, EXP_PER_GROUP)]
    me = functools.reduce(jnp.maximum, others, le)
    ee = jnp.exp(le - me)
    se = ee
    for d in range(1, EXP_PER_GROUP):
        se = se + _partner(ee, d, EXP_PER_GROUP)
    pe = ee / se
    e_rank = jnp.zeros_like(pe)
    for d in range(1, EXP_PER_GROUP):
        other = _partner(pe, d, EXP_PER_GROUP)
        wrapped = kidx + d >= EXP_PER_GROUP
        e_rank = e_rank + jnp.where((other > pe) | ((other == pe) & wrapped), 1.0, 0.0)
    top = jnp.where(e_rank < 2.0, pe, 0.0)
    den = top
    for d in range(1, EXP_PER_GROUP):
        den = den + _partner(top, d, EXP_PER_GROUP)
    comb = jnp.where((g_rank < 1.0) & (lane < N_EXPERTS), pg * (top / den), 0.0)
    comb_ref[...] = comb


def _out_router(x, oa, ob, oc, od, lw, tm):
    n = x.shape[0]
    row = lambda i: (i, 0)
    fixed = lambda i: (0, 0)
    mix = pl.BlockSpec((tm, 256), row)
    return pl.pallas_call(
        _out_body,
        grid=(n // tm,),
        in_specs=[pl.BlockSpec((tm, D_MODEL), row), mix, mix, mix, mix,
                  pl.BlockSpec((D_MODEL, D_MODEL), fixed), pl.BlockSpec((1, D_MODEL), fixed),
                  pl.BlockSpec((2, D_MODEL, 2 * LANE), lambda i: (0, 0, 0)), pl.BlockSpec((1, 2 * LANE), fixed)],
        out_specs=[pl.BlockSpec((tm, D_MODEL), row), pl.BlockSpec((tm, D_MODEL), row), pl.BlockSpec((tm, LANE), row)],
        out_shape=[jax.ShapeDtypeStruct((n, D_MODEL), F32), jax.ShapeDtypeStruct((n, D_MODEL), BF16),
                   jax.ShapeDtypeStruct((n, LANE), F32)],
        compiler_params=_cp("parallel"),
        name="out_router",
    )(x, oa, ob, oc, od, lw['w_out'], lw['norm_ffn'], lw['w_router'], lw['b_router'])


def _router_weights(w_grp, b_grp, w_exp, b_exp):
    we = jnp.transpose(w_exp, (1, 0, 2)).reshape(D_MODEL, N_EXPERTS)
    wg = jnp.repeat(w_grp, EXP_PER_GROUP, axis=1)
    reps = LANE // N_EXPERTS
    w = jnp.concatenate([jnp.tile(we, (1, reps)), jnp.tile(wg, (1, reps))], axis=1)
    b = jnp.concatenate([jnp.tile(b_exp.reshape(1, N_EXPERTS), (1, reps)),
                         jnp.tile(jnp.repeat(b_grp, EXP_PER_GROUP).reshape(1, N_EXPERTS), (1, reps))], axis=1)
    w_hi = w.astype(BF16)
    return jnp.stack([w_hi, (w - w_hi.astype(F32)).astype(BF16)]), b


def _moe_body(h_ref, comb_ref, x1_ref, wg_ref, wu_ref, wd_ref, nf_ref, *out_refs, final):
    acc_ref = out_refs[-1]
    e = pl.program_id(1)

    @pl.when(e == 0)
    def _():
        acc_ref[...] = x1_ref[...]

    h = h_ref[...]
    comb = comb_ref[...]
    lane = lax.broadcasted_iota(jnp.int32, comb.shape, 1)
    c = jnp.sum(jnp.where(lane == e, comb, 0.0), axis=-1, keepdims=True)
    hg = jnp.dot(h, wg_ref[0], preferred_element_type=F32)
    hu = jnp.dot(h, wu_ref[0], preferred_element_type=F32)
    act = (jax.nn.silu(hg) * hu * c).astype(BF16)
    acc_ref[...] += jnp.dot(act, wd_ref[0], preferred_element_type=F32)

    @pl.when(e == N_EXPERTS - 1)
    def _():
        x2 = acc_ref[...]
        if final:
            out_refs[0][...] = x2 * lax.rsqrt(jnp.mean(x2 * x2, axis=-1, keepdims=True) + RMS_EPS) * nf_ref[...]
        else:
            out_refs[0][...] = x2


def _moe(h, comb, x1, lw, norm_final, tm, final):
    n = x1.shape[0]
    row = lambda i, e: (i, 0)
    per_e = lambda i, e: (e, 0, 0)
    return pl.pallas_call(
        functools.partial(_moe_body, final=final),
        grid=(n // tm, N_EXPERTS),
        in_specs=[pl.BlockSpec((tm, D_MODEL), row), pl.BlockSpec((tm, LANE), row), pl.BlockSpec((tm, D_MODEL), row),
                  pl.BlockSpec((1, D_MODEL, D_EXPERT), per_e), pl.BlockSpec((1, D_MODEL, D_EXPERT), per_e),
                  pl.BlockSpec((1, D_EXPERT, D_MODEL), per_e), pl.BlockSpec((1, D_MODEL), lambda i, e: (0, 0))],
        out_specs=pl.BlockSpec((tm, D_MODEL), row),
        out_shape=jax.ShapeDtypeStruct((n, D_MODEL), F32),
        scratch_shapes=[pltpu.VMEM((tm, D_MODEL), F32)],
        compiler_params=_cp("parallel", "arbitrary"),
        name="moe",
    )(h, comb, x1, lw['moe_wg'], lw['moe_wu'], lw['moe_wd'], norm_final.reshape(1, D_MODEL))


def _prep_layer(l, p):
    w_in = p['w_in'][l]
    o = _offsets(SPLIT_SIZES)
    segs = jnp.split(w_in, o, axis=1)
    w_all = jnp.concatenate([segs[0], segs[1], _pad_to(segs[2], LANE, 1), _pad_to(segs[3], SHIFT_PAD, 1), segs[4], segs[5]],
                            axis=1).astype(BF16)
    w_prec = w_in[:, :PRECISE_COLS]
    w_lo = (w_prec - w_prec.astype(BF16).astype(F32)).astype(BF16)
    lw = {'layer': l, 'w_all': w_all, 'w_lo': w_lo, 'norm_mix': p['norm_mix'][l]}
    lw['cmp'] = _cmp_weights(p['nsa_cmp_w1'][l], p['nsa_cmp_b1'][l], p['nsa_cmp_w2'][l])
    lw['rwkv_mu'] = _pad_to(p['rwkv_mu'][l].reshape(1, SHIFT_B), SHIFT_PAD, 1)
    lw['rwkv_vec'] = _pad_to(p['rwkv_vec'][l], 8, 0)
    z = lambda a, b: jnp.zeros((a, b), F32)
    lw['rwkv_wup'] = jnp.concatenate([p['rwkv_w_up'][l], z(LANE - LORA_W, C_B)], axis=0)
    lw['rwkv_aup'] = jnp.concatenate([z(LORA_W, C_B), p['rwkv_a_up'][l], z(LANE - LORA_W - LORA_A, C_B)], axis=0)
    lw['rwkv_gup'] = jnp.concatenate([z(LORA_W + LORA_A, C_B), p['rwkv_g_up'][l], z(LANE - LORA_W - LORA_A - LORA_G, C_B)], axis=0)
    for name in ('s5_lambda_re', 's5_lambda_im', 's5_b', 's5_c', 's5_d', 's5_log_step', 's5_w_glu', 'ret_gn'):
        lw[name] = p[name][l]
    lw['s5'], lw['s5_step'] = _s5_params(lw)
    lw['w_out'] = p['w_out'][l].astype(BF16)
    lw['norm_ffn'] = p['norm_ffn'][l].reshape(1, D_MODEL)
    lw['w_router'], lw['b_router'] = _router_weights(p['moe_w_grp'][l], p['moe_b_grp'][l], p['moe_w_exp'][l], p['moe_b_exp'][l])
    lw['moe_wg'] = p['moe_w_gate'][l].astype(BF16)
    lw['moe_wu'] = p['moe_w_up'][l].astype(BF16)
    lw['moe_wd'] = p['moe_w_down'][l].astype(BF16)
    return lw


ROW_TILE = 512
MOE_ROW_TILE = 1024
SCAN_TILE = 256


def _prompt_layer(x, lw, tabs, b, t, norm_final):
    cos_a, sin_a, ret_tabs = tabs
    q, kv, gate, colsb, u, colsd = _proj(x, lw['norm_mix'], lw['w_all'], lw['w_lo'], cos_a, sin_a, ROW_TILE)
    o_a = _nsa_prompt_mixer(q, kv, gate, lw, b, t)
    o_b, s_rwkv, s_shift = _rwkv_mixer(colsb, jnp.zeros((b, SHIFT_PAD), F32), jnp.zeros((b, H_B, HEAD_DIM, HEAD_DIM), F32),
                                       lw, b, t, ROW_TILE, b, SCAN_TILE)
    o_c, s_s5 = _s5_mixer(u, jnp.zeros((b, G_C, S5_P, 2), F32), lw, b, t, SCAN_TILE, False)
    o_d, s_ret = _ret_mixer(colsd, jnp.zeros((b, H_D, HEAD_DIM, HEAD_DIM), F32), lw, ret_tabs, b, t)
    x1, h, comb = _out_router(x, o_a, o_b, o_c, o_d, lw, ROW_TILE)
    x2 = _moe(h, comb, x1, lw, norm_final, MOE_ROW_TILE, lw['layer'] == DEPTH - 1)
    kv3 = kv.reshape(b, t, 6 * LANE)
    rows = kv3[:, :, :4 * LANE].reshape(b, t, 4, N_KV_A, HEAD_DIM)
    win = kv3[:, t - min(WINDOW, t):, 4 * LANE:].reshape(b, min(WINDOW, t), 2, N_KV_A, HEAD_DIM)
    return x2, (rows, win, s_rwkv, s_shift, s_s5, s_ret)


def _sample_layer(x, lw, tabs, b, pos, cache_kv, page_table, win_buf, s_rwkv, s_shift, s_s5, s_ret, norm_final):
    cos_a, sin_a, ret_cs = tabs
    q, kv, gate, colsb, u, colsd = _proj(x, lw['norm_mix'], lw['w_all'], lw['w_lo'], cos_a, sin_a, b)
    o_a, win = _nsa_sample(q, kv, gate, cache_kv, win_buf, lw['layer'], page_table, lw['cmp'], int(pos[0]))
    rows = kv[:, :4 * LANE].reshape(b, 1, 4, N_KV_A, HEAD_DIM)
    o_b, s_rwkv = _rwkv_step(colsb, _pad_to(s_shift, SHIFT_PAD, 1), s_rwkv, lw['layer'], lw)
    s_shift = colsb[:, :SHIFT_B]
    o_c, s_s5 = _s5_step(u, s_s5, lw['layer'], lw)
    o_d, s_ret = _ret_step(colsd, ret_cs[0], ret_cs[1], s_ret, lw['layer'], lw['ret_gn'])
    x1, h, comb = _out_router(x, o_a, o_b, o_c, o_d, lw, b)
    x2 = _moe(h, comb, x1, lw, norm_final, b, lw['layer'] == DEPTH - 1)
    return x2, (rows, win, s_rwkv, s_shift, s_s5, s_ret)


def kernel(x_prompt, x_sample, cache_nsa_kv, cache_nsa_win, state_rwkv, state_rwkv_shift, state_s5, state_ret, page_table, norm_mix, w_in, nsa_cmp_w1, nsa_cmp_b1, nsa_cmp_w2, rwkv_mu, rwkv_vec, rwkv_w_up, rwkv_a_up, rwkv_g_up, s5_lambda_re, s5_lambda_im, s5_b, s5_c, s5_d, s5_log_step, s5_w_glu, ret_gn, w_out, norm_ffn, moe_w_grp, moe_b_grp, moe_w_exp, moe_b_exp, moe_w_gate, moe_w_up, moe_w_down, norm_final):
    p = dict(norm_mix=norm_mix, w_in=w_in, nsa_cmp_w1=nsa_cmp_w1, nsa_cmp_b1=nsa_cmp_b1, nsa_cmp_w2=nsa_cmp_w2,
             rwkv_mu=rwkv_mu, rwkv_vec=rwkv_vec, rwkv_w_up=rwkv_w_up, rwkv_a_up=rwkv_a_up, rwkv_g_up=rwkv_g_up,
             s5_lambda_re=s5_lambda_re, s5_lambda_im=s5_lambda_im, s5_b=s5_b, s5_c=s5_c, s5_d=s5_d,
             s5_log_step=s5_log_step, s5_w_glu=s5_w_glu, ret_gn=ret_gn, w_out=w_out, norm_ffn=norm_ffn,
             moe_w_grp=moe_w_grp, moe_b_grp=moe_b_grp, moe_w_exp=moe_w_exp, moe_b_exp=moe_b_exp,
             moe_w_gate=moe_w_gate, moe_w_up=moe_w_up, moe_w_down=moe_w_down)
    bp, tp = x_prompt.shape[:2]
    bs, ts = x_sample.shape[:2]
    assert ts == 1 and tp % ROW_TILE == 0 and (bp * tp) % MOE_ROW_TILE == 0 and tp % (4 * LANE) == 0 and bs % 8 == 0
    past_len = page_table.shape[1] * cache_nsa_kv.shape[2]
    pos_p = np.arange(tp)
    pos_s = past_len + np.arange(ts)
    c = RET_CHUNK if tp % RET_CHUNK == 0 else tp
    tabs_p = _rope_tables(pos_p, ROT_DIM, ROPE_THETA, HEAD_DIM, 2) + (_ret_tables(pos_p, c),)
    pos_rows = np.repeat(pos_s, bs)
    tabs_s = _rope_tables(pos_rows, ROT_DIM, ROPE_THETA, HEAD_DIM, 2) + (_rope_tables(pos_rows, HEAD_DIM, RET_THETA, HEAD_DIM, H_D),)
    xp = x_prompt.reshape(bp * tp, D_MODEL)
    xs = x_sample.reshape(bs * ts, D_MODEL)
    sts_p, sts_s = [], []
    for l in range(DEPTH):
        lw = _prep_layer(l, p)
        xp, st_p = _prompt_layer(xp, lw, tabs_p, bp, tp, norm_final)
        xs, st_s = _sample_layer(xs, lw, tabs_s, bs, pos_s, cache_nsa_kv, page_table, cache_nsa_win, state_rwkv,
                                     state_rwkv_shift[l], state_s5, state_ret, norm_final)
        rows, win, s1, s2, s3, s4 = st_s
        sts_s.append((rows, win, s1, s2, s3, s4))
        sts_p.append(st_p)
    new_p = [jnp.stack([st[i] for st in sts_p]) for i in range(6)]
    new_s = [jnp.stack([st[i] for st in sts_s]) for i in range(6)]
    return (xp.reshape(bp, tp, D_MODEL), xs.reshape(bs, ts, D_MODEL), new_p[0], new_s[0], new_p[1], new_s[1],
            new_p[2], new_s[2], new_p[3], new_s[3], new_p[4], new_s[4], new_p[5], new_s[5])
```

```python
import functools

import numpy as np
import jax
import jax.numpy as jnp
from jax import lax
from jax.experimental import pallas as pl
from jax.experimental.pallas import tpu as pltpu

F32 = jnp.float32
BF16 = jnp.bfloat16
HI = lax.Precision.HIGHEST

D_MODEL = 1024
DEPTH = 2
HEAD_DIM = 64
C_A = C_B = C_C = C_D = 256
H_A = 4
N_KV_A = 2
R_A = 2
ROT_DIM = 16
ROPE_THETA = 500000.0
CMP_BLOCK = 32
CMP_STRIDE = 16
CMP_HIDDEN = 128
SEL_BLOCK = 64
TOP_K = 16
WINDOW = 512
NEG_INF = -1e30
FORCED_SCORE = 1e9
BLOCKED_SCORE = -1e9
H_B = 4
LORA_W = 16
LORA_A = 16
LORA_G = 32
SHIFT_B = 832
SHIFT_PAD = 896
RWKV_GN_EPS = 64e-5
S5_CH = 16
G_C = 16
S5_P = 64
H_D = 4
RET_CHUNK = 128
RET_THETA = 10000.0
RET_GN_EPS = 1e-5
N_GROUPS = 4
EXP_PER_GROUP = 4
N_EXPERTS = 16
D_EXPERT = 256
RMS_EPS = 1e-6
SPLIT_SIZES = (C_A, 6 * N_KV_A * HEAD_DIM, 3 * H_A, SHIFT_B, C_C, 4 * C_D)
LANE = 128
VMEM_LIMIT = 56 * 1024 * 1024


def _cp(*sem, fuse_inputs=None):
    return pltpu.CompilerParams(dimension_semantics=sem, vmem_limit_bytes=VMEM_LIMIT, allow_input_fusion=fuse_inputs)


def _offsets(sizes):
    return [int(s) for s in np.cumsum(sizes)[:-1]]


def _pad_to(a, n, axis):
    pad = [(0, 0)] * a.ndim
    pad[axis] = (0, n - a.shape[axis])
    return jnp.pad(a, pad)


def _block_ones(n, blk, dtype):
    r = lax.broadcasted_iota(jnp.int32, (n, n), 0) // blk
    c = lax.broadcasted_iota(jnp.int32, (n, n), 1) // blk
    return (r == c).astype(dtype)


def _rope_tables(pos, rot_dim, theta, period, reps):
    half = rot_dim // 2
    inv = theta ** (-jnp.arange(half, dtype=F32) / half)
    ang = jnp.asarray(pos, F32)[:, None] * inv[None, :]
    cos, sin = jnp.cos(ang), jnp.sin(ang)
    n = ang.shape[0]
    rest = period - rot_dim
    c = jnp.concatenate([cos, cos, jnp.ones((n, rest), F32)], -1)
    s = jnp.concatenate([-sin, sin, jnp.zeros((n, rest), F32)], -1)
    return jnp.tile(c, (1, reps)), jnp.tile(s, (1, reps))


PRECISE_COLS = 384


def _proj_body(x_ref, nw_ref, w_ref, wlo_ref, cos_ref, sin_ref, q_ref, kv_ref, g_ref, cb_ref, u_ref, cd_ref):
    x = x_ref[...]
    h = x * lax.rsqrt(jnp.mean(x * x, axis=-1, keepdims=True) + RMS_EPS) * nw_ref[...]
    hb = h.astype(BF16)
    h_lo = (h - hb.astype(F32)).astype(BF16)
    c = cos_ref[...]
    s = sin_ref[...]
    first = (lax.broadcasted_iota(jnp.int32, c.shape, 1) % HEAD_DIM) < (ROT_DIM // 2)

    def rope(z):
        sw = jnp.where(first, pltpu.roll(z, LANE - ROT_DIM // 2, 1), pltpu.roll(z, ROT_DIM // 2, 1))
        return z * c + sw * s

    def dot(a, b):
        z = jnp.dot(hb, w_ref[:, a:b], preferred_element_type=F32)
        if b <= PRECISE_COLS:
            z = z + (jnp.dot(hb, wlo_ref[:, a:b], preferred_element_type=F32)
                     + jnp.dot(h_lo, w_ref[:, a:b], preferred_element_type=F32))
        return z

    for j in range(2):
        q_ref[:, LANE * j:LANE * (j + 1)] = rope(dot(LANE * j, LANE * (j + 1)))
    for j in range(6):
        z = dot(256 + LANE * j, 256 + LANE * (j + 1))
        kv_ref[:, LANE * j:LANE * (j + 1)] = rope(z) if j % 2 == 0 else z
    g_ref[...] = dot(1024, 1152)
    cb_ref[...] = dot(1152, 2048)
    u_ref[...] = dot(2048, 2304)
    cd_ref[...] = dot(2304, 3328)


def _proj(x2d, norm_w, w_all, w_lo, cos_t, sin_t, tm):
    n = x2d.shape[0]
    t_tiles = cos_t.shape[0] // tm
    row = lambda i: (i, 0)
    fixed = lambda i: (0, 0)
    tab = lambda i: (i % t_tiles, 0)
    widths = (256, 768, 128, SHIFT_PAD, 256, 1024)
    return pl.pallas_call(
        _proj_body,
        grid=(n // tm,),
        in_specs=[pl.BlockSpec((tm, D_MODEL), row), pl.BlockSpec((1, D_MODEL), fixed),
                  pl.BlockSpec((D_MODEL, 3328), fixed), pl.BlockSpec((D_MODEL, PRECISE_COLS), fixed),
                  pl.BlockSpec((tm, LANE), tab), pl.BlockSpec((tm, LANE), tab)],
        out_specs=[pl.BlockSpec((tm, w), row) for w in widths],
        out_shape=[jax.ShapeDtypeStruct((n, w), F32) for w in widths],
        compiler_params=_cp("parallel"),
        name="proj",
    )(x2d, norm_w.reshape(1, D_MODEL), w_all, w_lo, cos_t, sin_t)


def _cmp_mlp(xc, kind, w1_ref, b1_ref, w2_ref, w1k_ref, w2k_ref, nck):
    if kind == 0:
        hh = _dot3_presplit(xc, w1k_ref)
    else:
        hh = jnp.dot(xc.astype(BF16), w1_ref[kind], preferred_element_type=F32)
    hs = []
    for g in range(N_KV_A):
        hg = hh[g * nck:(g + 1) * nck]
        hs.append(jax.nn.gelu(b1_ref[kind] + hg[:, :CMP_HIDDEN] + pltpu.roll(hg[:, CMP_HIDDEN:], nck - 1, 0)))
    act = jnp.concatenate(hs, axis=1)
    if kind == 0:
        return _dot3_presplit(act, w2k_ref)
    return jnp.dot(act.astype(BF16), w2_ref[kind], preferred_element_type=F32)


def _cmp_body(xk_ref, xv_ref, w1_ref, b1_ref, w2_ref, w1k_ref, w2k_ref, kc_ref, vc_ref, vct_ref, xc_ref, *, n_chunks):
    lane = lax.broadcasted_iota(jnp.int32, (n_chunks, LANE), 1)
    lo = lane < HEAD_DIM
    for pair in range(CMP_STRIDE // 2):
        for kind, x_ref in enumerate((xk_ref, xv_ref)):
            ak = x_ref[0, pl.ds(2 * pair, n_chunks, stride=CMP_STRIDE), :]
            bk = x_ref[0, pl.ds(2 * pair + 1, n_chunks, stride=CMP_STRIDE), :]
            xc_ref[kind, 0:n_chunks, LANE * pair:LANE * (pair + 1)] = jnp.where(lo, ak, pltpu.roll(bk, HEAD_DIM, 1))
            xc_ref[kind, n_chunks:2 * n_chunks, LANE * pair:LANE * (pair + 1)] = jnp.where(lo, pltpu.roll(ak, HEAD_DIM, 1), bk)
    outs = [_cmp_mlp(xc_ref[kind], kind, w1_ref, b1_ref, w2_ref, w1k_ref, w2k_ref, n_chunks) for kind in range(2)]
    kc_ref[0] = outs[0]
    vc_ref[0] = outs[1]
    vct_ref[0] = outs[1].T


def _nsa_compress(rows, w1cat, b1, w2bd, w1k, w2k):
    b, tk = rows.shape[0], rows.shape[1]
    n_chunks = tk // CMP_STRIDE
    fixed3 = lambda i: (0, 0, 0)
    fixed2 = lambda i: (0, 0)
    return pl.pallas_call(
        functools.partial(_cmp_body, n_chunks=n_chunks),
        grid=(b,),
        in_specs=[pl.BlockSpec((1, tk, LANE), lambda i: (i, 0, 0)), pl.BlockSpec((1, tk, LANE), lambda i: (i, 0, 1)),
                  pl.BlockSpec((2, CMP_STRIDE * HEAD_DIM, 2 * CMP_HIDDEN), fixed3),
                  pl.BlockSpec((2, 1, CMP_HIDDEN), fixed3),
                  pl.BlockSpec((2, 2 * CMP_HIDDEN, LANE), fixed3),
                  pl.BlockSpec((2, CMP_STRIDE * HEAD_DIM, 2 * CMP_HIDDEN), fixed3), pl.BlockSpec((2, 2 * CMP_HIDDEN, LANE), fixed3)],
        out_specs=[pl.BlockSpec((1, n_chunks, LANE), lambda i: (i, 0, 0)),
                   pl.BlockSpec((1, n_chunks, LANE), lambda i: (i, 0, 0)),
                   pl.BlockSpec((1, LANE, n_chunks), lambda i: (i, 0, 0))],
        out_shape=[jax.ShapeDtypeStruct((b, n_chunks, LANE), F32), jax.ShapeDtypeStruct((b, n_chunks, LANE), F32),
                   jax.ShapeDtypeStruct((b, LANE, n_chunks), F32)],
        scratch_shapes=[pltpu.VMEM((2, N_KV_A * n_chunks, CMP_STRIDE * HEAD_DIM), F32)],
        compiler_params=_cp("parallel"),
        name="nsa_compress",
    )(rows, rows, w1cat, b1, w2bd, w1k, w2k)


def _cmp_weights(cmp_w1, cmp_b1, cmp_w2):
    m = CMP_BLOCK // CMP_STRIDE
    w1r = cmp_w1.reshape(2, m, CMP_STRIDE * HEAD_DIM, CMP_HIDDEN)
    w1cat = jnp.concatenate([w1r[:, j] for j in range(m)], axis=-1)
    z = jnp.zeros_like(cmp_w2)
    w2bd = jnp.concatenate([jnp.concatenate([cmp_w2, z], -1), jnp.concatenate([z, cmp_w2], -1)], axis=1)
    split = lambda w: jnp.stack([w.astype(BF16), (w - w.astype(BF16).astype(F32)).astype(BF16)])
    return w1cat.astype(BF16), cmp_b1.reshape(2, 1, CMP_HIDDEN), w2bd.astype(BF16), split(w1cat[0]), split(w2bd[0])


def _cmp_to_sel_t(n_chunks, n_cmp, n_sel):
    starts = np.arange(n_chunks) * CMP_STRIDE
    sel_s = np.arange(n_sel) * SEL_BLOCK
    ov = np.minimum(starts[:, None] + CMP_BLOCK, sel_s[None] + SEL_BLOCK) - np.maximum(starts[:, None], sel_s[None])
    ov = np.clip(ov, 0, None) / CMP_BLOCK
    ov[n_cmp:] = 0.0
    return jnp.asarray(ov.T, dtype=F32)


def _masked_softmax_cols(s, mask):
    m = jnp.max(jnp.where(mask, s, NEG_INF), axis=0, keepdims=True)
    e = jnp.where(mask, jnp.exp(s - m), 0.0)
    den = jnp.sum(e, axis=0, keepdims=True)
    return e * jnp.where(den > 0.0, 1.0 / den, 0.0)


def _nsa_prompt_body(qt_ref, gt_ref, kc_ref, vct_ref, ovt_ref, ks_ref, vst_ref, kw_ref, vwt_ref, o_ref, sel_ref,
                     *, n_cmp, n_sel, qb_size):
    qb = pl.program_id(1)
    tq = qb_size
    n_chunks = kc_ref.shape[1]
    qpos = qb * tq + lax.broadcasted_iota(jnp.int32, (1, tq), 1)
    qpos2 = jnp.concatenate([qpos, qpos], axis=1)
    zeros_q = jnp.zeros((HEAD_DIM, 2 * tq), F32)
    gates = jax.nn.sigmoid(gt_ref[0])
    kc = kc_ref[0]
    qpos4 = jnp.concatenate([qpos2, qpos2], axis=1)
    n_idx = lax.broadcasted_iota(jnp.int32, (n_chunks, 4 * tq), 0)
    cmp_mask = (n_idx * CMP_STRIDE + (CMP_BLOCK - 1) <= qpos4) & (n_idx < n_cmp)
    blk = lax.broadcasted_iota(jnp.int32, (n_sel, tq), 0)
    cur = qpos // SEL_BLOCK
    forced = (blk == 0) | (blk == cur) | (blk == cur - 1)
    causal_blk = blk * SEL_BLOCK <= qpos

    qpads32 = []
    for g in range(N_KV_A):
        q64 = jnp.concatenate([qt_ref[0, (2 * g) * HEAD_DIM:(2 * g + 1) * HEAD_DIM, :],
                               qt_ref[0, (2 * g + 1) * HEAD_DIM:(2 * g + 2) * HEAD_DIM, :]], axis=1) * (HEAD_DIM ** -0.5)
        qpads32.append(jnp.concatenate([q64, zeros_q], axis=0) if g == 0 else jnp.concatenate([zeros_q, q64], axis=0))
    q_all32 = jnp.concatenate(qpads32, axis=1)
    q_all = q_all32.astype(BF16)

    p_all = _masked_softmax_cols(_dot3(kc, q_all32), cmp_mask)
    o_cmp_all = jnp.dot(vct_ref[0].astype(BF16), p_all.astype(BF16), preferred_element_type=F32)
    o_cmps = [o_cmp_all[g * HEAD_DIM:(g + 1) * HEAD_DIM, g * 2 * tq:(g + 1) * 2 * tq] for g in range(N_KV_A)]

    for g in range(N_KV_A):
        psum = p_all[:, g * 2 * tq:g * 2 * tq + tq] + p_all[:, g * 2 * tq + tq:(g + 1) * 2 * tq]
        p1 = psum.astype(BF16)
        p2, p3 = _split_bf16(psum - p1.astype(F32))
        ovb = ovt_ref[...].astype(BF16)
        imp = (jnp.dot(ovb, p1, preferred_element_type=F32) + jnp.dot(ovb, p2, preferred_element_type=F32)
               + jnp.dot(ovb, p3, preferred_element_type=F32))
        imp = jnp.where(forced, FORCED_SCORE, jnp.where(causal_blk, imp, BLOCKED_SCORE))
        ngrp = n_sel // 8
        imp_g = [imp[8 * k:8 * k + 8, :] for k in range(ngrp)]
        rank_g = [jnp.zeros((8, tq), F32) for _ in range(ngrp)]
        for i in range(n_sel):
            row = imp[i:i + 1, :]
            for k in range(ngrp):
                if k > i // 8:
                    beats = row >= imp_g[k]
                elif k < i // 8:
                    beats = row > imp_g[k]
                else:
                    beats = (row > imp_g[k]) | ((row == imp_g[k]) & (blk[8 * k:8 * k + 8, :] > i))
                rank_g[k] = rank_g[k] + jnp.where(beats, 1.0, 0.0)
        rank = jnp.concatenate(rank_g, axis=0)
        sel_ref[g] = jnp.where(rank < float(min(TOP_K, n_sel)), 0.0, NEG_INF)

    tk_s = 4 * tq

    def slc(j, carry, causal=True):
        m, l, acc = carry
        off = pl.multiple_of(j * tk_s, tk_s)
        s = jnp.dot(ks_ref[0, pl.ds(off, tk_s), :].astype(BF16), q_all, preferred_element_type=F32)
        per_tile = tk_s // SEL_BLOCK
        biases = []
        for g in range(N_KV_A):
            rows = [jnp.broadcast_to(sel_ref[g, pl.ds(j * per_tile + a, 1), :], (SEL_BLOCK, tq)) for a in range(per_tile)]
            bias = jnp.concatenate(rows, axis=0)
            biases += [bias, bias]
        sm = s + jnp.concatenate(biases, axis=1)
        if causal:
            diff = qpos4 - (off + lax.broadcasted_iota(jnp.int32, (tk_s, 4 * tq), 0))
            sm = jnp.where(diff >= 0, sm, NEG_INF)
        m_new = jnp.maximum(m, jnp.max(sm, axis=0, keepdims=True))
        e = jnp.exp(sm - m_new)
        alpha = jnp.exp(m - m_new)
        l_new = alpha * l + jnp.sum(e, axis=0, keepdims=True)
        vt = vst_ref[0, :, pl.ds(off, tk_s)].astype(BF16)
        return m_new, l_new, alpha * acc + jnp.dot(vt, e.astype(BF16), preferred_element_type=F32)

    init = (jnp.full((1, 4 * tq), NEG_INF, F32), jnp.zeros((1, 4 * tq), F32), jnp.zeros((2 * HEAD_DIM, 4 * tq), F32))
    last_s = (qb * tq) // tk_s
    _, l_s, acc_s = slc(last_s, lax.fori_loop(0, last_s, functools.partial(slc, causal=False), init))
    t_keys = kw_ref.shape[1]
    tk_w = min(WINDOW + tq, t_keys)
    start = pl.multiple_of(jnp.clip(qb * tq - WINDOW, 0, t_keys - tk_w), tq)
    s_w = jnp.dot(kw_ref[0, pl.ds(start, tk_w), :].astype(BF16), q_all, preferred_element_type=F32)
    diff_w = qpos4 - (start + lax.broadcasted_iota(jnp.int32, (tk_w, 4 * tq), 0))
    mask_w = (diff_w >= 0) & (diff_w < WINDOW)
    m_w = jnp.max(jnp.where(mask_w, s_w, NEG_INF), axis=0, keepdims=True)
    e_w = jnp.where(mask_w, jnp.exp(s_w - m_w), 0.0)
    l_w = jnp.sum(e_w, axis=0, keepdims=True)
    acc_w = jnp.dot(vwt_ref[0, :, pl.ds(start, tk_w)].astype(BF16), e_w.astype(BF16), preferred_element_type=F32)

    for g in range(N_KV_A):
        blk_g = (slice(g * HEAD_DIM, (g + 1) * HEAD_DIM), slice(g * 2 * tq, (g + 1) * 2 * tq))
        o_slc = acc_s[blk_g] / l_s[:, blk_g[1]]
        o_win = acc_w[blk_g] / l_w[:, blk_g[1]]
        for r in range(R_A):
            h = 2 * g + r
            gr = gates[3 * h:3 * h + 3, :]
            sl = slice(r * tq, (r + 1) * tq)
            o_ref[0, h * HEAD_DIM:(h + 1) * HEAD_DIM, :] = (gr[0:1] * o_cmps[g][:, sl] + gr[1:2] * o_slc[:, sl]
                                                          + gr[2:3] * o_win[:, sl])


def _nsa_prompt(qt, gt, kc, vct, ovt, kv, vst, vwt, n_cmp):
    b, _, t = qt.shape
    tq = 128
    n_sel = t // SEL_BLOCK
    n_chunks = kc.shape[1]
    per_b = lambda i, j: (i, 0, 0)
    return pl.pallas_call(
        functools.partial(_nsa_prompt_body, n_cmp=n_cmp, n_sel=n_sel, qb_size=tq),
        grid=(b, t // tq),
        in_specs=[pl.BlockSpec((1, 256, tq), lambda i, j: (i, 0, j)),
                  pl.BlockSpec((1, 16, tq), lambda i, j: (i, 0, j)),
                  pl.BlockSpec((1, n_chunks, LANE), per_b),
                  pl.BlockSpec((1, LANE, n_chunks), per_b),
                  pl.BlockSpec((n_sel, n_chunks), lambda i, j: (0, 0)),
                  pl.BlockSpec((1, t, LANE), lambda i, j: (i, 0, 2)),
                  pl.BlockSpec((1, LANE, t), per_b),
                  pl.BlockSpec((1, t, LANE), lambda i, j: (i, 0, 4)),
                  pl.BlockSpec((1, LANE, t), per_b)],
        out_specs=pl.BlockSpec((1, 256, tq), lambda i, j: (i, 0, j)),
        out_shape=jax.ShapeDtypeStruct((b, 256, t), F32),
        scratch_shapes=[pltpu.VMEM((N_KV_A, n_sel, tq), F32)],
        compiler_params=_cp("parallel", "arbitrary", fuse_inputs=[True, True] + [False] * 7),
        name="nsa_prompt",
    )(qt, gt, kc, vct, ovt, kv, vst, kv, vwt)


def _nsa_prompt_mixer(q, kv, gate, lw, b, t):
    kv3 = kv.reshape(b, t, 6 * LANE)
    n_chunks = t // CMP_STRIDE
    n_cmp = (t - CMP_BLOCK) // CMP_STRIDE + 1
    kc, _, vct = _nsa_compress(kv3, *lw['cmp'])
    ovt = _cmp_to_sel_t(n_chunks, n_cmp, t // SEL_BLOCK)
    qt = jnp.swapaxes(q.reshape(b, t, 256), 1, 2)
    gt = jnp.swapaxes(gate.reshape(b, t, LANE)[:, :, :16], 1, 2)
    vst = jnp.swapaxes(kv3[:, :, 3 * LANE:4 * LANE], 1, 2)
    vwt = jnp.swapaxes(kv3[:, :, 5 * LANE:], 1, 2)
    ot = _nsa_prompt(qt, gt, kc, vct, ovt, kv3, vst, vwt, n_cmp)
    return jnp.swapaxes(ot, 1, 2).reshape(b * t, 256)


def _softmax_rows_with_extra(s, mask, s_new):
    m = jnp.maximum(jnp.max(jnp.where(mask, s, NEG_INF), axis=-1, keepdims=True), s_new)
    e = jnp.where(mask, jnp.exp(s - m), 0.0)
    e_new = jnp.exp(s_new - m)
    return e, e_new, 1.0 / (jnp.sum(e, axis=-1, keepdims=True) + e_new)


def _nsa_sample_body(pt_ref, *refs, n_pages, page, pos, n_sel, n_cmp, wb):
    del pt_ref
    n_in = n_pages
    pages = refs[:n_in]
    (qbd_ref, new_ref, gate_ref, win_ref, ov_ref, w1_ref, b1_ref, w2_ref, w1k_ref, w2k_ref,
     o_ref, nw_ref, tok_ref, xc_ref) = refs[n_in:]
    pg = lambda p, kind: pages[p].at[0, kind:kind + 1]
    nck = n_pages * page // CMP_STRIDE
    lane8 = lax.broadcasted_iota(jnp.int32, (8, LANE), 1)
    lo8 = lane8 < HEAD_DIM
    nt = lambda a, b: lax.dot_general(a, b, (((1,), (1,)), ((), ())), preferred_element_type=F32)

    lo_c = lax.broadcasted_iota(jnp.int32, (nck, LANE), 1) < HEAD_DIM
    for kind in range(2):
        for p in range(n_pages):
            tok_ref[kind, p * page:(p + 1) * page, :] = pg(p, kind)[0].T
        for pair in range(CMP_STRIDE // 2):
            a = tok_ref[kind, pl.ds(2 * pair, nck, stride=CMP_STRIDE), :]
            b = tok_ref[kind, pl.ds(2 * pair + 1, nck, stride=CMP_STRIDE), :]
            cols = slice(LANE * pair, LANE * (pair + 1))
            xc_ref[kind, 0:nck, cols] = jnp.where(lo_c, a, pltpu.roll(b, HEAD_DIM, 1))
            xc_ref[kind, nck:2 * nck, cols] = jnp.where(lo_c, pltpu.roll(a, HEAD_DIM, 1), b)
    kc, vc = [_cmp_mlp(xc_ref[kind], kind, w1_ref, b1_ref, w2_ref, w1k_ref, w2k_ref, nck) for kind in range(2)]

    q = qbd_ref[0] * (HEAD_DIM ** -0.5)
    qb = q.astype(BF16)
    new = new_ref[0]

    n_idx = lax.broadcasted_iota(jnp.int32, (8, nck), 1)
    cmask = (n_idx * CMP_STRIDE + (CMP_BLOCK - 1) <= pos) & (n_idx < n_cmp)
    s = _dot3(q, kc, _NT)
    m = jnp.max(jnp.where(cmask, s, NEG_INF), axis=-1, keepdims=True)
    e = jnp.where(cmask, jnp.exp(s - m), 0.0)
    den = jnp.sum(e, axis=-1, keepdims=True)
    p_cmp = e * jnp.where(den > 0.0, 1.0 / den, 0.0)
    o_cmp = jnp.dot(p_cmp.astype(BF16), vc.astype(BF16), preferred_element_type=F32)

    row8 = lax.broadcasted_iota(jnp.int32, (8, nck), 0)
    psum = jnp.where(row8 == 0, p_cmp[0:1] + p_cmp[1:2], jnp.where(row8 == 1, p_cmp[2:3] + p_cmp[3:4], 0.0))
    imp = jnp.dot(psum, ov_ref[...], precision=HI, preferred_element_type=F32)
    cur = pos // SEL_BLOCK
    forced = (lane8 == 0) | (lane8 == cur) | (lane8 == cur - 1)
    imp = jnp.where(forced, FORCED_SCORE, jnp.where(lane8 * SEL_BLOCK <= pos, imp, BLOCKED_SCORE))
    imp = jnp.where(lane8 < n_sel, imp, -3e38)
    rank = jnp.zeros((8, LANE), F32)
    for i in range(n_sel):
        col = imp[:, i:i + 1]
        rank = rank + jnp.where((col > imp) | ((col == imp) & (lane8 > i)), 1.0, 0.0)
    sel = jnp.where((rank < float(min(TOP_K, n_sel))) & (lane8 < n_sel), 1.0, 0.0)
    rsel = lax.broadcasted_iota(jnp.int32, (8, LANE), 0)
    selh = jnp.where(rsel < R_A, sel[0:1], jnp.where(rsel < 2 * R_A, sel[1:2], 0.0))

    per_page = page // SEL_BLOCK
    s_t, m_t = [], []
    for p in range(n_pages):
        s_t.append(jnp.dot(qb, pg(p, 2)[0].astype(BF16), preferred_element_type=F32))
        blk_sel = selh[:, per_page * p:per_page * p + 1]
        for a in range(1, per_page):
            blk_sel = jnp.where(lane8 < a * SEL_BLOCK, blk_sel, selh[:, per_page * p + a:per_page * p + a + 1])
        kpos = p * page + lane8
        m_t.append((blk_sel > 0.0) & (kpos <= pos))
    s_all = jnp.concatenate(s_t, axis=1)
    mk_all = jnp.concatenate(m_t, axis=1)
    s_new = jnp.sum(q * new[2:3], axis=-1, keepdims=True)
    e, e_new, inv = _softmax_rows_with_extra(s_all, mk_all, s_new)
    acc = e_new * new[3:4]
    for p in range(n_pages):
        acc = acc + nt(e[:, p * page:(p + 1) * page].astype(BF16), pg(p, 3)[0].astype(BF16))
    o_slc = acc * inv

    kw, vw = win_ref[0], win_ref[1]
    widx = lax.broadcasted_iota(jnp.int32, (8, wb), 1)
    diff = wb - widx
    s_w = jnp.dot(qb, kw.astype(BF16), preferred_element_type=F32)
    s_wnew = jnp.sum(q * new[4:5], axis=-1, keepdims=True)
    e, e_new, inv = _softmax_rows_with_extra(s_w, (diff >= 0) & (diff < WINDOW), s_wnew)
    o_win = (nt(e.astype(BF16), vw.astype(BF16)) + e_new * new[5:6]) * inv

    gts = jax.nn.sigmoid(gate_ref[0])
    o = gts[:, 0:1] * o_cmp + gts[:, 1:2] * o_slc + gts[:, 2:3] * o_win
    lo1 = lo8[0:1]
    o_ref[0] = jnp.concatenate([jnp.where(lo1, o[0:1], pltpu.roll(o[1:2], HEAD_DIM, 1)),
                                jnp.where(lo1, pltpu.roll(o[2:3], HEAD_DIM, 1), o[3:4])], axis=1)
    last = lax.broadcasted_iota(jnp.int32, (LANE, wb), 1) == wb - 1
    new_t = new.T
    nw_ref[0] = jnp.where(last, new_t[:, 4:5], pltpu.roll(kw, wb - 1, 1))
    nw_ref[1] = jnp.where(last, new_t[:, 5:6], pltpu.roll(vw, wb - 1, 1))


def _nsa_sample(q, kv, gate, cache_kv, cache_win, layer, page_table, cmp_w, pos):
    b = q.shape[0]
    n_pool, page = cache_kv.shape[1:3]
    n_pages = page_table.shape[1]
    wb = cache_win.shape[2]
    assert wb == WINDOW and page % CMP_STRIDE == 0 and page % SEL_BLOCK == 0 and page == LANE
    tk = n_pages * page + 1
    n_cmp = (tk - CMP_BLOCK) // CMP_STRIDE + 1
    nck = n_pages * page // CMP_STRIDE
    assert n_cmp <= nck
    n_sel = -(-tk // SEL_BLOCK)
    assert n_sel <= LANE and pos // SEL_BLOCK == n_sel - 1
    ov = _pad_to(_cmp_to_sel_t(nck, n_cmp, n_sel).T, LANE, 1)
    w1cat, b1, w2bd, w1k, w2k = cmp_w
    q4 = q.reshape(b, H_A, HEAD_DIM)
    z = jnp.zeros_like(q4)
    first = (jnp.arange(H_A) // R_A == 0)[None, :, None]
    qbd = jnp.concatenate([jnp.where(first, q4, z), jnp.where(first, z, q4)], axis=-1)
    qbd = _pad_to(qbd, 8, 1)
    new = _pad_to(kv.reshape(b, 6, LANE), 8, 1)
    g8 = _pad_to(_pad_to(gate[:, :3 * H_A].reshape(b, H_A, 3), LANE, 2), 8, 1)
    cache3 = jnp.transpose(cache_kv, (0, 1, 3, 4, 5, 2)).reshape(-1, 4, LANE, page)
    win3 = jnp.transpose(cache_win, (0, 1, 3, 4, 5, 2)).reshape(-1, LANE, wb)
    page_specs = [pl.BlockSpec((1, 4, LANE, page), functools.partial(lambda i, pt, p: (layer * n_pool + pt[i, p], 0, 0, 0), p=p))
                  for p in range(n_pages)]
    per_b = lambda i, pt: (i, 0, 0)
    fixed2 = lambda i, pt: (0, 0)
    fixed3 = lambda i, pt: (0, 0, 0)
    grid_spec = pltpu.PrefetchScalarGridSpec(
        num_scalar_prefetch=1,
        grid=(b,),
        in_specs=page_specs + [pl.BlockSpec((1, 8, LANE), per_b), pl.BlockSpec((1, 8, LANE), per_b), pl.BlockSpec((1, 8, LANE), per_b),
                               pl.BlockSpec((2, LANE, wb), lambda i, pt: (layer * b + i, 0, 0)), pl.BlockSpec((nck, LANE), fixed2),
                               pl.BlockSpec((2, CMP_STRIDE * HEAD_DIM, 2 * CMP_HIDDEN), fixed3),
                               pl.BlockSpec((2, 1, CMP_HIDDEN), fixed3), pl.BlockSpec((2, 2 * CMP_HIDDEN, LANE), fixed3),
                               pl.BlockSpec((2, CMP_STRIDE * HEAD_DIM, 2 * CMP_HIDDEN), fixed3),
                               pl.BlockSpec((2, 2 * CMP_HIDDEN, LANE), fixed3)],
        out_specs=[pl.BlockSpec((1, 1, 256), per_b), pl.BlockSpec((2, LANE, wb), per_b)],
        scratch_shapes=[pltpu.VMEM((2, n_pages * page, LANE), F32), pltpu.VMEM((2, 2 * nck, CMP_STRIDE * HEAD_DIM), F32)],
    )
    o, nw = pl.pallas_call(
        functools.partial(_nsa_sample_body, n_pages=n_pages, page=page, pos=pos, n_sel=n_sel, n_cmp=n_cmp, wb=wb),
        grid_spec=grid_spec,
        out_shape=[jax.ShapeDtypeStruct((b, 1, 256), F32), jax.ShapeDtypeStruct((2 * b, LANE, wb), F32)],
        compiler_params=_cp("parallel"),
        name="nsa_sample",
    )(page_table, *([cache3] * n_pages), qbd, new, g8, win3, ov, w1cat, b1, w2bd, w1k, w2k)
    return o.reshape(b, 256), jnp.transpose(nw.reshape(b, 2, N_KV_A, HEAD_DIM, wb), (0, 4, 1, 2, 3))


def _rwkv_prep_body(c_ref, s0_ref, mu_ref, vec_ref, wup_ref, aup_ref, gup_ref,
                    r_ref, lw_ref, k_ref, v_ref, kk_ref, ka_ref, g_ref, bonus_ref, carry_ref, *, tiles_per_seq):
    i = pl.program_id(0)
    cols = c_ref[...]
    tm = cols.shape[0]

    @pl.when(i % tiles_per_seq == 0)
    def _():
        carry_ref[...] = s0_ref[0]

    prev = pltpu.roll(cols, 1, 0)
    row0 = lax.broadcasted_iota(jnp.int32, cols.shape, 0) == 0
    prev = jnp.where(row0, carry_ref[...], prev)
    carry_ref[...] = cols[tm - 1:tm, :]
    _rwkv_features(cols, prev, mu_ref, vec_ref, wup_ref, aup_ref, gup_ref,
                   r_ref, lw_ref, k_ref, v_ref, kk_ref, ka_ref, g_ref, bonus_ref)


def _rwkv_step_prep_body(c_ref, s0_ref, mu_ref, vec_ref, wup_ref, aup_ref, gup_ref,
                         r_ref, lw_ref, k_ref, v_ref, kk_ref, ka_ref, g_ref, bonus_ref):
    _rwkv_features(c_ref[...], s0_ref[...], mu_ref, vec_ref, wup_ref, aup_ref, gup_ref,
                   r_ref, lw_ref, k_ref, v_ref, kk_ref, ka_ref, g_ref, bonus_ref)


def _rwkv_features(cols, prev, mu_ref, vec_ref, wup_ref, aup_ref, gup_ref,
                   r_ref, lw_ref, k_ref, v_ref, kk_ref, ka_ref, g_ref, bonus_ref):
    xs = cols + mu_ref[...] * (prev - cols)
    r, k, v, lo = xs[:, 0:256], xs[:, 256:512], xs[:, 512:768], xs[:, 768:896]
    vec = vec_ref[...]
    w0, a0, k_k, k_a, r_k = vec[0:1], vec[1:2], vec[2:3], vec[3:4], vec[4:5]
    w_log = -jax.nn.softplus(-(w0 + _dot3(jnp.tanh(lo), wup_ref[...]))) - 0.5
    a = jax.nn.sigmoid(a0 + _dot3(lo, aup_ref[...]))
    g_ref[...] = _dot3(jax.nn.sigmoid(lo), gup_ref[...])
    ones = _block_ones(256, HEAD_DIM, BF16)
    kk = k * k_k
    kk = kk * lax.rsqrt(_dot_bf16_exact_rhs(kk * kk, ones) + 1e-12)
    k2 = k * (1.0 + (a - 1.0) * k_a)
    r_ref[...] = r
    lw_ref[...] = -jnp.exp(w_log)
    k_ref[...] = k2
    v_ref[...] = v
    kk_ref[...] = kk
    ka_ref[...] = kk * a
    bonus_ref[...] = _dot_bf16_exact_rhs(r * k2 * r_k, ones) * v


def _rwkv_prep(colsb, shift0, lw, t, tm):
    n = colsb.shape[0]
    tiles_per_seq = t // tm
    row = lambda i: (i, 0)
    fixed = lambda i: (0, 0)
    outs = [jax.ShapeDtypeStruct((n, 256), F32)] * 8
    return pl.pallas_call(
        functools.partial(_rwkv_prep_body, tiles_per_seq=tiles_per_seq),
        grid=(n // tm,),
        in_specs=[pl.BlockSpec((tm, SHIFT_PAD), row),
                  pl.BlockSpec((1, 1, SHIFT_PAD), lambda i: (i // tiles_per_seq, 0, 0)),
                  pl.BlockSpec((1, SHIFT_PAD), fixed), pl.BlockSpec((8, 256), fixed),
                  pl.BlockSpec((LANE, 256), fixed), pl.BlockSpec((LANE, 256), fixed), pl.BlockSpec((LANE, 256), fixed)],
        out_specs=[pl.BlockSpec((tm, 256), row)] * 8,
        out_shape=outs,
        scratch_shapes=[pltpu.VMEM((1, SHIFT_PAD), F32)],
        compiler_params=_cp("arbitrary"),
        name="rwkv_prep",
    )(colsb, shift0.reshape(-1, 1, SHIFT_PAD), lw['rwkv_mu'], lw['rwkv_vec'], lw['rwkv_wup'], lw['rwkv_aup'], lw['rwkv_gup'])


RWKV_CHUNK = 64


def _split_bf16(x):
    hi = x.astype(BF16)
    return hi, (x - hi.astype(F32)).astype(BF16)


def _dot3(a, b, dims=(((1,), (0,)), ((), ()))):
    ah, al = _split_bf16(a)
    bh, bl = _split_bf16(b)
    dg = lambda x, y: lax.dot_general(x, y, dims, preferred_element_type=F32)
    return dg(ah, bh) + dg(ah, bl) + dg(al, bh)


def _dot3_presplit(x, w_ref):
    xh, xl = _split_bf16(x)
    return (jnp.dot(xh, w_ref[0], preferred_element_type=F32) + jnp.dot(xh, w_ref[1], preferred_element_type=F32)
            + jnp.dot(xl, w_ref[0], preferred_element_type=F32))


def _dot_bf16_exact_rhs(x, m_bf16):
    hi, lo = _split_bf16(x)
    return jnp.dot(hi, m_bf16, preferred_element_type=F32) + jnp.dot(lo, m_bf16, preferred_element_type=F32)


_NT = (((1,), (1,)), ((), ()))


def _rwkv_chunk_body(r_ref, lw_ref, k_ref, v_ref, kk_ref, ka_ref, g_ref, bonus_ref, s0_ref, vec_ref,
                     o_ref, st_ref, s_scr, y_scr, *, nb, tl):
    L = RWKV_CHUNK
    nc = tl // L

    @pl.when(pl.program_id(1) == 0)
    def _():
        s_scr[...] = s0_ref[...]

    ri = lax.broadcasted_iota(jnp.int32, (L, L), 0)
    ci = lax.broadcasted_iota(jnp.int32, (L, L), 1)
    strict, incl = ri > ci, ri >= ci
    ltri = incl.astype(F32)
    eye = (ri == ci).astype(F32)

    bnn = (((2,), (1,)), ((0,), (0,)))
    bnt = (((2,), (2,)), ((0,), (0,)))

    def chunk(c, carry):
        rows = pl.ds(pl.multiple_of(c * L, L), L)
        lhs_l, rhs_l, v_l, kw_l, wl_l = [], [], [], [], []
        for b in range(nb):
            r, lw, k, v, kk, ka = [ref[b, rows, :] for ref in (r_ref, lw_ref, k_ref, v_ref, kk_ref, ka_ref)]
            cl = jnp.dot(ltri, lw, precision=HI, preferred_element_type=F32)
            e_neg = jnp.exp(-cl)
            e_rem = jnp.exp(cl[L - 1:L, :] - cl)
            kkd, rd = kk * jnp.exp(cl - lw), r * jnp.exp(cl)
            kinv, kainv, kw, kaw = k * e_neg, ka * e_neg, k * e_rem, ka * e_rem
            w_last = jnp.exp(cl[L - 1:L, :])
            for h in range(H_B):
                sl = slice(h * HEAD_DIM, (h + 1) * HEAD_DIM)
                lhs_l.append(jnp.concatenate([kkd[:, sl], rd[:, sl]], axis=0))
                rhs_l.append(jnp.concatenate([kinv[:, sl], kainv[:, sl]], axis=0))
                v_l.append(v[:, sl])
                kw_l.append(jnp.concatenate([kw[:, sl], kaw[:, sl]], axis=0))
                wl_l.append(w_last[:, sl])
        lhs, rhs, vs, kws, wl = [jnp.stack(x) for x in (lhs_l, rhs_l, v_l, kw_l, wl_l)]
        gm = _dot3(lhs, rhs, bnt)
        a_vk = jnp.where(strict, gm[:, :L, :L], 0.0)
        n1 = jnp.where(strict, -gm[:, :L, L:], 0.0)
        t_inv, pw = eye + n1, n1
        for _ in range(L.bit_length() - 2):
            pw = _dot3(pw, pw, bnn)
            t_inv = _dot3(t_inv, eye + pw, bnn)
        s = s_scr[...].reshape(nb * H_B, HEAD_DIM, HEAD_DIM)
        xs = _dot3(lhs, s, bnt)
        u = _dot3(t_inv, xs[:, :L] + _dot3(a_vk, vs, bnn), bnn)
        b_vk = jnp.where(incl, gm[:, L:, :L], 0.0).astype(BF16)
        b_uk = jnp.where(incl, gm[:, L:, L:], 0.0).astype(BF16)
        y = (xs[:, L:] + lax.dot_general(b_vk, vs.astype(BF16), bnn, preferred_element_type=F32)
             - lax.dot_general(b_uk, u.astype(BF16), bnn, preferred_element_type=F32))
        vu_t = jnp.stack([jnp.concatenate([vs[n], -u[n]], axis=0).T for n in range(nb * H_B)])
        s_new = s * wl + _dot3(vu_t, kws, bnn)
        s_scr[...] = s_new.reshape(nb, H_B, HEAD_DIM, HEAD_DIM)
        for b in range(nb):
            for h in range(H_B):
                y_scr[b, rows, h * HEAD_DIM:(h + 1) * HEAD_DIM] = y[b * H_B + h]
        return carry

    lax.fori_loop(0, nc, chunk, 0)
    st_ref[...] = s_scr[...]
    vec = vec_ref[...]
    for b in range(nb):
        o_ref[b] = (_segment_norm(y_scr[b], RWKV_GN_EPS) * vec[5:6] + vec[6:7] + bonus_ref[b]) * g_ref[b]


def _rwkv_chunked(prep, s0, vec, b, t, nb, tl):
    arrs = [a.reshape(b, t, 256) for a in prep]
    seq = pl.BlockSpec((nb, tl, 256), lambda i, j: (i, j, 0))
    st = pl.BlockSpec((nb, H_B, HEAD_DIM, HEAD_DIM), lambda i, j: (i, 0, 0, 0))
    return pl.pallas_call(
        functools.partial(_rwkv_chunk_body, nb=nb, tl=tl),
        grid=(b // nb, t // tl),
        in_specs=[seq] * 8 + [st, pl.BlockSpec((8, 256), lambda i, j: (0, 0))],
        out_specs=[seq, st],
        out_shape=[jax.ShapeDtypeStruct((b, t, 256), F32), jax.ShapeDtypeStruct((b, H_B, HEAD_DIM, HEAD_DIM), F32)],
        scratch_shapes=[pltpu.VMEM((nb, H_B, HEAD_DIM, HEAD_DIM), F32), pltpu.VMEM((nb, tl, 256), F32)],
        compiler_params=_cp("parallel", "arbitrary"),
        name="rwkv_chunked",
    )(*arrs, s0, vec)


def _rwkv_mixer(colsb, shift0, s0, lw, b, t, tm, nb, tl):
    prep = _rwkv_prep(colsb, shift0, lw, t, tm)
    o, st = _rwkv_chunked(prep, s0, lw['rwkv_vec'], b, t, nb, tl)
    shift = colsb.reshape(b, t, SHIFT_PAD)[:, -1, :SHIFT_B]
    return o.reshape(b * t, 256), st, shift


def _segment_norm(y, eps):
    avg = _block_ones(256, HEAD_DIM, BF16) * (1.0 / HEAD_DIM)
    yc = y - _dot_bf16_exact_rhs(y, avg)
    return yc * lax.rsqrt(_dot_bf16_exact_rhs(yc * yc, avg) + eps)


def _rwkv_step_body(r_ref, lw_ref, k_ref, v_ref, kk_ref, ka_ref, g_ref, bonus_ref, vec_ref, s_ref, o_ref, st_ref, ft_scr, y_scr):
    h = pl.program_id(0)

    @pl.when(h == 0)
    def _():
        for n, ref in enumerate((r_ref, lw_ref, k_ref, v_ref, kk_ref, ka_ref)):
            ft_scr[n] = ref[...].T

    base = pl.multiple_of(h * HEAD_DIM, HEAD_DIM)
    head = lambda n: ft_scr[n, pl.ds(base, HEAD_DIM), :]
    r_t, w_t, k_t, kk_t, ka_t = head(0), jnp.exp(head(1)), head(2), head(4), head(5)

    def body(i, carry):
        rows = pl.ds(pl.multiple_of(i * HEAD_DIM, HEAD_DIM), HEAD_DIM)
        s = s_ref[rows, :]
        sk = jnp.sum(s * kk_t, axis=0, keepdims=True)
        s = s * w_t - sk * ka_t + ft_scr[3, pl.ds(base + i, 1), :] * k_t
        st_ref[rows, :] = s
        y_scr[pl.ds(base + i, 1), :] = jnp.sum(s * r_t, axis=0, keepdims=True)
        return carry

    lax.fori_loop(0, HEAD_DIM, body, 0)

    @pl.when(h == H_B - 1)
    def _():
        vec = vec_ref[...]
        o_ref[...] = (_segment_norm(y_scr[...].T, RWKV_GN_EPS) * vec[5:6] + vec[6:7] + bonus_ref[...]) * g_ref[...]


def _rwkv_step(colsb, shift0, s_all, layer, lw):
    b = colsb.shape[0]
    hd2 = HEAD_DIM * HEAD_DIM
    full = lambda *_: (0, 0)
    feat = pl.BlockSpec((b, 256), full)
    prep = pl.pallas_call(
        _rwkv_step_prep_body,
        grid=(1,),
        in_specs=[pl.BlockSpec((b, SHIFT_PAD), full), pl.BlockSpec((b, SHIFT_PAD), full), pl.BlockSpec((1, SHIFT_PAD), full),
                  pl.BlockSpec((8, 256), full)] + [pl.BlockSpec((LANE, 256), full)] * 3,
        out_specs=[feat] * 8,
        out_shape=[jax.ShapeDtypeStruct((b, 256), F32)] * 8,
        compiler_params=_cp("arbitrary"),
        name="rwkv_step_prep",
    )(colsb, shift0, lw['rwkv_mu'], lw['rwkv_vec'], lw['rwkv_wup'], lw['rwkv_aup'], lw['rwkv_gup'])
    s_rows = jnp.transpose(s_all, (0, 2, 3, 4, 1)).reshape(-1, b)
    o, st = pl.pallas_call(
        _rwkv_step_body,
        grid=(H_B,),
        in_specs=[feat] * 8 + [pl.BlockSpec((8, 256), full), pl.BlockSpec((hd2, b), lambda h: (layer * H_B + h, 0))],
        out_specs=[feat, pl.BlockSpec((hd2, b), lambda h: (h, 0))],
        out_shape=[jax.ShapeDtypeStruct((b, 256), F32), jax.ShapeDtypeStruct((H_B * hd2, b), F32)],
        scratch_shapes=[pltpu.VMEM((6, 256, b), F32), pltpu.VMEM((256, b), F32)],
        compiler_params=_cp("arbitrary"),
        name="rwkv_step",
    )(*prep, lw['rwkv_vec'], s_rows)
    return o, jnp.transpose(st.reshape(H_B, HEAD_DIM, HEAD_DIM, b), (3, 0, 1, 2))


def _ret_step_body(c_ref, cos_ref, sin_ref, gn_ref, r0_ref, o_ref, rt_ref, ft_scr, acc_scr):
    h = pl.program_id(0)

    @pl.when(h == 0)
    def _():
        x = c_ref[...]
        cs, sn = cos_ref[...], sin_ref[...]
        first = (lax.broadcasted_iota(jnp.int32, cs.shape, 1) % HEAD_DIM) < (HEAD_DIM // 2)

        def rope(z):
            sw = jnp.where(first, pltpu.roll(z, 256 - HEAD_DIM // 2, 1), pltpu.roll(z, HEAD_DIM // 2, 1))
            return z * cs + sw * sn

        ft_scr[0] = rope(x[:, 0:256]).T
        ft_scr[1] = (rope(x[:, 256:512]) * (HEAD_DIM ** -0.5)).T
        ft_scr[2] = x[:, 512:768].T

    base = pl.multiple_of(h * HEAD_DIM, HEAD_DIM)
    nseq = rt_ref.shape[1]
    gamma = jnp.exp(jnp.log1p(-jnp.exp2(-5.0 - jnp.full((1, nseq), h, jnp.int32).astype(F32))))
    q_t, k_t, v_t = [ft_scr[n, pl.ds(base, HEAD_DIM), :] for n in range(3)]
    qk = jnp.sum(q_t * k_t, axis=0, keepdims=True)

    def body(d, cross):
        rows = pl.ds(pl.multiple_of(d * HEAD_DIM, HEAD_DIM), HEAD_DIM)
        r_old = r0_ref[rows, :]
        rt_ref[rows, :] = gamma * r_old + ft_scr[1, pl.ds(base + d, 1), :] * v_t
        return cross + ft_scr[0, pl.ds(base + d, 1), :] * r_old

    cross = lax.fori_loop(0, HEAD_DIM, body, jnp.zeros((HEAD_DIM, nseq), F32))
    acc_scr[pl.ds(base, HEAD_DIM), :] = qk * v_t + gamma * cross

    @pl.when(h == H_D - 1)
    def _():
        o_ref[...] = jax.nn.silu(c_ref[:, 768:1024]) * (_segment_norm(acc_scr[...].T, RET_GN_EPS) * gn_ref[...])


def _ret_step(colsd, cos, sin, r_all, layer, gn):
    b = colsd.shape[0]
    hd2 = HEAD_DIM * HEAD_DIM
    full = lambda *_: (0, 0)
    r_rows = jnp.transpose(r_all, (0, 2, 3, 4, 1)).reshape(-1, b)
    o, rt = pl.pallas_call(
        _ret_step_body,
        grid=(H_D,),
        in_specs=[pl.BlockSpec((b, 1024), full), pl.BlockSpec((b, 256), full), pl.BlockSpec((b, 256), full),
                  pl.BlockSpec((1, 256), full), pl.BlockSpec((hd2, b), lambda h: (layer * H_D + h, 0))],
        out_specs=[pl.BlockSpec((b, 256), full), pl.BlockSpec((hd2, b), lambda h: (h, 0))],
        out_shape=[jax.ShapeDtypeStruct((b, 256), F32), jax.ShapeDtypeStruct((H_D * hd2, b), F32)],
        scratch_shapes=[pltpu.VMEM((3, 256, b), F32), pltpu.VMEM((256, b), F32)],
        compiler_params=_cp("arbitrary"),
        name="ret_step",
    )(colsd, cos, sin, gn.reshape(1, 256), r_rows)
    return o, jnp.transpose(rt.reshape(H_D, HEAD_DIM, HEAD_DIM, b), (3, 0, 1, 2))


def _s5_step_body(u_ref, x0_ref, a1_ref, a2_ref, bt_ref, ct_ref, d_ref, wg_ref, o_ref, xt_ref):
    u = u_ref[...]
    dot_hi = lambda a, b: jnp.dot(a, b, precision=HI, preferred_element_type=F32)
    x0 = x0_ref[...]
    n = x0.shape[0]
    even = (lax.broadcasted_iota(jnp.int32, x0.shape, 0) % 2) == 0
    partner = jnp.where(even, pltpu.roll(x0, n - 1, 0), pltpu.roll(x0, 1, 0))
    x = a1_ref[...] * x0 + a2_ref[...] * partner + dot_hi(bt_ref[...], u.T)
    xt_ref[...] = x
    y = dot_hi(ct_ref[...], x).T + d_ref[...] * u
    z = jax.nn.gelu(y)
    o_ref[...] = z * jax.nn.sigmoid(dot_hi(z, wg_ref[...]))


def _s5_step(u, x_all, layer, lw):
    b = u.shape[0]
    a1, a2, bt, ct, d_row = lw['s5_step']
    n = 2 * G_C * S5_P
    full = lambda *_: (0, 0)
    x_rows = jnp.transpose(x_all, (0, 2, 3, 4, 1)).reshape(-1, b)
    o, xt = pl.pallas_call(
        _s5_step_body,
        grid=(1,),
        in_specs=[pl.BlockSpec((b, C_C), full), pl.BlockSpec((n, b), lambda i: (layer, 0)), pl.BlockSpec((n, 1), full),
                  pl.BlockSpec((n, 1), full), pl.BlockSpec((n, C_C), full), pl.BlockSpec((C_C, n), full),
                  pl.BlockSpec((1, C_C), full), pl.BlockSpec((C_C, C_C), full)],
        out_specs=[pl.BlockSpec((b, C_C), full), pl.BlockSpec((n, b), full)],
        out_shape=[jax.ShapeDtypeStruct((b, C_C), F32), jax.ShapeDtypeStruct((n, b), F32)],
        compiler_params=_cp("arbitrary"),
        name="s5_step",
    )(u, x_rows, a1, a2, bt, ct, d_row, lw['s5_w_glu'])
    return o, jnp.transpose(xt.reshape(G_C, S5_P, 2, b), (3, 0, 1, 2))


def _s5_params(lw):
    lr, li = lw['s5_lambda_re'], lw['s5_lambda_im']
    dt = jnp.exp(lw['s5_log_step'])[:, None]
    mag = jnp.exp(lr * dt)
    ar, ai = mag * jnp.cos(li * dt), mag * jnp.sin(li * dt)
    nr, ni = ar - 1.0, ai
    den = lr * lr + li * li
    fr, fi = (nr * lr + ni * li) / den, (ni * lr - nr * li) / den
    b_re, b_im = lw['s5_b'][0], lw['s5_b'][1]
    bbr = fr[..., None] * b_re - fi[..., None] * b_im
    bbi = fr[..., None] * b_im + fi[..., None] * b_re
    eye = jnp.eye(G_C, dtype=F32)
    bd_in = lambda m: jnp.einsum('gpc,gh->gchp', m, eye).reshape(G_C * S5_CH, G_C * S5_P)
    bd_out = lambda m: jnp.einsum('gcp,gh->gphc', m, eye).reshape(G_C * S5_P, G_C * S5_CH)
    b_big = jnp.concatenate([bd_in(bbr), bd_in(bbi)], axis=1)
    c_big = jnp.concatenate([bd_out(lw['s5_c'][0]), -bd_out(lw['s5_c'][1])], axis=0)
    a_row = jnp.concatenate([ar.reshape(1, -1), ai.reshape(1, -1)], axis=1)
    d_row = lw['s5_d'].reshape(1, C_C)
    n = 2 * G_C * S5_P
    a1 = jnp.stack([ar, ar], axis=-1).reshape(n, 1)
    a2 = jnp.stack([-ai, ai], axis=-1).reshape(n, 1)
    bt = jnp.stack([jnp.einsum('gpc,gh->gphc', bbr, eye), jnp.einsum('gpc,gh->gphc', bbi, eye)], axis=2).reshape(n, G_C * S5_CH)
    ct = jnp.stack([jnp.einsum('gcp,gh->gchp', lw['s5_c'][0], eye), -jnp.einsum('gcp,gh->gchp', lw['s5_c'][1], eye)],
                   axis=-1).reshape(G_C * S5_CH, n)
    return (a_row, b_big, c_big, d_row), (a1, a2, bt, ct, d_row)


def _s5_body(u_ref, x0_ref, a_ref, b_ref, c_ref, d_ref, wg_ref, o_ref, xt_ref, x_scr, bu_scr, xs_scr, *, nb, tt, mm_dtype, prec):
    sub = 8

    @pl.when(pl.program_id(0) == 0)
    def _():
        x_scr[...] = jnp.zeros_like(x_scr)
        x_scr[0:nb, :] = x0_ref[...]
        bu_scr[...] = jnp.zeros_like(bu_scr)

    np_ = G_C * S5_P
    ncb = np_ // LANE
    for b in range(nb):
        bu = jnp.dot(u_ref[b].astype(mm_dtype), b_ref[...], precision=prec, preferred_element_type=F32)
        for cb in range(2 * ncb):
            bu_scr[cb, pl.ds(b, tt, stride=sub), :] = bu[:, cb * LANE:(cb + 1) * LANE]
    a = a_ref[...]

    def step(t, x):
        rows = pl.ds(pl.multiple_of(t * sub, sub), sub)
        new = [None] * (2 * ncb)
        for cb in range(ncb):
            re, im = slice(cb * LANE, (cb + 1) * LANE), slice(np_ + cb * LANE, np_ + (cb + 1) * LANE)
            ar, ai, xr, xi = a[:, re], a[:, im], x[:, re], x[:, im]
            new[cb] = ar * xr - ai * xi + bu_scr[cb, rows, :]
            new[ncb + cb] = ar * xi + ai * xr + bu_scr[ncb + cb, rows, :]
            xs_scr[cb, rows, :] = new[cb]
            xs_scr[ncb + cb, rows, :] = new[ncb + cb]
        return jnp.concatenate(new, axis=1)

    x_last = lax.fori_loop(0, tt, step, x_scr[...], unroll=8)
    x_scr[...] = x_last
    xt_ref[...] = x_last[0:nb, :]
    for b in range(nb):
        u = u_ref[b]
        xs = jnp.concatenate([xs_scr[cb, pl.ds(b, tt, stride=sub), :] for cb in range(2 * ncb)], axis=1)
        y = jnp.dot(xs.astype(mm_dtype), c_ref[...], precision=prec, preferred_element_type=F32) + d_ref[...] * u
        z = jax.nn.gelu(y)
        o_ref[b] = z * jax.nn.sigmoid(jnp.dot(z.astype(mm_dtype), wg_ref[...], precision=prec, preferred_element_type=F32))


def _s5_mixer(u, x0, lw, b, t, tt, exact):
    a_row, b_big, c_big, d_row = lw['s5']
    mm_dtype = F32 if exact else BF16
    prec = HI if exact else None
    x0l = jnp.concatenate([x0[..., 0].reshape(b, -1), x0[..., 1].reshape(b, -1)], axis=1)
    np2 = 2 * G_C * S5_P
    fixed = lambda i: (0, 0)
    o, xt = pl.pallas_call(
        functools.partial(_s5_body, nb=b, tt=tt, mm_dtype=mm_dtype, prec=prec),
        grid=(t // tt,),
        in_specs=[pl.BlockSpec((b, tt, C_C), lambda i: (0, i, 0)), pl.BlockSpec((b, np2), fixed),
                  pl.BlockSpec((1, np2), fixed), pl.BlockSpec((C_C, np2), fixed), pl.BlockSpec((np2, C_C), fixed),
                  pl.BlockSpec((1, C_C), fixed), pl.BlockSpec((C_C, C_C), fixed)],
        out_specs=[pl.BlockSpec((b, tt, C_C), lambda i: (0, i, 0)), pl.BlockSpec((b, np2), fixed)],
        out_shape=[jax.ShapeDtypeStruct((b, t, C_C), F32), jax.ShapeDtypeStruct((b, np2), F32)],
        scratch_shapes=[pltpu.VMEM((8, np2), F32), pltpu.VMEM((np2 // LANE, 8 * tt, LANE), F32),
                        pltpu.VMEM((np2 // LANE, 8 * tt, LANE), F32)],
        compiler_params=_cp("arbitrary"),
        name="s5",
    )(u.reshape(b, t, C_C), x0l, a_row, b_big.astype(mm_dtype), c_big.astype(mm_dtype), d_row, lw['s5_w_glu'].astype(mm_dtype))
    xt = xt.reshape(b, 2, G_C, S5_P)
    return o.reshape(b * t, C_C), jnp.stack([xt[:, 0], xt[:, 1]], axis=-1)


def _ret_tables(pos, c):
    cos, sin = _rope_tables(pos, HEAD_DIM, RET_THETA, HEAD_DIM, H_D)
    log_g = jnp.log1p(-jnp.exp2(-5.0 - jnp.arange(H_D, dtype=F32)))
    i = jnp.arange(c, dtype=F32)
    diff = i[:, None] - i[None, :]
    dmat = jnp.where(diff >= 0, jnp.exp(jnp.maximum(diff, 0.0)[None] * log_g[:, None, None]), 0.0).reshape(H_D * c, c)
    q_dec = jnp.repeat(jnp.exp((i + 1.0)[None] * log_g[:, None]).T, HEAD_DIM, axis=1)
    k_dec = jnp.repeat(jnp.exp((c - 1.0 - i)[None] * log_g[:, None]).T, HEAD_DIM, axis=1)
    chunk_dec = jnp.repeat(jnp.exp(c * log_g), HEAD_DIM).reshape(256, 1)
    return cos, sin, dmat, q_dec, k_dec, chunk_dec


def _ret_body(c_ref, cos_ref, sin_ref, dmat_ref, qdec_ref, kdec_ref, cdec_ref, r0_ref, gn_ref, o_ref, rt_ref, r_scr, *, c):
    @pl.when(pl.program_id(1) == 0)
    def _():
        r_scr[...] = r0_ref[0]

    x = c_ref[0]
    q, k, v, g = x[:, 0:256], x[:, 256:512], x[:, 512:768], x[:, 768:1024]
    cs, sn = cos_ref[...], sin_ref[...]
    lane = lax.broadcasted_iota(jnp.int32, (c, 256), 1)
    first = (lane % HEAD_DIM) < (HEAD_DIM // 2)

    def rope(z):
        sw = jnp.where(first, pltpu.roll(z, 256 - HEAD_DIM // 2, 1), pltpu.roll(z, HEAD_DIM // 2, 1))
        return z * cs + sw * sn

    q = rope(q)
    k = rope(k) * (HEAD_DIM ** -0.5)
    head = lane // HEAD_DIM
    kb, vb = k.astype(BF16), v.astype(BF16)
    qstack = jnp.concatenate([jnp.where(head == h, q, 0.0) for h in range(H_D)], axis=0).astype(BF16)
    s = lax.dot_general(qstack, kb, (((1,), (1,)), ((), ())), preferred_element_type=F32) * dmat_ref[...]
    pv = jnp.dot(s.astype(BF16), vb, preferred_element_type=F32)
    inner = jnp.zeros((c, 256), F32)
    for h in range(H_D):
        inner = inner + jnp.where(head == h, pv[h * c:(h + 1) * c], 0.0)
    r_old = r_scr[...]
    cross = jnp.dot((q * qdec_ref[...]).astype(BF16), r_old.astype(BF16), preferred_element_type=F32)
    kv = lax.dot_general((k * kdec_ref[...]).astype(BF16), vb, (((0,), (0,)), ((), ())), preferred_element_type=F32)
    bd = _block_ones(256, HEAD_DIM, F32)
    r_new = cdec_ref[...] * r_old + kv * bd
    r_scr[...] = r_new
    rt_ref[0] = r_new
    o_ref[0] = jax.nn.silu(g) * (_segment_norm(inner + cross, RET_GN_EPS) * gn_ref[...])


def _ret_mixer(colsd, r0, lw, tabs, b, t):
    c = RET_CHUNK if t % RET_CHUNK == 0 else t
    cos, sin, dmat, q_dec, k_dec, chunk_dec = tabs
    eye = jnp.eye(H_D, dtype=F32)
    r0l = jnp.einsum('bhde,hg->bhdge', r0, eye).reshape(b, 256, 256)
    n_t = t // c
    fixed = lambda i, j: (0, 0)
    o, rt = pl.pallas_call(
        functools.partial(_ret_body, c=c),
        grid=(b, n_t),
        in_specs=[pl.BlockSpec((1, c, 1024), lambda i, j: (i, j, 0)),
                  pl.BlockSpec((c, 256), lambda i, j: (j, 0)), pl.BlockSpec((c, 256), lambda i, j: (j, 0)),
                  pl.BlockSpec((H_D * c, c), fixed), pl.BlockSpec((c, 256), fixed), pl.BlockSpec((c, 256), fixed),
                  pl.BlockSpec((256, 1), fixed), pl.BlockSpec((1, 256, 256), lambda i, j: (i, 0, 0)),
                  pl.BlockSpec((1, 256), fixed)],
        out_specs=[pl.BlockSpec((1, c, 256), lambda i, j: (i, j, 0)), pl.BlockSpec((1, 256, 256), lambda i, j: (i, 0, 0))],
        out_shape=[jax.ShapeDtypeStruct((b, t, 256), F32), jax.ShapeDtypeStruct((b, 256, 256), F32)],
        scratch_shapes=[pltpu.VMEM((256, 256), F32)],
        compiler_params=_cp("parallel", "arbitrary"),
        name="retention",
    )(colsd.reshape(b, t, 1024), cos, sin, dmat, q_dec, k_dec, chunk_dec, r0l, lw['ret_gn'].reshape(1, 256))
    rt = jnp.einsum('bhdge,hg->bhde', rt.reshape(b, H_D, HEAD_DIM, H_D, HEAD_DIM), eye)
    return o.reshape(b * t, 256), rt


def _partner(x, d, period):
    pos = lax.broadcasted_iota(jnp.int32, x.shape, 1) % period
    return jnp.where(pos + d < period, pltpu.roll(x, LANE - d, 1), pltpu.roll(x, period - d, 1))


def _out_body(x_ref, oa_ref, ob_ref, oc_ref, od_ref, w_ref, nw_ref, wr_ref, br_ref, x1_ref, h_ref, comb_ref):
    acc = x_ref[...]
    for i, ref in enumerate((oa_ref, ob_ref, oc_ref, od_ref)):
        acc = acc + jnp.dot(ref[...].astype(BF16), w_ref[256 * i:256 * (i + 1), :], preferred_element_type=F32)
    x1_ref[...] = acc
    h = acc * lax.rsqrt(jnp.mean(acc * acc, axis=-1, keepdims=True) + RMS_EPS) * nw_ref[...]
    hb = h.astype(BF16)
    h_ref[...] = hb
    h_lo = (h - hb.astype(F32)).astype(BF16)
    logits = (jnp.dot(hb, wr_ref[0], preferred_element_type=F32) + jnp.dot(hb, wr_ref[1], preferred_element_type=F32)
              + jnp.dot(h_lo, wr_ref[0], preferred_element_type=F32) + br_ref[...])
    le, lg = logits[:, :LANE], logits[:, LANE:]
    lane = lax.broadcasted_iota(jnp.int32, le.shape, 1)
    mg = jnp.max(lg, axis=-1, keepdims=True)
    eg = jnp.exp(lg - mg)
    pg = eg / (jnp.sum(eg, axis=-1, keepdims=True) * (1.0 / 32.0))
    gidx = (lane % N_EXPERTS) // EXP_PER_GROUP
    g_rank = jnp.zeros_like(pg)
    for d in range(1, N_GROUPS):
        other = pltpu.roll(pg, LANE - EXP_PER_GROUP * d, 1)
        wrapped = gidx + d >= N_GROUPS
        g_rank = g_rank + jnp.where((other > pg) | ((other == pg) & wrapped), 1.0, 0.0)
    kidx = lane % EXP_PER_GROUP
    others = [_partner(le, d, EXP_PER_GROUP) for d in range(1, EXP_PER_GROUP)]
    me = functools.reduce(jnp.maximum, others, le)
    ee = jnp.exp(le - me)
    se = ee
    for d in range(1, EXP_PER_GROUP):
        se = se + _partner(ee, d, EXP_PER_GROUP)
    pe = ee / se
    e_rank = jnp.zeros_like(pe)
    for d in range(1, EXP_PER_GROUP):
        other = _partner(pe, d, EXP_PER_GROUP)
        wrapped = kidx + d >= EXP_PER_GROUP
        e_rank = e_rank + jnp.where((other > pe) | ((other == pe) & wrapped), 1.0, 0.0)
    top = jnp.where(e_rank < 2.0, pe, 0.0)
    den = top
    for d in range(1, EXP_PER_GROUP):
        den = den + _partner(top, d, EXP_PER_GROUP)
    comb = jnp.where((g_rank < 1.0) & (lane < N_EXPERTS), pg * (top / den), 0.0)
    comb_ref[...] = comb


def _out_router(x, oa, ob, oc, od, lw, tm):
    n = x.shape[0]
    row = lambda i: (i, 0)
    fixed = lambda i: (0, 0)
    mix = pl.BlockSpec((tm, 256), row)
    return pl.pallas_call(
        _out_body,
        grid=(n // tm,),
        in_specs=[pl.BlockSpec((tm, D_MODEL), row), mix, mix, mix, mix,
                  pl.BlockSpec((D_MODEL, D_MODEL), fixed), pl.BlockSpec((1, D_MODEL), fixed),
                  pl.BlockSpec((2, D_MODEL, 2 * LANE), lambda i: (0, 0, 0)), pl.BlockSpec((1, 2 * LANE), fixed)],
        out_specs=[pl.BlockSpec((tm, D_MODEL), row), pl.BlockSpec((tm, D_MODEL), row), pl.BlockSpec((tm, LANE), row)],
        out_shape=[jax.ShapeDtypeStruct((n, D_MODEL), F32), jax.ShapeDtypeStruct((n, D_MODEL), BF16),
                   jax.ShapeDtypeStruct((n, LANE), F32)],
        compiler_params=_cp("parallel"),
        name="out_router",
    )(x, oa, ob, oc, od, lw['w_out'], lw['norm_ffn'], lw['w_router'], lw['b_router'])


def _router_weights(w_grp, b_grp, w_exp, b_exp):
    we = jnp.transpose(w_exp, (1, 0, 2)).reshape(D_MODEL, N_EXPERTS)
    wg = jnp.repeat(w_grp, EXP_PER_GROUP, axis=1)
    reps = LANE // N_EXPERTS
    w = jnp.concatenate([jnp.tile(we, (1, reps)), jnp.tile(wg, (1, reps))], axis=1)
    b = jnp.concatenate([jnp.tile(b_exp.reshape(1, N_EXPERTS), (1, reps)),
                         jnp.tile(jnp.repeat(b_grp, EXP_PER_GROUP).reshape(1, N_EXPERTS), (1, reps))], axis=1)
    w_hi = w.astype(BF16)
    return jnp.stack([w_hi, (w - w_hi.astype(F32)).astype(BF16)]), b


def _moe_body(h_ref, comb_ref, x1_ref, wg_ref, wu_ref, wd_ref, nf_ref, *out_refs, final):
    acc_ref = out_refs[-1]
    e = pl.program_id(1)

    @pl.when(e == 0)
    def _():
        acc_ref[...] = x1_ref[...]

    h = h_ref[...]
    comb = comb_ref[...]
    lane = lax.broadcasted_iota(jnp.int32, comb.shape, 1)
    c = jnp.sum(jnp.where(lane == e, comb, 0.0), axis=-1, keepdims=True)
    hg = jnp.dot(h, wg_ref[0], preferred_element_type=F32)
    hu = jnp.dot(h, wu_ref[0], preferred_element_type=F32)
    act = (jax.nn.silu(hg) * hu * c).astype(BF16)
    acc_ref[...] += jnp.dot(act, wd_ref[0], preferred_element_type=F32)

    @pl.when(e == N_EXPERTS - 1)
    def _():
        x2 = acc_ref[...]
        if final:
            out_refs[0][...] = x2 * lax.rsqrt(jnp.mean(x2 * x2, axis=-1, keepdims=True) + RMS_EPS) * nf_ref[...]
        else:
            out_refs[0][...] = x2


def _moe(h, comb, x1, lw, norm_final, tm, final):
    n = x1.shape[0]
    row = lambda i, e: (i, 0)
    per_e = lambda i, e: (e, 0, 0)
    return pl.pallas_call(
        functools.partial(_moe_body, final=final),
        grid=(n // tm, N_EXPERTS),
        in_specs=[pl.BlockSpec((tm, D_MODEL), row), pl.BlockSpec((tm, LANE), row), pl.BlockSpec((tm, D_MODEL), row),
                  pl.BlockSpec((1, D_MODEL, D_EXPERT), per_e), pl.BlockSpec((1, D_MODEL, D_EXPERT), per_e),
                  pl.BlockSpec((1, D_EXPERT, D_MODEL), per_e), pl.BlockSpec((1, D_MODEL), lambda i, e: (0, 0))],
        out_specs=pl.BlockSpec((tm, D_MODEL), row),
        out_shape=jax.ShapeDtypeStruct((n, D_MODEL), F32),
        scratch_shapes=[pltpu.VMEM((tm, D_MODEL), F32)],
        compiler_params=_cp("parallel", "arbitrary"),
        name="moe",
    )(h, comb, x1, lw['moe_wg'], lw['moe_wu'], lw['moe_wd'], norm_final.reshape(1, D_MODEL))


def _prep_layer(l, p):
    w_in = p['w_in'][l]
    o = _offsets(SPLIT_SIZES)
    segs = jnp.split(w_in, o, axis=1)
    w_all = jnp.concatenate([segs[0], segs[1], _pad_to(segs[2], LANE, 1), _pad_to(segs[3], SHIFT_PAD, 1), segs[4], segs[5]],
                            axis=1).astype(BF16)
    w_prec = w_in[:, :PRECISE_COLS]
    w_lo = (w_prec - w_prec.astype(BF16).astype(F32)).astype(BF16)
    lw = {'layer': l, 'w_all': w_all, 'w_lo': w_lo, 'norm_mix': p['norm_mix'][l]}
    lw['cmp'] = _cmp_weights(p['nsa_cmp_w1'][l], p['nsa_cmp_b1'][l], p['nsa_cmp_w2'][l])
    lw['rwkv_mu'] = _pad_to(p['rwkv_mu'][l].reshape(1, SHIFT_B), SHIFT_PAD, 1)
    lw['rwkv_vec'] = _pad_to(p['rwkv_vec'][l], 8, 0)
    z = lambda a, b: jnp.zeros((a, b), F32)
    lw['rwkv_wup'] = jnp.concatenate([p['rwkv_w_up'][l], z(LANE - LORA_W, C_B)], axis=0)
    lw['rwkv_aup'] = jnp.concatenate([z(LORA_W, C_B), p['rwkv_a_up'][l], z(LANE - LORA_W - LORA_A, C_B)], axis=0)
    lw['rwkv_gup'] = jnp.concatenate([z(LORA_W + LORA_A, C_B), p['rwkv_g_up'][l], z(LANE - LORA_W - LORA_A - LORA_G, C_B)], axis=0)
    for name in ('s5_lambda_re', 's5_lambda_im', 's5_b', 's5_c', 's5_d', 's5_log_step', 's5_w_glu', 'ret_gn'):
        lw[name] = p[name][l]
    lw['s5'], lw['s5_step'] = _s5_params(lw)
    lw['w_out'] = p['w_out'][l].astype(BF16)
    lw['norm_ffn'] = p['norm_ffn'][l].reshape(1, D_MODEL)
    lw['w_router'], lw['b_router'] = _router_weights(p['moe_w_grp'][l], p['moe_b_grp'][l], p['moe_w_exp'][l], p['moe_b_exp'][l])
    lw['moe_wg'] = p['moe_w_gate'][l].astype(BF16)
    lw['moe_wu'] = p['moe_w_up'][l].astype(BF16)
    lw['moe_wd'] = p['moe_w_down'][l].astype(BF16)
    return lw


ROW_TILE = 512
MOE_ROW_TILE = 1024
SCAN_TILE = 256


def _prompt_layer(x, lw, tabs, b, t, norm_final):
    cos_a, sin_a, ret_tabs = tabs
    q, kv, gate, colsb, u, colsd = _proj(x, lw['norm_mix'], lw['w_all'], lw['w_lo'], cos_a, sin_a, ROW_TILE)
    o_a = _nsa_prompt_mixer(q, kv, gate, lw, b, t)
    o_b, s_rwkv, s_shift = _rwkv_mixer(colsb, jnp.zeros((b, SHIFT_PAD), F32), jnp.zeros((b, H_B, HEAD_DIM, HEAD_DIM), F32),
                                       lw, b, t, ROW_TILE, b, SCAN_TILE)
    o_c, s_s5 = _s5_mixer(u, jnp.zeros((b, G_C, S5_P, 2), F32), lw, b, t, SCAN_TILE, False)
    o_d, s_ret = _ret_mixer(colsd, jnp.zeros((b, H_D, HEAD_DIM, HEAD_DIM), F32), lw, ret_tabs, b, t)
    x1, h, comb = _out_router(x, o_a, o_b, o_c, o_d, lw, ROW_TILE)
    x2 = _moe(h, comb, x1, lw, norm_final, MOE_ROW_TILE, lw['layer'] == DEPTH - 1)
    kv3 = kv.reshape(b, t, 6 * LANE)
    rows = kv3[:, :, :4 * LANE].reshape(b, t, 4, N_KV_A, HEAD_DIM)
    win = kv3[:, t - min(WINDOW, t):, 4 * LANE:].reshape(b, min(WINDOW, t), 2, N_KV_A, HEAD_DIM)
    return x2, (rows, win, s_rwkv, s_shift, s_s5, s_ret)


def _sample_layer(x, lw, tabs, b, pos, cache_kv, page_table, win_buf, s_rwkv, s_shift, s_s5, s_ret, norm_final):
    cos_a, sin_a, ret_cs = tabs
    q, kv, gate, colsb, u, colsd = _proj(x, lw['norm_mix'], lw['w_all'], lw['w_lo'], cos_a, sin_a, b)
    o_a, win = _nsa_sample(q, kv, gate, cache_kv, win_buf, lw['layer'], page_table, lw['cmp'], int(pos[0]))
    rows = kv[:, :4 * LANE].reshape(b, 1, 4, N_KV_A, HEAD_DIM)
    o_b, s_rwkv = _rwkv_step(colsb, _pad_to(s_shift, SHIFT_PAD, 1), s_rwkv, lw['layer'], lw)
    s_shift = colsb[:, :SHIFT_B]
    o_c, s_s5 = _s5_step(u, s_s5, lw['layer'], lw)
    o_d, s_ret = _ret_step(colsd, ret_cs[0], ret_cs[1], s_ret, lw['layer'], lw['ret_gn'])
    x1, h, comb = _out_router(x, o_a, o_b, o_c, o_d, lw, b)
    x2 = _moe(h, comb, x1, lw, norm_final, b, lw['layer'] == DEPTH - 1)
    return x2, (rows, win, s_rwkv, s_shift, s_s5, s_ret)


def kernel(x_prompt, x_sample, cache_nsa_kv, cache_nsa_win, state_rwkv, state_rwkv_shift, state_s5, state_ret, page_table, norm_mix, w_in, nsa_cmp_w1, nsa_cmp_b1, nsa_cmp_w2, rwkv_mu, rwkv_vec, rwkv_w_up, rwkv_a_up, rwkv_g_up, s5_lambda_re, s5_lambda_im, s5_b, s5_c, s5_d, s5_log_step, s5_w_glu, ret_gn, w_out, norm_ffn, moe_w_grp, moe_b_grp, moe_w_exp, moe_b_exp, moe_w_gate, moe_w_up, moe_w_down, norm_final):
    p = dict(norm_mix=norm_mix, w_in=w_in, nsa_cmp_w1=nsa_cmp_w1, nsa_cmp_b1=nsa_cmp_b1, nsa_cmp_w2=nsa_cmp_w2,
             rwkv_mu=rwkv_mu, rwkv_vec=rwkv_vec, rwkv_w_up=rwkv_w_up, rwkv_a_up=rwkv_a_up, rwkv_g_up=rwkv_g_up,
             s5_lambda_re=s5_lambda_re, s5_lambda_im=s5_lambda_im, s5_b=s5_b, s5_c=s5_c, s5_d=s5_d,
             s5_log_step=s5_log_step, s5_w_glu=s5_w_glu, ret_gn=ret_gn, w_out=w_out, norm_ffn=norm_ffn,
             moe_w_grp=moe_w_grp, moe_b_grp=moe_b_grp, moe_w_exp=moe_w_exp, moe_b_exp=moe_b_exp,
             moe_w_gate=moe_w_gate, moe_w_up=moe_w_up, moe_w_down=moe_w_down)
    bp, tp = x_prompt.shape[:2]
    bs, ts = x_sample.shape[:2]
    assert ts == 1 and tp % ROW_TILE == 0 and (bp * tp) % MOE_ROW_TILE == 0 and tp % (4 * LANE) == 0 and bs % 8 == 0
    past_len = page_table.shape[1] * cache_nsa_kv.shape[2]
    pos_p = np.arange(tp)
    pos_s = past_len + np.arange(ts)
    c = RET_CHUNK if tp % RET_CHUNK == 0 else tp
    tabs_p = _rope_tables(pos_p, ROT_DIM, ROPE_THETA, HEAD_DIM, 2) + (_ret_tables(pos_p, c),)
    pos_rows = np.repeat(pos_s, bs)
    tabs_s = _rope_tables(pos_rows, ROT_DIM, ROPE_THETA, HEAD_DIM, 2) + (_rope_tables(pos_rows, HEAD_DIM, RET_THETA, HEAD_DIM, H_D),)
    xp = x_prompt.reshape(bp * tp, D_MODEL)
    xs = x_sample.reshape(bs * ts, D_MODEL)
    sts_p, sts_s = [], []
    for l in range(DEPTH):
        lw = _prep_layer(l, p)
        xp, st_p = _prompt_layer(xp, lw, tabs_p, bp, tp, norm_final)
        xs, st_s = _sample_layer(xs, lw, tabs_s, bs, pos_s, cache_nsa_kv, page_table, cache_nsa_win, state_rwkv,
                                     state_rwkv_shift[l], state_s5, state_ret, norm_final)
        rows, win, s1, s2, s3, s4 = st_s
        sts_s.append((rows, win, s1, s2, s3, s4))
        sts_p.append(st_p)
    new_p = [jnp.stack([st[i] for st in sts_p]) for i in range(6)]
    new_s = [jnp.stack([st[i] for st in sts_s]) for i in range(6)]
    return (xp.reshape(bp, tp, D_MODEL), xs.reshape(bs, ts, D_MODEL), new_p[0], new_s[0], new_p[1], new_s[1],
            new_p[2], new_s[2], new_p[3], new_s[3], new_p[4], new_s[4], new_p[5], new_s[5])
```
